```python
import math
import jax, jax.numpy as jnp
from jax import lax
import numpy as np

D_MODEL = 1024
BATCH = 4
SEQ = 4096
DEPTH = 1

MIX_WIDTH = D_MODEL
RET_HEAD_DIM = 64
RET_WIDTH = MIX_WIDTH // 2
RET_HEADS = RET_WIDTH // RET_HEAD_DIM
RET_CHUNK = 128
DIFF_QK_DIM = 64
DIFF_V_DIM = 2 * DIFF_QK_DIM
DIFF_WIDTH = MIX_WIDTH - RET_WIDTH
DIFF_HEADS = DIFF_WIDTH // DIFF_V_DIM
DIFF_QK_WIDTH = DIFF_HEADS * 2 * DIFF_QK_DIM
IN_WIDTH = 4 * RET_WIDTH + 2 * DIFF_QK_WIDTH + DIFF_WIDTH
Q_BLOCK = 128
N_GROUPS = 4
EXPERTS_PER_GROUP = 8
TOP_K_INNER = 2
D_EXPERT = 512
N_MOD = 6
ROPE_BASE = 10000.0
EPS = 1e-6

kernel_name = 'hybrid_retention_diffattn_hmoe_adaln'


def rms_norm(x, gain):
    xf = x.astype(jnp.float32)
    y = xf * lax.rsqrt(jnp.mean(xf * xf, axis=-1, keepdims=True) + EPS)
    return (y * gain.astype(jnp.float32)).astype(x.dtype)


def modulate(h, shift, scale):
    return h * (1.0 + scale[:, None, :]) + shift[:, None, :]


def rotary(x, positions):
    half = x.shape[-1] // 2
    inv_freq = 1.0 / (ROPE_BASE ** (jnp.arange(half, dtype=jnp.float32) / half))
    ang = positions.astype(jnp.float32)[:, None] * inv_freq[None, :]
    cos = jnp.cos(ang)[None, :, None, :]
    sin = jnp.sin(ang)[None, :, None, :]
    x1, x2 = x[..., :half], x[..., half:]
    return jnp.concatenate([x1 * cos - x2 * sin, x2 * cos + x1 * sin], axis=-1)


def retention(q, k, v, g, gn_gain):
    out_dtype = v.dtype
    B, S, H, dh = q.shape
    C = RET_CHUNK
    nc = S // C
    f32 = jnp.float32
    pos = jnp.arange(S)
    q = rotary(q.astype(f32), pos)
    k = rotary(k.astype(f32), pos) * (dh ** -0.5)
    v = v.astype(f32)
    log_gamma = jnp.log(1.0 - jnp.exp2(-5.0 - jnp.arange(H, dtype=f32)))
    idx = jnp.arange(C)
    rel = (idx[:, None] - idx[None, :]).astype(f32)
    decay = jnp.where(rel[None] >= 0, jnp.exp(log_gamma[:, None, None] * jnp.maximum(rel, 0.0)[None]), 0.0)
    qc = q.reshape(B, nc, C, H, dh)
    kc = k.reshape(B, nc, C, H, dh)
    vc = v.reshape(B, nc, C, H, dh)
    scores = jnp.einsum('bnihd,bnjhd->bhnij', qc, kc) * decay[None, :, None]
    intra = jnp.einsum('bhnij,bnjhv->bnihv', scores, vc)
    k_dec = kc * jnp.exp(log_gamma[None, :] * (C - 1 - idx).astype(f32)[:, None])[:, :, None]
    states = jnp.einsum('bnjhk,bnjhv->nbhkv', k_dec, vc)
    chunk_decay = jnp.exp(log_gamma * C)[None, :, None, None]

    def step(R, s_n):
        return R * chunk_decay + s_n, R

    _, R_prev = lax.scan(step, jnp.zeros((B, H, dh, dh), f32), states)
    cross = jnp.einsum('bnihk,nbhkv->bnihv', qc, R_prev)
    cross = cross * jnp.exp(log_gamma[None, :] * (idx + 1).astype(f32)[:, None])[:, :, None]
    y = (intra + cross).reshape(B, S, H, dh)
    mu = jnp.mean(y, axis=-1, keepdims=True)
    var = jnp.mean(jnp.square(y - mu), axis=-1, keepdims=True)
    y = (y - mu) * lax.rsqrt(var + EPS) * gn_gain.astype(f32).reshape(H, dh)
    y = jax.nn.silu(g.astype(f32)) * y
    return y.reshape(B, S, H * dh).astype(out_dtype)


def diff_attention(q, k, v, lam_q1, lam_k1, lam_q2, lam_k2, subln_gain, lambda_init):
    B, S, H, _, dq = q.shape
    dv = v.shape[-1]
    nb = S // Q_BLOCK
    f32 = jnp.float32
    lam = (jnp.exp(jnp.sum(lam_q1.astype(f32) * lam_k1.astype(f32)))
           - jnp.exp(jnp.sum(lam_q2.astype(f32) * lam_k2.astype(f32))) + lambda_init)
    kt = k.transpose(0, 2, 3, 1, 4)
    vt = v.transpose(0, 2, 1, 3)
    qb = (q * (dq ** -0.5)).reshape(B, nb, Q_BLOCK, H, 2, dq).transpose(1, 0, 3, 4, 2, 5)
    key_pos = jnp.arange(S)
    neg = jnp.finfo(f32).min

    def block(args):
        q_blk, blk = args
        s = jnp.einsum('bhpqd,bhpkd->bhpqk', q_blk, kt).astype(f32)
        q_pos = blk * Q_BLOCK + jnp.arange(Q_BLOCK)
        mask = key_pos[None, :] <= q_pos[:, None]
        a = jax.nn.softmax(jnp.where(mask, s, neg), axis=-1)
        w = a[:, :, 0] - lam * a[:, :, 1]
        return jnp.einsum('bhqk,bhkv->bhqv', w.astype(vt.dtype), vt)

    o = lax.map(block, (qb, jnp.arange(nb)))
    o = o.transpose(1, 0, 3, 2, 4).reshape(B, S, H, dv)
    o = rms_norm(o, subln_gain) * (1.0 - lambda_init)
    return o.reshape(B, S, H * dv)


def hier_moe(h, w_group, b_group, w_expert, b_expert, w_gate, w_up, w_down):
    B, S, D = h.shape
    t = h.reshape(B * S, D)
    f32 = jnp.float32
    group_logits = (t @ w_group + b_group).astype(f32)
    group_prob = jax.nn.softmax(group_logits, axis=-1)
    g_idx = jnp.argmax(group_logits, axis=-1)
    g_weight = jnp.take_along_axis(group_prob, g_idx[:, None], axis=-1)
    expert_logits = (jnp.einsum('td,dge->tge', t, w_expert) + b_expert).astype(f32)
    sel_logits = jnp.take_along_axis(expert_logits, g_idx[:, None, None], axis=1)[:, 0]
    top_val, top_idx = lax.top_k(sel_logits, TOP_K_INNER)
    top_w = jax.nn.softmax(top_val, axis=-1) * g_weight
    inner = jnp.einsum('tk,tke->te', top_w, jax.nn.one_hot(top_idx, EXPERTS_PER_GROUP, dtype=f32))
    combine = (jax.nn.one_hot(g_idx, N_GROUPS, dtype=f32)[:, :, None] * inner[:, None, :]).astype(t.dtype)
    y = jnp.zeros_like(t)
    for grp in range(N_GROUPS):
        a = jnp.einsum('td,edf->tef', t, w_gate[grp])
        u = jnp.einsum('td,edf->tef', t, w_up[grp])
        hid = jax.nn.silu(a) * u * combine[:, grp, :, None]
        y = y + jnp.einsum('tef,efd->td', hid, w_down[grp])
    return y.reshape(B, S, D)


def setup_inputs(seed: int = 0) -> dict:
    key = jax.random.key(seed)
    ks = jax.random.split(key, 24)
    f32 = jnp.float32

    def nrm(k, shape, scale):
        return jax.random.normal(k, shape, f32) * scale

    def gain(k, shape):
        return 1.0 + 0.01 * jax.random.normal(k, shape, f32)

    L, D, G, E = DEPTH, D_MODEL, N_GROUPS, EXPERTS_PER_GROUP
    return {
        'x': nrm(ks[0], (BATCH, SEQ, D), 1.0),
        'c': nrm(ks[1], (BATCH, D), 1.0),
        'ada_w': nrm(ks[2], (L, D, N_MOD * D), 0.5 * D ** -0.5),
        'ada_b': nrm(ks[3], (L, N_MOD * D), 0.01),
        'norm1_gain': gain(ks[4], (L, D)),
        'norm2_gain': gain(ks[5], (L, D)),
        'w_in': nrm(ks[6], (L, D, IN_WIDTH), D ** -0.5),
        'w_out': nrm(ks[7], (L, MIX_WIDTH, D), MIX_WIDTH ** -0.5),
        'ret_gn_gain': gain(ks[8], (L, RET_WIDTH)),
        'lam_q1': nrm(ks[9], (L, DIFF_QK_DIM), 0.1),
        'lam_k1': nrm(ks[10], (L, DIFF_QK_DIM), 0.1),
        'lam_q2': nrm(ks[11], (L, DIFF_QK_DIM), 0.1),
        'lam_k2': nrm(ks[12], (L, DIFF_QK_DIM), 0.1),
        'diff_subln_gain': gain(ks[13], (L, DIFF_V_DIM)),
        'w_group': nrm(ks[14], (L, D, G), D ** -0.5),
        'b_group': nrm(ks[15], (L, G), 0.01),
        'w_expert': nrm(ks[16], (L, D, G, E), D ** -0.5),
        'b_expert': nrm(ks[17], (L, G, E), 0.01),
        'w_gate': nrm(ks[18], (L, G, E, D, D_EXPERT), D ** -0.5),
        'w_up': nrm(ks[19], (L, G, E, D, D_EXPERT), D ** -0.5),
        'w_down': nrm(ks[20], (L, G, E, D_EXPERT, D), D_EXPERT ** -0.5),
        'final_gain': gain(ks[21], (D,)),
    }


def reference(x, c, ada_w, ada_b, norm1_gain, norm2_gain, w_in, w_out, ret_gn_gain,
              lam_q1, lam_k1, lam_q2, lam_k2, diff_subln_gain, w_group, b_group,
              w_expert, b_expert, w_gate, w_up, w_down, final_gain):
    B, S, D = x.shape
    c_act = jax.nn.silu(c)
    splits = [RET_WIDTH, 2 * RET_WIDTH, 3 * RET_WIDTH, 4 * RET_WIDTH,
              4 * RET_WIDTH + DIFF_QK_WIDTH, 4 * RET_WIDTH + 2 * DIFF_QK_WIDTH]
    for layer in range(DEPTH):
        mod = (c_act @ ada_w[layer] + ada_b[layer]).reshape(B, N_MOD, D)
        shift1, scale1, gate1 = mod[:, 0], mod[:, 1], mod[:, 2]
        shift2, scale2, gate2 = mod[:, 3], mod[:, 4], mod[:, 5]
        lambda_init = 0.8 - 0.6 * math.exp(-0.3 * layer)

        h = modulate(rms_norm(x, norm1_gain[layer]), shift1, scale1)
        proj = h @ w_in[layer]
        rq, rk, rv, rg, dq_, dk_, dv_ = jnp.split(proj, splits, axis=-1)
        ret_shape = (B, S, RET_HEADS, RET_HEAD_DIM)
        ret_out = retention(rq.reshape(ret_shape), rk.reshape(ret_shape), rv.reshape(ret_shape),
                            rg.reshape(ret_shape), ret_gn_gain[layer])
        qk_shape = (B, S, DIFF_HEADS, 2, DIFF_QK_DIM)
        diff_out = diff_attention(dq_.reshape(qk_shape), dk_.reshape(qk_shape),
                                  dv_.reshape(B, S, DIFF_HEADS, DIFF_V_DIM),
                                  lam_q1[layer], lam_k1[layer], lam_q2[layer], lam_k2[layer],
                                  diff_subln_gain[layer], lambda_init)
        mix = jnp.concatenate([ret_out, diff_out], axis=-1)
        x = x + gate1[:, None, :] * (mix @ w_out[layer])

        h = modulate(rms_norm(x, norm2_gain[layer]), shift2, scale2)
        moe_out = hier_moe(h, w_group[layer], b_group[layer], w_expert[layer], b_expert[layer],
                           w_gate[layer], w_up[layer], w_down[layer])
        x = x + gate2[:, None, :] * moe_out
    return rms_norm(x, final_gain)
```

```python
import functools
import math

import jax
import jax.numpy as jnp
from jax import lax
from jax.experimental import pallas as pl
from jax.experimental.pallas import tpu as pltpu

F32 = jnp.float32
BF16 = jnp.bfloat16

D_MODEL = 1024
RET_HEAD_DIM = 64
RET_WIDTH = 512
RET_HEADS = 8
RET_PAIRS = RET_HEADS // 2
DIFF_QK_DIM = 64
DIFF_V_DIM = 128
DIFF_HEADS = 4
DIFF_WIDTH = 512
N_GROUPS = 4
EXPERTS_PER_GROUP = 8
N_EXPERTS = N_GROUPS * EXPERTS_PER_GROUP
D_EXPERT = 512
N_MOD = 6
ROPE_BASE = 10000.0
EPS = 1e-6
LANES = 128
ROUTER_ROWS = 8 + N_EXPERTS
VMEM_LIMIT = 56 * 1024 * 1024


def _dot(a, b):
    return jnp.dot(a, b, preferred_element_type=F32)


def _dot_nt(a, b):
    return lax.dot_general(a, b, (((1,), (1,)), ((), ())), preferred_element_type=F32)


def _dot_tn(a, b):
    return lax.dot_general(a, b, (((0,), (0,)), ((), ())), preferred_element_type=F32)


def _split_bf16(x):
    hi = x.astype(BF16)
    lo = (x - hi.astype(F32)).astype(BF16)
    return hi, lo


def _silu(x):
    return x / (1.0 + jnp.exp(-x))


def _adaln_kernel(c_ref, w_ref, b_ref, o_ref):
    ca = _silu(c_ref[...])
    c_hi, c_lo = _split_bf16(ca)
    w_hi, w_lo = _split_bf16(w_ref[...])
    o_ref[...] = _dot(c_hi, w_hi) + _dot(c_lo, w_hi) + _dot(c_hi, w_lo) + b_ref[...]


def _adaln(c_pad, ada_w, ada_b):
    n_out = ada_w.shape[1]
    tn = D_MODEL
    return pl.pallas_call(
        _adaln_kernel,
        grid=(n_out // tn,),
        in_specs=[
            pl.BlockSpec((8, D_MODEL), lambda j: (0, 0)),
            pl.BlockSpec((D_MODEL, tn), lambda j: (0, j)),
            pl.BlockSpec((1, tn), lambda j: (0, j)),
        ],
        out_specs=pl.BlockSpec((8, tn), lambda j: (0, j)),
        out_shape=jax.ShapeDtypeStruct((8, n_out), F32),
        compiler_params=pltpu.CompilerParams(vmem_limit_bytes=VMEM_LIMIT),
        name="adaln",
    )(c_pad, ada_w, ada_b)


def _norm_modulate(x, gain, shift, scale):
    ms = jnp.mean(x * x, axis=-1, keepdims=True)
    y = x * lax.rsqrt(ms + EPS) * gain
    return y * (1.0 + scale) + shift


def _rotary_slab(x, cos, sin_signed, lane_lo):
    swapped = jnp.where(lane_lo, pltpu.roll(x, 96, 1), pltpu.roll(x, 32, 1))
    return x * cos + swapped * sin_signed


def _inproj_kernel(x_ref, mod_ref, gain_ref, w_ref, cos_ref, sin_ref,
                   rq_ref, rk_ref, rv_ref, rg_ref, dq_ref, dk_ref, dv_ref, *, tiles_per_batch):
    b = pl.program_id(0) // tiles_per_batch
    shift = mod_ref[pl.ds(b, 1), 0:D_MODEL]
    scale = mod_ref[pl.ds(b, 1), D_MODEL:2 * D_MODEL]
    h = _norm_modulate(x_ref[...], gain_ref[...], shift, scale).astype(BF16)
    cos = cos_ref[...]
    sin = sin_ref[...]
    lane = lax.broadcasted_iota(jnp.int32, cos.shape, 1)
    lane_lo = (lane % 64) < 32

    def proj(chunk):
        return _dot(h, w_ref[:, chunk * RET_WIDTH:(chunk + 1) * RET_WIDTH])

    def rotary(acc, out_ref, post_scale):
        for s in range(RET_WIDTH // LANES):
            sl = slice(s * LANES, (s + 1) * LANES)
            out_ref[:, sl] = (_rotary_slab(acc[:, sl], cos, sin, lane_lo) * post_scale).astype(BF16)

    rotary(proj(0), rq_ref, 1.0)
    rotary(proj(1), rk_ref, RET_HEAD_DIM ** -0.5)
    rv_ref[...] = proj(2).astype(BF16)
    rg_ref[...] = _silu(proj(3)).astype(BF16)
    dq_ref[...] = (proj(4) * (DIFF_QK_DIM ** -0.5)).astype(BF16)
    dk_ref[...] = proj(5).astype(BF16)
    dv_ref[...] = proj(6).astype(BF16)


def _inproj(x2, mod, gain, w_in, cos_t, sin_t, seq, tm):
    tokens = x2.shape[0]
    tiles_per_batch = seq // tm
    width = w_in.shape[1]
    tok_spec = lambda w: pl.BlockSpec((tm, w), lambda i: (i, 0))
    tab_spec = pl.BlockSpec((tm, LANES), lambda i: (i % tiles_per_batch, 0))
    out = jax.ShapeDtypeStruct((tokens, RET_WIDTH), BF16)
    return pl.pallas_call(
        functools.partial(_inproj_kernel, tiles_per_batch=tiles_per_batch),
        grid=(tokens // tm,),
        in_specs=[
            tok_spec(D_MODEL),
            pl.BlockSpec(mod.shape, lambda i: (0, 0)),
            pl.BlockSpec((1, D_MODEL), lambda i: (0, 0)),
            pl.BlockSpec((D_MODEL, width), lambda i: (0, 0)),
            tab_spec, tab_spec,
        ],
        out_specs=[tok_spec(RET_WIDTH)] * 7,
        out_shape=[out] * 7,
        compiler_params=pltpu.CompilerParams(
            dimension_semantics=("parallel",), vmem_limit_bytes=VMEM_LIMIT),
        name="inproj",
    )(x2, mod, gain, w_in, cos_t, sin_t)


def _retention_kernel(q_ref, k_ref, v_ref, g_ref, dec_ref, qdec_ref, kdec_ref, rdec_ref,
                      bmask_ref, gmean_ref, gain_ref, o_ref, state_ref, *, chunk):
    @pl.when(pl.program_id(1) == 0)
    def _():
        state_ref[...] = jnp.zeros_like(state_ref)

    lane = lax.broadcasted_iota(jnp.int32, (chunk, LANES), 1)
    first_head = lane < RET_HEAD_DIM
    gmean = gmean_ref[...]
    bmask = bmask_ref[...]
    for p in range(RET_PAIRS):
        sl = slice(p * LANES, (p + 1) * LANES)
        q = q_ref[:, sl]
        k = k_ref[:, sl]
        v = v_ref[:, sl]
        zero = jnp.zeros_like(q)
        q_stack = jnp.concatenate([jnp.where(first_head, q, zero), jnp.where(first_head, zero, q)], axis=0)
        scores = _dot_nt(q_stack, k) * dec_ref[p]
        intra2 = _dot(scores.astype(BF16), v)
        intra = jnp.where(first_head, intra2[:chunk], intra2[chunk:])
        state = state_ref[p]
        cross = _dot(q, state.astype(BF16)) * qdec_ref[:, sl]
        y = intra + cross
        k_dec = (k.astype(F32) * kdec_ref[:, sl]).astype(BF16)
        state_ref[p] = state * rdec_ref[p] + _dot_tn(k_dec, v) * bmask
        y_hi, y_lo = _split_bf16(y)
        mu = _dot(y_hi, gmean) + _dot(y_lo, gmean)
        d = y - mu
        d_hi, d_lo = _split_bf16(d * d)
        var = _dot(d_hi, gmean) + _dot(d_lo, gmean)
        yn = d * lax.rsqrt(var + EPS) * gain_ref[:, sl]
        o_ref[:, sl] = (g_ref[:, sl].astype(F32) * yn).astype(BF16)


def _retention_tables(chunk):
    heads = jnp.arange(RET_HEADS, dtype=F32)
    log_gamma = jnp.log(1.0 - jnp.exp2(-5.0 - heads))
    idx = jnp.arange(chunk)
    rel = (idx[:, None] - idx[None, :]).astype(F32)
    decay = jnp.where(rel[None] >= 0, jnp.exp(log_gamma[:, None, None] * jnp.maximum(rel, 0.0)[None]), 0.0)
    dec2 = decay.reshape(RET_PAIRS, 2 * chunk, chunk)
    lane_lg = jnp.repeat(log_gamma, RET_HEAD_DIM)
    qdec = jnp.exp(lane_lg[None, :] * (idx + 1).astype(F32)[:, None])
    kdec = jnp.exp(lane_lg[None, :] * (chunk - 1 - idx).astype(F32)[:, None])
    rdec = jnp.exp(lane_lg * chunk).reshape(RET_PAIRS, LANES, 1) * jnp.ones((1, 1, LANES), F32)
    blk = jnp.arange(LANES) // RET_HEAD_DIM
    bmask = (blk[:, None] == blk[None, :]).astype(F32)
    gmean = (bmask / RET_HEAD_DIM).astype(BF16)
    return dec2, qdec, kdec, rdec, bmask, gmean


def _retention(rq, rk, rv, rg, gn_gain, batch, seq, chunk):
    nc = seq // chunk
    dec2, qdec, kdec, rdec, bmask, gmean = _retention_tables(chunk)
    tok_spec = pl.BlockSpec((chunk, RET_WIDTH), lambda b, n: (b * nc + n, 0))
    full = lambda a: pl.BlockSpec(a.shape, lambda b, n: (0,) * a.ndim)
    return pl.pallas_call(
        functools.partial(_retention_kernel, chunk=chunk),
        grid=(batch, nc),
        in_specs=[tok_spec] * 4 + [full(dec2), full(qdec), full(kdec), full(rdec), full(bmask),
                                   full(gmean), full(gn_gain)],
        out_specs=tok_spec,
        out_shape=jax.ShapeDtypeStruct(rq.shape, BF16),
        scratch_shapes=[pltpu.VMEM((RET_PAIRS, LANES, LANES), F32)],
        compiler_params=pltpu.CompilerParams(
            dimension_semantics=("parallel", "arbitrary"), vmem_limit_bytes=VMEM_LIMIT),
        name="retention",
    )(rq, rk, rv, rg, dec2, qdec, kdec, rdec, bmask, gmean, gn_gain)


NEG_BIG = -1e30


def _diffattn_kernel(q_ref, k_ref, v_ref, lq1_ref, lk1_ref, lq2_ref, lk2_ref, gain_ref, o_ref,
                     qs_ref, m_ref, l_ref, acc_ref, *, tq, tk, lambda_init):
    i = pl.program_id(2)
    q = q_ref[...]
    lane = lax.broadcasted_iota(jnp.int32, q.shape, 1)
    zero = jnp.zeros_like(q)
    qs_ref[0:tq, :] = jnp.where(lane < DIFF_QK_DIM, q, zero)
    qs_ref[tq:2 * tq, :] = jnp.where(lane < DIFF_QK_DIM, zero, q)
    m_ref[...] = jnp.full_like(m_ref, NEG_BIG)
    l_ref[...] = jnp.zeros_like(l_ref)
    acc_ref[...] = jnp.zeros_like(acc_ref)

    def step(j, masked):
        start = pl.multiple_of(j * tk, tk)
        k = k_ref[pl.ds(start, tk), :]
        v = v_ref[pl.ds(start, tk), :]
        s = _dot_nt(qs_ref[...], k)
        if masked:
            row = lax.broadcasted_iota(jnp.int32, (2 * tq, tk), 0) % tq
            col = lax.broadcasted_iota(jnp.int32, (2 * tq, tk), 1)
            s = jnp.where(start + col <= i * tq + row, s, NEG_BIG)
        m_old = m_ref[...]
        m_new = jnp.maximum(m_old, jnp.max(s, axis=-1, keepdims=True))
        alpha = jnp.exp(m_old - m_new)
        p = jnp.exp(s - m_new)
        l_ref[...] = alpha * l_ref[...] + jnp.sum(p, axis=-1, keepdims=True)
        acc_ref[...] = alpha * acc_ref[...] + _dot(p.astype(BF16), v)
        m_ref[...] = m_new

    last = (i * tq + tq - 1) // tk
    lax.fori_loop(0, last, lambda j, c: (step(j, False), c)[1], 0)
    step(last, True)

    lam = (jnp.exp(jnp.sum(lq1_ref[...] * lk1_ref[...], axis=-1, keepdims=True))
           - jnp.exp(jnp.sum(lq2_ref[...] * lk2_ref[...], axis=-1, keepdims=True)) + lambda_init)
    o2 = acc_ref[...] / l_ref[...]
    o = o2[:tq] - lam * o2[tq:]
    ms = jnp.mean(o * o, axis=-1, keepdims=True)
    o = o * lax.rsqrt(ms + EPS) * gain_ref[...] * (1.0 - lambda_init)
    o_ref[...] = o.astype(BF16)


def _diffattn(dq, dk, dv, lam_q1, lam_k1, lam_q2, lam_k2, gain, batch, seq, lambda_init, tq, tk):
    nq = seq // tq
    q_spec = pl.BlockSpec((tq, LANES), lambda b, h, i: (b * nq + i, h))
    kv_spec = pl.BlockSpec((seq, LANES), lambda b, h, i: (b, h))
    vec = lambda a: pl.BlockSpec(a.shape, lambda b, h, i: (0, 0))
    return pl.pallas_call(
        functools.partial(_diffattn_kernel, tq=tq, tk=tk, lambda_init=lambda_init),
        grid=(batch, DIFF_HEADS, nq),
        in_specs=[q_spec, kv_spec, kv_spec, vec(lam_q1), vec(lam_k1), vec(lam_q2), vec(lam_k2), vec(gain)],
        out_specs=q_spec,
        out_shape=jax.ShapeDtypeStruct(dq.shape, BF16),
        scratch_shapes=[
            pltpu.VMEM((2 * tq, LANES), BF16),
            pltpu.VMEM((2 * tq, 1), F32),
            pltpu.VMEM((2 * tq, 1), F32),
            pltpu.VMEM((2 * tq, DIFF_V_DIM), F32),
        ],
        compiler_params=pltpu.CompilerParams(
            dimension_semantics=("parallel", "parallel", "arbitrary"), vmem_limit_bytes=VMEM_LIMIT),
        name="diffattn",
    )(dq, dk, dv, lam_q1, lam_k1, lam_q2, lam_k2, gain)


def _route(logits):
    r = [logits[g:g + 1, :] for g in range(N_GROUPS)]
    gmax = jnp.maximum(jnp.maximum(r[0], r[1]), jnp.maximum(r[2], r[3]))
    g_idx = jnp.where(r[0] == gmax, 0, jnp.where(r[1] == gmax, 1, jnp.where(r[2] == gmax, 2, 3)))
    denom = sum(jnp.exp(rg - gmax) for rg in r)
    g_weight = 1.0 / denom
    sel = jnp.zeros((EXPERTS_PER_GROUP, logits.shape[1]), F32)
    for g in range(N_GROUPS):
        rows = logits[8 + g * EXPERTS_PER_GROUP:8 + (g + 1) * EXPERTS_PER_GROUP, :]
        sel = jnp.where(g_idx == g, rows, sel)
    eidx = lax.broadcasted_iota(jnp.int32, sel.shape, 0)
    v1 = jnp.max(sel, axis=0, keepdims=True)
    i1 = jnp.min(jnp.where(sel == v1, eidx, EXPERTS_PER_GROUP), axis=0, keepdims=True)
    sel2 = jnp.where(eidx == i1, -jnp.inf, sel)
    v2 = jnp.max(sel2, axis=0, keepdims=True)
    i2 = jnp.min(jnp.where(sel2 == v2, eidx, EXPERTS_PER_GROUP), axis=0, keepdims=True)
    e2 = jnp.exp(v2 - v1)
    w1 = g_weight / (1.0 + e2)
    w2 = g_weight * e2 / (1.0 + e2)
    return g_idx, i1, i2, w1, w2


def _outproj_kernel(ret_ref, diff_ref, x_ref, mod_ref, gain_ref, wo_ref, wr_hi_ref, wr_lo_ref, br_ref,
                    x1_ref, h2_ref, comb_ref, *, tiles_per_batch):
    b = pl.program_id(0) // tiles_per_batch
    mix = _dot(ret_ref[...], wo_ref[0:RET_WIDTH, :]) + _dot(diff_ref[...], wo_ref[RET_WIDTH:, :])
    gate1 = mod_ref[pl.ds(b, 1), 2 * D_MODEL:3 * D_MODEL]
    x1 = x_ref[...] + gate1 * mix
    x1_ref[...] = x1
    shift = mod_ref[pl.ds(b, 1), 3 * D_MODEL:4 * D_MODEL]
    scale = mod_ref[pl.ds(b, 1), 4 * D_MODEL:5 * D_MODEL]
    h2 = _norm_modulate(x1, gain_ref[...], shift, scale)
    h_hi, h_lo = _split_bf16(h2)
    h2_ref[...] = h_hi
    wr_hi = wr_hi_ref[...]
    logits = _dot_nt(wr_hi, h_hi) + _dot_nt(wr_lo_ref[...], h_hi) + _dot_nt(wr_hi, h_lo) + br_ref[...]
    g_idx, i1, i2, w1, w2 = _route(logits)
    eidx = lax.broadcasted_iota(jnp.int32, (EXPERTS_PER_GROUP, logits.shape[1]), 0)
    inner = jnp.where(eidx == i1, w1, 0.0) + jnp.where(eidx == i2, w2, 0.0)
    for g in range(N_GROUPS):
        comb_ref[g * EXPERTS_PER_GROUP:(g + 1) * EXPERTS_PER_GROUP, :] = jnp.where(g_idx == g, inner, 0.0)


def _outproj(ret_out, diff_out, x2, mod, gain, w_out, wr_hi, wr_lo, br, seq, tm):
    tokens = x2.shape[0]
    tiles_per_batch = seq // tm
    tok_spec = lambda w: pl.BlockSpec((tm, w), lambda i: (i, 0))
    full = lambda a: pl.BlockSpec(a.shape, lambda i: (0,) * a.ndim)
    return pl.pallas_call(
        functools.partial(_outproj_kernel, tiles_per_batch=tiles_per_batch),
        grid=(tokens // tm,),
        in_specs=[tok_spec(RET_WIDTH), tok_spec(DIFF_WIDTH), tok_spec(D_MODEL), full(mod), full(gain),
                  full(w_out), full(wr_hi), full(wr_lo), full(br)],
        out_specs=[tok_spec(D_MODEL), tok_spec(D_MODEL), pl.BlockSpec((N_EXPERTS, tm), lambda i: (0, i))],
        out_shape=[jax.ShapeDtypeStruct((tokens, D_MODEL), F32),
                   jax.ShapeDtypeStruct((tokens, D_MODEL), BF16),
                   jax.ShapeDtypeStruct((N_EXPERTS, tokens), F32)],
        compiler_params=pltpu.CompilerParams(
            dimension_semantics=("parallel",), vmem_limit_bytes=VMEM_LIMIT),
        name="outproj",
    )(ret_out, diff_out, x2, mod, gain, w_out, wr_hi, wr_lo, br)


def _moe_kernel(h_ref, comb_ref, wg_ref, wu_ref, wd_ref, x1_ref, mod_ref, gain_ref, o_ref, acc_ref,
                *, tiles_per_batch):
    e = pl.program_id(1)

    @pl.when(e == 0)
    def _():
        acc_ref[...] = jnp.zeros_like(acc_ref)

    h = h_ref[...]
    comb = comb_ref[...]
    lane = lax.broadcasted_iota(jnp.int32, comb.shape, 1)
    w = jnp.sum(jnp.where(lane == e, comb, 0.0), axis=-1, keepdims=True)
    a = _dot(h, wg_ref[0])
    u = _dot(h, wu_ref[0])
    hid = (_silu(a) * u * w).astype(BF16)
    acc_ref[...] += _dot(hid, wd_ref[0])

    @pl.when(e == N_EXPERTS - 1)
    def _():
        b = pl.program_id(0) // tiles_per_batch
        gate2 = mod_ref[pl.ds(b, 1), 5 * D_MODEL:6 * D_MODEL]
        x2 = x1_ref[...] + gate2 * acc_ref[...]
        ms = jnp.mean(x2 * x2, axis=-1, keepdims=True)
        o_ref[...] = x2 * lax.rsqrt(ms + EPS) * gain_ref[...]


def _moe_dense(h2, comb, wg, wu, wd, x1, mod, gain, seq, tm):
    tokens = h2.shape[0]
    tiles_per_batch = seq // tm
    tok_spec = lambda w: pl.BlockSpec((tm, w), lambda i, e: (i, 0))
    full = lambda a: pl.BlockSpec(a.shape, lambda i, e: (0,) * a.ndim)
    return pl.pallas_call(
        functools.partial(_moe_kernel, tiles_per_batch=tiles_per_batch),
        grid=(tokens // tm, N_EXPERTS),
        in_specs=[tok_spec(D_MODEL), tok_spec(N_EXPERTS),
                  pl.BlockSpec((1, D_MODEL, D_EXPERT), lambda i, e: (e, 0, 0)),
                  pl.BlockSpec((1, D_MODEL, D_EXPERT), lambda i, e: (e, 0, 0)),
                  pl.BlockSpec((1, D_EXPERT, D_MODEL), lambda i, e: (e, 0, 0)),
                  tok_spec(D_MODEL), full(mod), full(gain)],
        out_specs=tok_spec(D_MODEL),
        out_shape=jax.ShapeDtypeStruct((tokens, D_MODEL), F32),
        scratch_shapes=[pltpu.VMEM((tm, D_MODEL), F32)],
        compiler_params=pltpu.CompilerParams(
            dimension_semantics=("parallel", "arbitrary"), vmem_limit_bytes=VMEM_LIMIT),
        name="moe",
    )(h2, comb, wg, wu, wd, x1, mod, gain)


def _rotary_tables(seq):
    half = RET_HEAD_DIM // 2
    inv_freq = 1.0 / (ROPE_BASE ** (jnp.arange(half, dtype=F32) / half))
    ang = jnp.arange(seq).astype(F32)[:, None] * inv_freq[None, :]
    cos = jnp.cos(ang)
    sin = jnp.sin(ang)
    return jnp.tile(cos, (1, 4)), jnp.concatenate([-sin, sin, -sin, sin], axis=1)


def _pick_tile(n, pref):
    t = min(n, pref)
    assert n % t == 0, (n, t)
    return t


def kernel(x, c, ada_w, ada_b, norm1_gain, norm2_gain, w_in, w_out, ret_gn_gain, lam_q1, lam_k1, lam_q2,
           lam_k2, diff_subln_gain, w_group, b_group, w_expert, b_expert, w_gate, w_up, w_down, final_gain):
    batch, seq, d = x.shape
    assert d == D_MODEL and batch <= 8 and ada_w.shape[0] == 1
    layer = 0
    lambda_init = 0.8 - 0.6 * math.exp(-0.3 * layer)
    tokens = batch * seq
    x2 = x.reshape(tokens, d)
    tm = _pick_tile(seq, 512)

    c_pad = jnp.zeros((8, d), F32).at[:batch].set(c)
    mod = _adaln(c_pad, ada_w[layer], ada_b[layer].reshape(1, -1))

    cos_t, sin_t = _rotary_tables(seq)
    rq, rk, rv, rg, dq, dk, dv = _inproj(
        x2, mod, norm1_gain[layer].reshape(1, d), w_in[layer].astype(BF16), cos_t, sin_t, seq, tm)

    ret_out = _retention(rq, rk, rv, rg, ret_gn_gain[layer].reshape(1, RET_WIDTH), batch, seq,
                         _pick_tile(seq, 128))
    diff_out = _diffattn(
        dq, dk, dv, lam_q1[layer].reshape(1, -1), lam_k1[layer].reshape(1, -1), lam_q2[layer].reshape(1, -1),
        lam_k2[layer].reshape(1, -1), diff_subln_gain[layer].reshape(1, -1), batch, seq, lambda_init,
        _pick_tile(seq, 256), _pick_tile(seq, 512))

    w_router = jnp.concatenate(
        [w_group[layer].T, jnp.zeros((8 - N_GROUPS, d), F32), w_expert[layer].reshape(d, N_EXPERTS).T], axis=0)
    b_router = jnp.concatenate(
        [b_group[layer], jnp.zeros((8 - N_GROUPS,), F32), b_expert[layer].reshape(N_EXPERTS)]).reshape(-1, 1)
    wr_hi = w_router.astype(BF16)
    wr_lo = (w_router - wr_hi.astype(F32)).astype(BF16)
    x1, h2, comb_t = _outproj(ret_out, diff_out, x2, mod, norm2_gain[layer].reshape(1, d),
                              w_out[layer].astype(BF16), wr_hi, wr_lo, b_router, seq, tm)

    wg = w_gate[layer].reshape(N_EXPERTS, d, D_EXPERT).astype(BF16)
    wu = w_up[layer].reshape(N_EXPERTS, d, D_EXPERT).astype(BF16)
    wd = w_down[layer].reshape(N_EXPERTS, D_EXPERT, d).astype(BF16)
    out = _moe_dense(h2, comb_t.T, wg, wu, wd, x1, mod, final_gain.reshape(1, d), seq, _pick_tile(seq, 1024))
    return out.reshape(batch, seq, d)
```

```python
import functools
import math

import jax
import jax.numpy as jnp
from jax import lax
from jax.experimental import pallas as pl
from jax.experimental.pallas import tpu as pltpu

F32 = jnp.float32
BF16 = jnp.bfloat16

D_MODEL = 1024
RET_HEAD_DIM = 64
RET_WIDTH = 512
RET_HEADS = 8
RET_PAIRS = RET_HEADS // 2
DIFF_QK_DIM = 64
DIFF_V_DIM = 128
DIFF_HEADS = 4
DIFF_WIDTH = 512
N_GROUPS = 4
EXPERTS_PER_GROUP = 8
N_EXPERTS = N_GROUPS * EXPERTS_PER_GROUP
D_EXPERT = 512
N_MOD = 6
ROPE_BASE = 10000.0
EPS = 1e-6
LANES = 128
ROUTER_ROWS = 8 + N_EXPERTS
VMEM_LIMIT = 56 * 1024 * 1024


def _dot(a, b):
    return jnp.dot(a, b, preferred_element_type=F32)


def _dot_nt(a, b):
    return lax.dot_general(a, b, (((1,), (1,)), ((), ())), preferred_element_type=F32)


def _dot_tn(a, b):
    return lax.dot_general(a, b, (((0,), (0,)), ((), ())), preferred_element_type=F32)


def _split_bf16(x):
    hi = x.astype(BF16)
    lo = (x - hi.astype(F32)).astype(BF16)
    return hi, lo


def _silu(x):
    return x / (1.0 + jnp.exp(-x))


def _adaln_kernel(c_ref, w_ref, b_ref, o_ref):
    ca = _silu(c_ref[...])
    c_hi, c_lo = _split_bf16(ca)
    w_hi, w_lo = _split_bf16(w_ref[...])
    o_ref[...] = _dot(c_hi, w_hi) + _dot(c_lo, w_hi) + _dot(c_hi, w_lo) + b_ref[...]


def _adaln(c_pad, ada_w, ada_b):
    n_out = ada_w.shape[1]
    tn = D_MODEL
    return pl.pallas_call(
        _adaln_kernel,
        grid=(n_out // tn,),
        in_specs=[
            pl.BlockSpec((8, D_MODEL), lambda j: (0, 0)),
            pl.BlockSpec((D_MODEL, tn), lambda j: (0, j)),
            pl.BlockSpec((1, tn), lambda j: (0, j)),
        ],
        out_specs=pl.BlockSpec((8, tn), lambda j: (0, j)),
        out_shape=jax.ShapeDtypeStruct((8, n_out), F32),
        compiler_params=pltpu.CompilerParams(vmem_limit_bytes=VMEM_LIMIT),
        name="adaln",
    )(c_pad, ada_w, ada_b)


def _norm_modulate(x, gain, shift, scale):
    ms = jnp.mean(x * x, axis=-1, keepdims=True)
    y = x * lax.rsqrt(ms + EPS) * gain
    return y * (1.0 + scale) + shift


def _rotary_slab(x, cos, sin_signed, lane_lo):
    swapped = jnp.where(lane_lo, pltpu.roll(x, 96, 1), pltpu.roll(x, 32, 1))
    return x * cos + swapped * sin_signed


def _inproj_kernel(x_ref, mod_ref, gain_ref, w_ref, wvt_ref, cos_ref, sin_ref,
                   rq_ref, rk_ref, rv_ref, rg_ref, dq_ref, dk_ref, dvt_ref, *, tiles_per_batch):
    b = pl.program_id(0) // tiles_per_batch
    shift = mod_ref[pl.ds(b, 1), 0:D_MODEL]
    scale = mod_ref[pl.ds(b, 1), D_MODEL:2 * D_MODEL]
    h = _norm_modulate(x_ref[...], gain_ref[...], shift, scale).astype(BF16)
    cos = cos_ref[...]
    sin = sin_ref[...]
    lane = lax.broadcasted_iota(jnp.int32, cos.shape, 1)
    lane_lo = (lane % 64) < 32

    def proj(chunk):
        return _dot(h, w_ref[:, chunk * RET_WIDTH:(chunk + 1) * RET_WIDTH])

    def rotary(acc, out_ref, post_scale):
        for s in range(RET_WIDTH // LANES):
            sl = slice(s * LANES, (s + 1) * LANES)
            out_ref[:, sl] = (_rotary_slab(acc[:, sl], cos, sin, lane_lo) * post_scale).astype(BF16)

    rotary(proj(0), rq_ref, 1.0)
    rotary(proj(1), rk_ref, RET_HEAD_DIM ** -0.5)
    rv_ref[...] = proj(2).astype(BF16)
    rg_ref[...] = _silu(proj(3)).astype(BF16)
    dq_ref[...] = (proj(4) * (DIFF_QK_DIM ** -0.5 * math.log2(math.e))).astype(BF16)
    dk_ref[...] = proj(5).astype(BF16)
    dvt_ref[...] = _dot_nt(wvt_ref[...], h).astype(BF16)


def _inproj(x2, mod, gain, w_main, w_vt, cos_t, sin_t, seq, tm):
    tokens = x2.shape[0]
    tiles_per_batch = seq // tm
    tok_spec = lambda w: pl.BlockSpec((tm, w), lambda i: (i, 0))
    tab_spec = pl.BlockSpec((tm, LANES), lambda i: (i % tiles_per_batch, 0))
    full = lambda a: pl.BlockSpec(a.shape, lambda i: (0,) * a.ndim)
    out = jax.ShapeDtypeStruct((tokens, RET_WIDTH), BF16)
    return pl.pallas_call(
        functools.partial(_inproj_kernel, tiles_per_batch=tiles_per_batch),
        grid=(tokens // tm,),
        in_specs=[tok_spec(D_MODEL), full(mod), full(gain), full(w_main), full(w_vt), tab_spec, tab_spec],
        out_specs=[tok_spec(RET_WIDTH)] * 6 + [pl.BlockSpec((DIFF_WIDTH, tm), lambda i: (0, i))],
        out_shape=[out] * 6 + [jax.ShapeDtypeStruct((DIFF_WIDTH, tokens), BF16)],
        compiler_params=pltpu.CompilerParams(
            dimension_semantics=("parallel",), vmem_limit_bytes=VMEM_LIMIT),
        name="inproj",
    )(x2, mod, gain, w_main, w_vt, cos_t, sin_t)


def _retention_kernel(q_ref, k_ref, v_ref, g_ref, dec_ref, qdec_ref, kdec_ref, rdec_ref,
                      bmask_ref, gmean_ref, gain_ref, o_ref, state_ref, *, chunk):
    @pl.when(pl.program_id(1) == 0)
    def _():
        state_ref[...] = jnp.zeros_like(state_ref)

    lane = lax.broadcasted_iota(jnp.int32, (chunk, LANES), 1)
    first_head = lane < RET_HEAD_DIM
    gmean = gmean_ref[...]
    bmask = bmask_ref[...]
    for p in range(RET_PAIRS):
        sl = slice(p * LANES, (p + 1) * LANES)
        q = q_ref[:, sl]
        k = k_ref[:, sl]
        v = v_ref[:, sl]
        zero = jnp.zeros_like(q)
        q_stack = jnp.concatenate([jnp.where(first_head, q, zero), jnp.where(first_head, zero, q)], axis=0)
        scores = _dot_nt(q_stack, k) * dec_ref[p]
        intra2 = _dot(scores.astype(BF16), v)
        intra = jnp.where(first_head, intra2[:chunk], intra2[chunk:])
        state = state_ref[p]
        cross = _dot(q, state.astype(BF16)) * qdec_ref[:, sl]
        y = intra + cross
        k_dec = (k.astype(F32) * kdec_ref[:, sl]).astype(BF16)
        state_ref[p] = state * rdec_ref[p] + _dot_tn(k_dec, v) * bmask
        y_hi, y_lo = _split_bf16(y)
        mu = _dot(y_hi, gmean) + _dot(y_lo, gmean)
        d = y - mu
        d_hi, d_lo = _split_bf16(d * d)
        var = _dot(d_hi, gmean) + _dot(d_lo, gmean)
        yn = d * lax.rsqrt(var + EPS) * gain_ref[:, sl]
        o_ref[:, sl] = (g_ref[:, sl].astype(F32) * yn).astype(BF16)


def _retention_tables(chunk):
    heads = jnp.arange(RET_HEADS, dtype=F32)
    log_gamma = jnp.log(1.0 - jnp.exp2(-5.0 - heads))
    idx = jnp.arange(chunk)
    rel = (idx[:, None] - idx[None, :]).astype(F32)
    decay = jnp.where(rel[None] >= 0, jnp.exp(log_gamma[:, None, None] * jnp.maximum(rel, 0.0)[None]), 0.0)
    dec2 = decay.reshape(RET_PAIRS, 2 * chunk, chunk)
    lane_lg = jnp.repeat(log_gamma, RET_HEAD_DIM)
    qdec = jnp.exp(lane_lg[None, :] * (idx + 1).astype(F32)[:, None])
    kdec = jnp.exp(lane_lg[None, :] * (chunk - 1 - idx).astype(F32)[:, None])
    rdec = jnp.exp(lane_lg * chunk).reshape(RET_PAIRS, LANES, 1) * jnp.ones((1, 1, LANES), F32)
    blk = jnp.arange(LANES) // RET_HEAD_DIM
    bmask = (blk[:, None] == blk[None, :]).astype(F32)
    gmean = (bmask / RET_HEAD_DIM).astype(BF16)
    return dec2, qdec, kdec, rdec, bmask, gmean


def _retention(rq, rk, rv, rg, gn_gain, batch, seq, chunk):
    nc = seq // chunk
    dec2, qdec, kdec, rdec, bmask, gmean = _retention_tables(chunk)
    tok_spec = pl.BlockSpec((chunk, RET_WIDTH), lambda b, n: (b * nc + n, 0))
    full = lambda a: pl.BlockSpec(a.shape, lambda b, n: (0,) * a.ndim)
    return pl.pallas_call(
        functools.partial(_retention_kernel, chunk=chunk),
        grid=(batch, nc),
        in_specs=[tok_spec] * 4 + [full(dec2), full(qdec), full(kdec), full(rdec), full(bmask),
                                   full(gmean), full(gn_gain)],
        out_specs=tok_spec,
        out_shape=jax.ShapeDtypeStruct(rq.shape, BF16),
        scratch_shapes=[pltpu.VMEM((RET_PAIRS, LANES, LANES), F32)],
        compiler_params=pltpu.CompilerParams(
            dimension_semantics=("parallel", "arbitrary"), vmem_limit_bytes=VMEM_LIMIT),
        name="retention",
    )(rq, rk, rv, rg, dec2, qdec, kdec, rdec, bmask, gmean, gn_gain)


NEG_BIG = -1e30


V_EXT_ROWS = DIFF_V_DIM + 16
QUERY_CHUNK = 256
SCORES_AHEAD = 1


def _diffattn_kernel(q_ref, k_ref, vt_ref, lq1_ref, lk1_ref, lq2_ref, lk2_ref, gain_ref, o_ref,
                     qs_ref, vext_ref, m_ref, acc_ref, *, tq, tk, lambda_init):
    i = pl.program_id(2)
    nk = vext_ref.shape[0]

    @pl.when(i == 0)
    def _():
        for j in range(nk):
            vext_ref[j, 0:DIFF_V_DIM, :] = vt_ref[:, j * tk:(j + 1) * tk]
            vext_ref[j, DIFF_V_DIM:V_EXT_ROWS, :] = jnp.ones((V_EXT_ROWS - DIFF_V_DIM, tk), BF16)

    q = q_ref[...]
    lane = lax.broadcasted_iota(jnp.int32, q.shape, 1)
    zero = jnp.zeros_like(q)
    qs_ref[0:tq, :] = jnp.where(lane < DIFF_QK_DIM, q, zero)
    qs_ref[tq:2 * tq, :] = jnp.where(lane < DIFF_QK_DIM, zero, q)
    m_ref[...] = jnp.full_like(m_ref, NEG_BIG)
    acc_ref[...] = jnp.zeros_like(acc_ref)

    def step(j, masked):
        start = pl.multiple_of(j * tk, tk)
        k = k_ref[pl.ds(start, tk), :]
        vext = vext_ref[j]
        n_chunks = 2 * tq // QUERY_CHUNK
        chunk = lambda c: slice(c * QUERY_CHUNK, (c + 1) * QUERY_CHUNK)
        scores = lambda c: _dot_nt(k, qs_ref[chunk(c), :])
        ahead = [scores(c) for c in range(SCORES_AHEAD)]
        for c in range(n_chunks):
            cs = chunk(c)
            st = ahead.pop(0)
            if c + SCORES_AHEAD < n_chunks:
                ahead.append(scores(c + SCORES_AHEAD))
            if masked:
                key = start + lax.broadcasted_iota(jnp.int32, st.shape, 0)
                query = i * tq + (c * QUERY_CHUNK) % tq + lax.broadcasted_iota(jnp.int32, st.shape, 1)
                st = jnp.where(key <= query, st, NEG_BIG)
            m_old = m_ref[:, cs]
            m_new = jnp.maximum(m_old, jnp.max(st, axis=0, keepdims=True))
            alpha = jnp.exp2(m_old - m_new)
            p = jnp.exp2(st - m_new).astype(BF16)
            acc_ref[:, cs] = alpha * acc_ref[:, cs] + _dot(vext, p)
            m_ref[:, cs] = m_new

    last = (i * tq + tq - 1) // tk
    lax.fori_loop(0, last, lambda j, c: (step(j, False), c)[1], 0)
    step(last, True)

    lam = (jnp.exp(jnp.sum(lq1_ref[...] * lk1_ref[...], axis=-1, keepdims=True))
           - jnp.exp(jnp.sum(lq2_ref[...] * lk2_ref[...], axis=-1, keepdims=True)) + lambda_init)
    acc = acc_ref[...]
    o2 = acc[0:DIFF_V_DIM, :] * (1.0 / acc[DIFF_V_DIM:DIFF_V_DIM + 1, :])
    o = (o2[:, :tq] - lam * o2[:, tq:]).T
    ms = jnp.mean(o * o, axis=-1, keepdims=True)
    o = o * lax.rsqrt(ms + EPS) * gain_ref[...] * (1.0 - lambda_init)
    o_ref[...] = o.astype(BF16)


def _diffattn(dq, dk, dvt, lam_q1, lam_k1, lam_q2, lam_k2, gain, batch, seq, lambda_init, tq, tk):
    nq = seq // tq
    q_spec = pl.BlockSpec((tq, LANES), lambda b, h, i: (b * nq + i, h))
    k_spec = pl.BlockSpec((seq, LANES), lambda b, h, i: (b, h))
    vt_spec = pl.BlockSpec((DIFF_V_DIM, seq), lambda b, h, i: (h, b))
    vec = lambda a: pl.BlockSpec(a.shape, lambda b, h, i: (0, 0))
    return pl.pallas_call(
        functools.partial(_diffattn_kernel, tq=tq, tk=tk, lambda_init=lambda_init),
        grid=(batch, DIFF_HEADS, nq),
        in_specs=[q_spec, k_spec, vt_spec, vec(lam_q1), vec(lam_k1), vec(lam_q2), vec(lam_k2), vec(gain)],
        out_specs=q_spec,
        out_shape=jax.ShapeDtypeStruct(dq.shape, BF16),
        scratch_shapes=[
            pltpu.VMEM((2 * tq, LANES), BF16),
            pltpu.VMEM((seq // tk, V_EXT_ROWS, tk), BF16),
            pltpu.VMEM((1, 2 * tq), F32),
            pltpu.VMEM((V_EXT_ROWS, 2 * tq), F32),
        ],
        compiler_params=pltpu.CompilerParams(
            dimension_semantics=("parallel", "parallel", "arbitrary"), vmem_limit_bytes=VMEM_LIMIT),
        name="diffattn",
    )(dq, dk, dvt, lam_q1, lam_k1, lam_q2, lam_k2, gain)


def _route(logits):
    r = [logits[g:g + 1, :] for g in range(N_GROUPS)]
    gmax = jnp.maximum(jnp.maximum(r[0], r[1]), jnp.maximum(r[2], r[3]))
    g_idx = jnp.where(r[0] == gmax, 0, jnp.where(r[1] == gmax, 1, jnp.where(r[2] == gmax, 2, 3)))
    denom = sum(jnp.exp(rg - gmax) for rg in r)
    g_weight = 1.0 / denom
    sel = jnp.zeros((EXPERTS_PER_GROUP, logits.shape[1]), F32)
    for g in range(N_GROUPS):
        rows = logits[8 + g * EXPERTS_PER_GROUP:8 + (g + 1) * EXPERTS_PER_GROUP, :]
        sel = jnp.where(g_idx == g, rows, sel)
    eidx = lax.broadcasted_iota(jnp.int32, sel.shape, 0)
    v1 = jnp.max(sel, axis=0, keepdims=True)
    i1 = jnp.min(jnp.where(sel == v1, eidx, EXPERTS_PER_GROUP), axis=0, keepdims=True)
    sel2 = jnp.where(eidx == i1, -jnp.inf, sel)
    v2 = jnp.max(sel2, axis=0, keepdims=True)
    i2 = jnp.min(jnp.where(sel2 == v2, eidx, EXPERTS_PER_GROUP), axis=0, keepdims=True)
    e2 = jnp.exp(v2 - v1)
    w1 = g_weight / (1.0 + e2)
    w2 = g_weight * e2 / (1.0 + e2)
    return g_idx, i1, i2, w1, w2


def _outproj_kernel(ret_ref, diff_ref, x_ref, mod_ref, gain_ref, wo_ref, wr_hi_ref, wr_lo_ref, br_ref,
                    x1_ref, h2_ref, comb_ref, *, tiles_per_batch):
    b = pl.program_id(0) // tiles_per_batch
    mix = _dot(ret_ref[...], wo_ref[0:RET_WIDTH, :]) + _dot(diff_ref[...], wo_ref[RET_WIDTH:, :])
    gate1 = mod_ref[pl.ds(b, 1), 2 * D_MODEL:3 * D_MODEL]
    x1 = x_ref[...] + gate1 * mix
    x1_ref[...] = x1
    shift = mod_ref[pl.ds(b, 1), 3 * D_MODEL:4 * D_MODEL]
    scale = mod_ref[pl.ds(b, 1), 4 * D_MODEL:5 * D_MODEL]
    h2 = _norm_modulate(x1, gain_ref[...], shift, scale)
    h_hi, h_lo = _split_bf16(h2)
    h2_ref[...] = h_hi
    wr_hi = wr_hi_ref[...]
    logits = _dot_nt(wr_hi, h_hi) + _dot_nt(wr_lo_ref[...], h_hi) + _dot_nt(wr_hi, h_lo) + br_ref[...]
    g_idx, i1, i2, w1, w2 = _route(logits)
    eidx = lax.broadcasted_iota(jnp.int32, (EXPERTS_PER_GROUP, logits.shape[1]), 0)
    inner = jnp.where(eidx == i1, w1, 0.0) + jnp.where(eidx == i2, w2, 0.0)
    for g in range(N_GROUPS):
        comb_ref[g * EXPERTS_PER_GROUP:(g + 1) * EXPERTS_PER_GROUP, :] = jnp.where(g_idx == g, inner, 0.0)


def _outproj(ret_out, diff_out, x2, mod, gain, w_out, wr_hi, wr_lo, br, seq, tm):
    tokens = x2.shape[0]
    tiles_per_batch = seq // tm
    tok_spec = lambda w: pl.BlockSpec((tm, w), lambda i: (i, 0))
    full = lambda a: pl.BlockSpec(a.shape, lambda i: (0,) * a.ndim)
    return pl.pallas_call(
        functools.partial(_outproj_kernel, tiles_per_batch=tiles_per_batch),
        grid=(tokens // tm,),
        in_specs=[tok_spec(RET_WIDTH), tok_spec(DIFF_WIDTH), tok_spec(D_MODEL), full(mod), full(gain),
                  full(w_out), full(wr_hi), full(wr_lo), full(br)],
        out_specs=[tok_spec(D_MODEL), tok_spec(D_MODEL), pl.BlockSpec((N_EXPERTS, tm), lambda i: (0, i))],
        out_shape=[jax.ShapeDtypeStruct((tokens, D_MODEL), F32),
                   jax.ShapeDtypeStruct((tokens, D_MODEL), BF16),
                   jax.ShapeDtypeStruct((N_EXPERTS, tokens), F32)],
        compiler_params=pltpu.CompilerParams(
            dimension_semantics=("parallel",), vmem_limit_bytes=VMEM_LIMIT),
        name="outproj",
    )(ret_out, diff_out, x2, mod, gain, w_out, wr_hi, wr_lo, br)


def _moe_kernel(h_ref, comb_ref, wg_ref, wu_ref, wd_ref, x1_ref, mod_ref, gain_ref, o_ref, acc_ref,
                *, tiles_per_batch):
    e = pl.program_id(1)

    @pl.when(e == 0)
    def _():
        acc_ref[...] = jnp.zeros_like(acc_ref)

    h = h_ref[...]
    comb = comb_ref[...]
    lane = lax.broadcasted_iota(jnp.int32, comb.shape, 1)
    w = jnp.sum(jnp.where(lane == e, comb, 0.0), axis=-1, keepdims=True)
    a = _dot(h, wg_ref[0])
    u = _dot(h, wu_ref[0])
    hid = (_silu(a) * u * w).astype(BF16)
    acc_ref[...] += _dot(hid, wd_ref[0])

    @pl.when(e == N_EXPERTS - 1)
    def _():
        b = pl.program_id(0) // tiles_per_batch
        gate2 = mod_ref[pl.ds(b, 1), 5 * D_MODEL:6 * D_MODEL]
        x2 = x1_ref[...] + gate2 * acc_ref[...]
        ms = jnp.mean(x2 * x2, axis=-1, keepdims=True)
        o_ref[...] = x2 * lax.rsqrt(ms + EPS) * gain_ref[...]


def _moe_dense(h2, comb, wg, wu, wd, x1, mod, gain, seq, tm):
    tokens = h2.shape[0]
    tiles_per_batch = seq // tm
    tok_spec = lambda w: pl.BlockSpec((tm, w), lambda i, e: (i, 0))
    full = lambda a: pl.BlockSpec(a.shape, lambda i, e: (0,) * a.ndim)
    return pl.pallas_call(
        functools.partial(_moe_kernel, tiles_per_batch=tiles_per_batch),
        grid=(tokens // tm, N_EXPERTS),
        in_specs=[tok_spec(D_MODEL), tok_spec(N_EXPERTS),
                  pl.BlockSpec((1, D_MODEL, D_EXPERT), lambda i, e: (e, 0, 0)),
                  pl.BlockSpec((1, D_MODEL, D_EXPERT), lambda i, e: (e, 0, 0)),
                  pl.BlockSpec((1, D_EXPERT, D_MODEL), lambda i, e: (e, 0, 0)),
                  tok_spec(D_MODEL), full(mod), full(gain)],
        out_specs=tok_spec(D_MODEL),
        out_shape=jax.ShapeDtypeStruct((tokens, D_MODEL), F32),
        scratch_shapes=[pltpu.VMEM((tm, D_MODEL), F32)],
        compiler_params=pltpu.CompilerParams(
            dimension_semantics=("parallel", "arbitrary"), vmem_limit_bytes=VMEM_LIMIT),
        name="moe",
    )(h2, comb, wg, wu, wd, x1, mod, gain)


def _rotary_tables(seq):
    half = RET_HEAD_DIM // 2
    inv_freq = 1.0 / (ROPE_BASE ** (jnp.arange(half, dtype=F32) / half))
    ang = jnp.arange(seq).astype(F32)[:, None] * inv_freq[None, :]
    cos = jnp.cos(ang)
    sin = jnp.sin(ang)
    return jnp.tile(cos, (1, 4)), jnp.concatenate([-sin, sin, -sin, sin], axis=1)


def _pick_tile(n, pref):
    t = min(n, pref)
    assert n % t == 0, (n, t)
    return t


def kernel(x, c, ada_w, ada_b, norm1_gain, norm2_gain, w_in, w_out, ret_gn_gain, lam_q1, lam_k1, lam_q2,
           lam_k2, diff_subln_gain, w_group, b_group, w_expert, b_expert, w_gate, w_up, w_down, final_gain):
    batch, seq, d = x.shape
    assert d == D_MODEL and batch <= 8 and ada_w.shape[0] == 1
    layer = 0
    lambda_init = 0.8 - 0.6 * math.exp(-0.3 * layer)
    tokens = batch * seq
    x2 = x.reshape(tokens, d)
    tm = _pick_tile(seq, 512)

    c_pad = jnp.zeros((8, d), F32).at[:batch].set(c)
    mod = _adaln(c_pad, ada_w[layer], ada_b[layer].reshape(1, -1))

    cos_t, sin_t = _rotary_tables(seq)
    n_main = 4 * RET_WIDTH + 2 * DIFF_WIDTH
    w_in_bf = w_in[layer].astype(BF16)
    rq, rk, rv, rg, dq, dk, dvt = _inproj(
        x2, mod, norm1_gain[layer].reshape(1, d), w_in_bf[:, :n_main], w_in_bf[:, n_main:].T, cos_t, sin_t,
        seq, tm)

    ret_out = _retention(rq, rk, rv, rg, ret_gn_gain[layer].reshape(1, RET_WIDTH), batch, seq,
                         _pick_tile(seq, 128))
    diff_out = _diffattn(
        dq, dk, dvt, lam_q1[layer].reshape(1, -1), lam_k1[layer].reshape(1, -1), lam_q2[layer].reshape(1, -1),
        lam_k2[layer].reshape(1, -1), diff_subln_gain[layer].reshape(1, -1), batch, seq, lambda_init,
        _pick_tile(seq, 512), _pick_tile(seq, 512))

    w_router = jnp.concatenate(
        [w_group[layer].T, jnp.zeros((8 - N_GROUPS, d), F32), w_expert[layer].reshape(d, N_EXPERTS).T], axis=0)
    b_router = jnp.concatenate(
        [b_group[layer], jnp.zeros((8 - N_GROUPS,), F32), b_expert[layer].reshape(N_EXPERTS)]).reshape(-1, 1)
    wr_hi = w_router.astype(BF16)
    wr_lo = (w_router - wr_hi.astype(F32)).astype(BF16)
    x1, h2, comb_t = _outproj(ret_out, diff_out, x2, mod, norm2_gain[layer].reshape(1, d),
                              w_out[layer].astype(BF16), wr_hi, wr_lo, b_router, seq, tm)

    wg = w_gate[layer].reshape(N_EXPERTS, d, D_EXPERT).astype(BF16)
    wu = w_up[layer].reshape(N_EXPERTS, d, D_EXPERT).astype(BF16)
    wd = w_down[layer].reshape(N_EXPERTS, D_EXPERT, d).astype(BF16)
    out = _moe_dense(h2, comb_t.T, wg, wu, wd, x1, mod, final_gain.reshape(1, d), seq, _pick_tile(seq, 1024))
    return out.reshape(batch, seq, d)
```

```python
import functools
import math

import jax
import jax.numpy as jnp
from jax import lax
from jax.experimental import pallas as pl
from jax.experimental.pallas import tpu as pltpu

F32 = jnp.float32
BF16 = jnp.bfloat16

D_MODEL = 1024
RET_HEAD_DIM = 64
RET_WIDTH = 512
RET_HEADS = 8
RET_PAIRS = RET_HEADS // 2
DIFF_QK_DIM = 64
DIFF_V_DIM = 128
DIFF_HEADS = 4
DIFF_WIDTH = 512
N_GROUPS = 4
EXPERTS_PER_GROUP = 8
N_EXPERTS = N_GROUPS * EXPERTS_PER_GROUP
D_EXPERT = 512
N_MOD = 6
ROPE_BASE = 10000.0
EPS = 1e-6
LANES = 128
ROUTER_ROWS = 8 + N_EXPERTS
VMEM_LIMIT = 56 * 1024 * 1024


def _dot(a, b):
    return jnp.dot(a, b, preferred_element_type=F32)


def _dot_nt(a, b):
    return lax.dot_general(a, b, (((1,), (1,)), ((), ())), preferred_element_type=F32)


def _dot_tn(a, b):
    return lax.dot_general(a, b, (((0,), (0,)), ((), ())), preferred_element_type=F32)


def _split_bf16(x):
    hi = x.astype(BF16)
    lo = (x - hi.astype(F32)).astype(BF16)
    return hi, lo


def _silu(x):
    return x / (1.0 + jnp.exp(-x))


def _adaln_kernel(c_ref, w_ref, b_ref, o_ref):
    ca = _silu(c_ref[...])
    c_hi, c_lo = _split_bf16(ca)
    w_hi, w_lo = _split_bf16(w_ref[...])
    o_ref[...] = _dot(c_hi, w_hi) + _dot(c_lo, w_hi) + _dot(c_hi, w_lo) + b_ref[...]


def _adaln(c_pad, ada_w, ada_b):
    n_out = ada_w.shape[1]
    tn = D_MODEL
    return pl.pallas_call(
        _adaln_kernel,
        grid=(n_out // tn,),
        in_specs=[
            pl.BlockSpec((8, D_MODEL), lambda j: (0, 0)),
            pl.BlockSpec((D_MODEL, tn), lambda j: (0, j)),
            pl.BlockSpec((1, tn), lambda j: (0, j)),
        ],
        out_specs=pl.BlockSpec((8, tn), lambda j: (0, j)),
        out_shape=jax.ShapeDtypeStruct((8, n_out), F32),
        compiler_params=pltpu.CompilerParams(vmem_limit_bytes=VMEM_LIMIT),
        name="adaln",
    )(c_pad, ada_w, ada_b)


def _norm_modulate(x, gain, shift, scale):
    ms = jnp.mean(x * x, axis=-1, keepdims=True)
    y = x * lax.rsqrt(ms + EPS) * gain
    return y * (1.0 + scale) + shift


def _rotary_slab(x, cos, sin_signed, lane_lo):
    swapped = jnp.where(lane_lo, pltpu.roll(x, 96, 1), pltpu.roll(x, 32, 1))
    return x * cos + swapped * sin_signed


def _inproj_kernel(x_ref, mod_ref, gain_ref, w_ref, wvt_ref, cos_ref, sin_ref,
                   rq_ref, rk_ref, rv_ref, rg_ref, dq_ref, dk_ref, dvt_ref, *, tiles_per_batch):
    b = pl.program_id(0) // tiles_per_batch
    shift = mod_ref[pl.ds(b, 1), 0:D_MODEL]
    scale = mod_ref[pl.ds(b, 1), D_MODEL:2 * D_MODEL]
    h = _norm_modulate(x_ref[...], gain_ref[...], shift, scale).astype(BF16)
    cos = cos_ref[...]
    sin = sin_ref[...]
    lane = lax.broadcasted_iota(jnp.int32, cos.shape, 1)
    lane_lo = (lane % 64) < 32

    def proj(chunk):
        return _dot(h, w_ref[:, chunk * RET_WIDTH:(chunk + 1) * RET_WIDTH])

    def rotary(acc, out_ref, post_scale):
        for s in range(RET_WIDTH // LANES):
            sl = slice(s * LANES, (s + 1) * LANES)
            out_ref[:, sl] = (_rotary_slab(acc[:, sl], cos, sin, lane_lo) * post_scale).astype(BF16)

    rotary(proj(0), rq_ref, 1.0)
    rotary(proj(1), rk_ref, RET_HEAD_DIM ** -0.5)
    rv_ref[...] = proj(2).astype(BF16)
    rg_ref[...] = _silu(proj(3)).astype(BF16)
    dq_ref[...] = (proj(4) * (DIFF_QK_DIM ** -0.5 * math.log2(math.e))).astype(BF16)
    dk_ref[...] = proj(5).astype(BF16)
    dvt_ref[...] = _dot_nt(wvt_ref[...], h).astype(BF16)


def _inproj(x2, mod, gain, w_main, w_vt, cos_t, sin_t, seq, tm):
    tokens = x2.shape[0]
    tiles_per_batch = seq // tm
    tok_spec = lambda w: pl.BlockSpec((tm, w), lambda i: (i, 0))
    tab_spec = pl.BlockSpec((tm, LANES), lambda i: (i % tiles_per_batch, 0))
    full = lambda a: pl.BlockSpec(a.shape, lambda i: (0,) * a.ndim)
    out = jax.ShapeDtypeStruct((tokens, RET_WIDTH), BF16)
    return pl.pallas_call(
        functools.partial(_inproj_kernel, tiles_per_batch=tiles_per_batch),
        grid=(tokens // tm,),
        in_specs=[tok_spec(D_MODEL), full(mod), full(gain), full(w_main), full(w_vt), tab_spec, tab_spec],
        out_specs=[tok_spec(RET_WIDTH)] * 6 + [pl.BlockSpec((DIFF_WIDTH, tm), lambda i: (0, i))],
        out_shape=[out] * 6 + [jax.ShapeDtypeStruct((DIFF_WIDTH, tokens), BF16)],
        compiler_params=pltpu.CompilerParams(
            dimension_semantics=("parallel",), vmem_limit_bytes=VMEM_LIMIT),
        name="inproj",
    )(x2, mod, gain, w_main, w_vt, cos_t, sin_t)


def _retention_kernel(q_ref, k_ref, v_ref, g_ref, dec_ref, qdec_ref, kdec_ref, rdec_ref,
                      bmask_ref, gmean_ref, gain_ref, o_ref, state_ref, *, chunk):
    @pl.when(pl.program_id(1) == 0)
    def _():
        state_ref[...] = jnp.zeros_like(state_ref)

    lane = lax.broadcasted_iota(jnp.int32, (chunk, LANES), 1)
    first_head = lane < RET_HEAD_DIM
    gmean = gmean_ref[...]
    bmask = bmask_ref[...]
    for p in range(RET_PAIRS):
        sl = slice(p * LANES, (p + 1) * LANES)
        q = q_ref[:, sl]
        k = k_ref[:, sl]
        v = v_ref[:, sl]
        zero = jnp.zeros_like(q)
        q_stack = jnp.concatenate([jnp.where(first_head, q, zero), jnp.where(first_head, zero, q)], axis=0)
        scores = _dot_nt(q_stack, k) * dec_ref[p]
        intra2 = _dot(scores.astype(BF16), v)
        intra = jnp.where(first_head, intra2[:chunk], intra2[chunk:])
        state = state_ref[p]
        cross = _dot(q, state.astype(BF16)) * qdec_ref[:, sl]
        y = intra + cross
        k_dec = (k.astype(F32) * kdec_ref[:, sl]).astype(BF16)
        state_ref[p] = state * rdec_ref[p] + _dot_tn(k_dec, v) * bmask
        y_hi, y_lo = _split_bf16(y)
        mu = _dot(y_hi, gmean) + _dot(y_lo, gmean)
        d = y - mu
        d_hi, d_lo = _split_bf16(d * d)
        var = _dot(d_hi, gmean) + _dot(d_lo, gmean)
        yn = d * lax.rsqrt(var + EPS) * gain_ref[:, sl]
        o_ref[:, sl] = (g_ref[:, sl].astype(F32) * yn).astype(BF16)


def _retention_tables(chunk):
    heads = jnp.arange(RET_HEADS, dtype=F32)
    log_gamma = jnp.log(1.0 - jnp.exp2(-5.0 - heads))
    idx = jnp.arange(chunk)
    rel = (idx[:, None] - idx[None, :]).astype(F32)
    decay = jnp.where(rel[None] >= 0, jnp.exp(log_gamma[:, None, None] * jnp.maximum(rel, 0.0)[None]), 0.0)
    dec2 = decay.reshape(RET_PAIRS, 2 * chunk, chunk)
    lane_lg = jnp.repeat(log_gamma, RET_HEAD_DIM)
    qdec = jnp.exp(lane_lg[None, :] * (idx + 1).astype(F32)[:, None])
    kdec = jnp.exp(lane_lg[None, :] * (chunk - 1 - idx).astype(F32)[:, None])
    rdec = jnp.exp(lane_lg * chunk).reshape(RET_PAIRS, LANES, 1) * jnp.ones((1, 1, LANES), F32)
    blk = jnp.arange(LANES) // RET_HEAD_DIM
    bmask = (blk[:, None] == blk[None, :]).astype(F32)
    gmean = (bmask / RET_HEAD_DIM).astype(BF16)
    return dec2, qdec, kdec, rdec, bmask, gmean


def _retention(rq, rk, rv, rg, gn_gain, batch, seq, chunk):
    nc = seq // chunk
    dec2, qdec, kdec, rdec, bmask, gmean = _retention_tables(chunk)
    tok_spec = pl.BlockSpec((chunk, RET_WIDTH), lambda b, n: (b * nc + n, 0))
    full = lambda a: pl.BlockSpec(a.shape, lambda b, n: (0,) * a.ndim)
    return pl.pallas_call(
        functools.partial(_retention_kernel, chunk=chunk),
        grid=(batch, nc),
        in_specs=[tok_spec] * 4 + [full(dec2), full(qdec), full(kdec), full(rdec), full(bmask),
                                   full(gmean), full(gn_gain)],
        out_specs=tok_spec,
        out_shape=jax.ShapeDtypeStruct(rq.shape, BF16),
        scratch_shapes=[pltpu.VMEM((RET_PAIRS, LANES, LANES), F32)],
        compiler_params=pltpu.CompilerParams(
            dimension_semantics=("parallel", "arbitrary"), vmem_limit_bytes=VMEM_LIMIT),
        name="retention",
    )(rq, rk, rv, rg, dec2, qdec, kdec, rdec, bmask, gmean, gn_gain)


NEG_BIG = -1e30


V_EXT_ROWS = DIFF_V_DIM + 16
QUERY_CHUNK = 256
SCORES_AHEAD = 1


def _diffattn_kernel(q_ref, k_ref, vt_ref, lq1_ref, lk1_ref, lq2_ref, lk2_ref, gain_ref, o_ref,
                     qs_ref, vext_ref, m_ref, acc_ref, *, tq, tk, lambda_init):
    i = pl.program_id(2)
    nk = vext_ref.shape[0]

    @pl.when(i == 0)
    def _():
        for j in range(nk):
            vext_ref[j, 0:DIFF_V_DIM, :] = vt_ref[:, j * tk:(j + 1) * tk]
            vext_ref[j, DIFF_V_DIM:V_EXT_ROWS, :] = jnp.ones((V_EXT_ROWS - DIFF_V_DIM, tk), BF16)

    q = q_ref[...]
    lane = lax.broadcasted_iota(jnp.int32, q.shape, 1)
    zero = jnp.zeros_like(q)
    qs_ref[0:tq, :] = jnp.where(lane < DIFF_QK_DIM, q, zero)
    qs_ref[tq:2 * tq, :] = jnp.where(lane < DIFF_QK_DIM, zero, q)
    m_ref[...] = jnp.full_like(m_ref, NEG_BIG)
    acc_ref[...] = jnp.zeros_like(acc_ref)

    def step(j, masked):
        start = pl.multiple_of(j * tk, tk)
        k = k_ref[pl.ds(start, tk), :]
        vext = vext_ref[j]
        n_chunks = 2 * tq // QUERY_CHUNK
        chunk = lambda c: slice(c * QUERY_CHUNK, (c + 1) * QUERY_CHUNK)
        scores = lambda c: _dot_nt(k, qs_ref[chunk(c), :])
        ahead = [scores(c) for c in range(SCORES_AHEAD)]
        for c in range(n_chunks):
            cs = chunk(c)
            st = ahead.pop(0)
            if c + SCORES_AHEAD < n_chunks:
                ahead.append(scores(c + SCORES_AHEAD))
            if masked:
                key = start + lax.broadcasted_iota(jnp.int32, st.shape, 0)
                query = i * tq + (c * QUERY_CHUNK) % tq + lax.broadcasted_iota(jnp.int32, st.shape, 1)
                st = jnp.where(key <= query, st, NEG_BIG)
            m_old = m_ref[:, cs]
            m_new = jnp.maximum(m_old, jnp.max(st, axis=0, keepdims=True))
            alpha = jnp.exp2(m_old - m_new)
            p = jnp.exp2(st - m_new).astype(BF16)
            acc_ref[:, cs] = alpha * acc_ref[:, cs] + _dot(vext, p)
            m_ref[:, cs] = m_new

    last = (i * tq + tq - 1) // tk
    lax.fori_loop(0, last, lambda j, c: (step(j, False), c)[1], 0)
    step(last, True)

    lam = (jnp.exp(jnp.sum(lq1_ref[...] * lk1_ref[...], axis=-1, keepdims=True))
           - jnp.exp(jnp.sum(lq2_ref[...] * lk2_ref[...], axis=-1, keepdims=True)) + lambda_init)
    acc = acc_ref[...]
    o2 = acc[0:DIFF_V_DIM, :] * (1.0 / acc[DIFF_V_DIM:DIFF_V_DIM + 1, :])
    o = (o2[:, :tq] - lam * o2[:, tq:]).T
    ms = jnp.mean(o * o, axis=-1, keepdims=True)
    o = o * lax.rsqrt(ms + EPS) * gain_ref[...] * (1.0 - lambda_init)
    o_ref[...] = o.astype(BF16)


def _diffattn(dq, dk, dvt, lam_q1, lam_k1, lam_q2, lam_k2, gain, batch, seq, lambda_init, tq, tk):
    nq = seq // tq
    q_spec = pl.BlockSpec((tq, LANES), lambda b, h, i: (b * nq + i, h))
    k_spec = pl.BlockSpec((seq, LANES), lambda b, h, i: (b, h))
    vt_spec = pl.BlockSpec((DIFF_V_DIM, seq), lambda b, h, i: (h, b))
    vec = lambda a: pl.BlockSpec(a.shape, lambda b, h, i: (0, 0))
    return pl.pallas_call(
        functools.partial(_diffattn_kernel, tq=tq, tk=tk, lambda_init=lambda_init),
        grid=(batch, DIFF_HEADS, nq),
        in_specs=[q_spec, k_spec, vt_spec, vec(lam_q1), vec(lam_k1), vec(lam_q2), vec(lam_k2), vec(gain)],
        out_specs=q_spec,
        out_shape=jax.ShapeDtypeStruct(dq.shape, BF16),
        scratch_shapes=[
            pltpu.VMEM((2 * tq, LANES), BF16),
            pltpu.VMEM((seq // tk, V_EXT_ROWS, tk), BF16),
            pltpu.VMEM((1, 2 * tq), F32),
            pltpu.VMEM((V_EXT_ROWS, 2 * tq), F32),
        ],
        compiler_params=pltpu.CompilerParams(
            dimension_semantics=("parallel", "parallel", "arbitrary"), vmem_limit_bytes=VMEM_LIMIT),
        name="diffattn",
    )(dq, dk, dvt, lam_q1, lam_k1, lam_q2, lam_k2, gain)


def _route(logits):
    r = [logits[g:g + 1, :] for g in range(N_GROUPS)]
    gmax = jnp.maximum(jnp.maximum(r[0], r[1]), jnp.maximum(r[2], r[3]))
    g_idx = jnp.where(r[0] == gmax, 0, jnp.where(r[1] == gmax, 1, jnp.where(r[2] == gmax, 2, 3)))
    denom = sum(jnp.exp(rg - gmax) for rg in r)
    g_weight = 1.0 / denom
    sel = jnp.zeros((EXPERTS_PER_GROUP, logits.shape[1]), F32)
    for g in range(N_GROUPS):
        rows = logits[8 + g * EXPERTS_PER_GROUP:8 + (g + 1) * EXPERTS_PER_GROUP, :]
        sel = jnp.where(g_idx == g, rows, sel)
    eidx = lax.broadcasted_iota(jnp.int32, sel.shape, 0)
    v1 = jnp.max(sel, axis=0, keepdims=True)
    i1 = jnp.min(jnp.where(sel == v1, eidx, EXPERTS_PER_GROUP), axis=0, keepdims=True)
    sel2 = jnp.where(eidx == i1, -jnp.inf, sel)
    v2 = jnp.max(sel2, axis=0, keepdims=True)
    i2 = jnp.min(jnp.where(sel2 == v2, eidx, EXPERTS_PER_GROUP), axis=0, keepdims=True)
    e2 = jnp.exp(v2 - v1)
    w1 = g_weight / (1.0 + e2)
    w2 = g_weight * e2 / (1.0 + e2)
    return g_idx, i1, i2, w1, w2


def _outproj_kernel(ret_ref, diff_ref, x_ref, mod_ref, gain_ref, wo_ref, wr_hi_ref, wr_lo_ref, br_ref, tri_ref,
                    x1_ref, h2_ref, ri_ref, rw_ref, cnt_ref, *, tiles_per_batch):
    b = pl.program_id(0) // tiles_per_batch
    mix = _dot(ret_ref[...], wo_ref[0:RET_WIDTH, :]) + _dot(diff_ref[...], wo_ref[RET_WIDTH:, :])
    gate1 = mod_ref[pl.ds(b, 1), 2 * D_MODEL:3 * D_MODEL]
    x1 = x_ref[...] + gate1 * mix
    x1_ref[...] = x1
    shift = mod_ref[pl.ds(b, 1), 3 * D_MODEL:4 * D_MODEL]
    scale = mod_ref[pl.ds(b, 1), 4 * D_MODEL:5 * D_MODEL]
    h2 = _norm_modulate(x1, gain_ref[...], shift, scale)
    h_hi, h_lo = _split_bf16(h2)
    h2_ref[...] = h_hi
    wr_hi = wr_hi_ref[...]
    logits = _dot_nt(wr_hi, h_hi) + _dot_nt(wr_lo_ref[...], h_hi) + _dot_nt(wr_hi, h_lo) + br_ref[...]
    g_idx, i1, i2, w1, w2 = _route(logits)
    e1 = g_idx * EXPERTS_PER_GROUP + i1
    e2 = g_idx * EXPERTS_PER_GROUP + i2
    eidx = lax.broadcasted_iota(jnp.int32, (N_EXPERTS, logits.shape[1]), 0)
    hit1 = eidx == e1
    hit2 = eidx == e2
    onehot = jnp.where(hit1 | hit2, 1.0, 0.0)
    before = _dot(onehot.astype(BF16), tri_ref[...])
    r1 = jnp.sum(jnp.where(hit1, before, 0.0), axis=0, keepdims=True)
    r2 = jnp.sum(jnp.where(hit2, before, 0.0), axis=0, keepdims=True)
    zi = jnp.zeros_like(e1)
    ri_ref[...] = jnp.concatenate([e1, e2, r1.astype(jnp.int32), r2.astype(jnp.int32), zi, zi, zi, zi], axis=0)
    zf = jnp.zeros_like(w1)
    rw_ref[...] = jnp.concatenate([w1, w2, zf, zf, zf, zf, zf, zf], axis=0)
    counts = jnp.sum(onehot, axis=1, keepdims=True)
    cnt_ref[0] = jnp.broadcast_to(counts, (N_EXPERTS, LANES)).astype(jnp.int32)


def _outproj(ret_out, diff_out, x2, mod, gain, w_out, wr_hi, wr_lo, br, seq, tm):
    tokens = x2.shape[0]
    tiles_per_batch = seq // tm
    n_tiles = tokens // tm
    tri = (jnp.arange(tm)[:, None] < jnp.arange(tm)[None, :]).astype(BF16)
    tok_spec = lambda w: pl.BlockSpec((tm, w), lambda i: (i, 0))
    row_spec = pl.BlockSpec((8, tm), lambda i: (0, i))
    full = lambda a: pl.BlockSpec(a.shape, lambda i: (0,) * a.ndim)
    return pl.pallas_call(
        functools.partial(_outproj_kernel, tiles_per_batch=tiles_per_batch),
        grid=(n_tiles,),
        in_specs=[tok_spec(RET_WIDTH), tok_spec(DIFF_WIDTH), tok_spec(D_MODEL), full(mod), full(gain),
                  full(w_out), full(wr_hi), full(wr_lo), full(br), full(tri)],
        out_specs=[tok_spec(D_MODEL), tok_spec(D_MODEL), row_spec, row_spec,
                   pl.BlockSpec((1, N_EXPERTS, LANES), lambda i: (i, 0, 0))],
        out_shape=[jax.ShapeDtypeStruct((tokens, D_MODEL), F32),
                   jax.ShapeDtypeStruct((tokens, D_MODEL), BF16),
                   jax.ShapeDtypeStruct((8, tokens), jnp.int32),
                   jax.ShapeDtypeStruct((8, tokens), F32),
                   jax.ShapeDtypeStruct((n_tiles, N_EXPERTS, LANES), jnp.int32)],
        compiler_params=pltpu.CompilerParams(
            dimension_semantics=("parallel",), vmem_limit_bytes=VMEM_LIMIT),
        name="outproj",
    )(ret_out, diff_out, x2, mod, gain, w_out, wr_hi, wr_lo, br, tri)


CHUNK = 8
TMX = 256
PACKED = D_MODEL // 2
MAX_TAIL_CHUNKS = N_EXPERTS * (TMX // CHUNK - 1)
U32 = jnp.uint32


def _local_rows(tm):
    rows = 2 * tm + N_EXPERTS * (CHUNK - 1)
    return (rows + 15) // 16 * 16


def _sorted_rows_alloc(tokens, tm):
    worst = 2 * tokens + (tokens // tm) * N_EXPERTS * (CHUNK - 1) + N_EXPERTS * (TMX - CHUNK)
    return (worst + TMX - 1) // TMX * TMX


def _pack_rows(x):
    lo = lax.bitcast_convert_type(x[:, :PACKED], U32) >> 16
    hi = lax.bitcast_convert_type(x[:, PACKED:], U32) & jnp.uint32(0xFFFF0000)
    return lo | hi


def _unpack_rows(w):
    lo = lax.bitcast_convert_type(w << 16, F32)
    hi = lax.bitcast_convert_type(w & jnp.uint32(0xFFFF0000), F32)
    return jnp.concatenate([lo, hi], axis=1)


def _dispatch_plan(cnt, tokens, tm):
    i32 = jnp.int32
    nch_max = _local_rows(tm) // CHUNK
    pad = (cnt + CHUNK - 1) // CHUNK * CHUNK
    local_end = jnp.cumsum(pad, axis=1)
    local_start = local_end - pad
    seg_rows = jnp.sum(pad, axis=0)
    seg_pad = (seg_rows + TMX - 1) // TMX * TMX
    seg_end = jnp.cumsum(seg_pad)
    seg_start = seg_end - seg_pad
    run_dst = seg_start[None, :] + jnp.cumsum(pad, axis=0) - pad
    row = CHUNK * jnp.arange(nch_max, dtype=i32)
    owner = jnp.minimum(jnp.sum(local_end[:, None, :] <= row[None, :, None], axis=-1), N_EXPERTS - 1)
    chunk_dst = (jnp.take_along_axis(run_dst, owner, axis=1) + row[None, :]
                 - jnp.take_along_axis(local_start, owner, axis=1))
    tail_n = (seg_pad - seg_rows) // CHUNK
    tail_end = jnp.cumsum(tail_n)
    k = jnp.arange(MAX_TAIL_CHUNKS, dtype=i32)
    towner = jnp.minimum(jnp.sum(tail_end[None, :] <= k[:, None], axis=-1), N_EXPERTS - 1)
    tail_dst = (seg_start + seg_rows)[towner] + CHUNK * (k - (tail_end - tail_n)[towner])
    m = TMX * jnp.arange(_sorted_rows_alloc(tokens, tm) // TMX, dtype=i32)
    tile_expert = jnp.minimum(jnp.sum(seg_end[None, :] <= m[:, None], axis=-1), N_EXPERTS - 1)
    return dict(
        local_start=local_start.reshape(-1).astype(i32),
        n_chunks=(local_end[:, -1] // CHUNK).astype(i32),
        chunk_dst=chunk_dst.reshape(-1).astype(i32),
        tail_dst=tail_dst.astype(i32),
        n_tail=tail_end[-1:].astype(i32),
        tile_expert=tile_expert.astype(i32),
        n_used=(seg_end[-1:] // TMX).astype(i32),
    )


def _local_slots(ri_ref, local_start_ref, tile):
    e1, e2 = ri_ref[0:1, :], ri_ref[1:2, :]
    s1, s2 = ri_ref[2:3, :], ri_ref[3:4, :]
    for e in range(N_EXPERTS):
        start = local_start_ref[tile * N_EXPERTS + e]
        s1 = s1 + jnp.where(e1 == e, start, 0)
        s2 = s2 + jnp.where(e2 == e, start, 0)
    return s1, s2


def _dispatch_kernel(local_start_ref, n_chunks_ref, chunk_dst_ref, tail_dst_ref, n_tail_ref, n_used_ref,
                     h_ref, ri_ref, xs_ref, buf_ref, zero_ref, sem_ref, tail_sem_ref, *, r_loc):
    b = pl.program_id(0)
    nb = pl.num_programs(0)
    slot = b % 2
    nch_max = r_loc // CHUNK

    def chunk_copy(tile, sl, j):
        row = pl.multiple_of(j * CHUNK, CHUNK)
        dst = pl.multiple_of(chunk_dst_ref[tile * nch_max + j], CHUNK)
        return pltpu.make_async_copy(buf_ref.at[sl, pl.ds(row, CHUNK), :], xs_ref.at[pl.ds(dst, CHUNK), :],
                                     sem_ref.at[sl])

    def drain(tile, sl):
        lax.fori_loop(0, n_chunks_ref[tile], lambda j, c: (chunk_copy(tile, sl, j).wait(), c)[1], 0)

    @pl.when(b >= 2)
    def _():
        drain(b - 2, slot)

    s1, s2 = _local_slots(ri_ref, local_start_ref, b)
    rows = lax.broadcasted_iota(jnp.int32, (r_loc, s1.shape[1]), 0)
    perm = jnp.where((rows == s1) | (rows == s2), 1.0, 0.0).astype(BF16)
    buf_ref[slot] = _pack_rows(_dot(perm, h_ref[...]))
    lax.fori_loop(0, n_chunks_ref[b], lambda j, c: (chunk_copy(b, slot, j).start(), c)[1], 0)

    @pl.when(b == nb - 1)
    def _():
        zero_ref[...] = jnp.zeros_like(zero_ref)

        def tail_copy(k):
            dst = pl.multiple_of(tail_dst_ref[k], CHUNK)
            return pltpu.make_async_copy(zero_ref.at[pl.ds(0, CHUNK), :], xs_ref.at[pl.ds(dst, CHUNK), :],
                                         tail_sem_ref.at[0])

        def unused_tile_copy(m):
            dst = pl.multiple_of(m * TMX, TMX)
            return pltpu.make_async_copy(zero_ref, xs_ref.at[pl.ds(dst, TMX), :], tail_sem_ref.at[1])

        n_alloc = xs_ref.shape[0] // TMX
        lax.fori_loop(0, n_tail_ref[0], lambda k, c: (tail_copy(k).start(), c)[1], 0)
        lax.fori_loop(n_used_ref[0], n_alloc, lambda m, c: (unused_tile_copy(m).start(), c)[1], 0)
        lax.fori_loop(0, n_tail_ref[0], lambda k, c: (tail_copy(k).wait(), c)[1], 0)
        lax.fori_loop(n_used_ref[0], n_alloc, lambda m, c: (unused_tile_copy(m).wait(), c)[1], 0)

        @pl.when(b >= 1)
        def _():
            drain(b - 1, 1 - slot)

        drain(b, slot)


def _dispatch(h2, ri, plan, tm):
    tokens = h2.shape[0]
    r_loc = _local_rows(tm)
    grid_spec = pltpu.PrefetchScalarGridSpec(
        num_scalar_prefetch=6,
        grid=(tokens // tm,),
        in_specs=[pl.BlockSpec((tm, D_MODEL), lambda i, *_: (i, 0)),
                  pl.BlockSpec((8, tm), lambda i, *_: (0, i))],
        out_specs=pl.BlockSpec(memory_space=pl.ANY),
        scratch_shapes=[pltpu.VMEM((2, r_loc, PACKED), U32), pltpu.VMEM((TMX, PACKED), U32),
                        pltpu.SemaphoreType.DMA((2,)), pltpu.SemaphoreType.DMA((2,))],
    )
    return pl.pallas_call(
        functools.partial(_dispatch_kernel, r_loc=r_loc),
        grid_spec=grid_spec,
        out_shape=jax.ShapeDtypeStruct((_sorted_rows_alloc(tokens, tm), PACKED), U32),
        compiler_params=pltpu.CompilerParams(
            dimension_semantics=("arbitrary",), vmem_limit_bytes=VMEM_LIMIT),
        name="dispatch",
    )(plan["local_start"], plan["n_chunks"], plan["chunk_dst"], plan["tail_dst"], plan["n_tail"], plan["n_used"],
      h2, ri)


def _experts_kernel(tile_expert_ref, n_used_ref, xs_ref, wg_ref, wu_ref, wd_ref, ys_ref, wg_bf, wu_bf, wd_bf):
    m = pl.program_id(0)

    @pl.when(m < n_used_ref[0])
    def _():
        @pl.when((m == 0) | (tile_expert_ref[m] != tile_expert_ref[jnp.maximum(m - 1, 0)]))
        def _():
            wg_bf[...] = wg_ref[0].astype(BF16)
            wu_bf[...] = wu_ref[0].astype(BF16)
            wd_bf[...] = wd_ref[0].astype(BF16)

        x = _unpack_rows(xs_ref[...]).astype(BF16)
        a = _dot(x, wg_bf[...])
        u = _dot(x, wu_bf[...])
        hid = (_silu(a) * u).astype(BF16)
        y = _dot(hid, wd_bf[...])
        ys_ref[...] = _pack_rows(y.astype(BF16).astype(F32))

    @pl.when(m >= n_used_ref[0])
    def _():
        ys_ref[...] = jnp.zeros_like(ys_ref)


def _experts(xs, plan, wg, wu, wd):
    n_tiles = xs.shape[0] // TMX
    last_used = lambda m, n_used: jnp.minimum(m, n_used[0] - 1)
    row_spec = pl.BlockSpec((TMX, PACKED), lambda m, te, nu: (last_used(m, nu), 0))
    out_spec = pl.BlockSpec((TMX, PACKED), lambda m, te, nu: (m, 0))
    w_spec = lambda shape: pl.BlockSpec((1,) + shape, lambda m, te, nu: (te[last_used(m, nu)], 0, 0))
    grid_spec = pltpu.PrefetchScalarGridSpec(
        num_scalar_prefetch=2,
        grid=(n_tiles,),
        in_specs=[row_spec, w_spec((D_MODEL, D_EXPERT)), w_spec((D_MODEL, D_EXPERT)),
                  w_spec((D_EXPERT, D_MODEL))],
        out_specs=out_spec,
        scratch_shapes=[pltpu.VMEM((D_MODEL, D_EXPERT), BF16), pltpu.VMEM((D_MODEL, D_EXPERT), BF16),
                        pltpu.VMEM((D_EXPERT, D_MODEL), BF16)],
    )
    return pl.pallas_call(
        _experts_kernel,
        grid_spec=grid_spec,
        out_shape=jax.ShapeDtypeStruct(xs.shape, U32),
        compiler_params=pltpu.CompilerParams(
            dimension_semantics=("arbitrary",), vmem_limit_bytes=VMEM_LIMIT),
        name="experts",
    )(plan["tile_expert"], plan["n_used"], xs, wg, wu, wd)


def _combine_kernel(local_start_ref, n_chunks_ref, chunk_dst_ref, ys_ref, ri_ref, rw_ref, x1_ref, mod_ref,
                    gain_ref, o_ref, buf_ref, sem_ref, *, r_loc, tiles_per_batch):
    b = pl.program_id(0)
    nb = pl.num_programs(0)
    slot = b % 2
    nch_max = r_loc // CHUNK

    def chunk_copy(tile, sl, j):
        row = pl.multiple_of(j * CHUNK, CHUNK)
        src = pl.multiple_of(chunk_dst_ref[tile * nch_max + j], CHUNK)
        return pltpu.make_async_copy(ys_ref.at[pl.ds(src, CHUNK), :], buf_ref.at[sl, pl.ds(row, CHUNK), :],
                                     sem_ref.at[sl])

    def fetch(tile, sl):
        lax.fori_loop(0, n_chunks_ref[tile], lambda j, c: (chunk_copy(tile, sl, j).start(), c)[1], 0)

    @pl.when(b == 0)
    def _():
        buf_ref[...] = jnp.zeros_like(buf_ref)
        fetch(0, 0)

    @pl.when(b + 1 < nb)
    def _():
        fetch(b + 1, 1 - slot)

    lax.fori_loop(0, n_chunks_ref[b], lambda j, c: (chunk_copy(b, slot, j).wait(), c)[1], 0)

    s1, s2 = _local_slots(ri_ref, local_start_ref, b)
    rows = lax.broadcasted_iota(jnp.int32, (r_loc, s1.shape[1]), 0)
    hit1 = rows == s1
    hit2 = rows == s2
    w_row = jnp.sum(jnp.where(hit1, rw_ref[0:1, :], jnp.where(hit2, rw_ref[1:2, :], 0.0)), axis=1, keepdims=True)
    perm = jnp.where(hit1 | hit2, 1.0, 0.0).astype(BF16)
    yw = (_unpack_rows(buf_ref[slot]) * w_row).astype(BF16)
    moe = _dot_tn(perm, yw)
    batch = b // tiles_per_batch
    gate2 = mod_ref[pl.ds(batch, 1), 5 * D_MODEL:6 * D_MODEL]
    x2 = x1_ref[...] + gate2 * moe
    ms = jnp.mean(x2 * x2, axis=-1, keepdims=True)
    o_ref[...] = x2 * lax.rsqrt(ms + EPS) * gain_ref[...]


def _combine(ys, ri, rw, x1, mod, gain, plan, seq, tm):
    tokens = x1.shape[0]
    r_loc = _local_rows(tm)
    row_spec = pl.BlockSpec((8, tm), lambda i, *_: (0, i))
    tok_spec = pl.BlockSpec((tm, D_MODEL), lambda i, *_: (i, 0))
    full = lambda a: pl.BlockSpec(a.shape, lambda i, *_: (0,) * a.ndim)
    grid_spec = pltpu.PrefetchScalarGridSpec(
        num_scalar_prefetch=3,
        grid=(tokens // tm,),
        in_specs=[pl.BlockSpec(memory_space=pl.ANY), row_spec, row_spec, tok_spec, full(mod), full(gain)],
        out_specs=tok_spec,
        scratch_shapes=[pltpu.VMEM((2, r_loc, PACKED), U32), pltpu.SemaphoreType.DMA((2,))],
    )
    return pl.pallas_call(
        functools.partial(_combine_kernel, r_loc=r_loc, tiles_per_batch=seq // tm),
        grid_spec=grid_spec,
        out_shape=jax.ShapeDtypeStruct((tokens, D_MODEL), F32),
        compiler_params=pltpu.CompilerParams(
            dimension_semantics=("arbitrary",), vmem_limit_bytes=VMEM_LIMIT),
        name="combine",
    )(plan["local_start"], plan["n_chunks"], plan["chunk_dst"], ys, ri, rw, x1, mod, gain)


def _rotary_tables(seq):
    half = RET_HEAD_DIM // 2
    inv_freq = 1.0 / (ROPE_BASE ** (jnp.arange(half, dtype=F32) / half))
    ang = jnp.arange(seq).astype(F32)[:, None] * inv_freq[None, :]
    cos = jnp.cos(ang)
    sin = jnp.sin(ang)
    return jnp.tile(cos, (1, 4)), jnp.concatenate([-sin, sin, -sin, sin], axis=1)


def _pick_tile(n, pref):
    t = min(n, pref)
    assert n % t == 0, (n, t)
    return t


def kernel(x, c, ada_w, ada_b, norm1_gain, norm2_gain, w_in, w_out, ret_gn_gain, lam_q1, lam_k1, lam_q2,
           lam_k2, diff_subln_gain, w_group, b_group, w_expert, b_expert, w_gate, w_up, w_down, final_gain):
    batch, seq, d = x.shape
    assert d == D_MODEL and batch <= 8 and ada_w.shape[0] == 1
    layer = 0
    lambda_init = 0.8 - 0.6 * math.exp(-0.3 * layer)
    tokens = batch * seq
    x2 = x.reshape(tokens, d)
    tm = _pick_tile(seq, 512)

    c_pad = jnp.zeros((8, d), F32).at[:batch].set(c)
    mod = _adaln(c_pad, ada_w[layer], ada_b[layer].reshape(1, -1))

    cos_t, sin_t = _rotary_tables(seq)
    n_main = 4 * RET_WIDTH + 2 * DIFF_WIDTH
    w_in_bf = w_in[layer].astype(BF16)
    rq, rk, rv, rg, dq, dk, dvt = _inproj(
        x2, mod, norm1_gain[layer].reshape(1, d), w_in_bf[:, :n_main], w_in_bf[:, n_main:].T, cos_t, sin_t,
        seq, tm)

    ret_out = _retention(rq, rk, rv, rg, ret_gn_gain[layer].reshape(1, RET_WIDTH), batch, seq,
                         _pick_tile(seq, 128))
    diff_out = _diffattn(
        dq, dk, dvt, lam_q1[layer].reshape(1, -1), lam_k1[layer].reshape(1, -1), lam_q2[layer].reshape(1, -1),
        lam_k2[layer].reshape(1, -1), diff_subln_gain[layer].reshape(1, -1), batch, seq, lambda_init,
        _pick_tile(seq, 512), _pick_tile(seq, 512))

    w_router = jnp.concatenate(
        [w_group[layer].T, jnp.zeros((8 - N_GROUPS, d), F32), w_expert[layer].reshape(d, N_EXPERTS).T], axis=0)
    b_router = jnp.concatenate(
        [b_group[layer], jnp.zeros((8 - N_GROUPS,), F32), b_expert[layer].reshape(N_EXPERTS)]).reshape(-1, 1)
    wr_hi = w_router.astype(BF16)
    wr_lo = (w_router - wr_hi.astype(F32)).astype(BF16)
    x1, h2, ri, rw, cnt = _outproj(ret_out, diff_out, x2, mod, norm2_gain[layer].reshape(1, d),
                                   w_out[layer].astype(BF16), wr_hi, wr_lo, b_router, seq, tm)

    plan = _dispatch_plan(cnt[:, :, 0], tokens, tm)
    xs = _dispatch(h2, ri, plan, tm)
    ys = _experts(xs, plan, w_gate[layer].reshape(N_EXPERTS, d, D_EXPERT),
                  w_up[layer].reshape(N_EXPERTS, d, D_EXPERT), w_down[layer].reshape(N_EXPERTS, D_EXPERT, d))
    out = _combine(ys, ri, rw, x1, mod, final_gain.reshape(1, d), plan, seq, tm)
    return out.reshape(batch, seq, d)
```

```python
import functools
import math

import jax
import jax.numpy as jnp
from jax import lax
from jax.experimental import pallas as pl
from jax.experimental.pallas import tpu as pltpu

F32 = jnp.float32
BF16 = jnp.bfloat16

D_MODEL = 1024
RET_HEAD_DIM = 64
RET_WIDTH = 512
RET_HEADS = 8
RET_PAIRS = RET_HEADS // 2
DIFF_QK_DIM = 64
DIFF_V_DIM = 128
DIFF_HEADS = 4
DIFF_WIDTH = 512
N_GROUPS = 4
EXPERTS_PER_GROUP = 8
N_EXPERTS = N_GROUPS * EXPERTS_PER_GROUP
D_EXPERT = 512
N_MOD = 6
ROPE_BASE = 10000.0
EPS = 1e-6
LANES = 128
ROUTER_ROWS = 8 + N_EXPERTS
VMEM_LIMIT = 56 * 1024 * 1024


def _dot(a, b):
    return jnp.dot(a, b, preferred_element_type=F32)


def _dot_nt(a, b):
    return lax.dot_general(a, b, (((1,), (1,)), ((), ())), preferred_element_type=F32)


def _dot_tn(a, b):
    return lax.dot_general(a, b, (((0,), (0,)), ((), ())), preferred_element_type=F32)


def _split_bf16(x):
    hi = x.astype(BF16)
    lo = (x - hi.astype(F32)).astype(BF16)
    return hi, lo


def _silu(x):
    return x / (1.0 + jnp.exp(-x))


def _adaln_kernel(c_ref, w_ref, b_ref, o_ref):
    ca = _silu(c_ref[...])
    c_hi, c_lo = _split_bf16(ca)
    w_hi, w_lo = _split_bf16(w_ref[...])
    o_ref[...] = _dot(c_hi, w_hi) + _dot(c_lo, w_hi) + _dot(c_hi, w_lo) + b_ref[...]


def _adaln(c_pad, ada_w, ada_b):
    n_out = ada_w.shape[1]
    tn = D_MODEL
    return pl.pallas_call(
        _adaln_kernel,
        grid=(n_out // tn,),
        in_specs=[
            pl.BlockSpec((8, D_MODEL), lambda j: (0, 0)),
            pl.BlockSpec((D_MODEL, tn), lambda j: (0, j)),
            pl.BlockSpec((1, tn), lambda j: (0, j)),
        ],
        out_specs=pl.BlockSpec((8, tn), lambda j: (0, j)),
        out_shape=jax.ShapeDtypeStruct((8, n_out), F32),
        compiler_params=pltpu.CompilerParams(vmem_limit_bytes=VMEM_LIMIT),
        name="adaln",
    )(c_pad, ada_w, ada_b)


def _norm_modulate(x, gain, shift, scale):
    ms = jnp.mean(x * x, axis=-1, keepdims=True)
    y = x * lax.rsqrt(ms + EPS) * gain
    return y * (1.0 + scale) + shift


def _rotary_slab(x, cos, sin_signed, lane_lo):
    swapped = jnp.where(lane_lo, pltpu.roll(x, 96, 1), pltpu.roll(x, 32, 1))
    return x * cos + swapped * sin_signed


def _inproj_kernel(x_ref, mod_ref, gain_ref, w_ref, wvt_ref, cos_ref, sin_ref,
                   rq_ref, rk_ref, rv_ref, rg_ref, dq_ref, dk_ref, dvt_ref, *, tiles_per_batch):
    b = pl.program_id(0) // tiles_per_batch
    shift = mod_ref[pl.ds(b, 1), 0:D_MODEL]
    scale = mod_ref[pl.ds(b, 1), D_MODEL:2 * D_MODEL]
    h = _norm_modulate(x_ref[...], gain_ref[...], shift, scale).astype(BF16)
    cos = cos_ref[...]
    sin = sin_ref[...]
    lane = lax.broadcasted_iota(jnp.int32, cos.shape, 1)
    lane_lo = (lane % 64) < 32

    def proj(chunk):
        return _dot(h, w_ref[:, chunk * RET_WIDTH:(chunk + 1) * RET_WIDTH])

    def rotary(acc, out_ref, post_scale):
        for s in range(RET_WIDTH // LANES):
            sl = slice(s * LANES, (s + 1) * LANES)
            out_ref[:, sl] = (_rotary_slab(acc[:, sl], cos, sin, lane_lo) * post_scale).astype(BF16)

    rotary(proj(0), rq_ref, 1.0)
    rotary(proj(1), rk_ref, RET_HEAD_DIM ** -0.5)
    rv_ref[...] = proj(2).astype(BF16)
    rg_ref[...] = _silu(proj(3)).astype(BF16)
    dq_ref[...] = (proj(4) * (DIFF_QK_DIM ** -0.5 * math.log2(math.e))).astype(BF16)
    dk_ref[...] = proj(5).astype(BF16)
    dvt_ref[...] = _dot_nt(wvt_ref[...], h).astype(BF16)


def _inproj(x2, mod, gain, w_main, w_vt, cos_t, sin_t, seq, tm):
    tokens = x2.shape[0]
    tiles_per_batch = seq // tm
    tok_spec = lambda w: pl.BlockSpec((tm, w), lambda i: (i, 0))
    tab_spec = pl.BlockSpec((tm, LANES), lambda i: (i % tiles_per_batch, 0))
    full = lambda a: pl.BlockSpec(a.shape, lambda i: (0,) * a.ndim)
    out = jax.ShapeDtypeStruct((tokens, RET_WIDTH), BF16)
    return pl.pallas_call(
        functools.partial(_inproj_kernel, tiles_per_batch=tiles_per_batch),
        grid=(tokens // tm,),
        in_specs=[tok_spec(D_MODEL), full(mod), full(gain), full(w_main), full(w_vt), tab_spec, tab_spec],
        out_specs=[tok_spec(RET_WIDTH)] * 6 + [pl.BlockSpec((DIFF_WIDTH, tm), lambda i: (0, i))],
        out_shape=[out] * 6 + [jax.ShapeDtypeStruct((DIFF_WIDTH, tokens), BF16)],
        compiler_params=pltpu.CompilerParams(
            dimension_semantics=("parallel",), vmem_limit_bytes=VMEM_LIMIT),
        name="inproj",
    )(x2, mod, gain, w_main, w_vt, cos_t, sin_t)


def _retention_kernel(q_ref, k_ref, v_ref, g_ref, dec_ref, qdec_ref, kdec_ref, rdec_ref,
                      bmask_ref, gmean_ref, gain_ref, o_ref, state_ref, *, chunk):
    @pl.when(pl.program_id(1) == 0)
    def _():
        state_ref[...] = jnp.zeros_like(state_ref)

    lane = lax.broadcasted_iota(jnp.int32, (chunk, LANES), 1)
    first_head = lane < RET_HEAD_DIM
    gmean = gmean_ref[...]
    bmask = bmask_ref[...]
    for p in range(RET_PAIRS):
        sl = slice(p * LANES, (p + 1) * LANES)
        q = q_ref[:, sl]
        k = k_ref[:, sl]
        v = v_ref[:, sl]
        zero = jnp.zeros_like(q)
        q_stack = jnp.concatenate([jnp.where(first_head, q, zero), jnp.where(first_head, zero, q)], axis=0)
        scores = _dot_nt(q_stack, k) * dec_ref[p]
        intra2 = _dot(scores.astype(BF16), v)
        intra = jnp.where(first_head, intra2[:chunk], intra2[chunk:])
        state = state_ref[p]
        cross = _dot(q, state.astype(BF16)) * qdec_ref[:, sl]
        y = intra + cross
        k_dec = (k.astype(F32) * kdec_ref[:, sl]).astype(BF16)
        state_ref[p] = state * rdec_ref[p] + _dot_tn(k_dec, v) * bmask
        y_hi, y_lo = _split_bf16(y)
        mu = _dot(y_hi, gmean) + _dot(y_lo, gmean)
        d = y - mu
        d_hi, d_lo = _split_bf16(d * d)
        var = _dot(d_hi, gmean) + _dot(d_lo, gmean)
        yn = d * lax.rsqrt(var + EPS) * gain_ref[:, sl]
        o_ref[:, sl] = (g_ref[:, sl].astype(F32) * yn).astype(BF16)


def _retention_tables(chunk):
    heads = jnp.arange(RET_HEADS, dtype=F32)
    log_gamma = jnp.log(1.0 - jnp.exp2(-5.0 - heads))
    idx = jnp.arange(chunk)
    rel = (idx[:, None] - idx[None, :]).astype(F32)
    decay = jnp.where(rel[None] >= 0, jnp.exp(log_gamma[:, None, None] * jnp.maximum(rel, 0.0)[None]), 0.0)
    dec2 = decay.reshape(RET_PAIRS, 2 * chunk, chunk)
    lane_lg = jnp.repeat(log_gamma, RET_HEAD_DIM)
    qdec = jnp.exp(lane_lg[None, :] * (idx + 1).astype(F32)[:, None])
    kdec = jnp.exp(lane_lg[None, :] * (chunk - 1 - idx).astype(F32)[:, None])
    rdec = jnp.exp(lane_lg * chunk).reshape(RET_PAIRS, LANES, 1) * jnp.ones((1, 1, LANES), F32)
    blk = jnp.arange(LANES) // RET_HEAD_DIM
    bmask = (blk[:, None] == blk[None, :]).astype(F32)
    gmean = (bmask / RET_HEAD_DIM).astype(BF16)
    return dec2, qdec, kdec, rdec, bmask, gmean


def _retention(rq, rk, rv, rg, gn_gain, batch, seq, chunk):
    nc = seq // chunk
    dec2, qdec, kdec, rdec, bmask, gmean = _retention_tables(chunk)
    tok_spec = pl.BlockSpec((chunk, RET_WIDTH), lambda b, n: (b * nc + n, 0))
    full = lambda a: pl.BlockSpec(a.shape, lambda b, n: (0,) * a.ndim)
    return pl.pallas_call(
        functools.partial(_retention_kernel, chunk=chunk),
        grid=(batch, nc),
        in_specs=[tok_spec] * 4 + [full(dec2), full(qdec), full(kdec), full(rdec), full(bmask),
                                   full(gmean), full(gn_gain)],
        out_specs=tok_spec,
        out_shape=jax.ShapeDtypeStruct(rq.shape, BF16),
        scratch_shapes=[pltpu.VMEM((RET_PAIRS, LANES, LANES), F32)],
        compiler_params=pltpu.CompilerParams(
            dimension_semantics=("parallel", "arbitrary"), vmem_limit_bytes=VMEM_LIMIT),
        name="retention",
    )(rq, rk, rv, rg, dec2, qdec, kdec, rdec, bmask, gmean, gn_gain)


NEG_BIG = -1e30


V_EXT_ROWS = DIFF_V_DIM + 16
QUERY_CHUNK = 256
SCORES_AHEAD = 1


def _diffattn_kernel(q_ref, k_ref, vt_ref, lq1_ref, lk1_ref, lq2_ref, lk2_ref, gain_ref, o_ref,
                     qs_ref, vext_ref, m_ref, acc_ref, *, tq, tk, lambda_init):
    i = pl.program_id(2)
    nk = vext_ref.shape[0]

    @pl.when(i == 0)
    def _():
        for j in range(nk):
            vext_ref[j, 0:DIFF_V_DIM, :] = vt_ref[:, j * tk:(j + 1) * tk]
            vext_ref[j, DIFF_V_DIM:V_EXT_ROWS, :] = jnp.ones((V_EXT_ROWS - DIFF_V_DIM, tk), BF16)

    q = q_ref[...]
    lane = lax.broadcasted_iota(jnp.int32, q.shape, 1)
    zero = jnp.zeros_like(q)
    qs_ref[0:tq, :] = jnp.where(lane < DIFF_QK_DIM, q, zero)
    qs_ref[tq:2 * tq, :] = jnp.where(lane < DIFF_QK_DIM, zero, q)
    m_ref[...] = jnp.full_like(m_ref, NEG_BIG)
    acc_ref[...] = jnp.zeros_like(acc_ref)

    def step(j, masked):
        start = pl.multiple_of(j * tk, tk)
        k = k_ref[pl.ds(start, tk), :]
        vext = vext_ref[j]
        n_chunks = 2 * tq // QUERY_CHUNK
        chunk = lambda c: slice(c * QUERY_CHUNK, (c + 1) * QUERY_CHUNK)
        scores = lambda c: _dot_nt(k, qs_ref[chunk(c), :])
        ahead = [scores(c) for c in range(SCORES_AHEAD)]
        for c in range(n_chunks):
            cs = chunk(c)
            st = ahead.pop(0)
            if c + SCORES_AHEAD < n_chunks:
                ahead.append(scores(c + SCORES_AHEAD))
            if masked:
                key = start + lax.broadcasted_iota(jnp.int32, st.shape, 0)
                query = i * tq + (c * QUERY_CHUNK) % tq + lax.broadcasted_iota(jnp.int32, st.shape, 1)
                st = jnp.where(key <= query, st, NEG_BIG)
            m_old = m_ref[:, cs]
            m_new = jnp.maximum(m_old, jnp.max(st, axis=0, keepdims=True))
            alpha = jnp.exp2(m_old - m_new)
            p = jnp.exp2(st - m_new).astype(BF16)
            acc_ref[:, cs] = alpha * acc_ref[:, cs] + _dot(vext, p)
            m_ref[:, cs] = m_new

    last = (i * tq + tq - 1) // tk
    lax.fori_loop(0, last, lambda j, c: (step(j, False), c)[1], 0)
    step(last, True)

    lam = (jnp.exp(jnp.sum(lq1_ref[...] * lk1_ref[...], axis=-1, keepdims=True))
           - jnp.exp(jnp.sum(lq2_ref[...] * lk2_ref[...], axis=-1, keepdims=True)) + lambda_init)
    acc = acc_ref[...]
    o2 = acc[0:DIFF_V_DIM, :] * (1.0 / acc[DIFF_V_DIM:DIFF_V_DIM + 1, :])
    o = (o2[:, :tq] - lam * o2[:, tq:]).T
    ms = jnp.mean(o * o, axis=-1, keepdims=True)
    o = o * lax.rsqrt(ms + EPS) * gain_ref[...] * (1.0 - lambda_init)
    o_ref[...] = o.astype(BF16)


def _diffattn(dq, dk, dvt, lam_q1, lam_k1, lam_q2, lam_k2, gain, batch, seq, lambda_init, tq, tk):
    nq = seq // tq
    q_spec = pl.BlockSpec((tq, LANES), lambda b, h, i: (b * nq + i, h))
    k_spec = pl.BlockSpec((seq, LANES), lambda b, h, i: (b, h))
    vt_spec = pl.BlockSpec((DIFF_V_DIM, seq), lambda b, h, i: (h, b))
    vec = lambda a: pl.BlockSpec(a.shape, lambda b, h, i: (0, 0))
    return pl.pallas_call(
        functools.partial(_diffattn_kernel, tq=tq, tk=tk, lambda_init=lambda_init),
        grid=(batch, DIFF_HEADS, nq),
        in_specs=[q_spec, k_spec, vt_spec, vec(lam_q1), vec(lam_k1), vec(lam_q2), vec(lam_k2), vec(gain)],
        out_specs=q_spec,
        out_shape=jax.ShapeDtypeStruct(dq.shape, BF16),
        scratch_shapes=[
            pltpu.VMEM((2 * tq, LANES), BF16),
            pltpu.VMEM((seq // tk, V_EXT_ROWS, tk), BF16),
            pltpu.VMEM((1, 2 * tq), F32),
            pltpu.VMEM((V_EXT_ROWS, 2 * tq), F32),
        ],
        compiler_params=pltpu.CompilerParams(
            dimension_semantics=("parallel", "parallel", "arbitrary"), vmem_limit_bytes=VMEM_LIMIT),
        name="diffattn",
    )(dq, dk, dvt, lam_q1, lam_k1, lam_q2, lam_k2, gain)


def _route(logits):
    r = [logits[g:g + 1, :] for g in range(N_GROUPS)]
    gmax = jnp.maximum(jnp.maximum(r[0], r[1]), jnp.maximum(r[2], r[3]))
    g_idx = jnp.where(r[0] == gmax, 0, jnp.where(r[1] == gmax, 1, jnp.where(r[2] == gmax, 2, 3)))
    denom = sum(jnp.exp(rg - gmax) for rg in r)
    g_weight = 1.0 / denom
    sel = jnp.zeros((EXPERTS_PER_GROUP, logits.shape[1]), F32)
    for g in range(N_GROUPS):
        rows = logits[8 + g * EXPERTS_PER_GROUP:8 + (g + 1) * EXPERTS_PER_GROUP, :]
        sel = jnp.where(g_idx == g, rows, sel)
    eidx = lax.broadcasted_iota(jnp.int32, sel.shape, 0)
    v1 = jnp.max(sel, axis=0, keepdims=True)
    i1 = jnp.min(jnp.where(sel == v1, eidx, EXPERTS_PER_GROUP), axis=0, keepdims=True)
    sel2 = jnp.where(eidx == i1, -jnp.inf, sel)
    v2 = jnp.max(sel2, axis=0, keepdims=True)
    i2 = jnp.min(jnp.where(sel2 == v2, eidx, EXPERTS_PER_GROUP), axis=0, keepdims=True)
    e2 = jnp.exp(v2 - v1)
    w1 = g_weight / (1.0 + e2)
    w2 = g_weight * e2 / (1.0 + e2)
    return g_idx, i1, i2, w1, w2


def _outproj_kernel(ret_ref, diff_ref, x_ref, mod_ref, gain_ref, wo_ref, wr_hi_ref, wr_lo_ref, br_ref, tri_ref,
                    x1_ref, h2_ref, ri_ref, rw_ref, cnt_ref, *, tiles_per_batch):
    b = pl.program_id(0) // tiles_per_batch
    mix = _dot(ret_ref[...], wo_ref[0:RET_WIDTH, :]) + _dot(diff_ref[...], wo_ref[RET_WIDTH:, :])
    gate1 = mod_ref[pl.ds(b, 1), 2 * D_MODEL:3 * D_MODEL]
    x1 = x_ref[...] + gate1 * mix
    x1_ref[...] = x1
    shift = mod_ref[pl.ds(b, 1), 3 * D_MODEL:4 * D_MODEL]
    scale = mod_ref[pl.ds(b, 1), 4 * D_MODEL:5 * D_MODEL]
    h2 = _norm_modulate(x1, gain_ref[...], shift, scale)
    h_hi, h_lo = _split_bf16(h2)
    h2_ref[...] = h_hi
    wr_hi = wr_hi_ref[...]
    logits = _dot_nt(wr_hi, h_hi) + _dot_nt(wr_lo_ref[...], h_hi) + _dot_nt(wr_hi, h_lo) + br_ref[...]
    g_idx, i1, i2, w1, w2 = _route(logits)
    e1 = g_idx * EXPERTS_PER_GROUP + i1
    e2 = g_idx * EXPERTS_PER_GROUP + i2
    eidx = lax.broadcasted_iota(jnp.int32, (N_EXPERTS, logits.shape[1]), 0)
    hit1 = eidx == e1
    hit2 = eidx == e2
    onehot = jnp.where(hit1 | hit2, 1.0, 0.0)
    before = _dot(onehot.astype(BF16), tri_ref[...])
    r1 = jnp.sum(jnp.where(hit1, before, 0.0), axis=0, keepdims=True)
    r2 = jnp.sum(jnp.where(hit2, before, 0.0), axis=0, keepdims=True)
    zi = jnp.zeros_like(e1)
    ri_ref[...] = jnp.concatenate([e1, e2, r1.astype(jnp.int32), r2.astype(jnp.int32), zi, zi, zi, zi], axis=0)
    zf = jnp.zeros_like(w1)
    rw_ref[...] = jnp.concatenate([w1, w2, zf, zf, zf, zf, zf, zf], axis=0)
    counts = jnp.sum(onehot, axis=1, keepdims=True)
    cnt_ref[0] = jnp.broadcast_to(counts, (N_EXPERTS, LANES)).astype(jnp.int32)


def _outproj(ret_out, diff_out, x2, mod, gain, w_out, wr_hi, wr_lo, br, seq, tm):
    tokens = x2.shape[0]
    tiles_per_batch = seq // tm
    n_tiles = tokens // tm
    tri = (jnp.arange(tm)[:, None] < jnp.arange(tm)[None, :]).astype(BF16)
    tok_spec = lambda w: pl.BlockSpec((tm, w), lambda i: (i, 0))
    row_spec = pl.BlockSpec((8, tm), lambda i: (0, i))
    full = lambda a: pl.BlockSpec(a.shape, lambda i: (0,) * a.ndim)
    return pl.pallas_call(
        functools.partial(_outproj_kernel, tiles_per_batch=tiles_per_batch),
        grid=(n_tiles,),
        in_specs=[tok_spec(RET_WIDTH), tok_spec(DIFF_WIDTH), tok_spec(D_MODEL), full(mod), full(gain),
                  full(w_out), full(wr_hi), full(wr_lo), full(br), full(tri)],
        out_specs=[tok_spec(D_MODEL), tok_spec(D_MODEL), row_spec, row_spec,
                   pl.BlockSpec((1, N_EXPERTS, LANES), lambda i: (i, 0, 0))],
        out_shape=[jax.ShapeDtypeStruct((tokens, D_MODEL), F32),
                   jax.ShapeDtypeStruct((tokens, D_MODEL), BF16),
                   jax.ShapeDtypeStruct((8, tokens), jnp.int32),
                   jax.ShapeDtypeStruct((8, tokens), F32),
                   jax.ShapeDtypeStruct((n_tiles, N_EXPERTS, LANES), jnp.int32)],
        compiler_params=pltpu.CompilerParams(
            dimension_semantics=("parallel",), vmem_limit_bytes=VMEM_LIMIT),
        name="outproj",
    )(ret_out, diff_out, x2, mod, gain, w_out, wr_hi, wr_lo, br, tri)


CHUNK = 8
TMX = 256
MAX_TAIL_CHUNKS = N_EXPERTS * (TMX // CHUNK - 1)


def _local_rows(tm):
    rows = 2 * tm + N_EXPERTS * (CHUNK - 1)
    return (rows + 15) // 16 * 16


def _sorted_rows_alloc(tokens, tm):
    worst = 2 * tokens + (tokens // tm) * N_EXPERTS * (CHUNK - 1) + N_EXPERTS * (TMX - CHUNK)
    return (worst + TMX - 1) // TMX * TMX


def _dispatch_plan(cnt, tokens, tm):
    i32 = jnp.int32
    nch_max = _local_rows(tm) // CHUNK
    pad = (cnt + CHUNK - 1) // CHUNK * CHUNK
    local_end = jnp.cumsum(pad, axis=1)
    local_start = local_end - pad
    seg_rows = jnp.sum(pad, axis=0)
    seg_pad = (seg_rows + TMX - 1) // TMX * TMX
    seg_end = jnp.cumsum(seg_pad)
    seg_start = seg_end - seg_pad
    run_dst = seg_start[None, :] + jnp.cumsum(pad, axis=0) - pad
    row = CHUNK * jnp.arange(nch_max, dtype=i32)[None, :, None]
    owns = (local_start[:, None, :] <= row) & (row < local_end[:, None, :])
    chunk_dst = row[:, :, 0] + jnp.sum(jnp.where(owns, (run_dst - local_start)[:, None, :], 0), axis=-1)
    tail_n = (seg_pad - seg_rows) // CHUNK
    tail_end = jnp.cumsum(tail_n)
    tail_start = tail_end - tail_n
    k = jnp.arange(MAX_TAIL_CHUNKS, dtype=i32)[:, None]
    towns = (tail_start[None, :] <= k) & (k < tail_end[None, :])
    tail_dst = jnp.sum(jnp.where(towns, (seg_start + seg_rows)[None, :] + CHUNK * (k - tail_start[None, :]), 0),
                       axis=-1)
    m = TMX * jnp.arange(_sorted_rows_alloc(tokens, tm) // TMX, dtype=i32)
    tile_expert = jnp.minimum(jnp.sum(seg_end[None, :] <= m[:, None], axis=-1), N_EXPERTS - 1)
    return dict(
        local_start=local_start.reshape(-1).astype(i32),
        n_chunks=(local_end[:, -1] // CHUNK).astype(i32),
        chunk_dst=chunk_dst.reshape(-1).astype(i32),
        tail_dst=tail_dst.astype(i32),
        n_tail=tail_end[-1:].astype(i32),
        tile_expert=tile_expert.astype(i32),
        n_used=(seg_end[-1:] // TMX).astype(i32),
    )


def _local_slots(ri_ref, local_start_ref, tile):
    e1, e2 = ri_ref[0:1, :], ri_ref[1:2, :]
    s1, s2 = ri_ref[2:3, :], ri_ref[3:4, :]
    for e in range(N_EXPERTS):
        start = local_start_ref[tile * N_EXPERTS + e]
        s1 = s1 + jnp.where(e1 == e, start, 0)
        s2 = s2 + jnp.where(e2 == e, start, 0)
    return s1, s2


def _dispatch_kernel(local_start_ref, n_chunks_ref, chunk_dst_ref, tail_dst_ref, n_tail_ref, n_used_ref,
                     h_ref, ri_ref, xs_ref, buf_ref, zero_ref, sem_ref, tail_sem_ref, *, r_loc):
    b = pl.program_id(0)
    nb = pl.num_programs(0)
    slot = b % 2
    nch_max = r_loc // CHUNK

    def chunk_copy(tile, sl, j):
        row = pl.multiple_of(j * CHUNK, CHUNK)
        dst = pl.multiple_of(chunk_dst_ref[tile * nch_max + j], CHUNK)
        return pltpu.make_async_copy(buf_ref.at[sl, pl.ds(row, CHUNK), :], xs_ref.at[pl.ds(dst, CHUNK), :],
                                     sem_ref.at[sl])

    def drain(tile, sl):
        lax.fori_loop(0, n_chunks_ref[tile], lambda j, c: (chunk_copy(tile, sl, j).wait(), c)[1], 0)

    @pl.when(b >= 2)
    def _():
        drain(b - 2, slot)

    s1, s2 = _local_slots(ri_ref, local_start_ref, b)
    rows = lax.broadcasted_iota(jnp.int32, (r_loc, s1.shape[1]), 0)
    perm = jnp.where((rows == s1) | (rows == s2), 1.0, 0.0).astype(BF16)
    buf_ref[slot] = _dot(perm, h_ref[...])
    lax.fori_loop(0, n_chunks_ref[b], lambda j, c: (chunk_copy(b, slot, j).start(), c)[1], 0)

    @pl.when(b == nb - 1)
    def _():
        zero_ref[...] = jnp.zeros_like(zero_ref)

        def tail_copy(k):
            dst = pl.multiple_of(tail_dst_ref[k], CHUNK)
            return pltpu.make_async_copy(zero_ref.at[pl.ds(0, CHUNK), :], xs_ref.at[pl.ds(dst, CHUNK), :],
                                         tail_sem_ref.at[0])

        def unused_tile_copy(m):
            dst = pl.multiple_of(m * TMX, TMX)
            return pltpu.make_async_copy(zero_ref, xs_ref.at[pl.ds(dst, TMX), :], tail_sem_ref.at[1])

        n_alloc = xs_ref.shape[0] // TMX
        lax.fori_loop(0, n_tail_ref[0], lambda k, c: (tail_copy(k).start(), c)[1], 0)
        lax.fori_loop(n_used_ref[0], n_alloc, lambda m, c: (unused_tile_copy(m).start(), c)[1], 0)
        lax.fori_loop(0, n_tail_ref[0], lambda k, c: (tail_copy(k).wait(), c)[1], 0)
        lax.fori_loop(n_used_ref[0], n_alloc, lambda m, c: (unused_tile_copy(m).wait(), c)[1], 0)

        @pl.when(b >= 1)
        def _():
            drain(b - 1, 1 - slot)

        drain(b, slot)


def _dispatch(h2, ri, plan, tm):
    tokens = h2.shape[0]
    r_loc = _local_rows(tm)
    grid_spec = pltpu.PrefetchScalarGridSpec(
        num_scalar_prefetch=6,
        grid=(tokens // tm,),
        in_specs=[pl.BlockSpec((tm, D_MODEL), lambda i, *_: (i, 0)),
                  pl.BlockSpec((8, tm), lambda i, *_: (0, i))],
        out_specs=pl.BlockSpec(memory_space=pl.ANY),
        scratch_shapes=[pltpu.VMEM((2, r_loc, D_MODEL), F32), pltpu.VMEM((TMX, D_MODEL), F32),
                        pltpu.SemaphoreType.DMA((2,)), pltpu.SemaphoreType.DMA((2,))],
    )
    return pl.pallas_call(
        functools.partial(_dispatch_kernel, r_loc=r_loc),
        grid_spec=grid_spec,
        out_shape=jax.ShapeDtypeStruct((_sorted_rows_alloc(tokens, tm), D_MODEL), F32),
        compiler_params=pltpu.CompilerParams(
            dimension_semantics=("arbitrary",), vmem_limit_bytes=VMEM_LIMIT),
        name="dispatch",
    )(plan["local_start"], plan["n_chunks"], plan["chunk_dst"], plan["tail_dst"], plan["n_tail"], plan["n_used"],
      h2, ri)


def _experts_kernel(tile_expert_ref, n_used_ref, xs_ref, wg_ref, wu_ref, wd_ref, ys_ref, wg_bf, wu_bf, wd_bf):
    m = pl.program_id(0)

    @pl.when(m < n_used_ref[0])
    def _():
        @pl.when((m == 0) | (tile_expert_ref[m] != tile_expert_ref[jnp.maximum(m - 1, 0)]))
        def _():
            wg_bf[...] = wg_ref[0].astype(BF16)
            wu_bf[...] = wu_ref[0].astype(BF16)
            wd_bf[...] = wd_ref[0].astype(BF16)

        x = xs_ref[...].astype(BF16)
        a = _dot(x, wg_bf[...])
        u = _dot(x, wu_bf[...])
        hid = (_silu(a) * u).astype(BF16)
        ys_ref[...] = _dot(hid, wd_bf[...])

    @pl.when(m >= n_used_ref[0])
    def _():
        ys_ref[...] = jnp.zeros_like(ys_ref)


def _experts(xs, plan, wg, wu, wd):
    n_tiles = xs.shape[0] // TMX
    last_used = lambda m, n_used: jnp.minimum(m, n_used[0] - 1)
    row_spec = pl.BlockSpec((TMX, D_MODEL), lambda m, te, nu: (last_used(m, nu), 0))
    out_spec = pl.BlockSpec((TMX, D_MODEL), lambda m, te, nu: (m, 0))
    w_spec = lambda shape: pl.BlockSpec((1,) + shape, lambda m, te, nu: (te[last_used(m, nu)], 0, 0))
    grid_spec = pltpu.PrefetchScalarGridSpec(
        num_scalar_prefetch=2,
        grid=(n_tiles,),
        in_specs=[row_spec, w_spec((D_MODEL, D_EXPERT)), w_spec((D_MODEL, D_EXPERT)),
                  w_spec((D_EXPERT, D_MODEL))],
        out_specs=out_spec,
        scratch_shapes=[pltpu.VMEM((D_MODEL, D_EXPERT), BF16), pltpu.VMEM((D_MODEL, D_EXPERT), BF16),
                        pltpu.VMEM((D_EXPERT, D_MODEL), BF16)],
    )
    return pl.pallas_call(
        _experts_kernel,
        grid_spec=grid_spec,
        out_shape=jax.ShapeDtypeStruct(xs.shape, F32),
        compiler_params=pltpu.CompilerParams(
            dimension_semantics=("arbitrary",), vmem_limit_bytes=VMEM_LIMIT),
        name="experts",
    )(plan["tile_expert"], plan["n_used"], xs, wg, wu, wd)


def _combine_kernel(local_start_ref, n_chunks_ref, chunk_dst_ref, ys_ref, ri_ref, rw_ref, x1_ref, mod_ref,
                    gain_ref, o_ref, buf_ref, sem_ref, *, r_loc, tiles_per_batch):
    b = pl.program_id(0)
    nb = pl.num_programs(0)
    slot = b % 2
    nch_max = r_loc // CHUNK

    def chunk_copy(tile, sl, j):
        row = pl.multiple_of(j * CHUNK, CHUNK)
        src = pl.multiple_of(chunk_dst_ref[tile * nch_max + j], CHUNK)
        return pltpu.make_async_copy(ys_ref.at[pl.ds(src, CHUNK), :], buf_ref.at[sl, pl.ds(row, CHUNK), :],
                                     sem_ref.at[sl])

    def fetch(tile, sl):
        lax.fori_loop(0, n_chunks_ref[tile], lambda j, c: (chunk_copy(tile, sl, j).start(), c)[1], 0)

    @pl.when(b == 0)
    def _():
        buf_ref[...] = jnp.zeros_like(buf_ref)
        fetch(0, 0)

    @pl.when(b + 1 < nb)
    def _():
        fetch(b + 1, 1 - slot)

    lax.fori_loop(0, n_chunks_ref[b], lambda j, c: (chunk_copy(b, slot, j).wait(), c)[1], 0)

    s1, s2 = _local_slots(ri_ref, local_start_ref, b)
    rows = lax.broadcasted_iota(jnp.int32, (r_loc, s1.shape[1]), 0)
    hit1 = rows == s1
    hit2 = rows == s2
    w_row = jnp.sum(jnp.where(hit1, rw_ref[0:1, :], jnp.where(hit2, rw_ref[1:2, :], 0.0)), axis=1, keepdims=True)
    perm = jnp.where(hit1 | hit2, 1.0, 0.0).astype(BF16)
    yw = (buf_ref[slot] * w_row).astype(BF16)
    moe = _dot_tn(perm, yw)
    batch = b // tiles_per_batch
    gate2 = mod_ref[pl.ds(batch, 1), 5 * D_MODEL:6 * D_MODEL]
    x2 = x1_ref[...] + gate2 * moe
    ms = jnp.mean(x2 * x2, axis=-1, keepdims=True)
    o_ref[...] = x2 * lax.rsqrt(ms + EPS) * gain_ref[...]


def _combine(ys, ri, rw, x1, mod, gain, plan, seq, tm):
    tokens = x1.shape[0]
    r_loc = _local_rows(tm)
    row_spec = pl.BlockSpec((8, tm), lambda i, *_: (0, i))
    tok_spec = pl.BlockSpec((tm, D_MODEL), lambda i, *_: (i, 0))
    full = lambda a: pl.BlockSpec(a.shape, lambda i, *_: (0,) * a.ndim)
    grid_spec = pltpu.PrefetchScalarGridSpec(
        num_scalar_prefetch=3,
        grid=(tokens // tm,),
        in_specs=[pl.BlockSpec(memory_space=pl.ANY), row_spec, row_spec, tok_spec, full(mod), full(gain)],
        out_specs=tok_spec,
        scratch_shapes=[pltpu.VMEM((2, r_loc, D_MODEL), F32), pltpu.SemaphoreType.DMA((2,))],
    )
    return pl.pallas_call(
        functools.partial(_combine_kernel, r_loc=r_loc, tiles_per_batch=seq // tm),
        grid_spec=grid_spec,
        out_shape=jax.ShapeDtypeStruct((tokens, D_MODEL), F32),
        compiler_params=pltpu.CompilerParams(
            dimension_semantics=("arbitrary",), vmem_limit_bytes=VMEM_LIMIT),
        name="combine",
    )(plan["local_start"], plan["n_chunks"], plan["chunk_dst"], ys, ri, rw, x1, mod, gain)


def _rotary_tables(seq):
    half = RET_HEAD_DIM // 2
    inv_freq = 1.0 / (ROPE_BASE ** (jnp.arange(half, dtype=F32) / half))
    ang = jnp.arange(seq).astype(F32)[:, None] * inv_freq[None, :]
    cos = jnp.cos(ang)
    sin = jnp.sin(ang)
    return jnp.tile(cos, (1, 4)), jnp.concatenate([-sin, sin, -sin, sin], axis=1)


def _pick_tile(n, pref):
    t = min(n, pref)
    assert n % t == 0, (n, t)
    return t


def kernel(x, c, ada_w, ada_b, norm1_gain, norm2_gain, w_in, w_out, ret_gn_gain, lam_q1, lam_k1, lam_q2,
           lam_k2, diff_subln_gain, w_group, b_group, w_expert, b_expert, w_gate, w_up, w_down, final_gain):
    batch, seq, d = x.shape
    assert d == D_MODEL and batch <= 8 and ada_w.shape[0] == 1
    layer = 0
    lambda_init = 0.8 - 0.6 * math.exp(-0.3 * layer)
    tokens = batch * seq
    x2 = x.reshape(tokens, d)
    tm = _pick_tile(seq, 512)

    c_pad = jnp.zeros((8, d), F32).at[:batch].set(c)
    mod = _adaln(c_pad, ada_w[layer], ada_b[layer].reshape(1, -1))

    cos_t, sin_t = _rotary_tables(seq)
    n_main = 4 * RET_WIDTH + 2 * DIFF_WIDTH
    w_in_bf = w_in[layer].astype(BF16)
    rq, rk, rv, rg, dq, dk, dvt = _inproj(
        x2, mod, norm1_gain[layer].reshape(1, d), w_in_bf[:, :n_main], w_in_bf[:, n_main:].T, cos_t, sin_t,
        seq, tm)

    ret_out = _retention(rq, rk, rv, rg, ret_gn_gain[layer].reshape(1, RET_WIDTH), batch, seq,
                         _pick_tile(seq, 128))
    diff_out = _diffattn(
        dq, dk, dvt, lam_q1[layer].reshape(1, -1), lam_k1[layer].reshape(1, -1), lam_q2[layer].reshape(1, -1),
        lam_k2[layer].reshape(1, -1), diff_subln_gain[layer].reshape(1, -1), batch, seq, lambda_init,
        _pick_tile(seq, 512), _pick_tile(seq, 512))

    w_router = jnp.concatenate(
        [w_group[layer].T, jnp.zeros((8 - N_GROUPS, d), F32), w_expert[layer].reshape(d, N_EXPERTS).T], axis=0)
    b_router = jnp.concatenate(
        [b_group[layer], jnp.zeros((8 - N_GROUPS,), F32), b_expert[layer].reshape(N_EXPERTS)]).reshape(-1, 1)
    wr_hi = w_router.astype(BF16)
    wr_lo = (w_router - wr_hi.astype(F32)).astype(BF16)
    x1, h2, ri, rw, cnt = _outproj(ret_out, diff_out, x2, mod, norm2_gain[layer].reshape(1, d),
                                   w_out[layer].astype(BF16), wr_hi, wr_lo, b_router, seq, tm)

    plan = _dispatch_plan(cnt[:, :, 0], tokens, tm)
    xs = _dispatch(h2, ri, plan, tm)
    ys = _experts(xs, plan, w_gate[layer].reshape(N_EXPERTS, d, D_EXPERT),
                  w_up[layer].reshape(N_EXPERTS, d, D_EXPERT), w_down[layer].reshape(N_EXPERTS, D_EXPERT, d))
    out = _combine(ys, ri, rw, x1, mod, final_gain.reshape(1, d), plan, seq, tm)
    return out.reshape(batch, seq, d)
```

```python
import functools
import math

import jax
import jax.numpy as jnp
from jax import lax
from jax.experimental import pallas as pl
from jax.experimental.pallas import tpu as pltpu

F32 = jnp.float32
BF16 = jnp.bfloat16

D_MODEL = 1024
RET_HEAD_DIM = 64
RET_WIDTH = 512
RET_HEADS = 8
RET_PAIRS = RET_HEADS // 2
DIFF_QK_DIM = 64
DIFF_V_DIM = 128
DIFF_HEADS = 4
DIFF_WIDTH = 512
N_GROUPS = 4
EXPERTS_PER_GROUP = 8
N_EXPERTS = N_GROUPS * EXPERTS_PER_GROUP
D_EXPERT = 512
N_MOD = 6
ROPE_BASE = 10000.0
EPS = 1e-6
LANES = 128
ROUTER_ROWS = 8 + N_EXPERTS
VMEM_LIMIT = 56 * 1024 * 1024


def _dot(a, b):
    return jnp.dot(a, b, preferred_element_type=F32)


def _dot_nt(a, b):
    return lax.dot_general(a, b, (((1,), (1,)), ((), ())), preferred_element_type=F32)


def _dot_tn(a, b):
    return lax.dot_general(a, b, (((0,), (0,)), ((), ())), preferred_element_type=F32)


def _split_bf16(x):
    hi = x.astype(BF16)
    lo = (x - hi.astype(F32)).astype(BF16)
    return hi, lo


def _silu(x):
    return x / (1.0 + jnp.exp(-x))


def _adaln_kernel(c_ref, w_ref, b_ref, o_ref):
    ca = _silu(c_ref[...])
    c_hi, c_lo = _split_bf16(ca)
    w_hi, w_lo = _split_bf16(w_ref[...])
    o_ref[...] = _dot(c_hi, w_hi) + _dot(c_lo, w_hi) + _dot(c_hi, w_lo) + b_ref[...]


def _adaln(c_pad, ada_w, ada_b):
    n_out = ada_w.shape[1]
    tn = D_MODEL
    return pl.pallas_call(
        _adaln_kernel,
        grid=(n_out // tn,),
        in_specs=[
            pl.BlockSpec((8, D_MODEL), lambda j: (0, 0)),
            pl.BlockSpec((D_MODEL, tn), lambda j: (0, j)),
            pl.BlockSpec((1, tn), lambda j: (0, j)),
        ],
        out_specs=pl.BlockSpec((8, tn), lambda j: (0, j)),
        out_shape=jax.ShapeDtypeStruct((8, n_out), F32),
        compiler_params=pltpu.CompilerParams(vmem_limit_bytes=VMEM_LIMIT),
        name="adaln",
    )(c_pad, ada_w, ada_b)


def _norm_modulate(x, gain, shift, scale):
    ms = jnp.mean(x * x, axis=-1, keepdims=True)
    y = x * lax.rsqrt(ms + EPS) * gain
    return y * (1.0 + scale) + shift


def _rotary_slab(x, cos, sin_signed, lane_lo):
    swapped = jnp.where(lane_lo, pltpu.roll(x, 96, 1), pltpu.roll(x, 32, 1))
    return x * cos + swapped * sin_signed


def _inproj_kernel(x_ref, mod_ref, gain_ref, w_ref, wvt_ref, cos_ref, sin_ref,
                   rq_ref, rk_ref, rv_ref, rg_ref, dq_ref, dk_ref, dvt_ref, *, tiles_per_batch):
    b = pl.program_id(0) // tiles_per_batch
    shift = mod_ref[pl.ds(b, 1), 0:D_MODEL]
    scale = mod_ref[pl.ds(b, 1), D_MODEL:2 * D_MODEL]
    h = _norm_modulate(x_ref[...], gain_ref[...], shift, scale).astype(BF16)
    cos = cos_ref[...]
    sin = sin_ref[...]
    lane = lax.broadcasted_iota(jnp.int32, cos.shape, 1)
    lane_lo = (lane % 64) < 32

    def proj(chunk):
        return _dot(h, w_ref[:, chunk * RET_WIDTH:(chunk + 1) * RET_WIDTH])

    def rotary(acc, out_ref, post_scale):
        for s in range(RET_WIDTH // LANES):
            sl = slice(s * LANES, (s + 1) * LANES)
            out_ref[:, sl] = (_rotary_slab(acc[:, sl], cos, sin, lane_lo) * post_scale).astype(BF16)

    rotary(proj(0), rq_ref, 1.0)
    rotary(proj(1), rk_ref, RET_HEAD_DIM ** -0.5)
    rv_ref[...] = proj(2).astype(BF16)
    rg_ref[...] = _silu(proj(3)).astype(BF16)
    dq_ref[...] = (proj(4) * (DIFF_QK_DIM ** -0.5 * math.log2(math.e))).astype(BF16)
    dk_ref[...] = proj(5).astype(BF16)
    dvt_ref[...] = _dot_nt(wvt_ref[...], h).astype(BF16)


def _inproj(x2, mod, gain, w_main, w_vt, cos_t, sin_t, seq, tm):
    tokens = x2.shape[0]
    tiles_per_batch = seq // tm
    tok_spec = lambda w: pl.BlockSpec((tm, w), lambda i: (i, 0))
    tab_spec = pl.BlockSpec((tm, LANES), lambda i: (i % tiles_per_batch, 0))
    full = lambda a: pl.BlockSpec(a.shape, lambda i: (0,) * a.ndim)
    out = jax.ShapeDtypeStruct((tokens, RET_WIDTH), BF16)
    return pl.pallas_call(
        functools.partial(_inproj_kernel, tiles_per_batch=tiles_per_batch),
        grid=(tokens // tm,),
        in_specs=[tok_spec(D_MODEL), full(mod), full(gain), full(w_main), full(w_vt), tab_spec, tab_spec],
        out_specs=[tok_spec(RET_WIDTH)] * 6 + [pl.BlockSpec((DIFF_WIDTH, tm), lambda i: (0, i))],
        out_shape=[out] * 6 + [jax.ShapeDtypeStruct((DIFF_WIDTH, tokens), BF16)],
        compiler_params=pltpu.CompilerParams(
            dimension_semantics=("parallel",), vmem_limit_bytes=VMEM_LIMIT),
        name="inproj",
    )(x2, mod, gain, w_main, w_vt, cos_t, sin_t)


def _retention_kernel(q_ref, k_ref, v_ref, g_ref, dec_ref, qdec_ref, kdec_ref, rdec_ref,
                      bmask_ref, gmean_ref, gain_ref, o_ref, state_ref, *, chunk):
    @pl.when(pl.program_id(1) == 0)
    def _():
        state_ref[...] = jnp.zeros_like(state_ref)

    lane = lax.broadcasted_iota(jnp.int32, (chunk, LANES), 1)
    first_head = lane < RET_HEAD_DIM
    gmean = gmean_ref[...]
    bmask = bmask_ref[...]
    pairs = range(RET_PAIRS)
    sl = [slice(p * LANES, (p + 1) * LANES) for p in pairs]
    q = [q_ref[:, sl[p]] for p in pairs]
    k = [k_ref[:, sl[p]] for p in pairs]
    v = [v_ref[:, sl[p]] for p in pairs]
    zero = jnp.zeros_like(q[0])
    q_stack = [jnp.concatenate([jnp.where(first_head, q[p], zero), jnp.where(first_head, zero, q[p])], axis=0)
               for p in pairs]
    scores = [(_dot_nt(q_stack[p], k[p]) * dec_ref[p]).astype(BF16) for p in pairs]
    state = [state_ref[p] for p in pairs]
    cross = [_dot(q[p], state[p].astype(BF16)) * qdec_ref[:, sl[p]] for p in pairs]
    k_dec = [(k[p].astype(F32) * kdec_ref[:, sl[p]]).astype(BF16) for p in pairs]
    for p in pairs:
        state_ref[p] = state[p] * rdec_ref[p] + _dot_tn(k_dec[p], v[p]) * bmask
    intra2 = [_dot(scores[p], v[p]) for p in pairs]
    y = [jnp.where(first_head, intra2[p][:chunk], intra2[p][chunk:]) + cross[p] for p in pairs]
    y_split = [_split_bf16(y[p]) for p in pairs]
    mu = [_dot(y_split[p][0], gmean) + _dot(y_split[p][1], gmean) for p in pairs]
    d = [y[p] - mu[p] for p in pairs]
    d_split = [_split_bf16(d[p] * d[p]) for p in pairs]
    var = [_dot(d_split[p][0], gmean) + _dot(d_split[p][1], gmean) for p in pairs]
    for p in pairs:
        yn = d[p] * lax.rsqrt(var[p] + EPS) * gain_ref[:, sl[p]]
        o_ref[:, sl[p]] = (g_ref[:, sl[p]].astype(F32) * yn).astype(BF16)


def _retention_tables(chunk):
    heads = jnp.arange(RET_HEADS, dtype=F32)
    log_gamma = jnp.log(1.0 - jnp.exp2(-5.0 - heads))
    idx = jnp.arange(chunk)
    rel = (idx[:, None] - idx[None, :]).astype(F32)
    decay = jnp.where(rel[None] >= 0, jnp.exp(log_gamma[:, None, None] * jnp.maximum(rel, 0.0)[None]), 0.0)
    dec2 = decay.reshape(RET_PAIRS, 2 * chunk, chunk)
    lane_lg = jnp.repeat(log_gamma, RET_HEAD_DIM)
    qdec = jnp.exp(lane_lg[None, :] * (idx + 1).astype(F32)[:, None])
    kdec = jnp.exp(lane_lg[None, :] * (chunk - 1 - idx).astype(F32)[:, None])
    rdec = jnp.exp(lane_lg * chunk).reshape(RET_PAIRS, LANES, 1) * jnp.ones((1, 1, LANES), F32)
    blk = jnp.arange(LANES) // RET_HEAD_DIM
    bmask = (blk[:, None] == blk[None, :]).astype(F32)
    gmean = (bmask / RET_HEAD_DIM).astype(BF16)
    return dec2, qdec, kdec, rdec, bmask, gmean


def _retention(rq, rk, rv, rg, gn_gain, batch, seq, chunk):
    nc = seq // chunk
    dec2, qdec, kdec, rdec, bmask, gmean = _retention_tables(chunk)
    tok_spec = pl.BlockSpec((chunk, RET_WIDTH), lambda b, n: (b * nc + n, 0))
    full = lambda a: pl.BlockSpec(a.shape, lambda b, n: (0,) * a.ndim)
    return pl.pallas_call(
        functools.partial(_retention_kernel, chunk=chunk),
        grid=(batch, nc),
        in_specs=[tok_spec] * 4 + [full(dec2), full(qdec), full(kdec), full(rdec), full(bmask),
                                   full(gmean), full(gn_gain)],
        out_specs=tok_spec,
        out_shape=jax.ShapeDtypeStruct(rq.shape, BF16),
        scratch_shapes=[pltpu.VMEM((RET_PAIRS, LANES, LANES), F32)],
        compiler_params=pltpu.CompilerParams(
            dimension_semantics=("parallel", "arbitrary"), vmem_limit_bytes=VMEM_LIMIT),
        name="retention",
    )(rq, rk, rv, rg, dec2, qdec, kdec, rdec, bmask, gmean, gn_gain)


NEG_BIG = -1e30


V_EXT_ROWS = DIFF_V_DIM + 16
QUERY_CHUNK = 256
SCORES_AHEAD_FULL = 3
SCORES_AHEAD_DIAG = 1


def _diag_chunks(tq, tk, d):
    assert tk == 2 * QUERY_CHUNK
    per_softmax = tq // QUERY_CHUNK
    out = []
    for c in range(2 * per_softmax):
        q0 = (c % per_softmax) * QUERY_CHUNK
        if q0 + QUERY_CHUNK - 1 < d * tk:
            continue
        kind = "full" if q0 >= (d + 1) * tk else ("tri" if q0 == d * tk else "low_tri")
        out.append((c, kind))
    return out


def _accumulate(acc_ref, cs, alpha, pv):
    acc_ref[:, cs] = alpha * acc_ref[:, cs] + pv


def _diffattn_kernel(q_ref, k_ref, vt_ref, lq1_ref, lk1_ref, lq2_ref, lk2_ref, gain_ref, bias_ref, o_ref,
                     qs_ref, vext_ref, m_ref, acc_ref, *, tq, tk, lambda_init):
    i = pl.program_id(2)
    nk = vext_ref.shape[0]

    @pl.when(i == 0)
    def _():
        for j in range(nk):
            vext_ref[j, 0:DIFF_V_DIM, :] = vt_ref[:, j * tk:(j + 1) * tk]
            vext_ref[j, DIFF_V_DIM:V_EXT_ROWS, :] = jnp.ones((V_EXT_ROWS - DIFF_V_DIM, tk), BF16)

    q = q_ref[...]
    lane = lax.broadcasted_iota(jnp.int32, q.shape, 1)
    zero = jnp.zeros_like(q)
    qs_ref[0:tq, :] = jnp.where(lane < DIFF_QK_DIM, q, zero)
    qs_ref[tq:2 * tq, :] = jnp.where(lane < DIFF_QK_DIM, zero, q)
    m_ref[...] = jnp.full_like(m_ref, NEG_BIG)
    acc_ref[...] = jnp.zeros_like(acc_ref)

    def step(j, chunks, n_ahead):
        start = pl.multiple_of(j * tk, tk)
        chunk = lambda c: slice(c * QUERY_CHUNK, (c + 1) * QUERY_CHUNK)

        def scores(c, kind):
            n_keys = QUERY_CHUNK if kind == "tri" else tk
            return _dot_nt(k_ref[pl.ds(start, n_keys), :], qs_ref[chunk(c), :])

        ahead = [scores(*chunks[n]) for n in range(min(n_ahead, len(chunks)))]
        pending = None
        for n, (c, kind) in enumerate(chunks):
            cs = chunk(c)
            st = ahead.pop(0)
            if n + n_ahead < len(chunks):
                ahead.append(scores(*chunks[n + n_ahead]))
            if kind == "tri":
                st = st + bias_ref[QUERY_CHUNK:, :]
            elif kind == "low_tri":
                st = st + bias_ref[...]
            m_old = m_ref[:, cs]
            m_new = jnp.maximum(m_old, jnp.max(st, axis=0, keepdims=True))
            alpha = jnp.exp2(m_old - m_new)
            p = jnp.exp2(st - m_new).astype(BF16)
            m_ref[:, cs] = m_new
            pv = _dot(vext_ref[j, :, 0:st.shape[0]], p)
            if pending is not None:
                pending()
            pending = functools.partial(_accumulate, acc_ref, cs, alpha, pv)
        pending()

    tiles_per_q = tq // tk
    all_chunks = [(c, "full") for c in range(2 * tq // QUERY_CHUNK)]
    lax.fori_loop(0, i * tiles_per_q, lambda j, c: (step(j, all_chunks, SCORES_AHEAD_FULL), c)[1], 0)
    for d in range(tiles_per_q):
        step(i * tiles_per_q + d, _diag_chunks(tq, tk, d), SCORES_AHEAD_DIAG)

    lam = (jnp.exp(jnp.sum(lq1_ref[...] * lk1_ref[...], axis=-1, keepdims=True))
           - jnp.exp(jnp.sum(lq2_ref[...] * lk2_ref[...], axis=-1, keepdims=True)) + lambda_init)
    acc = acc_ref[...]
    o2 = acc[0:DIFF_V_DIM, :] * (1.0 / acc[DIFF_V_DIM:DIFF_V_DIM + 1, :])
    o = (o2[:, :tq] - lam * o2[:, tq:]).T
    ms = jnp.mean(o * o, axis=-1, keepdims=True)
    o = o * lax.rsqrt(ms + EPS) * gain_ref[...] * (1.0 - lambda_init)
    o_ref[...] = o.astype(BF16)


def _diffattn(dq, dk, dvt, lam_q1, lam_k1, lam_q2, lam_k2, gain, batch, seq, lambda_init, tq, tk):
    nq = seq // tq
    q_spec = pl.BlockSpec((tq, LANES), lambda b, h, i: (b * nq + i, h))
    k_spec = pl.BlockSpec((seq, LANES), lambda b, h, i: (b, h))
    vt_spec = pl.BlockSpec((DIFF_V_DIM, seq), lambda b, h, i: (h, b))
    vec = lambda a: pl.BlockSpec(a.shape, lambda b, h, i: (0, 0))
    key = jnp.arange(tk)[:, None]
    query = QUERY_CHUNK + jnp.arange(QUERY_CHUNK)[None, :]
    bias = jnp.where(key <= query, 0.0, NEG_BIG).astype(F32)
    return pl.pallas_call(
        functools.partial(_diffattn_kernel, tq=tq, tk=tk, lambda_init=lambda_init),
        grid=(batch, DIFF_HEADS, nq),
        in_specs=[q_spec, k_spec, vt_spec, vec(lam_q1), vec(lam_k1), vec(lam_q2), vec(lam_k2), vec(gain),
                  vec(bias)],
        out_specs=q_spec,
        out_shape=jax.ShapeDtypeStruct(dq.shape, BF16),
        scratch_shapes=[
            pltpu.VMEM((2 * tq, LANES), BF16),
            pltpu.VMEM((seq // tk, V_EXT_ROWS, tk), BF16),
            pltpu.VMEM((1, 2 * tq), F32),
            pltpu.VMEM((V_EXT_ROWS, 2 * tq), F32),
        ],
        compiler_params=pltpu.CompilerParams(
            dimension_semantics=("parallel", "parallel", "arbitrary"), vmem_limit_bytes=VMEM_LIMIT),
        name="diffattn",
    )(dq, dk, dvt, lam_q1, lam_k1, lam_q2, lam_k2, gain, bias)


def _route(logits):
    r = [logits[g:g + 1, :] for g in range(N_GROUPS)]
    gmax = jnp.maximum(jnp.maximum(r[0], r[1]), jnp.maximum(r[2], r[3]))
    g_idx = jnp.where(r[0] == gmax, 0, jnp.where(r[1] == gmax, 1, jnp.where(r[2] == gmax, 2, 3)))
    denom = sum(jnp.exp(rg - gmax) for rg in r)
    g_weight = 1.0 / denom
    sel = jnp.zeros((EXPERTS_PER_GROUP, logits.shape[1]), F32)
    for g in range(N_GROUPS):
        rows = logits[8 + g * EXPERTS_PER_GROUP:8 + (g + 1) * EXPERTS_PER_GROUP, :]
        sel = jnp.where(g_idx == g, rows, sel)
    eidx = lax.broadcasted_iota(jnp.int32, sel.shape, 0)
    v1 = jnp.max(sel, axis=0, keepdims=True)
    i1 = jnp.min(jnp.where(sel == v1, eidx, EXPERTS_PER_GROUP), axis=0, keepdims=True)
    sel2 = jnp.where(eidx == i1, -jnp.inf, sel)
    v2 = jnp.max(sel2, axis=0, keepdims=True)
    i2 = jnp.min(jnp.where(sel2 == v2, eidx, EXPERTS_PER_GROUP), axis=0, keepdims=True)
    e2 = jnp.exp(v2 - v1)
    w1 = g_weight / (1.0 + e2)
    w2 = g_weight * e2 / (1.0 + e2)
    return g_idx, i1, i2, w1, w2


def _outproj_kernel(ret_ref, diff_ref, x_ref, mod_ref, gain_ref, wo_ref, wr_hi_ref, wr_lo_ref, br_ref, tri_ref,
                    x1_ref, h2_ref, ri_ref, rw_ref, cnt_ref, *, tiles_per_batch):
    b = pl.program_id(0) // tiles_per_batch
    mix = _dot(ret_ref[...], wo_ref[0:RET_WIDTH, :]) + _dot(diff_ref[...], wo_ref[RET_WIDTH:, :])
    gate1 = mod_ref[pl.ds(b, 1), 2 * D_MODEL:3 * D_MODEL]
    x1 = x_ref[...] + gate1 * mix
    x1_ref[...] = x1
    shift = mod_ref[pl.ds(b, 1), 3 * D_MODEL:4 * D_MODEL]
    scale = mod_ref[pl.ds(b, 1), 4 * D_MODEL:5 * D_MODEL]
    h2 = _norm_modulate(x1, gain_ref[...], shift, scale)
    h_hi, h_lo = _split_bf16(h2)
    h2_ref[...] = h_hi
    wr_hi = wr_hi_ref[...]
    logits = _dot_nt(wr_hi, h_hi) + _dot_nt(wr_lo_ref[...], h_hi) + _dot_nt(wr_hi, h_lo) + br_ref[...]
    g_idx, i1, i2, w1, w2 = _route(logits)
    e1 = g_idx * EXPERTS_PER_GROUP + i1
    e2 = g_idx * EXPERTS_PER_GROUP + i2
    eidx = lax.broadcasted_iota(jnp.int32, (N_EXPERTS, logits.shape[1]), 0)
    hit1 = eidx == e1
    hit2 = eidx == e2
    onehot = jnp.where(hit1 | hit2, 1.0, 0.0)
    before = _dot(onehot.astype(BF16), tri_ref[...])
    r1 = jnp.sum(jnp.where(hit1, before, 0.0), axis=0, keepdims=True)
    r2 = jnp.sum(jnp.where(hit2, before, 0.0), axis=0, keepdims=True)
    zi = jnp.zeros_like(e1)
    ri_ref[...] = jnp.concatenate([e1, e2, r1.astype(jnp.int32), r2.astype(jnp.int32), zi, zi, zi, zi], axis=0)
    zf = jnp.zeros_like(w1)
    rw_ref[...] = jnp.concatenate([w1, w2, zf, zf, zf, zf, zf, zf], axis=0)
    counts = jnp.sum(onehot, axis=1, keepdims=True)
    cnt_ref[0] = jnp.broadcast_to(counts, (N_EXPERTS, LANES)).astype(jnp.int32)


def _outproj(ret_out, diff_out, x2, mod, gain, w_out, wr_hi, wr_lo, br, seq, tm):
    tokens = x2.shape[0]
    tiles_per_batch = seq // tm
    n_tiles = tokens // tm
    tri = (jnp.arange(tm)[:, None] < jnp.arange(tm)[None, :]).astype(BF16)
    tok_spec = lambda w: pl.BlockSpec((tm, w), lambda i: (i, 0))
    row_spec = pl.BlockSpec((8, tm), lambda i: (0, i))
    full = lambda a: pl.BlockSpec(a.shape, lambda i: (0,) * a.ndim)
    return pl.pallas_call(
        functools.partial(_outproj_kernel, tiles_per_batch=tiles_per_batch),
        grid=(n_tiles,),
        in_specs=[tok_spec(RET_WIDTH), tok_spec(DIFF_WIDTH), tok_spec(D_MODEL), full(mod), full(gain),
                  full(w_out), full(wr_hi), full(wr_lo), full(br), full(tri)],
        out_specs=[tok_spec(D_MODEL), tok_spec(D_MODEL), row_spec, row_spec,
                   pl.BlockSpec((1, N_EXPERTS, LANES), lambda i: (i, 0, 0))],
        out_shape=[jax.ShapeDtypeStruct((tokens, D_MODEL), F32),
                   jax.ShapeDtypeStruct((tokens, D_MODEL), BF16),
                   jax.ShapeDtypeStruct((8, tokens), jnp.int32),
                   jax.ShapeDtypeStruct((8, tokens), F32),
                   jax.ShapeDtypeStruct((n_tiles, N_EXPERTS, LANES), jnp.int32)],
        compiler_params=pltpu.CompilerParams(
            dimension_semantics=("parallel",), vmem_limit_bytes=VMEM_LIMIT),
        name="outproj",
    )(ret_out, diff_out, x2, mod, gain, w_out, wr_hi, wr_lo, br, tri)


CHUNK = 8
TMX = 256
MAX_TAIL_CHUNKS = N_EXPERTS * (TMX // CHUNK - 1)


def _local_rows(tm):
    rows = 2 * tm + N_EXPERTS * (CHUNK - 1)
    return (rows + 15) // 16 * 16


def _sorted_rows_alloc(tokens, tm):
    worst = 2 * tokens + (tokens // tm) * N_EXPERTS * (CHUNK - 1) + N_EXPERTS * (TMX - CHUNK)
    return (worst + TMX - 1) // TMX * TMX


def _dispatch_plan(cnt, tokens, tm):
    i32 = jnp.int32
    nch_max = _local_rows(tm) // CHUNK
    pad = (cnt + CHUNK - 1) // CHUNK * CHUNK
    local_end = jnp.cumsum(pad, axis=1)
    local_start = local_end - pad
    seg_rows = jnp.sum(pad, axis=0)
    seg_pad = (seg_rows + TMX - 1) // TMX * TMX
    seg_end = jnp.cumsum(seg_pad)
    seg_start = seg_end - seg_pad
    run_dst = seg_start[None, :] + jnp.cumsum(pad, axis=0) - pad
    row = CHUNK * jnp.arange(nch_max, dtype=i32)[None, :, None]
    owns = (local_start[:, None, :] <= row) & (row < local_end[:, None, :])
    chunk_dst = row[:, :, 0] + jnp.sum(jnp.where(owns, (run_dst - local_start)[:, None, :], 0), axis=-1)
    tail_n = (seg_pad - seg_rows) // CHUNK
    tail_end = jnp.cumsum(tail_n)
    tail_start = tail_end - tail_n
    k = jnp.arange(MAX_TAIL_CHUNKS, dtype=i32)[:, None]
    towns = (tail_start[None, :] <= k) & (k < tail_end[None, :])
    tail_dst = jnp.sum(jnp.where(towns, (seg_start + seg_rows)[None, :] + CHUNK * (k - tail_start[None, :]), 0),
                       axis=-1)
    m = TMX * jnp.arange(_sorted_rows_alloc(tokens, tm) // TMX, dtype=i32)
    tile_expert = jnp.minimum(jnp.sum(seg_end[None, :] <= m[:, None], axis=-1), N_EXPERTS - 1)
    towns = (seg_start[None, :] <= m[:, None]) & (m[:, None] < seg_end[None, :])
    used = seg_pad > 0
    parity = (jnp.cumsum(used) - used) % 2
    eids = jnp.arange(N_EXPERTS, dtype=i32)
    later_used = (eids[None, :] > eids[:, None]) & used[None, :]
    next_used = jnp.min(jnp.where(later_used, eids[None, :], N_EXPERTS), axis=1)
    next_used = jnp.where(next_used == N_EXPERTS, -1, next_used)
    pick = lambda per_expert: jnp.sum(jnp.where(towns, per_expert[None, :], 0), axis=-1)
    tile_first = jnp.sum(jnp.where(towns & (seg_start[None, :] == m[:, None]), 1, 0), axis=-1)
    tile_next = jnp.where(jnp.any(towns, axis=-1), pick(next_used), -1)
    return dict(
        tile_first=tile_first.astype(i32),
        tile_slot=pick(parity).astype(i32),
        tile_next=tile_next.astype(i32),
        local_start=local_start.reshape(-1).astype(i32),
        n_chunks=(local_end[:, -1] // CHUNK).astype(i32),
        chunk_dst=chunk_dst.reshape(-1).astype(i32),
        tail_dst=tail_dst.astype(i32),
        n_tail=tail_end[-1:].astype(i32),
        tile_expert=tile_expert.astype(i32),
        n_used=(seg_end[-1:] // TMX).astype(i32),
    )


def _local_slots(ri_ref, local_start_ref, tile):
    e1, e2 = ri_ref[0:1, :], ri_ref[1:2, :]
    s1, s2 = ri_ref[2:3, :], ri_ref[3:4, :]
    for e in range(N_EXPERTS):
        start = local_start_ref[tile * N_EXPERTS + e]
        s1 = s1 + jnp.where(e1 == e, start, 0)
        s2 = s2 + jnp.where(e2 == e, start, 0)
    return s1, s2


def _dispatch_kernel(local_start_ref, n_chunks_ref, chunk_dst_ref, tail_dst_ref, n_tail_ref, n_used_ref,
                     h_ref, ri_ref, xs_ref, buf_ref, zero_ref, sem_ref, tail_sem_ref, *, r_loc):
    b = pl.program_id(0)
    nb = pl.num_programs(0)
    slot = b % 2
    nch_max = r_loc // CHUNK

    def chunk_copy(tile, sl, j):
        row = pl.multiple_of(j * CHUNK, CHUNK)
        dst = pl.multiple_of(chunk_dst_ref[tile * nch_max + j], CHUNK)
        return pltpu.make_async_copy(buf_ref.at[sl, pl.ds(row, CHUNK), :], xs_ref.at[pl.ds(dst, CHUNK), :],
                                     sem_ref.at[sl])

    def drain(tile, sl):
        lax.fori_loop(0, n_chunks_ref[tile], lambda j, c: (chunk_copy(tile, sl, j).wait(), c)[1], 0)

    @pl.when(b >= 2)
    def _():
        drain(b - 2, slot)

    s1, s2 = _local_slots(ri_ref, local_start_ref, b)
    rows = lax.broadcasted_iota(jnp.int32, (r_loc, s1.shape[1]), 0)
    perm = jnp.where((rows == s1) | (rows == s2), 1.0, 0.0).astype(BF16)
    buf_ref[slot] = _dot(perm, h_ref[...])
    lax.fori_loop(0, n_chunks_ref[b], lambda j, c: (chunk_copy(b, slot, j).start(), c)[1], 0)

    @pl.when(b == nb - 1)
    def _():
        zero_ref[...] = jnp.zeros_like(zero_ref)

        def tail_copy(k):
            dst = pl.multiple_of(tail_dst_ref[k], CHUNK)
            return pltpu.make_async_copy(zero_ref.at[pl.ds(0, CHUNK), :], xs_ref.at[pl.ds(dst, CHUNK), :],
                                         tail_sem_ref.at[0])

        def unused_tile_copy(m):
            dst = pl.multiple_of(m * TMX, TMX)
            return pltpu.make_async_copy(zero_ref, xs_ref.at[pl.ds(dst, TMX), :], tail_sem_ref.at[1])

        n_alloc = xs_ref.shape[0] // TMX
        lax.fori_loop(0, n_tail_ref[0], lambda k, c: (tail_copy(k).start(), c)[1], 0)
        lax.fori_loop(n_used_ref[0], n_alloc, lambda m, c: (unused_tile_copy(m).start(), c)[1], 0)
        lax.fori_loop(0, n_tail_ref[0], lambda k, c: (tail_copy(k).wait(), c)[1], 0)
        lax.fori_loop(n_used_ref[0], n_alloc, lambda m, c: (unused_tile_copy(m).wait(), c)[1], 0)

        @pl.when(b >= 1)
        def _():
            drain(b - 1, 1 - slot)

        drain(b, slot)


def _dispatch(h2, ri, plan, tm):
    tokens = h2.shape[0]
    r_loc = _local_rows(tm)
    grid_spec = pltpu.PrefetchScalarGridSpec(
        num_scalar_prefetch=6,
        grid=(tokens // tm,),
        in_specs=[pl.BlockSpec((tm, D_MODEL), lambda i, *_: (i, 0)),
                  pl.BlockSpec((8, tm), lambda i, *_: (0, i))],
        out_specs=pl.BlockSpec(memory_space=pl.ANY),
        scratch_shapes=[pltpu.VMEM((2, r_loc, D_MODEL), F32), pltpu.VMEM((TMX, D_MODEL), F32),
                        pltpu.SemaphoreType.DMA((2,)), pltpu.SemaphoreType.DMA((2,))],
    )
    return pl.pallas_call(
        functools.partial(_dispatch_kernel, r_loc=r_loc),
        grid_spec=grid_spec,
        out_shape=jax.ShapeDtypeStruct((_sorted_rows_alloc(tokens, tm), D_MODEL), F32),
        compiler_params=pltpu.CompilerParams(
            dimension_semantics=("arbitrary",), vmem_limit_bytes=VMEM_LIMIT),
        name="dispatch",
    )(plan["local_start"], plan["n_chunks"], plan["chunk_dst"], plan["tail_dst"], plan["n_tail"], plan["n_used"],
      h2, ri)


def _experts_kernel(tile_expert_ref, n_used_ref, first_ref, slot_ref, next_ref, xs_ref, wg_hbm, wu_hbm, wd_hbm,
                    ys_ref, wg_st, wu_st, wd_st, wg_bf, wu_bf, wd_bf, sem_ref):
    m = pl.program_id(0)

    def weight_copies(e, s):
        return [pltpu.make_async_copy(src.at[e], dst.at[s], sem_ref.at[s, n])
                for n, (src, dst) in enumerate([(wg_hbm, wg_st), (wu_hbm, wu_st), (wd_hbm, wd_st)])]

    @pl.when(m < n_used_ref[0])
    def _():
        @pl.when(first_ref[m] == 1)
        def _():
            s = slot_ref[m]

            @pl.when(m == 0)
            def _():
                for cp in weight_copies(tile_expert_ref[0], 0):
                    cp.start()

            for cp in weight_copies(tile_expert_ref[m], s):
                cp.wait()

            @pl.when(next_ref[m] >= 0)
            def _():
                for cp in weight_copies(next_ref[m], 1 - s):
                    cp.start()

            wg_bf[...] = wg_st[s].astype(BF16)
            wu_bf[...] = wu_st[s].astype(BF16)
            wd_bf[...] = wd_st[s].astype(BF16)

        x = xs_ref[...].astype(BF16)
        a = _dot(x, wg_bf[...])
        u = _dot(x, wu_bf[...])
        hid = (_silu(a) * u).astype(BF16)
        ys_ref[...] = _dot(hid, wd_bf[...])

    @pl.when(m >= n_used_ref[0])
    def _():
        ys_ref[...] = jnp.zeros_like(ys_ref)


def _experts(xs, plan, wg, wu, wd):
    n_tiles = xs.shape[0] // TMX
    last_used = lambda m, n_used: jnp.minimum(m, n_used[0] - 1)
    row_spec = pl.BlockSpec((TMX, D_MODEL), lambda m, te, nu, *_: (last_used(m, nu), 0))
    out_spec = pl.BlockSpec((TMX, D_MODEL), lambda m, *_: (m, 0))
    hbm = pl.BlockSpec(memory_space=pl.ANY)
    up_shape, down_shape = (D_MODEL, D_EXPERT), (D_EXPERT, D_MODEL)
    grid_spec = pltpu.PrefetchScalarGridSpec(
        num_scalar_prefetch=5,
        grid=(n_tiles,),
        in_specs=[row_spec, hbm, hbm, hbm],
        out_specs=out_spec,
        scratch_shapes=[pltpu.VMEM((2,) + up_shape, F32), pltpu.VMEM((2,) + up_shape, F32),
                        pltpu.VMEM((2,) + down_shape, F32),
                        pltpu.VMEM(up_shape, BF16), pltpu.VMEM(up_shape, BF16), pltpu.VMEM(down_shape, BF16),
                        pltpu.SemaphoreType.DMA((2, 3))],
    )
    return pl.pallas_call(
        _experts_kernel,
        grid_spec=grid_spec,
        out_shape=jax.ShapeDtypeStruct(xs.shape, F32),
        compiler_params=pltpu.CompilerParams(
            dimension_semantics=("arbitrary",), vmem_limit_bytes=VMEM_LIMIT),
        name="experts",
    )(plan["tile_expert"], plan["n_used"], plan["tile_first"], plan["tile_slot"], plan["tile_next"], xs, wg, wu, wd)


def _combine_kernel(local_start_ref, n_chunks_ref, chunk_dst_ref, ys_ref, ri_ref, rw_ref, x1_ref, mod_ref,
                    gain_ref, o_ref, buf_ref, sem_ref, *, r_loc, tiles_per_batch):
    b = pl.program_id(0)
    nb = pl.num_programs(0)
    slot = b % 2
    nch_max = r_loc // CHUNK

    def chunk_copy(tile, sl, j):
        row = pl.multiple_of(j * CHUNK, CHUNK)
        src = pl.multiple_of(chunk_dst_ref[tile * nch_max + j], CHUNK)
        return pltpu.make_async_copy(ys_ref.at[pl.ds(src, CHUNK), :], buf_ref.at[sl, pl.ds(row, CHUNK), :],
                                     sem_ref.at[sl])

    def fetch(tile, sl):
        lax.fori_loop(0, n_chunks_ref[tile], lambda j, c: (chunk_copy(tile, sl, j).start(), c)[1], 0)

    @pl.when(b == 0)
    def _():
        buf_ref[...] = jnp.zeros_like(buf_ref)
        fetch(0, 0)

    @pl.when(b + 1 < nb)
    def _():
        fetch(b + 1, 1 - slot)

    lax.fori_loop(0, n_chunks_ref[b], lambda j, c: (chunk_copy(b, slot, j).wait(), c)[1], 0)

    s1, s2 = _local_slots(ri_ref, local_start_ref, b)
    rows = lax.broadcasted_iota(jnp.int32, (r_loc, s1.shape[1]), 0)
    hit1 = rows == s1
    hit2 = rows == s2
    w_row = jnp.sum(jnp.where(hit1, rw_ref[0:1, :], jnp.where(hit2, rw_ref[1:2, :], 0.0)), axis=1, keepdims=True)
    perm = jnp.where(hit1 | hit2, 1.0, 0.0).astype(BF16)
    yw = (buf_ref[slot] * w_row).astype(BF16)
    moe = _dot_tn(perm, yw)
    batch = b // tiles_per_batch
    gate2 = mod_ref[pl.ds(batch, 1), 5 * D_MODEL:6 * D_MODEL]
    x2 = x1_ref[...] + gate2 * moe
    ms = jnp.mean(x2 * x2, axis=-1, keepdims=True)
    o_ref[...] = x2 * lax.rsqrt(ms + EPS) * gain_ref[...]


def _combine(ys, ri, rw, x1, mod, gain, plan, seq, tm):
    tokens = x1.shape[0]
    r_loc = _local_rows(tm)
    row_spec = pl.BlockSpec((8, tm), lambda i, *_: (0, i))
    tok_spec = pl.BlockSpec((tm, D_MODEL), lambda i, *_: (i, 0))
    full = lambda a: pl.BlockSpec(a.shape, lambda i, *_: (0,) * a.ndim)
    grid_spec = pltpu.PrefetchScalarGridSpec(
        num_scalar_prefetch=3,
        grid=(tokens // tm,),
        in_specs=[pl.BlockSpec(memory_space=pl.ANY), row_spec, row_spec, tok_spec, full(mod), full(gain)],
        out_specs=tok_spec,
        scratch_shapes=[pltpu.VMEM((2, r_loc, D_MODEL), F32), pltpu.SemaphoreType.DMA((2,))],
    )
    return pl.pallas_call(
        functools.partial(_combine_kernel, r_loc=r_loc, tiles_per_batch=seq // tm),
        grid_spec=grid_spec,
        out_shape=jax.ShapeDtypeStruct((tokens, D_MODEL), F32),
        compiler_params=pltpu.CompilerParams(
            dimension_semantics=("arbitrary",), vmem_limit_bytes=VMEM_LIMIT),
        name="combine",
    )(plan["local_start"], plan["n_chunks"], plan["chunk_dst"], ys, ri, rw, x1, mod, gain)


def _rotary_tables(seq):
    half = RET_HEAD_DIM // 2
    inv_freq = 1.0 / (ROPE_BASE ** (jnp.arange(half, dtype=F32) / half))
    ang = jnp.arange(seq).astype(F32)[:, None] * inv_freq[None, :]
    cos = jnp.cos(ang)
    sin = jnp.sin(ang)
    return jnp.tile(cos, (1, 4)), jnp.concatenate([-sin, sin, -sin, sin], axis=1)


def _pick_tile(n, pref):
    t = min(n, pref)
    assert n % t == 0, (n, t)
    return t


def kernel(x, c, ada_w, ada_b, norm1_gain, norm2_gain, w_in, w_out, ret_gn_gain, lam_q1, lam_k1, lam_q2,
           lam_k2, diff_subln_gain, w_group, b_group, w_expert, b_expert, w_gate, w_up, w_down, final_gain):
    batch, seq, d = x.shape
    assert d == D_MODEL and batch <= 8 and ada_w.shape[0] == 1
    layer = 0
    lambda_init = 0.8 - 0.6 * math.exp(-0.3 * layer)
    tokens = batch * seq
    x2 = x.reshape(tokens, d)
    tm = _pick_tile(seq, 512)

    c_pad = jnp.zeros((8, d), F32).at[:batch].set(c)
    mod = _adaln(c_pad, ada_w[layer], ada_b[layer].reshape(1, -1))

    cos_t, sin_t = _rotary_tables(seq)
    n_main = 4 * RET_WIDTH + 2 * DIFF_WIDTH
    w_in_bf = w_in[layer].astype(BF16)
    rq, rk, rv, rg, dq, dk, dvt = _inproj(
        x2, mod, norm1_gain[layer].reshape(1, d), w_in_bf[:, :n_main], w_in_bf[:, n_main:].T, cos_t, sin_t,
        seq, tm)

    ret_out = _retention(rq, rk, rv, rg, ret_gn_gain[layer].reshape(1, RET_WIDTH), batch, seq,
                         _pick_tile(seq, 128))
    diff_out = _diffattn(
        dq, dk, dvt, lam_q1[layer].reshape(1, -1), lam_k1[layer].reshape(1, -1), lam_q2[layer].reshape(1, -1),
        lam_k2[layer].reshape(1, -1), diff_subln_gain[layer].reshape(1, -1), batch, seq, lambda_init,
        _pick_tile(seq, 1024), 2 * QUERY_CHUNK)

    w_router = jnp.concatenate(
        [w_group[layer].T, jnp.zeros((8 - N_GROUPS, d), F32), w_expert[layer].reshape(d, N_EXPERTS).T], axis=0)
    b_router = jnp.concatenate(
        [b_group[layer], jnp.zeros((8 - N_GROUPS,), F32), b_expert[layer].reshape(N_EXPERTS)]).reshape(-1, 1)
    wr_hi = w_router.astype(BF16)
    wr_lo = (w_router - wr_hi.astype(F32)).astype(BF16)
    x1, h2, ri, rw, cnt = _outproj(ret_out, diff_out, x2, mod, norm2_gain[layer].reshape(1, d),
                                   w_out[layer].astype(BF16), wr_hi, wr_lo, b_router, seq, tm)

    plan = _dispatch_plan(cnt[:, :, 0], tokens, tm)
    xs = _dispatch(h2, ri, plan, tm)
    ys = _experts(xs, plan, w_gate[layer].reshape(N_EXPERTS, d, D_EXPERT),
                  w_up[layer].reshape(N_EXPERTS, d, D_EXPERT), w_down[layer].reshape(N_EXPERTS, D_EXPERT, d))
    out = _combine(ys, ri, rw, x1, mod, final_gain.reshape(1, d), plan, seq, tm)
    return out.reshape(batch, seq, d)
```

```python
import functools
import math

import jax
import jax.numpy as jnp
from jax import lax
from jax.experimental import pallas as pl
from jax.experimental.pallas import tpu as pltpu

F32 = jnp.float32
BF16 = jnp.bfloat16

D_MODEL = 1024
RET_HEAD_DIM = 64
RET_WIDTH = 512
RET_HEADS = 8
RET_PAIRS = RET_HEADS // 2
DIFF_QK_DIM = 64
DIFF_V_DIM = 128
DIFF_HEADS = 4
DIFF_WIDTH = 512
N_GROUPS = 4
EXPERTS_PER_GROUP = 8
N_EXPERTS = N_GROUPS * EXPERTS_PER_GROUP
D_EXPERT = 512
N_MOD = 6
ROPE_BASE = 10000.0
EPS = 1e-6
LANES = 128
ROUTER_ROWS = 8 + N_EXPERTS
VMEM_LIMIT = 56 * 1024 * 1024


def _dot(a, b):
    return jnp.dot(a, b, preferred_element_type=F32)


def _dot_nt(a, b):
    return lax.dot_general(a, b, (((1,), (1,)), ((), ())), preferred_element_type=F32)


def _dot_tn(a, b):
    return lax.dot_general(a, b, (((0,), (0,)), ((), ())), preferred_element_type=F32)


def _split_bf16(x):
    hi = x.astype(BF16)
    lo = (x - hi.astype(F32)).astype(BF16)
    return hi, lo


def _silu(x):
    return x / (1.0 + jnp.exp(-x))


def _adaln_kernel(c_ref, w_ref, b_ref, o_ref):
    ca = _silu(c_ref[...])
    c_hi, c_lo = _split_bf16(ca)
    w_hi, w_lo = _split_bf16(w_ref[...])
    o_ref[...] = _dot(c_hi, w_hi) + _dot(c_lo, w_hi) + _dot(c_hi, w_lo) + b_ref[...]


def _adaln(c_pad, ada_w, ada_b):
    n_out = ada_w.shape[1]
    tn = D_MODEL
    return pl.pallas_call(
        _adaln_kernel,
        grid=(n_out // tn,),
        in_specs=[
            pl.BlockSpec((8, D_MODEL), lambda j: (0, 0)),
            pl.BlockSpec((D_MODEL, tn), lambda j: (0, j)),
            pl.BlockSpec((1, tn), lambda j: (0, j)),
        ],
        out_specs=pl.BlockSpec((8, tn), lambda j: (0, j)),
        out_shape=jax.ShapeDtypeStruct((8, n_out), F32),
        compiler_params=pltpu.CompilerParams(vmem_limit_bytes=VMEM_LIMIT),
        name="adaln",
    )(c_pad, ada_w, ada_b)


def _norm_modulate(x, gain, shift, scale):
    ms = jnp.mean(x * x, axis=-1, keepdims=True)
    y = x * lax.rsqrt(ms + EPS) * gain
    return y * (1.0 + scale) + shift


def _rotary_slab(x, cos, sin_signed, lane_lo):
    swapped = jnp.where(lane_lo, pltpu.roll(x, 96, 1), pltpu.roll(x, 32, 1))
    return x * cos + swapped * sin_signed


def _inproj_kernel(x_ref, mod_ref, gain_ref, w_ref, wvt_ref, cos_ref, sin_ref,
                   rq_ref, rk_ref, rv_ref, rg_ref, dq_ref, dk_ref, dvt_ref, *, tiles_per_batch):
    b = pl.program_id(0) // tiles_per_batch
    shift = mod_ref[pl.ds(b, 1), 0:D_MODEL]
    scale = mod_ref[pl.ds(b, 1), D_MODEL:2 * D_MODEL]
    h = _norm_modulate(x_ref[...], gain_ref[...], shift, scale).astype(BF16)
    cos = cos_ref[...]
    sin = sin_ref[...]
    lane = lax.broadcasted_iota(jnp.int32, cos.shape, 1)
    lane_lo = (lane % 64) < 32

    def proj(chunk):
        return _dot(h, w_ref[:, chunk * RET_WIDTH:(chunk + 1) * RET_WIDTH])

    def rotary(acc, out_ref, post_scale):
        for s in range(RET_WIDTH // LANES):
            sl = slice(s * LANES, (s + 1) * LANES)
            out_ref[:, sl] = (_rotary_slab(acc[:, sl], cos, sin, lane_lo) * post_scale).astype(BF16)

    rotary(proj(0), rq_ref, 1.0)
    rotary(proj(1), rk_ref, RET_HEAD_DIM ** -0.5)
    rv_ref[...] = proj(2).astype(BF16)
    rg_ref[...] = _silu(proj(3)).astype(BF16)
    dq_ref[...] = (proj(4) * (DIFF_QK_DIM ** -0.5 * math.log2(math.e))).astype(BF16)
    dk_ref[...] = proj(5).astype(BF16)
    dvt_ref[...] = _dot_nt(wvt_ref[...], h).astype(BF16)


def _inproj(x2, mod, gain, w_main, w_vt, cos_t, sin_t, seq, tm):
    tokens = x2.shape[0]
    tiles_per_batch = seq // tm
    tok_spec = lambda w: pl.BlockSpec((tm, w), lambda i: (i, 0))
    tab_spec = pl.BlockSpec((tm, LANES), lambda i: (i % tiles_per_batch, 0))
    full = lambda a: pl.BlockSpec(a.shape, lambda i: (0,) * a.ndim)
    out = jax.ShapeDtypeStruct((tokens, RET_WIDTH), BF16)
    return pl.pallas_call(
        functools.partial(_inproj_kernel, tiles_per_batch=tiles_per_batch),
        grid=(tokens // tm,),
        in_specs=[tok_spec(D_MODEL), full(mod), full(gain), full(w_main), full(w_vt), tab_spec, tab_spec],
        out_specs=[tok_spec(RET_WIDTH)] * 6 + [pl.BlockSpec((DIFF_WIDTH, tm), lambda i: (0, i))],
        out_shape=[out] * 6 + [jax.ShapeDtypeStruct((DIFF_WIDTH, tokens), BF16)],
        compiler_params=pltpu.CompilerParams(
            dimension_semantics=("parallel",), vmem_limit_bytes=VMEM_LIMIT),
        name="inproj",
    )(x2, mod, gain, w_main, w_vt, cos_t, sin_t)


def _retention_kernel(q_ref, k_ref, v_ref, g_ref, dec_ref, qdec_ref, kdec_ref, rdec_ref,
                      bmask_ref, gmean_ref, gain_ref, o_ref, state_ref, *, chunk):
    @pl.when(pl.program_id(1) == 0)
    def _():
        state_ref[...] = jnp.zeros_like(state_ref)

    lane = lax.broadcasted_iota(jnp.int32, (chunk, LANES), 1)
    first_head = lane < RET_HEAD_DIM
    gmean = gmean_ref[...]
    bmask = bmask_ref[...]
    pairs = range(RET_PAIRS)
    sl = [slice(p * LANES, (p + 1) * LANES) for p in pairs]
    q = [q_ref[:, sl[p]] for p in pairs]
    k = [k_ref[:, sl[p]] for p in pairs]
    v = [v_ref[:, sl[p]] for p in pairs]
    zero = jnp.zeros_like(q[0])
    q_stack = [jnp.concatenate([jnp.where(first_head, q[p], zero), jnp.where(first_head, zero, q[p])], axis=0)
               for p in pairs]
    scores = [(_dot_nt(q_stack[p], k[p]) * dec_ref[p]).astype(BF16) for p in pairs]
    state = [state_ref[p] for p in pairs]
    cross = [_dot(q[p], state[p].astype(BF16)) * qdec_ref[:, sl[p]] for p in pairs]
    k_dec = [(k[p].astype(F32) * kdec_ref[:, sl[p]]).astype(BF16) for p in pairs]
    for p in pairs:
        state_ref[p] = state[p] * rdec_ref[p] + _dot_tn(k_dec[p], v[p]) * bmask
    intra2 = [_dot(scores[p], v[p]) for p in pairs]
    y = [jnp.where(first_head, intra2[p][:chunk], intra2[p][chunk:]) + cross[p] for p in pairs]
    y_split = [_split_bf16(y[p]) for p in pairs]
    mu = [_dot(y_split[p][0], gmean) + _dot(y_split[p][1], gmean) for p in pairs]
    d = [y[p] - mu[p] for p in pairs]
    d_split = [_split_bf16(d[p] * d[p]) for p in pairs]
    var = [_dot(d_split[p][0], gmean) + _dot(d_split[p][1], gmean) for p in pairs]
    for p in pairs:
        yn = d[p] * lax.rsqrt(var[p] + EPS) * gain_ref[:, sl[p]]
        o_ref[:, sl[p]] = (g_ref[:, sl[p]].astype(F32) * yn).astype(BF16)


def _retention_tables(chunk):
    heads = jnp.arange(RET_HEADS, dtype=F32)
    log_gamma = jnp.log(1.0 - jnp.exp2(-5.0 - heads))
    idx = jnp.arange(chunk)
    rel = (idx[:, None] - idx[None, :]).astype(F32)
    decay = jnp.where(rel[None] >= 0, jnp.exp(log_gamma[:, None, None] * jnp.maximum(rel, 0.0)[None]), 0.0)
    dec2 = decay.reshape(RET_PAIRS, 2 * chunk, chunk)
    lane_lg = jnp.repeat(log_gamma, RET_HEAD_DIM)
    qdec = jnp.exp(lane_lg[None, :] * (idx + 1).astype(F32)[:, None])
    kdec = jnp.exp(lane_lg[None, :] * (chunk - 1 - idx).astype(F32)[:, None])
    rdec = jnp.exp(lane_lg * chunk).reshape(RET_PAIRS, LANES, 1) * jnp.ones((1, 1, LANES), F32)
    blk = jnp.arange(LANES) // RET_HEAD_DIM
    bmask = (blk[:, None] == blk[None, :]).astype(F32)
    gmean = (bmask / RET_HEAD_DIM).astype(BF16)
    return dec2, qdec, kdec, rdec, bmask, gmean


def _retention(rq, rk, rv, rg, gn_gain, batch, seq, chunk):
    nc = seq // chunk
    dec2, qdec, kdec, rdec, bmask, gmean = _retention_tables(chunk)
    tok_spec = pl.BlockSpec((chunk, RET_WIDTH), lambda b, n: (b * nc + n, 0))
    full = lambda a: pl.BlockSpec(a.shape, lambda b, n: (0,) * a.ndim)
    return pl.pallas_call(
        functools.partial(_retention_kernel, chunk=chunk),
        grid=(batch, nc),
        in_specs=[tok_spec] * 4 + [full(dec2), full(qdec), full(kdec), full(rdec), full(bmask),
                                   full(gmean), full(gn_gain)],
        out_specs=tok_spec,
        out_shape=jax.ShapeDtypeStruct(rq.shape, BF16),
        scratch_shapes=[pltpu.VMEM((RET_PAIRS, LANES, LANES), F32)],
        compiler_params=pltpu.CompilerParams(
            dimension_semantics=("parallel", "arbitrary"), vmem_limit_bytes=VMEM_LIMIT),
        name="retention",
    )(rq, rk, rv, rg, dec2, qdec, kdec, rdec, bmask, gmean, gn_gain)


NEG_BIG = -1e30


V_EXT_ROWS = DIFF_V_DIM + 16
QUERY_CHUNK = 256
SCORES_AHEAD_FULL = 3
SCORES_AHEAD_DIAG = 1


def _diag_chunks(tq, tk, d):
    assert tk == 2 * QUERY_CHUNK
    per_softmax = tq // QUERY_CHUNK
    out = []
    for c in range(2 * per_softmax):
        q0 = (c % per_softmax) * QUERY_CHUNK
        if q0 + QUERY_CHUNK - 1 < d * tk:
            continue
        kind = "full" if q0 >= (d + 1) * tk else ("tri" if q0 == d * tk else "low_tri")
        out.append((c, kind))
    return out


def _accumulate(acc_ref, cs, alpha, pv):
    acc_ref[:, cs] = alpha * acc_ref[:, cs] + pv


def _diffattn_kernel(q_ref, k_ref, vt_ref, lq1_ref, lk1_ref, lq2_ref, lk2_ref, gain_ref, bias_ref, o_ref,
                     qs_ref, vext_ref, m_ref, acc_ref, *, tq, tk, lambda_init):
    i = pl.program_id(2)
    nk = vext_ref.shape[0]

    @pl.when(i == 0)
    def _():
        for j in range(nk):
            vext_ref[j, 0:DIFF_V_DIM, :] = vt_ref[:, j * tk:(j + 1) * tk]
            vext_ref[j, DIFF_V_DIM:V_EXT_ROWS, :] = jnp.ones((V_EXT_ROWS - DIFF_V_DIM, tk), BF16)

    q = q_ref[...]
    lane = lax.broadcasted_iota(jnp.int32, q.shape, 1)
    zero = jnp.zeros_like(q)
    qs_ref[0:tq, :] = jnp.where(lane < DIFF_QK_DIM, q, zero)
    qs_ref[tq:2 * tq, :] = jnp.where(lane < DIFF_QK_DIM, zero, q)
    m_ref[...] = jnp.full_like(m_ref, NEG_BIG)
    acc_ref[...] = jnp.zeros_like(acc_ref)

    def step(j, chunks, n_ahead):
        start = pl.multiple_of(j * tk, tk)
        chunk = lambda c: slice(c * QUERY_CHUNK, (c + 1) * QUERY_CHUNK)

        def scores(c, kind):
            n_keys = QUERY_CHUNK if kind == "tri" else tk
            return _dot_nt(k_ref[pl.ds(start, n_keys), :], qs_ref[chunk(c), :])

        ahead = [scores(*chunks[n]) for n in range(min(n_ahead, len(chunks)))]
        pending = None
        for n, (c, kind) in enumerate(chunks):
            cs = chunk(c)
            st = ahead.pop(0)
            if n + n_ahead < len(chunks):
                ahead.append(scores(*chunks[n + n_ahead]))
            if kind == "tri":
                st = st + bias_ref[QUERY_CHUNK:, :]
            elif kind == "low_tri":
                st = st + bias_ref[...]
            m_old = m_ref[:, cs]
            m_new = jnp.maximum(m_old, jnp.max(st, axis=0, keepdims=True))
            alpha = jnp.exp2(m_old - m_new)
            p = jnp.exp2(st - m_new).astype(BF16)
            m_ref[:, cs] = m_new
            pv = _dot(vext_ref[j, :, 0:st.shape[0]], p)
            if pending is not None:
                pending()
            pending = functools.partial(_accumulate, acc_ref, cs, alpha, pv)
        pending()

    tiles_per_q = tq // tk
    all_chunks = [(c, "full") for c in range(2 * tq // QUERY_CHUNK)]
    lax.fori_loop(0, i * tiles_per_q, lambda j, c: (step(j, all_chunks, SCORES_AHEAD_FULL), c)[1], 0)
    for d in range(tiles_per_q):
        step(i * tiles_per_q + d, _diag_chunks(tq, tk, d), SCORES_AHEAD_DIAG)

    lam = (jnp.exp(jnp.sum(lq1_ref[...] * lk1_ref[...], axis=-1, keepdims=True))
           - jnp.exp(jnp.sum(lq2_ref[...] * lk2_ref[...], axis=-1, keepdims=True)) + lambda_init)
    acc = acc_ref[...]
    o2 = acc[0:DIFF_V_DIM, :] * (1.0 / acc[DIFF_V_DIM:DIFF_V_DIM + 1, :])
    o = (o2[:, :tq] - lam * o2[:, tq:]).T
    ms = jnp.mean(o * o, axis=-1, keepdims=True)
    o = o * lax.rsqrt(ms + EPS) * gain_ref[...] * (1.0 - lambda_init)
    o_ref[...] = o.astype(BF16)


def _diffattn(dq, dk, dvt, lam_q1, lam_k1, lam_q2, lam_k2, gain, batch, seq, lambda_init, tq, tk):
    nq = seq // tq
    q_spec = pl.BlockSpec((tq, LANES), lambda b, h, i: (b * nq + i, h))
    k_spec = pl.BlockSpec((seq, LANES), lambda b, h, i: (b, h))
    vt_spec = pl.BlockSpec((DIFF_V_DIM, seq), lambda b, h, i: (h, b))
    vec = lambda a: pl.BlockSpec(a.shape, lambda b, h, i: (0, 0))
    key = jnp.arange(tk)[:, None]
    query = QUERY_CHUNK + jnp.arange(QUERY_CHUNK)[None, :]
    bias = jnp.where(key <= query, 0.0, NEG_BIG).astype(F32)
    return pl.pallas_call(
        functools.partial(_diffattn_kernel, tq=tq, tk=tk, lambda_init=lambda_init),
        grid=(batch, DIFF_HEADS, nq),
        in_specs=[q_spec, k_spec, vt_spec, vec(lam_q1), vec(lam_k1), vec(lam_q2), vec(lam_k2), vec(gain),
                  vec(bias)],
        out_specs=q_spec,
        out_shape=jax.ShapeDtypeStruct(dq.shape, BF16),
        scratch_shapes=[
            pltpu.VMEM((2 * tq, LANES), BF16),
            pltpu.VMEM((seq // tk, V_EXT_ROWS, tk), BF16),
            pltpu.VMEM((1, 2 * tq), F32),
            pltpu.VMEM((V_EXT_ROWS, 2 * tq), F32),
        ],
        compiler_params=pltpu.CompilerParams(
            dimension_semantics=("parallel", "parallel", "arbitrary"), vmem_limit_bytes=VMEM_LIMIT),
        name="diffattn",
    )(dq, dk, dvt, lam_q1, lam_k1, lam_q2, lam_k2, gain, bias)


def _route(logits):
    r = [logits[g:g + 1, :] for g in range(N_GROUPS)]
    gmax = jnp.maximum(jnp.maximum(r[0], r[1]), jnp.maximum(r[2], r[3]))
    g_idx = jnp.where(r[0] == gmax, 0, jnp.where(r[1] == gmax, 1, jnp.where(r[2] == gmax, 2, 3)))
    denom = sum(jnp.exp(rg - gmax) for rg in r)
    g_weight = 1.0 / denom
    sel = jnp.zeros((EXPERTS_PER_GROUP, logits.shape[1]), F32)
    for g in range(N_GROUPS):
        rows = logits[8 + g * EXPERTS_PER_GROUP:8 + (g + 1) * EXPERTS_PER_GROUP, :]
        sel = jnp.where(g_idx == g, rows, sel)
    eidx = lax.broadcasted_iota(jnp.int32, sel.shape, 0)
    v1 = jnp.max(sel, axis=0, keepdims=True)
    i1 = jnp.min(jnp.where(sel == v1, eidx, EXPERTS_PER_GROUP), axis=0, keepdims=True)
    sel2 = jnp.where(eidx == i1, -jnp.inf, sel)
    v2 = jnp.max(sel2, axis=0, keepdims=True)
    i2 = jnp.min(jnp.where(sel2 == v2, eidx, EXPERTS_PER_GROUP), axis=0, keepdims=True)
    e2 = jnp.exp(v2 - v1)
    w1 = g_weight / (1.0 + e2)
    w2 = g_weight * e2 / (1.0 + e2)
    return g_idx, i1, i2, w1, w2


def _outproj_kernel(ret_ref, diff_ref, x_ref, mod_ref, gain_ref, wo_ref, wr_hi_ref, wr_lo_ref, br_ref, tri_ref,
                    x1_ref, h2_ref, ri_ref, rw_ref, cnt_ref, *, tiles_per_batch):
    b = pl.program_id(0) // tiles_per_batch
    mix = _dot(ret_ref[...], wo_ref[0:RET_WIDTH, :]) + _dot(diff_ref[...], wo_ref[RET_WIDTH:, :])
    gate1 = mod_ref[pl.ds(b, 1), 2 * D_MODEL:3 * D_MODEL]
    x1 = x_ref[...] + gate1 * mix
    x1_ref[...] = x1
    shift = mod_ref[pl.ds(b, 1), 3 * D_MODEL:4 * D_MODEL]
    scale = mod_ref[pl.ds(b, 1), 4 * D_MODEL:5 * D_MODEL]
    h2 = _norm_modulate(x1, gain_ref[...], shift, scale)
    h_hi, h_lo = _split_bf16(h2)
    h2_ref[...] = h_hi
    wr_hi = wr_hi_ref[...]
    logits = _dot_nt(wr_hi, h_hi) + _dot_nt(wr_lo_ref[...], h_hi) + _dot_nt(wr_hi, h_lo) + br_ref[...]
    g_idx, i1, i2, w1, w2 = _route(logits)
    e1 = g_idx * EXPERTS_PER_GROUP + i1
    e2 = g_idx * EXPERTS_PER_GROUP + i2
    eidx = lax.broadcasted_iota(jnp.int32, (N_EXPERTS, logits.shape[1]), 0)
    hit1 = eidx == e1
    hit2 = eidx == e2
    onehot = jnp.where(hit1 | hit2, 1.0, 0.0)
    before = _dot(onehot.astype(BF16), tri_ref[...])
    r1 = jnp.sum(jnp.where(hit1, before, 0.0), axis=0, keepdims=True)
    r2 = jnp.sum(jnp.where(hit2, before, 0.0), axis=0, keepdims=True)
    zi = jnp.zeros_like(e1)
    ri_ref[...] = jnp.concatenate([e1, e2, r1.astype(jnp.int32), r2.astype(jnp.int32), zi, zi, zi, zi], axis=0)
    zf = jnp.zeros_like(w1)
    rw_ref[...] = jnp.concatenate([w1, w2, zf, zf, zf, zf, zf, zf], axis=0)
    counts = jnp.sum(onehot, axis=1, keepdims=True)
    cnt_ref[0] = jnp.broadcast_to(counts, (N_EXPERTS, LANES)).astype(jnp.int32)


def _outproj(ret_out, diff_out, x2, mod, gain, w_out, wr_hi, wr_lo, br, seq, tm):
    tokens = x2.shape[0]
    tiles_per_batch = seq // tm
    n_tiles = tokens // tm
    tri = (jnp.arange(tm)[:, None] < jnp.arange(tm)[None, :]).astype(BF16)
    tok_spec = lambda w: pl.BlockSpec((tm, w), lambda i: (i, 0))
    row_spec = pl.BlockSpec((8, tm), lambda i: (0, i))
    full = lambda a: pl.BlockSpec(a.shape, lambda i: (0,) * a.ndim)
    return pl.pallas_call(
        functools.partial(_outproj_kernel, tiles_per_batch=tiles_per_batch),
        grid=(n_tiles,),
        in_specs=[tok_spec(RET_WIDTH), tok_spec(DIFF_WIDTH), tok_spec(D_MODEL), full(mod), full(gain),
                  full(w_out), full(wr_hi), full(wr_lo), full(br), full(tri)],
        out_specs=[tok_spec(D_MODEL), tok_spec(D_MODEL), row_spec, row_spec,
                   pl.BlockSpec((1, N_EXPERTS, LANES), lambda i: (i, 0, 0))],
        out_shape=[jax.ShapeDtypeStruct((tokens, D_MODEL), F32),
                   jax.ShapeDtypeStruct((tokens, D_MODEL), BF16),
                   jax.ShapeDtypeStruct((8, tokens), jnp.int32),
                   jax.ShapeDtypeStruct((8, tokens), F32),
                   jax.ShapeDtypeStruct((n_tiles, N_EXPERTS, LANES), jnp.int32)],
        compiler_params=pltpu.CompilerParams(
            dimension_semantics=("parallel",), vmem_limit_bytes=VMEM_LIMIT),
        name="outproj",
    )(ret_out, diff_out, x2, mod, gain, w_out, wr_hi, wr_lo, br, tri)


CHUNK = 8
TMX = 512
MAX_TAIL_CHUNKS = N_EXPERTS * (TMX // CHUNK - 1)


def _local_rows(tm):
    rows = 2 * tm + N_EXPERTS * (CHUNK - 1)
    return (rows + 15) // 16 * 16


def _sorted_rows_alloc(tokens, tm):
    worst = 2 * tokens + (tokens // tm) * N_EXPERTS * (CHUNK - 1) + N_EXPERTS * (TMX - CHUNK)
    return (worst + TMX - 1) // TMX * TMX


def _dispatch_plan(cnt, tokens, tm):
    i32 = jnp.int32
    nch_max = _local_rows(tm) // CHUNK
    pad = (cnt + CHUNK - 1) // CHUNK * CHUNK
    local_end = jnp.cumsum(pad, axis=1)
    local_start = local_end - pad
    seg_rows = jnp.sum(pad, axis=0)
    seg_pad = (seg_rows + TMX - 1) // TMX * TMX
    seg_end = jnp.cumsum(seg_pad)
    seg_start = seg_end - seg_pad
    run_dst = seg_start[None, :] + jnp.cumsum(pad, axis=0) - pad
    row = CHUNK * jnp.arange(nch_max, dtype=i32)[None, :, None]
    owns = (local_start[:, None, :] <= row) & (row < local_end[:, None, :])
    chunk_dst = row[:, :, 0] + jnp.sum(jnp.where(owns, (run_dst - local_start)[:, None, :], 0), axis=-1)
    tail_n = (seg_pad - seg_rows) // CHUNK
    tail_end = jnp.cumsum(tail_n)
    tail_start = tail_end - tail_n
    k = jnp.arange(MAX_TAIL_CHUNKS, dtype=i32)[:, None]
    towns = (tail_start[None, :] <= k) & (k < tail_end[None, :])
    tail_dst = jnp.sum(jnp.where(towns, (seg_start + seg_rows)[None, :] + CHUNK * (k - tail_start[None, :]), 0),
                       axis=-1)
    m = TMX * jnp.arange(_sorted_rows_alloc(tokens, tm) // TMX, dtype=i32)
    tile_expert = jnp.minimum(jnp.sum(seg_end[None, :] <= m[:, None], axis=-1), N_EXPERTS - 1)
    towns = (seg_start[None, :] <= m[:, None]) & (m[:, None] < seg_end[None, :])
    used = seg_pad > 0
    parity = (jnp.cumsum(used) - used) % 2
    eids = jnp.arange(N_EXPERTS, dtype=i32)
    later_used = (eids[None, :] > eids[:, None]) & used[None, :]
    next_used = jnp.min(jnp.where(later_used, eids[None, :], N_EXPERTS), axis=1)
    next_used = jnp.where(next_used == N_EXPERTS, -1, next_used)
    pick = lambda per_expert: jnp.sum(jnp.where(towns, per_expert[None, :], 0), axis=-1)
    tile_first = jnp.sum(jnp.where(towns & (seg_start[None, :] == m[:, None]), 1, 0), axis=-1)
    tile_next = jnp.where(jnp.any(towns, axis=-1), pick(next_used), -1)
    return dict(
        tile_first=tile_first.astype(i32),
        tile_slot=pick(parity).astype(i32),
        tile_next=tile_next.astype(i32),
        local_start=local_start.reshape(-1).astype(i32),
        n_chunks=(local_end[:, -1] // CHUNK).astype(i32),
        chunk_dst=chunk_dst.reshape(-1).astype(i32),
        tail_dst=tail_dst.astype(i32),
        n_tail=tail_end[-1:].astype(i32),
        tile_expert=tile_expert.astype(i32),
        n_used=(seg_end[-1:] // TMX).astype(i32),
    )


def _local_slots(ri_ref, local_start_ref, tile):
    e1, e2 = ri_ref[0:1, :], ri_ref[1:2, :]
    s1, s2 = ri_ref[2:3, :], ri_ref[3:4, :]
    for e in range(N_EXPERTS):
        start = local_start_ref[tile * N_EXPERTS + e]
        s1 = s1 + jnp.where(e1 == e, start, 0)
        s2 = s2 + jnp.where(e2 == e, start, 0)
    return s1, s2


def _dispatch_kernel(local_start_ref, n_chunks_ref, chunk_dst_ref, tail_dst_ref, n_tail_ref, n_used_ref,
                     h_ref, ri_ref, xs_ref, buf_ref, zero_ref, sem_ref, tail_sem_ref, *, r_loc):
    b = pl.program_id(0)
    nb = pl.num_programs(0)
    slot = b % 2
    nch_max = r_loc // CHUNK

    def chunk_copy(tile, sl, j):
        row = pl.multiple_of(j * CHUNK, CHUNK)
        dst = pl.multiple_of(chunk_dst_ref[tile * nch_max + j], CHUNK)
        return pltpu.make_async_copy(buf_ref.at[sl, pl.ds(row, CHUNK), :], xs_ref.at[pl.ds(dst, CHUNK), :],
                                     sem_ref.at[sl])

    def drain(tile, sl):
        lax.fori_loop(0, n_chunks_ref[tile], lambda j, c: (chunk_copy(tile, sl, j).wait(), c)[1], 0)

    @pl.when(b >= 2)
    def _():
        drain(b - 2, slot)

    s1, s2 = _local_slots(ri_ref, local_start_ref, b)
    rows = lax.broadcasted_iota(jnp.int32, (r_loc, s1.shape[1]), 0)
    perm = jnp.where((rows == s1) | (rows == s2), 1.0, 0.0).astype(BF16)
    buf_ref[slot] = _dot(perm, h_ref[...])
    lax.fori_loop(0, n_chunks_ref[b], lambda j, c: (chunk_copy(b, slot, j).start(), c)[1], 0)

    @pl.when(b == nb - 1)
    def _():
        zero_ref[...] = jnp.zeros_like(zero_ref)

        def tail_copy(k):
            dst = pl.multiple_of(tail_dst_ref[k], CHUNK)
            return pltpu.make_async_copy(zero_ref.at[pl.ds(0, CHUNK), :], xs_ref.at[pl.ds(dst, CHUNK), :],
                                         tail_sem_ref.at[0])

        def unused_tile_copy(m):
            dst = pl.multiple_of(m * TMX, TMX)
            return pltpu.make_async_copy(zero_ref, xs_ref.at[pl.ds(dst, TMX), :], tail_sem_ref.at[1])

        n_alloc = xs_ref.shape[0] // TMX
        lax.fori_loop(0, n_tail_ref[0], lambda k, c: (tail_copy(k).start(), c)[1], 0)
        lax.fori_loop(n_used_ref[0], n_alloc, lambda m, c: (unused_tile_copy(m).start(), c)[1], 0)
        lax.fori_loop(0, n_tail_ref[0], lambda k, c: (tail_copy(k).wait(), c)[1], 0)
        lax.fori_loop(n_used_ref[0], n_alloc, lambda m, c: (unused_tile_copy(m).wait(), c)[1], 0)

        @pl.when(b >= 1)
        def _():
            drain(b - 1, 1 - slot)

        drain(b, slot)


def _dispatch(h2, ri, plan, tm):
    tokens = h2.shape[0]
    r_loc = _local_rows(tm)
    grid_spec = pltpu.PrefetchScalarGridSpec(
        num_scalar_prefetch=6,
        grid=(tokens // tm,),
        in_specs=[pl.BlockSpec((tm, D_MODEL), lambda i, *_: (i, 0)),
                  pl.BlockSpec((8, tm), lambda i, *_: (0, i))],
        out_specs=pl.BlockSpec(memory_space=pl.ANY),
        scratch_shapes=[pltpu.VMEM((2, r_loc, D_MODEL), F32), pltpu.VMEM((TMX, D_MODEL), F32),
                        pltpu.SemaphoreType.DMA((2,)), pltpu.SemaphoreType.DMA((2,))],
    )
    return pl.pallas_call(
        functools.partial(_dispatch_kernel, r_loc=r_loc),
        grid_spec=grid_spec,
        out_shape=jax.ShapeDtypeStruct((_sorted_rows_alloc(tokens, tm), D_MODEL), F32),
        compiler_params=pltpu.CompilerParams(
            dimension_semantics=("arbitrary",), vmem_limit_bytes=VMEM_LIMIT),
        name="dispatch",
    )(plan["local_start"], plan["n_chunks"], plan["chunk_dst"], plan["tail_dst"], plan["n_tail"], plan["n_used"],
      h2, ri)


def _experts_kernel(tile_expert_ref, n_used_ref, first_ref, slot_ref, next_ref, xs_ref, wg_hbm, wu_hbm, wd_hbm,
                    ys_ref, wg_st, wu_st, wd_st, wg_bf, wu_bf, wd_bf, sem_ref):
    m = pl.program_id(0)

    def weight_copies(e, s):
        return [pltpu.make_async_copy(src.at[e], dst.at[s], sem_ref.at[s, n])
                for n, (src, dst) in enumerate([(wg_hbm, wg_st), (wu_hbm, wu_st), (wd_hbm, wd_st)])]

    @pl.when(m < n_used_ref[0])
    def _():
        @pl.when(first_ref[m] == 1)
        def _():
            s = slot_ref[m]

            @pl.when(m == 0)
            def _():
                for cp in weight_copies(tile_expert_ref[0], 0):
                    cp.start()

            for cp in weight_copies(tile_expert_ref[m], s):
                cp.wait()

            @pl.when(next_ref[m] >= 0)
            def _():
                for cp in weight_copies(next_ref[m], 1 - s):
                    cp.start()

            wg_bf[...] = wg_st[s].astype(BF16)
            wu_bf[...] = wu_st[s].astype(BF16)
            wd_bf[...] = wd_st[s].astype(BF16)

        x = xs_ref[...].astype(BF16)
        a = _dot(x, wg_bf[...])
        u = _dot(x, wu_bf[...])
        hid = (_silu(a) * u).astype(BF16)
        ys_ref[...] = _dot(hid, wd_bf[...])

    @pl.when(m >= n_used_ref[0])
    def _():
        ys_ref[...] = jnp.zeros_like(ys_ref)


def _experts(xs, plan, wg, wu, wd):
    n_tiles = xs.shape[0] // TMX
    last_used = lambda m, n_used: jnp.minimum(m, n_used[0] - 1)
    row_spec = pl.BlockSpec((TMX, D_MODEL), lambda m, te, nu, *_: (last_used(m, nu), 0))
    out_spec = pl.BlockSpec((TMX, D_MODEL), lambda m, *_: (m, 0))
    hbm = pl.BlockSpec(memory_space=pl.ANY)
    up_shape, down_shape = (D_MODEL, D_EXPERT), (D_EXPERT, D_MODEL)
    grid_spec = pltpu.PrefetchScalarGridSpec(
        num_scalar_prefetch=5,
        grid=(n_tiles,),
        in_specs=[row_spec, hbm, hbm, hbm],
        out_specs=out_spec,
        scratch_shapes=[pltpu.VMEM((2,) + up_shape, F32), pltpu.VMEM((2,) + up_shape, F32),
                        pltpu.VMEM((2,) + down_shape, F32),
                        pltpu.VMEM(up_shape, BF16), pltpu.VMEM(up_shape, BF16), pltpu.VMEM(down_shape, BF16),
                        pltpu.SemaphoreType.DMA((2, 3))],
    )
    return pl.pallas_call(
        _experts_kernel,
        grid_spec=grid_spec,
        out_shape=jax.ShapeDtypeStruct(xs.shape, F32),
        compiler_params=pltpu.CompilerParams(
            dimension_semantics=("arbitrary",), vmem_limit_bytes=VMEM_LIMIT),
        name="experts",
    )(plan["tile_expert"], plan["n_used"], plan["tile_first"], plan["tile_slot"], plan["tile_next"], xs, wg, wu, wd)


def _combine_kernel(local_start_ref, n_chunks_ref, chunk_dst_ref, ys_ref, ri_ref, rw_ref, x1_ref, mod_ref,
                    gain_ref, o_ref, buf_ref, sem_ref, *, r_loc, tiles_per_batch):
    b = pl.program_id(0)
    nb = pl.num_programs(0)
    slot = b % 2
    nch_max = r_loc // CHUNK

    def chunk_copy(tile, sl, j):
        row = pl.multiple_of(j * CHUNK, CHUNK)
        src = pl.multiple_of(chunk_dst_ref[tile * nch_max + j], CHUNK)
        return pltpu.make_async_copy(ys_ref.at[pl.ds(src, CHUNK), :], buf_ref.at[sl, pl.ds(row, CHUNK), :],
                                     sem_ref.at[sl])

    def fetch(tile, sl):
        lax.fori_loop(0, n_chunks_ref[tile], lambda j, c: (chunk_copy(tile, sl, j).start(), c)[1], 0)

    @pl.when(b == 0)
    def _():
        buf_ref[...] = jnp.zeros_like(buf_ref)
        fetch(0, 0)

    @pl.when(b + 1 < nb)
    def _():
        fetch(b + 1, 1 - slot)

    lax.fori_loop(0, n_chunks_ref[b], lambda j, c: (chunk_copy(b, slot, j).wait(), c)[1], 0)

    s1, s2 = _local_slots(ri_ref, local_start_ref, b)
    rows = lax.broadcasted_iota(jnp.int32, (r_loc, s1.shape[1]), 0)
    hit1 = rows == s1
    hit2 = rows == s2
    w_row = jnp.sum(jnp.where(hit1, rw_ref[0:1, :], jnp.where(hit2, rw_ref[1:2, :], 0.0)), axis=1, keepdims=True)
    perm = jnp.where(hit1 | hit2, 1.0, 0.0).astype(BF16)
    yw = (buf_ref[slot] * w_row).astype(BF16)
    moe = _dot_tn(perm, yw)
    batch = b // tiles_per_batch
    gate2 = mod_ref[pl.ds(batch, 1), 5 * D_MODEL:6 * D_MODEL]
    x2 = x1_ref[...] + gate2 * moe
    ms = jnp.mean(x2 * x2, axis=-1, keepdims=True)
    o_ref[...] = x2 * lax.rsqrt(ms + EPS) * gain_ref[...]


def _combine(ys, ri, rw, x1, mod, gain, plan, seq, tm):
    tokens = x1.shape[0]
    r_loc = _local_rows(tm)
    row_spec = pl.BlockSpec((8, tm), lambda i, *_: (0, i))
    tok_spec = pl.BlockSpec((tm, D_MODEL), lambda i, *_: (i, 0))
    full = lambda a: pl.BlockSpec(a.shape, lambda i, *_: (0,) * a.ndim)
    grid_spec = pltpu.PrefetchScalarGridSpec(
        num_scalar_prefetch=3,
        grid=(tokens // tm,),
        in_specs=[pl.BlockSpec(memory_space=pl.ANY), row_spec, row_spec, tok_spec, full(mod), full(gain)],
        out_specs=tok_spec,
        scratch_shapes=[pltpu.VMEM((2, r_loc, D_MODEL), F32), pltpu.SemaphoreType.DMA((2,))],
    )
    return pl.pallas_call(
        functools.partial(_combine_kernel, r_loc=r_loc, tiles_per_batch=seq // tm),
        grid_spec=grid_spec,
        out_shape=jax.ShapeDtypeStruct((tokens, D_MODEL), F32),
        compiler_params=pltpu.CompilerParams(
            dimension_semantics=("arbitrary",), vmem_limit_bytes=VMEM_LIMIT),
        name="combine",
    )(plan["local_start"], plan["n_chunks"], plan["chunk_dst"], ys, ri, rw, x1, mod, gain)


def _rotary_tables(seq):
    half = RET_HEAD_DIM // 2
    inv_freq = 1.0 / (ROPE_BASE ** (jnp.arange(half, dtype=F32) / half))
    ang = jnp.arange(seq).astype(F32)[:, None] * inv_freq[None, :]
    cos = jnp.cos(ang)
    sin = jnp.sin(ang)
    return jnp.tile(cos, (1, 4)), jnp.concatenate([-sin, sin, -sin, sin], axis=1)


def _pick_tile(n, pref):
    t = min(n, pref)
    assert n % t == 0, (n, t)
    return t


def kernel(x, c, ada_w, ada_b, norm1_gain, norm2_gain, w_in, w_out, ret_gn_gain, lam_q1, lam_k1, lam_q2,
           lam_k2, diff_subln_gain, w_group, b_group, w_expert, b_expert, w_gate, w_up, w_down, final_gain):
    batch, seq, d = x.shape
    assert d == D_MODEL and batch <= 8 and ada_w.shape[0] == 1
    layer = 0
    lambda_init = 0.8 - 0.6 * math.exp(-0.3 * layer)
    tokens = batch * seq
    x2 = x.reshape(tokens, d)
    tm = _pick_tile(seq, 512)

    c_pad = jnp.zeros((8, d), F32).at[:batch].set(c)
    mod = _adaln(c_pad, ada_w[layer], ada_b[layer].reshape(1, -1))

    cos_t, sin_t = _rotary_tables(seq)
    n_main = 4 * RET_WIDTH + 2 * DIFF_WIDTH
    rq, rk, rv, rg, dq, dk, dvt = _inproj(
        x2, mod, norm1_gain[layer].reshape(1, d), w_in[layer, :, :n_main].astype(BF16),
        w_in[layer, :, n_main:].T.astype(BF16), cos_t, sin_t, seq, tm)

    ret_out = _retention(rq, rk, rv, rg, ret_gn_gain[layer].reshape(1, RET_WIDTH), batch, seq,
                         _pick_tile(seq, 128))
    diff_out = _diffattn(
        dq, dk, dvt, lam_q1[layer].reshape(1, -1), lam_k1[layer].reshape(1, -1), lam_q2[layer].reshape(1, -1),
        lam_k2[layer].reshape(1, -1), diff_subln_gain[layer].reshape(1, -1), batch, seq, lambda_init,
        _pick_tile(seq, 1024), 2 * QUERY_CHUNK)

    w_router = jnp.concatenate(
        [w_group[layer].T, jnp.zeros((8 - N_GROUPS, d), F32), w_expert[layer].reshape(d, N_EXPERTS).T], axis=0)
    b_router = jnp.concatenate(
        [b_group[layer], jnp.zeros((8 - N_GROUPS,), F32), b_expert[layer].reshape(N_EXPERTS)]).reshape(-1, 1)
    wr_hi = w_router.astype(BF16)
    wr_lo = (w_router - wr_hi.astype(F32)).astype(BF16)
    x1, h2, ri, rw, cnt = _outproj(ret_out, diff_out, x2, mod, norm2_gain[layer].reshape(1, d),
                                   w_out[layer].astype(BF16), wr_hi, wr_lo, b_router, seq, tm)

    plan = _dispatch_plan(cnt[:, :, 0], tokens, tm)
    xs = _dispatch(h2, ri, plan, tm)
    ys = _experts(xs, plan, w_gate[layer].reshape(N_EXPERTS, d, D_EXPERT),
                  w_up[layer].reshape(N_EXPERTS, d, D_EXPERT), w_down[layer].reshape(N_EXPERTS, D_EXPERT, d))
    out = _combine(ys, ri, rw, x1, mod, final_gain.reshape(1, d), plan, seq, tm)
    return out.reshape(batch, seq, d)
```

```python
import functools
import math

import jax
import jax.numpy as jnp
from jax import lax
from jax.experimental import pallas as pl
from jax.experimental.pallas import tpu as pltpu

F32 = jnp.float32
BF16 = jnp.bfloat16

D_MODEL = 1024
RET_HEAD_DIM = 64
RET_WIDTH = 512
RET_HEADS = 8
RET_PAIRS = RET_HEADS // 2
DIFF_QK_DIM = 64
DIFF_V_DIM = 128
DIFF_HEADS = 4
DIFF_WIDTH = 512
N_GROUPS = 4
EXPERTS_PER_GROUP = 8
N_EXPERTS = N_GROUPS * EXPERTS_PER_GROUP
D_EXPERT = 512
N_MOD = 6
ROPE_BASE = 10000.0
EPS = 1e-6
LANES = 128
ROUTER_ROWS = 8 + N_EXPERTS
VMEM_LIMIT = 56 * 1024 * 1024


def _dot(a, b):
    return jnp.dot(a, b, preferred_element_type=F32)


def _dot_nt(a, b):
    return lax.dot_general(a, b, (((1,), (1,)), ((), ())), preferred_element_type=F32)


def _dot_tn(a, b):
    return lax.dot_general(a, b, (((0,), (0,)), ((), ())), preferred_element_type=F32)


def _split_bf16(x):
    hi = x.astype(BF16)
    lo = (x - hi.astype(F32)).astype(BF16)
    return hi, lo


def _silu(x):
    return x / (1.0 + jnp.exp(-x))


def _adaln_kernel(c_ref, w_ref, b_ref, o_ref):
    ca = _silu(c_ref[...])
    c_hi, c_lo = _split_bf16(ca)
    w_hi, w_lo = _split_bf16(w_ref[...])
    o_ref[...] = _dot(c_hi, w_hi) + _dot(c_lo, w_hi) + _dot(c_hi, w_lo) + b_ref[...]


def _adaln(c_pad, ada_w, ada_b):
    n_out = ada_w.shape[1]
    tn = D_MODEL
    return pl.pallas_call(
        _adaln_kernel,
        grid=(n_out // tn,),
        in_specs=[
            pl.BlockSpec((8, D_MODEL), lambda j: (0, 0)),
            pl.BlockSpec((D_MODEL, tn), lambda j: (0, j)),
            pl.BlockSpec((1, tn), lambda j: (0, j)),
        ],
        out_specs=pl.BlockSpec((8, tn), lambda j: (0, j)),
        out_shape=jax.ShapeDtypeStruct((8, n_out), F32),
        compiler_params=pltpu.CompilerParams(vmem_limit_bytes=VMEM_LIMIT),
        name="adaln",
    )(c_pad, ada_w, ada_b)


def _norm_modulate(x, gain, shift, scale):
    ms = jnp.mean(x * x, axis=-1, keepdims=True)
    y = x * lax.rsqrt(ms + EPS) * gain
    return y * (1.0 + scale) + shift


def _rotary_slab(x, cos, sin_signed, lane_lo):
    swapped = jnp.where(lane_lo, pltpu.roll(x, 96, 1), pltpu.roll(x, 32, 1))
    return x * cos + swapped * sin_signed


def _inproj_kernel(x_ref, mod_ref, gain_ref, w_ref, cos_ref, sin_ref,
                   rq_ref, rk_ref, rv_ref, rg_ref, dq_ref, dk_ref, dvt_ref, *, tiles_per_batch):
    b = pl.program_id(0) // tiles_per_batch
    shift = mod_ref[pl.ds(b, 1), 0:D_MODEL]
    scale = mod_ref[pl.ds(b, 1), D_MODEL:2 * D_MODEL]
    h = _norm_modulate(x_ref[...], gain_ref[...], shift, scale).astype(BF16)
    cos = cos_ref[...]
    sin = sin_ref[...]
    lane = lax.broadcasted_iota(jnp.int32, cos.shape, 1)
    lane_lo = (lane % 64) < 32

    def proj(chunk):
        return _dot(h, w_ref[:, chunk * RET_WIDTH:(chunk + 1) * RET_WIDTH])

    def rotary(acc, out_ref, post_scale):
        for s in range(RET_WIDTH // LANES):
            sl = slice(s * LANES, (s + 1) * LANES)
            out_ref[:, sl] = (_rotary_slab(acc[:, sl], cos, sin, lane_lo) * post_scale).astype(BF16)

    rotary(proj(0), rq_ref, 1.0)
    rotary(proj(1), rk_ref, RET_HEAD_DIM ** -0.5)
    rv_ref[...] = proj(2).astype(BF16)
    rg_ref[...] = _silu(proj(3)).astype(BF16)
    dq_ref[...] = (proj(4) * (DIFF_QK_DIM ** -0.5 * math.log2(math.e))).astype(BF16)
    dk_ref[...] = proj(5).astype(BF16)
    dvt_ref[...] = proj(6).T.astype(BF16)


def _inproj(x2, mod, gain, w_in, cos_t, sin_t, seq, tm):
    tokens = x2.shape[0]
    tiles_per_batch = seq // tm
    tok_spec = lambda w: pl.BlockSpec((tm, w), lambda i: (i, 0))
    tab_spec = pl.BlockSpec((tm, LANES), lambda i: (i % tiles_per_batch, 0))
    full = lambda a: pl.BlockSpec(a.shape, lambda i: (0,) * a.ndim)
    out = jax.ShapeDtypeStruct((tokens, RET_WIDTH), BF16)
    return pl.pallas_call(
        functools.partial(_inproj_kernel, tiles_per_batch=tiles_per_batch),
        grid=(tokens // tm,),
        in_specs=[tok_spec(D_MODEL), full(mod), full(gain), full(w_in), tab_spec, tab_spec],
        out_specs=[tok_spec(RET_WIDTH)] * 6 + [pl.BlockSpec((DIFF_WIDTH, tm), lambda i: (0, i))],
        out_shape=[out] * 6 + [jax.ShapeDtypeStruct((DIFF_WIDTH, tokens), BF16)],
        compiler_params=pltpu.CompilerParams(
            dimension_semantics=("parallel",), vmem_limit_bytes=VMEM_LIMIT),
        name="inproj",
    )(x2, mod, gain, w_in, cos_t, sin_t)


def _retention_kernel(q_ref, k_ref, v_ref, g_ref, dec_ref, qdec_ref, kdec_ref, rdec_ref,
                      bmask_ref, gmean_ref, gain_ref, o_ref, state_ref, *, chunk):
    @pl.when(pl.program_id(1) == 0)
    def _():
        state_ref[...] = jnp.zeros_like(state_ref)

    lane = lax.broadcasted_iota(jnp.int32, (chunk, LANES), 1)
    first_head = lane < RET_HEAD_DIM
    gmean = gmean_ref[...]
    bmask = bmask_ref[...]
    pairs = range(RET_PAIRS)
    sl = [slice(p * LANES, (p + 1) * LANES) for p in pairs]
    q = [q_ref[:, sl[p]] for p in pairs]
    k = [k_ref[:, sl[p]] for p in pairs]
    v = [v_ref[:, sl[p]] for p in pairs]
    zero = jnp.zeros_like(q[0])
    q_stack = [jnp.concatenate([jnp.where(first_head, q[p], zero), jnp.where(first_head, zero, q[p])], axis=0)
               for p in pairs]
    scores = [(_dot_nt(q_stack[p], k[p]) * dec_ref[p]).astype(BF16) for p in pairs]
    state = [state_ref[p] for p in pairs]
    cross = [_dot(q[p], state[p].astype(BF16)) * qdec_ref[:, sl[p]] for p in pairs]
    k_dec = [(k[p].astype(F32) * kdec_ref[:, sl[p]]).astype(BF16) for p in pairs]
    for p in pairs:
        state_ref[p] = state[p] * rdec_ref[p] + _dot_tn(k_dec[p], v[p]) * bmask
    intra2 = [_dot(scores[p], v[p]) for p in pairs]
    y = [jnp.where(first_head, intra2[p][:chunk], intra2[p][chunk:]) + cross[p] for p in pairs]
    y_split = [_split_bf16(y[p]) for p in pairs]
    mu = [_dot(y_split[p][0], gmean) + _dot(y_split[p][1], gmean) for p in pairs]
    d = [y[p] - mu[p] for p in pairs]
    d_split = [_split_bf16(d[p] * d[p]) for p in pairs]
    var = [_dot(d_split[p][0], gmean) + _dot(d_split[p][1], gmean) for p in pairs]
    for p in pairs:
        yn = d[p] * lax.rsqrt(var[p] + EPS) * gain_ref[:, sl[p]]
        o_ref[:, sl[p]] = (g_ref[:, sl[p]].astype(F32) * yn).astype(BF16)


def _retention_tables(chunk):
    heads = jnp.arange(RET_HEADS, dtype=F32)
    log_gamma = jnp.log(1.0 - jnp.exp2(-5.0 - heads))
    idx = jnp.arange(chunk)
    rel = (idx[:, None] - idx[None, :]).astype(F32)
    decay = jnp.where(rel[None] >= 0, jnp.exp(log_gamma[:, None, None] * jnp.maximum(rel, 0.0)[None]), 0.0)
    dec2 = decay.reshape(RET_PAIRS, 2 * chunk, chunk)
    lane_lg = jnp.repeat(log_gamma, RET_HEAD_DIM)
    qdec = jnp.exp(lane_lg[None, :] * (idx + 1).astype(F32)[:, None])
    kdec = jnp.exp(lane_lg[None, :] * (chunk - 1 - idx).astype(F32)[:, None])
    rdec = jnp.exp(lane_lg * chunk).reshape(RET_PAIRS, LANES, 1) * jnp.ones((1, 1, LANES), F32)
    blk = jnp.arange(LANES) // RET_HEAD_DIM
    bmask = (blk[:, None] == blk[None, :]).astype(F32)
    gmean = (bmask / RET_HEAD_DIM).astype(BF16)
    return dec2, qdec, kdec, rdec, bmask, gmean


def _retention(rq, rk, rv, rg, gn_gain, batch, seq, chunk):
    nc = seq // chunk
    dec2, qdec, kdec, rdec, bmask, gmean = _retention_tables(chunk)
    tok_spec = pl.BlockSpec((chunk, RET_WIDTH), lambda b, n: (b * nc + n, 0))
    full = lambda a: pl.BlockSpec(a.shape, lambda b, n: (0,) * a.ndim)
    return pl.pallas_call(
        functools.partial(_retention_kernel, chunk=chunk),
        grid=(batch, nc),
        in_specs=[tok_spec] * 4 + [full(dec2), full(qdec), full(kdec), full(rdec), full(bmask),
                                   full(gmean), full(gn_gain)],
        out_specs=tok_spec,
        out_shape=jax.ShapeDtypeStruct(rq.shape, BF16),
        scratch_shapes=[pltpu.VMEM((RET_PAIRS, LANES, LANES), F32)],
        compiler_params=pltpu.CompilerParams(
            dimension_semantics=("parallel", "arbitrary"), vmem_limit_bytes=VMEM_LIMIT),
        name="retention",
    )(rq, rk, rv, rg, dec2, qdec, kdec, rdec, bmask, gmean, gn_gain)


NEG_BIG = -1e30


V_EXT_ROWS = DIFF_V_DIM + 16
QUERY_CHUNK = 256
SCORES_AHEAD_FULL = 3
SCORES_AHEAD_DIAG = 1


def _diag_chunks(tq, tk, d):
    assert tk == 2 * QUERY_CHUNK
    per_softmax = tq // QUERY_CHUNK
    out = []
    for c in range(2 * per_softmax):
        q0 = (c % per_softmax) * QUERY_CHUNK
        if q0 + QUERY_CHUNK - 1 < d * tk:
            continue
        kind = "full" if q0 >= (d + 1) * tk else ("tri" if q0 == d * tk else "low_tri")
        out.append((c, kind))
    return out


def _accumulate(acc_ref, cs, alpha, pv):
    acc_ref[:, cs] = alpha * acc_ref[:, cs] + pv


def _diffattn_kernel(q_ref, k_ref, vt_ref, lq1_ref, lk1_ref, lq2_ref, lk2_ref, gain_ref, bias_ref, o_ref,
                     qs_ref, vext_ref, m_ref, acc_ref, *, tq, tk, lambda_init):
    i = pl.program_id(2)
    nk = vext_ref.shape[0]

    @pl.when(i == 0)
    def _():
        for j in range(nk):
            vext_ref[j, 0:DIFF_V_DIM, :] = vt_ref[:, j * tk:(j + 1) * tk]
            vext_ref[j, DIFF_V_DIM:V_EXT_ROWS, :] = jnp.ones((V_EXT_ROWS - DIFF_V_DIM, tk), BF16)

    q = q_ref[...]
    lane = lax.broadcasted_iota(jnp.int32, q.shape, 1)
    zero = jnp.zeros_like(q)
    qs_ref[0:tq, :] = jnp.where(lane < DIFF_QK_DIM, q, zero)
    qs_ref[tq:2 * tq, :] = jnp.where(lane < DIFF_QK_DIM, zero, q)
    m_ref[...] = jnp.full_like(m_ref, NEG_BIG)
    acc_ref[...] = jnp.zeros_like(acc_ref)

    def step(j, chunks, n_ahead):
        start = pl.multiple_of(j * tk, tk)
        chunk = lambda c: slice(c * QUERY_CHUNK, (c + 1) * QUERY_CHUNK)

        def scores(c, kind):
            n_keys = QUERY_CHUNK if kind == "tri" else tk
            return _dot_nt(k_ref[pl.ds(start, n_keys), :], qs_ref[chunk(c), :])

        ahead = [scores(*chunks[n]) for n in range(min(n_ahead, len(chunks)))]
        pending = None
        for n, (c, kind) in enumerate(chunks):
            cs = chunk(c)
            st = ahead.pop(0)
            if n + n_ahead < len(chunks):
                ahead.append(scores(*chunks[n + n_ahead]))
            if kind == "tri":
                st = st + bias_ref[QUERY_CHUNK:, :]
            elif kind == "low_tri":
                st = st + bias_ref[...]
            m_old = m_ref[:, cs]
            m_new = jnp.maximum(m_old, jnp.max(st, axis=0, keepdims=True))
            alpha = jnp.exp2(m_old - m_new)
            p = jnp.exp2(st - m_new).astype(BF16)
            m_ref[:, cs] = m_new
            pv = _dot(vext_ref[j, :, 0:st.shape[0]], p)
            if pending is not None:
                pending()
            pending = functools.partial(_accumulate, acc_ref, cs, alpha, pv)
        pending()

    tiles_per_q = tq // tk
    all_chunks = [(c, "full") for c in range(2 * tq // QUERY_CHUNK)]
    lax.fori_loop(0, i * tiles_per_q, lambda j, c: (step(j, all_chunks, SCORES_AHEAD_FULL), c)[1], 0)
    for d in range(tiles_per_q):
        step(i * tiles_per_q + d, _diag_chunks(tq, tk, d), SCORES_AHEAD_DIAG)

    lam = (jnp.exp(jnp.sum(lq1_ref[...] * lk1_ref[...], axis=-1, keepdims=True))
           - jnp.exp(jnp.sum(lq2_ref[...] * lk2_ref[...], axis=-1, keepdims=True)) + lambda_init)
    acc = acc_ref[...]
    o2 = acc[0:DIFF_V_DIM, :] * (1.0 / acc[DIFF_V_DIM:DIFF_V_DIM + 1, :])
    o = (o2[:, :tq] - lam * o2[:, tq:]).T
    ms = jnp.mean(o * o, axis=-1, keepdims=True)
    o = o * lax.rsqrt(ms + EPS) * gain_ref[...] * (1.0 - lambda_init)
    o_ref[...] = o.astype(BF16)


def _diffattn(dq, dk, dvt, lam_q1, lam_k1, lam_q2, lam_k2, gain, batch, seq, lambda_init, tq, tk):
    nq = seq // tq
    q_spec = pl.BlockSpec((tq, LANES), lambda b, h, i: (b * nq + i, h))
    k_spec = pl.BlockSpec((seq, LANES), lambda b, h, i: (b, h))
    vt_spec = pl.BlockSpec((DIFF_V_DIM, seq), lambda b, h, i: (h, b))
    vec = lambda a: pl.BlockSpec(a.shape, lambda b, h, i: (0, 0))
    key = jnp.arange(tk)[:, None]
    query = QUERY_CHUNK + jnp.arange(QUERY_CHUNK)[None, :]
    bias = jnp.where(key <= query, 0.0, NEG_BIG).astype(F32)
    return pl.pallas_call(
        functools.partial(_diffattn_kernel, tq=tq, tk=tk, lambda_init=lambda_init),
        grid=(batch, DIFF_HEADS, nq),
        in_specs=[q_spec, k_spec, vt_spec, vec(lam_q1), vec(lam_k1), vec(lam_q2), vec(lam_k2), vec(gain),
                  vec(bias)],
        out_specs=q_spec,
        out_shape=jax.ShapeDtypeStruct(dq.shape, BF16),
        scratch_shapes=[
            pltpu.VMEM((2 * tq, LANES), BF16),
            pltpu.VMEM((seq // tk, V_EXT_ROWS, tk), BF16),
            pltpu.VMEM((1, 2 * tq), F32),
            pltpu.VMEM((V_EXT_ROWS, 2 * tq), F32),
        ],
        compiler_params=pltpu.CompilerParams(
            dimension_semantics=("parallel", "parallel", "arbitrary"), vmem_limit_bytes=VMEM_LIMIT),
        name="diffattn",
    )(dq, dk, dvt, lam_q1, lam_k1, lam_q2, lam_k2, gain, bias)


def _route(logits):
    r = [logits[g:g + 1, :] for g in range(N_GROUPS)]
    gmax = jnp.maximum(jnp.maximum(r[0], r[1]), jnp.maximum(r[2], r[3]))
    g_idx = jnp.where(r[0] == gmax, 0, jnp.where(r[1] == gmax, 1, jnp.where(r[2] == gmax, 2, 3)))
    denom = sum(jnp.exp(rg - gmax) for rg in r)
    g_weight = 1.0 / denom
    sel = jnp.zeros((EXPERTS_PER_GROUP, logits.shape[1]), F32)
    for g in range(N_GROUPS):
        rows = logits[8 + g * EXPERTS_PER_GROUP:8 + (g + 1) * EXPERTS_PER_GROUP, :]
        sel = jnp.where(g_idx == g, rows, sel)
    eidx = lax.broadcasted_iota(jnp.int32, sel.shape, 0)
    v1 = jnp.max(sel, axis=0, keepdims=True)
    i1 = jnp.min(jnp.where(sel == v1, eidx, EXPERTS_PER_GROUP), axis=0, keepdims=True)
    sel2 = jnp.where(eidx == i1, -jnp.inf, sel)
    v2 = jnp.max(sel2, axis=0, keepdims=True)
    i2 = jnp.min(jnp.where(sel2 == v2, eidx, EXPERTS_PER_GROUP), axis=0, keepdims=True)
    e2 = jnp.exp(v2 - v1)
    w1 = g_weight / (1.0 + e2)
    w2 = g_weight * e2 / (1.0 + e2)
    return g_idx, i1, i2, w1, w2


def _outproj_kernel(ret_ref, diff_ref, x_ref, mod_ref, gain_ref, wo_ref, wr_hi_ref, wr_lo_ref, br_ref, tri_ref,
                    x1_ref, h2_ref, ri_ref, rw_ref, cnt_ref, *, tiles_per_batch):
    b = pl.program_id(0) // tiles_per_batch
    mix = _dot(ret_ref[...], wo_ref[0:RET_WIDTH, :]) + _dot(diff_ref[...], wo_ref[RET_WIDTH:, :])
    gate1 = mod_ref[pl.ds(b, 1), 2 * D_MODEL:3 * D_MODEL]
    x1 = x_ref[...] + gate1 * mix
    x1_ref[...] = x1
    shift = mod_ref[pl.ds(b, 1), 3 * D_MODEL:4 * D_MODEL]
    scale = mod_ref[pl.ds(b, 1), 4 * D_MODEL:5 * D_MODEL]
    h2 = _norm_modulate(x1, gain_ref[...], shift, scale)
    h_hi, h_lo = _split_bf16(h2)
    h2_ref[...] = h_hi
    wr_hi = wr_hi_ref[...]
    logits = _dot_nt(wr_hi, h_hi) + _dot_nt(wr_lo_ref[...], h_hi) + _dot_nt(wr_hi, h_lo) + br_ref[...]
    g_idx, i1, i2, w1, w2 = _route(logits)
    e1 = g_idx * EXPERTS_PER_GROUP + i1
    e2 = g_idx * EXPERTS_PER_GROUP + i2
    eidx = lax.broadcasted_iota(jnp.int32, (N_EXPERTS, logits.shape[1]), 0)
    hit1 = eidx == e1
    hit2 = eidx == e2
    onehot = jnp.where(hit1 | hit2, 1.0, 0.0)
    before = _dot(onehot.astype(BF16), tri_ref[...])
    r1 = jnp.sum(jnp.where(hit1, before, 0.0), axis=0, keepdims=True)
    r2 = jnp.sum(jnp.where(hit2, before, 0.0), axis=0, keepdims=True)
    zi = jnp.zeros_like(e1)
    ri_ref[...] = jnp.concatenate([e1, e2, r1.astype(jnp.int32), r2.astype(jnp.int32), zi, zi, zi, zi], axis=0)
    zf = jnp.zeros_like(w1)
    rw_ref[...] = jnp.concatenate([w1, w2, zf, zf, zf, zf, zf, zf], axis=0)
    counts = jnp.sum(onehot, axis=1, keepdims=True)
    cnt_ref[0] = jnp.broadcast_to(counts, (N_EXPERTS, LANES)).astype(jnp.int32)


def _outproj(ret_out, diff_out, x2, mod, gain, w_out, wr_hi, wr_lo, br, seq, tm):
    tokens = x2.shape[0]
    tiles_per_batch = seq // tm
    n_tiles = tokens // tm
    tri = (jnp.arange(tm)[:, None] < jnp.arange(tm)[None, :]).astype(BF16)
    tok_spec = lambda w: pl.BlockSpec((tm, w), lambda i: (i, 0))
    row_spec = pl.BlockSpec((8, tm), lambda i: (0, i))
    full = lambda a: pl.BlockSpec(a.shape, lambda i: (0,) * a.ndim)
    return pl.pallas_call(
        functools.partial(_outproj_kernel, tiles_per_batch=tiles_per_batch),
        grid=(n_tiles,),
        in_specs=[tok_spec(RET_WIDTH), tok_spec(DIFF_WIDTH), tok_spec(D_MODEL), full(mod), full(gain),
                  full(w_out), full(wr_hi), full(wr_lo), full(br), full(tri)],
        out_specs=[tok_spec(D_MODEL), tok_spec(D_MODEL), row_spec, row_spec,
                   pl.BlockSpec((1, N_EXPERTS, LANES), lambda i: (i, 0, 0))],
        out_shape=[jax.ShapeDtypeStruct((tokens, D_MODEL), F32),
                   jax.ShapeDtypeStruct((tokens, D_MODEL), BF16),
                   jax.ShapeDtypeStruct((8, tokens), jnp.int32),
                   jax.ShapeDtypeStruct((8, tokens), F32),
                   jax.ShapeDtypeStruct((n_tiles, N_EXPERTS, LANES), jnp.int32)],
        compiler_params=pltpu.CompilerParams(
            dimension_semantics=("parallel",), vmem_limit_bytes=VMEM_LIMIT),
        name="outproj",
    )(ret_out, diff_out, x2, mod, gain, w_out, wr_hi, wr_lo, br, tri)


CHUNK = 8
TMX = 512


def _local_rows(tm):
    rows = 2 * tm + N_EXPERTS * (CHUNK - 1)
    return (rows + 15) // 16 * 16


def _sorted_rows_alloc(tokens, tm):
    worst = 2 * tokens + (tokens // tm) * N_EXPERTS * (CHUNK - 1) + N_EXPERTS * (TMX - CHUNK)
    return (worst + TMX - 1) // TMX * TMX


def _dispatch_plan(cnt, tokens, tm):
    i32 = jnp.int32
    nch_max = _local_rows(tm) // CHUNK
    pad = (cnt + CHUNK - 1) // CHUNK * CHUNK
    local_end = jnp.cumsum(pad, axis=1)
    local_start = local_end - pad
    seg_rows = jnp.sum(pad, axis=0)
    seg_pad = (seg_rows + TMX - 1) // TMX * TMX
    seg_end = jnp.cumsum(seg_pad)
    seg_start = seg_end - seg_pad
    run_dst = seg_start[None, :] + jnp.cumsum(pad, axis=0) - pad
    row = CHUNK * jnp.arange(nch_max, dtype=i32)[None, :, None]
    owns = (local_start[:, None, :] <= row) & (row < local_end[:, None, :])
    chunk_dst = row[:, :, 0] + jnp.sum(jnp.where(owns, (run_dst - local_start)[:, None, :], 0), axis=-1)
    m = TMX * jnp.arange(_sorted_rows_alloc(tokens, tm) // TMX, dtype=i32)
    tile_expert = jnp.minimum(jnp.sum(seg_end[None, :] <= m[:, None], axis=-1), N_EXPERTS - 1)
    towns = (seg_start[None, :] <= m[:, None]) & (m[:, None] < seg_end[None, :])
    used = seg_pad > 0
    parity = (jnp.cumsum(used) - used) % 2
    eids = jnp.arange(N_EXPERTS, dtype=i32)
    later_used = (eids[None, :] > eids[:, None]) & used[None, :]
    next_used = jnp.min(jnp.where(later_used, eids[None, :], N_EXPERTS), axis=1)
    next_used = jnp.where(next_used == N_EXPERTS, -1, next_used)
    pick = lambda per_expert: jnp.sum(jnp.where(towns, per_expert[None, :], 0), axis=-1)
    tile_first = jnp.sum(jnp.where(towns & (seg_start[None, :] == m[:, None]), 1, 0), axis=-1)
    tile_next = jnp.where(jnp.any(towns, axis=-1), pick(next_used), -1)
    return dict(
        tile_first=tile_first.astype(i32),
        tile_slot=pick(parity).astype(i32),
        tile_next=tile_next.astype(i32),
        local_start=local_start.reshape(-1).astype(i32),
        n_chunks=(local_end[:, -1] // CHUNK).astype(i32),
        chunk_dst=chunk_dst.reshape(-1).astype(i32),
        tail_base=(seg_start + seg_rows).astype(i32),
        tail_rows=(seg_pad - seg_rows).astype(i32),
        tile_expert=tile_expert.astype(i32),
        n_used=(seg_end[-1:] // TMX).astype(i32),
    )


def _local_slots(ri_ref, local_start_ref, tile):
    e1, e2 = ri_ref[0:1, :], ri_ref[1:2, :]
    s1, s2 = ri_ref[2:3, :], ri_ref[3:4, :]
    for e in range(N_EXPERTS):
        start = local_start_ref[tile * N_EXPERTS + e]
        s1 = s1 + jnp.where(e1 == e, start, 0)
        s2 = s2 + jnp.where(e2 == e, start, 0)
    return s1, s2


def _dispatch_kernel(local_start_ref, n_chunks_ref, chunk_dst_ref, tail_base_ref, tail_rows_ref, n_used_ref,
                     h_ref, ri_ref, xs_ref, buf_ref, zero_ref, sem_ref, tail_sem_ref, *, r_loc):
    b = pl.program_id(0)
    nb = pl.num_programs(0)
    slot = b % 2
    nch_max = r_loc // CHUNK

    def chunk_copy(tile, sl, j):
        row = pl.multiple_of(j * CHUNK, CHUNK)
        dst = pl.multiple_of(chunk_dst_ref[tile * nch_max + j], CHUNK)
        return pltpu.make_async_copy(buf_ref.at[sl, pl.ds(row, CHUNK), :], xs_ref.at[pl.ds(dst, CHUNK), :],
                                     sem_ref.at[sl])

    def drain(tile, sl):
        lax.fori_loop(0, n_chunks_ref[tile], lambda j, c: (chunk_copy(tile, sl, j).wait(), c)[1], 0)

    @pl.when(b >= 2)
    def _():
        drain(b - 2, slot)

    s1, s2 = _local_slots(ri_ref, local_start_ref, b)
    rows = lax.broadcasted_iota(jnp.int32, (r_loc, s1.shape[1]), 0)
    perm = jnp.where((rows == s1) | (rows == s2), 1.0, 0.0).astype(BF16)
    buf_ref[slot] = _dot(perm, h_ref[...])
    lax.fori_loop(0, n_chunks_ref[b], lambda j, c: (chunk_copy(b, slot, j).start(), c)[1], 0)

    @pl.when(b == nb - 1)
    def _():
        zero_ref[...] = jnp.zeros_like(zero_ref)

        def tail_pieces(e, act):
            n = tail_rows_ref[e]
            size = TMX // 2
            while size >= CHUNK:
                dst = pl.multiple_of(tail_base_ref[e] + (n & (-2 * size)), CHUNK)
                cp = pltpu.make_async_copy(zero_ref.at[pl.ds(0, size), :], xs_ref.at[pl.ds(dst, size), :],
                                           tail_sem_ref.at[0])
                pl.when((n & size) != 0)(functools.partial(act, cp))
                size //= 2

        def unused_tile_copy(m):
            dst = pl.multiple_of(m * TMX, TMX)
            return pltpu.make_async_copy(zero_ref, xs_ref.at[pl.ds(dst, TMX), :], tail_sem_ref.at[1])

        n_alloc = xs_ref.shape[0] // TMX
        lax.fori_loop(0, N_EXPERTS, lambda e, c: (tail_pieces(e, lambda cp: cp.start()), c)[1], 0)
        lax.fori_loop(n_used_ref[0], n_alloc, lambda m, c: (unused_tile_copy(m).start(), c)[1], 0)
        lax.fori_loop(0, N_EXPERTS, lambda e, c: (tail_pieces(e, lambda cp: cp.wait()), c)[1], 0)
        lax.fori_loop(n_used_ref[0], n_alloc, lambda m, c: (unused_tile_copy(m).wait(), c)[1], 0)

        @pl.when(b >= 1)
        def _():
            drain(b - 1, 1 - slot)

        drain(b, slot)


def _dispatch(h2, ri, plan, tm):
    tokens = h2.shape[0]
    r_loc = _local_rows(tm)
    grid_spec = pltpu.PrefetchScalarGridSpec(
        num_scalar_prefetch=6,
        grid=(tokens // tm,),
        in_specs=[pl.BlockSpec((tm, D_MODEL), lambda i, *_: (i, 0)),
                  pl.BlockSpec((8, tm), lambda i, *_: (0, i))],
        out_specs=pl.BlockSpec(memory_space=pl.ANY),
        scratch_shapes=[pltpu.VMEM((2, r_loc, D_MODEL), F32), pltpu.VMEM((TMX, D_MODEL), F32),
                        pltpu.SemaphoreType.DMA((2,)), pltpu.SemaphoreType.DMA((2,))],
    )
    return pl.pallas_call(
        functools.partial(_dispatch_kernel, r_loc=r_loc),
        grid_spec=grid_spec,
        out_shape=jax.ShapeDtypeStruct((_sorted_rows_alloc(tokens, tm), D_MODEL), F32),
        compiler_params=pltpu.CompilerParams(
            dimension_semantics=("arbitrary",), vmem_limit_bytes=VMEM_LIMIT),
        name="dispatch",
    )(plan["local_start"], plan["n_chunks"], plan["chunk_dst"], plan["tail_base"], plan["tail_rows"], plan["n_used"],
      h2, ri)


def _experts_kernel(tile_expert_ref, n_used_ref, first_ref, slot_ref, next_ref, xs_ref, wg_hbm, wu_hbm, wd_hbm,
                    ys_ref, wg_st, wu_st, wd_st, wg_bf, wu_bf, wd_bf, sem_ref):
    m = pl.program_id(0)

    def weight_copies(e, s):
        return [pltpu.make_async_copy(src.at[e], dst.at[s], sem_ref.at[s, n])
                for n, (src, dst) in enumerate([(wg_hbm, wg_st), (wu_hbm, wu_st), (wd_hbm, wd_st)])]

    @pl.when(m < n_used_ref[0])
    def _():
        @pl.when(first_ref[m] == 1)
        def _():
            s = slot_ref[m]

            @pl.when(m == 0)
            def _():
                for cp in weight_copies(tile_expert_ref[0], 0):
                    cp.start()

            for cp in weight_copies(tile_expert_ref[m], s):
                cp.wait()

            @pl.when(next_ref[m] >= 0)
            def _():
                for cp in weight_copies(next_ref[m], 1 - s):
                    cp.start()

            wg_bf[...] = wg_st[s].astype(BF16)
            wu_bf[...] = wu_st[s].astype(BF16)
            wd_bf[...] = wd_st[s].astype(BF16)

        x = xs_ref[...].astype(BF16)
        a = _dot(x, wg_bf[...])
        u = _dot(x, wu_bf[...])
        hid = (_silu(a) * u).astype(BF16)
        ys_ref[...] = _dot(hid, wd_bf[...])

    @pl.when(m >= n_used_ref[0])
    def _():
        ys_ref[...] = jnp.zeros_like(ys_ref)


def _experts(xs, plan, wg, wu, wd):
    n_tiles = xs.shape[0] // TMX
    last_used = lambda m, n_used: jnp.minimum(m, n_used[0] - 1)
    row_spec = pl.BlockSpec((TMX, D_MODEL), lambda m, te, nu, *_: (last_used(m, nu), 0))
    out_spec = pl.BlockSpec((TMX, D_MODEL), lambda m, *_: (m, 0))
    hbm = pl.BlockSpec(memory_space=pl.ANY)
    up_shape, down_shape = (D_MODEL, D_EXPERT), (D_EXPERT, D_MODEL)
    grid_spec = pltpu.PrefetchScalarGridSpec(
        num_scalar_prefetch=5,
        grid=(n_tiles,),
        in_specs=[row_spec, hbm, hbm, hbm],
        out_specs=out_spec,
        scratch_shapes=[pltpu.VMEM((2,) + up_shape, F32), pltpu.VMEM((2,) + up_shape, F32),
                        pltpu.VMEM((2,) + down_shape, F32),
                        pltpu.VMEM(up_shape, BF16), pltpu.VMEM(up_shape, BF16), pltpu.VMEM(down_shape, BF16),
                        pltpu.SemaphoreType.DMA((2, 3))],
    )
    return pl.pallas_call(
        _experts_kernel,
        grid_spec=grid_spec,
        out_shape=jax.ShapeDtypeStruct(xs.shape, F32),
        compiler_params=pltpu.CompilerParams(
            dimension_semantics=("arbitrary",), vmem_limit_bytes=VMEM_LIMIT),
        name="experts",
    )(plan["tile_expert"], plan["n_used"], plan["tile_first"], plan["tile_slot"], plan["tile_next"], xs, wg, wu, wd)


def _combine_kernel(local_start_ref, n_chunks_ref, chunk_dst_ref, ys_ref, ri_ref, rw_ref, x1_ref, mod_ref,
                    gain_ref, o_ref, buf_ref, sem_ref, *, r_loc, tiles_per_batch):
    b = pl.program_id(0)
    nb = pl.num_programs(0)
    slot = b % 2
    nch_max = r_loc // CHUNK

    def chunk_copy(tile, sl, j):
        row = pl.multiple_of(j * CHUNK, CHUNK)
        src = pl.multiple_of(chunk_dst_ref[tile * nch_max + j], CHUNK)
        return pltpu.make_async_copy(ys_ref.at[pl.ds(src, CHUNK), :], buf_ref.at[sl, pl.ds(row, CHUNK), :],
                                     sem_ref.at[sl])

    def fetch(tile, sl):
        lax.fori_loop(0, n_chunks_ref[tile], lambda j, c: (chunk_copy(tile, sl, j).start(), c)[1], 0)

    @pl.when(b == 0)
    def _():
        buf_ref[...] = jnp.zeros_like(buf_ref)
        fetch(0, 0)

    @pl.when(b + 1 < nb)
    def _():
        fetch(b + 1, 1 - slot)

    lax.fori_loop(0, n_chunks_ref[b], lambda j, c: (chunk_copy(b, slot, j).wait(), c)[1], 0)

    s1, s2 = _local_slots(ri_ref, local_start_ref, b)
    rows = lax.broadcasted_iota(jnp.int32, (r_loc, s1.shape[1]), 0)
    hit1 = rows == s1
    hit2 = rows == s2
    w_row = jnp.sum(jnp.where(hit1, rw_ref[0:1, :], jnp.where(hit2, rw_ref[1:2, :], 0.0)), axis=1, keepdims=True)
    perm = jnp.where(hit1 | hit2, 1.0, 0.0).astype(BF16)
    yw = (buf_ref[slot] * w_row).astype(BF16)
    moe = _dot_tn(perm, yw)
    batch = b // tiles_per_batch
    gate2 = mod_ref[pl.ds(batch, 1), 5 * D_MODEL:6 * D_MODEL]
    x2 = x1_ref[...] + gate2 * moe
    ms = jnp.mean(x2 * x2, axis=-1, keepdims=True)
    o_ref[...] = x2 * lax.rsqrt(ms + EPS) * gain_ref[...]


def _combine(ys, ri, rw, x1, mod, gain, plan, seq, tm):
    tokens = x1.shape[0]
    r_loc = _local_rows(tm)
    row_spec = pl.BlockSpec((8, tm), lambda i, *_: (0, i))
    tok_spec = pl.BlockSpec((tm, D_MODEL), lambda i, *_: (i, 0))
    full = lambda a: pl.BlockSpec(a.shape, lambda i, *_: (0,) * a.ndim)
    grid_spec = pltpu.PrefetchScalarGridSpec(
        num_scalar_prefetch=3,
        grid=(tokens // tm,),
        in_specs=[pl.BlockSpec(memory_space=pl.ANY), row_spec, row_spec, tok_spec, full(mod), full(gain)],
        out_specs=tok_spec,
        scratch_shapes=[pltpu.VMEM((2, r_loc, D_MODEL), F32), pltpu.SemaphoreType.DMA((2,))],
    )
    return pl.pallas_call(
        functools.partial(_combine_kernel, r_loc=r_loc, tiles_per_batch=seq // tm),
        grid_spec=grid_spec,
        out_shape=jax.ShapeDtypeStruct((tokens, D_MODEL), F32),
        compiler_params=pltpu.CompilerParams(
            dimension_semantics=("arbitrary",), vmem_limit_bytes=VMEM_LIMIT),
        name="combine",
    )(plan["local_start"], plan["n_chunks"], plan["chunk_dst"], ys, ri, rw, x1, mod, gain)


def _rotary_tables(seq):
    half = RET_HEAD_DIM // 2
    inv_freq = 1.0 / (ROPE_BASE ** (jnp.arange(half, dtype=F32) / half))
    ang = jnp.arange(seq).astype(F32)[:, None] * inv_freq[None, :]
    cos = jnp.cos(ang)
    sin = jnp.sin(ang)
    return jnp.tile(cos, (1, 4)), jnp.concatenate([-sin, sin, -sin, sin], axis=1)


def _pick_tile(n, pref):
    t = min(n, pref)
    assert n % t == 0, (n, t)
    return t


def kernel(x, c, ada_w, ada_b, norm1_gain, norm2_gain, w_in, w_out, ret_gn_gain, lam_q1, lam_k1, lam_q2,
           lam_k2, diff_subln_gain, w_group, b_group, w_expert, b_expert, w_gate, w_up, w_down, final_gain):
    batch, seq, d = x.shape
    assert d == D_MODEL and batch <= 8 and ada_w.shape[0] == 1
    layer = 0
    lambda_init = 0.8 - 0.6 * math.exp(-0.3 * layer)
    tokens = batch * seq
    x2 = x.reshape(tokens, d)
    tm = _pick_tile(seq, 512)

    c_pad = jnp.zeros((8, d), F32).at[:batch].set(c)
    mod = _adaln(c_pad, ada_w[layer], ada_b[layer].reshape(1, -1))

    cos_t, sin_t = _rotary_tables(seq)
    rq, rk, rv, rg, dq, dk, dvt = _inproj(
        x2, mod, norm1_gain[layer].reshape(1, d), w_in[layer].astype(BF16), cos_t, sin_t, seq, tm)

    ret_out = _retention(rq, rk, rv, rg, ret_gn_gain[layer].reshape(1, RET_WIDTH), batch, seq,
                         _pick_tile(seq, 128))
    diff_out = _diffattn(
        dq, dk, dvt, lam_q1[layer].reshape(1, -1), lam_k1[layer].reshape(1, -1), lam_q2[layer].reshape(1, -1),
        lam_k2[layer].reshape(1, -1), diff_subln_gain[layer].reshape(1, -1), batch, seq, lambda_init,
        _pick_tile(seq, 1024), 2 * QUERY_CHUNK)

    w_router = jnp.concatenate(
        [w_group[layer].T, jnp.zeros((8 - N_GROUPS, d), F32), w_expert[layer].reshape(d, N_EXPERTS).T], axis=0)
    b_router = jnp.concatenate(
        [b_group[layer], jnp.zeros((8 - N_GROUPS,), F32), b_expert[layer].reshape(N_EXPERTS)]).reshape(-1, 1)
    wr_hi = w_router.astype(BF16)
    wr_lo = (w_router - wr_hi.astype(F32)).astype(BF16)
    x1, h2, ri, rw, cnt = _outproj(ret_out, diff_out, x2, mod, norm2_gain[layer].reshape(1, d),
                                   w_out[layer].astype(BF16), wr_hi, wr_lo, b_router, seq, tm)

    plan = _dispatch_plan(cnt[:, :, 0], tokens, tm)
    xs = _dispatch(h2, ri, plan, tm)
    ys = _experts(xs, plan, w_gate[layer].reshape(N_EXPERTS, d, D_EXPERT),
                  w_up[layer].reshape(N_EXPERTS, d, D_EXPERT), w_down[layer].reshape(N_EXPERTS, D_EXPERT, d))
    out = _combine(ys, ri, rw, x1, mod, final_gain.reshape(1, d), plan, seq, tm)
    return out.reshape(batch, seq, d)
```

```python
import functools
import math

import jax
import jax.numpy as jnp
from jax import lax
from jax.experimental import pallas as pl
from jax.experimental.pallas import tpu as pltpu

F32 = jnp.float32
BF16 = jnp.bfloat16

D_MODEL = 1024
RET_HEAD_DIM = 64
RET_WIDTH = 512
RET_HEADS = 8
RET_PAIRS = RET_HEADS // 2
DIFF_QK_DIM = 64
DIFF_V_DIM = 128
DIFF_HEADS = 4
DIFF_WIDTH = 512
N_GROUPS = 4
EXPERTS_PER_GROUP = 8
N_EXPERTS = N_GROUPS * EXPERTS_PER_GROUP
D_EXPERT = 512
N_MOD = 6
ROPE_BASE = 10000.0
EPS = 1e-6
LANES = 128
ROUTER_ROWS = 8 + N_EXPERTS
VMEM_LIMIT = 56 * 1024 * 1024


def _dot(a, b):
    return jnp.dot(a, b, preferred_element_type=F32)


def _dot_nt(a, b):
    return lax.dot_general(a, b, (((1,), (1,)), ((), ())), preferred_element_type=F32)


def _dot_tn(a, b):
    return lax.dot_general(a, b, (((0,), (0,)), ((), ())), preferred_element_type=F32)


def _split_bf16(x):
    hi = x.astype(BF16)
    lo = (x - hi.astype(F32)).astype(BF16)
    return hi, lo


def _silu(x):
    return x / (1.0 + jnp.exp(-x))


def _adaln_kernel(c_ref, w_ref, b_ref, o_ref):
    ca = _silu(c_ref[...])
    c_hi, c_lo = _split_bf16(ca)
    w_hi, w_lo = _split_bf16(w_ref[...])
    o_ref[...] = _dot(c_hi, w_hi) + _dot(c_lo, w_hi) + _dot(c_hi, w_lo) + b_ref[...]


def _adaln(c_pad, ada_w, ada_b):
    n_out = ada_w.shape[1]
    tn = D_MODEL
    return pl.pallas_call(
        _adaln_kernel,
        grid=(n_out // tn,),
        in_specs=[
            pl.BlockSpec((8, D_MODEL), lambda j: (0, 0)),
            pl.BlockSpec((D_MODEL, tn), lambda j: (0, j)),
            pl.BlockSpec((1, tn), lambda j: (0, j)),
        ],
        out_specs=pl.BlockSpec((8, tn), lambda j: (0, j)),
        out_shape=jax.ShapeDtypeStruct((8, n_out), F32),
        compiler_params=pltpu.CompilerParams(vmem_limit_bytes=VMEM_LIMIT),
        name="adaln",
    )(c_pad, ada_w, ada_b)


def _norm_modulate(x, gain, shift, scale):
    ms = jnp.mean(x * x, axis=-1, keepdims=True)
    y = x * lax.rsqrt(ms + EPS) * gain
    return y * (1.0 + scale) + shift


def _rotary_slab(x, cos, sin_signed, lane_lo):
    swapped = jnp.where(lane_lo, pltpu.roll(x, 96, 1), pltpu.roll(x, 32, 1))
    return x * cos + swapped * sin_signed


def _inproj_kernel(x_ref, mod_ref, gain_ref, w_ref, cos_ref, sin_ref,
                   rq_ref, rk_ref, rv_ref, rg_ref, dq_ref, dk_ref, dvt_ref, *, tiles_per_batch):
    b = pl.program_id(0) // tiles_per_batch
    shift = mod_ref[pl.ds(b, 1), 0:D_MODEL]
    scale = mod_ref[pl.ds(b, 1), D_MODEL:2 * D_MODEL]
    h = _norm_modulate(x_ref[...], gain_ref[...], shift, scale).astype(BF16)
    cos = cos_ref[...]
    sin = sin_ref[...]
    lane = lax.broadcasted_iota(jnp.int32, cos.shape, 1)
    lane_lo = (lane % 64) < 32

    def proj(chunk):
        return _dot(h, w_ref[:, chunk * RET_WIDTH:(chunk + 1) * RET_WIDTH])

    def rotary(acc, out_ref, post_scale):
        for s in range(RET_WIDTH // LANES):
            sl = slice(s * LANES, (s + 1) * LANES)
            out_ref[:, sl] = (_rotary_slab(acc[:, sl], cos, sin, lane_lo) * post_scale).astype(BF16)

    rotary(proj(0), rq_ref, 1.0)
    rotary(proj(1), rk_ref, RET_HEAD_DIM ** -0.5)
    rv_ref[...] = proj(2).astype(BF16)
    rg_ref[...] = _silu(proj(3)).astype(BF16)
    dq_ref[...] = (proj(4) * (DIFF_QK_DIM ** -0.5 * math.log2(math.e))).astype(BF16)
    dk_ref[...] = proj(5).astype(BF16)
    dvt_ref[...] = proj(6).T.astype(BF16)


def _inproj(x2, mod, gain, w_in, cos_t, sin_t, seq, tm):
    tokens = x2.shape[0]
    tiles_per_batch = seq // tm
    tok_spec = lambda w: pl.BlockSpec((tm, w), lambda i: (i, 0))
    tab_spec = pl.BlockSpec((tm, LANES), lambda i: (i % tiles_per_batch, 0))
    full = lambda a: pl.BlockSpec(a.shape, lambda i: (0,) * a.ndim)
    out = jax.ShapeDtypeStruct((tokens, RET_WIDTH), BF16)
    return pl.pallas_call(
        functools.partial(_inproj_kernel, tiles_per_batch=tiles_per_batch),
        grid=(tokens // tm,),
        in_specs=[tok_spec(D_MODEL), full(mod), full(gain), full(w_in), tab_spec, tab_spec],
        out_specs=[tok_spec(RET_WIDTH)] * 6 + [pl.BlockSpec((DIFF_WIDTH, tm), lambda i: (0, i))],
        out_shape=[out] * 6 + [jax.ShapeDtypeStruct((DIFF_WIDTH, tokens), BF16)],
        compiler_params=pltpu.CompilerParams(
            dimension_semantics=("parallel",), vmem_limit_bytes=VMEM_LIMIT),
        name="inproj",
    )(x2, mod, gain, w_in, cos_t, sin_t)


def _retention_kernel(q_ref, k_ref, v_ref, g_ref, dec_ref, qdec_ref, kdec_ref, rdec_ref,
                      bmask_ref, gmean_ref, gain_ref, o_ref, state_ref, *, chunk):
    @pl.when(pl.program_id(1) == 0)
    def _():
        state_ref[...] = jnp.zeros_like(state_ref)

    lane = lax.broadcasted_iota(jnp.int32, (chunk, LANES), 1)
    first_head = lane < RET_HEAD_DIM
    gmean = gmean_ref[...]
    bmask = bmask_ref[...]
    pairs = range(RET_PAIRS)
    sl = [slice(p * LANES, (p + 1) * LANES) for p in pairs]
    q = [q_ref[:, sl[p]] for p in pairs]
    k = [k_ref[:, sl[p]] for p in pairs]
    v = [v_ref[:, sl[p]] for p in pairs]
    zero = jnp.zeros_like(q[0])
    q_stack = [jnp.concatenate([jnp.where(first_head, q[p], zero), jnp.where(first_head, zero, q[p])], axis=0)
               for p in pairs]
    scores = [(_dot_nt(q_stack[p], k[p]) * dec_ref[p]).astype(BF16) for p in pairs]
    state = [state_ref[p] for p in pairs]
    cross = [_dot(q[p], state[p].astype(BF16)) * qdec_ref[:, sl[p]] for p in pairs]
    k_dec = [(k[p].astype(F32) * kdec_ref[:, sl[p]]).astype(BF16) for p in pairs]
    for p in pairs:
        state_ref[p] = state[p] * rdec_ref[p] + _dot_tn(k_dec[p], v[p]) * bmask
    intra2 = [_dot(scores[p], v[p]) for p in pairs]
    y = [jnp.where(first_head, intra2[p][:chunk], intra2[p][chunk:]) + cross[p] for p in pairs]
    y_split = [_split_bf16(y[p]) for p in pairs]
    mu = [_dot(y_split[p][0], gmean) + _dot(y_split[p][1], gmean) for p in pairs]
    d = [y[p] - mu[p] for p in pairs]
    d_split = [_split_bf16(d[p] * d[p]) for p in pairs]
    var = [_dot(d_split[p][0], gmean) + _dot(d_split[p][1], gmean) for p in pairs]
    for p in pairs:
        yn = d[p] * lax.rsqrt(var[p] + EPS) * gain_ref[:, sl[p]]
        o_ref[:, sl[p]] = (g_ref[:, sl[p]].astype(F32) * yn).astype(BF16)


def _retention_tables(chunk):
    heads = jnp.arange(RET_HEADS, dtype=F32)
    log_gamma = jnp.log(1.0 - jnp.exp2(-5.0 - heads))
    idx = jnp.arange(chunk)
    rel = (idx[:, None] - idx[None, :]).astype(F32)
    decay = jnp.where(rel[None] >= 0, jnp.exp(log_gamma[:, None, None] * jnp.maximum(rel, 0.0)[None]), 0.0)
    dec2 = decay.reshape(RET_PAIRS, 2 * chunk, chunk)
    lane_lg = jnp.repeat(log_gamma, RET_HEAD_DIM)
    qdec = jnp.exp(lane_lg[None, :] * (idx + 1).astype(F32)[:, None])
    kdec = jnp.exp(lane_lg[None, :] * (chunk - 1 - idx).astype(F32)[:, None])
    rdec = jnp.exp(lane_lg * chunk).reshape(RET_PAIRS, LANES, 1) * jnp.ones((1, 1, LANES), F32)
    blk = jnp.arange(LANES) // RET_HEAD_DIM
    bmask = (blk[:, None] == blk[None, :]).astype(F32)
    gmean = (bmask / RET_HEAD_DIM).astype(BF16)
    return dec2, qdec, kdec, rdec, bmask, gmean


def _retention(rq, rk, rv, rg, gn_gain, batch, seq, chunk):
    nc = seq // chunk
    dec2, qdec, kdec, rdec, bmask, gmean = _retention_tables(chunk)
    tok_spec = pl.BlockSpec((chunk, RET_WIDTH), lambda b, n: (b * nc + n, 0))
    full = lambda a: pl.BlockSpec(a.shape, lambda b, n: (0,) * a.ndim)
    return pl.pallas_call(
        functools.partial(_retention_kernel, chunk=chunk),
        grid=(batch, nc),
        in_specs=[tok_spec] * 4 + [full(dec2), full(qdec), full(kdec), full(rdec), full(bmask),
                                   full(gmean), full(gn_gain)],
        out_specs=tok_spec,
        out_shape=jax.ShapeDtypeStruct(rq.shape, BF16),
        scratch_shapes=[pltpu.VMEM((RET_PAIRS, LANES, LANES), F32)],
        compiler_params=pltpu.CompilerParams(
            dimension_semantics=("parallel", "arbitrary"), vmem_limit_bytes=VMEM_LIMIT),
        name="retention",
    )(rq, rk, rv, rg, dec2, qdec, kdec, rdec, bmask, gmean, gn_gain)


NEG_BIG = -1e30


V_EXT_ROWS = DIFF_V_DIM + 16
QUERY_CHUNK = 256
SCORES_AHEAD_FULL = 3
SCORES_AHEAD_DIAG = 1


def _diag_chunks(tq, tk, d):
    assert tk == 2 * QUERY_CHUNK
    per_softmax = tq // QUERY_CHUNK
    out = []
    for c in range(2 * per_softmax):
        q0 = (c % per_softmax) * QUERY_CHUNK
        if q0 + QUERY_CHUNK - 1 < d * tk:
            continue
        kind = "full" if q0 >= (d + 1) * tk else ("tri" if q0 == d * tk else "low_tri")
        out.append((c, kind))
    return out


def _accumulate(acc_ref, cs, alpha, pv):
    acc_ref[:, cs] = alpha * acc_ref[:, cs] + pv


def _diffattn_kernel(q_ref, k_ref, vt_ref, lq1_ref, lk1_ref, lq2_ref, lk2_ref, gain_ref, bias_ref, o_ref,
                     qs_ref, vext_ref, m_ref, acc_ref, *, tq, tk, lambda_init):
    i = pl.program_id(2)
    nk = vext_ref.shape[0]

    @pl.when(i == 0)
    def _():
        for j in range(nk):
            vext_ref[j, 0:DIFF_V_DIM, :] = vt_ref[:, j * tk:(j + 1) * tk]
            vext_ref[j, DIFF_V_DIM:V_EXT_ROWS, :] = jnp.ones((V_EXT_ROWS - DIFF_V_DIM, tk), BF16)

    q = q_ref[...]
    lane = lax.broadcasted_iota(jnp.int32, q.shape, 1)
    zero = jnp.zeros_like(q)
    qs_ref[0:tq, :] = jnp.where(lane < DIFF_QK_DIM, q, zero)
    qs_ref[tq:2 * tq, :] = jnp.where(lane < DIFF_QK_DIM, zero, q)
    m_ref[...] = jnp.full_like(m_ref, NEG_BIG)
    acc_ref[...] = jnp.zeros_like(acc_ref)

    def step(j, chunks, n_ahead):
        start = pl.multiple_of(j * tk, tk)
        chunk = lambda c: slice(c * QUERY_CHUNK, (c + 1) * QUERY_CHUNK)

        def scores(c, kind):
            n_keys = QUERY_CHUNK if kind == "tri" else tk
            return _dot_nt(k_ref[pl.ds(start, n_keys), :], qs_ref[chunk(c), :])

        ahead = [scores(*chunks[n]) for n in range(min(n_ahead, len(chunks)))]
        pending = None
        for n, (c, kind) in enumerate(chunks):
            cs = chunk(c)
            st = ahead.pop(0)
            if n + n_ahead < len(chunks):
                ahead.append(scores(*chunks[n + n_ahead]))
            if kind == "tri":
                st = st + bias_ref[QUERY_CHUNK:, :]
            elif kind == "low_tri":
                st = st + bias_ref[...]
            m_old = m_ref[:, cs]
            m_new = jnp.maximum(m_old, jnp.max(st, axis=0, keepdims=True))
            alpha = jnp.exp2(m_old - m_new)
            p = jnp.exp2(st - m_new).astype(BF16)
            m_ref[:, cs] = m_new
            pv = _dot(vext_ref[j, :, 0:st.shape[0]], p)
            if pending is not None:
                pending()
            pending = functools.partial(_accumulate, acc_ref, cs, alpha, pv)
        pending()

    tiles_per_q = tq // tk
    all_chunks = [(c, "full") for c in range(2 * tq // QUERY_CHUNK)]
    lax.fori_loop(0, i * tiles_per_q, lambda j, c: (step(j, all_chunks, SCORES_AHEAD_FULL), c)[1], 0)
    for d in range(tiles_per_q):
        step(i * tiles_per_q + d, _diag_chunks(tq, tk, d), SCORES_AHEAD_DIAG)

    lam = (jnp.exp(jnp.sum(lq1_ref[...] * lk1_ref[...], axis=-1, keepdims=True))
           - jnp.exp(jnp.sum(lq2_ref[...] * lk2_ref[...], axis=-1, keepdims=True)) + lambda_init)
    acc = acc_ref[...]
    o2 = acc[0:DIFF_V_DIM, :] * (1.0 / acc[DIFF_V_DIM:DIFF_V_DIM + 1, :])
    o = (o2[:, :tq] - lam * o2[:, tq:]).T
    ms = jnp.mean(o * o, axis=-1, keepdims=True)
    o = o * lax.rsqrt(ms + EPS) * gain_ref[...] * (1.0 - lambda_init)
    o_ref[...] = o.astype(BF16)


def _diffattn(dq, dk, dvt, lam_q1, lam_k1, lam_q2, lam_k2, gain, batch, seq, lambda_init, tq, tk):
    nq = seq // tq
    q_spec = pl.BlockSpec((tq, LANES), lambda b, h, i: (b * nq + i, h))
    k_spec = pl.BlockSpec((seq, LANES), lambda b, h, i: (b, h))
    vt_spec = pl.BlockSpec((DIFF_V_DIM, seq), lambda b, h, i: (h, b))
    vec = lambda a: pl.BlockSpec(a.shape, lambda b, h, i: (0, 0))
    key = jnp.arange(tk)[:, None]
    query = QUERY_CHUNK + jnp.arange(QUERY_CHUNK)[None, :]
    bias = jnp.where(key <= query, 0.0, NEG_BIG).astype(F32)
    return pl.pallas_call(
        functools.partial(_diffattn_kernel, tq=tq, tk=tk, lambda_init=lambda_init),
        grid=(batch, DIFF_HEADS, nq),
        in_specs=[q_spec, k_spec, vt_spec, vec(lam_q1), vec(lam_k1), vec(lam_q2), vec(lam_k2), vec(gain),
                  vec(bias)],
        out_specs=q_spec,
        out_shape=jax.ShapeDtypeStruct(dq.shape, BF16),
        scratch_shapes=[
            pltpu.VMEM((2 * tq, LANES), BF16),
            pltpu.VMEM((seq // tk, V_EXT_ROWS, tk), BF16),
            pltpu.VMEM((1, 2 * tq), F32),
            pltpu.VMEM((V_EXT_ROWS, 2 * tq), F32),
        ],
        compiler_params=pltpu.CompilerParams(
            dimension_semantics=("parallel", "parallel", "arbitrary"), vmem_limit_bytes=VMEM_LIMIT),
        name="diffattn",
    )(dq, dk, dvt, lam_q1, lam_k1, lam_q2, lam_k2, gain, bias)


def _route(logits):
    r = [logits[g:g + 1, :] for g in range(N_GROUPS)]
    gmax = jnp.maximum(jnp.maximum(r[0], r[1]), jnp.maximum(r[2], r[3]))
    g_idx = jnp.where(r[0] == gmax, 0, jnp.where(r[1] == gmax, 1, jnp.where(r[2] == gmax, 2, 3)))
    denom = sum(jnp.exp(rg - gmax) for rg in r)
    g_weight = 1.0 / denom
    sel = jnp.zeros((EXPERTS_PER_GROUP, logits.shape[1]), F32)
    for g in range(N_GROUPS):
        rows = logits[8 + g * EXPERTS_PER_GROUP:8 + (g + 1) * EXPERTS_PER_GROUP, :]
        sel = jnp.where(g_idx == g, rows, sel)
    eidx = lax.broadcasted_iota(jnp.int32, sel.shape, 0)
    v1 = jnp.max(sel, axis=0, keepdims=True)
    i1 = jnp.min(jnp.where(sel == v1, eidx, EXPERTS_PER_GROUP), axis=0, keepdims=True)
    sel2 = jnp.where(eidx == i1, -jnp.inf, sel)
    v2 = jnp.max(sel2, axis=0, keepdims=True)
    i2 = jnp.min(jnp.where(sel2 == v2, eidx, EXPERTS_PER_GROUP), axis=0, keepdims=True)
    e2 = jnp.exp(v2 - v1)
    w1 = g_weight / (1.0 + e2)
    w2 = g_weight * e2 / (1.0 + e2)
    return g_idx, i1, i2, w1, w2


def _outproj_kernel(ret_ref, diff_ref, x_ref, mod_ref, gain_ref, wo_ref, wr_hi_ref, wr_lo_ref, br_ref, tri_ref,
                    x1_ref, h2_ref, ri_ref, rw_ref, cnt_ref, *, tiles_per_batch):
    b = pl.program_id(0) // tiles_per_batch
    mix = _dot(ret_ref[...], wo_ref[0:RET_WIDTH, :]) + _dot(diff_ref[...], wo_ref[RET_WIDTH:, :])
    gate1 = mod_ref[pl.ds(b, 1), 2 * D_MODEL:3 * D_MODEL]
    x1 = x_ref[...] + gate1 * mix
    x1_ref[...] = x1
    shift = mod_ref[pl.ds(b, 1), 3 * D_MODEL:4 * D_MODEL]
    scale = mod_ref[pl.ds(b, 1), 4 * D_MODEL:5 * D_MODEL]
    h2 = _norm_modulate(x1, gain_ref[...], shift, scale)
    h_hi, h_lo = _split_bf16(h2)
    h2_ref[...] = h_hi
    wr_hi = wr_hi_ref[...]
    logits = _dot_nt(wr_hi, h_hi) + _dot_nt(wr_lo_ref[...], h_hi) + _dot_nt(wr_hi, h_lo) + br_ref[...]
    g_idx, i1, i2, w1, w2 = _route(logits)
    e1 = g_idx * EXPERTS_PER_GROUP + i1
    e2 = g_idx * EXPERTS_PER_GROUP + i2
    eidx = lax.broadcasted_iota(jnp.int32, (N_EXPERTS, logits.shape[1]), 0)
    hit1 = eidx == e1
    hit2 = eidx == e2
    onehot = jnp.where(hit1 | hit2, 1.0, 0.0)
    before = _dot(onehot.astype(BF16), tri_ref[...])
    r1 = jnp.sum(jnp.where(hit1, before, 0.0), axis=0, keepdims=True)
    r2 = jnp.sum(jnp.where(hit2, before, 0.0), axis=0, keepdims=True)
    zi = jnp.zeros_like(e1)
    ri_ref[...] = jnp.concatenate([e1, e2, r1.astype(jnp.int32), r2.astype(jnp.int32), zi, zi, zi, zi], axis=0)
    zf = jnp.zeros_like(w1)
    rw_ref[...] = jnp.concatenate([w1, w2, zf, zf, zf, zf, zf, zf], axis=0)
    counts = jnp.sum(onehot, axis=1, keepdims=True)
    cnt_ref[0] = jnp.broadcast_to(counts, (N_EXPERTS, LANES)).astype(jnp.int32)


def _outproj(ret_out, diff_out, x2, mod, gain, w_out, wr_hi, wr_lo, br, seq, tm):
    tokens = x2.shape[0]
    tiles_per_batch = seq // tm
    n_tiles = tokens // tm
    tri = (jnp.arange(tm)[:, None] < jnp.arange(tm)[None, :]).astype(BF16)
    tok_spec = lambda w: pl.BlockSpec((tm, w), lambda i: (i, 0))
    row_spec = pl.BlockSpec((8, tm), lambda i: (0, i))
    full = lambda a: pl.BlockSpec(a.shape, lambda i: (0,) * a.ndim)
    return pl.pallas_call(
        functools.partial(_outproj_kernel, tiles_per_batch=tiles_per_batch),
        grid=(n_tiles,),
        in_specs=[tok_spec(RET_WIDTH), tok_spec(DIFF_WIDTH), tok_spec(D_MODEL), full(mod), full(gain),
                  full(w_out), full(wr_hi), full(wr_lo), full(br), full(tri)],
        out_specs=[tok_spec(D_MODEL), tok_spec(D_MODEL), row_spec, row_spec,
                   pl.BlockSpec((1, N_EXPERTS, LANES), lambda i: (i, 0, 0))],
        out_shape=[jax.ShapeDtypeStruct((tokens, D_MODEL), F32),
                   jax.ShapeDtypeStruct((tokens, D_MODEL), BF16),
                   jax.ShapeDtypeStruct((8, tokens), jnp.int32),
                   jax.ShapeDtypeStruct((8, tokens), F32),
                   jax.ShapeDtypeStruct((n_tiles, N_EXPERTS, LANES), jnp.int32)],
        compiler_params=pltpu.CompilerParams(
            dimension_semantics=("parallel",), vmem_limit_bytes=VMEM_LIMIT),
        name="outproj",
    )(ret_out, diff_out, x2, mod, gain, w_out, wr_hi, wr_lo, br, tri)


CHUNK = 8
TMX = 512


def _local_rows(tm):
    rows = 2 * tm + N_EXPERTS * (CHUNK - 1)
    return (rows + 15) // 16 * 16


def _sorted_rows_alloc(tokens, tm):
    worst = 2 * tokens + (tokens // tm) * N_EXPERTS * (CHUNK - 1) + N_EXPERTS * (TMX - CHUNK)
    return (worst + TMX - 1) // TMX * TMX


def _dispatch_plan(cnt, tokens, tm):
    i32 = jnp.int32
    nch_max = _local_rows(tm) // CHUNK
    pad = (cnt + CHUNK - 1) // CHUNK * CHUNK
    local_end = jnp.cumsum(pad, axis=1)
    local_start = local_end - pad
    seg_rows = jnp.sum(pad, axis=0)
    seg_pad = (seg_rows + TMX - 1) // TMX * TMX
    seg_end = jnp.cumsum(seg_pad)
    seg_start = seg_end - seg_pad
    run_dst = seg_start[None, :] + jnp.cumsum(pad, axis=0) - pad
    row = CHUNK * jnp.arange(nch_max, dtype=i32)[None, :, None]
    owns = (local_start[:, None, :] <= row) & (row < local_end[:, None, :])
    chunk_dst = row[:, :, 0] + jnp.sum(jnp.where(owns, (run_dst - local_start)[:, None, :], 0), axis=-1)
    m = TMX * jnp.arange(_sorted_rows_alloc(tokens, tm) // TMX, dtype=i32)
    tile_expert = jnp.minimum(jnp.sum(seg_end[None, :] <= m[:, None], axis=-1), N_EXPERTS - 1)
    towns = (seg_start[None, :] <= m[:, None]) & (m[:, None] < seg_end[None, :])
    used = seg_pad > 0
    parity = (jnp.cumsum(used) - used) % 2
    eids = jnp.arange(N_EXPERTS, dtype=i32)
    later_used = (eids[None, :] > eids[:, None]) & used[None, :]
    next_used = jnp.min(jnp.where(later_used, eids[None, :], N_EXPERTS), axis=1)
    next_used = jnp.where(next_used == N_EXPERTS, -1, next_used)
    pick = lambda per_expert: jnp.sum(jnp.where(towns, per_expert[None, :], 0), axis=-1)
    tile_first = jnp.sum(jnp.where(towns & (seg_start[None, :] == m[:, None]), 1, 0), axis=-1)
    tile_next = jnp.where(jnp.any(towns, axis=-1), pick(next_used), -1)
    return dict(
        tile_first=tile_first.astype(i32),
        tile_slot=pick(parity).astype(i32),
        tile_next=tile_next.astype(i32),
        local_start=local_start.reshape(-1).astype(i32),
        n_chunks=(local_end[:, -1] // CHUNK).astype(i32),
        chunk_dst=chunk_dst.reshape(-1).astype(i32),
        tail_base=(seg_start + seg_rows).astype(i32),
        tail_rows=(seg_pad - seg_rows).astype(i32),
        tile_expert=tile_expert.astype(i32),
        n_used=(seg_end[-1:] // TMX).astype(i32),
    )


WAIT_UNROLL = 8


def _wait_times(copy, n):
    lax.fori_loop(0, n // WAIT_UNROLL, lambda i, c: ([copy.wait() for _ in range(WAIT_UNROLL)], c)[1], 0)
    lax.fori_loop(0, n % WAIT_UNROLL, lambda i, c: (copy.wait(), c)[1], 0)


def _for_each(n, body, unroll=4):
    main = n // unroll
    lax.fori_loop(0, main, lambda i, c: ([body(i * unroll + u) for u in range(unroll)], c)[1], 0)
    lax.fori_loop(main * unroll, n, lambda j, c: (body(j), c)[1], 0)


def _local_slots(ri_ref, local_start_ref, tile):
    e1, e2 = ri_ref[0:1, :], ri_ref[1:2, :]
    s1, s2 = ri_ref[2:3, :], ri_ref[3:4, :]
    for e in range(N_EXPERTS):
        start = local_start_ref[tile * N_EXPERTS + e]
        s1 = s1 + jnp.where(e1 == e, start, 0)
        s2 = s2 + jnp.where(e2 == e, start, 0)
    return s1, s2


def _dispatch_kernel(local_start_ref, n_chunks_ref, chunk_dst_ref, tail_base_ref, tail_rows_ref, n_used_ref,
                     h_ref, ri_ref, xs_ref, buf_ref, zero_ref, sem_ref, tail_sem_ref, *, r_loc):
    b = pl.program_id(0)
    nb = pl.num_programs(0)
    slot = b % 2
    nch_max = r_loc // CHUNK

    def chunk_copy(tile, sl, j):
        row = pl.multiple_of(j * CHUNK, CHUNK)
        dst = pl.multiple_of(chunk_dst_ref[tile * nch_max + j], CHUNK)
        return pltpu.make_async_copy(buf_ref.at[sl, pl.ds(row, CHUNK), :], xs_ref.at[pl.ds(dst, CHUNK), :],
                                     sem_ref.at[sl])

    def drain(tile, sl):
        one_chunk = pltpu.make_async_copy(buf_ref.at[sl, pl.ds(0, CHUNK), :], xs_ref.at[pl.ds(0, CHUNK), :],
                                          sem_ref.at[sl])
        _wait_times(one_chunk, n_chunks_ref[tile])

    @pl.when(b >= 2)
    def _():
        drain(b - 2, slot)

    s1, s2 = _local_slots(ri_ref, local_start_ref, b)
    rows = lax.broadcasted_iota(jnp.int32, (r_loc, s1.shape[1]), 0)
    perm = jnp.where((rows == s1) | (rows == s2), 1.0, 0.0).astype(BF16)
    buf_ref[slot] = _dot(perm, h_ref[...])
    _for_each(n_chunks_ref[b], lambda j: chunk_copy(b, slot, j).start())

    @pl.when(b == nb - 1)
    def _():
        zero_ref[...] = jnp.zeros_like(zero_ref)

        def tail_pieces(e, act):
            n = tail_rows_ref[e]
            size = TMX // 2
            while size >= CHUNK:
                dst = pl.multiple_of(tail_base_ref[e] + (n & (-2 * size)), CHUNK)
                cp = pltpu.make_async_copy(zero_ref.at[pl.ds(0, size), :], xs_ref.at[pl.ds(dst, size), :],
                                           tail_sem_ref.at[0])
                pl.when((n & size) != 0)(functools.partial(act, cp))
                size //= 2

        def unused_tile_copy(m):
            dst = pl.multiple_of(m * TMX, TMX)
            return pltpu.make_async_copy(zero_ref, xs_ref.at[pl.ds(dst, TMX), :], tail_sem_ref.at[1])

        n_alloc = xs_ref.shape[0] // TMX
        lax.fori_loop(0, N_EXPERTS, lambda e, c: (tail_pieces(e, lambda cp: cp.start()), c)[1], 0)
        lax.fori_loop(n_used_ref[0], n_alloc, lambda m, c: (unused_tile_copy(m).start(), c)[1], 0)
        lax.fori_loop(0, N_EXPERTS, lambda e, c: (tail_pieces(e, lambda cp: cp.wait()), c)[1], 0)
        lax.fori_loop(n_used_ref[0], n_alloc, lambda m, c: (unused_tile_copy(m).wait(), c)[1], 0)

        @pl.when(b >= 1)
        def _():
            drain(b - 1, 1 - slot)

        drain(b, slot)


def _dispatch(h2, ri, plan, tm):
    tokens = h2.shape[0]
    r_loc = _local_rows(tm)
    grid_spec = pltpu.PrefetchScalarGridSpec(
        num_scalar_prefetch=6,
        grid=(tokens // tm,),
        in_specs=[pl.BlockSpec((tm, D_MODEL), lambda i, *_: (i, 0)),
                  pl.BlockSpec((8, tm), lambda i, *_: (0, i))],
        out_specs=pl.BlockSpec(memory_space=pl.ANY),
        scratch_shapes=[pltpu.VMEM((2, r_loc, D_MODEL), F32), pltpu.VMEM((TMX, D_MODEL), F32),
                        pltpu.SemaphoreType.DMA((2,)), pltpu.SemaphoreType.DMA((2,))],
    )
    return pl.pallas_call(
        functools.partial(_dispatch_kernel, r_loc=r_loc),
        grid_spec=grid_spec,
        out_shape=jax.ShapeDtypeStruct((_sorted_rows_alloc(tokens, tm), D_MODEL), F32),
        compiler_params=pltpu.CompilerParams(
            dimension_semantics=("arbitrary",), vmem_limit_bytes=VMEM_LIMIT),
        name="dispatch",
    )(plan["local_start"], plan["n_chunks"], plan["chunk_dst"], plan["tail_base"], plan["tail_rows"], plan["n_used"],
      h2, ri)


def _experts_kernel(tile_expert_ref, n_used_ref, first_ref, slot_ref, next_ref, xs_ref, wg_hbm, wu_hbm, wd_hbm,
                    ys_ref, wg_st, wu_st, wd_st, wg_bf, wu_bf, wd_bf, sem_ref):
    m = pl.program_id(0)

    def weight_copies(e, s):
        return [pltpu.make_async_copy(src.at[e], dst.at[s], sem_ref.at[s, n])
                for n, (src, dst) in enumerate([(wg_hbm, wg_st), (wu_hbm, wu_st), (wd_hbm, wd_st)])]

    @pl.when(m < n_used_ref[0])
    def _():
        @pl.when(first_ref[m] == 1)
        def _():
            s = slot_ref[m]

            @pl.when(m == 0)
            def _():
                for cp in weight_copies(tile_expert_ref[0], 0):
                    cp.start()

            for cp in weight_copies(tile_expert_ref[m], s):
                cp.wait()

            @pl.when(next_ref[m] >= 0)
            def _():
                for cp in weight_copies(next_ref[m], 1 - s):
                    cp.start()

            wg_bf[...] = wg_st[s].astype(BF16)
            wu_bf[...] = wu_st[s].astype(BF16)
            wd_bf[...] = wd_st[s].astype(BF16)

        x = xs_ref[...].astype(BF16)
        a = _dot(x, wg_bf[...])
        u = _dot(x, wu_bf[...])
        hid = (_silu(a) * u).astype(BF16)
        ys_ref[...] = _dot(hid, wd_bf[...])

    @pl.when(m >= n_used_ref[0])
    def _():
        ys_ref[...] = jnp.zeros_like(ys_ref)


def _experts(xs, plan, wg, wu, wd):
    n_tiles = xs.shape[0] // TMX
    last_used = lambda m, n_used: jnp.minimum(m, n_used[0] - 1)
    row_spec = pl.BlockSpec((TMX, D_MODEL), lambda m, te, nu, *_: (last_used(m, nu), 0))
    out_spec = pl.BlockSpec((TMX, D_MODEL), lambda m, *_: (m, 0))
    hbm = pl.BlockSpec(memory_space=pl.ANY)
    up_shape, down_shape = (D_MODEL, D_EXPERT), (D_EXPERT, D_MODEL)
    grid_spec = pltpu.PrefetchScalarGridSpec(
        num_scalar_prefetch=5,
        grid=(n_tiles,),
        in_specs=[row_spec, hbm, hbm, hbm],
        out_specs=out_spec,
        scratch_shapes=[pltpu.VMEM((2,) + up_shape, F32), pltpu.VMEM((2,) + up_shape, F32),
                        pltpu.VMEM((2,) + down_shape, F32),
                        pltpu.VMEM(up_shape, BF16), pltpu.VMEM(up_shape, BF16), pltpu.VMEM(down_shape, BF16),
                        pltpu.SemaphoreType.DMA((2, 3))],
    )
    return pl.pallas_call(
        _experts_kernel,
        grid_spec=grid_spec,
        out_shape=jax.ShapeDtypeStruct(xs.shape, F32),
        compiler_params=pltpu.CompilerParams(
            dimension_semantics=("arbitrary",), vmem_limit_bytes=VMEM_LIMIT),
        name="experts",
    )(plan["tile_expert"], plan["n_used"], plan["tile_first"], plan["tile_slot"], plan["tile_next"], xs, wg, wu, wd)


def _combine_kernel(local_start_ref, n_chunks_ref, chunk_dst_ref, ys_ref, ri_ref, rw_ref, x1_ref, mod_ref,
                    gain_ref, o_ref, buf_ref, sem_ref, *, r_loc, tiles_per_batch):
    b = pl.program_id(0)
    nb = pl.num_programs(0)
    slot = b % 2
    nch_max = r_loc // CHUNK

    def chunk_copy(tile, sl, j):
        row = pl.multiple_of(j * CHUNK, CHUNK)
        src = pl.multiple_of(chunk_dst_ref[tile * nch_max + j], CHUNK)
        return pltpu.make_async_copy(ys_ref.at[pl.ds(src, CHUNK), :], buf_ref.at[sl, pl.ds(row, CHUNK), :],
                                     sem_ref.at[sl])

    def fetch(tile, sl):
        _for_each(n_chunks_ref[tile], lambda j: chunk_copy(tile, sl, j).start())

    @pl.when(b == 0)
    def _():
        buf_ref[...] = jnp.zeros_like(buf_ref)
        fetch(0, 0)

    @pl.when(b + 1 < nb)
    def _():
        fetch(b + 1, 1 - slot)

    one_chunk = pltpu.make_async_copy(ys_ref.at[pl.ds(0, CHUNK), :], buf_ref.at[slot, pl.ds(0, CHUNK), :],
                                      sem_ref.at[slot])
    _wait_times(one_chunk, n_chunks_ref[b])

    s1, s2 = _local_slots(ri_ref, local_start_ref, b)
    rows = lax.broadcasted_iota(jnp.int32, (r_loc, s1.shape[1]), 0)
    hit1 = rows == s1
    hit2 = rows == s2
    w_row = jnp.sum(jnp.where(hit1, rw_ref[0:1, :], jnp.where(hit2, rw_ref[1:2, :], 0.0)), axis=1, keepdims=True)
    perm = jnp.where(hit1 | hit2, 1.0, 0.0).astype(BF16)
    yw = (buf_ref[slot] * w_row).astype(BF16)
    moe = _dot_tn(perm, yw)
    batch = b // tiles_per_batch
    gate2 = mod_ref[pl.ds(batch, 1), 5 * D_MODEL:6 * D_MODEL]
    x2 = x1_ref[...] + gate2 * moe
    ms = jnp.mean(x2 * x2, axis=-1, keepdims=True)
    o_ref[...] = x2 * lax.rsqrt(ms + EPS) * gain_ref[...]


def _combine(ys, ri, rw, x1, mod, gain, plan, seq, tm):
    tokens = x1.shape[0]
    r_loc = _local_rows(tm)
    row_spec = pl.BlockSpec((8, tm), lambda i, *_: (0, i))
    tok_spec = pl.BlockSpec((tm, D_MODEL), lambda i, *_: (i, 0))
    full = lambda a: pl.BlockSpec(a.shape, lambda i, *_: (0,) * a.ndim)
    grid_spec = pltpu.PrefetchScalarGridSpec(
        num_scalar_prefetch=3,
        grid=(tokens // tm,),
        in_specs=[pl.BlockSpec(memory_space=pl.ANY), row_spec, row_spec, tok_spec, full(mod), full(gain)],
        out_specs=tok_spec,
        scratch_shapes=[pltpu.VMEM((2, r_loc, D_MODEL), F32), pltpu.SemaphoreType.DMA((2,))],
    )
    return pl.pallas_call(
        functools.partial(_combine_kernel, r_loc=r_loc, tiles_per_batch=seq // tm),
        grid_spec=grid_spec,
        out_shape=jax.ShapeDtypeStruct((tokens, D_MODEL), F32),
        compiler_params=pltpu.CompilerParams(
            dimension_semantics=("arbitrary",), vmem_limit_bytes=VMEM_LIMIT),
        name="combine",
    )(plan["local_start"], plan["n_chunks"], plan["chunk_dst"], ys, ri, rw, x1, mod, gain)


def _rotary_tables(seq):
    half = RET_HEAD_DIM // 2
    inv_freq = 1.0 / (ROPE_BASE ** (jnp.arange(half, dtype=F32) / half))
    ang = jnp.arange(seq).astype(F32)[:, None] * inv_freq[None, :]
    cos = jnp.cos(ang)
    sin = jnp.sin(ang)
    return jnp.tile(cos, (1, 4)), jnp.concatenate([-sin, sin, -sin, sin], axis=1)


def _pick_tile(n, pref):
    t = min(n, pref)
    assert n % t == 0, (n, t)
    return t


def kernel(x, c, ada_w, ada_b, norm1_gain, norm2_gain, w_in, w_out, ret_gn_gain, lam_q1, lam_k1, lam_q2,
           lam_k2, diff_subln_gain, w_group, b_group, w_expert, b_expert, w_gate, w_up, w_down, final_gain):
    batch, seq, d = x.shape
    assert d == D_MODEL and batch <= 8 and ada_w.shape[0] == 1
    layer = 0
    lambda_init = 0.8 - 0.6 * math.exp(-0.3 * layer)
    tokens = batch * seq
    x2 = x.reshape(tokens, d)
    tm = _pick_tile(seq, 512)

    c_pad = jnp.zeros((8, d), F32).at[:batch].set(c)
    mod = _adaln(c_pad, ada_w[layer], ada_b[layer].reshape(1, -1))

    cos_t, sin_t = _rotary_tables(seq)
    rq, rk, rv, rg, dq, dk, dvt = _inproj(
        x2, mod, norm1_gain[layer].reshape(1, d), w_in[layer].astype(BF16), cos_t, sin_t, seq, tm)

    ret_out = _retention(rq, rk, rv, rg, ret_gn_gain[layer].reshape(1, RET_WIDTH), batch, seq,
                         _pick_tile(seq, 256))
    diff_out = _diffattn(
        dq, dk, dvt, lam_q1[layer].reshape(1, -1), lam_k1[layer].reshape(1, -1), lam_q2[layer].reshape(1, -1),
        lam_k2[layer].reshape(1, -1), diff_subln_gain[layer].reshape(1, -1), batch, seq, lambda_init,
        _pick_tile(seq, 1024), 2 * QUERY_CHUNK)

    w_router = jnp.concatenate(
        [w_group[layer].T, jnp.zeros((8 - N_GROUPS, d), F32), w_expert[layer].reshape(d, N_EXPERTS).T], axis=0)
    b_router = jnp.concatenate(
        [b_group[layer], jnp.zeros((8 - N_GROUPS,), F32), b_expert[layer].reshape(N_EXPERTS)]).reshape(-1, 1)
    wr_hi = w_router.astype(BF16)
    wr_lo = (w_router - wr_hi.astype(F32)).astype(BF16)
    x1, h2, ri, rw, cnt = _outproj(ret_out, diff_out, x2, mod, norm2_gain[layer].reshape(1, d),
                                   w_out[layer].astype(BF16), wr_hi, wr_lo, b_router, seq, tm)

    plan = _dispatch_plan(cnt[:, :, 0], tokens, tm)
    xs = _dispatch(h2, ri, plan, tm)
    ys = _experts(xs, plan, w_gate[layer].reshape(N_EXPERTS, d, D_EXPERT),
                  w_up[layer].reshape(N_EXPERTS, d, D_EXPERT), w_down[layer].reshape(N_EXPERTS, D_EXPERT, d))
    out = _combine(ys, ri, rw, x1, mod, final_gain.reshape(1, d), plan, seq, tm)
    return out.reshape(batch, seq, d)
```

```python
import functools
import math

import jax
import jax.numpy as jnp
from jax import lax
from jax.experimental import pallas as pl
from jax.experimental.pallas import tpu as pltpu

F32 = jnp.float32
BF16 = jnp.bfloat16

D_MODEL = 1024
RET_HEAD_DIM = 64
RET_WIDTH = 512
RET_HEADS = 8
RET_PAIRS = RET_HEADS // 2
DIFF_QK_DIM = 64
DIFF_V_DIM = 128
DIFF_HEADS = 4
DIFF_WIDTH = 512
N_GROUPS = 4
EXPERTS_PER_GROUP = 8
N_EXPERTS = N_GROUPS * EXPERTS_PER_GROUP
D_EXPERT = 512
N_MOD = 6
ROPE_BASE = 10000.0
EPS = 1e-6
LANES = 128
ROUTER_ROWS = 8 + N_EXPERTS
VMEM_LIMIT = 56 * 1024 * 1024


def _dot(a, b):
    return jnp.dot(a, b, preferred_element_type=F32)


def _dot_nt(a, b):
    return lax.dot_general(a, b, (((1,), (1,)), ((), ())), preferred_element_type=F32)


def _dot_tn(a, b):
    return lax.dot_general(a, b, (((0,), (0,)), ((), ())), preferred_element_type=F32)


def _split_bf16(x):
    hi = x.astype(BF16)
    lo = (x - hi.astype(F32)).astype(BF16)
    return hi, lo


def _silu(x):
    return x / (1.0 + jnp.exp(-x))


def _adaln_kernel(c_ref, w_ref, b_ref, o_ref):
    ca = _silu(c_ref[...])
    c_hi, c_lo = _split_bf16(ca)
    w_hi, w_lo = _split_bf16(w_ref[...])
    o_ref[...] = _dot(c_hi, w_hi) + _dot(c_lo, w_hi) + _dot(c_hi, w_lo) + b_ref[...]


def _adaln(c_pad, ada_w, ada_b):
    n_out = ada_w.shape[1]
    tn = D_MODEL
    return pl.pallas_call(
        _adaln_kernel,
        grid=(n_out // tn,),
        in_specs=[
            pl.BlockSpec((8, D_MODEL), lambda j: (0, 0)),
            pl.BlockSpec((D_MODEL, tn), lambda j: (0, j)),
            pl.BlockSpec((1, tn), lambda j: (0, j)),
        ],
        out_specs=pl.BlockSpec((8, tn), lambda j: (0, j)),
        out_shape=jax.ShapeDtypeStruct((8, n_out), F32),
        compiler_params=pltpu.CompilerParams(vmem_limit_bytes=VMEM_LIMIT),
        name="adaln",
    )(c_pad, ada_w, ada_b)


def _norm_modulate(x, gain, shift, scale):
    ms = jnp.mean(x * x, axis=-1, keepdims=True)
    y = x * lax.rsqrt(ms + EPS) * gain
    return y * (1.0 + scale) + shift


def _rotary_slab(x, cos, sin_signed, lane_lo):
    swapped = jnp.where(lane_lo, pltpu.roll(x, 96, 1), pltpu.roll(x, 32, 1))
    return x * cos + swapped * sin_signed


def _inproj_kernel(x_ref, mod_ref, gain_ref, w_ref, cos_ref, sin_ref,
                   rq_ref, rk_ref, rv_ref, rg_ref, dq_ref, dk_ref, dvt_ref, *, tiles_per_batch):
    b = pl.program_id(0) // tiles_per_batch
    shift = mod_ref[pl.ds(b, 1), 0:D_MODEL]
    scale = mod_ref[pl.ds(b, 1), D_MODEL:2 * D_MODEL]
    h = _norm_modulate(x_ref[...], gain_ref[...], shift, scale).astype(BF16)
    cos = cos_ref[...]
    sin = sin_ref[...]
    lane = lax.broadcasted_iota(jnp.int32, cos.shape, 1)
    lane_lo = (lane % 64) < 32

    def proj(chunk):
        return _dot(h, w_ref[:, chunk * RET_WIDTH:(chunk + 1) * RET_WIDTH])

    def rotary(acc, out_ref, post_scale):
        for s in range(RET_WIDTH // LANES):
            sl = slice(s * LANES, (s + 1) * LANES)
            out_ref[:, sl] = (_rotary_slab(acc[:, sl], cos, sin, lane_lo) * post_scale).astype(BF16)

    rotary(proj(0), rq_ref, 1.0)
    rotary(proj(1), rk_ref, RET_HEAD_DIM ** -0.5)
    rv_ref[...] = proj(2).astype(BF16)
    rg_ref[...] = _silu(proj(3)).astype(BF16)
    dq_ref[...] = (proj(4) * (DIFF_QK_DIM ** -0.5 * math.log2(math.e))).astype(BF16)
    dk_ref[...] = proj(5).astype(BF16)
    dvt_ref[...] = proj(6).T.astype(BF16)


def _inproj(x2, mod, gain, w_in, cos_t, sin_t, seq, tm):
    tokens = x2.shape[0]
    tiles_per_batch = seq // tm
    tok_spec = lambda w: pl.BlockSpec((tm, w), lambda i: (i, 0))
    tab_spec = pl.BlockSpec((tm, LANES), lambda i: (i % tiles_per_batch, 0))
    full = lambda a: pl.BlockSpec(a.shape, lambda i: (0,) * a.ndim)
    out = jax.ShapeDtypeStruct((tokens, RET_WIDTH), BF16)
    return pl.pallas_call(
        functools.partial(_inproj_kernel, tiles_per_batch=tiles_per_batch),
        grid=(tokens // tm,),
        in_specs=[tok_spec(D_MODEL), full(mod), full(gain), full(w_in), tab_spec, tab_spec],
        out_specs=[tok_spec(RET_WIDTH)] * 6 + [pl.BlockSpec((DIFF_WIDTH, tm), lambda i: (0, i))],
        out_shape=[out] * 6 + [jax.ShapeDtypeStruct((DIFF_WIDTH, tokens), BF16)],
        compiler_params=pltpu.CompilerParams(
            dimension_semantics=("parallel",), vmem_limit_bytes=VMEM_LIMIT),
        name="inproj",
    )(x2, mod, gain, w_in, cos_t, sin_t)


def _retention_kernel(q_ref, k_ref, v_ref, g_ref, dec_ref, qdec_ref, kdec_ref, rdec_ref,
                      bmask_ref, gmean_ref, gain_ref, o_ref, state_ref, *, chunk):
    @pl.when(pl.program_id(1) == 0)
    def _():
        state_ref[...] = jnp.zeros_like(state_ref)

    lane = lax.broadcasted_iota(jnp.int32, (chunk, LANES), 1)
    first_head = lane < RET_HEAD_DIM
    gmean = gmean_ref[...]
    bmask = bmask_ref[...]
    pairs = range(RET_PAIRS)
    sl = [slice(p * LANES, (p + 1) * LANES) for p in pairs]
    q = [q_ref[:, sl[p]] for p in pairs]
    k = [k_ref[:, sl[p]] for p in pairs]
    v = [v_ref[:, sl[p]] for p in pairs]
    zero = jnp.zeros_like(q[0])
    q_stack = [jnp.concatenate([jnp.where(first_head, q[p], zero), jnp.where(first_head, zero, q[p])], axis=0)
               for p in pairs]
    scores = [(_dot_nt(q_stack[p], k[p]) * dec_ref[p]).astype(BF16) for p in pairs]
    state = [state_ref[p] for p in pairs]
    cross = [_dot(q[p], state[p].astype(BF16)) * qdec_ref[:, sl[p]] for p in pairs]
    k_dec = [(k[p].astype(F32) * kdec_ref[:, sl[p]]).astype(BF16) for p in pairs]
    for p in pairs:
        state_ref[p] = state[p] * rdec_ref[p] + _dot_tn(k_dec[p], v[p]) * bmask
    intra2 = [_dot(scores[p], v[p]) for p in pairs]
    y = [jnp.where(first_head, intra2[p][:chunk], intra2[p][chunk:]) + cross[p] for p in pairs]
    y_split = [_split_bf16(y[p]) for p in pairs]
    mu = [_dot(y_split[p][0], gmean) + _dot(y_split[p][1], gmean) for p in pairs]
    d = [y[p] - mu[p] for p in pairs]
    d_split = [_split_bf16(d[p] * d[p]) for p in pairs]
    var = [_dot(d_split[p][0], gmean) + _dot(d_split[p][1], gmean) for p in pairs]
    for p in pairs:
        yn = d[p] * lax.rsqrt(var[p] + EPS) * gain_ref[:, sl[p]]
        o_ref[:, sl[p]] = (g_ref[:, sl[p]].astype(F32) * yn).astype(BF16)


def _retention_tables(chunk):
    heads = jnp.arange(RET_HEADS, dtype=F32)
    log_gamma = jnp.log(1.0 - jnp.exp2(-5.0 - heads))
    idx = jnp.arange(chunk)
    rel = (idx[:, None] - idx[None, :]).astype(F32)
    decay = jnp.where(rel[None] >= 0, jnp.exp(log_gamma[:, None, None] * jnp.maximum(rel, 0.0)[None]), 0.0)
    dec2 = decay.reshape(RET_PAIRS, 2 * chunk, chunk)
    lane_lg = jnp.repeat(log_gamma, RET_HEAD_DIM)
    qdec = jnp.exp(lane_lg[None, :] * (idx + 1).astype(F32)[:, None])
    kdec = jnp.exp(lane_lg[None, :] * (chunk - 1 - idx).astype(F32)[:, None])
    rdec = jnp.exp(lane_lg * chunk).reshape(RET_PAIRS, LANES, 1) * jnp.ones((1, 1, LANES), F32)
    blk = jnp.arange(LANES) // RET_HEAD_DIM
    bmask = (blk[:, None] == blk[None, :]).astype(F32)
    gmean = (bmask / RET_HEAD_DIM).astype(BF16)
    return dec2, qdec, kdec, rdec, bmask, gmean


def _retention(rq, rk, rv, rg, gn_gain, batch, seq, chunk):
    nc = seq // chunk
    dec2, qdec, kdec, rdec, bmask, gmean = _retention_tables(chunk)
    tok_spec = pl.BlockSpec((chunk, RET_WIDTH), lambda b, n: (b * nc + n, 0))
    full = lambda a: pl.BlockSpec(a.shape, lambda b, n: (0,) * a.ndim)
    return pl.pallas_call(
        functools.partial(_retention_kernel, chunk=chunk),
        grid=(batch, nc),
        in_specs=[tok_spec] * 4 + [full(dec2), full(qdec), full(kdec), full(rdec), full(bmask),
                                   full(gmean), full(gn_gain)],
        out_specs=tok_spec,
        out_shape=jax.ShapeDtypeStruct(rq.shape, BF16),
        scratch_shapes=[pltpu.VMEM((RET_PAIRS, LANES, LANES), F32)],
        compiler_params=pltpu.CompilerParams(
            dimension_semantics=("parallel", "arbitrary"), vmem_limit_bytes=VMEM_LIMIT),
        name="retention",
    )(rq, rk, rv, rg, dec2, qdec, kdec, rdec, bmask, gmean, gn_gain)


NEG_BIG = -1e30


V_EXT_ROWS = DIFF_V_DIM + 16
QUERY_CHUNK = 256
SCORES_AHEAD_FULL = 3
SCORES_AHEAD_DIAG = 1


def _diag_chunks(tq, tk, d):
    assert tk == 2 * QUERY_CHUNK
    per_softmax = tq // QUERY_CHUNK
    out = []
    for c in range(2 * per_softmax):
        q0 = (c % per_softmax) * QUERY_CHUNK
        if q0 + QUERY_CHUNK - 1 < d * tk:
            continue
        kind = "full" if q0 >= (d + 1) * tk else ("tri" if q0 == d * tk else "low_tri")
        out.append((c, kind))
    return out


def _accumulate(acc_ref, cs, alpha, pv):
    acc_ref[:, cs] = alpha * acc_ref[:, cs] + pv


def _diffattn_kernel(q_ref, k_ref, vt_ref, lq1_ref, lk1_ref, lq2_ref, lk2_ref, gain_ref, bias_ref, o_ref,
                     qs_ref, vext_ref, m_ref, acc_ref, *, tq, tk, lambda_init):
    i = pl.program_id(2)
    nk = vext_ref.shape[0]

    @pl.when(i == 0)
    def _():
        for j in range(nk):
            vext_ref[j, 0:DIFF_V_DIM, :] = vt_ref[:, j * tk:(j + 1) * tk]
            vext_ref[j, DIFF_V_DIM:V_EXT_ROWS, :] = jnp.ones((V_EXT_ROWS - DIFF_V_DIM, tk), BF16)

    q = q_ref[...]
    lane = lax.broadcasted_iota(jnp.int32, q.shape, 1)
    zero = jnp.zeros_like(q)
    qs_ref[0:tq, :] = jnp.where(lane < DIFF_QK_DIM, q, zero)
    qs_ref[tq:2 * tq, :] = jnp.where(lane < DIFF_QK_DIM, zero, q)
    m_ref[...] = jnp.full_like(m_ref, NEG_BIG)
    acc_ref[...] = jnp.zeros_like(acc_ref)

    def step(work, n_ahead):
        chunk = lambda c: slice(c * QUERY_CHUNK, (c + 1) * QUERY_CHUNK)

        def scores(j, c, kind):
            n_keys = QUERY_CHUNK if kind == "tri" else tk
            start = pl.multiple_of(j * tk, tk)
            return _dot_nt(k_ref[pl.ds(start, n_keys), :], qs_ref[chunk(c), :])

        ahead = [scores(*work[n]) for n in range(min(n_ahead, len(work)))]
        pending = None
        for n, (j, c, kind) in enumerate(work):
            cs = chunk(c)
            st = ahead.pop(0)
            if n + n_ahead < len(work):
                ahead.append(scores(*work[n + n_ahead]))
            if kind == "tri":
                st = st + bias_ref[QUERY_CHUNK:, :]
            elif kind == "low_tri":
                st = st + bias_ref[...]
            m_old = m_ref[:, cs]
            m_new = jnp.maximum(m_old, jnp.max(st, axis=0, keepdims=True))
            alpha = jnp.exp2(m_old - m_new)
            p = jnp.exp2(st - m_new).astype(BF16)
            m_ref[:, cs] = m_new
            pv = _dot(vext_ref[j, :, 0:st.shape[0]], p)
            if pending is not None:
                pending()
            pending = functools.partial(_accumulate, acc_ref, cs, alpha, pv)
        pending()

    tiles_per_q = tq // tk
    n_chunks = 2 * tq // QUERY_CHUNK

    def full_tiles(it):
        return [(it * tiles_per_q + d, c, "full") for d in range(tiles_per_q) for c in range(n_chunks)]

    lax.fori_loop(0, i, lambda it, c: (step(full_tiles(it), SCORES_AHEAD_FULL), c)[1], 0)
    step([(i * tiles_per_q + d, c, kind) for d in range(tiles_per_q) for c, kind in _diag_chunks(tq, tk, d)],
         SCORES_AHEAD_DIAG)

    lam = (jnp.exp(jnp.sum(lq1_ref[...] * lk1_ref[...], axis=-1, keepdims=True))
           - jnp.exp(jnp.sum(lq2_ref[...] * lk2_ref[...], axis=-1, keepdims=True)) + lambda_init)
    acc = acc_ref[...]
    o2 = acc[0:DIFF_V_DIM, :] * (1.0 / acc[DIFF_V_DIM:DIFF_V_DIM + 1, :])
    o = (o2[:, :tq] - lam * o2[:, tq:]).T
    ms = jnp.mean(o * o, axis=-1, keepdims=True)
    o = o * lax.rsqrt(ms + EPS) * gain_ref[...] * (1.0 - lambda_init)
    o_ref[...] = o.astype(BF16)


def _diffattn(dq, dk, dvt, lam_q1, lam_k1, lam_q2, lam_k2, gain, batch, seq, lambda_init, tq, tk):
    nq = seq // tq
    q_spec = pl.BlockSpec((tq, LANES), lambda b, h, i: (b * nq + i, h))
    k_spec = pl.BlockSpec((seq, LANES), lambda b, h, i: (b, h))
    vt_spec = pl.BlockSpec((DIFF_V_DIM, seq), lambda b, h, i: (h, b))
    vec = lambda a: pl.BlockSpec(a.shape, lambda b, h, i: (0, 0))
    key = jnp.arange(tk)[:, None]
    query = QUERY_CHUNK + jnp.arange(QUERY_CHUNK)[None, :]
    bias = jnp.where(key <= query, 0.0, NEG_BIG).astype(F32)
    return pl.pallas_call(
        functools.partial(_diffattn_kernel, tq=tq, tk=tk, lambda_init=lambda_init),
        grid=(batch, DIFF_HEADS, nq),
        in_specs=[q_spec, k_spec, vt_spec, vec(lam_q1), vec(lam_k1), vec(lam_q2), vec(lam_k2), vec(gain),
                  vec(bias)],
        out_specs=q_spec,
        out_shape=jax.ShapeDtypeStruct(dq.shape, BF16),
        scratch_shapes=[
            pltpu.VMEM((2 * tq, LANES), BF16),
            pltpu.VMEM((seq // tk, V_EXT_ROWS, tk), BF16),
            pltpu.VMEM((1, 2 * tq), F32),
            pltpu.VMEM((V_EXT_ROWS, 2 * tq), F32),
        ],
        compiler_params=pltpu.CompilerParams(
            dimension_semantics=("parallel", "parallel", "arbitrary"), vmem_limit_bytes=VMEM_LIMIT),
        name="diffattn",
    )(dq, dk, dvt, lam_q1, lam_k1, lam_q2, lam_k2, gain, bias)


def _route(logits):
    r = [logits[g:g + 1, :] for g in range(N_GROUPS)]
    gmax = jnp.maximum(jnp.maximum(r[0], r[1]), jnp.maximum(r[2], r[3]))
    g_idx = jnp.where(r[0] == gmax, 0, jnp.where(r[1] == gmax, 1, jnp.where(r[2] == gmax, 2, 3)))
    denom = sum(jnp.exp(rg - gmax) for rg in r)
    g_weight = 1.0 / denom
    sel = jnp.zeros((EXPERTS_PER_GROUP, logits.shape[1]), F32)
    for g in range(N_GROUPS):
        rows = logits[8 + g * EXPERTS_PER_GROUP:8 + (g + 1) * EXPERTS_PER_GROUP, :]
        sel = jnp.where(g_idx == g, rows, sel)
    eidx = lax.broadcasted_iota(jnp.int32, sel.shape, 0)
    v1 = jnp.max(sel, axis=0, keepdims=True)
    i1 = jnp.min(jnp.where(sel == v1, eidx, EXPERTS_PER_GROUP), axis=0, keepdims=True)
    sel2 = jnp.where(eidx == i1, -jnp.inf, sel)
    v2 = jnp.max(sel2, axis=0, keepdims=True)
    i2 = jnp.min(jnp.where(sel2 == v2, eidx, EXPERTS_PER_GROUP), axis=0, keepdims=True)
    e2 = jnp.exp(v2 - v1)
    w1 = g_weight / (1.0 + e2)
    w2 = g_weight * e2 / (1.0 + e2)
    return g_idx, i1, i2, w1, w2


OUTPROJ_PARTS = 2


def _outproj_kernel(ret_ref, diff_ref, x_ref, mod_ref, gain_ref, wo_ref, wr_ref, br_ref, tri_ref,
                    x1_ref, h2_ref, ri_ref, rw_ref, cnt_ref, *, tiles_per_batch):
    b = pl.program_id(0) // tiles_per_batch
    gate1 = mod_ref[pl.ds(b, 1), 2 * D_MODEL:3 * D_MODEL]
    shift = mod_ref[pl.ds(b, 1), 3 * D_MODEL:4 * D_MODEL]
    scale = mod_ref[pl.ds(b, 1), 4 * D_MODEL:5 * D_MODEL]
    wr = wr_ref[...]
    tm = x_ref.shape[0]
    parts = [slice(n * tm // OUTPROJ_PARTS, (n + 1) * tm // OUTPROJ_PARTS) for n in range(OUTPROJ_PARTS)]
    mix = [_dot(ret_ref[r, :], wo_ref[0:RET_WIDTH, :]) + _dot(diff_ref[r, :], wo_ref[RET_WIDTH:, :]) for r in parts]
    x1 = [x_ref[r, :] + gate1 * m for r, m in zip(parts, mix)]
    for r, v in zip(parts, x1):
        x1_ref[r, :] = v
    h_split = [_split_bf16(_norm_modulate(v, gain_ref[...], shift, scale)) for v in x1]
    for r, (h_hi, _) in zip(parts, h_split):
        h2_ref[r, :] = h_hi
    by_hi = [_dot_nt(wr, h_hi) for h_hi, _ in h_split]
    by_lo = [_dot_nt(wr[:ROUTER_ROWS], h_lo) for _, h_lo in h_split]
    logits = [a[:ROUTER_ROWS] + a[ROUTER_ROWS:] + c + br_ref[...] for a, c in zip(by_hi, by_lo)]
    routed = [_route(lg) for lg in logits]
    g_idx, i1, i2, w1, w2 = [jnp.concatenate([rt[n] for rt in routed], axis=1) for n in range(5)]
    logits = jnp.concatenate(logits, axis=1)
    e1 = g_idx * EXPERTS_PER_GROUP + i1
    e2 = g_idx * EXPERTS_PER_GROUP + i2
    eidx = lax.broadcasted_iota(jnp.int32, (N_EXPERTS, logits.shape[1]), 0)
    hit1 = eidx == e1
    hit2 = eidx == e2
    onehot = jnp.where(hit1 | hit2, 1.0, 0.0)
    before = _dot(onehot.astype(BF16), tri_ref[...])
    r1 = jnp.sum(jnp.where(hit1, before, 0.0), axis=0, keepdims=True)
    r2 = jnp.sum(jnp.where(hit2, before, 0.0), axis=0, keepdims=True)
    zi = jnp.zeros_like(e1)
    ri_ref[...] = jnp.concatenate([e1, e2, r1.astype(jnp.int32), r2.astype(jnp.int32), zi, zi, zi, zi], axis=0)
    zf = jnp.zeros_like(w1)
    rw_ref[...] = jnp.concatenate([w1, w2, zf, zf, zf, zf, zf, zf], axis=0)
    counts = jnp.sum(onehot, axis=1, keepdims=True)
    cnt_ref[0] = jnp.broadcast_to(counts, (N_EXPERTS, LANES)).astype(jnp.int32)


def _outproj(ret_out, diff_out, x2, mod, gain, w_out, wr, br, seq, tm):
    tokens = x2.shape[0]
    tiles_per_batch = seq // tm
    n_tiles = tokens // tm
    tri = (jnp.arange(tm)[:, None] < jnp.arange(tm)[None, :]).astype(BF16)
    tok_spec = lambda w: pl.BlockSpec((tm, w), lambda i: (i, 0))
    row_spec = pl.BlockSpec((8, tm), lambda i: (0, i))
    full = lambda a: pl.BlockSpec(a.shape, lambda i: (0,) * a.ndim)
    return pl.pallas_call(
        functools.partial(_outproj_kernel, tiles_per_batch=tiles_per_batch),
        grid=(n_tiles,),
        in_specs=[tok_spec(RET_WIDTH), tok_spec(DIFF_WIDTH), tok_spec(D_MODEL), full(mod), full(gain),
                  full(w_out), full(wr), full(br), full(tri)],
        out_specs=[tok_spec(D_MODEL), tok_spec(D_MODEL), row_spec, row_spec,
                   pl.BlockSpec((1, N_EXPERTS, LANES), lambda i: (i, 0, 0))],
        out_shape=[jax.ShapeDtypeStruct((tokens, D_MODEL), F32),
                   jax.ShapeDtypeStruct((tokens, D_MODEL), BF16),
                   jax.ShapeDtypeStruct((8, tokens), jnp.int32),
                   jax.ShapeDtypeStruct((8, tokens), F32),
                   jax.ShapeDtypeStruct((n_tiles, N_EXPERTS, LANES), jnp.int32)],
        compiler_params=pltpu.CompilerParams(
            dimension_semantics=("parallel",), vmem_limit_bytes=VMEM_LIMIT),
        name="outproj",
    )(ret_out, diff_out, x2, mod, gain, w_out, wr, br, tri)


CHUNK = 8
TMX = 512


def _local_rows(tm):
    rows = 2 * tm + N_EXPERTS * (CHUNK - 1)
    return (rows + 15) // 16 * 16


def _sorted_rows_alloc(tokens, tm):
    worst = 2 * tokens + (tokens // tm) * N_EXPERTS * (CHUNK - 1) + N_EXPERTS * (TMX - CHUNK)
    return (worst + TMX - 1) // TMX * TMX


def _dispatch_plan(cnt, tokens, tm):
    i32 = jnp.int32
    nch_max = _local_rows(tm) // CHUNK
    pad = (cnt + CHUNK - 1) // CHUNK * CHUNK
    local_end = jnp.cumsum(pad, axis=1)
    local_start = local_end - pad
    seg_rows = jnp.sum(pad, axis=0)
    seg_pad = (seg_rows + TMX - 1) // TMX * TMX
    seg_end = jnp.cumsum(seg_pad)
    seg_start = seg_end - seg_pad
    run_dst = seg_start[None, :] + jnp.cumsum(pad, axis=0) - pad
    row = CHUNK * jnp.arange(nch_max, dtype=i32)[None, :, None]
    owns = (local_start[:, None, :] <= row) & (row < local_end[:, None, :])
    chunk_dst = row[:, :, 0] + jnp.sum(jnp.where(owns, (run_dst - local_start)[:, None, :], 0), axis=-1)
    m = TMX * jnp.arange(_sorted_rows_alloc(tokens, tm) // TMX, dtype=i32)
    tile_expert = jnp.minimum(jnp.sum(seg_end[None, :] <= m[:, None], axis=-1), N_EXPERTS - 1)
    towns = (seg_start[None, :] <= m[:, None]) & (m[:, None] < seg_end[None, :])
    used = seg_pad > 0
    parity = (jnp.cumsum(used) - used) % 2
    eids = jnp.arange(N_EXPERTS, dtype=i32)
    later_used = (eids[None, :] > eids[:, None]) & used[None, :]
    next_used = jnp.min(jnp.where(later_used, eids[None, :], N_EXPERTS), axis=1)
    next_used = jnp.where(next_used == N_EXPERTS, -1, next_used)
    pick = lambda per_expert: jnp.sum(jnp.where(towns, per_expert[None, :], 0), axis=-1)
    tile_first = jnp.sum(jnp.where(towns & (seg_start[None, :] == m[:, None]), 1, 0), axis=-1)
    tile_next = jnp.where(jnp.any(towns, axis=-1), pick(next_used), -1)
    return dict(
        tile_first=tile_first.astype(i32),
        tile_slot=pick(parity).astype(i32),
        tile_next=tile_next.astype(i32),
        local_start=local_start.reshape(-1).astype(i32),
        n_chunks=(local_end[:, -1] // CHUNK).astype(i32),
        chunk_dst=chunk_dst.reshape(-1).astype(i32),
        tail_base=(seg_start + seg_rows).astype(i32),
        tail_rows=(seg_pad - seg_rows).astype(i32),
        tile_expert=tile_expert.astype(i32),
        n_used=(seg_end[-1:] // TMX).astype(i32),
    )


WAIT_UNROLL = 8


def _wait_times(copy, n):
    lax.fori_loop(0, n // WAIT_UNROLL, lambda i, c: ([copy.wait() for _ in range(WAIT_UNROLL)], c)[1], 0)
    lax.fori_loop(0, n % WAIT_UNROLL, lambda i, c: (copy.wait(), c)[1], 0)


def _for_each(n, body, unroll=4):
    main = n // unroll
    lax.fori_loop(0, main, lambda i, c: ([body(i * unroll + u) for u in range(unroll)], c)[1], 0)
    lax.fori_loop(main * unroll, n, lambda j, c: (body(j), c)[1], 0)


def _local_slots(ri_ref, local_start_ref, tile):
    e1, e2 = ri_ref[0:1, :], ri_ref[1:2, :]
    s1, s2 = ri_ref[2:3, :], ri_ref[3:4, :]
    for e in range(N_EXPERTS):
        start = local_start_ref[tile * N_EXPERTS + e]
        s1 = s1 + jnp.where(e1 == e, start, 0)
        s2 = s2 + jnp.where(e2 == e, start, 0)
    return s1, s2


def _dispatch_kernel(local_start_ref, n_chunks_ref, chunk_dst_ref, tail_base_ref, tail_rows_ref, n_used_ref,
                     h_ref, ri_ref, xs_ref, buf_ref, zero_ref, sem_ref, tail_sem_ref, *, r_loc):
    b = pl.program_id(0)
    nb = pl.num_programs(0)
    slot = b % 2
    nch_max = r_loc // CHUNK

    def chunk_copy(tile, sl, j):
        row = pl.multiple_of(j * CHUNK, CHUNK)
        dst = pl.multiple_of(chunk_dst_ref[tile * nch_max + j], CHUNK)
        return pltpu.make_async_copy(buf_ref.at[sl, pl.ds(row, CHUNK), :], xs_ref.at[pl.ds(dst, CHUNK), :],
                                     sem_ref.at[sl])

    def drain(tile, sl):
        one_chunk = pltpu.make_async_copy(buf_ref.at[sl, pl.ds(0, CHUNK), :], xs_ref.at[pl.ds(0, CHUNK), :],
                                          sem_ref.at[sl])
        _wait_times(one_chunk, n_chunks_ref[tile])

    @pl.when(b >= 2)
    def _():
        drain(b - 2, slot)

    s1, s2 = _local_slots(ri_ref, local_start_ref, b)
    rows = lax.broadcasted_iota(jnp.int32, (r_loc, s1.shape[1]), 0)
    perm = jnp.where((rows == s1) | (rows == s2), 1.0, 0.0).astype(BF16)
    buf_ref[slot] = _dot(perm, h_ref[...])
    _for_each(n_chunks_ref[b], lambda j: chunk_copy(b, slot, j).start())

    @pl.when(b == nb - 1)
    def _():
        zero_ref[...] = jnp.zeros_like(zero_ref)

        def tail_pieces(e, act):
            n = tail_rows_ref[e]
            size = TMX // 2
            while size >= CHUNK:
                dst = pl.multiple_of(tail_base_ref[e] + (n & (-2 * size)), CHUNK)
                cp = pltpu.make_async_copy(zero_ref.at[pl.ds(0, size), :], xs_ref.at[pl.ds(dst, size), :],
                                           tail_sem_ref.at[0])
                pl.when((n & size) != 0)(functools.partial(act, cp))
                size //= 2

        def unused_tile_copy(m):
            dst = pl.multiple_of(m * TMX, TMX)
            return pltpu.make_async_copy(zero_ref, xs_ref.at[pl.ds(dst, TMX), :], tail_sem_ref.at[1])

        n_alloc = xs_ref.shape[0] // TMX
        lax.fori_loop(0, N_EXPERTS, lambda e, c: (tail_pieces(e, lambda cp: cp.start()), c)[1], 0)
        lax.fori_loop(n_used_ref[0], n_alloc, lambda m, c: (unused_tile_copy(m).start(), c)[1], 0)
        lax.fori_loop(0, N_EXPERTS, lambda e, c: (tail_pieces(e, lambda cp: cp.wait()), c)[1], 0)
        lax.fori_loop(n_used_ref[0], n_alloc, lambda m, c: (unused_tile_copy(m).wait(), c)[1], 0)

        @pl.when(b >= 1)
        def _():
            drain(b - 1, 1 - slot)

        drain(b, slot)


def _dispatch(h2, ri, plan, tm):
    tokens = h2.shape[0]
    r_loc = _local_rows(tm)
    grid_spec = pltpu.PrefetchScalarGridSpec(
        num_scalar_prefetch=6,
        grid=(tokens // tm,),
        in_specs=[pl.BlockSpec((tm, D_MODEL), lambda i, *_: (i, 0)),
                  pl.BlockSpec((8, tm), lambda i, *_: (0, i))],
        out_specs=pl.BlockSpec(memory_space=pl.ANY),
        scratch_shapes=[pltpu.VMEM((2, r_loc, D_MODEL), F32), pltpu.VMEM((TMX, D_MODEL), F32),
                        pltpu.SemaphoreType.DMA((2,)), pltpu.SemaphoreType.DMA((2,))],
    )
    return pl.pallas_call(
        functools.partial(_dispatch_kernel, r_loc=r_loc),
        grid_spec=grid_spec,
        out_shape=jax.ShapeDtypeStruct((_sorted_rows_alloc(tokens, tm), D_MODEL), F32),
        compiler_params=pltpu.CompilerParams(
            dimension_semantics=("arbitrary",), vmem_limit_bytes=VMEM_LIMIT),
        name="dispatch",
    )(plan["local_start"], plan["n_chunks"], plan["chunk_dst"], plan["tail_base"], plan["tail_rows"], plan["n_used"],
      h2, ri)


def _experts_kernel(tile_expert_ref, n_used_ref, first_ref, slot_ref, next_ref, xs_ref, wg_hbm, wu_hbm, wd_hbm,
                    ys_ref, wg_st, wu_st, wd_st, wg_bf, wu_bf, wd_bf, sem_ref):
    m = pl.program_id(0)

    def weight_copies(e, s):
        return [pltpu.make_async_copy(src.at[e], dst.at[s], sem_ref.at[s, n])
                for n, (src, dst) in enumerate([(wg_hbm, wg_st), (wu_hbm, wu_st), (wd_hbm, wd_st)])]

    @pl.when(m < n_used_ref[0])
    def _():
        @pl.when(first_ref[m] == 1)
        def _():
            s = slot_ref[m]

            @pl.when(m == 0)
            def _():
                for cp in weight_copies(tile_expert_ref[0], 0):
                    cp.start()

            for cp in weight_copies(tile_expert_ref[m], s):
                cp.wait()

            @pl.when(next_ref[m] >= 0)
            def _():
                for cp in weight_copies(next_ref[m], 1 - s):
                    cp.start()

            wg_bf[...] = wg_st[s].astype(BF16)
            wu_bf[...] = wu_st[s].astype(BF16)
            wd_bf[...] = wd_st[s].astype(BF16)

        x = xs_ref[...].astype(BF16)
        a = _dot(x, wg_bf[...])
        u = _dot(x, wu_bf[...])
        hid = (_silu(a) * u).astype(BF16)
        ys_ref[...] = _dot(hid, wd_bf[...])

    @pl.when(m >= n_used_ref[0])
    def _():
        ys_ref[...] = jnp.zeros_like(ys_ref)


def _experts(xs, plan, wg, wu, wd):
    n_tiles = xs.shape[0] // TMX
    last_used = lambda m, n_used: jnp.minimum(m, n_used[0] - 1)
    row_spec = pl.BlockSpec((TMX, D_MODEL), lambda m, te, nu, *_: (last_used(m, nu), 0))
    out_spec = pl.BlockSpec((TMX, D_MODEL), lambda m, *_: (m, 0))
    hbm = pl.BlockSpec(memory_space=pl.ANY)
    up_shape, down_shape = (D_MODEL, D_EXPERT), (D_EXPERT, D_MODEL)
    grid_spec = pltpu.PrefetchScalarGridSpec(
        num_scalar_prefetch=5,
        grid=(n_tiles,),
        in_specs=[row_spec, hbm, hbm, hbm],
        out_specs=out_spec,
        scratch_shapes=[pltpu.VMEM((2,) + up_shape, F32), pltpu.VMEM((2,) + up_shape, F32),
                        pltpu.VMEM((2,) + down_shape, F32),
                        pltpu.VMEM(up_shape, BF16), pltpu.VMEM(up_shape, BF16), pltpu.VMEM(down_shape, BF16),
                        pltpu.SemaphoreType.DMA((2, 3))],
    )
    return pl.pallas_call(
        _experts_kernel,
        grid_spec=grid_spec,
        out_shape=jax.ShapeDtypeStruct(xs.shape, F32),
        compiler_params=pltpu.CompilerParams(
            dimension_semantics=("arbitrary",), vmem_limit_bytes=VMEM_LIMIT),
        name="experts",
    )(plan["tile_expert"], plan["n_used"], plan["tile_first"], plan["tile_slot"], plan["tile_next"], xs, wg, wu, wd)


def _combine_kernel(local_start_ref, n_chunks_ref, chunk_dst_ref, ys_ref, ri_ref, rw_ref, x1_ref, mod_ref,
                    gain_ref, o_ref, buf_ref, sem_ref, *, r_loc, tiles_per_batch):
    b = pl.program_id(0)
    nb = pl.num_programs(0)
    slot = b % 2
    nch_max = r_loc // CHUNK

    def chunk_copy(tile, sl, j):
        row = pl.multiple_of(j * CHUNK, CHUNK)
        src = pl.multiple_of(chunk_dst_ref[tile * nch_max + j], CHUNK)
        return pltpu.make_async_copy(ys_ref.at[pl.ds(src, CHUNK), :], buf_ref.at[sl, pl.ds(row, CHUNK), :],
                                     sem_ref.at[sl])

    def fetch(tile, sl):
        _for_each(n_chunks_ref[tile], lambda j: chunk_copy(tile, sl, j).start())

    @pl.when(b == 0)
    def _():
        buf_ref[...] = jnp.zeros_like(buf_ref)
        fetch(0, 0)

    @pl.when(b + 1 < nb)
    def _():
        fetch(b + 1, 1 - slot)

    one_chunk = pltpu.make_async_copy(ys_ref.at[pl.ds(0, CHUNK), :], buf_ref.at[slot, pl.ds(0, CHUNK), :],
                                      sem_ref.at[slot])
    _wait_times(one_chunk, n_chunks_ref[b])

    s1, s2 = _local_slots(ri_ref, local_start_ref, b)
    rows = lax.broadcasted_iota(jnp.int32, (r_loc, s1.shape[1]), 0)
    hit1 = rows == s1
    hit2 = rows == s2
    w_row = jnp.sum(jnp.where(hit1, rw_ref[0:1, :], jnp.where(hit2, rw_ref[1:2, :], 0.0)), axis=1, keepdims=True)
    perm = jnp.where(hit1 | hit2, 1.0, 0.0).astype(BF16)
    yw = (buf_ref[slot] * w_row).astype(BF16)
    moe = _dot_tn(perm, yw)
    batch = b // tiles_per_batch
    gate2 = mod_ref[pl.ds(batch, 1), 5 * D_MODEL:6 * D_MODEL]
    x2 = x1_ref[...] + gate2 * moe
    ms = jnp.mean(x2 * x2, axis=-1, keepdims=True)
    o_ref[...] = x2 * lax.rsqrt(ms + EPS) * gain_ref[...]


def _combine(ys, ri, rw, x1, mod, gain, plan, seq, tm):
    tokens = x1.shape[0]
    r_loc = _local_rows(tm)
    row_spec = pl.BlockSpec((8, tm), lambda i, *_: (0, i))
    tok_spec = pl.BlockSpec((tm, D_MODEL), lambda i, *_: (i, 0))
    full = lambda a: pl.BlockSpec(a.shape, lambda i, *_: (0,) * a.ndim)
    grid_spec = pltpu.PrefetchScalarGridSpec(
        num_scalar_prefetch=3,
        grid=(tokens // tm,),
        in_specs=[pl.BlockSpec(memory_space=pl.ANY), row_spec, row_spec, tok_spec, full(mod), full(gain)],
        out_specs=tok_spec,
        scratch_shapes=[pltpu.VMEM((2, r_loc, D_MODEL), F32), pltpu.SemaphoreType.DMA((2,))],
    )
    return pl.pallas_call(
        functools.partial(_combine_kernel, r_loc=r_loc, tiles_per_batch=seq // tm),
        grid_spec=grid_spec,
        out_shape=jax.ShapeDtypeStruct((tokens, D_MODEL), F32),
        compiler_params=pltpu.CompilerParams(
            dimension_semantics=("arbitrary",), vmem_limit_bytes=VMEM_LIMIT),
        name="combine",
    )(plan["local_start"], plan["n_chunks"], plan["chunk_dst"], ys, ri, rw, x1, mod, gain)


def _rotary_tables(seq):
    half = RET_HEAD_DIM // 2
    inv_freq = 1.0 / (ROPE_BASE ** (jnp.arange(half, dtype=F32) / half))
    ang = jnp.arange(seq).astype(F32)[:, None] * inv_freq[None, :]
    cos = jnp.cos(ang)
    sin = jnp.sin(ang)
    return jnp.tile(cos, (1, 4)), jnp.concatenate([-sin, sin, -sin, sin], axis=1)


def _pick_tile(n, pref):
    t = min(n, pref)
    assert n % t == 0, (n, t)
    return t


def kernel(x, c, ada_w, ada_b, norm1_gain, norm2_gain, w_in, w_out, ret_gn_gain, lam_q1, lam_k1, lam_q2,
           lam_k2, diff_subln_gain, w_group, b_group, w_expert, b_expert, w_gate, w_up, w_down, final_gain):
    batch, seq, d = x.shape
    assert d == D_MODEL and batch <= 8 and ada_w.shape[0] == 1
    layer = 0
    lambda_init = 0.8 - 0.6 * math.exp(-0.3 * layer)
    tokens = batch * seq
    x2 = x.reshape(tokens, d)
    tm = _pick_tile(seq, 512)

    c_pad = jnp.zeros((8, d), F32).at[:batch].set(c)
    mod = _adaln(c_pad, ada_w[layer], ada_b[layer].reshape(1, -1))

    cos_t, sin_t = _rotary_tables(seq)
    rq, rk, rv, rg, dq, dk, dvt = _inproj(
        x2, mod, norm1_gain[layer].reshape(1, d), w_in[layer].astype(BF16), cos_t, sin_t, seq, tm)

    ret_out = _retention(rq, rk, rv, rg, ret_gn_gain[layer].reshape(1, RET_WIDTH), batch, seq,
                         _pick_tile(seq, 256))
    diff_out = _diffattn(
        dq, dk, dvt, lam_q1[layer].reshape(1, -1), lam_k1[layer].reshape(1, -1), lam_q2[layer].reshape(1, -1),
        lam_k2[layer].reshape(1, -1), diff_subln_gain[layer].reshape(1, -1), batch, seq, lambda_init,
        _pick_tile(seq, 1024), 2 * QUERY_CHUNK)

    w_router = jnp.concatenate(
        [w_group[layer].T, jnp.zeros((8 - N_GROUPS, d), F32), w_expert[layer].reshape(d, N_EXPERTS).T], axis=0)
    b_router = jnp.concatenate(
        [b_group[layer], jnp.zeros((8 - N_GROUPS,), F32), b_expert[layer].reshape(N_EXPERTS)]).reshape(-1, 1)
    wr_hi = w_router.astype(BF16)
    wr_lo = (w_router - wr_hi.astype(F32)).astype(BF16)
    x1, h2, ri, rw, cnt = _outproj(ret_out, diff_out, x2, mod, norm2_gain[layer].reshape(1, d),
                                   w_out[layer].astype(BF16), jnp.concatenate([wr_hi, wr_lo], axis=0), b_router,
                                   seq, tm)

    plan = _dispatch_plan(cnt[:, :, 0], tokens, tm)
    xs = _dispatch(h2, ri, plan, tm)
    ys = _experts(xs, plan, w_gate[layer].reshape(N_EXPERTS, d, D_EXPERT),
                  w_up[layer].reshape(N_EXPERTS, d, D_EXPERT), w_down[layer].reshape(N_EXPERTS, D_EXPERT, d))
    out = _combine(ys, ri, rw, x1, mod, final_gain.reshape(1, d), plan, seq, tm)
    return out.reshape(batch, seq, d)
```

```python
import functools
import math

import jax
import jax.numpy as jnp
from jax import lax
from jax.experimental import pallas as pl
from jax.experimental.pallas import tpu as pltpu

F32 = jnp.float32
BF16 = jnp.bfloat16

D_MODEL = 1024
RET_HEAD_DIM = 64
RET_WIDTH = 512
RET_HEADS = 8
RET_PAIRS = RET_HEADS // 2
DIFF_QK_DIM = 64
DIFF_V_DIM = 128
DIFF_HEADS = 4
DIFF_WIDTH = 512
N_GROUPS = 4
EXPERTS_PER_GROUP = 8
N_EXPERTS = N_GROUPS * EXPERTS_PER_GROUP
D_EXPERT = 512
N_MOD = 6
ROPE_BASE = 10000.0
EPS = 1e-6
LANES = 128
ROUTER_ROWS = 8 + N_EXPERTS
VMEM_LIMIT = 56 * 1024 * 1024


def _dot(a, b):
    return jnp.dot(a, b, preferred_element_type=F32)


def _dot_nt(a, b):
    return lax.dot_general(a, b, (((1,), (1,)), ((), ())), preferred_element_type=F32)


def _dot_tn(a, b):
    return lax.dot_general(a, b, (((0,), (0,)), ((), ())), preferred_element_type=F32)


def _split_bf16(x):
    hi = x.astype(BF16)
    lo = (x - hi.astype(F32)).astype(BF16)
    return hi, lo


def _silu(x):
    return x / (1.0 + jnp.exp(-x))


def _adaln_kernel(c_ref, w_ref, b_ref, o_ref):
    ca = _silu(c_ref[...])
    c_hi, c_lo = _split_bf16(ca)
    w_hi, w_lo = _split_bf16(w_ref[...])
    o_ref[...] = _dot(c_hi, w_hi) + _dot(c_lo, w_hi) + _dot(c_hi, w_lo) + b_ref[...]


def _adaln(c_pad, ada_w, ada_b):
    n_out = ada_w.shape[1]
    tn = D_MODEL
    return pl.pallas_call(
        _adaln_kernel,
        grid=(n_out // tn,),
        in_specs=[
            pl.BlockSpec((8, D_MODEL), lambda j: (0, 0)),
            pl.BlockSpec((D_MODEL, tn), lambda j: (0, j)),
            pl.BlockSpec((1, tn), lambda j: (0, j)),
        ],
        out_specs=pl.BlockSpec((8, tn), lambda j: (0, j)),
        out_shape=jax.ShapeDtypeStruct((8, n_out), F32),
        compiler_params=pltpu.CompilerParams(vmem_limit_bytes=VMEM_LIMIT),
        name="adaln",
    )(c_pad, ada_w, ada_b)


def _norm_modulate(x, gain, shift, scale):
    ms = jnp.mean(x * x, axis=-1, keepdims=True)
    y = x * lax.rsqrt(ms + EPS) * gain
    return y * (1.0 + scale) + shift


def _rotary_slab(x, cos, sin_signed, lane_lo):
    swapped = jnp.where(lane_lo, pltpu.roll(x, 96, 1), pltpu.roll(x, 32, 1))
    return x * cos + swapped * sin_signed


def _inproj_kernel(x_ref, mod_ref, gain_ref, w_ref, cos_ref, sin_ref,
                   rq_ref, rk_ref, rv_ref, rg_ref, dq_ref, dk_ref, dvt_ref, *, tiles_per_batch):
    b = pl.program_id(0) // tiles_per_batch
    shift = mod_ref[pl.ds(b, 1), 0:D_MODEL]
    scale = mod_ref[pl.ds(b, 1), D_MODEL:2 * D_MODEL]
    h = _norm_modulate(x_ref[...], gain_ref[...], shift, scale).astype(BF16)
    cos = cos_ref[...]
    sin = sin_ref[...]
    lane = lax.broadcasted_iota(jnp.int32, cos.shape, 1)
    lane_lo = (lane % 64) < 32

    def proj(chunk):
        return _dot(h, w_ref[:, chunk * RET_WIDTH:(chunk + 1) * RET_WIDTH])

    def rotary(acc, out_ref, post_scale):
        for s in range(RET_WIDTH // LANES):
            sl = slice(s * LANES, (s + 1) * LANES)
            out_ref[:, sl] = (_rotary_slab(acc[:, sl], cos, sin, lane_lo) * post_scale).astype(BF16)

    rotary(proj(0), rq_ref, 1.0)
    rotary(proj(1), rk_ref, RET_HEAD_DIM ** -0.5)
    rv_ref[...] = proj(2).astype(BF16)
    rg_ref[...] = _silu(proj(3)).astype(BF16)
    dq_ref[...] = (proj(4) * (DIFF_QK_DIM ** -0.5 * math.log2(math.e))).astype(BF16)
    dk_ref[...] = proj(5).astype(BF16)
    dvt_ref[...] = proj(6).T.astype(BF16)


def _inproj(x2, mod, gain, w_in, cos_t, sin_t, seq, tm):
    tokens = x2.shape[0]
    tiles_per_batch = seq // tm
    tok_spec = lambda w: pl.BlockSpec((tm, w), lambda i: (i, 0))
    tab_spec = pl.BlockSpec((tm, LANES), lambda i: (i % tiles_per_batch, 0))
    full = lambda a: pl.BlockSpec(a.shape, lambda i: (0,) * a.ndim)
    out = jax.ShapeDtypeStruct((tokens, RET_WIDTH), BF16)
    return pl.pallas_call(
        functools.partial(_inproj_kernel, tiles_per_batch=tiles_per_batch),
        grid=(tokens // tm,),
        in_specs=[tok_spec(D_MODEL), full(mod), full(gain), full(w_in), tab_spec, tab_spec],
        out_specs=[tok_spec(RET_WIDTH)] * 6 + [pl.BlockSpec((DIFF_WIDTH, tm), lambda i: (0, i))],
        out_shape=[out] * 6 + [jax.ShapeDtypeStruct((DIFF_WIDTH, tokens), BF16)],
        compiler_params=pltpu.CompilerParams(
            dimension_semantics=("parallel",), vmem_limit_bytes=VMEM_LIMIT),
        name="inproj",
    )(x2, mod, gain, w_in, cos_t, sin_t)


def _retention_kernel(q_ref, k_ref, v_ref, g_ref, dec_ref, qdec_ref, kdec_ref, rdec_ref,
                      bmask_ref, gmean_ref, gain_ref, o_ref, state_ref, *, chunk):
    @pl.when(pl.program_id(1) == 0)
    def _():
        state_ref[...] = jnp.zeros_like(state_ref)

    lane = lax.broadcasted_iota(jnp.int32, (chunk, LANES), 1)
    first_head = lane < RET_HEAD_DIM
    gmean = gmean_ref[...]
    bmask = bmask_ref[...]
    pairs = range(RET_PAIRS)
    sl = [slice(p * LANES, (p + 1) * LANES) for p in pairs]
    q = [q_ref[:, sl[p]] for p in pairs]
    k = [k_ref[:, sl[p]] for p in pairs]
    v = [v_ref[:, sl[p]] for p in pairs]
    zero = jnp.zeros_like(q[0])
    q_stack = [jnp.concatenate([jnp.where(first_head, q[p], zero), jnp.where(first_head, zero, q[p])], axis=0)
               for p in pairs]
    scores = [(_dot_nt(q_stack[p], k[p]) * dec_ref[p]).astype(BF16) for p in pairs]
    state = [state_ref[p] for p in pairs]
    cross = [_dot(q[p], state[p].astype(BF16)) * qdec_ref[:, sl[p]] for p in pairs]
    k_dec = [(k[p].astype(F32) * kdec_ref[:, sl[p]]).astype(BF16) for p in pairs]
    for p in pairs:
        state_ref[p] = state[p] * rdec_ref[p] + _dot_tn(k_dec[p], v[p]) * bmask
    intra2 = [_dot(scores[p], v[p]) for p in pairs]
    y = [jnp.where(first_head, intra2[p][:chunk], intra2[p][chunk:]) + cross[p] for p in pairs]
    y_split = [_split_bf16(y[p]) for p in pairs]
    mu = [_dot(y_split[p][0], gmean) + _dot(y_split[p][1], gmean) for p in pairs]
    d = [y[p] - mu[p] for p in pairs]
    d_split = [_split_bf16(d[p] * d[p]) for p in pairs]
    var = [_dot(d_split[p][0], gmean) + _dot(d_split[p][1], gmean) for p in pairs]
    for p in pairs:
        yn = d[p] * lax.rsqrt(var[p] + EPS) * gain_ref[:, sl[p]]
        o_ref[:, sl[p]] = (g_ref[:, sl[p]].astype(F32) * yn).astype(BF16)


def _retention_tables(chunk):
    heads = jnp.arange(RET_HEADS, dtype=F32)
    log_gamma = jnp.log(1.0 - jnp.exp2(-5.0 - heads))
    idx = jnp.arange(chunk)
    rel = (idx[:, None] - idx[None, :]).astype(F32)
    decay = jnp.where(rel[None] >= 0, jnp.exp(log_gamma[:, None, None] * jnp.maximum(rel, 0.0)[None]), 0.0)
    dec2 = decay.reshape(RET_PAIRS, 2 * chunk, chunk)
    lane_lg = jnp.repeat(log_gamma, RET_HEAD_DIM)
    qdec = jnp.exp(lane_lg[None, :] * (idx + 1).astype(F32)[:, None])
    kdec = jnp.exp(lane_lg[None, :] * (chunk - 1 - idx).astype(F32)[:, None])
    rdec = jnp.exp(lane_lg * chunk).reshape(RET_PAIRS, LANES, 1) * jnp.ones((1, 1, LANES), F32)
    blk = jnp.arange(LANES) // RET_HEAD_DIM
    bmask = (blk[:, None] == blk[None, :]).astype(F32)
    gmean = (bmask / RET_HEAD_DIM).astype(BF16)
    return dec2, qdec, kdec, rdec, bmask, gmean


def _retention(rq, rk, rv, rg, gn_gain, batch, seq, chunk):
    nc = seq // chunk
    dec2, qdec, kdec, rdec, bmask, gmean = _retention_tables(chunk)
    tok_spec = pl.BlockSpec((chunk, RET_WIDTH), lambda b, n: (b * nc + n, 0))
    full = lambda a: pl.BlockSpec(a.shape, lambda b, n: (0,) * a.ndim)
    return pl.pallas_call(
        functools.partial(_retention_kernel, chunk=chunk),
        grid=(batch, nc),
        in_specs=[tok_spec] * 4 + [full(dec2), full(qdec), full(kdec), full(rdec), full(bmask),
                                   full(gmean), full(gn_gain)],
        out_specs=tok_spec,
        out_shape=jax.ShapeDtypeStruct(rq.shape, BF16),
        scratch_shapes=[pltpu.VMEM((RET_PAIRS, LANES, LANES), F32)],
        compiler_params=pltpu.CompilerParams(
            dimension_semantics=("parallel", "arbitrary"), vmem_limit_bytes=VMEM_LIMIT),
        name="retention",
    )(rq, rk, rv, rg, dec2, qdec, kdec, rdec, bmask, gmean, gn_gain)


NEG_BIG = -1e30


V_EXT_ROWS = DIFF_V_DIM + 16
QUERY_CHUNK = 256
SCORES_AHEAD_FULL = 3
SCORES_AHEAD_DIAG = 1


def _diag_chunks(tq, tk, d):
    assert tk == 2 * QUERY_CHUNK
    per_softmax = tq // QUERY_CHUNK
    out = []
    for c in range(2 * per_softmax):
        q0 = (c % per_softmax) * QUERY_CHUNK
        if q0 + QUERY_CHUNK - 1 < d * tk:
            continue
        kind = "full" if q0 >= (d + 1) * tk else ("tri" if q0 == d * tk else "low_tri")
        out.append((c, kind))
    return out


def _accumulate(acc_ref, cs, alpha, pv):
    acc_ref[:, cs] = alpha * acc_ref[:, cs] + pv


def _diffattn_kernel(q_ref, k_ref, vt_ref, lq1_ref, lk1_ref, lq2_ref, lk2_ref, gain_ref, bias_ref, o_ref,
                     qs_ref, vext_ref, m_ref, acc_ref, *, tq, tk, lambda_init):
    i = pl.program_id(2)
    nk = vext_ref.shape[0]

    @pl.when(i == 0)
    def _():
        for j in range(nk):
            vext_ref[j, 0:DIFF_V_DIM, :] = vt_ref[:, j * tk:(j + 1) * tk]
            vext_ref[j, DIFF_V_DIM:V_EXT_ROWS, :] = jnp.ones((V_EXT_ROWS - DIFF_V_DIM, tk), BF16)

    q = q_ref[...]
    lane = lax.broadcasted_iota(jnp.int32, q.shape, 1)
    zero = jnp.zeros_like(q)
    qs_ref[0:tq, :] = jnp.where(lane < DIFF_QK_DIM, q, zero)
    qs_ref[tq:2 * tq, :] = jnp.where(lane < DIFF_QK_DIM, zero, q)
    m_ref[...] = jnp.full_like(m_ref, NEG_BIG)
    acc_ref[...] = jnp.zeros_like(acc_ref)

    def step(work, n_ahead):
        chunk = lambda c: slice(c * QUERY_CHUNK, (c + 1) * QUERY_CHUNK)

        def scores(j, c, kind):
            n_keys = QUERY_CHUNK if kind == "tri" else tk
            start = pl.multiple_of(j * tk, tk)
            return _dot_nt(k_ref[pl.ds(start, n_keys), :], qs_ref[chunk(c), :])

        ahead = [scores(*work[n]) for n in range(min(n_ahead, len(work)))]
        pending = None
        for n, (j, c, kind) in enumerate(work):
            cs = chunk(c)
            st = ahead.pop(0)
            if n + n_ahead < len(work):
                ahead.append(scores(*work[n + n_ahead]))
            if kind == "tri":
                st = st + bias_ref[QUERY_CHUNK:, :]
            elif kind == "low_tri":
                st = st + bias_ref[...]
            m_old = m_ref[:, cs]
            m_new = jnp.maximum(m_old, jnp.max(st, axis=0, keepdims=True))
            alpha = jnp.exp2(m_old - m_new)
            p = jnp.exp2(st - m_new).astype(BF16)
            m_ref[:, cs] = m_new
            pv = _dot(vext_ref[j, :, 0:st.shape[0]], p)
            if pending is not None:
                pending()
            pending = functools.partial(_accumulate, acc_ref, cs, alpha, pv)
        pending()

    tiles_per_q = tq // tk
    n_chunks = 2 * tq // QUERY_CHUNK

    def full_tiles(it):
        return [(it * tiles_per_q + d, c, "full") for d in range(tiles_per_q) for c in range(n_chunks)]

    lax.fori_loop(0, i, lambda it, c: (step(full_tiles(it), SCORES_AHEAD_FULL), c)[1], 0)
    step([(i * tiles_per_q + d, c, kind) for d in range(tiles_per_q) for c, kind in _diag_chunks(tq, tk, d)],
         SCORES_AHEAD_DIAG)

    lam = (jnp.exp(jnp.sum(lq1_ref[...] * lk1_ref[...], axis=-1, keepdims=True))
           - jnp.exp(jnp.sum(lq2_ref[...] * lk2_ref[...], axis=-1, keepdims=True)) + lambda_init)
    acc = acc_ref[...]
    o2 = acc[0:DIFF_V_DIM, :] * (1.0 / acc[DIFF_V_DIM:DIFF_V_DIM + 1, :])
    o = (o2[:, :tq] - lam * o2[:, tq:]).T
    ms = jnp.mean(o * o, axis=-1, keepdims=True)
    o = o * lax.rsqrt(ms + EPS) * gain_ref[...] * (1.0 - lambda_init)
    o_ref[...] = o.astype(BF16)


def _diffattn(dq, dk, dvt, lam_q1, lam_k1, lam_q2, lam_k2, gain, batch, seq, lambda_init, tq, tk):
    nq = seq // tq
    q_spec = pl.BlockSpec((tq, LANES), lambda b, h, i: (b * nq + i, h))
    k_spec = pl.BlockSpec((seq, LANES), lambda b, h, i: (b, h))
    vt_spec = pl.BlockSpec((DIFF_V_DIM, seq), lambda b, h, i: (h, b))
    vec = lambda a: pl.BlockSpec(a.shape, lambda b, h, i: (0, 0))
    key = jnp.arange(tk)[:, None]
    query = QUERY_CHUNK + jnp.arange(QUERY_CHUNK)[None, :]
    bias = jnp.where(key <= query, 0.0, NEG_BIG).astype(F32)
    return pl.pallas_call(
        functools.partial(_diffattn_kernel, tq=tq, tk=tk, lambda_init=lambda_init),
        grid=(batch, DIFF_HEADS, nq),
        in_specs=[q_spec, k_spec, vt_spec, vec(lam_q1), vec(lam_k1), vec(lam_q2), vec(lam_k2), vec(gain),
                  vec(bias)],
        out_specs=q_spec,
        out_shape=jax.ShapeDtypeStruct(dq.shape, BF16),
        scratch_shapes=[
            pltpu.VMEM((2 * tq, LANES), BF16),
            pltpu.VMEM((seq // tk, V_EXT_ROWS, tk), BF16),
            pltpu.VMEM((1, 2 * tq), F32),
            pltpu.VMEM((V_EXT_ROWS, 2 * tq), F32),
        ],
        compiler_params=pltpu.CompilerParams(
            dimension_semantics=("parallel", "parallel", "arbitrary"), vmem_limit_bytes=VMEM_LIMIT),
        name="diffattn",
    )(dq, dk, dvt, lam_q1, lam_k1, lam_q2, lam_k2, gain, bias)


def _route(logits):
    r = [logits[g:g + 1, :] for g in range(N_GROUPS)]
    gmax = jnp.maximum(jnp.maximum(r[0], r[1]), jnp.maximum(r[2], r[3]))
    g_idx = jnp.where(r[0] == gmax, 0, jnp.where(r[1] == gmax, 1, jnp.where(r[2] == gmax, 2, 3)))
    denom = sum(jnp.exp(rg - gmax) for rg in r)
    g_weight = 1.0 / denom
    sel = jnp.zeros((EXPERTS_PER_GROUP, logits.shape[1]), F32)
    for g in range(N_GROUPS):
        rows = logits[8 + g * EXPERTS_PER_GROUP:8 + (g + 1) * EXPERTS_PER_GROUP, :]
        sel = jnp.where(g_idx == g, rows, sel)
    eidx = lax.broadcasted_iota(jnp.int32, sel.shape, 0)
    v1 = jnp.max(sel, axis=0, keepdims=True)
    i1 = jnp.min(jnp.where(sel == v1, eidx, EXPERTS_PER_GROUP), axis=0, keepdims=True)
    sel2 = jnp.where(eidx == i1, -jnp.inf, sel)
    v2 = jnp.max(sel2, axis=0, keepdims=True)
    i2 = jnp.min(jnp.where(sel2 == v2, eidx, EXPERTS_PER_GROUP), axis=0, keepdims=True)
    e2 = jnp.exp(v2 - v1)
    w1 = g_weight / (1.0 + e2)
    w2 = g_weight * e2 / (1.0 + e2)
    return g_idx, i1, i2, w1, w2


OUTPROJ_PARTS = 2


def _outproj_kernel(ret_ref, diff_ref, x_ref, mod_ref, gain_ref, wo_ref, wr_ref, br_ref, tri_ref,
                    x1_ref, h2_ref, ri_ref, rw_ref, cnt_ref, *, tiles_per_batch):
    b = pl.program_id(0) // tiles_per_batch
    gate1 = mod_ref[pl.ds(b, 1), 2 * D_MODEL:3 * D_MODEL]
    shift = mod_ref[pl.ds(b, 1), 3 * D_MODEL:4 * D_MODEL]
    scale = mod_ref[pl.ds(b, 1), 4 * D_MODEL:5 * D_MODEL]
    wr = wr_ref[...]
    tm = x_ref.shape[0]
    parts = [slice(n * tm // OUTPROJ_PARTS, (n + 1) * tm // OUTPROJ_PARTS) for n in range(OUTPROJ_PARTS)]
    mix = [_dot(ret_ref[r, :], wo_ref[0:RET_WIDTH, :]) + _dot(diff_ref[r, :], wo_ref[RET_WIDTH:, :]) for r in parts]
    x1 = [x_ref[r, :] + gate1 * m for r, m in zip(parts, mix)]
    for r, v in zip(parts, x1):
        x1_ref[r, :] = v
    h_split = [_split_bf16(_norm_modulate(v, gain_ref[...], shift, scale)) for v in x1]
    for r, (h_hi, _) in zip(parts, h_split):
        h2_ref[r, :] = h_hi
    by_hi = [_dot_nt(wr, h_hi) for h_hi, _ in h_split]
    by_lo = [_dot_nt(wr[:ROUTER_ROWS], h_lo) for _, h_lo in h_split]
    logits = [a[:ROUTER_ROWS] + a[ROUTER_ROWS:] + c + br_ref[...] for a, c in zip(by_hi, by_lo)]
    routed = [_route(lg) for lg in logits]
    g_idx, i1, i2, w1, w2 = [jnp.concatenate([rt[n] for rt in routed], axis=1) for n in range(5)]
    logits = jnp.concatenate(logits, axis=1)
    e1 = g_idx * EXPERTS_PER_GROUP + i1
    e2 = g_idx * EXPERTS_PER_GROUP + i2
    eidx = lax.broadcasted_iota(jnp.int32, (N_EXPERTS, logits.shape[1]), 0)
    hit1 = eidx == e1
    hit2 = eidx == e2
    onehot = jnp.where(hit1 | hit2, 1.0, 0.0)
    before = _dot(onehot.astype(BF16), tri_ref[...])
    r1 = jnp.sum(jnp.where(hit1, before, 0.0), axis=0, keepdims=True)
    r2 = jnp.sum(jnp.where(hit2, before, 0.0), axis=0, keepdims=True)
    zi = jnp.zeros_like(e1)
    ri_ref[...] = jnp.concatenate([e1, e2, r1.astype(jnp.int32), r2.astype(jnp.int32), zi, zi, zi, zi], axis=0)
    zf = jnp.zeros_like(w1)
    rw_ref[...] = jnp.concatenate([w1, w2, zf, zf, zf, zf, zf, zf], axis=0)
    counts = jnp.sum(onehot, axis=1, keepdims=True)
    cnt_ref[0] = jnp.broadcast_to(counts, (N_EXPERTS, LANES)).astype(jnp.int32)


def _outproj(ret_out, diff_out, x2, mod, gain, w_out, wr, br, seq, tm):
    tokens = x2.shape[0]
    tiles_per_batch = seq // tm
    n_tiles = tokens // tm
    tri = (jnp.arange(tm)[:, None] < jnp.arange(tm)[None, :]).astype(BF16)
    tok_spec = lambda w: pl.BlockSpec((tm, w), lambda i: (i, 0))
    row_spec = pl.BlockSpec((8, tm), lambda i: (0, i))
    full = lambda a: pl.BlockSpec(a.shape, lambda i: (0,) * a.ndim)
    return pl.pallas_call(
        functools.partial(_outproj_kernel, tiles_per_batch=tiles_per_batch),
        grid=(n_tiles,),
        in_specs=[tok_spec(RET_WIDTH), tok_spec(DIFF_WIDTH), tok_spec(D_MODEL), full(mod), full(gain),
                  full(w_out), full(wr), full(br), full(tri)],
        out_specs=[tok_spec(D_MODEL), tok_spec(D_MODEL), row_spec, row_spec,
                   pl.BlockSpec((1, N_EXPERTS, LANES), lambda i: (i, 0, 0))],
        out_shape=[jax.ShapeDtypeStruct((tokens, D_MODEL), F32),
                   jax.ShapeDtypeStruct((tokens, D_MODEL), BF16),
                   jax.ShapeDtypeStruct((8, tokens), jnp.int32),
                   jax.ShapeDtypeStruct((8, tokens), F32),
                   jax.ShapeDtypeStruct((n_tiles, N_EXPERTS, LANES), jnp.int32)],
        compiler_params=pltpu.CompilerParams(
            dimension_semantics=("parallel",), vmem_limit_bytes=VMEM_LIMIT),
        name="outproj",
    )(ret_out, diff_out, x2, mod, gain, w_out, wr, br, tri)


CHUNK = 8
TMX = 512


def _local_rows(tm):
    rows = 2 * tm + N_EXPERTS * (CHUNK - 1)
    return (rows + 15) // 16 * 16


def _sorted_rows_alloc(tokens, tm):
    worst = 2 * tokens + (tokens // tm) * N_EXPERTS * (CHUNK - 1) + N_EXPERTS * (TMX - CHUNK)
    return (worst + TMX - 1) // TMX * TMX


def _dispatch_plan(cnt, tokens, tm):
    i32 = jnp.int32
    nch_max = _local_rows(tm) // CHUNK
    pad = (cnt + CHUNK - 1) // CHUNK * CHUNK
    local_end = jnp.cumsum(pad, axis=1)
    local_start = local_end - pad
    seg_rows = jnp.sum(pad, axis=0)
    seg_pad = (seg_rows + TMX - 1) // TMX * TMX
    seg_end = jnp.cumsum(seg_pad)
    seg_start = seg_end - seg_pad
    run_dst = seg_start[None, :] + jnp.cumsum(pad, axis=0) - pad
    row = CHUNK * jnp.arange(nch_max, dtype=i32)[None, :, None]
    owns = (local_start[:, None, :] <= row) & (row < local_end[:, None, :])
    chunk_dst = row[:, :, 0] + jnp.sum(jnp.where(owns, (run_dst - local_start)[:, None, :], 0), axis=-1)
    m = TMX * jnp.arange(_sorted_rows_alloc(tokens, tm) // TMX, dtype=i32)
    tile_expert = jnp.minimum(jnp.sum(seg_end[None, :] <= m[:, None], axis=-1), N_EXPERTS - 1)
    towns = (seg_start[None, :] <= m[:, None]) & (m[:, None] < seg_end[None, :])
    used = seg_pad > 0
    parity = (jnp.cumsum(used) - used) % 2
    eids = jnp.arange(N_EXPERTS, dtype=i32)
    later_used = (eids[None, :] > eids[:, None]) & used[None, :]
    next_used = jnp.min(jnp.where(later_used, eids[None, :], N_EXPERTS), axis=1)
    next_used = jnp.where(next_used == N_EXPERTS, -1, next_used)
    pick = lambda per_expert: jnp.sum(jnp.where(towns, per_expert[None, :], 0), axis=-1)
    tile_first = jnp.sum(jnp.where(towns & (seg_start[None, :] == m[:, None]), 1, 0), axis=-1)
    tile_next = jnp.where(jnp.any(towns, axis=-1), pick(next_used), -1)
    return dict(
        tile_first=tile_first.astype(i32),
        tile_slot=pick(parity).astype(i32),
        tile_next=tile_next.astype(i32),
        local_start=local_start.reshape(-1).astype(i32),
        n_chunks=(local_end[:, -1] // CHUNK).astype(i32),
        chunk_dst=chunk_dst.reshape(-1).astype(i32),
        tail_base=(seg_start + seg_rows).astype(i32),
        tail_rows=(seg_pad - seg_rows).astype(i32),
        tile_expert=tile_expert.astype(i32),
        n_used=(seg_end[-1:] // TMX).astype(i32),
    )


WAIT_UNROLL = 8


def _wait_times(copy, n):
    lax.fori_loop(0, n // WAIT_UNROLL, lambda i, c: ([copy.wait() for _ in range(WAIT_UNROLL)], c)[1], 0)
    lax.fori_loop(0, n % WAIT_UNROLL, lambda i, c: (copy.wait(), c)[1], 0)


def _for_each(n, body, unroll=4):
    main = n // unroll
    lax.fori_loop(0, main, lambda i, c: ([body(i * unroll + u) for u in range(unroll)], c)[1], 0)
    lax.fori_loop(main * unroll, n, lambda j, c: (body(j), c)[1], 0)


def _local_slots(ri_ref, local_start_ref, tile):
    e1, e2 = ri_ref[0:1, :], ri_ref[1:2, :]
    s1, s2 = ri_ref[2:3, :], ri_ref[3:4, :]
    for e in range(N_EXPERTS):
        start = local_start_ref[tile * N_EXPERTS + e]
        s1 = s1 + jnp.where(e1 == e, start, 0)
        s2 = s2 + jnp.where(e2 == e, start, 0)
    return s1, s2


def _dispatch_kernel(local_start_ref, n_chunks_ref, chunk_dst_ref, tail_base_ref, tail_rows_ref, n_used_ref,
                     h_ref, ri_ref, xs_ref, buf_ref, zero_ref, sem_ref, tail_sem_ref, *, r_loc):
    b = pl.program_id(0)
    nb = pl.num_programs(0)
    slot = b % 2
    nch_max = r_loc // CHUNK

    def chunk_copy(tile, sl, j):
        row = pl.multiple_of(j * CHUNK, CHUNK)
        dst = pl.multiple_of(chunk_dst_ref[tile * nch_max + j], CHUNK)
        return pltpu.make_async_copy(buf_ref.at[sl, pl.ds(row, CHUNK), :], xs_ref.at[pl.ds(dst, CHUNK), :],
                                     sem_ref.at[sl])

    def drain(tile, sl):
        one_chunk = pltpu.make_async_copy(buf_ref.at[sl, pl.ds(0, CHUNK), :], xs_ref.at[pl.ds(0, CHUNK), :],
                                          sem_ref.at[sl])
        _wait_times(one_chunk, n_chunks_ref[tile])

    @pl.when(b >= 2)
    def _():
        drain(b - 2, slot)

    s1, s2 = _local_slots(ri_ref, local_start_ref, b)
    rows = lax.broadcasted_iota(jnp.int32, (r_loc, s1.shape[1]), 0)
    perm = jnp.where((rows == s1) | (rows == s2), 1.0, 0.0).astype(BF16)
    buf_ref[slot] = _dot(perm, h_ref[...])
    _for_each(n_chunks_ref[b], lambda j: chunk_copy(b, slot, j).start())

    def tail_pieces(e, act):
        n = tail_rows_ref[e]
        size = TMX // 2
        while size >= CHUNK:
            dst = pl.multiple_of(tail_base_ref[e] + (n & (-2 * size)), CHUNK)
            cp = pltpu.make_async_copy(zero_ref.at[pl.ds(0, size), :], xs_ref.at[pl.ds(dst, size), :],
                                       tail_sem_ref.at[0])
            pl.when((n & size) != 0)(functools.partial(act, cp))
            size //= 2

    def unused_tile_copy(m):
        dst = pl.multiple_of(m * TMX, TMX)
        return pltpu.make_async_copy(zero_ref, xs_ref.at[pl.ds(dst, TMX), :], tail_sem_ref.at[1])

    n_alloc = xs_ref.shape[0] // TMX

    @pl.when(b == 0)
    def _():
        zero_ref[...] = jnp.zeros_like(zero_ref)
        lax.fori_loop(0, N_EXPERTS, lambda e, c: (tail_pieces(e, lambda cp: cp.start()), c)[1], 0)
        lax.fori_loop(n_used_ref[0], n_alloc, lambda m, c: (unused_tile_copy(m).start(), c)[1], 0)

    @pl.when(b == nb - 1)
    def _():
        lax.fori_loop(0, N_EXPERTS, lambda e, c: (tail_pieces(e, lambda cp: cp.wait()), c)[1], 0)
        lax.fori_loop(n_used_ref[0], n_alloc, lambda m, c: (unused_tile_copy(m).wait(), c)[1], 0)

        @pl.when(b >= 1)
        def _():
            drain(b - 1, 1 - slot)

        drain(b, slot)


def _dispatch(h2, ri, plan, tm):
    tokens = h2.shape[0]
    r_loc = _local_rows(tm)
    grid_spec = pltpu.PrefetchScalarGridSpec(
        num_scalar_prefetch=6,
        grid=(tokens // tm,),
        in_specs=[pl.BlockSpec((tm, D_MODEL), lambda i, *_: (i, 0)),
                  pl.BlockSpec((8, tm), lambda i, *_: (0, i))],
        out_specs=pl.BlockSpec(memory_space=pl.ANY),
        scratch_shapes=[pltpu.VMEM((2, r_loc, D_MODEL), F32), pltpu.VMEM((TMX, D_MODEL), F32),
                        pltpu.SemaphoreType.DMA((2,)), pltpu.SemaphoreType.DMA((2,))],
    )
    return pl.pallas_call(
        functools.partial(_dispatch_kernel, r_loc=r_loc),
        grid_spec=grid_spec,
        out_shape=jax.ShapeDtypeStruct((_sorted_rows_alloc(tokens, tm), D_MODEL), F32),
        compiler_params=pltpu.CompilerParams(
            dimension_semantics=("arbitrary",), vmem_limit_bytes=VMEM_LIMIT),
        name="dispatch",
    )(plan["local_start"], plan["n_chunks"], plan["chunk_dst"], plan["tail_base"], plan["tail_rows"], plan["n_used"],
      h2, ri)


def _experts_kernel(tile_expert_ref, n_used_ref, first_ref, slot_ref, next_ref, xs_ref, wg_hbm, wu_hbm, wd_hbm,
                    ys_ref, wg_st, wu_st, wd_st, wg_bf, wu_bf, wd_bf, sem_ref):
    m = pl.program_id(0)

    def weight_copies(e, s):
        return [pltpu.make_async_copy(src.at[e], dst.at[s], sem_ref.at[s, n])
                for n, (src, dst) in enumerate([(wg_hbm, wg_st), (wu_hbm, wu_st), (wd_hbm, wd_st)])]

    @pl.when(m < n_used_ref[0])
    def _():
        @pl.when(first_ref[m] == 1)
        def _():
            s = slot_ref[m]

            @pl.when(m == 0)
            def _():
                for cp in weight_copies(tile_expert_ref[0], 0):
                    cp.start()

            for cp in weight_copies(tile_expert_ref[m], s):
                cp.wait()

            @pl.when(next_ref[m] >= 0)
            def _():
                for cp in weight_copies(next_ref[m], 1 - s):
                    cp.start()

            wg_bf[...] = wg_st[s].astype(BF16)
            wu_bf[...] = wu_st[s].astype(BF16)
            wd_bf[...] = wd_st[s].astype(BF16)

        x = xs_ref[...].astype(BF16)
        a = _dot(x, wg_bf[...])
        u = _dot(x, wu_bf[...])
        hid = (_silu(a) * u).astype(BF16)
        ys_ref[...] = _dot(hid, wd_bf[...])


def _experts(xs, plan, wg, wu, wd):
    n_tiles = xs.shape[0] // TMX
    last_used = lambda m, n_used: jnp.minimum(m, n_used[0] - 1)
    row_spec = pl.BlockSpec((TMX, D_MODEL), lambda m, te, nu, *_: (last_used(m, nu), 0))
    hbm = pl.BlockSpec(memory_space=pl.ANY)
    up_shape, down_shape = (D_MODEL, D_EXPERT), (D_EXPERT, D_MODEL)
    grid_spec = pltpu.PrefetchScalarGridSpec(
        num_scalar_prefetch=5,
        grid=(n_tiles,),
        in_specs=[row_spec, hbm, hbm, hbm],
        out_specs=row_spec,
        scratch_shapes=[pltpu.VMEM((2,) + up_shape, F32), pltpu.VMEM((2,) + up_shape, F32),
                        pltpu.VMEM((2,) + down_shape, F32),
                        pltpu.VMEM(up_shape, BF16), pltpu.VMEM(up_shape, BF16), pltpu.VMEM(down_shape, BF16),
                        pltpu.SemaphoreType.DMA((2, 3))],
    )
    return pl.pallas_call(
        _experts_kernel,
        grid_spec=grid_spec,
        out_shape=jax.ShapeDtypeStruct(xs.shape, F32),
        input_output_aliases={5: 0},
        compiler_params=pltpu.CompilerParams(
            dimension_semantics=("arbitrary",), vmem_limit_bytes=VMEM_LIMIT),
        name="experts",
    )(plan["tile_expert"], plan["n_used"], plan["tile_first"], plan["tile_slot"], plan["tile_next"], xs, wg, wu, wd)


def _combine_kernel(local_start_ref, n_chunks_ref, chunk_dst_ref, ys_ref, ri_ref, rw_ref, x1_ref, mod_ref,
                    gain_ref, o_ref, buf_ref, sem_ref, *, r_loc, tiles_per_batch):
    b = pl.program_id(0)
    nb = pl.num_programs(0)
    slot = b % 2
    nch_max = r_loc // CHUNK

    def chunk_copy(tile, sl, j):
        row = pl.multiple_of(j * CHUNK, CHUNK)
        src = pl.multiple_of(chunk_dst_ref[tile * nch_max + j], CHUNK)
        return pltpu.make_async_copy(ys_ref.at[pl.ds(src, CHUNK), :], buf_ref.at[sl, pl.ds(row, CHUNK), :],
                                     sem_ref.at[sl])

    def fetch(tile, sl):
        _for_each(n_chunks_ref[tile], lambda j: chunk_copy(tile, sl, j).start())

    @pl.when(b == 0)
    def _():
        buf_ref[...] = jnp.zeros_like(buf_ref)
        fetch(0, 0)

    @pl.when(b + 1 < nb)
    def _():
        fetch(b + 1, 1 - slot)

    one_chunk = pltpu.make_async_copy(ys_ref.at[pl.ds(0, CHUNK), :], buf_ref.at[slot, pl.ds(0, CHUNK), :],
                                      sem_ref.at[slot])
    _wait_times(one_chunk, n_chunks_ref[b])

    s1, s2 = _local_slots(ri_ref, local_start_ref, b)
    rows = lax.broadcasted_iota(jnp.int32, (r_loc, s1.shape[1]), 0)
    hit1 = rows == s1
    hit2 = rows == s2
    w_row = jnp.sum(jnp.where(hit1, rw_ref[0:1, :], jnp.where(hit2, rw_ref[1:2, :], 0.0)), axis=1, keepdims=True)
    perm = jnp.where(hit1 | hit2, 1.0, 0.0).astype(BF16)
    yw = (buf_ref[slot] * w_row).astype(BF16)
    moe = _dot_tn(perm, yw)
    batch = b // tiles_per_batch
    gate2 = mod_ref[pl.ds(batch, 1), 5 * D_MODEL:6 * D_MODEL]
    x2 = x1_ref[...] + gate2 * moe
    ms = jnp.mean(x2 * x2, axis=-1, keepdims=True)
    o_ref[...] = x2 * lax.rsqrt(ms + EPS) * gain_ref[...]


def _combine(ys, ri, rw, x1, mod, gain, plan, seq, tm):
    tokens = x1.shape[0]
    r_loc = _local_rows(tm)
    row_spec = pl.BlockSpec((8, tm), lambda i, *_: (0, i))
    tok_spec = pl.BlockSpec((tm, D_MODEL), lambda i, *_: (i, 0))
    full = lambda a: pl.BlockSpec(a.shape, lambda i, *_: (0,) * a.ndim)
    grid_spec = pltpu.PrefetchScalarGridSpec(
        num_scalar_prefetch=3,
        grid=(tokens // tm,),
        in_specs=[pl.BlockSpec(memory_space=pl.ANY), row_spec, row_spec, tok_spec, full(mod), full(gain)],
        out_specs=tok_spec,
        scratch_shapes=[pltpu.VMEM((2, r_loc, D_MODEL), F32), pltpu.SemaphoreType.DMA((2,))],
    )
    return pl.pallas_call(
        functools.partial(_combine_kernel, r_loc=r_loc, tiles_per_batch=seq // tm),
        grid_spec=grid_spec,
        out_shape=jax.ShapeDtypeStruct((tokens, D_MODEL), F32),
        compiler_params=pltpu.CompilerParams(
            dimension_semantics=("arbitrary",), vmem_limit_bytes=VMEM_LIMIT),
        name="combine",
    )(plan["local_start"], plan["n_chunks"], plan["chunk_dst"], ys, ri, rw, x1, mod, gain)


def _rotary_tables(seq):
    half = RET_HEAD_DIM // 2
    inv_freq = 1.0 / (ROPE_BASE ** (jnp.arange(half, dtype=F32) / half))
    ang = jnp.arange(seq).astype(F32)[:, None] * inv_freq[None, :]
    cos = jnp.cos(ang)
    sin = jnp.sin(ang)
    return jnp.tile(cos, (1, 4)), jnp.concatenate([-sin, sin, -sin, sin], axis=1)


def _pick_tile(n, pref):
    t = min(n, pref)
    assert n % t == 0, (n, t)
    return t


def kernel(x, c, ada_w, ada_b, norm1_gain, norm2_gain, w_in, w_out, ret_gn_gain, lam_q1, lam_k1, lam_q2,
           lam_k2, diff_subln_gain, w_group, b_group, w_expert, b_expert, w_gate, w_up, w_down, final_gain):
    batch, seq, d = x.shape
    assert d == D_MODEL and batch <= 8 and ada_w.shape[0] == 1
    layer = 0
    lambda_init = 0.8 - 0.6 * math.exp(-0.3 * layer)
    tokens = batch * seq
    x2 = x.reshape(tokens, d)
    tm = _pick_tile(seq, 512)

    c_pad = jnp.zeros((8, d), F32).at[:batch].set(c)
    mod = _adaln(c_pad, ada_w[layer], ada_b[layer].reshape(1, -1))

    cos_t, sin_t = _rotary_tables(seq)
    rq, rk, rv, rg, dq, dk, dvt = _inproj(
        x2, mod, norm1_gain[layer].reshape(1, d), w_in[layer].astype(BF16), cos_t, sin_t, seq, tm)

    ret_out = _retention(rq, rk, rv, rg, ret_gn_gain[layer].reshape(1, RET_WIDTH), batch, seq,
                         _pick_tile(seq, 256))
    diff_out = _diffattn(
        dq, dk, dvt, lam_q1[layer].reshape(1, -1), lam_k1[layer].reshape(1, -1), lam_q2[layer].reshape(1, -1),
        lam_k2[layer].reshape(1, -1), diff_subln_gain[layer].reshape(1, -1), batch, seq, lambda_init,
        _pick_tile(seq, 1024), 2 * QUERY_CHUNK)

    w_router = jnp.concatenate(
        [w_group[layer].T, jnp.zeros((8 - N_GROUPS, d), F32), w_expert[layer].reshape(d, N_EXPERTS).T], axis=0)
    b_router = jnp.concatenate(
        [b_group[layer], jnp.zeros((8 - N_GROUPS,), F32), b_expert[layer].reshape(N_EXPERTS)]).reshape(-1, 1)
    wr_hi = w_router.astype(BF16)
    wr_lo = (w_router - wr_hi.astype(F32)).astype(BF16)
    x1, h2, ri, rw, cnt = _outproj(ret_out, diff_out, x2, mod, norm2_gain[layer].reshape(1, d),
                                   w_out[layer].astype(BF16), jnp.concatenate([wr_hi, wr_lo], axis=0), b_router,
                                   seq, tm)

    plan = _dispatch_plan(cnt[:, :, 0], tokens, tm)
    xs = _dispatch(h2, ri, plan, tm)
    ys = _experts(xs, plan, w_gate[layer].reshape(N_EXPERTS, d, D_EXPERT),
                  w_up[layer].reshape(N_EXPERTS, d, D_EXPERT), w_down[layer].reshape(N_EXPERTS, D_EXPERT, d))
    out = _combine(ys, ri, rw, x1, mod, final_gain.reshape(1, d), plan, seq, tm)
    return out.reshape(batch, seq, d)
```

```python
import functools
import math

import jax
import jax.numpy as jnp
from jax import lax
from jax.experimental import pallas as pl
from jax.experimental.pallas import tpu as pltpu

F32 = jnp.float32
BF16 = jnp.bfloat16

D_MODEL = 1024
RET_HEAD_DIM = 64
RET_WIDTH = 512
RET_HEADS = 8
RET_PAIRS = RET_HEADS // 2
DIFF_QK_DIM = 64
DIFF_V_DIM = 128
DIFF_HEADS = 4
DIFF_WIDTH = 512
N_GROUPS = 4
EXPERTS_PER_GROUP = 8
N_EXPERTS = N_GROUPS * EXPERTS_PER_GROUP
D_EXPERT = 512
N_MOD = 6
ROPE_BASE = 10000.0
EPS = 1e-6
LANES = 128
ROUTER_ROWS = 8 + N_EXPERTS
VMEM_LIMIT = 56 * 1024 * 1024


def _dot(a, b):
    return jnp.dot(a, b, preferred_element_type=F32)


def _dot_nt(a, b):
    return lax.dot_general(a, b, (((1,), (1,)), ((), ())), preferred_element_type=F32)


def _dot_tn(a, b):
    return lax.dot_general(a, b, (((0,), (0,)), ((), ())), preferred_element_type=F32)


def _split_bf16(x):
    hi = x.astype(BF16)
    lo = (x - hi.astype(F32)).astype(BF16)
    return hi, lo


def _silu(x):
    return x / (1.0 + jnp.exp(-x))


def _adaln_kernel(c_ref, w_ref, b_ref, o_ref):
    ca = _silu(c_ref[...])
    c_hi, c_lo = _split_bf16(ca)
    w_hi, w_lo = _split_bf16(w_ref[...])
    o_ref[...] = _dot(c_hi, w_hi) + _dot(c_lo, w_hi) + _dot(c_hi, w_lo) + b_ref[...]


def _adaln(c_pad, ada_w, ada_b):
    n_out = ada_w.shape[1]
    tn = D_MODEL
    return pl.pallas_call(
        _adaln_kernel,
        grid=(n_out // tn,),
        in_specs=[
            pl.BlockSpec((8, D_MODEL), lambda j: (0, 0)),
            pl.BlockSpec((D_MODEL, tn), lambda j: (0, j)),
            pl.BlockSpec((1, tn), lambda j: (0, j)),
        ],
        out_specs=pl.BlockSpec((8, tn), lambda j: (0, j)),
        out_shape=jax.ShapeDtypeStruct((8, n_out), F32),
        compiler_params=pltpu.CompilerParams(vmem_limit_bytes=VMEM_LIMIT),
        name="adaln",
    )(c_pad, ada_w, ada_b)


def _norm_modulate(x, gain, shift, scale):
    ms = jnp.mean(x * x, axis=-1, keepdims=True)
    y = x * lax.rsqrt(ms + EPS) * gain
    return y * (1.0 + scale) + shift


def _rotary_slab(x, cos, sin_signed, lane_lo):
    swapped = jnp.where(lane_lo, pltpu.roll(x, 96, 1), pltpu.roll(x, 32, 1))
    return x * cos + swapped * sin_signed


def _inproj_kernel(x_ref, mod_ref, gain_ref, w_ref, cos_ref, sin_ref,
                   rq_ref, rk_ref, rv_ref, rg_ref, dq_ref, dk_ref, dvt_ref, *, tiles_per_batch):
    b = pl.program_id(0) // tiles_per_batch
    shift = mod_ref[pl.ds(b, 1), 0:D_MODEL]
    scale = mod_ref[pl.ds(b, 1), D_MODEL:2 * D_MODEL]
    h = _norm_modulate(x_ref[...], gain_ref[...], shift, scale).astype(BF16)
    cos = cos_ref[...]
    sin = sin_ref[...]
    lane = lax.broadcasted_iota(jnp.int32, cos.shape, 1)
    lane_lo = (lane % 64) < 32

    def proj(chunk):
        return _dot(h, w_ref[:, chunk * RET_WIDTH:(chunk + 1) * RET_WIDTH])

    def rotary(acc, out_ref, post_scale):
        for s in range(RET_WIDTH // LANES):
            sl = slice(s * LANES, (s + 1) * LANES)
            out_ref[:, sl] = (_rotary_slab(acc[:, sl], cos, sin, lane_lo) * post_scale).astype(BF16)

    rotary(proj(0), rq_ref, 1.0)
    rotary(proj(1), rk_ref, RET_HEAD_DIM ** -0.5)
    rv_ref[...] = proj(2).astype(BF16)
    rg_ref[...] = _silu(proj(3)).astype(BF16)
    dq_ref[...] = (proj(4) * (DIFF_QK_DIM ** -0.5 * math.log2(math.e))).astype(BF16)
    dk_ref[...] = proj(5).astype(BF16)
    dvt_ref[...] = proj(6).T.astype(BF16)


def _inproj(x2, mod, gain, w_in, cos_t, sin_t, seq, tm):
    tokens = x2.shape[0]
    tiles_per_batch = seq // tm
    tok_spec = lambda w: pl.BlockSpec((tm, w), lambda i: (i, 0))
    tab_spec = pl.BlockSpec((tm, LANES), lambda i: (i % tiles_per_batch, 0))
    full = lambda a: pl.BlockSpec(a.shape, lambda i: (0,) * a.ndim)
    out = jax.ShapeDtypeStruct((tokens, RET_WIDTH), BF16)
    return pl.pallas_call(
        functools.partial(_inproj_kernel, tiles_per_batch=tiles_per_batch),
        grid=(tokens // tm,),
        in_specs=[tok_spec(D_MODEL), full(mod), full(gain), full(w_in), tab_spec, tab_spec],
        out_specs=[tok_spec(RET_WIDTH)] * 6 + [pl.BlockSpec((DIFF_WIDTH, tm), lambda i: (0, i))],
        out_shape=[out] * 6 + [jax.ShapeDtypeStruct((DIFF_WIDTH, tokens), BF16)],
        compiler_params=pltpu.CompilerParams(
            dimension_semantics=("parallel",), vmem_limit_bytes=VMEM_LIMIT),
        name="inproj",
    )(x2, mod, gain, w_in, cos_t, sin_t)


def _retention_kernel(q_ref, k_ref, v_ref, g_ref, dec_ref, qdec_ref, kdec_ref, rdec_ref,
                      bmask_ref, gmean_ref, gain_ref, o_ref, state_ref, *, chunk):
    @pl.when(pl.program_id(1) == 0)
    def _():
        state_ref[...] = jnp.zeros_like(state_ref)

    lane = lax.broadcasted_iota(jnp.int32, (chunk, LANES), 1)
    first_head = lane < RET_HEAD_DIM
    gmean = gmean_ref[...]
    bmask = bmask_ref[...]
    pairs = range(RET_PAIRS)
    sl = [slice(p * LANES, (p + 1) * LANES) for p in pairs]
    q = [q_ref[:, sl[p]] for p in pairs]
    k = [k_ref[:, sl[p]] for p in pairs]
    v = [v_ref[:, sl[p]] for p in pairs]
    zero = jnp.zeros_like(q[0])
    q_stack = [jnp.concatenate([jnp.where(first_head, q[p], zero), jnp.where(first_head, zero, q[p])], axis=0)
               for p in pairs]
    scores = [(_dot_nt(q_stack[p], k[p]) * dec_ref[p]).astype(BF16) for p in pairs]
    state = [state_ref[p] for p in pairs]
    cross = [_dot(q[p], state[p].astype(BF16)) * qdec_ref[:, sl[p]] for p in pairs]
    k_dec = [(k[p].astype(F32) * kdec_ref[:, sl[p]]).astype(BF16) for p in pairs]
    for p in pairs:
        state_ref[p] = state[p] * rdec_ref[p] + _dot_tn(k_dec[p], v[p]) * bmask
    intra2 = [_dot(scores[p], v[p]) for p in pairs]
    y = [jnp.where(first_head, intra2[p][:chunk], intra2[p][chunk:]) + cross[p] for p in pairs]
    y_split = [_split_bf16(y[p]) for p in pairs]
    mu = [_dot(y_split[p][0], gmean) + _dot(y_split[p][1], gmean) for p in pairs]
    d = [y[p] - mu[p] for p in pairs]
    d_split = [_split_bf16(d[p] * d[p]) for p in pairs]
    var = [_dot(d_split[p][0], gmean) + _dot(d_split[p][1], gmean) for p in pairs]
    for p in pairs:
        yn = d[p] * lax.rsqrt(var[p] + EPS) * gain_ref[:, sl[p]]
        o_ref[:, sl[p]] = (g_ref[:, sl[p]].astype(F32) * yn).astype(BF16)


def _retention_tables(chunk):
    heads = jnp.arange(RET_HEADS, dtype=F32)
    log_gamma = jnp.log(1.0 - jnp.exp2(-5.0 - heads))
    idx = jnp.arange(chunk)
    rel = (idx[:, None] - idx[None, :]).astype(F32)
    decay = jnp.where(rel[None] >= 0, jnp.exp(log_gamma[:, None, None] * jnp.maximum(rel, 0.0)[None]), 0.0)
    dec2 = decay.reshape(RET_PAIRS, 2 * chunk, chunk)
    lane_lg = jnp.repeat(log_gamma, RET_HEAD_DIM)
    qdec = jnp.exp(lane_lg[None, :] * (idx + 1).astype(F32)[:, None])
    kdec = jnp.exp(lane_lg[None, :] * (chunk - 1 - idx).astype(F32)[:, None])
    rdec = jnp.exp(lane_lg * chunk).reshape(RET_PAIRS, LANES, 1) * jnp.ones((1, 1, LANES), F32)
    blk = jnp.arange(LANES) // RET_HEAD_DIM
    bmask = (blk[:, None] == blk[None, :]).astype(F32)
    gmean = (bmask / RET_HEAD_DIM).astype(BF16)
    return dec2, qdec, kdec, rdec, bmask, gmean


def _retention(rq, rk, rv, rg, gn_gain, batch, seq, chunk):
    nc = seq // chunk
    dec2, qdec, kdec, rdec, bmask, gmean = _retention_tables(chunk)
    tok_spec = pl.BlockSpec((chunk, RET_WIDTH), lambda b, n: (b * nc + n, 0))
    full = lambda a: pl.BlockSpec(a.shape, lambda b, n: (0,) * a.ndim)
    return pl.pallas_call(
        functools.partial(_retention_kernel, chunk=chunk),
        grid=(batch, nc),
        in_specs=[tok_spec] * 4 + [full(dec2), full(qdec), full(kdec), full(rdec), full(bmask),
                                   full(gmean), full(gn_gain)],
        out_specs=tok_spec,
        out_shape=jax.ShapeDtypeStruct(rq.shape, BF16),
        scratch_shapes=[pltpu.VMEM((RET_PAIRS, LANES, LANES), F32)],
        compiler_params=pltpu.CompilerParams(
            dimension_semantics=("parallel", "arbitrary"), vmem_limit_bytes=VMEM_LIMIT),
        name="retention",
    )(rq, rk, rv, rg, dec2, qdec, kdec, rdec, bmask, gmean, gn_gain)


NEG_BIG = -1e30


V_EXT_ROWS = DIFF_V_DIM + 16
QUERY_CHUNK = 256
SCORES_AHEAD_FULL = 3
SCORES_AHEAD_DIAG = 3


def _diag_chunks(tq, tk, d):
    assert tk == 2 * QUERY_CHUNK
    per_softmax = tq // QUERY_CHUNK
    out = []
    for c in range(2 * per_softmax):
        q0 = (c % per_softmax) * QUERY_CHUNK
        if q0 + QUERY_CHUNK - 1 < d * tk:
            continue
        kind = "full" if q0 >= (d + 1) * tk else ("tri" if q0 == d * tk else "low_tri")
        out.append((c, kind))
    return out


def _accumulate(acc_ref, cs, alpha, pv):
    acc_ref[:, cs] = alpha * acc_ref[:, cs] + pv


def _diffattn_kernel(q_ref, k_ref, vt_ref, lq1_ref, lk1_ref, lq2_ref, lk2_ref, gain_ref, bias_ref, o_ref,
                     qs_ref, vext_ref, m_ref, acc_ref, stage_ref, *, tq, tk, lambda_init):
    i = pl.program_id(2)
    nk = vext_ref.shape[0]

    @pl.when(i == 0)
    def _():
        for j in range(nk):
            vext_ref[j, 0:DIFF_V_DIM, :] = vt_ref[:, j * tk:(j + 1) * tk]
            vext_ref[j, DIFF_V_DIM:V_EXT_ROWS, :] = jnp.ones((V_EXT_ROWS - DIFF_V_DIM, tk), BF16)

    q = q_ref[...]
    lane = lax.broadcasted_iota(jnp.int32, q.shape, 1)
    zero = jnp.zeros_like(q)
    qs_ref[0:tq, :] = jnp.where(lane < DIFF_QK_DIM, q, zero)
    qs_ref[tq:2 * tq, :] = jnp.where(lane < DIFF_QK_DIM, zero, q)
    m_ref[...] = jnp.full_like(m_ref, NEG_BIG)
    acc_ref[...] = jnp.zeros_like(acc_ref)

    def step(work, n_ahead):
        chunk = lambda c: slice(c * QUERY_CHUNK, (c + 1) * QUERY_CHUNK)

        def scores(j, c, kind):
            n_keys = QUERY_CHUNK if kind == "tri" else tk
            start = pl.multiple_of(j * tk, tk)
            return _dot_nt(k_ref[pl.ds(start, n_keys), :], qs_ref[chunk(c), :])

        ahead = [scores(*work[n]) for n in range(min(n_ahead, len(work)))]
        pending = None
        masked_seen = 0
        for n, (j, c, kind) in enumerate(work):
            cs = chunk(c)
            st = ahead.pop(0)
            if n + n_ahead < len(work):
                ahead.append(scores(*work[n + n_ahead]))
            if kind != "full":
                slot = masked_seen
                masked_seen += 1
                n_keys = st.shape[0]
                stage_ref[slot, 0:n_keys, :] = st
                causal = stage_ref[slot, n_keys - QUERY_CHUNK:n_keys, :] + bias_ref[...]
                st = causal if kind == "tri" else jnp.concatenate(
                    [stage_ref[slot, 0:n_keys - QUERY_CHUNK, :], causal], axis=0)
            m_old = m_ref[:, cs]
            m_new = jnp.maximum(m_old, jnp.max(st, axis=0, keepdims=True))
            alpha = jnp.exp2(m_old - m_new)
            p = jnp.exp2(st - m_new).astype(BF16)
            m_ref[:, cs] = m_new
            pv = _dot(vext_ref[j, :, 0:st.shape[0]], p)
            if pending is not None:
                pending()
            pending = functools.partial(_accumulate, acc_ref, cs, alpha, pv)
        pending()

    tiles_per_q = tq // tk
    n_chunks = 2 * tq // QUERY_CHUNK

    def full_tiles(it):
        return [(it * tiles_per_q + d, c, "full") for d in range(tiles_per_q) for c in range(n_chunks)]

    lax.fori_loop(0, i, lambda it, c: (step(full_tiles(it), SCORES_AHEAD_FULL), c)[1], 0)
    kind_order = {"full": 0, "low_tri": 1, "tri": 2}
    diag = lambda it: [(it * tiles_per_q + d, c, kind) for d in range(tiles_per_q)
                       for c, kind in sorted(_diag_chunks(tq, tk, d), key=lambda ck: kind_order[ck[1]])]
    lax.fori_loop(i, i + 1, lambda it, c: (step(diag(it), SCORES_AHEAD_DIAG), c)[1], 0)

    lam = (jnp.exp(jnp.sum(lq1_ref[...] * lk1_ref[...], axis=-1, keepdims=True))
           - jnp.exp(jnp.sum(lq2_ref[...] * lk2_ref[...], axis=-1, keepdims=True)) + lambda_init)
    acc = acc_ref[...]
    o2 = acc[0:DIFF_V_DIM, :] * (1.0 / acc[DIFF_V_DIM:DIFF_V_DIM + 1, :])
    o = (o2[:, :tq] - lam * o2[:, tq:]).T
    ms = jnp.mean(o * o, axis=-1, keepdims=True)
    o = o * lax.rsqrt(ms + EPS) * gain_ref[...] * (1.0 - lambda_init)
    o_ref[...] = o.astype(BF16)


def _diffattn(dq, dk, dvt, lam_q1, lam_k1, lam_q2, lam_k2, gain, batch, seq, lambda_init, tq, tk):
    nq = seq // tq
    q_spec = pl.BlockSpec((tq, LANES), lambda b, h, i: (b * nq + i, h))
    k_spec = pl.BlockSpec((seq, LANES), lambda b, h, i: (b, h))
    vt_spec = pl.BlockSpec((DIFF_V_DIM, seq), lambda b, h, i: (h, b))
    vec = lambda a: pl.BlockSpec(a.shape, lambda b, h, i: (0, 0))
    key = jnp.arange(QUERY_CHUNK)[:, None]
    query = jnp.arange(QUERY_CHUNK)[None, :]
    bias = jnp.where(key <= query, 0.0, NEG_BIG).astype(F32)
    n_masked = sum(kind != "full" for d in range(tq // tk) for _, kind in _diag_chunks(tq, tk, d))
    return pl.pallas_call(
        functools.partial(_diffattn_kernel, tq=tq, tk=tk, lambda_init=lambda_init),
        grid=(batch, DIFF_HEADS, nq),
        in_specs=[q_spec, k_spec, vt_spec, vec(lam_q1), vec(lam_k1), vec(lam_q2), vec(lam_k2), vec(gain),
                  vec(bias)],
        out_specs=q_spec,
        out_shape=jax.ShapeDtypeStruct(dq.shape, BF16),
        scratch_shapes=[
            pltpu.VMEM((2 * tq, LANES), BF16),
            pltpu.VMEM((seq // tk, V_EXT_ROWS, tk), BF16),
            pltpu.VMEM((1, 2 * tq), F32),
            pltpu.VMEM((V_EXT_ROWS, 2 * tq), F32),
            pltpu.VMEM((n_masked, tk, QUERY_CHUNK), F32),
        ],
        compiler_params=pltpu.CompilerParams(
            dimension_semantics=("parallel", "parallel", "arbitrary"), vmem_limit_bytes=VMEM_LIMIT),
        name="diffattn",
    )(dq, dk, dvt, lam_q1, lam_k1, lam_q2, lam_k2, gain, bias)


def _route(logits):
    r = [logits[g:g + 1, :] for g in range(N_GROUPS)]
    gmax = jnp.maximum(jnp.maximum(r[0], r[1]), jnp.maximum(r[2], r[3]))
    g_idx = jnp.where(r[0] == gmax, 0, jnp.where(r[1] == gmax, 1, jnp.where(r[2] == gmax, 2, 3)))
    denom = sum(jnp.exp(rg - gmax) for rg in r)
    g_weight = 1.0 / denom
    sel = jnp.zeros((EXPERTS_PER_GROUP, logits.shape[1]), F32)
    for g in range(N_GROUPS):
        rows = logits[8 + g * EXPERTS_PER_GROUP:8 + (g + 1) * EXPERTS_PER_GROUP, :]
        sel = jnp.where(g_idx == g, rows, sel)
    eidx = lax.broadcasted_iota(jnp.int32, sel.shape, 0)
    v1 = jnp.max(sel, axis=0, keepdims=True)
    i1 = jnp.min(jnp.where(sel == v1, eidx, EXPERTS_PER_GROUP), axis=0, keepdims=True)
    sel2 = jnp.where(eidx == i1, -jnp.inf, sel)
    v2 = jnp.max(sel2, axis=0, keepdims=True)
    i2 = jnp.min(jnp.where(sel2 == v2, eidx, EXPERTS_PER_GROUP), axis=0, keepdims=True)
    e2 = jnp.exp(v2 - v1)
    w1 = g_weight / (1.0 + e2)
    w2 = g_weight * e2 / (1.0 + e2)
    return g_idx, i1, i2, w1, w2


OUTPROJ_PARTS = 2


def _outproj_kernel(ret_ref, diff_ref, x_ref, mod_ref, gain_ref, wo_ref, wr_ref, br_ref, tri_ref,
                    x1_ref, h2_ref, ri_ref, rw_ref, cnt_ref, *, tiles_per_batch):
    b = pl.program_id(0) // tiles_per_batch
    gate1 = mod_ref[pl.ds(b, 1), 2 * D_MODEL:3 * D_MODEL]
    shift = mod_ref[pl.ds(b, 1), 3 * D_MODEL:4 * D_MODEL]
    scale = mod_ref[pl.ds(b, 1), 4 * D_MODEL:5 * D_MODEL]
    wr = wr_ref[...]
    tm = x_ref.shape[0]
    parts = [slice(n * tm // OUTPROJ_PARTS, (n + 1) * tm // OUTPROJ_PARTS) for n in range(OUTPROJ_PARTS)]
    mix = [_dot(ret_ref[r, :], wo_ref[0:RET_WIDTH, :]) + _dot(diff_ref[r, :], wo_ref[RET_WIDTH:, :]) for r in parts]
    x1 = [x_ref[r, :] + gate1 * m for r, m in zip(parts, mix)]
    for r, v in zip(parts, x1):
        x1_ref[r, :] = v
    h_split = [_split_bf16(_norm_modulate(v, gain_ref[...], shift, scale)) for v in x1]
    for r, (h_hi, _) in zip(parts, h_split):
        h2_ref[r, :] = h_hi
    by_hi = [_dot_nt(wr, h_hi) for h_hi, _ in h_split]
    by_lo = [_dot_nt(wr[:ROUTER_ROWS], h_lo) for _, h_lo in h_split]
    logits = [a[:ROUTER_ROWS] + a[ROUTER_ROWS:] + c + br_ref[...] for a, c in zip(by_hi, by_lo)]
    routed = [_route(lg) for lg in logits]
    g_idx, i1, i2, w1, w2 = [jnp.concatenate([rt[n] for rt in routed], axis=1) for n in range(5)]
    logits = jnp.concatenate(logits, axis=1)
    e1 = g_idx * EXPERTS_PER_GROUP + i1
    e2 = g_idx * EXPERTS_PER_GROUP + i2
    eidx = lax.broadcasted_iota(jnp.int32, (N_EXPERTS, logits.shape[1]), 0)
    hit1 = eidx == e1
    hit2 = eidx == e2
    onehot = jnp.where(hit1 | hit2, 1.0, 0.0)
    before = _dot(onehot.astype(BF16), tri_ref[...])
    r1 = jnp.sum(jnp.where(hit1, before, 0.0), axis=0, keepdims=True)
    r2 = jnp.sum(jnp.where(hit2, before, 0.0), axis=0, keepdims=True)
    zi = jnp.zeros_like(e1)
    ri_ref[...] = jnp.concatenate([e1, e2, r1.astype(jnp.int32), r2.astype(jnp.int32), zi, zi, zi, zi], axis=0)
    zf = jnp.zeros_like(w1)
    rw_ref[...] = jnp.concatenate([w1, w2, zf, zf, zf, zf, zf, zf], axis=0)
    counts = jnp.sum(onehot, axis=1, keepdims=True)
    cnt_ref[0] = jnp.broadcast_to(counts, (N_EXPERTS, LANES)).astype(jnp.int32)


def _outproj(ret_out, diff_out, x2, mod, gain, w_out, wr, br, seq, tm):
    tokens = x2.shape[0]
    tiles_per_batch = seq // tm
    n_tiles = tokens // tm
    tri = (jnp.arange(tm)[:, None] < jnp.arange(tm)[None, :]).astype(BF16)
    tok_spec = lambda w: pl.BlockSpec((tm, w), lambda i: (i, 0))
    row_spec = pl.BlockSpec((8, tm), lambda i: (0, i))
    full = lambda a: pl.BlockSpec(a.shape, lambda i: (0,) * a.ndim)
    return pl.pallas_call(
        functools.partial(_outproj_kernel, tiles_per_batch=tiles_per_batch),
        grid=(n_tiles,),
        in_specs=[tok_spec(RET_WIDTH), tok_spec(DIFF_WIDTH), tok_spec(D_MODEL), full(mod), full(gain),
                  full(w_out), full(wr), full(br), full(tri)],
        out_specs=[tok_spec(D_MODEL), tok_spec(D_MODEL), row_spec, row_spec,
                   pl.BlockSpec((1, N_EXPERTS, LANES), lambda i: (i, 0, 0))],
        out_shape=[jax.ShapeDtypeStruct((tokens, D_MODEL), F32),
                   jax.ShapeDtypeStruct((tokens, D_MODEL), BF16),
                   jax.ShapeDtypeStruct((8, tokens), jnp.int32),
                   jax.ShapeDtypeStruct((8, tokens), F32),
                   jax.ShapeDtypeStruct((n_tiles, N_EXPERTS, LANES), jnp.int32)],
        compiler_params=pltpu.CompilerParams(
            dimension_semantics=("parallel",), vmem_limit_bytes=VMEM_LIMIT),
        name="outproj",
    )(ret_out, diff_out, x2, mod, gain, w_out, wr, br, tri)


CHUNK = 8
TMX = 512


def _local_rows(tm):
    rows = 2 * tm + N_EXPERTS * (CHUNK - 1)
    return (rows + 15) // 16 * 16


def _sorted_rows_alloc(tokens, tm):
    worst = 2 * tokens + (tokens // tm) * N_EXPERTS * (CHUNK - 1) + N_EXPERTS * (TMX - CHUNK)
    return (worst + TMX - 1) // TMX * TMX


def _dispatch_plan(cnt, tokens, tm):
    i32 = jnp.int32
    nch_max = _local_rows(tm) // CHUNK
    pad = (cnt + CHUNK - 1) // CHUNK * CHUNK
    local_end = jnp.cumsum(pad, axis=1)
    local_start = local_end - pad
    seg_rows = jnp.sum(pad, axis=0)
    seg_pad = (seg_rows + TMX - 1) // TMX * TMX
    seg_end = jnp.cumsum(seg_pad)
    seg_start = seg_end - seg_pad
    run_dst = seg_start[None, :] + jnp.cumsum(pad, axis=0) - pad
    row = CHUNK * jnp.arange(nch_max, dtype=i32)[None, :, None]
    owns = (local_start[:, None, :] <= row) & (row < local_end[:, None, :])
    chunk_dst = row[:, :, 0] + jnp.sum(jnp.where(owns, (run_dst - local_start)[:, None, :], 0), axis=-1)
    m = TMX * jnp.arange(_sorted_rows_alloc(tokens, tm) // TMX, dtype=i32)
    tile_expert = jnp.minimum(jnp.sum(seg_end[None, :] <= m[:, None], axis=-1), N_EXPERTS - 1)
    towns = (seg_start[None, :] <= m[:, None]) & (m[:, None] < seg_end[None, :])
    used = seg_pad > 0
    parity = (jnp.cumsum(used) - used) % 2
    eids = jnp.arange(N_EXPERTS, dtype=i32)
    later_used = (eids[None, :] > eids[:, None]) & used[None, :]
    next_used = jnp.min(jnp.where(later_used, eids[None, :], N_EXPERTS), axis=1)
    next_used = jnp.where(next_used == N_EXPERTS, -1, next_used)
    pick = lambda per_expert: jnp.sum(jnp.where(towns, per_expert[None, :], 0), axis=-1)
    tile_first = jnp.sum(jnp.where(towns & (seg_start[None, :] == m[:, None]), 1, 0), axis=-1)
    tile_next = jnp.where(jnp.any(towns, axis=-1), pick(next_used), -1)
    return dict(
        tile_first=tile_first.astype(i32),
        tile_slot=pick(parity).astype(i32),
        tile_next=tile_next.astype(i32),
        local_start=local_start.reshape(-1).astype(i32),
        n_chunks=(local_end[:, -1] // CHUNK).astype(i32),
        chunk_dst=chunk_dst.reshape(-1).astype(i32),
        tail_base=(seg_start + seg_rows).astype(i32),
        tail_rows=(seg_pad - seg_rows).astype(i32),
        tile_expert=tile_expert.astype(i32),
        n_used=(seg_end[-1:] // TMX).astype(i32),
    )


WAIT_UNROLL = 8


def _wait_times(copy, n):
    lax.fori_loop(0, n // WAIT_UNROLL, lambda i, c: ([copy.wait() for _ in range(WAIT_UNROLL)], c)[1], 0)
    lax.fori_loop(0, n % WAIT_UNROLL, lambda i, c: (copy.wait(), c)[1], 0)


def _for_each(n, body, unroll=4):
    main = n // unroll
    lax.fori_loop(0, main, lambda i, c: ([body(i * unroll + u) for u in range(unroll)], c)[1], 0)
    lax.fori_loop(main * unroll, n, lambda j, c: (body(j), c)[1], 0)


def _local_slots(ri_ref, local_start_ref, tile):
    e1, e2 = ri_ref[0:1, :], ri_ref[1:2, :]
    s1, s2 = ri_ref[2:3, :], ri_ref[3:4, :]
    for e in range(N_EXPERTS):
        start = local_start_ref[tile * N_EXPERTS + e]
        s1 = s1 + jnp.where(e1 == e, start, 0)
        s2 = s2 + jnp.where(e2 == e, start, 0)
    return s1, s2


def _dispatch_kernel(local_start_ref, n_chunks_ref, chunk_dst_ref, tail_base_ref, tail_rows_ref, n_used_ref,
                     h_ref, ri_ref, xs_ref, buf_ref, zero_ref, sem_ref, tail_sem_ref, *, r_loc):
    b = pl.program_id(0)
    nb = pl.num_programs(0)
    slot = b % 2
    nch_max = r_loc // CHUNK

    def chunk_copy(tile, sl, j):
        row = pl.multiple_of(j * CHUNK, CHUNK)
        dst = pl.multiple_of(chunk_dst_ref[tile * nch_max + j], CHUNK)
        return pltpu.make_async_copy(buf_ref.at[sl, pl.ds(row, CHUNK), :], xs_ref.at[pl.ds(dst, CHUNK), :],
                                     sem_ref.at[sl])

    def drain(tile, sl):
        one_chunk = pltpu.make_async_copy(buf_ref.at[sl, pl.ds(0, CHUNK), :], xs_ref.at[pl.ds(0, CHUNK), :],
                                          sem_ref.at[sl])
        _wait_times(one_chunk, n_chunks_ref[tile])

    @pl.when(b >= 2)
    def _():
        drain(b - 2, slot)

    s1, s2 = _local_slots(ri_ref, local_start_ref, b)
    rows = lax.broadcasted_iota(jnp.int32, (r_loc, s1.shape[1]), 0)
    perm = jnp.where((rows == s1) | (rows == s2), 1.0, 0.0).astype(BF16)
    buf_ref[slot] = _dot(perm, h_ref[...])
    _for_each(n_chunks_ref[b], lambda j: chunk_copy(b, slot, j).start())

    def tail_pieces(e, act):
        n = tail_rows_ref[e]
        size = TMX // 2
        while size >= CHUNK:
            dst = pl.multiple_of(tail_base_ref[e] + (n & (-2 * size)), CHUNK)
            cp = pltpu.make_async_copy(zero_ref.at[pl.ds(0, size), :], xs_ref.at[pl.ds(dst, size), :],
                                       tail_sem_ref.at[0])
            pl.when((n & size) != 0)(functools.partial(act, cp))
            size //= 2

    def unused_tile_copy(m):
        dst = pl.multiple_of(m * TMX, TMX)
        return pltpu.make_async_copy(zero_ref, xs_ref.at[pl.ds(dst, TMX), :], tail_sem_ref.at[1])

    n_alloc = xs_ref.shape[0] // TMX

    @pl.when(b == 0)
    def _():
        zero_ref[...] = jnp.zeros_like(zero_ref)
        lax.fori_loop(0, N_EXPERTS, lambda e, c: (tail_pieces(e, lambda cp: cp.start()), c)[1], 0)
        lax.fori_loop(n_used_ref[0], n_alloc, lambda m, c: (unused_tile_copy(m).start(), c)[1], 0)

    @pl.when(b == nb - 1)
    def _():
        lax.fori_loop(0, N_EXPERTS, lambda e, c: (tail_pieces(e, lambda cp: cp.wait()), c)[1], 0)
        lax.fori_loop(n_used_ref[0], n_alloc, lambda m, c: (unused_tile_copy(m).wait(), c)[1], 0)

        @pl.when(b >= 1)
        def _():
            drain(b - 1, 1 - slot)

        drain(b, slot)


def _dispatch(h2, ri, plan, tm):
    tokens = h2.shape[0]
    r_loc = _local_rows(tm)
    grid_spec = pltpu.PrefetchScalarGridSpec(
        num_scalar_prefetch=6,
        grid=(tokens // tm,),
        in_specs=[pl.BlockSpec((tm, D_MODEL), lambda i, *_: (i, 0)),
                  pl.BlockSpec((8, tm), lambda i, *_: (0, i))],
        out_specs=pl.BlockSpec(memory_space=pl.ANY),
        scratch_shapes=[pltpu.VMEM((2, r_loc, D_MODEL), F32), pltpu.VMEM((TMX, D_MODEL), F32),
                        pltpu.SemaphoreType.DMA((2,)), pltpu.SemaphoreType.DMA((2,))],
    )
    return pl.pallas_call(
        functools.partial(_dispatch_kernel, r_loc=r_loc),
        grid_spec=grid_spec,
        out_shape=jax.ShapeDtypeStruct((_sorted_rows_alloc(tokens, tm), D_MODEL), F32),
        compiler_params=pltpu.CompilerParams(
            dimension_semantics=("arbitrary",), vmem_limit_bytes=VMEM_LIMIT),
        name="dispatch",
    )(plan["local_start"], plan["n_chunks"], plan["chunk_dst"], plan["tail_base"], plan["tail_rows"], plan["n_used"],
      h2, ri)


def _experts_kernel(tile_expert_ref, n_used_ref, first_ref, slot_ref, next_ref, xs_ref, wg_hbm, wu_hbm, wd_hbm,
                    ys_ref, wg_st, wu_st, wd_st, wg_bf, wu_bf, wd_bf, sem_ref):
    m = pl.program_id(0)

    def weight_copies(e, s):
        return [pltpu.make_async_copy(src.at[e], dst.at[s], sem_ref.at[s, n])
                for n, (src, dst) in enumerate([(wg_hbm, wg_st), (wu_hbm, wu_st), (wd_hbm, wd_st)])]

    @pl.when(m < n_used_ref[0])
    def _():
        @pl.when(first_ref[m] == 1)
        def _():
            s = slot_ref[m]

            @pl.when(m == 0)
            def _():
                for cp in weight_copies(tile_expert_ref[0], 0):
                    cp.start()

            for cp in weight_copies(tile_expert_ref[m], s):
                cp.wait()

            @pl.when(next_ref[m] >= 0)
            def _():
                for cp in weight_copies(next_ref[m], 1 - s):
                    cp.start()

            wg_bf[...] = wg_st[s].astype(BF16)
            wu_bf[...] = wu_st[s].astype(BF16)
            wd_bf[...] = wd_st[s].astype(BF16)

        x = xs_ref[...].astype(BF16)
        a = _dot(x, wg_bf[...])
        u = _dot(x, wu_bf[...])
        hid = (_silu(a) * u).astype(BF16)
        ys_ref[...] = _dot(hid, wd_bf[...])


def _experts(xs, plan, wg, wu, wd):
    n_tiles = xs.shape[0] // TMX
    last_used = lambda m, n_used: jnp.minimum(m, n_used[0] - 1)
    row_spec = pl.BlockSpec((TMX, D_MODEL), lambda m, te, nu, *_: (last_used(m, nu), 0))
    hbm = pl.BlockSpec(memory_space=pl.ANY)
    up_shape, down_shape = (D_MODEL, D_EXPERT), (D_EXPERT, D_MODEL)
    grid_spec = pltpu.PrefetchScalarGridSpec(
        num_scalar_prefetch=5,
        grid=(n_tiles,),
        in_specs=[row_spec, hbm, hbm, hbm],
        out_specs=row_spec,
        scratch_shapes=[pltpu.VMEM((2,) + up_shape, F32), pltpu.VMEM((2,) + up_shape, F32),
                        pltpu.VMEM((2,) + down_shape, F32),
                        pltpu.VMEM(up_shape, BF16), pltpu.VMEM(up_shape, BF16), pltpu.VMEM(down_shape, BF16),
                        pltpu.SemaphoreType.DMA((2, 3))],
    )
    return pl.pallas_call(
        _experts_kernel,
        grid_spec=grid_spec,
        out_shape=jax.ShapeDtypeStruct(xs.shape, F32),
        input_output_aliases={5: 0},
        compiler_params=pltpu.CompilerParams(
            dimension_semantics=("arbitrary",), vmem_limit_bytes=VMEM_LIMIT),
        name="experts",
    )(plan["tile_expert"], plan["n_used"], plan["tile_first"], plan["tile_slot"], plan["tile_next"], xs, wg, wu, wd)


def _combine_kernel(local_start_ref, n_chunks_ref, chunk_dst_ref, ys_ref, ri_ref, rw_ref, x1_ref, mod_ref,
                    gain_ref, o_ref, buf_ref, sem_ref, *, r_loc, tiles_per_batch):
    b = pl.program_id(0)
    nb = pl.num_programs(0)
    slot = b % 2
    nch_max = r_loc // CHUNK

    def chunk_copy(tile, sl, j):
        row = pl.multiple_of(j * CHUNK, CHUNK)
        src = pl.multiple_of(chunk_dst_ref[tile * nch_max + j], CHUNK)
        return pltpu.make_async_copy(ys_ref.at[pl.ds(src, CHUNK), :], buf_ref.at[sl, pl.ds(row, CHUNK), :],
                                     sem_ref.at[sl])

    def fetch(tile, sl):
        _for_each(n_chunks_ref[tile], lambda j: chunk_copy(tile, sl, j).start())

    @pl.when(b == 0)
    def _():
        buf_ref[...] = jnp.zeros_like(buf_ref)
        fetch(0, 0)

    @pl.when(b + 1 < nb)
    def _():
        fetch(b + 1, 1 - slot)

    one_chunk = pltpu.make_async_copy(ys_ref.at[pl.ds(0, CHUNK), :], buf_ref.at[slot, pl.ds(0, CHUNK), :],
                                      sem_ref.at[slot])
    _wait_times(one_chunk, n_chunks_ref[b])

    s1, s2 = _local_slots(ri_ref, local_start_ref, b)
    rows = lax.broadcasted_iota(jnp.int32, (r_loc, s1.shape[1]), 0)
    hit1 = rows == s1
    hit2 = rows == s2
    w_row = jnp.sum(jnp.where(hit1, rw_ref[0:1, :], jnp.where(hit2, rw_ref[1:2, :], 0.0)), axis=1, keepdims=True)
    perm = jnp.where(hit1 | hit2, 1.0, 0.0).astype(BF16)
    yw = (buf_ref[slot] * w_row).astype(BF16)
    moe = _dot_tn(perm, yw)
    batch = b // tiles_per_batch
    gate2 = mod_ref[pl.ds(batch, 1), 5 * D_MODEL:6 * D_MODEL]
    x2 = x1_ref[...] + gate2 * moe
    ms = jnp.mean(x2 * x2, axis=-1, keepdims=True)
    o_ref[...] = x2 * lax.rsqrt(ms + EPS) * gain_ref[...]


def _combine(ys, ri, rw, x1, mod, gain, plan, seq, tm):
    tokens = x1.shape[0]
    r_loc = _local_rows(tm)
    row_spec = pl.BlockSpec((8, tm), lambda i, *_: (0, i))
    tok_spec = pl.BlockSpec((tm, D_MODEL), lambda i, *_: (i, 0))
    full = lambda a: pl.BlockSpec(a.shape, lambda i, *_: (0,) * a.ndim)
    grid_spec = pltpu.PrefetchScalarGridSpec(
        num_scalar_prefetch=3,
        grid=(tokens // tm,),
        in_specs=[pl.BlockSpec(memory_space=pl.ANY), row_spec, row_spec, tok_spec, full(mod), full(gain)],
        out_specs=tok_spec,
        scratch_shapes=[pltpu.VMEM((2, r_loc, D_MODEL), F32), pltpu.SemaphoreType.DMA((2,))],
    )
    return pl.pallas_call(
        functools.partial(_combine_kernel, r_loc=r_loc, tiles_per_batch=seq // tm),
        grid_spec=grid_spec,
        out_shape=jax.ShapeDtypeStruct((tokens, D_MODEL), F32),
        compiler_params=pltpu.CompilerParams(
            dimension_semantics=("arbitrary",), vmem_limit_bytes=VMEM_LIMIT),
        name="combine",
    )(plan["local_start"], plan["n_chunks"], plan["chunk_dst"], ys, ri, rw, x1, mod, gain)


def _rotary_tables(seq):
    half = RET_HEAD_DIM // 2
    inv_freq = 1.0 / (ROPE_BASE ** (jnp.arange(half, dtype=F32) / half))
    ang = jnp.arange(seq).astype(F32)[:, None] * inv_freq[None, :]
    cos = jnp.cos(ang)
    sin = jnp.sin(ang)
    return jnp.tile(cos, (1, 4)), jnp.concatenate([-sin, sin, -sin, sin], axis=1)


def _pick_tile(n, pref):
    t = min(n, pref)
    assert n % t == 0, (n, t)
    return t


def kernel(x, c, ada_w, ada_b, norm1_gain, norm2_gain, w_in, w_out, ret_gn_gain, lam_q1, lam_k1, lam_q2,
           lam_k2, diff_subln_gain, w_group, b_group, w_expert, b_expert, w_gate, w_up, w_down, final_gain):
    batch, seq, d = x.shape
    assert d == D_MODEL and batch <= 8 and ada_w.shape[0] == 1
    layer = 0
    lambda_init = 0.8 - 0.6 * math.exp(-0.3 * layer)
    tokens = batch * seq
    x2 = x.reshape(tokens, d)
    tm = _pick_tile(seq, 512)

    c_pad = jnp.zeros((8, d), F32).at[:batch].set(c)
    mod = _adaln(c_pad, ada_w[layer], ada_b[layer].reshape(1, -1))

    cos_t, sin_t = _rotary_tables(seq)
    rq, rk, rv, rg, dq, dk, dvt = _inproj(
        x2, mod, norm1_gain[layer].reshape(1, d), w_in[layer].astype(BF16), cos_t, sin_t, seq, tm)

    ret_out = _retention(rq, rk, rv, rg, ret_gn_gain[layer].reshape(1, RET_WIDTH), batch, seq,
                         _pick_tile(seq, 256))
    diff_out = _diffattn(
        dq, dk, dvt, lam_q1[layer].reshape(1, -1), lam_k1[layer].reshape(1, -1), lam_q2[layer].reshape(1, -1),
        lam_k2[layer].reshape(1, -1), diff_subln_gain[layer].reshape(1, -1), batch, seq, lambda_init,
        _pick_tile(seq, 1024), 2 * QUERY_CHUNK)

    w_router = jnp.concatenate(
        [w_group[layer].T, jnp.zeros((8 - N_GROUPS, d), F32), w_expert[layer].reshape(d, N_EXPERTS).T], axis=0)
    b_router = jnp.concatenate(
        [b_group[layer], jnp.zeros((8 - N_GROUPS,), F32), b_expert[layer].reshape(N_EXPERTS)]).reshape(-1, 1)
    wr_hi = w_router.astype(BF16)
    wr_lo = (w_router - wr_hi.astype(F32)).astype(BF16)
    x1, h2, ri, rw, cnt = _outproj(ret_out, diff_out, x2, mod, norm2_gain[layer].reshape(1, d),
                                   w_out[layer].astype(BF16), jnp.concatenate([wr_hi, wr_lo], axis=0), b_router,
                                   seq, tm)

    plan = _dispatch_plan(cnt[:, :, 0], tokens, tm)
    xs = _dispatch(h2, ri, plan, tm)
    ys = _experts(xs, plan, w_gate[layer].reshape(N_EXPERTS, d, D_EXPERT),
                  w_up[layer].reshape(N_EXPERTS, d, D_EXPERT), w_down[layer].reshape(N_EXPERTS, D_EXPERT, d))
    out = _combine(ys, ri, rw, x1, mod, final_gain.reshape(1, d), plan, seq, tm)
    return out.reshape(batch, seq, d)
```

```python
import functools
import math

import jax
import jax.numpy as jnp
import numpy as np
from jax import lax
from jax.experimental import pallas as pl
from jax.experimental.pallas import tpu as pltpu

F32 = jnp.float32
BF16 = jnp.bfloat16

D_MODEL = 1024
RET_HEAD_DIM = 64
RET_WIDTH = 512
RET_HEADS = 8
RET_PAIRS = RET_HEADS // 2
DIFF_QK_DIM = 64
DIFF_V_DIM = 128
DIFF_HEADS = 4
DIFF_WIDTH = 512
N_GROUPS = 4
EXPERTS_PER_GROUP = 8
N_EXPERTS = N_GROUPS * EXPERTS_PER_GROUP
D_EXPERT = 512
N_MOD = 6
ROPE_BASE = 10000.0
EPS = 1e-6
LANES = 128
ROUTER_ROWS = 8 + N_EXPERTS
VMEM_LIMIT = 56 * 1024 * 1024


def _dot(a, b):
    return jnp.dot(a, b, preferred_element_type=F32)


def _dot_nt(a, b):
    return lax.dot_general(a, b, (((1,), (1,)), ((), ())), preferred_element_type=F32)


def _dot_tn(a, b):
    return lax.dot_general(a, b, (((0,), (0,)), ((), ())), preferred_element_type=F32)


def _split_bf16(x):
    hi = x.astype(BF16)
    lo = (x - hi.astype(F32)).astype(BF16)
    return hi, lo


def _silu(x):
    return x / (1.0 + jnp.exp(-x))


def _adaln_kernel(c_ref, w_ref, b_ref, o_ref):
    ca = _silu(c_ref[...])
    c_hi, c_lo = _split_bf16(ca)
    w_hi, w_lo = _split_bf16(w_ref[...])
    o_ref[...] = _dot(c_hi, w_hi) + _dot(c_lo, w_hi) + _dot(c_hi, w_lo) + b_ref[...]


def _adaln(c_pad, ada_w, ada_b):
    n_out = ada_w.shape[1]
    tn = D_MODEL
    return pl.pallas_call(
        _adaln_kernel,
        grid=(n_out // tn,),
        in_specs=[
            pl.BlockSpec((8, D_MODEL), lambda j: (0, 0)),
            pl.BlockSpec((D_MODEL, tn), lambda j: (0, j)),
            pl.BlockSpec((1, tn), lambda j: (0, j)),
        ],
        out_specs=pl.BlockSpec((8, tn), lambda j: (0, j)),
        out_shape=jax.ShapeDtypeStruct((8, n_out), F32),
        compiler_params=pltpu.CompilerParams(vmem_limit_bytes=VMEM_LIMIT),
        name="adaln",
    )(c_pad, ada_w, ada_b)


def _norm_modulate(x, gain, shift, scale):
    ms = jnp.mean(x * x, axis=-1, keepdims=True)
    y = x * lax.rsqrt(ms + EPS) * gain
    return y * (1.0 + scale) + shift


def _rotary_slab(x, cos, sin_signed, lane_lo):
    swapped = jnp.where(lane_lo, pltpu.roll(x, 96, 1), pltpu.roll(x, 32, 1))
    return x * cos + swapped * sin_signed


def _inproj_kernel(x_ref, mod_ref, gain_ref, w_ref, cos_ref, sin_ref,
                   rq_ref, rk_ref, rv_ref, rg_ref, dq_ref, dk_ref, dvt_ref, *, tiles_per_batch):
    b = pl.program_id(0) // tiles_per_batch
    shift = mod_ref[pl.ds(b, 1), 0:D_MODEL]
    scale = mod_ref[pl.ds(b, 1), D_MODEL:2 * D_MODEL]
    h = _norm_modulate(x_ref[...], gain_ref[...], shift, scale).astype(BF16)
    cos = cos_ref[...]
    sin = sin_ref[...]
    lane = lax.broadcasted_iota(jnp.int32, cos.shape, 1)
    lane_lo = (lane % 64) < 32

    def proj(chunk):
        return _dot(h, w_ref[:, chunk * RET_WIDTH:(chunk + 1) * RET_WIDTH])

    def rotary(acc, out_ref, post_scale):
        for s in range(RET_WIDTH // LANES):
            sl = slice(s * LANES, (s + 1) * LANES)
            out_ref[:, sl] = (_rotary_slab(acc[:, sl], cos, sin, lane_lo) * post_scale).astype(BF16)

    rotary(proj(0), rq_ref, 1.0)
    rotary(proj(1), rk_ref, RET_HEAD_DIM ** -0.5)
    rv_ref[...] = proj(2).astype(BF16)
    rg_ref[...] = _silu(proj(3)).astype(BF16)
    dq_ref[...] = (proj(4) * (DIFF_QK_DIM ** -0.5 * math.log2(math.e))).astype(BF16)
    dk_ref[...] = proj(5).astype(BF16)
    dvt_ref[...] = proj(6).T.astype(BF16)


def _inproj(x2, mod, gain, w_in, cos_t, sin_t, seq, tm):
    tokens = x2.shape[0]
    tiles_per_batch = seq // tm
    tok_spec = lambda w: pl.BlockSpec((tm, w), lambda i: (i, 0))
    tab_spec = pl.BlockSpec((tm, LANES), lambda i: (i % tiles_per_batch, 0))
    full = lambda a: pl.BlockSpec(a.shape, lambda i: (0,) * a.ndim)
    out = jax.ShapeDtypeStruct((tokens, RET_WIDTH), BF16)
    return pl.pallas_call(
        functools.partial(_inproj_kernel, tiles_per_batch=tiles_per_batch),
        grid=(tokens // tm,),
        in_specs=[tok_spec(D_MODEL), full(mod), full(gain), full(w_in), tab_spec, tab_spec],
        out_specs=[tok_spec(RET_WIDTH)] * 6 + [pl.BlockSpec((DIFF_WIDTH, tm), lambda i: (0, i))],
        out_shape=[out] * 6 + [jax.ShapeDtypeStruct((DIFF_WIDTH, tokens), BF16)],
        compiler_params=pltpu.CompilerParams(
            dimension_semantics=("parallel",), vmem_limit_bytes=VMEM_LIMIT),
        name="inproj",
    )(x2, mod, gain, w_in, cos_t, sin_t)


def _retention_kernel(q_ref, k_ref, v_ref, g_ref, dec_ref, qdec_ref, kdec_ref, rdec_ref,
                      bmask_ref, gmean_ref, gain_ref, o_ref, state_ref, *, chunk):
    @pl.when(pl.program_id(1) == 0)
    def _():
        state_ref[...] = jnp.zeros_like(state_ref)

    lane = lax.broadcasted_iota(jnp.int32, (chunk, LANES), 1)
    first_head = lane < RET_HEAD_DIM
    gmean = gmean_ref[...]
    bmask = bmask_ref[...]
    pairs = range(RET_PAIRS)
    sl = [slice(p * LANES, (p + 1) * LANES) for p in pairs]
    q = [q_ref[:, sl[p]] for p in pairs]
    k = [k_ref[:, sl[p]] for p in pairs]
    v = [v_ref[:, sl[p]] for p in pairs]
    zero = jnp.zeros_like(q[0])
    q_stack = [jnp.concatenate([jnp.where(first_head, q[p], zero), jnp.where(first_head, zero, q[p])], axis=0)
               for p in pairs]
    scores = [(_dot_nt(q_stack[p], k[p]) * dec_ref[p]).astype(BF16) for p in pairs]
    state = [state_ref[p] for p in pairs]
    cross = [_dot(q[p], state[p].astype(BF16)) * qdec_ref[:, sl[p]] for p in pairs]
    k_dec = [(k[p].astype(F32) * kdec_ref[:, sl[p]]).astype(BF16) for p in pairs]
    for p in pairs:
        state_ref[p] = state[p] * rdec_ref[p] + _dot_tn(k_dec[p], v[p]) * bmask
    intra2 = [_dot(scores[p], v[p]) for p in pairs]
    y = [jnp.where(first_head, intra2[p][:chunk], intra2[p][chunk:]) + cross[p] for p in pairs]
    y_split = [_split_bf16(y[p]) for p in pairs]
    mu = [_dot(y_split[p][0], gmean) + _dot(y_split[p][1], gmean) for p in pairs]
    d = [y[p] - mu[p] for p in pairs]
    d_split = [_split_bf16(d[p] * d[p]) for p in pairs]
    var = [_dot(d_split[p][0], gmean) + _dot(d_split[p][1], gmean) for p in pairs]
    for p in pairs:
        yn = d[p] * lax.rsqrt(var[p] + EPS) * gain_ref[:, sl[p]]
        o_ref[:, sl[p]] = (g_ref[:, sl[p]].astype(F32) * yn).astype(BF16)


def _retention_tables(chunk):
    heads = np.arange(RET_HEADS, dtype=np.float64)
    log_gamma = np.log(1.0 - np.exp2(-5.0 - heads))
    idx = np.arange(chunk)
    rel = (idx[:, None] - idx[None, :]).astype(np.float64)
    decay = np.where(rel[None] >= 0, np.exp(log_gamma[:, None, None] * np.maximum(rel, 0.0)[None]), 0.0)
    dec2 = decay.reshape(RET_PAIRS, 2 * chunk, chunk)
    lane_lg = np.repeat(log_gamma, RET_HEAD_DIM)
    qdec = np.exp(lane_lg[None, :] * (idx + 1)[:, None])
    kdec = np.exp(lane_lg[None, :] * (chunk - 1 - idx)[:, None])
    rdec = np.exp(lane_lg * chunk).reshape(RET_PAIRS, LANES, 1) * np.ones((1, 1, LANES))
    blk = np.arange(LANES) // RET_HEAD_DIM
    bmask = (blk[:, None] == blk[None, :]).astype(np.float64)
    f32 = lambda a: jnp.asarray(a.astype(np.float32))
    return f32(dec2), f32(qdec), f32(kdec), f32(rdec), f32(bmask), f32(bmask / RET_HEAD_DIM).astype(BF16)


def _retention(rq, rk, rv, rg, gn_gain, batch, seq, chunk):
    nc = seq // chunk
    dec2, qdec, kdec, rdec, bmask, gmean = _retention_tables(chunk)
    tok_spec = pl.BlockSpec((chunk, RET_WIDTH), lambda b, n: (b * nc + n, 0))
    full = lambda a: pl.BlockSpec(a.shape, lambda b, n: (0,) * a.ndim)
    return pl.pallas_call(
        functools.partial(_retention_kernel, chunk=chunk),
        grid=(batch, nc),
        in_specs=[tok_spec] * 4 + [full(dec2), full(qdec), full(kdec), full(rdec), full(bmask),
                                   full(gmean), full(gn_gain)],
        out_specs=tok_spec,
        out_shape=jax.ShapeDtypeStruct(rq.shape, BF16),
        scratch_shapes=[pltpu.VMEM((RET_PAIRS, LANES, LANES), F32)],
        compiler_params=pltpu.CompilerParams(
            dimension_semantics=("parallel", "arbitrary"), vmem_limit_bytes=VMEM_LIMIT),
        name="retention",
    )(rq, rk, rv, rg, dec2, qdec, kdec, rdec, bmask, gmean, gn_gain)


NEG_BIG = -1e30


V_EXT_ROWS = DIFF_V_DIM + 16
QUERY_CHUNK = 256
SCORES_AHEAD_FULL = 3
SCORES_AHEAD_DIAG = 3


def _diag_chunks(tq, tk, d):
    assert tk == 2 * QUERY_CHUNK
    per_softmax = tq // QUERY_CHUNK
    out = []
    for c in range(2 * per_softmax):
        q0 = (c % per_softmax) * QUERY_CHUNK
        if q0 + QUERY_CHUNK - 1 < d * tk:
            continue
        kind = "full" if q0 >= (d + 1) * tk else ("tri" if q0 == d * tk else "low_tri")
        out.append((c, kind))
    return out


def _accumulate(acc_ref, cs, alpha, pv):
    acc_ref[:, cs] = alpha * acc_ref[:, cs] + pv


def _diffattn_kernel(q_ref, k_ref, vt_ref, lq1_ref, lk1_ref, lq2_ref, lk2_ref, gain_ref, bias_ref, o_ref,
                     qs_ref, vext_ref, m_ref, acc_ref, stage_ref, *, tq, tk, lambda_init):
    i = pl.program_id(2)
    nk = vext_ref.shape[0]

    @pl.when(i == 0)
    def _():
        for j in range(nk):
            vext_ref[j, 0:DIFF_V_DIM, :] = vt_ref[:, j * tk:(j + 1) * tk]
            vext_ref[j, DIFF_V_DIM:V_EXT_ROWS, :] = jnp.ones((V_EXT_ROWS - DIFF_V_DIM, tk), BF16)

    q = q_ref[...]
    lane = lax.broadcasted_iota(jnp.int32, q.shape, 1)
    zero = jnp.zeros_like(q)
    qs_ref[0:tq, :] = jnp.where(lane < DIFF_QK_DIM, q, zero)
    qs_ref[tq:2 * tq, :] = jnp.where(lane < DIFF_QK_DIM, zero, q)
    m_ref[...] = jnp.full_like(m_ref, NEG_BIG)
    acc_ref[...] = jnp.zeros_like(acc_ref)

    def step(work, n_ahead):
        chunk = lambda c: slice(c * QUERY_CHUNK, (c + 1) * QUERY_CHUNK)

        def scores(j, c, kind):
            n_keys = QUERY_CHUNK if kind == "tri" else tk
            start = pl.multiple_of(j * tk, tk)
            return _dot_nt(k_ref[pl.ds(start, n_keys), :], qs_ref[chunk(c), :])

        ahead = [scores(*work[n]) for n in range(min(n_ahead, len(work)))]
        pending = None
        masked_seen = 0
        for n, (j, c, kind) in enumerate(work):
            cs = chunk(c)
            st = ahead.pop(0)
            if n + n_ahead < len(work):
                ahead.append(scores(*work[n + n_ahead]))
            if kind != "full":
                slot = masked_seen
                masked_seen += 1
                n_keys = st.shape[0]
                stage_ref[slot, 0:n_keys, :] = st
                causal = stage_ref[slot, n_keys - QUERY_CHUNK:n_keys, :] + bias_ref[...]
                st = causal if kind == "tri" else jnp.concatenate(
                    [stage_ref[slot, 0:n_keys - QUERY_CHUNK, :], causal], axis=0)
            m_old = m_ref[:, cs]
            m_new = jnp.maximum(m_old, jnp.max(st, axis=0, keepdims=True))
            alpha = jnp.exp2(m_old - m_new)
            p = jnp.exp2(st - m_new).astype(BF16)
            m_ref[:, cs] = m_new
            pv = _dot(vext_ref[j, :, 0:st.shape[0]], p)
            if pending is not None:
                pending()
            pending = functools.partial(_accumulate, acc_ref, cs, alpha, pv)
        pending()

    tiles_per_q = tq // tk
    n_chunks = 2 * tq // QUERY_CHUNK

    def full_tiles(it):
        return [(it * tiles_per_q + d, c, "full") for d in range(tiles_per_q) for c in range(n_chunks)]

    lax.fori_loop(0, i, lambda it, c: (step(full_tiles(it), SCORES_AHEAD_FULL), c)[1], 0)
    kind_order = {"full": 0, "low_tri": 1, "tri": 2}
    diag = lambda it: [(it * tiles_per_q + d, c, kind) for d in range(tiles_per_q)
                       for c, kind in sorted(_diag_chunks(tq, tk, d), key=lambda ck: kind_order[ck[1]])]
    lax.fori_loop(i, i + 1, lambda it, c: (step(diag(it), SCORES_AHEAD_DIAG), c)[1], 0)

    lam = (jnp.exp(jnp.sum(lq1_ref[...] * lk1_ref[...], axis=-1, keepdims=True))
           - jnp.exp(jnp.sum(lq2_ref[...] * lk2_ref[...], axis=-1, keepdims=True)) + lambda_init)
    acc = acc_ref[...]
    o2 = acc[0:DIFF_V_DIM, :] * (1.0 / acc[DIFF_V_DIM:DIFF_V_DIM + 1, :])
    o = (o2[:, :tq] - lam * o2[:, tq:]).T
    ms = jnp.mean(o * o, axis=-1, keepdims=True)
    o = o * lax.rsqrt(ms + EPS) * gain_ref[...] * (1.0 - lambda_init)
    o_ref[...] = o.astype(BF16)


def _diffattn(dq, dk, dvt, lam_q1, lam_k1, lam_q2, lam_k2, gain, batch, seq, lambda_init, tq, tk):
    nq = seq // tq
    q_spec = pl.BlockSpec((tq, LANES), lambda b, h, i: (b * nq + i, h))
    k_spec = pl.BlockSpec((seq, LANES), lambda b, h, i: (b, h))
    vt_spec = pl.BlockSpec((DIFF_V_DIM, seq), lambda b, h, i: (h, b))
    vec = lambda a: pl.BlockSpec(a.shape, lambda b, h, i: (0, 0))
    key = np.arange(QUERY_CHUNK)[:, None]
    query = np.arange(QUERY_CHUNK)[None, :]
    bias = jnp.asarray(np.where(key <= query, 0.0, NEG_BIG), F32)
    n_masked = sum(kind != "full" for d in range(tq // tk) for _, kind in _diag_chunks(tq, tk, d))
    return pl.pallas_call(
        functools.partial(_diffattn_kernel, tq=tq, tk=tk, lambda_init=lambda_init),
        grid=(batch, DIFF_HEADS, nq),
        in_specs=[q_spec, k_spec, vt_spec, vec(lam_q1), vec(lam_k1), vec(lam_q2), vec(lam_k2), vec(gain),
                  vec(bias)],
        out_specs=q_spec,
        out_shape=jax.ShapeDtypeStruct(dq.shape, BF16),
        scratch_shapes=[
            pltpu.VMEM((2 * tq, LANES), BF16),
            pltpu.VMEM((seq // tk, V_EXT_ROWS, tk), BF16),
            pltpu.VMEM((1, 2 * tq), F32),
            pltpu.VMEM((V_EXT_ROWS, 2 * tq), F32),
            pltpu.VMEM((n_masked, tk, QUERY_CHUNK), F32),
        ],
        compiler_params=pltpu.CompilerParams(
            dimension_semantics=("parallel", "parallel", "arbitrary"), vmem_limit_bytes=VMEM_LIMIT),
        name="diffattn",
    )(dq, dk, dvt, lam_q1, lam_k1, lam_q2, lam_k2, gain, bias)


def _route(logits):
    r = [logits[g:g + 1, :] for g in range(N_GROUPS)]
    gmax = jnp.maximum(jnp.maximum(r[0], r[1]), jnp.maximum(r[2], r[3]))
    g_idx = jnp.where(r[0] == gmax, 0, jnp.where(r[1] == gmax, 1, jnp.where(r[2] == gmax, 2, 3)))
    denom = sum(jnp.exp(rg - gmax) for rg in r)
    g_weight = 1.0 / denom
    sel = jnp.zeros((EXPERTS_PER_GROUP, logits.shape[1]), F32)
    for g in range(N_GROUPS):
        rows = logits[8 + g * EXPERTS_PER_GROUP:8 + (g + 1) * EXPERTS_PER_GROUP, :]
        sel = jnp.where(g_idx == g, rows, sel)
    eidx = lax.broadcasted_iota(jnp.int32, sel.shape, 0)
    v1 = jnp.max(sel, axis=0, keepdims=True)
    i1 = jnp.min(jnp.where(sel == v1, eidx, EXPERTS_PER_GROUP), axis=0, keepdims=True)
    sel2 = jnp.where(eidx == i1, -jnp.inf, sel)
    v2 = jnp.max(sel2, axis=0, keepdims=True)
    i2 = jnp.min(jnp.where(sel2 == v2, eidx, EXPERTS_PER_GROUP), axis=0, keepdims=True)
    e2 = jnp.exp(v2 - v1)
    w1 = g_weight / (1.0 + e2)
    w2 = g_weight * e2 / (1.0 + e2)
    return g_idx, i1, i2, w1, w2


OUTPROJ_PARTS = 2


def _outproj_kernel(ret_ref, diff_ref, x_ref, mod_ref, gain_ref, wo_ref, wr_ref, br_ref, tri_ref,
                    x1_ref, h2_ref, ri_ref, rw_ref, cnt_ref, *, tiles_per_batch):
    b = pl.program_id(0) // tiles_per_batch
    gate1 = mod_ref[pl.ds(b, 1), 2 * D_MODEL:3 * D_MODEL]
    shift = mod_ref[pl.ds(b, 1), 3 * D_MODEL:4 * D_MODEL]
    scale = mod_ref[pl.ds(b, 1), 4 * D_MODEL:5 * D_MODEL]
    wr = wr_ref[...]
    tm = x_ref.shape[0]
    parts = [slice(n * tm // OUTPROJ_PARTS, (n + 1) * tm // OUTPROJ_PARTS) for n in range(OUTPROJ_PARTS)]
    mix = [_dot(ret_ref[r, :], wo_ref[0:RET_WIDTH, :]) + _dot(diff_ref[r, :], wo_ref[RET_WIDTH:, :]) for r in parts]
    x1 = [x_ref[r, :] + gate1 * m for r, m in zip(parts, mix)]
    for r, v in zip(parts, x1):
        x1_ref[r, :] = v
    h_split = [_split_bf16(_norm_modulate(v, gain_ref[...], shift, scale)) for v in x1]
    for r, (h_hi, _) in zip(parts, h_split):
        h2_ref[r, :] = h_hi
    by_hi = [_dot_nt(wr, h_hi) for h_hi, _ in h_split]
    by_lo = [_dot_nt(wr[:ROUTER_ROWS], h_lo) for _, h_lo in h_split]
    logits = [a[:ROUTER_ROWS] + a[ROUTER_ROWS:] + c + br_ref[...] for a, c in zip(by_hi, by_lo)]
    routed = [_route(lg) for lg in logits]
    g_idx, i1, i2, w1, w2 = [jnp.concatenate([rt[n] for rt in routed], axis=1) for n in range(5)]
    logits = jnp.concatenate(logits, axis=1)
    e1 = g_idx * EXPERTS_PER_GROUP + i1
    e2 = g_idx * EXPERTS_PER_GROUP + i2
    eidx = lax.broadcasted_iota(jnp.int32, (N_EXPERTS, logits.shape[1]), 0)
    hit1 = eidx == e1
    hit2 = eidx == e2
    onehot = jnp.where(hit1 | hit2, 1.0, 0.0)
    before = _dot(onehot.astype(BF16), tri_ref[...])
    r1 = jnp.sum(jnp.where(hit1, before, 0.0), axis=0, keepdims=True)
    r2 = jnp.sum(jnp.where(hit2, before, 0.0), axis=0, keepdims=True)
    zi = jnp.zeros_like(e1)
    ri_ref[...] = jnp.concatenate([e1, e2, r1.astype(jnp.int32), r2.astype(jnp.int32), zi, zi, zi, zi], axis=0)
    zf = jnp.zeros_like(w1)
    rw_ref[...] = jnp.concatenate([w1, w2, zf, zf, zf, zf, zf, zf], axis=0)
    counts = jnp.sum(onehot, axis=1, keepdims=True)
    cnt_ref[0] = jnp.broadcast_to(counts, (N_EXPERTS, LANES)).astype(jnp.int32)


def _outproj(ret_out, diff_out, x2, mod, gain, w_out, wr, br, seq, tm):
    tokens = x2.shape[0]
    tiles_per_batch = seq // tm
    n_tiles = tokens // tm
    tri = jnp.asarray(np.arange(tm)[:, None] < np.arange(tm)[None, :], BF16)
    tok_spec = lambda w: pl.BlockSpec((tm, w), lambda i: (i, 0))
    row_spec = pl.BlockSpec((8, tm), lambda i: (0, i))
    full = lambda a: pl.BlockSpec(a.shape, lambda i: (0,) * a.ndim)
    return pl.pallas_call(
        functools.partial(_outproj_kernel, tiles_per_batch=tiles_per_batch),
        grid=(n_tiles,),
        in_specs=[tok_spec(RET_WIDTH), tok_spec(DIFF_WIDTH), tok_spec(D_MODEL), full(mod), full(gain),
                  full(w_out), full(wr), full(br), full(tri)],
        out_specs=[tok_spec(D_MODEL), tok_spec(D_MODEL), row_spec, row_spec,
                   pl.BlockSpec((1, N_EXPERTS, LANES), lambda i: (i, 0, 0))],
        out_shape=[jax.ShapeDtypeStruct((tokens, D_MODEL), F32),
                   jax.ShapeDtypeStruct((tokens, D_MODEL), BF16),
                   jax.ShapeDtypeStruct((8, tokens), jnp.int32),
                   jax.ShapeDtypeStruct((8, tokens), F32),
                   jax.ShapeDtypeStruct((n_tiles, N_EXPERTS, LANES), jnp.int32)],
        compiler_params=pltpu.CompilerParams(
            dimension_semantics=("parallel",), vmem_limit_bytes=VMEM_LIMIT),
        name="outproj",
    )(ret_out, diff_out, x2, mod, gain, w_out, wr, br, tri)


CHUNK = 8
TMX = 512


def _local_rows(tm):
    rows = 2 * tm + N_EXPERTS * (CHUNK - 1)
    return (rows + 15) // 16 * 16


def _sorted_rows_alloc(tokens, tm):
    worst = 2 * tokens + (tokens // tm) * N_EXPERTS * (CHUNK - 1) + N_EXPERTS * (TMX - CHUNK)
    return (worst + TMX - 1) // TMX * TMX


def _dispatch_plan(cnt, tokens, tm):
    i32 = jnp.int32
    nch_max = _local_rows(tm) // CHUNK
    pad = (cnt + CHUNK - 1) // CHUNK * CHUNK
    local_end = jnp.cumsum(pad, axis=1)
    local_start = local_end - pad
    seg_rows = jnp.sum(pad, axis=0)
    seg_pad = (seg_rows + TMX - 1) // TMX * TMX
    seg_end = jnp.cumsum(seg_pad)
    seg_start = seg_end - seg_pad
    run_dst = seg_start[None, :] + jnp.cumsum(pad, axis=0) - pad
    row = CHUNK * jnp.arange(nch_max, dtype=i32)[None, :, None]
    owns = (local_start[:, None, :] <= row) & (row < local_end[:, None, :])
    chunk_dst = row[:, :, 0] + jnp.sum(jnp.where(owns, (run_dst - local_start)[:, None, :], 0), axis=-1)
    m = TMX * jnp.arange(_sorted_rows_alloc(tokens, tm) // TMX, dtype=i32)
    tile_expert = jnp.minimum(jnp.sum(seg_end[None, :] <= m[:, None], axis=-1), N_EXPERTS - 1)
    towns = (seg_start[None, :] <= m[:, None]) & (m[:, None] < seg_end[None, :])
    used = seg_pad > 0
    parity = (jnp.cumsum(used) - used) % 2
    eids = jnp.arange(N_EXPERTS, dtype=i32)
    later_used = (eids[None, :] > eids[:, None]) & used[None, :]
    next_used = jnp.min(jnp.where(later_used, eids[None, :], N_EXPERTS), axis=1)
    next_used = jnp.where(next_used == N_EXPERTS, -1, next_used)
    pick = lambda per_expert: jnp.sum(jnp.where(towns, per_expert[None, :], 0), axis=-1)
    tile_first = jnp.sum(jnp.where(towns & (seg_start[None, :] == m[:, None]), 1, 0), axis=-1)
    tile_next = jnp.where(jnp.any(towns, axis=-1), pick(next_used), -1)
    return dict(
        tile_first=tile_first.astype(i32),
        tile_slot=pick(parity).astype(i32),
        tile_next=tile_next.astype(i32),
        tile_rows=jnp.clip(pick(seg_start + seg_rows) - m, 0, TMX).astype(i32),
        local_start=local_start.reshape(-1).astype(i32),
        n_chunks=(local_end[:, -1] // CHUNK).astype(i32),
        chunk_dst=chunk_dst.reshape(-1).astype(i32),
        tail_base=(seg_start + seg_rows).astype(i32),
        tail_rows=(seg_pad - seg_rows).astype(i32),
        tile_expert=tile_expert.astype(i32),
        n_used=(seg_end[-1:] // TMX).astype(i32),
    )


WAIT_UNROLL = 8


def _wait_times(copy, n):
    lax.fori_loop(0, n // WAIT_UNROLL, lambda i, c: ([copy.wait() for _ in range(WAIT_UNROLL)], c)[1], 0)
    lax.fori_loop(0, n % WAIT_UNROLL, lambda i, c: (copy.wait(), c)[1], 0)


def _for_each(n, body, unroll=4):
    main = n // unroll
    lax.fori_loop(0, main, lambda i, c: ([body(i * unroll + u) for u in range(unroll)], c)[1], 0)
    lax.fori_loop(main * unroll, n, lambda j, c: (body(j), c)[1], 0)


def _local_slots(ri_ref, local_start_ref, tile):
    e1, e2 = ri_ref[0:1, :], ri_ref[1:2, :]
    s1, s2 = ri_ref[2:3, :], ri_ref[3:4, :]
    for e in range(N_EXPERTS):
        start = local_start_ref[tile * N_EXPERTS + e]
        s1 = s1 + jnp.where(e1 == e, start, 0)
        s2 = s2 + jnp.where(e2 == e, start, 0)
    return s1, s2


def _dispatch_kernel(local_start_ref, n_chunks_ref, chunk_dst_ref, tail_base_ref, tail_rows_ref, n_used_ref,
                     h_ref, ri_ref, xs_ref, buf_ref, zero_ref, sem_ref, tail_sem_ref, *, r_loc):
    b = pl.program_id(0)
    nb = pl.num_programs(0)
    slot = b % 2
    nch_max = r_loc // CHUNK

    def chunk_copy(tile, sl, j):
        row = pl.multiple_of(j * CHUNK, CHUNK)
        dst = pl.multiple_of(chunk_dst_ref[tile * nch_max + j], CHUNK)
        return pltpu.make_async_copy(buf_ref.at[sl, pl.ds(row, CHUNK), :], xs_ref.at[pl.ds(dst, CHUNK), :],
                                     sem_ref.at[sl])

    def drain(tile, sl):
        one_chunk = pltpu.make_async_copy(buf_ref.at[sl, pl.ds(0, CHUNK), :], xs_ref.at[pl.ds(0, CHUNK), :],
                                          sem_ref.at[sl])
        _wait_times(one_chunk, n_chunks_ref[tile])

    @pl.when(b >= 2)
    def _():
        drain(b - 2, slot)

    s1, s2 = _local_slots(ri_ref, local_start_ref, b)
    rows = lax.broadcasted_iota(jnp.int32, (r_loc, s1.shape[1]), 0)
    perm = jnp.where((rows == s1) | (rows == s2), 1.0, 0.0).astype(BF16)
    buf_ref[slot] = _dot(perm, h_ref[...])
    _for_each(n_chunks_ref[b], lambda j: chunk_copy(b, slot, j).start())

    def tail_pieces(e, act):
        n = tail_rows_ref[e]
        size = TMX // 2
        while size >= CHUNK:
            dst = pl.multiple_of(tail_base_ref[e] + (n & (-2 * size)), CHUNK)
            cp = pltpu.make_async_copy(zero_ref.at[pl.ds(0, size), :], xs_ref.at[pl.ds(dst, size), :],
                                       tail_sem_ref.at[0])
            pl.when((n & size) != 0)(functools.partial(act, cp))
            size //= 2

    def unused_tile_copy(m):
        dst = pl.multiple_of(m * TMX, TMX)
        return pltpu.make_async_copy(zero_ref, xs_ref.at[pl.ds(dst, TMX), :], tail_sem_ref.at[1])

    n_alloc = xs_ref.shape[0] // TMX

    @pl.when(b == 0)
    def _():
        zero_ref[...] = jnp.zeros_like(zero_ref)
        lax.fori_loop(0, N_EXPERTS, lambda e, c: (tail_pieces(e, lambda cp: cp.start()), c)[1], 0)
        lax.fori_loop(n_used_ref[0], n_alloc, lambda m, c: (unused_tile_copy(m).start(), c)[1], 0)

    @pl.when(b == nb - 1)
    def _():
        lax.fori_loop(0, N_EXPERTS, lambda e, c: (tail_pieces(e, lambda cp: cp.wait()), c)[1], 0)
        lax.fori_loop(n_used_ref[0], n_alloc, lambda m, c: (unused_tile_copy(m).wait(), c)[1], 0)

        @pl.when(b >= 1)
        def _():
            drain(b - 1, 1 - slot)

        drain(b, slot)


def _dispatch(h2, ri, plan, tm):
    tokens = h2.shape[0]
    r_loc = _local_rows(tm)
    grid_spec = pltpu.PrefetchScalarGridSpec(
        num_scalar_prefetch=6,
        grid=(tokens // tm,),
        in_specs=[pl.BlockSpec((tm, D_MODEL), lambda i, *_: (i, 0)),
                  pl.BlockSpec((8, tm), lambda i, *_: (0, i))],
        out_specs=pl.BlockSpec(memory_space=pl.ANY),
        scratch_shapes=[pltpu.VMEM((2, r_loc, D_MODEL), F32), pltpu.VMEM((TMX, D_MODEL), F32),
                        pltpu.SemaphoreType.DMA((2,)), pltpu.SemaphoreType.DMA((2,))],
    )
    return pl.pallas_call(
        functools.partial(_dispatch_kernel, r_loc=r_loc),
        grid_spec=grid_spec,
        out_shape=jax.ShapeDtypeStruct((_sorted_rows_alloc(tokens, tm), D_MODEL), F32),
        compiler_params=pltpu.CompilerParams(
            dimension_semantics=("arbitrary",), vmem_limit_bytes=VMEM_LIMIT),
        name="dispatch",
    )(plan["local_start"], plan["n_chunks"], plan["chunk_dst"], plan["tail_base"], plan["tail_rows"], plan["n_used"],
      h2, ri)


def _experts_kernel(tile_expert_ref, n_used_ref, first_ref, slot_ref, next_ref, rows_ref, xs_ref, wg_hbm, wu_hbm, wd_hbm,
                    ys_ref, wg_st, wu_st, wd_st, wg_bf, wu_bf, wd_bf, sem_ref):
    m = pl.program_id(0)

    def weight_copies(e, s):
        return [pltpu.make_async_copy(src.at[e], dst.at[s], sem_ref.at[s, n])
                for n, (src, dst) in enumerate([(wg_hbm, wg_st), (wu_hbm, wu_st), (wd_hbm, wd_st)])]

    @pl.when(m < n_used_ref[0])
    def _():
        @pl.when(first_ref[m] == 1)
        def _():
            s = slot_ref[m]

            @pl.when(m == 0)
            def _():
                for cp in weight_copies(tile_expert_ref[0], 0):
                    cp.start()

            for cp in weight_copies(tile_expert_ref[m], s):
                cp.wait()

            @pl.when(next_ref[m] >= 0)
            def _():
                for cp in weight_copies(next_ref[m], 1 - s):
                    cp.start()

            wg_bf[...] = wg_st[s].astype(BF16)
            wu_bf[...] = wu_st[s].astype(BF16)
            wd_bf[...] = wd_st[s].astype(BF16)

        def mlp(rows):
            x = xs_ref[rows, :].astype(BF16)
            a = _dot(x, wg_bf[...])
            u = _dot(x, wu_bf[...])
            hid = (_silu(a) * u).astype(BF16)
            ys_ref[rows, :] = _dot(hid, wd_bf[...])

        half = TMX // 2

        @pl.when(rows_ref[m] > half)
        def _():
            mlp(slice(0, TMX))

        @pl.when(rows_ref[m] <= half)
        def _():
            mlp(slice(0, half))
            ys_ref[half:, :] = jnp.zeros((TMX - half, D_MODEL), F32)


def _experts(xs, plan, wg, wu, wd):
    n_tiles = xs.shape[0] // TMX
    last_used = lambda m, n_used: jnp.minimum(m, n_used[0] - 1)
    row_spec = pl.BlockSpec((TMX, D_MODEL), lambda m, te, nu, *_: (last_used(m, nu), 0))
    hbm = pl.BlockSpec(memory_space=pl.ANY)
    up_shape, down_shape = (D_MODEL, D_EXPERT), (D_EXPERT, D_MODEL)
    grid_spec = pltpu.PrefetchScalarGridSpec(
        num_scalar_prefetch=6,
        grid=(n_tiles,),
        in_specs=[row_spec, hbm, hbm, hbm],
        out_specs=row_spec,
        scratch_shapes=[pltpu.VMEM((2,) + up_shape, F32), pltpu.VMEM((2,) + up_shape, F32),
                        pltpu.VMEM((2,) + down_shape, F32),
                        pltpu.VMEM(up_shape, BF16), pltpu.VMEM(up_shape, BF16), pltpu.VMEM(down_shape, BF16),
                        pltpu.SemaphoreType.DMA((2, 3))],
    )
    return pl.pallas_call(
        _experts_kernel,
        grid_spec=grid_spec,
        out_shape=jax.ShapeDtypeStruct(xs.shape, F32),
        input_output_aliases={6: 0},
        compiler_params=pltpu.CompilerParams(
            dimension_semantics=("arbitrary",), vmem_limit_bytes=VMEM_LIMIT),
        name="experts",
    )(plan["tile_expert"], plan["n_used"], plan["tile_first"], plan["tile_slot"], plan["tile_next"], plan["tile_rows"],
      xs, wg, wu, wd)


def _combine_kernel(local_start_ref, n_chunks_ref, chunk_dst_ref, ys_ref, ri_ref, rw_ref, x1_ref, mod_ref,
                    gain_ref, o_ref, buf_ref, sem_ref, *, r_loc, tiles_per_batch):
    b = pl.program_id(0)
    nb = pl.num_programs(0)
    slot = b % 2
    nch_max = r_loc // CHUNK

    def chunk_copy(tile, sl, j):
        row = pl.multiple_of(j * CHUNK, CHUNK)
        src = pl.multiple_of(chunk_dst_ref[tile * nch_max + j], CHUNK)
        return pltpu.make_async_copy(ys_ref.at[pl.ds(src, CHUNK), :], buf_ref.at[sl, pl.ds(row, CHUNK), :],
                                     sem_ref.at[sl])

    def fetch(tile, sl):
        _for_each(n_chunks_ref[tile], lambda j: chunk_copy(tile, sl, j).start())

    @pl.when(b == 0)
    def _():
        buf_ref[...] = jnp.zeros_like(buf_ref)
        fetch(0, 0)

    @pl.when(b + 1 < nb)
    def _():
        fetch(b + 1, 1 - slot)

    one_chunk = pltpu.make_async_copy(ys_ref.at[pl.ds(0, CHUNK), :], buf_ref.at[slot, pl.ds(0, CHUNK), :],
                                      sem_ref.at[slot])
    _wait_times(one_chunk, n_chunks_ref[b])

    s1, s2 = _local_slots(ri_ref, local_start_ref, b)
    rows = lax.broadcasted_iota(jnp.int32, (r_loc, s1.shape[1]), 0)
    hit1 = rows == s1
    hit2 = rows == s2
    w_row = jnp.sum(jnp.where(hit1, rw_ref[0:1, :], jnp.where(hit2, rw_ref[1:2, :], 0.0)), axis=1, keepdims=True)
    perm = jnp.where(hit1 | hit2, 1.0, 0.0).astype(BF16)
    yw = (buf_ref[slot] * w_row).astype(BF16)
    moe = _dot_tn(perm, yw)
    batch = b // tiles_per_batch
    gate2 = mod_ref[pl.ds(batch, 1), 5 * D_MODEL:6 * D_MODEL]
    x2 = x1_ref[...] + gate2 * moe
    ms = jnp.mean(x2 * x2, axis=-1, keepdims=True)
    o_ref[...] = x2 * lax.rsqrt(ms + EPS) * gain_ref[...]


def _combine(ys, ri, rw, x1, mod, gain, plan, seq, tm):
    tokens = x1.shape[0]
    r_loc = _local_rows(tm)
    row_spec = pl.BlockSpec((8, tm), lambda i, *_: (0, i))
    tok_spec = pl.BlockSpec((tm, D_MODEL), lambda i, *_: (i, 0))
    full = lambda a: pl.BlockSpec(a.shape, lambda i, *_: (0,) * a.ndim)
    grid_spec = pltpu.PrefetchScalarGridSpec(
        num_scalar_prefetch=3,
        grid=(tokens // tm,),
        in_specs=[pl.BlockSpec(memory_space=pl.ANY), row_spec, row_spec, tok_spec, full(mod), full(gain)],
        out_specs=tok_spec,
        scratch_shapes=[pltpu.VMEM((2, r_loc, D_MODEL), F32), pltpu.SemaphoreType.DMA((2,))],
    )
    return pl.pallas_call(
        functools.partial(_combine_kernel, r_loc=r_loc, tiles_per_batch=seq // tm),
        grid_spec=grid_spec,
        out_shape=jax.ShapeDtypeStruct((tokens, D_MODEL), F32),
        compiler_params=pltpu.CompilerParams(
            dimension_semantics=("arbitrary",), vmem_limit_bytes=VMEM_LIMIT),
        name="combine",
    )(plan["local_start"], plan["n_chunks"], plan["chunk_dst"], ys, ri, rw, x1, mod, gain)


def _rotary_tables(seq):
    half = RET_HEAD_DIM // 2
    inv_freq = 1.0 / (ROPE_BASE ** (np.arange(half, dtype=np.float64) / half))
    ang = np.arange(seq, dtype=np.float64)[:, None] * inv_freq[None, :]
    cos = np.cos(ang)
    sin = np.sin(ang)
    f32 = lambda a: jnp.asarray(a.astype(np.float32))
    return f32(np.tile(cos, (1, 4))), f32(np.concatenate([-sin, sin, -sin, sin], axis=1))


def _pick_tile(n, pref):
    t = min(n, pref)
    assert n % t == 0, (n, t)
    return t


def kernel(x, c, ada_w, ada_b, norm1_gain, norm2_gain, w_in, w_out, ret_gn_gain, lam_q1, lam_k1, lam_q2,
           lam_k2, diff_subln_gain, w_group, b_group, w_expert, b_expert, w_gate, w_up, w_down, final_gain):
    batch, seq, d = x.shape
    assert d == D_MODEL and batch <= 8 and ada_w.shape[0] == 1
    layer = 0
    lambda_init = 0.8 - 0.6 * math.exp(-0.3 * layer)
    tokens = batch * seq
    x2 = x.reshape(tokens, d)
    tm = _pick_tile(seq, 512)

    c_pad = jnp.zeros((8, d), F32).at[:batch].set(c)
    mod = _adaln(c_pad, ada_w[layer], ada_b[layer].reshape(1, -1))

    cos_t, sin_t = _rotary_tables(seq)
    rq, rk, rv, rg, dq, dk, dvt = _inproj(
        x2, mod, norm1_gain[layer].reshape(1, d), w_in[layer].astype(BF16), cos_t, sin_t, seq, tm)

    ret_out = _retention(rq, rk, rv, rg, ret_gn_gain[layer].reshape(1, RET_WIDTH), batch, seq,
                         _pick_tile(seq, 256))
    diff_out = _diffattn(
        dq, dk, dvt, lam_q1[layer].reshape(1, -1), lam_k1[layer].reshape(1, -1), lam_q2[layer].reshape(1, -1),
        lam_k2[layer].reshape(1, -1), diff_subln_gain[layer].reshape(1, -1), batch, seq, lambda_init,
        _pick_tile(seq, 1024), 2 * QUERY_CHUNK)

    w_router = jnp.concatenate(
        [w_group[layer].T, jnp.zeros((8 - N_GROUPS, d), F32), w_expert[layer].reshape(d, N_EXPERTS).T], axis=0)
    b_router = jnp.concatenate(
        [b_group[layer], jnp.zeros((8 - N_GROUPS,), F32), b_expert[layer].reshape(N_EXPERTS)]).reshape(-1, 1)
    wr_hi = w_router.astype(BF16)
    wr_lo = (w_router - wr_hi.astype(F32)).astype(BF16)
    x1, h2, ri, rw, cnt = _outproj(ret_out, diff_out, x2, mod, norm2_gain[layer].reshape(1, d),
                                   w_out[layer].astype(BF16), jnp.concatenate([wr_hi, wr_lo], axis=0), b_router,
                                   seq, tm)

    plan = _dispatch_plan(cnt[:, :, 0], tokens, tm)
    xs = _dispatch(h2, ri, plan, tm)
    ys = _experts(xs, plan, w_gate[layer].reshape(N_EXPERTS, d, D_EXPERT),
                  w_up[layer].reshape(N_EXPERTS, d, D_EXPERT), w_down[layer].reshape(N_EXPERTS, D_EXPERT, d))
    out = _combine(ys, ri, rw, x1, mod, final_gain.reshape(1, d), plan, seq, tm)
    return out.reshape(batch, seq, d)
```

```python
import functools
import math

import jax
import jax.numpy as jnp
import numpy as np
from jax import lax
from jax.experimental import pallas as pl
from jax.experimental.pallas import tpu as pltpu

F32 = jnp.float32
BF16 = jnp.bfloat16

D_MODEL = 1024
RET_HEAD_DIM = 64
RET_WIDTH = 512
RET_HEADS = 8
RET_PAIRS = RET_HEADS // 2
DIFF_QK_DIM = 64
DIFF_V_DIM = 128
DIFF_HEADS = 4
DIFF_WIDTH = 512
N_GROUPS = 4
EXPERTS_PER_GROUP = 8
N_EXPERTS = N_GROUPS * EXPERTS_PER_GROUP
D_EXPERT = 512
N_MOD = 6
ROPE_BASE = 10000.0
EPS = 1e-6
LANES = 128
ROUTER_ROWS = 8 + N_EXPERTS
VMEM_LIMIT = 56 * 1024 * 1024


def _dot(a, b):
    return jnp.dot(a, b, preferred_element_type=F32)


def _dot_nt(a, b):
    return lax.dot_general(a, b, (((1,), (1,)), ((), ())), preferred_element_type=F32)


def _dot_tn(a, b):
    return lax.dot_general(a, b, (((0,), (0,)), ((), ())), preferred_element_type=F32)


def _split_bf16(x):
    hi = x.astype(BF16)
    lo = (x - hi.astype(F32)).astype(BF16)
    return hi, lo


def _silu(x):
    return x / (1.0 + jnp.exp(-x))


def _adaln_kernel(c_ref, w_ref, b_ref, o_ref):
    ca = _silu(c_ref[...])
    c_hi, c_lo = _split_bf16(ca)
    w_hi, w_lo = _split_bf16(w_ref[...])
    o_ref[...] = _dot(c_hi, w_hi) + _dot(c_lo, w_hi) + _dot(c_hi, w_lo) + b_ref[...]


def _adaln(c_pad, ada_w, ada_b):
    n_out = ada_w.shape[1]
    tn = D_MODEL
    return pl.pallas_call(
        _adaln_kernel,
        grid=(n_out // tn,),
        in_specs=[
            pl.BlockSpec((8, D_MODEL), lambda j: (0, 0)),
            pl.BlockSpec((D_MODEL, tn), lambda j: (0, j)),
            pl.BlockSpec((1, tn), lambda j: (0, j)),
        ],
        out_specs=pl.BlockSpec((8, tn), lambda j: (0, j)),
        out_shape=jax.ShapeDtypeStruct((8, n_out), F32),
        compiler_params=pltpu.CompilerParams(vmem_limit_bytes=VMEM_LIMIT),
        name="adaln",
    )(c_pad, ada_w, ada_b)


def _norm_modulate(x, gain, shift, scale):
    ms = jnp.mean(x * x, axis=-1, keepdims=True)
    y = x * lax.rsqrt(ms + EPS) * gain
    return y * (1.0 + scale) + shift


def _rotary_slab(x, cos, sin_signed, lane_lo):
    swapped = jnp.where(lane_lo, pltpu.roll(x, 96, 1), pltpu.roll(x, 32, 1))
    return x * cos + swapped * sin_signed


def _inproj_kernel(x_ref, mod_ref, gain_ref, w_ref, cos_ref, sin_ref,
                   rq_ref, rk_ref, rv_ref, rg_ref, dq_ref, dk_ref, dvt_ref, *, tiles_per_batch):
    b = pl.program_id(0) // tiles_per_batch
    shift = mod_ref[pl.ds(b, 1), 0:D_MODEL]
    scale = mod_ref[pl.ds(b, 1), D_MODEL:2 * D_MODEL]
    h = _norm_modulate(x_ref[...], gain_ref[...], shift, scale).astype(BF16)
    cos = cos_ref[...]
    sin = sin_ref[...]
    lane = lax.broadcasted_iota(jnp.int32, cos.shape, 1)
    lane_lo = (lane % 64) < 32

    def proj(chunk):
        return _dot(h, w_ref[:, chunk * RET_WIDTH:(chunk + 1) * RET_WIDTH])

    def rotary(acc, out_ref, post_scale):
        for s in range(RET_WIDTH // LANES):
            sl = slice(s * LANES, (s + 1) * LANES)
            out_ref[:, sl] = (_rotary_slab(acc[:, sl], cos, sin, lane_lo) * post_scale).astype(BF16)

    rotary(proj(0), rq_ref, 1.0)
    rotary(proj(1), rk_ref, RET_HEAD_DIM ** -0.5)
    rv_ref[...] = proj(2).astype(BF16)
    rg_ref[...] = _silu(proj(3)).astype(BF16)
    dq_ref[...] = (proj(4) * (DIFF_QK_DIM ** -0.5 * math.log2(math.e))).astype(BF16)
    dk_ref[...] = proj(5).astype(BF16)
    dvt_ref[...] = proj(6).T.astype(BF16)


def _inproj(x2, mod, gain, w_in, cos_t, sin_t, seq, tm):
    tokens = x2.shape[0]
    tiles_per_batch = seq // tm
    tok_spec = lambda w: pl.BlockSpec((tm, w), lambda i: (i, 0))
    tab_spec = pl.BlockSpec((tm, LANES), lambda i: (i % tiles_per_batch, 0))
    full = lambda a: pl.BlockSpec(a.shape, lambda i: (0,) * a.ndim)
    out = jax.ShapeDtypeStruct((tokens, RET_WIDTH), BF16)
    return pl.pallas_call(
        functools.partial(_inproj_kernel, tiles_per_batch=tiles_per_batch),
        grid=(tokens // tm,),
        in_specs=[tok_spec(D_MODEL), full(mod), full(gain), full(w_in), tab_spec, tab_spec],
        out_specs=[tok_spec(RET_WIDTH)] * 6 + [pl.BlockSpec((DIFF_WIDTH, tm), lambda i: (0, i))],
        out_shape=[out] * 6 + [jax.ShapeDtypeStruct((DIFF_WIDTH, tokens), BF16)],
        compiler_params=pltpu.CompilerParams(
            dimension_semantics=("parallel",), vmem_limit_bytes=VMEM_LIMIT),
        name="inproj",
    )(x2, mod, gain, w_in, cos_t, sin_t)


def _retention_kernel(q_ref, k_ref, v_ref, g_ref, dec_ref, qdec_ref, kdec_ref, rdec_ref,
                      bmask_ref, gmean_ref, gain_ref, o_ref, state_ref, *, chunk):
    @pl.when(pl.program_id(1) == 0)
    def _():
        state_ref[...] = jnp.zeros_like(state_ref)

    lane = lax.broadcasted_iota(jnp.int32, (chunk, LANES), 1)
    first_head = lane < RET_HEAD_DIM
    gmean = gmean_ref[...]
    bmask = bmask_ref[...]
    pairs = range(RET_PAIRS)
    sl = [slice(p * LANES, (p + 1) * LANES) for p in pairs]
    q = [q_ref[:, sl[p]] for p in pairs]
    k = [k_ref[:, sl[p]] for p in pairs]
    v = [v_ref[:, sl[p]] for p in pairs]
    zero = jnp.zeros_like(q[0])
    q_stack = [jnp.concatenate([jnp.where(first_head, q[p], zero), jnp.where(first_head, zero, q[p])], axis=0)
               for p in pairs]
    scores = [(_dot_nt(q_stack[p], k[p]) * dec_ref[p]).astype(BF16) for p in pairs]
    state = [state_ref[p] for p in pairs]
    cross = [_dot(q[p], state[p].astype(BF16)) * qdec_ref[:, sl[p]] for p in pairs]
    k_dec = [(k[p].astype(F32) * kdec_ref[:, sl[p]]).astype(BF16) for p in pairs]
    for p in pairs:
        state_ref[p] = state[p] * rdec_ref[p] + _dot_tn(k_dec[p], v[p]) * bmask
    intra2 = [_dot(scores[p], v[p]) for p in pairs]
    y = [jnp.where(first_head, intra2[p][:chunk], intra2[p][chunk:]) + cross[p] for p in pairs]
    y_split = [_split_bf16(y[p]) for p in pairs]
    mu = [_dot(y_split[p][0], gmean) + _dot(y_split[p][1], gmean) for p in pairs]
    d = [y[p] - mu[p] for p in pairs]
    d_split = [_split_bf16(d[p] * d[p]) for p in pairs]
    var = [_dot(d_split[p][0], gmean) + _dot(d_split[p][1], gmean) for p in pairs]
    for p in pairs:
        yn = d[p] * lax.rsqrt(var[p] + EPS) * gain_ref[:, sl[p]]
        o_ref[:, sl[p]] = (g_ref[:, sl[p]].astype(F32) * yn).astype(BF16)


def _retention_tables(chunk):
    heads = np.arange(RET_HEADS, dtype=np.float64)
    log_gamma = np.log(1.0 - np.exp2(-5.0 - heads))
    idx = np.arange(chunk)
    rel = (idx[:, None] - idx[None, :]).astype(np.float64)
    decay = np.where(rel[None] >= 0, np.exp(log_gamma[:, None, None] * np.maximum(rel, 0.0)[None]), 0.0)
    dec2 = decay.reshape(RET_PAIRS, 2 * chunk, chunk)
    lane_lg = np.repeat(log_gamma, RET_HEAD_DIM)
    qdec = np.exp(lane_lg[None, :] * (idx + 1)[:, None])
    kdec = np.exp(lane_lg[None, :] * (chunk - 1 - idx)[:, None])
    rdec = np.exp(lane_lg * chunk).reshape(RET_PAIRS, LANES, 1) * np.ones((1, 1, LANES))
    blk = np.arange(LANES) // RET_HEAD_DIM
    bmask = (blk[:, None] == blk[None, :]).astype(np.float64)
    f32 = lambda a: jnp.asarray(a.astype(np.float32))
    return f32(dec2), f32(qdec), f32(kdec), f32(rdec), f32(bmask), f32(bmask / RET_HEAD_DIM).astype(BF16)


def _retention(rq, rk, rv, rg, gn_gain, batch, seq, chunk):
    nc = seq // chunk
    dec2, qdec, kdec, rdec, bmask, gmean = _retention_tables(chunk)
    tok_spec = pl.BlockSpec((chunk, RET_WIDTH), lambda b, n: (b * nc + n, 0))
    full = lambda a: pl.BlockSpec(a.shape, lambda b, n: (0,) * a.ndim)
    return pl.pallas_call(
        functools.partial(_retention_kernel, chunk=chunk),
        grid=(batch, nc),
        in_specs=[tok_spec] * 4 + [full(dec2), full(qdec), full(kdec), full(rdec), full(bmask),
                                   full(gmean), full(gn_gain)],
        out_specs=tok_spec,
        out_shape=jax.ShapeDtypeStruct(rq.shape, BF16),
        scratch_shapes=[pltpu.VMEM((RET_PAIRS, LANES, LANES), F32)],
        compiler_params=pltpu.CompilerParams(
            dimension_semantics=("parallel", "arbitrary"), vmem_limit_bytes=VMEM_LIMIT),
        name="retention",
    )(rq, rk, rv, rg, dec2, qdec, kdec, rdec, bmask, gmean, gn_gain)


NEG_BIG = -1e30


V_EXT_ROWS = DIFF_V_DIM + 16
QUERY_CHUNK = 256
SCORES_AHEAD_FULL = 3
SCORES_AHEAD_DIAG = 3
STAGE_SLOTS = 16


def _diag_chunks(tq, tk, d):
    assert tk == 2 * QUERY_CHUNK
    per_softmax = tq // QUERY_CHUNK
    out = []
    for c in range(2 * per_softmax):
        q0 = (c % per_softmax) * QUERY_CHUNK
        if q0 + QUERY_CHUNK - 1 < d * tk:
            continue
        kind = "full" if q0 >= (d + 1) * tk else ("tri" if q0 == d * tk else "low_tri")
        out.append((c, kind))
    return out


def _accumulate(acc_ref, cs, alpha, pv):
    acc_ref[:, cs] = alpha * acc_ref[:, cs] + pv


def _diffattn_kernel(q_ref, k_ref, vt_ref, lq1_ref, lk1_ref, lq2_ref, lk2_ref, gain_ref, bias_ref, o_ref,
                     qs_ref, vext_ref, m_ref, acc_ref, stage_ref, *, tq, tk, lambda_init):
    i = pl.program_id(2)
    nk = vext_ref.shape[0]

    @pl.when(i == 0)
    def _():
        for j in range(nk):
            vext_ref[j, 0:DIFF_V_DIM, :] = vt_ref[:, j * tk:(j + 1) * tk]
            vext_ref[j, DIFF_V_DIM:V_EXT_ROWS, :] = jnp.ones((V_EXT_ROWS - DIFF_V_DIM, tk), BF16)

    q = q_ref[...]
    lane = lax.broadcasted_iota(jnp.int32, q.shape, 1)
    zero = jnp.zeros_like(q)
    qs_ref[0:tq, :] = jnp.where(lane < DIFF_QK_DIM, q, zero)
    qs_ref[tq:2 * tq, :] = jnp.where(lane < DIFF_QK_DIM, zero, q)
    m_ref[...] = jnp.full_like(m_ref, NEG_BIG)
    acc_ref[...] = jnp.zeros_like(acc_ref)

    def step(work, n_ahead):
        chunk = lambda c: slice(c * QUERY_CHUNK, (c + 1) * QUERY_CHUNK)

        def scores(j, c, kind):
            n_keys = QUERY_CHUNK if kind == "tri" else tk
            start = pl.multiple_of(j * tk, tk)
            return _dot_nt(k_ref[pl.ds(start, n_keys), :], qs_ref[chunk(c), :])

        ahead = [scores(*work[n]) for n in range(min(n_ahead, len(work)))]
        pending = None
        for n, (j, c, kind) in enumerate(work):
            cs = chunk(c)
            st = ahead.pop(0)
            if n + n_ahead < len(work):
                ahead.append(scores(*work[n + n_ahead]))
            slot = n % stage_ref.shape[0]
            n_keys = st.shape[0]
            stage_ref[slot, 0:n_keys, :] = st
            if kind == "full":
                st = stage_ref[slot]
            else:
                causal = stage_ref[slot, n_keys - QUERY_CHUNK:n_keys, :] + bias_ref[...]
                st = causal if kind == "tri" else jnp.concatenate(
                    [stage_ref[slot, 0:n_keys - QUERY_CHUNK, :], causal], axis=0)
            m_old = m_ref[:, cs]
            m_new = jnp.maximum(m_old, jnp.max(st, axis=0, keepdims=True))
            alpha = jnp.exp2(m_old - m_new)
            p = jnp.exp2(st - m_new).astype(BF16)
            m_ref[:, cs] = m_new
            pv = _dot(vext_ref[j, :, 0:st.shape[0]], p)
            if pending is not None:
                pending()
            pending = functools.partial(_accumulate, acc_ref, cs, alpha, pv)
        pending()

    tiles_per_q = tq // tk
    n_chunks = 2 * tq // QUERY_CHUNK

    def full_tiles(it):
        return [(it * tiles_per_q + d, c, "full") for d in range(tiles_per_q) for c in range(n_chunks)]

    lax.fori_loop(0, i, lambda it, c: (step(full_tiles(it), SCORES_AHEAD_FULL), c)[1], 0)
    kind_order = {"full": 0, "low_tri": 1, "tri": 2}
    diag = lambda it: [(it * tiles_per_q + d, c, kind) for d in range(tiles_per_q)
                       for c, kind in sorted(_diag_chunks(tq, tk, d), key=lambda ck: kind_order[ck[1]])]
    lax.fori_loop(i, i + 1, lambda it, c: (step(diag(it), SCORES_AHEAD_DIAG), c)[1], 0)

    lam = (jnp.exp(jnp.sum(lq1_ref[...] * lk1_ref[...], axis=-1, keepdims=True))
           - jnp.exp(jnp.sum(lq2_ref[...] * lk2_ref[...], axis=-1, keepdims=True)) + lambda_init)
    acc = acc_ref[...]
    o2 = acc[0:DIFF_V_DIM, :] * (1.0 / acc[DIFF_V_DIM:DIFF_V_DIM + 1, :])
    o = (o2[:, :tq] - lam * o2[:, tq:]).T
    ms = jnp.mean(o * o, axis=-1, keepdims=True)
    o = o * lax.rsqrt(ms + EPS) * gain_ref[...] * (1.0 - lambda_init)
    o_ref[...] = o.astype(BF16)


def _diffattn(dq, dk, dvt, lam_q1, lam_k1, lam_q2, lam_k2, gain, batch, seq, lambda_init, tq, tk):
    nq = seq // tq
    q_spec = pl.BlockSpec((tq, LANES), lambda b, h, i: (b * nq + i, h))
    k_spec = pl.BlockSpec((seq, LANES), lambda b, h, i: (b, h))
    vt_spec = pl.BlockSpec((DIFF_V_DIM, seq), lambda b, h, i: (h, b))
    vec = lambda a: pl.BlockSpec(a.shape, lambda b, h, i: (0, 0))
    key = np.arange(QUERY_CHUNK)[:, None]
    query = np.arange(QUERY_CHUNK)[None, :]
    bias = jnp.asarray(np.where(key <= query, 0.0, NEG_BIG), F32)
    return pl.pallas_call(
        functools.partial(_diffattn_kernel, tq=tq, tk=tk, lambda_init=lambda_init),
        grid=(batch, DIFF_HEADS, nq),
        in_specs=[q_spec, k_spec, vt_spec, vec(lam_q1), vec(lam_k1), vec(lam_q2), vec(lam_k2), vec(gain),
                  vec(bias)],
        out_specs=q_spec,
        out_shape=jax.ShapeDtypeStruct(dq.shape, BF16),
        scratch_shapes=[
            pltpu.VMEM((2 * tq, LANES), BF16),
            pltpu.VMEM((seq // tk, V_EXT_ROWS, tk), BF16),
            pltpu.VMEM((1, 2 * tq), F32),
            pltpu.VMEM((V_EXT_ROWS, 2 * tq), F32),
            pltpu.VMEM((STAGE_SLOTS, tk, QUERY_CHUNK), F32),
        ],
        compiler_params=pltpu.CompilerParams(
            dimension_semantics=("parallel", "parallel", "arbitrary"), vmem_limit_bytes=VMEM_LIMIT),
        name="diffattn",
    )(dq, dk, dvt, lam_q1, lam_k1, lam_q2, lam_k2, gain, bias)


def _route(logits):
    r = [logits[g:g + 1, :] for g in range(N_GROUPS)]
    gmax = jnp.maximum(jnp.maximum(r[0], r[1]), jnp.maximum(r[2], r[3]))
    g_idx = jnp.where(r[0] == gmax, 0, jnp.where(r[1] == gmax, 1, jnp.where(r[2] == gmax, 2, 3)))
    denom = sum(jnp.exp(rg - gmax) for rg in r)
    g_weight = 1.0 / denom
    sel = jnp.zeros((EXPERTS_PER_GROUP, logits.shape[1]), F32)
    for g in range(N_GROUPS):
        rows = logits[8 + g * EXPERTS_PER_GROUP:8 + (g + 1) * EXPERTS_PER_GROUP, :]
        sel = jnp.where(g_idx == g, rows, sel)
    eidx = lax.broadcasted_iota(jnp.int32, sel.shape, 0)
    v1 = jnp.max(sel, axis=0, keepdims=True)
    i1 = jnp.min(jnp.where(sel == v1, eidx, EXPERTS_PER_GROUP), axis=0, keepdims=True)
    sel2 = jnp.where(eidx == i1, -jnp.inf, sel)
    v2 = jnp.max(sel2, axis=0, keepdims=True)
    i2 = jnp.min(jnp.where(sel2 == v2, eidx, EXPERTS_PER_GROUP), axis=0, keepdims=True)
    e2 = jnp.exp(v2 - v1)
    w1 = g_weight / (1.0 + e2)
    w2 = g_weight * e2 / (1.0 + e2)
    return g_idx, i1, i2, w1, w2


OUTPROJ_PARTS = 2


def _outproj_kernel(ret_ref, diff_ref, x_ref, mod_ref, gain_ref, wo_ref, wr_ref, br_ref, tri_ref,
                    x1_ref, h2_ref, ri_ref, rw_ref, cnt_ref, *, tiles_per_batch):
    b = pl.program_id(0) // tiles_per_batch
    gate1 = mod_ref[pl.ds(b, 1), 2 * D_MODEL:3 * D_MODEL]
    shift = mod_ref[pl.ds(b, 1), 3 * D_MODEL:4 * D_MODEL]
    scale = mod_ref[pl.ds(b, 1), 4 * D_MODEL:5 * D_MODEL]
    wr = wr_ref[...]
    tm = x_ref.shape[0]
    parts = [slice(n * tm // OUTPROJ_PARTS, (n + 1) * tm // OUTPROJ_PARTS) for n in range(OUTPROJ_PARTS)]
    mix = [_dot(ret_ref[r, :], wo_ref[0:RET_WIDTH, :]) + _dot(diff_ref[r, :], wo_ref[RET_WIDTH:, :]) for r in parts]
    for r, m in zip(parts, mix):
        x1_ref[r, :] = x_ref[r, :] + gate1 * m
    h_split = [_split_bf16(_norm_modulate(x1_ref[r, :], gain_ref[...], shift, scale)) for r in parts]
    for r, (h_hi, _) in zip(parts, h_split):
        h2_ref[r, :] = h_hi
    by_hi = [_dot_nt(wr, h_hi) for h_hi, _ in h_split]
    by_lo = [_dot_nt(wr[:ROUTER_ROWS], h_lo) for _, h_lo in h_split]
    logits = [a[:ROUTER_ROWS] + a[ROUTER_ROWS:] + c + br_ref[...] for a, c in zip(by_hi, by_lo)]
    routed = [_route(lg) for lg in logits]
    g_idx, i1, i2, w1, w2 = [jnp.concatenate([rt[n] for rt in routed], axis=1) for n in range(5)]
    logits = jnp.concatenate(logits, axis=1)
    e1 = g_idx * EXPERTS_PER_GROUP + i1
    e2 = g_idx * EXPERTS_PER_GROUP + i2
    eidx = lax.broadcasted_iota(jnp.int32, (N_EXPERTS, logits.shape[1]), 0)
    hit1 = eidx == e1
    hit2 = eidx == e2
    onehot = jnp.where(hit1 | hit2, 1.0, 0.0)
    before = _dot(onehot.astype(BF16), tri_ref[...])
    r1 = jnp.sum(jnp.where(hit1, before, 0.0), axis=0, keepdims=True)
    r2 = jnp.sum(jnp.where(hit2, before, 0.0), axis=0, keepdims=True)
    zi = jnp.zeros_like(e1)
    ri_ref[...] = jnp.concatenate([e1, e2, r1.astype(jnp.int32), r2.astype(jnp.int32), zi, zi, zi, zi], axis=0)
    zf = jnp.zeros_like(w1)
    rw_ref[...] = jnp.concatenate([w1, w2, zf, zf, zf, zf, zf, zf], axis=0)
    counts = jnp.sum(onehot, axis=1, keepdims=True)
    cnt_ref[0] = jnp.broadcast_to(counts, (N_EXPERTS, LANES)).astype(jnp.int32)


def _outproj(ret_out, diff_out, x2, mod, gain, w_out, wr, br, seq, tm):
    tokens = x2.shape[0]
    tiles_per_batch = seq // tm
    n_tiles = tokens // tm
    tri = jnp.asarray(np.arange(tm)[:, None] < np.arange(tm)[None, :], BF16)
    tok_spec = lambda w: pl.BlockSpec((tm, w), lambda i: (i, 0))
    row_spec = pl.BlockSpec((8, tm), lambda i: (0, i))
    full = lambda a: pl.BlockSpec(a.shape, lambda i: (0,) * a.ndim)
    return pl.pallas_call(
        functools.partial(_outproj_kernel, tiles_per_batch=tiles_per_batch),
        grid=(n_tiles,),
        in_specs=[tok_spec(RET_WIDTH), tok_spec(DIFF_WIDTH), tok_spec(D_MODEL), full(mod), full(gain),
                  full(w_out), full(wr), full(br), full(tri)],
        out_specs=[tok_spec(D_MODEL), tok_spec(D_MODEL), row_spec, row_spec,
                   pl.BlockSpec((1, N_EXPERTS, LANES), lambda i: (i, 0, 0))],
        out_shape=[jax.ShapeDtypeStruct((tokens, D_MODEL), F32),
                   jax.ShapeDtypeStruct((tokens, D_MODEL), BF16),
                   jax.ShapeDtypeStruct((8, tokens), jnp.int32),
                   jax.ShapeDtypeStruct((8, tokens), F32),
                   jax.ShapeDtypeStruct((n_tiles, N_EXPERTS, LANES), jnp.int32)],
        compiler_params=pltpu.CompilerParams(
            dimension_semantics=("parallel",), vmem_limit_bytes=VMEM_LIMIT),
        name="outproj",
    )(ret_out, diff_out, x2, mod, gain, w_out, wr, br, tri)


CHUNK = 8
TMX = 512


def _local_rows(tm):
    rows = 2 * tm + N_EXPERTS * (CHUNK - 1)
    return (rows + 15) // 16 * 16


def _sorted_rows_alloc(tokens, tm):
    worst = 2 * tokens + (tokens // tm) * N_EXPERTS * (CHUNK - 1) + N_EXPERTS * (TMX - CHUNK)
    return (worst + TMX - 1) // TMX * TMX


def _dispatch_plan(cnt, tokens, tm):
    i32 = jnp.int32
    nch_max = _local_rows(tm) // CHUNK
    pad = (cnt + CHUNK - 1) // CHUNK * CHUNK
    local_end = jnp.cumsum(pad, axis=1)
    local_start = local_end - pad
    seg_rows = jnp.sum(pad, axis=0)
    seg_pad = (seg_rows + TMX - 1) // TMX * TMX
    seg_end = jnp.cumsum(seg_pad)
    seg_start = seg_end - seg_pad
    run_dst = seg_start[None, :] + jnp.cumsum(pad, axis=0) - pad
    row = CHUNK * jnp.arange(nch_max, dtype=i32)[None, :, None]
    owns = (local_start[:, None, :] <= row) & (row < local_end[:, None, :])
    chunk_dst = row[:, :, 0] + jnp.sum(jnp.where(owns, (run_dst - local_start)[:, None, :], 0), axis=-1)
    m = TMX * jnp.arange(_sorted_rows_alloc(tokens, tm) // TMX, dtype=i32)
    tile_expert = jnp.minimum(jnp.sum(seg_end[None, :] <= m[:, None], axis=-1), N_EXPERTS - 1)
    towns = (seg_start[None, :] <= m[:, None]) & (m[:, None] < seg_end[None, :])
    used = seg_pad > 0
    parity = (jnp.cumsum(used) - used) % 2
    eids = jnp.arange(N_EXPERTS, dtype=i32)
    later_used = (eids[None, :] > eids[:, None]) & used[None, :]
    next_used = jnp.min(jnp.where(later_used, eids[None, :], N_EXPERTS), axis=1)
    next_used = jnp.where(next_used == N_EXPERTS, -1, next_used)
    pick = lambda per_expert: jnp.sum(jnp.where(towns, per_expert[None, :], 0), axis=-1)
    tile_first = jnp.sum(jnp.where(towns & (seg_start[None, :] == m[:, None]), 1, 0), axis=-1)
    tile_next = jnp.where(jnp.any(towns, axis=-1), pick(next_used), -1)
    return dict(
        tile_first=tile_first.astype(i32),
        tile_slot=pick(parity).astype(i32),
        tile_next=tile_next.astype(i32),
        tile_rows=jnp.clip(pick(seg_start + seg_rows) - m, 0, TMX).astype(i32),
        local_start=local_start.reshape(-1).astype(i32),
        n_chunks=(local_end[:, -1] // CHUNK).astype(i32),
        chunk_dst=chunk_dst.reshape(-1).astype(i32),
        tail_base=(seg_start + seg_rows).astype(i32),
        tail_rows=(seg_pad - seg_rows).astype(i32),
        tile_expert=tile_expert.astype(i32),
        n_used=(seg_end[-1:] // TMX).astype(i32),
    )


WAIT_UNROLL = 8


def _wait_times(copy, n):
    lax.fori_loop(0, n // WAIT_UNROLL, lambda i, c: ([copy.wait() for _ in range(WAIT_UNROLL)], c)[1], 0)
    lax.fori_loop(0, n % WAIT_UNROLL, lambda i, c: (copy.wait(), c)[1], 0)


def _for_each(n, body, unroll=4):
    main = n // unroll
    lax.fori_loop(0, main, lambda i, c: ([body(i * unroll + u) for u in range(unroll)], c)[1], 0)
    lax.fori_loop(main * unroll, n, lambda j, c: (body(j), c)[1], 0)


def _local_slots(ri_ref, local_start_ref, tile):
    e1, e2 = ri_ref[0:1, :], ri_ref[1:2, :]
    s1, s2 = ri_ref[2:3, :], ri_ref[3:4, :]
    for e in range(N_EXPERTS):
        start = local_start_ref[tile * N_EXPERTS + e]
        s1 = s1 + jnp.where(e1 == e, start, 0)
        s2 = s2 + jnp.where(e2 == e, start, 0)
    return s1, s2


def _dispatch_kernel(local_start_ref, n_chunks_ref, chunk_dst_ref, tail_base_ref, tail_rows_ref, n_used_ref,
                     h_ref, ri_ref, xs_ref, buf_ref, zero_ref, sem_ref, tail_sem_ref, *, r_loc):
    b = pl.program_id(0)
    nb = pl.num_programs(0)
    slot = b % 2
    nch_max = r_loc // CHUNK

    def chunk_copy(tile, sl, j):
        row = pl.multiple_of(j * CHUNK, CHUNK)
        dst = pl.multiple_of(chunk_dst_ref[tile * nch_max + j], CHUNK)
        return pltpu.make_async_copy(buf_ref.at[sl, pl.ds(row, CHUNK), :], xs_ref.at[pl.ds(dst, CHUNK), :],
                                     sem_ref.at[sl])

    def drain(tile, sl):
        one_chunk = pltpu.make_async_copy(buf_ref.at[sl, pl.ds(0, CHUNK), :], xs_ref.at[pl.ds(0, CHUNK), :],
                                          sem_ref.at[sl])
        _wait_times(one_chunk, n_chunks_ref[tile])

    @pl.when(b >= 2)
    def _():
        drain(b - 2, slot)

    s1, s2 = _local_slots(ri_ref, local_start_ref, b)
    rows = lax.broadcasted_iota(jnp.int32, (r_loc, s1.shape[1]), 0)
    perm = jnp.where((rows == s1) | (rows == s2), 1.0, 0.0).astype(BF16)
    buf_ref[slot] = _dot(perm, h_ref[...])
    _for_each(n_chunks_ref[b], lambda j: chunk_copy(b, slot, j).start())

    def tail_pieces(e, act):
        n = tail_rows_ref[e]
        size = TMX // 2
        while size >= CHUNK:
            dst = pl.multiple_of(tail_base_ref[e] + (n & (-2 * size)), CHUNK)
            cp = pltpu.make_async_copy(zero_ref.at[pl.ds(0, size), :], xs_ref.at[pl.ds(dst, size), :],
                                       tail_sem_ref.at[0])
            pl.when((n & size) != 0)(functools.partial(act, cp))
            size //= 2

    def unused_tile_copy(m):
        dst = pl.multiple_of(m * TMX, TMX)
        return pltpu.make_async_copy(zero_ref, xs_ref.at[pl.ds(dst, TMX), :], tail_sem_ref.at[1])

    n_alloc = xs_ref.shape[0] // TMX

    @pl.when(b == 0)
    def _():
        zero_ref[...] = jnp.zeros_like(zero_ref)
        lax.fori_loop(0, N_EXPERTS, lambda e, c: (tail_pieces(e, lambda cp: cp.start()), c)[1], 0)
        lax.fori_loop(n_used_ref[0], n_alloc, lambda m, c: (unused_tile_copy(m).start(), c)[1], 0)

    @pl.when(b == nb - 1)
    def _():
        lax.fori_loop(0, N_EXPERTS, lambda e, c: (tail_pieces(e, lambda cp: cp.wait()), c)[1], 0)
        lax.fori_loop(n_used_ref[0], n_alloc, lambda m, c: (unused_tile_copy(m).wait(), c)[1], 0)

        @pl.when(b >= 1)
        def _():
            drain(b - 1, 1 - slot)

        drain(b, slot)


def _dispatch(h2, ri, plan, tm):
    tokens = h2.shape[0]
    r_loc = _local_rows(tm)
    grid_spec = pltpu.PrefetchScalarGridSpec(
        num_scalar_prefetch=6,
        grid=(tokens // tm,),
        in_specs=[pl.BlockSpec((tm, D_MODEL), lambda i, *_: (i, 0)),
                  pl.BlockSpec((8, tm), lambda i, *_: (0, i))],
        out_specs=pl.BlockSpec(memory_space=pl.ANY),
        scratch_shapes=[pltpu.VMEM((2, r_loc, D_MODEL), F32), pltpu.VMEM((TMX, D_MODEL), F32),
                        pltpu.SemaphoreType.DMA((2,)), pltpu.SemaphoreType.DMA((2,))],
    )
    return pl.pallas_call(
        functools.partial(_dispatch_kernel, r_loc=r_loc),
        grid_spec=grid_spec,
        out_shape=jax.ShapeDtypeStruct((_sorted_rows_alloc(tokens, tm), D_MODEL), F32),
        compiler_params=pltpu.CompilerParams(
            dimension_semantics=("arbitrary",), vmem_limit_bytes=VMEM_LIMIT),
        name="dispatch",
    )(plan["local_start"], plan["n_chunks"], plan["chunk_dst"], plan["tail_base"], plan["tail_rows"], plan["n_used"],
      h2, ri)


def _experts_kernel(tile_expert_ref, n_used_ref, first_ref, slot_ref, next_ref, rows_ref, xs_ref, wg_hbm, wu_hbm, wd_hbm,
                    ys_ref, wg_st, wu_st, wd_st, wg_bf, wu_bf, wd_bf, a_ref, u_ref, sem_ref):
    m = pl.program_id(0)

    def weight_copies(e, s):
        return [pltpu.make_async_copy(src.at[e], dst.at[s], sem_ref.at[s, n])
                for n, (src, dst) in enumerate([(wg_hbm, wg_st), (wu_hbm, wu_st), (wd_hbm, wd_st)])]

    @pl.when(m < n_used_ref[0])
    def _():
        @pl.when(first_ref[m] == 1)
        def _():
            s = slot_ref[m]

            @pl.when(m == 0)
            def _():
                for cp in weight_copies(tile_expert_ref[0], 0):
                    cp.start()

            for cp in weight_copies(tile_expert_ref[m], s):
                cp.wait()

            @pl.when(next_ref[m] >= 0)
            def _():
                for cp in weight_copies(next_ref[m], 1 - s):
                    cp.start()

            wg_bf[...] = wg_st[s].astype(BF16)
            wu_bf[...] = wu_st[s].astype(BF16)
            wd_bf[...] = wd_st[s].astype(BF16)

        def mlp(rows):
            x = xs_ref[rows, :].astype(BF16)
            a_ref[rows, :] = _dot(x, wg_bf[...])
            u_ref[rows, :] = _dot(x, wu_bf[...])
            hid = (_silu(a_ref[rows, :]) * u_ref[rows, :]).astype(BF16)
            ys_ref[rows, :] = _dot(hid, wd_bf[...])

        half = TMX // 2

        @pl.when(rows_ref[m] > half)
        def _():
            mlp(slice(0, TMX))

        @pl.when(rows_ref[m] <= half)
        def _():
            mlp(slice(0, half))
            ys_ref[half:, :] = jnp.zeros((TMX - half, D_MODEL), F32)


def _experts(xs, plan, wg, wu, wd):
    n_tiles = xs.shape[0] // TMX
    last_used = lambda m, n_used: jnp.minimum(m, n_used[0] - 1)
    row_spec = pl.BlockSpec((TMX, D_MODEL), lambda m, te, nu, *_: (last_used(m, nu), 0))
    hbm = pl.BlockSpec(memory_space=pl.ANY)
    up_shape, down_shape = (D_MODEL, D_EXPERT), (D_EXPERT, D_MODEL)
    grid_spec = pltpu.PrefetchScalarGridSpec(
        num_scalar_prefetch=6,
        grid=(n_tiles,),
        in_specs=[row_spec, hbm, hbm, hbm],
        out_specs=row_spec,
        scratch_shapes=[pltpu.VMEM((2,) + up_shape, F32), pltpu.VMEM((2,) + up_shape, F32),
                        pltpu.VMEM((2,) + down_shape, F32),
                        pltpu.VMEM(up_shape, BF16), pltpu.VMEM(up_shape, BF16), pltpu.VMEM(down_shape, BF16),
                        pltpu.VMEM((TMX, D_EXPERT), F32), pltpu.VMEM((TMX, D_EXPERT), F32),
                        pltpu.SemaphoreType.DMA((2, 3))],
    )
    return pl.pallas_call(
        _experts_kernel,
        grid_spec=grid_spec,
        out_shape=jax.ShapeDtypeStruct(xs.shape, F32),
        input_output_aliases={6: 0},
        compiler_params=pltpu.CompilerParams(
            dimension_semantics=("arbitrary",), vmem_limit_bytes=VMEM_LIMIT),
        name="experts",
    )(plan["tile_expert"], plan["n_used"], plan["tile_first"], plan["tile_slot"], plan["tile_next"], plan["tile_rows"],
      xs, wg, wu, wd)


def _combine_kernel(local_start_ref, n_chunks_ref, chunk_dst_ref, ys_ref, ri_ref, rw_ref, x1_ref, mod_ref,
                    gain_ref, o_ref, buf_ref, sem_ref, *, r_loc, tiles_per_batch):
    b = pl.program_id(0)
    nb = pl.num_programs(0)
    slot = b % 2
    nch_max = r_loc // CHUNK

    def chunk_copy(tile, sl, j):
        row = pl.multiple_of(j * CHUNK, CHUNK)
        src = pl.multiple_of(chunk_dst_ref[tile * nch_max + j], CHUNK)
        return pltpu.make_async_copy(ys_ref.at[pl.ds(src, CHUNK), :], buf_ref.at[sl, pl.ds(row, CHUNK), :],
                                     sem_ref.at[sl])

    def fetch(tile, sl):
        _for_each(n_chunks_ref[tile], lambda j: chunk_copy(tile, sl, j).start())

    @pl.when(b == 0)
    def _():
        buf_ref[...] = jnp.zeros_like(buf_ref)
        fetch(0, 0)

    @pl.when(b + 1 < nb)
    def _():
        fetch(b + 1, 1 - slot)

    one_chunk = pltpu.make_async_copy(ys_ref.at[pl.ds(0, CHUNK), :], buf_ref.at[slot, pl.ds(0, CHUNK), :],
                                      sem_ref.at[slot])
    _wait_times(one_chunk, n_chunks_ref[b])

    s1, s2 = _local_slots(ri_ref, local_start_ref, b)
    rows = lax.broadcasted_iota(jnp.int32, (r_loc, s1.shape[1]), 0)
    hit1 = rows == s1
    hit2 = rows == s2
    w_row = jnp.sum(jnp.where(hit1, rw_ref[0:1, :], jnp.where(hit2, rw_ref[1:2, :], 0.0)), axis=1, keepdims=True)
    perm = jnp.where(hit1 | hit2, 1.0, 0.0).astype(BF16)
    yw = (buf_ref[slot] * w_row).astype(BF16)
    moe = _dot_tn(perm, yw)
    batch = b // tiles_per_batch
    gate2 = mod_ref[pl.ds(batch, 1), 5 * D_MODEL:6 * D_MODEL]
    x2 = x1_ref[...] + gate2 * moe
    ms = jnp.mean(x2 * x2, axis=-1, keepdims=True)
    o_ref[...] = x2 * lax.rsqrt(ms + EPS) * gain_ref[...]


def _combine(ys, ri, rw, x1, mod, gain, plan, seq, tm):
    tokens = x1.shape[0]
    r_loc = _local_rows(tm)
    row_spec = pl.BlockSpec((8, tm), lambda i, *_: (0, i))
    tok_spec = pl.BlockSpec((tm, D_MODEL), lambda i, *_: (i, 0))
    full = lambda a: pl.BlockSpec(a.shape, lambda i, *_: (0,) * a.ndim)
    grid_spec = pltpu.PrefetchScalarGridSpec(
        num_scalar_prefetch=3,
        grid=(tokens // tm,),
        in_specs=[pl.BlockSpec(memory_space=pl.ANY), row_spec, row_spec, tok_spec, full(mod), full(gain)],
        out_specs=tok_spec,
        scratch_shapes=[pltpu.VMEM((2, r_loc, D_MODEL), F32), pltpu.SemaphoreType.DMA((2,))],
    )
    return pl.pallas_call(
        functools.partial(_combine_kernel, r_loc=r_loc, tiles_per_batch=seq // tm),
        grid_spec=grid_spec,
        out_shape=jax.ShapeDtypeStruct((tokens, D_MODEL), F32),
        compiler_params=pltpu.CompilerParams(
            dimension_semantics=("arbitrary",), vmem_limit_bytes=VMEM_LIMIT),
        name="combine",
    )(plan["local_start"], plan["n_chunks"], plan["chunk_dst"], ys, ri, rw, x1, mod, gain)


def _rotary_tables(seq):
    half = RET_HEAD_DIM // 2
    inv_freq = 1.0 / (ROPE_BASE ** (np.arange(half, dtype=np.float64) / half))
    ang = np.arange(seq, dtype=np.float64)[:, None] * inv_freq[None, :]
    cos = np.cos(ang)
    sin = np.sin(ang)
    f32 = lambda a: jnp.asarray(a.astype(np.float32))
    return f32(np.tile(cos, (1, 4))), f32(np.concatenate([-sin, sin, -sin, sin], axis=1))


def _pick_tile(n, pref):
    t = min(n, pref)
    assert n % t == 0, (n, t)
    return t


def kernel(x, c, ada_w, ada_b, norm1_gain, norm2_gain, w_in, w_out, ret_gn_gain, lam_q1, lam_k1, lam_q2,
           lam_k2, diff_subln_gain, w_group, b_group, w_expert, b_expert, w_gate, w_up, w_down, final_gain):
    batch, seq, d = x.shape
    assert d == D_MODEL and batch <= 8 and ada_w.shape[0] == 1
    layer = 0
    lambda_init = 0.8 - 0.6 * math.exp(-0.3 * layer)
    tokens = batch * seq
    x2 = x.reshape(tokens, d)
    tm = _pick_tile(seq, 512)

    c_pad = jnp.zeros((8, d), F32).at[:batch].set(c)
    mod = _adaln(c_pad, ada_w[layer], ada_b[layer].reshape(1, -1))

    cos_t, sin_t = _rotary_tables(seq)
    rq, rk, rv, rg, dq, dk, dvt = _inproj(
        x2, mod, norm1_gain[layer].reshape(1, d), w_in[layer].astype(BF16), cos_t, sin_t, seq, tm)

    ret_out = _retention(rq, rk, rv, rg, ret_gn_gain[layer].reshape(1, RET_WIDTH), batch, seq,
                         _pick_tile(seq, 256))
    diff_out = _diffattn(
        dq, dk, dvt, lam_q1[layer].reshape(1, -1), lam_k1[layer].reshape(1, -1), lam_q2[layer].reshape(1, -1),
        lam_k2[layer].reshape(1, -1), diff_subln_gain[layer].reshape(1, -1), batch, seq, lambda_init,
        _pick_tile(seq, 1024), 2 * QUERY_CHUNK)

    w_router = jnp.concatenate(
        [w_group[layer].T, jnp.zeros((8 - N_GROUPS, d), F32), w_expert[layer].reshape(d, N_EXPERTS).T], axis=0)
    b_router = jnp.concatenate(
        [b_group[layer], jnp.zeros((8 - N_GROUPS,), F32), b_expert[layer].reshape(N_EXPERTS)]).reshape(-1, 1)
    wr_hi = w_router.astype(BF16)
    wr_lo = (w_router - wr_hi.astype(F32)).astype(BF16)
    x1, h2, ri, rw, cnt = _outproj(ret_out, diff_out, x2, mod, norm2_gain[layer].reshape(1, d),
                                   w_out[layer].astype(BF16), jnp.concatenate([wr_hi, wr_lo], axis=0), b_router,
                                   seq, tm)

    plan = _dispatch_plan(cnt[:, :, 0], tokens, tm)
    xs = _dispatch(h2, ri, plan, tm)
    ys = _experts(xs, plan, w_gate[layer].reshape(N_EXPERTS, d, D_EXPERT),
                  w_up[layer].reshape(N_EXPERTS, d, D_EXPERT), w_down[layer].reshape(N_EXPERTS, D_EXPERT, d))
    out = _combine(ys, ri, rw, x1, mod, final_gain.reshape(1, d), plan, seq, tm)
    return out.reshape(batch, seq, d)
```

```python
import functools
import math

import jax
import jax.numpy as jnp
import numpy as np
from jax import lax
from jax.experimental import pallas as pl
from jax.experimental.pallas import tpu as pltpu

F32 = jnp.float32
BF16 = jnp.bfloat16

D_MODEL = 1024
RET_HEAD_DIM = 64
RET_WIDTH = 512
RET_HEADS = 8
RET_PAIRS = RET_HEADS // 2
DIFF_QK_DIM = 64
DIFF_V_DIM = 128
DIFF_HEADS = 4
DIFF_WIDTH = 512
N_GROUPS = 4
EXPERTS_PER_GROUP = 8
N_EXPERTS = N_GROUPS * EXPERTS_PER_GROUP
D_EXPERT = 512
N_MOD = 6
ROPE_BASE = 10000.0
EPS = 1e-6
LANES = 128
ROUTER_ROWS = 8 + N_EXPERTS
VMEM_LIMIT = 56 * 1024 * 1024


def _dot(a, b):
    return jnp.dot(a, b, preferred_element_type=F32)


def _dot_nt(a, b):
    return lax.dot_general(a, b, (((1,), (1,)), ((), ())), preferred_element_type=F32)


def _dot_tn(a, b):
    return lax.dot_general(a, b, (((0,), (0,)), ((), ())), preferred_element_type=F32)


def _split_bf16(x):
    hi = x.astype(BF16)
    lo = (x - hi.astype(F32)).astype(BF16)
    return hi, lo


def _silu(x):
    return x / (1.0 + jnp.exp(-x))


def _adaln_kernel(c_ref, w_ref, b_ref, o_ref):
    ca = _silu(c_ref[...])
    c_hi, c_lo = _split_bf16(ca)
    w_hi, w_lo = _split_bf16(w_ref[...])
    o_ref[...] = _dot(c_hi, w_hi) + _dot(c_lo, w_hi) + _dot(c_hi, w_lo) + b_ref[...]


def _adaln(c_pad, ada_w, ada_b):
    n_out = ada_w.shape[1]
    tn = D_MODEL
    return pl.pallas_call(
        _adaln_kernel,
        grid=(n_out // tn,),
        in_specs=[
            pl.BlockSpec((8, D_MODEL), lambda j: (0, 0)),
            pl.BlockSpec((D_MODEL, tn), lambda j: (0, j)),
            pl.BlockSpec((1, tn), lambda j: (0, j)),
        ],
        out_specs=pl.BlockSpec((8, tn), lambda j: (0, j)),
        out_shape=jax.ShapeDtypeStruct((8, n_out), F32),
        compiler_params=pltpu.CompilerParams(vmem_limit_bytes=VMEM_LIMIT),
        name="adaln",
    )(c_pad, ada_w, ada_b)


def _norm_modulate(x, gain, shift, scale):
    ms = jnp.mean(x * x, axis=-1, keepdims=True)
    y = x * lax.rsqrt(ms + EPS) * gain
    return y * (1.0 + scale) + shift


def _rotary_slab(x, cos, sin_signed, lane_lo):
    swapped = jnp.where(lane_lo, pltpu.roll(x, 96, 1), pltpu.roll(x, 32, 1))
    return x * cos + swapped * sin_signed


def _inproj_kernel(x_ref, mod_ref, gain_ref, w_ref, cos_ref, sin_ref,
                   rq_ref, rk_ref, rv_ref, rg_ref, dq_ref, dk_ref, dvt_ref, *, tiles_per_batch):
    b = pl.program_id(0) // tiles_per_batch
    shift = mod_ref[pl.ds(b, 1), 0:D_MODEL]
    scale = mod_ref[pl.ds(b, 1), D_MODEL:2 * D_MODEL]
    h = _norm_modulate(x_ref[...], gain_ref[...], shift, scale).astype(BF16)
    cos = cos_ref[...]
    sin = sin_ref[...]
    lane = lax.broadcasted_iota(jnp.int32, cos.shape, 1)
    lane_lo = (lane % 64) < 32

    def proj(chunk):
        return _dot(h, w_ref[:, chunk * RET_WIDTH:(chunk + 1) * RET_WIDTH])

    def rotary(acc, out_ref, post_scale):
        for s in range(RET_WIDTH // LANES):
            sl = slice(s * LANES, (s + 1) * LANES)
            out_ref[:, sl] = (_rotary_slab(acc[:, sl], cos, sin, lane_lo) * post_scale).astype(BF16)

    rotary(proj(0), rq_ref, 1.0)
    rotary(proj(1), rk_ref, RET_HEAD_DIM ** -0.5)
    rv_ref[...] = proj(2).astype(BF16)
    rg_ref[...] = _silu(proj(3)).astype(BF16)
    dq_ref[...] = (proj(4) * (DIFF_QK_DIM ** -0.5 * math.log2(math.e))).astype(BF16)
    dk_ref[...] = proj(5).astype(BF16)
    dvt_ref[...] = proj(6).T.astype(BF16)


def _inproj(x2, mod, gain, w_in, cos_t, sin_t, seq, tm):
    tokens = x2.shape[0]
    tiles_per_batch = seq // tm
    tok_spec = lambda w: pl.BlockSpec((tm, w), lambda i: (i, 0))
    tab_spec = pl.BlockSpec((tm, LANES), lambda i: (i % tiles_per_batch, 0))
    full = lambda a: pl.BlockSpec(a.shape, lambda i: (0,) * a.ndim)
    out = jax.ShapeDtypeStruct((tokens, RET_WIDTH), BF16)
    return pl.pallas_call(
        functools.partial(_inproj_kernel, tiles_per_batch=tiles_per_batch),
        grid=(tokens // tm,),
        in_specs=[tok_spec(D_MODEL), full(mod), full(gain), full(w_in), tab_spec, tab_spec],
        out_specs=[tok_spec(RET_WIDTH)] * 6 + [pl.BlockSpec((DIFF_WIDTH, tm), lambda i: (0, i))],
        out_shape=[out] * 6 + [jax.ShapeDtypeStruct((DIFF_WIDTH, tokens), BF16)],
        compiler_params=pltpu.CompilerParams(
            dimension_semantics=("parallel",), vmem_limit_bytes=VMEM_LIMIT),
        name="inproj",
    )(x2, mod, gain, w_in, cos_t, sin_t)


def _retention_kernel(q_ref, k_ref, v_ref, g_ref, dec_ref, qdec_ref, kdec_ref, rdec_ref,
                      bmask_ref, gmean_ref, gain_ref, o_ref, state_ref, *, chunk):
    @pl.when(pl.program_id(1) == 0)
    def _():
        state_ref[...] = jnp.zeros_like(state_ref)

    lane = lax.broadcasted_iota(jnp.int32, (chunk, LANES), 1)
    first_head = lane < RET_HEAD_DIM
    gmean = gmean_ref[...]
    bmask = bmask_ref[...]
    pairs = range(RET_PAIRS)
    sl = [slice(p * LANES, (p + 1) * LANES) for p in pairs]
    q = [q_ref[:, sl[p]] for p in pairs]
    k = [k_ref[:, sl[p]] for p in pairs]
    v = [v_ref[:, sl[p]] for p in pairs]
    zero = jnp.zeros_like(q[0])
    q_stack = [jnp.concatenate([jnp.where(first_head, q[p], zero), jnp.where(first_head, zero, q[p])], axis=0)
               for p in pairs]
    scores = [(_dot_nt(q_stack[p], k[p]) * dec_ref[p]).astype(BF16) for p in pairs]
    state = [state_ref[p] for p in pairs]
    cross = [_dot(q[p], state[p].astype(BF16)) * qdec_ref[:, sl[p]] for p in pairs]
    k_dec = [(k[p].astype(F32) * kdec_ref[:, sl[p]]).astype(BF16) for p in pairs]
    for p in pairs:
        state_ref[p] = state[p] * rdec_ref[p] + _dot_tn(k_dec[p], v[p]) * bmask
    intra2 = [_dot(scores[p], v[p]) for p in pairs]
    y = [jnp.where(first_head, intra2[p][:chunk], intra2[p][chunk:]) + cross[p] for p in pairs]
    seg_mean = lambda x: _dot(jnp.concatenate(_split_bf16(x), axis=1), gmean)
    mu = [seg_mean(y[p]) for p in pairs]
    d = [y[p] - mu[p] for p in pairs]
    var = [seg_mean(d[p] * d[p]) for p in pairs]
    for p in pairs:
        yn = d[p] * lax.rsqrt(var[p] + EPS) * gain_ref[:, sl[p]]
        o_ref[:, sl[p]] = (g_ref[:, sl[p]].astype(F32) * yn).astype(BF16)


def _retention_tables(chunk):
    heads = np.arange(RET_HEADS, dtype=np.float64)
    log_gamma = np.log(1.0 - np.exp2(-5.0 - heads))
    idx = np.arange(chunk)
    rel = (idx[:, None] - idx[None, :]).astype(np.float64)
    decay = np.where(rel[None] >= 0, np.exp(log_gamma[:, None, None] * np.maximum(rel, 0.0)[None]), 0.0)
    dec2 = decay.reshape(RET_PAIRS, 2 * chunk, chunk)
    lane_lg = np.repeat(log_gamma, RET_HEAD_DIM)
    qdec = np.exp(lane_lg[None, :] * (idx + 1)[:, None])
    kdec = np.exp(lane_lg[None, :] * (chunk - 1 - idx)[:, None])
    rdec = np.exp(lane_lg * chunk).reshape(RET_PAIRS, LANES, 1) * np.ones((1, 1, LANES))
    blk = np.arange(LANES) // RET_HEAD_DIM
    bmask = (blk[:, None] == blk[None, :]).astype(np.float64)
    f32 = lambda a: jnp.asarray(a.astype(np.float32))
    gmean2 = np.concatenate([bmask, bmask], axis=0) / RET_HEAD_DIM
    return f32(dec2), f32(qdec), f32(kdec), f32(rdec), f32(bmask), f32(gmean2).astype(BF16)


def _retention(rq, rk, rv, rg, gn_gain, batch, seq, chunk):
    nc = seq // chunk
    dec2, qdec, kdec, rdec, bmask, gmean = _retention_tables(chunk)
    tok_spec = pl.BlockSpec((chunk, RET_WIDTH), lambda b, n: (b * nc + n, 0))
    full = lambda a: pl.BlockSpec(a.shape, lambda b, n: (0,) * a.ndim)
    return pl.pallas_call(
        functools.partial(_retention_kernel, chunk=chunk),
        grid=(batch, nc),
        in_specs=[tok_spec] * 4 + [full(dec2), full(qdec), full(kdec), full(rdec), full(bmask),
                                   full(gmean), full(gn_gain)],
        out_specs=tok_spec,
        out_shape=jax.ShapeDtypeStruct(rq.shape, BF16),
        scratch_shapes=[pltpu.VMEM((RET_PAIRS, LANES, LANES), F32)],
        compiler_params=pltpu.CompilerParams(
            dimension_semantics=("parallel", "arbitrary"), vmem_limit_bytes=VMEM_LIMIT),
        name="retention",
    )(rq, rk, rv, rg, dec2, qdec, kdec, rdec, bmask, gmean, gn_gain)


NEG_BIG = -1e30


V_EXT_ROWS = DIFF_V_DIM + 16
QUERY_CHUNK = 256
SCORES_AHEAD_FULL = 3
SCORES_AHEAD_DIAG = 3
STAGE_SLOTS = 16


def _diag_chunks(tq, tk, d):
    assert tk == 2 * QUERY_CHUNK
    per_softmax = tq // QUERY_CHUNK
    out = []
    for c in range(2 * per_softmax):
        q0 = (c % per_softmax) * QUERY_CHUNK
        if q0 + QUERY_CHUNK - 1 < d * tk:
            continue
        kind = "full" if q0 >= (d + 1) * tk else ("tri" if q0 == d * tk else "low_tri")
        out.append((c, kind))
    return out


def _accumulate(acc_ref, cs, alpha, pv):
    acc_ref[:, cs] = alpha * acc_ref[:, cs] + pv


def _diffattn_kernel(q_ref, k_ref, vt_ref, lq1_ref, lk1_ref, lq2_ref, lk2_ref, gain_ref, bias_ref, o_ref,
                     qs_ref, vext_ref, m_ref, acc_ref, stage_ref, *, tq, tk, lambda_init):
    i = pl.program_id(2)
    nk = vext_ref.shape[0]

    @pl.when(i == 0)
    def _():
        for j in range(nk):
            vext_ref[j, 0:DIFF_V_DIM, :] = vt_ref[:, j * tk:(j + 1) * tk]
            vext_ref[j, DIFF_V_DIM:V_EXT_ROWS, :] = jnp.ones((V_EXT_ROWS - DIFF_V_DIM, tk), BF16)

    q = q_ref[...]
    lane = lax.broadcasted_iota(jnp.int32, q.shape, 1)
    zero = jnp.zeros_like(q)
    qs_ref[0:tq, :] = jnp.where(lane < DIFF_QK_DIM, q, zero)
    qs_ref[tq:2 * tq, :] = jnp.where(lane < DIFF_QK_DIM, zero, q)
    m_ref[...] = jnp.full_like(m_ref, NEG_BIG)
    acc_ref[...] = jnp.zeros_like(acc_ref)

    def step(work, n_ahead):
        chunk = lambda c: slice(c * QUERY_CHUNK, (c + 1) * QUERY_CHUNK)

        def scores(j, c, kind):
            n_keys = QUERY_CHUNK if kind == "tri" else tk
            start = pl.multiple_of(j * tk, tk)
            return _dot_nt(k_ref[pl.ds(start, n_keys), :], qs_ref[chunk(c), :])

        ahead = [scores(*work[n]) for n in range(min(n_ahead, len(work)))]
        pending = None
        for n, (j, c, kind) in enumerate(work):
            cs = chunk(c)
            st = ahead.pop(0)
            if n + n_ahead < len(work):
                ahead.append(scores(*work[n + n_ahead]))
            slot = n % stage_ref.shape[0]
            n_keys = st.shape[0]
            stage_ref[slot, 0:n_keys, :] = st
            if kind == "full":
                st = stage_ref[slot]
            else:
                causal = stage_ref[slot, n_keys - QUERY_CHUNK:n_keys, :] + bias_ref[...]
                st = causal if kind == "tri" else jnp.concatenate(
                    [stage_ref[slot, 0:n_keys - QUERY_CHUNK, :], causal], axis=0)
            m_old = m_ref[:, cs]
            m_new = jnp.maximum(m_old, jnp.max(st, axis=0, keepdims=True))
            alpha = jnp.exp2(m_old - m_new)
            p = jnp.exp2(st - m_new).astype(BF16)
            m_ref[:, cs] = m_new
            pv = _dot(vext_ref[j, :, 0:st.shape[0]], p)
            if pending is not None:
                pending()
            pending = functools.partial(_accumulate, acc_ref, cs, alpha, pv)
        pending()

    tiles_per_q = tq // tk
    n_chunks = 2 * tq // QUERY_CHUNK

    def full_tiles(it):
        return [(it * tiles_per_q + d, c, "full") for d in range(tiles_per_q) for c in range(n_chunks)]

    lax.fori_loop(0, i, lambda it, c: (step(full_tiles(it), SCORES_AHEAD_FULL), c)[1], 0)
    kind_order = {"full": 0, "low_tri": 1, "tri": 2}
    diag = lambda it: [(it * tiles_per_q + d, c, kind) for d in range(tiles_per_q)
                       for c, kind in sorted(_diag_chunks(tq, tk, d), key=lambda ck: kind_order[ck[1]])]
    lax.fori_loop(i, i + 1, lambda it, c: (step(diag(it), SCORES_AHEAD_DIAG), c)[1], 0)

    lam = (jnp.exp(jnp.sum(lq1_ref[...] * lk1_ref[...], axis=-1, keepdims=True))
           - jnp.exp(jnp.sum(lq2_ref[...] * lk2_ref[...], axis=-1, keepdims=True)) + lambda_init)
    acc = acc_ref[...]
    o2 = acc[0:DIFF_V_DIM, :] * (1.0 / acc[DIFF_V_DIM:DIFF_V_DIM + 1, :])
    ot = o2[:, :tq] - lam * o2[:, tq:]
    ms = jnp.mean(ot * ot, axis=0, keepdims=True)
    ot = ot * lax.rsqrt(ms + EPS) * gain_ref[...] * (1.0 - lambda_init)
    o_ref[...] = ot.T.astype(BF16)


def _diffattn(dq, dk, dvt, lam_q1, lam_k1, lam_q2, lam_k2, gain, batch, seq, lambda_init, tq, tk):
    nq = seq // tq
    q_spec = pl.BlockSpec((tq, LANES), lambda b, h, i: (b * nq + i, h))
    k_spec = pl.BlockSpec((seq, LANES), lambda b, h, i: (b, h))
    vt_spec = pl.BlockSpec((DIFF_V_DIM, seq), lambda b, h, i: (h, b))
    vec = lambda a: pl.BlockSpec(a.shape, lambda b, h, i: (0, 0))
    key = np.arange(QUERY_CHUNK)[:, None]
    query = np.arange(QUERY_CHUNK)[None, :]
    bias = jnp.asarray(np.where(key <= query, 0.0, NEG_BIG), F32)
    return pl.pallas_call(
        functools.partial(_diffattn_kernel, tq=tq, tk=tk, lambda_init=lambda_init),
        grid=(batch, DIFF_HEADS, nq),
        in_specs=[q_spec, k_spec, vt_spec, vec(lam_q1), vec(lam_k1), vec(lam_q2), vec(lam_k2), vec(gain),
                  vec(bias)],
        out_specs=q_spec,
        out_shape=jax.ShapeDtypeStruct(dq.shape, BF16),
        scratch_shapes=[
            pltpu.VMEM((2 * tq, LANES), BF16),
            pltpu.VMEM((seq // tk, V_EXT_ROWS, tk), BF16),
            pltpu.VMEM((1, 2 * tq), F32),
            pltpu.VMEM((V_EXT_ROWS, 2 * tq), F32),
            pltpu.VMEM((STAGE_SLOTS, tk, QUERY_CHUNK), F32),
        ],
        compiler_params=pltpu.CompilerParams(
            dimension_semantics=("parallel", "parallel", "arbitrary"), vmem_limit_bytes=VMEM_LIMIT),
        name="diffattn",
    )(dq, dk, dvt, lam_q1, lam_k1, lam_q2, lam_k2, gain, bias)


def _route(logits):
    r = [logits[g:g + 1, :] for g in range(N_GROUPS)]
    gmax = jnp.maximum(jnp.maximum(r[0], r[1]), jnp.maximum(r[2], r[3]))
    g_idx = jnp.where(r[0] == gmax, 0, jnp.where(r[1] == gmax, 1, jnp.where(r[2] == gmax, 2, 3)))
    denom = sum(jnp.exp(rg - gmax) for rg in r)
    g_weight = 1.0 / denom
    sel = jnp.zeros((EXPERTS_PER_GROUP, logits.shape[1]), F32)
    for g in range(N_GROUPS):
        rows = logits[8 + g * EXPERTS_PER_GROUP:8 + (g + 1) * EXPERTS_PER_GROUP, :]
        sel = jnp.where(g_idx == g, rows, sel)
    eidx = lax.broadcasted_iota(jnp.int32, sel.shape, 0)
    v1 = jnp.max(sel, axis=0, keepdims=True)
    i1 = jnp.min(jnp.where(sel == v1, eidx, EXPERTS_PER_GROUP), axis=0, keepdims=True)
    sel2 = jnp.where(eidx == i1, -jnp.inf, sel)
    v2 = jnp.max(sel2, axis=0, keepdims=True)
    i2 = jnp.min(jnp.where(sel2 == v2, eidx, EXPERTS_PER_GROUP), axis=0, keepdims=True)
    e2 = jnp.exp(v2 - v1)
    w1 = g_weight / (1.0 + e2)
    w2 = g_weight * e2 / (1.0 + e2)
    return g_idx, i1, i2, w1, w2


OUTPROJ_PARTS = 2


def _outproj_kernel(ret_ref, diff_ref, x_ref, mod_ref, gain_ref, wo_ref, wr_ref, br_ref, tri_ref,
                    x1_ref, h2_ref, ri_ref, rw_ref, cnt_ref, *, tiles_per_batch):
    b = pl.program_id(0) // tiles_per_batch
    gate1 = mod_ref[pl.ds(b, 1), 2 * D_MODEL:3 * D_MODEL]
    shift = mod_ref[pl.ds(b, 1), 3 * D_MODEL:4 * D_MODEL]
    scale = mod_ref[pl.ds(b, 1), 4 * D_MODEL:5 * D_MODEL]
    wr = wr_ref[...]
    tm = x_ref.shape[0]
    parts = [slice(n * tm // OUTPROJ_PARTS, (n + 1) * tm // OUTPROJ_PARTS) for n in range(OUTPROJ_PARTS)]
    mix = [_dot(jnp.concatenate([ret_ref[r, :], diff_ref[r, :]], axis=1), wo_ref[...]) for r in parts]
    for r, m in zip(parts, mix):
        x1_ref[r, :] = x_ref[r, :] + gate1 * m
    h_split = [_split_bf16(_norm_modulate(x1_ref[r, :], gain_ref[...], shift, scale)) for r in parts]
    for r, (h_hi, _) in zip(parts, h_split):
        h2_ref[r, :] = h_hi
    by_hi = [_dot_nt(wr, h_hi) for h_hi, _ in h_split]
    by_lo = [_dot_nt(wr[:ROUTER_ROWS], h_lo) for _, h_lo in h_split]
    logits = [a[:ROUTER_ROWS] + a[ROUTER_ROWS:] + c + br_ref[...] for a, c in zip(by_hi, by_lo)]
    routed = [_route(lg) for lg in logits]
    g_idx, i1, i2, w1, w2 = [jnp.concatenate([rt[n] for rt in routed], axis=1) for n in range(5)]
    logits = jnp.concatenate(logits, axis=1)
    e1 = g_idx * EXPERTS_PER_GROUP + i1
    e2 = g_idx * EXPERTS_PER_GROUP + i2
    eidx = lax.broadcasted_iota(jnp.int32, (N_EXPERTS, logits.shape[1]), 0)
    hit1 = eidx == e1
    hit2 = eidx == e2
    onehot = jnp.where(hit1 | hit2, 1.0, 0.0)
    before = _dot(onehot.astype(BF16), tri_ref[...])
    r1 = jnp.sum(jnp.where(hit1, before, 0.0), axis=0, keepdims=True)
    r2 = jnp.sum(jnp.where(hit2, before, 0.0), axis=0, keepdims=True)
    zi = jnp.zeros_like(e1)
    ri_ref[...] = jnp.concatenate([e1, e2, r1.astype(jnp.int32), r2.astype(jnp.int32), zi, zi, zi, zi], axis=0)
    zf = jnp.zeros_like(w1)
    rw_ref[...] = jnp.concatenate([w1, w2, zf, zf, zf, zf, zf, zf], axis=0)
    counts = jnp.sum(onehot, axis=1, keepdims=True)
    cnt_ref[0] = jnp.broadcast_to(counts, (N_EXPERTS, LANES)).astype(jnp.int32)


def _outproj(ret_out, diff_out, x2, mod, gain, w_out, wr, br, seq, tm):
    tokens = x2.shape[0]
    tiles_per_batch = seq // tm
    n_tiles = tokens // tm
    tri = jnp.asarray(np.arange(tm)[:, None] < np.arange(tm)[None, :], BF16)
    tok_spec = lambda w: pl.BlockSpec((tm, w), lambda i: (i, 0))
    row_spec = pl.BlockSpec((8, tm), lambda i: (0, i))
    full = lambda a: pl.BlockSpec(a.shape, lambda i: (0,) * a.ndim)
    return pl.pallas_call(
        functools.partial(_outproj_kernel, tiles_per_batch=tiles_per_batch),
        grid=(n_tiles,),
        in_specs=[tok_spec(RET_WIDTH), tok_spec(DIFF_WIDTH), tok_spec(D_MODEL), full(mod), full(gain),
                  full(w_out), full(wr), full(br), full(tri)],
        out_specs=[tok_spec(D_MODEL), tok_spec(D_MODEL), row_spec, row_spec,
                   pl.BlockSpec((1, N_EXPERTS, LANES), lambda i: (i, 0, 0))],
        out_shape=[jax.ShapeDtypeStruct((tokens, D_MODEL), F32),
                   jax.ShapeDtypeStruct((tokens, D_MODEL), BF16),
                   jax.ShapeDtypeStruct((8, tokens), jnp.int32),
                   jax.ShapeDtypeStruct((8, tokens), F32),
                   jax.ShapeDtypeStruct((n_tiles, N_EXPERTS, LANES), jnp.int32)],
        compiler_params=pltpu.CompilerParams(
            dimension_semantics=("parallel",), vmem_limit_bytes=VMEM_LIMIT),
        name="outproj",
    )(ret_out, diff_out, x2, mod, gain, w_out, wr, br, tri)


CHUNK = 8
BIG_PIECE = 4 * CHUNK
MAX_SMALL_PIECES = N_EXPERTS * (BIG_PIECE // CHUNK - 1)
PIECE_TABLES = ("n_big", "big_src", "big_dst", "n_small", "small_src", "small_dst")
TMX = 512


def _local_rows(tm):
    rows = 2 * tm + N_EXPERTS * (CHUNK - 1)
    return (rows + 15) // 16 * 16


def _max_big_pieces(tm):
    return _local_rows(tm) // BIG_PIECE


def _sorted_rows_alloc(tokens, tm):
    worst = 2 * tokens + (tokens // tm) * N_EXPERTS * (CHUNK - 1) + N_EXPERTS * (TMX - CHUNK)
    return (worst + TMX - 1) // TMX * TMX


def _dispatch_plan(cnt, tokens, tm):
    i32 = jnp.int32
    pad = (cnt + CHUNK - 1) // CHUNK * CHUNK
    local_end = jnp.cumsum(pad, axis=1)
    local_start = local_end - pad
    seg_rows = jnp.sum(pad, axis=0)
    seg_pad = (seg_rows + TMX - 1) // TMX * TMX
    seg_end = jnp.cumsum(seg_pad)
    seg_start = seg_end - seg_pad
    run_dst = seg_start[None, :] + jnp.cumsum(pad, axis=0) - pad

    def pieces(count, offset, size, max_n):
        end = jnp.cumsum(count, axis=1)
        start = end - count
        k = jnp.arange(max_n, dtype=i32)[None, :, None]
        owns = (start[:, None, :] <= k) & (k < end[:, None, :])
        within = size * (k - start[:, None, :]) + offset[:, None, :]
        src = jnp.sum(jnp.where(owns, local_start[:, None, :] + within, 0), axis=-1)
        dst = jnp.sum(jnp.where(owns, run_dst[:, None, :] + within, 0), axis=-1)
        return end[:, -1].astype(i32), src.reshape(-1).astype(i32), dst.reshape(-1).astype(i32)

    n_big, big_src, big_dst = pieces(pad // BIG_PIECE, jnp.zeros_like(pad), BIG_PIECE, _max_big_pieces(tm))
    n_small, small_src, small_dst = pieces(pad % BIG_PIECE // CHUNK, pad // BIG_PIECE * BIG_PIECE, CHUNK,
                                           MAX_SMALL_PIECES)
    m = TMX * jnp.arange(_sorted_rows_alloc(tokens, tm) // TMX, dtype=i32)
    tile_expert = jnp.minimum(jnp.sum(seg_end[None, :] <= m[:, None], axis=-1), N_EXPERTS - 1)
    towns = (seg_start[None, :] <= m[:, None]) & (m[:, None] < seg_end[None, :])
    used = seg_pad > 0
    parity = (jnp.cumsum(used) - used) % 2
    eids = jnp.arange(N_EXPERTS, dtype=i32)
    later_used = (eids[None, :] > eids[:, None]) & used[None, :]
    next_used = jnp.min(jnp.where(later_used, eids[None, :], N_EXPERTS), axis=1)
    next_used = jnp.where(next_used == N_EXPERTS, -1, next_used)
    pick = lambda per_expert: jnp.sum(jnp.where(towns, per_expert[None, :], 0), axis=-1)
    tile_first = jnp.sum(jnp.where(towns & (seg_start[None, :] == m[:, None]), 1, 0), axis=-1)
    tile_next = jnp.where(jnp.any(towns, axis=-1), pick(next_used), -1)
    return dict(
        tile_first=tile_first.astype(i32),
        tile_slot=pick(parity).astype(i32),
        tile_next=tile_next.astype(i32),
        tile_rows=jnp.clip(pick(seg_start + seg_rows) - m, 0, TMX).astype(i32),
        local_start=local_start.reshape(-1).astype(i32),
        n_big=n_big, big_src=big_src, big_dst=big_dst,
        n_small=n_small, small_src=small_src, small_dst=small_dst,
        tail_base=(seg_start + seg_rows).astype(i32),
        tail_rows=(seg_pad - seg_rows).astype(i32),
        tile_expert=tile_expert.astype(i32),
        n_used=(seg_end[-1:] // TMX).astype(i32),
    )


WAIT_UNROLL = 8


def _wait_times(copy, n):
    lax.fori_loop(0, n // WAIT_UNROLL, lambda i, c: ([copy.wait() for _ in range(WAIT_UNROLL)], c)[1], 0)
    lax.fori_loop(0, n % WAIT_UNROLL, lambda i, c: (copy.wait(), c)[1], 0)


def _for_each(n, body, unroll=4):
    main = n // unroll
    lax.fori_loop(0, main, lambda i, c: ([body(i * unroll + u) for u in range(unroll)], c)[1], 0)
    lax.fori_loop(main * unroll, n, lambda j, c: (body(j), c)[1], 0)


def _local_slots(ri_ref, local_start_ref, tile):
    e1, e2 = ri_ref[0:1, :], ri_ref[1:2, :]
    s1, s2 = ri_ref[2:3, :], ri_ref[3:4, :]
    for e in range(N_EXPERTS):
        start = local_start_ref[tile * N_EXPERTS + e]
        s1 = s1 + jnp.where(e1 == e, start, 0)
        s2 = s2 + jnp.where(e2 == e, start, 0)
    return s1, s2


def _run_pieces(piece_refs, local_ref, sorted_ref, sem_ref, to_sorted):
    n_big_ref, big_src_ref, big_dst_ref, n_small_ref, small_src_ref, small_dst_ref = piece_refs
    r_loc = local_ref.shape[1]
    kinds = [(BIG_PIECE, n_big_ref, big_src_ref, big_dst_ref, r_loc // BIG_PIECE),
             (CHUNK, n_small_ref, small_src_ref, small_dst_ref, MAX_SMALL_PIECES)]

    def copy(sl, size, local_row, sorted_row):
        local = local_ref.at[sl, pl.ds(pl.multiple_of(local_row, CHUNK), size), :]
        srt = sorted_ref.at[pl.ds(pl.multiple_of(sorted_row, CHUNK), size), :]
        return pltpu.make_async_copy(local, srt, sem_ref.at[sl]) if to_sorted else \
            pltpu.make_async_copy(srt, local, sem_ref.at[sl])

    def start(tile, sl):
        for size, n_ref, src_ref, dst_ref, max_n in kinds:
            _for_each(n_ref[tile], lambda k: copy(sl, size, src_ref[tile * max_n + k], dst_ref[tile * max_n + k])
                      .start())

    def wait(tile, sl):
        for size, n_ref, _, _, _ in kinds:
            _wait_times(copy(sl, size, 0, 0), n_ref[tile])

    return start, wait


def _dispatch_kernel(local_start_ref, n_big_ref, big_src_ref, big_dst_ref, n_small_ref, small_src_ref, small_dst_ref,
                     tail_base_ref, tail_rows_ref, n_used_ref,
                     h_ref, ri_ref, xs_ref, buf_ref, zero_ref, sem_ref, tail_sem_ref, *, r_loc):
    b = pl.program_id(0)
    nb = pl.num_programs(0)
    slot = b % 2
    start_runs, drain = _run_pieces(
        (n_big_ref, big_src_ref, big_dst_ref, n_small_ref, small_src_ref, small_dst_ref), buf_ref, xs_ref, sem_ref,
        to_sorted=True)

    @pl.when(b >= 2)
    def _():
        drain(b - 2, slot)

    s1, s2 = _local_slots(ri_ref, local_start_ref, b)
    rows = lax.broadcasted_iota(jnp.int32, (r_loc, s1.shape[1]), 0)
    perm = jnp.where((rows == s1) | (rows == s2), 1.0, 0.0).astype(BF16)
    buf_ref[slot] = _dot(perm, h_ref[...])
    start_runs(b, slot)

    def tail_pieces(e, act):
        n = tail_rows_ref[e]
        size = TMX // 2
        while size >= CHUNK:
            dst = pl.multiple_of(tail_base_ref[e] + (n & (-2 * size)), CHUNK)
            cp = pltpu.make_async_copy(zero_ref.at[pl.ds(0, size), :], xs_ref.at[pl.ds(dst, size), :],
                                       tail_sem_ref.at[0])
            pl.when((n & size) != 0)(functools.partial(act, cp))
            size //= 2

    def unused_tile_copy(m):
        dst = pl.multiple_of(m * TMX, TMX)
        return pltpu.make_async_copy(zero_ref, xs_ref.at[pl.ds(dst, TMX), :], tail_sem_ref.at[1])

    n_alloc = xs_ref.shape[0] // TMX

    @pl.when(b == 0)
    def _():
        zero_ref[...] = jnp.zeros_like(zero_ref)
        lax.fori_loop(0, N_EXPERTS, lambda e, c: (tail_pieces(e, lambda cp: cp.start()), c)[1], 0)
        lax.fori_loop(n_used_ref[0], n_alloc, lambda m, c: (unused_tile_copy(m).start(), c)[1], 0)

    @pl.when(b == nb - 1)
    def _():
        lax.fori_loop(0, N_EXPERTS, lambda e, c: (tail_pieces(e, lambda cp: cp.wait()), c)[1], 0)
        lax.fori_loop(n_used_ref[0], n_alloc, lambda m, c: (unused_tile_copy(m).wait(), c)[1], 0)

        @pl.when(b >= 1)
        def _():
            drain(b - 1, 1 - slot)

        drain(b, slot)


def _dispatch(h2, ri, plan, tm):
    tokens = h2.shape[0]
    r_loc = _local_rows(tm)
    prefetch = [plan["local_start"]] + [plan[k] for k in PIECE_TABLES] + [
        plan["tail_base"], plan["tail_rows"], plan["n_used"]]
    grid_spec = pltpu.PrefetchScalarGridSpec(
        num_scalar_prefetch=len(prefetch),
        grid=(tokens // tm,),
        in_specs=[pl.BlockSpec((tm, D_MODEL), lambda i, *_: (i, 0)),
                  pl.BlockSpec((8, tm), lambda i, *_: (0, i))],
        out_specs=pl.BlockSpec(memory_space=pl.ANY),
        scratch_shapes=[pltpu.VMEM((2, r_loc, D_MODEL), F32), pltpu.VMEM((TMX, D_MODEL), F32),
                        pltpu.SemaphoreType.DMA((2,)), pltpu.SemaphoreType.DMA((2,))],
    )
    return pl.pallas_call(
        functools.partial(_dispatch_kernel, r_loc=r_loc),
        grid_spec=grid_spec,
        out_shape=jax.ShapeDtypeStruct((_sorted_rows_alloc(tokens, tm), D_MODEL), F32),
        compiler_params=pltpu.CompilerParams(
            dimension_semantics=("arbitrary",), vmem_limit_bytes=VMEM_LIMIT),
        name="dispatch",
    )(*prefetch, h2, ri)


def _experts_kernel(tile_expert_ref, n_used_ref, first_ref, slot_ref, next_ref, rows_ref, xs_ref, wg_hbm, wu_hbm, wd_hbm,
                    ys_ref, wg_st, wu_st, wd_st, wg_bf, wu_bf, wd_bf, a_ref, u_ref, sem_ref):
    m = pl.program_id(0)

    def weight_copies(e, s):
        return [pltpu.make_async_copy(src.at[e], dst.at[s], sem_ref.at[s, n])
                for n, (src, dst) in enumerate([(wg_hbm, wg_st), (wu_hbm, wu_st), (wd_hbm, wd_st)])]

    @pl.when(m < n_used_ref[0])
    def _():
        @pl.when(first_ref[m] == 1)
        def _():
            s = slot_ref[m]

            @pl.when(m == 0)
            def _():
                for cp in weight_copies(tile_expert_ref[0], 0):
                    cp.start()

            for cp in weight_copies(tile_expert_ref[m], s):
                cp.wait()

            @pl.when(next_ref[m] >= 0)
            def _():
                for cp in weight_copies(next_ref[m], 1 - s):
                    cp.start()

            wg_bf[...] = wg_st[s].astype(BF16)
            wu_bf[...] = wu_st[s].astype(BF16)
            wd_bf[...] = wd_st[s].astype(BF16)

        def mlp(rows):
            x = xs_ref[rows, :].astype(BF16)
            a_ref[rows, :] = _dot(x, wg_bf[...])
            u_ref[rows, :] = _dot(x, wu_bf[...])
            hid = (_silu(a_ref[rows, :]) * u_ref[rows, :]).astype(BF16)
            ys_ref[rows, :] = _dot(hid, wd_bf[...])

        half = TMX // 2

        @pl.when(rows_ref[m] > half)
        def _():
            mlp(slice(0, TMX))

        @pl.when(rows_ref[m] <= half)
        def _():
            mlp(slice(0, half))
            ys_ref[half:, :] = jnp.zeros((TMX - half, D_MODEL), F32)


def _experts(xs, plan, wg, wu, wd):
    n_tiles = xs.shape[0] // TMX
    last_used = lambda m, n_used: jnp.minimum(m, n_used[0] - 1)
    row_spec = pl.BlockSpec((TMX, D_MODEL), lambda m, te, nu, *_: (last_used(m, nu), 0))
    hbm = pl.BlockSpec(memory_space=pl.ANY)
    up_shape, down_shape = (D_MODEL, D_EXPERT), (D_EXPERT, D_MODEL)
    grid_spec = pltpu.PrefetchScalarGridSpec(
        num_scalar_prefetch=6,
        grid=(n_tiles,),
        in_specs=[row_spec, hbm, hbm, hbm],
        out_specs=row_spec,
        scratch_shapes=[pltpu.VMEM((2,) + up_shape, F32), pltpu.VMEM((2,) + up_shape, F32),
                        pltpu.VMEM((2,) + down_shape, F32),
                        pltpu.VMEM(up_shape, BF16), pltpu.VMEM(up_shape, BF16), pltpu.VMEM(down_shape, BF16),
                        pltpu.VMEM((TMX, D_EXPERT), F32), pltpu.VMEM((TMX, D_EXPERT), F32),
                        pltpu.SemaphoreType.DMA((2, 3))],
    )
    return pl.pallas_call(
        _experts_kernel,
        grid_spec=grid_spec,
        out_shape=jax.ShapeDtypeStruct(xs.shape, F32),
        input_output_aliases={6: 0},
        compiler_params=pltpu.CompilerParams(
            dimension_semantics=("arbitrary",), vmem_limit_bytes=VMEM_LIMIT),
        name="experts",
    )(plan["tile_expert"], plan["n_used"], plan["tile_first"], plan["tile_slot"], plan["tile_next"], plan["tile_rows"],
      xs, wg, wu, wd)


def _combine_kernel(local_start_ref, n_big_ref, big_src_ref, big_dst_ref, n_small_ref, small_src_ref, small_dst_ref,
                    ys_ref, ri_ref, rw_ref, x1_ref, mod_ref, gain_ref, o_ref, buf_ref, sem_ref,
                    *, r_loc, tiles_per_batch):
    b = pl.program_id(0)
    nb = pl.num_programs(0)
    slot = b % 2
    fetch, wait_runs = _run_pieces(
        (n_big_ref, big_src_ref, big_dst_ref, n_small_ref, small_src_ref, small_dst_ref), buf_ref, ys_ref, sem_ref,
        to_sorted=False)

    @pl.when(b == 0)
    def _():
        buf_ref[...] = jnp.zeros_like(buf_ref)
        fetch(0, 0)

    @pl.when(b + 1 < nb)
    def _():
        fetch(b + 1, 1 - slot)

    wait_runs(b, slot)

    s1, s2 = _local_slots(ri_ref, local_start_ref, b)
    rows = lax.broadcasted_iota(jnp.int32, (r_loc, s1.shape[1]), 0)
    hit1 = rows == s1
    hit2 = rows == s2
    w_row = jnp.sum(jnp.where(hit1, rw_ref[0:1, :], jnp.where(hit2, rw_ref[1:2, :], 0.0)), axis=1, keepdims=True)
    perm = jnp.where(hit1 | hit2, 1.0, 0.0).astype(BF16)
    yw = (buf_ref[slot] * w_row).astype(BF16)
    moe = _dot_tn(perm, yw)
    batch = b // tiles_per_batch
    gate2 = mod_ref[pl.ds(batch, 1), 5 * D_MODEL:6 * D_MODEL]
    x2 = x1_ref[...] + gate2 * moe
    ms = jnp.mean(x2 * x2, axis=-1, keepdims=True)
    o_ref[...] = x2 * lax.rsqrt(ms + EPS) * gain_ref[...]


def _combine(ys, ri, rw, x1, mod, gain, plan, seq, tm):
    tokens = x1.shape[0]
    r_loc = _local_rows(tm)
    row_spec = pl.BlockSpec((8, tm), lambda i, *_: (0, i))
    tok_spec = pl.BlockSpec((tm, D_MODEL), lambda i, *_: (i, 0))
    full = lambda a: pl.BlockSpec(a.shape, lambda i, *_: (0,) * a.ndim)
    prefetch = [plan["local_start"]] + [plan[k] for k in PIECE_TABLES]
    grid_spec = pltpu.PrefetchScalarGridSpec(
        num_scalar_prefetch=len(prefetch),
        grid=(tokens // tm,),
        in_specs=[pl.BlockSpec(memory_space=pl.ANY), row_spec, row_spec, tok_spec, full(mod), full(gain)],
        out_specs=tok_spec,
        scratch_shapes=[pltpu.VMEM((2, r_loc, D_MODEL), F32), pltpu.SemaphoreType.DMA((2,))],
    )
    return pl.pallas_call(
        functools.partial(_combine_kernel, r_loc=r_loc, tiles_per_batch=seq // tm),
        grid_spec=grid_spec,
        out_shape=jax.ShapeDtypeStruct((tokens, D_MODEL), F32),
        compiler_params=pltpu.CompilerParams(
            dimension_semantics=("arbitrary",), vmem_limit_bytes=VMEM_LIMIT),
        name="combine",
    )(*prefetch, ys, ri, rw, x1, mod, gain)


def _rotary_tables(seq):
    half = RET_HEAD_DIM // 2
    inv_freq = 1.0 / (ROPE_BASE ** (np.arange(half, dtype=np.float64) / half))
    ang = np.arange(seq, dtype=np.float64)[:, None] * inv_freq[None, :]
    cos = np.cos(ang)
    sin = np.sin(ang)
    f32 = lambda a: jnp.asarray(a.astype(np.float32))
    return f32(np.tile(cos, (1, 4))), f32(np.concatenate([-sin, sin, -sin, sin], axis=1))


def _pick_tile(n, pref):
    t = min(n, pref)
    assert n % t == 0, (n, t)
    return t


def kernel(x, c, ada_w, ada_b, norm1_gain, norm2_gain, w_in, w_out, ret_gn_gain, lam_q1, lam_k1, lam_q2,
           lam_k2, diff_subln_gain, w_group, b_group, w_expert, b_expert, w_gate, w_up, w_down, final_gain):
    batch, seq, d = x.shape
    assert d == D_MODEL and batch <= 8 and ada_w.shape[0] == 1
    layer = 0
    lambda_init = 0.8 - 0.6 * math.exp(-0.3 * layer)
    tokens = batch * seq
    x2 = x.reshape(tokens, d)
    tm = _pick_tile(seq, 512)

    c_pad = jnp.zeros((8, d), F32).at[:batch].set(c)
    mod = _adaln(c_pad, ada_w[layer], ada_b[layer].reshape(1, -1))

    cos_t, sin_t = _rotary_tables(seq)
    rq, rk, rv, rg, dq, dk, dvt = _inproj(
        x2, mod, norm1_gain[layer].reshape(1, d), w_in[layer].astype(BF16), cos_t, sin_t, seq, tm)

    ret_out = _retention(rq, rk, rv, rg, ret_gn_gain[layer].reshape(1, RET_WIDTH), batch, seq,
                         _pick_tile(seq, 256))
    diff_out = _diffattn(
        dq, dk, dvt, lam_q1[layer].reshape(1, -1), lam_k1[layer].reshape(1, -1), lam_q2[layer].reshape(1, -1),
        lam_k2[layer].reshape(1, -1), diff_subln_gain[layer].reshape(-1, 1), batch, seq, lambda_init,
        _pick_tile(seq, 2048), 2 * QUERY_CHUNK)

    w_router = jnp.concatenate(
        [w_group[layer].T, jnp.zeros((8 - N_GROUPS, d), F32), w_expert[layer].reshape(d, N_EXPERTS).T], axis=0)
    b_router = jnp.concatenate(
        [b_group[layer], jnp.zeros((8 - N_GROUPS,), F32), b_expert[layer].reshape(N_EXPERTS)]).reshape(-1, 1)
    wr_hi = w_router.astype(BF16)
    wr_lo = (w_router - wr_hi.astype(F32)).astype(BF16)
    x1, h2, ri, rw, cnt = _outproj(ret_out, diff_out, x2, mod, norm2_gain[layer].reshape(1, d),
                                   w_out[layer].astype(BF16), jnp.concatenate([wr_hi, wr_lo], axis=0), b_router,
                                   seq, tm)

    plan = _dispatch_plan(cnt[:, :, 0], tokens, tm)
    xs = _dispatch(h2, ri, plan, tm)
    ys = _experts(xs, plan, w_gate[layer].reshape(N_EXPERTS, d, D_EXPERT),
                  w_up[layer].reshape(N_EXPERTS, d, D_EXPERT), w_down[layer].reshape(N_EXPERTS, D_EXPERT, d))
    out = _combine(ys, ri, rw, x1, mod, final_gain.reshape(1, d), plan, seq, tm)
    return out.reshape(batch, seq, d)
```

```python
import functools
import math

import jax
import jax.numpy as jnp
import numpy as np
from jax import lax
from jax.experimental import pallas as pl
from jax.experimental.pallas import tpu as pltpu

F32 = jnp.float32
BF16 = jnp.bfloat16

D_MODEL = 1024
RET_HEAD_DIM = 64
RET_WIDTH = 512
RET_HEADS = 8
RET_PAIRS = RET_HEADS // 2
DIFF_QK_DIM = 64
DIFF_V_DIM = 128
DIFF_HEADS = 4
DIFF_WIDTH = 512
N_GROUPS = 4
EXPERTS_PER_GROUP = 8
N_EXPERTS = N_GROUPS * EXPERTS_PER_GROUP
D_EXPERT = 512
N_MOD = 6
ROPE_BASE = 10000.0
EPS = 1e-6
LANES = 128
ROUTER_ROWS = 8 + N_EXPERTS
VMEM_LIMIT = 56 * 1024 * 1024


def _dot(a, b):
    return jnp.dot(a, b, preferred_element_type=F32)


def _dot_nt(a, b):
    return lax.dot_general(a, b, (((1,), (1,)), ((), ())), preferred_element_type=F32)


def _dot_tn(a, b):
    return lax.dot_general(a, b, (((0,), (0,)), ((), ())), preferred_element_type=F32)


def _split_bf16(x):
    hi = x.astype(BF16)
    lo = (x - hi.astype(F32)).astype(BF16)
    return hi, lo


def _silu(x):
    return x / (1.0 + jnp.exp(-x))


def _adaln_kernel(c_ref, w_ref, b_ref, o_ref):
    ca = _silu(c_ref[...])
    c_hi, c_lo = _split_bf16(ca)
    w_hi, w_lo = _split_bf16(w_ref[...])
    o_ref[...] = _dot(c_hi, w_hi) + _dot(c_lo, w_hi) + _dot(c_hi, w_lo) + b_ref[...]


def _adaln(c_pad, ada_w, ada_b):
    n_out = ada_w.shape[1]
    tn = D_MODEL
    return pl.pallas_call(
        _adaln_kernel,
        grid=(n_out // tn,),
        in_specs=[
            pl.BlockSpec((8, D_MODEL), lambda j: (0, 0)),
            pl.BlockSpec((D_MODEL, tn), lambda j: (0, j)),
            pl.BlockSpec((1, tn), lambda j: (0, j)),
        ],
        out_specs=pl.BlockSpec((8, tn), lambda j: (0, j)),
        out_shape=jax.ShapeDtypeStruct((8, n_out), F32),
        compiler_params=pltpu.CompilerParams(vmem_limit_bytes=VMEM_LIMIT),
        name="adaln",
    )(c_pad, ada_w, ada_b)


def _norm_modulate(x, gain, shift, scale):
    ms = jnp.mean(x * x, axis=-1, keepdims=True)
    y = x * lax.rsqrt(ms + EPS) * gain
    return y * (1.0 + scale) + shift


def _rotary_slab(x, cos, sin_signed, lane_lo):
    swapped = jnp.where(lane_lo, pltpu.roll(x, 96, 1), pltpu.roll(x, 32, 1))
    return x * cos + swapped * sin_signed


def _inproj_kernel(x_ref, mod_ref, gain_ref, w_ref, cos_ref, sin_ref,
                   rq_ref, rk_ref, rv_ref, rg_ref, dq_ref, dk_ref, dvt_ref, *, tiles_per_batch):
    b = pl.program_id(0) // tiles_per_batch
    shift = mod_ref[pl.ds(b, 1), 0:D_MODEL]
    scale = mod_ref[pl.ds(b, 1), D_MODEL:2 * D_MODEL]
    h = _norm_modulate(x_ref[...], gain_ref[...], shift, scale).astype(BF16)
    cos = cos_ref[...]
    sin = sin_ref[...]
    lane = lax.broadcasted_iota(jnp.int32, cos.shape, 1)
    lane_lo = (lane % 64) < 32

    def proj(chunk):
        return _dot(h, w_ref[:, chunk * RET_WIDTH:(chunk + 1) * RET_WIDTH])

    def rotary(acc, out_ref, post_scale):
        for s in range(RET_WIDTH // LANES):
            sl = slice(s * LANES, (s + 1) * LANES)
            out_ref[:, sl] = (_rotary_slab(acc[:, sl], cos, sin, lane_lo) * post_scale).astype(BF16)

    rotary(proj(0), rq_ref, 1.0)
    rotary(proj(1), rk_ref, RET_HEAD_DIM ** -0.5)
    rv_ref[...] = proj(2).astype(BF16)
    rg_ref[...] = _silu(proj(3)).astype(BF16)
    dq_ref[...] = (proj(4) * (DIFF_QK_DIM ** -0.5 * math.log2(math.e))).astype(BF16)
    dk_ref[...] = proj(5).astype(BF16)
    dvt_ref[...] = proj(6).T.astype(BF16)


def _inproj(x2, mod, gain, w_in, cos_t, sin_t, seq, tm):
    tokens = x2.shape[0]
    tiles_per_batch = seq // tm
    tok_spec = lambda w: pl.BlockSpec((tm, w), lambda i: (i, 0))
    tab_spec = pl.BlockSpec((tm, LANES), lambda i: (i % tiles_per_batch, 0))
    full = lambda a: pl.BlockSpec(a.shape, lambda i: (0,) * a.ndim)
    out = jax.ShapeDtypeStruct((tokens, RET_WIDTH), BF16)
    return pl.pallas_call(
        functools.partial(_inproj_kernel, tiles_per_batch=tiles_per_batch),
        grid=(tokens // tm,),
        in_specs=[tok_spec(D_MODEL), full(mod), full(gain), full(w_in), tab_spec, tab_spec],
        out_specs=[tok_spec(RET_WIDTH)] * 6 + [pl.BlockSpec((DIFF_WIDTH, tm), lambda i: (0, i))],
        out_shape=[out] * 6 + [jax.ShapeDtypeStruct((DIFF_WIDTH, tokens), BF16)],
        compiler_params=pltpu.CompilerParams(
            dimension_semantics=("parallel",), vmem_limit_bytes=VMEM_LIMIT),
        name="inproj",
    )(x2, mod, gain, w_in, cos_t, sin_t)


RET_SEQ_GROUP = 4


def _retention_kernel(q_ref, k_ref, v_ref, g_ref, dec_ref, qdec_ref, kdec_ref, rdec_ref,
                      bmask_ref, gmean_ref, gain_ref, o_ref, state_ref, *, chunk):
    @pl.when(pl.program_id(1) == 0)
    def _():
        state_ref[...] = jnp.zeros_like(state_ref)

    lane = lax.broadcasted_iota(jnp.int32, (chunk, LANES), 1)
    first_head = lane < RET_HEAD_DIM
    gmean = gmean_ref[...]
    bmask = bmask_ref[...]
    units = [(s, p) for s in range(q_ref.shape[0]) for p in range(RET_PAIRS)]
    sl = lambda p: slice(p * LANES, (p + 1) * LANES)
    q = [q_ref[s, :, sl(p)] for s, p in units]
    k = [k_ref[s, :, sl(p)] for s, p in units]
    v = [v_ref[s, :, sl(p)] for s, p in units]
    zero = jnp.zeros_like(q[0])
    q_stack = [jnp.concatenate([jnp.where(first_head, qu, zero), jnp.where(first_head, zero, qu)], axis=0)
               for qu in q]
    scores = [(_dot_nt(q_stack[n], k[n]) * dec_ref[p]).astype(BF16) for n, (_, p) in enumerate(units)]
    state = [state_ref[n] for n in range(len(units))]
    cross = [_dot(q[n], state[n].astype(BF16)) * qdec_ref[:, sl(p)] for n, (_, p) in enumerate(units)]
    k_dec = [(k[n].astype(F32) * kdec_ref[:, sl(p)]).astype(BF16) for n, (_, p) in enumerate(units)]
    for n, (_, p) in enumerate(units):
        state_ref[n] = state[n] * rdec_ref[p] + _dot_tn(k_dec[n], v[n]) * bmask
    intra2 = [_dot(scores[n], v[n]) for n in range(len(units))]
    y = [jnp.where(first_head, intra2[n][:chunk], intra2[n][chunk:]) + cross[n] for n in range(len(units))]
    seg_mean = lambda x: _dot(jnp.concatenate(_split_bf16(x), axis=1), gmean)
    mu = [seg_mean(yu) for yu in y]
    d = [yu - mu_u for yu, mu_u in zip(y, mu)]
    var = [seg_mean(du * du) for du in d]
    for n, (s, p) in enumerate(units):
        yn = d[n] * lax.rsqrt(var[n] + EPS) * gain_ref[:, sl(p)]
        o_ref[s, :, sl(p)] = (g_ref[s, :, sl(p)].astype(F32) * yn).astype(BF16)


def _retention_tables(chunk):
    heads = np.arange(RET_HEADS, dtype=np.float64)
    log_gamma = np.log(1.0 - np.exp2(-5.0 - heads))
    idx = np.arange(chunk)
    rel = (idx[:, None] - idx[None, :]).astype(np.float64)
    decay = np.where(rel[None] >= 0, np.exp(log_gamma[:, None, None] * np.maximum(rel, 0.0)[None]), 0.0)
    dec2 = decay.reshape(RET_PAIRS, 2 * chunk, chunk)
    lane_lg = np.repeat(log_gamma, RET_HEAD_DIM)
    qdec = np.exp(lane_lg[None, :] * (idx + 1)[:, None])
    kdec = np.exp(lane_lg[None, :] * (chunk - 1 - idx)[:, None])
    rdec = np.exp(lane_lg * chunk).reshape(RET_PAIRS, LANES, 1) * np.ones((1, 1, LANES))
    blk = np.arange(LANES) // RET_HEAD_DIM
    bmask = (blk[:, None] == blk[None, :]).astype(np.float64)
    f32 = lambda a: jnp.asarray(a.astype(np.float32))
    gmean2 = np.concatenate([bmask, bmask], axis=0) / RET_HEAD_DIM
    return f32(dec2), f32(qdec), f32(kdec), f32(rdec), f32(bmask), f32(gmean2).astype(BF16)


def _retention(rq, rk, rv, rg, gn_gain, batch, seq, chunk):
    nc = seq // chunk
    group = RET_SEQ_GROUP if batch % RET_SEQ_GROUP == 0 else 1
    dec2, qdec, kdec, rdec, bmask, gmean = _retention_tables(chunk)
    tok_spec = pl.BlockSpec((group, chunk, RET_WIDTH), lambda b, n: (b, n, 0))
    full = lambda a: pl.BlockSpec(a.shape, lambda b, n: (0,) * a.ndim)
    by_seq = lambda a: a.reshape(batch, seq, RET_WIDTH)
    out = pl.pallas_call(
        functools.partial(_retention_kernel, chunk=chunk),
        grid=(batch // group, nc),
        in_specs=[tok_spec] * 4 + [full(dec2), full(qdec), full(kdec), full(rdec), full(bmask),
                                   full(gmean), full(gn_gain)],
        out_specs=tok_spec,
        out_shape=jax.ShapeDtypeStruct((batch, seq, RET_WIDTH), BF16),
        scratch_shapes=[pltpu.VMEM((group * RET_PAIRS, LANES, LANES), F32)],
        compiler_params=pltpu.CompilerParams(
            dimension_semantics=("parallel", "arbitrary"), vmem_limit_bytes=VMEM_LIMIT),
        name="retention",
    )(by_seq(rq), by_seq(rk), by_seq(rv), by_seq(rg), dec2, qdec, kdec, rdec, bmask, gmean, gn_gain)
    return out.reshape(batch * seq, RET_WIDTH)


NEG_BIG = -1e30


V_EXT_ROWS = DIFF_V_DIM + 16
QUERY_CHUNK = 256
SCORES_AHEAD_FULL = 3
SCORES_AHEAD_DIAG = 3
STAGE_SLOTS = 16


def _diag_chunks(tq, tk, d):
    assert tk == 2 * QUERY_CHUNK
    per_softmax = tq // QUERY_CHUNK
    out = []
    for c in range(2 * per_softmax):
        q0 = (c % per_softmax) * QUERY_CHUNK
        if q0 + QUERY_CHUNK - 1 < d * tk:
            continue
        kind = "full" if q0 >= (d + 1) * tk else ("tri" if q0 == d * tk else "low_tri")
        out.append((c, kind))
    return out


def _accumulate(acc_ref, cs, alpha, pv):
    acc_ref[:, cs] = alpha * acc_ref[:, cs] + pv


def _diffattn_kernel(q_ref, k_ref, vt_ref, lq1_ref, lk1_ref, lq2_ref, lk2_ref, gain_ref, bias_ref, o_ref,
                     qs_ref, vext_ref, m_ref, acc_ref, stage_ref, *, tq, tk, lambda_init):
    i = pl.program_id(2)
    nk = vext_ref.shape[0]

    @pl.when(i == 0)
    def _():
        for j in range(nk):
            vext_ref[j, 0:DIFF_V_DIM, :] = vt_ref[:, j * tk:(j + 1) * tk]
            vext_ref[j, DIFF_V_DIM:V_EXT_ROWS, :] = jnp.ones((V_EXT_ROWS - DIFF_V_DIM, tk), BF16)

    q = q_ref[...]
    lane = lax.broadcasted_iota(jnp.int32, q.shape, 1)
    zero = jnp.zeros_like(q)
    qs_ref[0:tq, :] = jnp.where(lane < DIFF_QK_DIM, q, zero)
    qs_ref[tq:2 * tq, :] = jnp.where(lane < DIFF_QK_DIM, zero, q)
    m_ref[...] = jnp.full_like(m_ref, NEG_BIG)
    acc_ref[...] = jnp.zeros_like(acc_ref)

    def step(work, n_ahead):
        chunk = lambda c: slice(c * QUERY_CHUNK, (c + 1) * QUERY_CHUNK)

        def scores(j, c, kind):
            n_keys = QUERY_CHUNK if kind == "tri" else tk
            start = pl.multiple_of(j * tk, tk)
            return _dot_nt(k_ref[pl.ds(start, n_keys), :], qs_ref[chunk(c), :])

        ahead = [scores(*work[n]) for n in range(min(n_ahead, len(work)))]
        pending = None
        for n, (j, c, kind) in enumerate(work):
            cs = chunk(c)
            st = ahead.pop(0)
            if n + n_ahead < len(work):
                ahead.append(scores(*work[n + n_ahead]))
            slot = n % stage_ref.shape[0]
            n_keys = st.shape[0]
            stage_ref[slot, 0:n_keys, :] = st
            if kind == "full":
                st = stage_ref[slot]
            else:
                causal = stage_ref[slot, n_keys - QUERY_CHUNK:n_keys, :] + bias_ref[...]
                st = causal if kind == "tri" else jnp.concatenate(
                    [stage_ref[slot, 0:n_keys - QUERY_CHUNK, :], causal], axis=0)
            m_old = m_ref[:, cs]
            m_new = jnp.maximum(m_old, jnp.max(st, axis=0, keepdims=True))
            alpha = jnp.exp2(m_old - m_new)
            p = jnp.exp2(st - m_new).astype(BF16)
            m_ref[:, cs] = m_new
            pv = _dot(vext_ref[j, :, 0:st.shape[0]], p)
            if pending is not None:
                pending()
            pending = functools.partial(_accumulate, acc_ref, cs, alpha, pv)
        pending()

    tiles_per_q = tq // tk
    n_chunks = 2 * tq // QUERY_CHUNK

    def full_tiles(it):
        return [(it * tiles_per_q + d, c, "full") for d in range(tiles_per_q) for c in range(n_chunks)]

    lax.fori_loop(0, i, lambda it, c: (step(full_tiles(it), SCORES_AHEAD_FULL), c)[1], 0)
    kind_order = {"full": 0, "low_tri": 1, "tri": 2}
    diag = lambda it: [(it * tiles_per_q + d, c, kind) for d in range(tiles_per_q)
                       for c, kind in sorted(_diag_chunks(tq, tk, d), key=lambda ck: kind_order[ck[1]])]
    lax.fori_loop(i, i + 1, lambda it, c: (step(diag(it), SCORES_AHEAD_DIAG), c)[1], 0)

    lam = (jnp.exp(jnp.sum(lq1_ref[...] * lk1_ref[...], axis=-1, keepdims=True))
           - jnp.exp(jnp.sum(lq2_ref[...] * lk2_ref[...], axis=-1, keepdims=True)) + lambda_init)
    acc = acc_ref[...]
    o2 = acc[0:DIFF_V_DIM, :] * (1.0 / acc[DIFF_V_DIM:DIFF_V_DIM + 1, :])
    ot = o2[:, :tq] - lam * o2[:, tq:]
    ms = jnp.mean(ot * ot, axis=0, keepdims=True)
    ot = ot * lax.rsqrt(ms + EPS) * gain_ref[...] * (1.0 - lambda_init)
    o_ref[...] = ot.T.astype(BF16)


def _diffattn(dq, dk, dvt, lam_q1, lam_k1, lam_q2, lam_k2, gain, batch, seq, lambda_init, tq, tk):
    nq = seq // tq
    q_spec = pl.BlockSpec((tq, LANES), lambda b, h, i: (b * nq + i, h))
    k_spec = pl.BlockSpec((seq, LANES), lambda b, h, i: (b, h))
    vt_spec = pl.BlockSpec((DIFF_V_DIM, seq), lambda b, h, i: (h, b))
    vec = lambda a: pl.BlockSpec(a.shape, lambda b, h, i: (0, 0))
    key = np.arange(QUERY_CHUNK)[:, None]
    query = np.arange(QUERY_CHUNK)[None, :]
    bias = jnp.asarray(np.where(key <= query, 0.0, NEG_BIG), F32)
    return pl.pallas_call(
        functools.partial(_diffattn_kernel, tq=tq, tk=tk, lambda_init=lambda_init),
        grid=(batch, DIFF_HEADS, nq),
        in_specs=[q_spec, k_spec, vt_spec, vec(lam_q1), vec(lam_k1), vec(lam_q2), vec(lam_k2), vec(gain),
                  vec(bias)],
        out_specs=q_spec,
        out_shape=jax.ShapeDtypeStruct(dq.shape, BF16),
        scratch_shapes=[
            pltpu.VMEM((2 * tq, LANES), BF16),
            pltpu.VMEM((seq // tk, V_EXT_ROWS, tk), BF16),
            pltpu.VMEM((1, 2 * tq), F32),
            pltpu.VMEM((V_EXT_ROWS, 2 * tq), F32),
            pltpu.VMEM((STAGE_SLOTS, tk, QUERY_CHUNK), F32),
        ],
        compiler_params=pltpu.CompilerParams(
            dimension_semantics=("parallel", "parallel", "arbitrary"), vmem_limit_bytes=VMEM_LIMIT),
        name="diffattn",
    )(dq, dk, dvt, lam_q1, lam_k1, lam_q2, lam_k2, gain, bias)


def _route(logits):
    r = [logits[g:g + 1, :] for g in range(N_GROUPS)]
    gmax = jnp.maximum(jnp.maximum(r[0], r[1]), jnp.maximum(r[2], r[3]))
    g_idx = jnp.where(r[0] == gmax, 0, jnp.where(r[1] == gmax, 1, jnp.where(r[2] == gmax, 2, 3)))
    denom = sum(jnp.exp(rg - gmax) for rg in r)
    g_weight = 1.0 / denom
    sel = jnp.zeros((EXPERTS_PER_GROUP, logits.shape[1]), F32)
    for g in range(N_GROUPS):
        rows = logits[8 + g * EXPERTS_PER_GROUP:8 + (g + 1) * EXPERTS_PER_GROUP, :]
        sel = jnp.where(g_idx == g, rows, sel)
    eidx = lax.broadcasted_iota(jnp.int32, sel.shape, 0)
    v1 = jnp.max(sel, axis=0, keepdims=True)
    i1 = jnp.min(jnp.where(sel == v1, eidx, EXPERTS_PER_GROUP), axis=0, keepdims=True)
    sel2 = jnp.where(eidx == i1, -jnp.inf, sel)
    v2 = jnp.max(sel2, axis=0, keepdims=True)
    i2 = jnp.min(jnp.where(sel2 == v2, eidx, EXPERTS_PER_GROUP), axis=0, keepdims=True)
    e2 = jnp.exp(v2 - v1)
    w1 = g_weight / (1.0 + e2)
    w2 = g_weight * e2 / (1.0 + e2)
    return g_idx, i1, i2, w1, w2


OUTPROJ_PARTS = 2


def _outproj_kernel(ret_ref, diff_ref, x_ref, mod_ref, gain_ref, wo_ref, wr_ref, br_ref, tri_ref,
                    x1_ref, h2_ref, ri_ref, rw_ref, cnt_ref, *, tiles_per_batch):
    b = pl.program_id(0) // tiles_per_batch
    gate1 = mod_ref[pl.ds(b, 1), 2 * D_MODEL:3 * D_MODEL]
    shift = mod_ref[pl.ds(b, 1), 3 * D_MODEL:4 * D_MODEL]
    scale = mod_ref[pl.ds(b, 1), 4 * D_MODEL:5 * D_MODEL]
    wr = wr_ref[...]
    tm = x_ref.shape[0]
    parts = [slice(n * tm // OUTPROJ_PARTS, (n + 1) * tm // OUTPROJ_PARTS) for n in range(OUTPROJ_PARTS)]
    mix = [_dot(jnp.concatenate([ret_ref[r, :], diff_ref[r, :]], axis=1), wo_ref[...]) for r in parts]
    for r, m in zip(parts, mix):
        x1_ref[r, :] = x_ref[r, :] + gate1 * m
    h_split = [_split_bf16(_norm_modulate(x1_ref[r, :], gain_ref[...], shift, scale)) for r in parts]
    for r, (h_hi, _) in zip(parts, h_split):
        h2_ref[r, :] = h_hi
    by_hi = [_dot_nt(wr, h_hi) for h_hi, _ in h_split]
    by_lo = [_dot_nt(wr[:ROUTER_ROWS], h_lo) for _, h_lo in h_split]
    logits = [a[:ROUTER_ROWS] + a[ROUTER_ROWS:] + c + br_ref[...] for a, c in zip(by_hi, by_lo)]
    routed = [_route(lg) for lg in logits]
    g_idx, i1, i2, w1, w2 = [jnp.concatenate([rt[n] for rt in routed], axis=1) for n in range(5)]
    logits = jnp.concatenate(logits, axis=1)
    e1 = g_idx * EXPERTS_PER_GROUP + i1
    e2 = g_idx * EXPERTS_PER_GROUP + i2
    eidx = lax.broadcasted_iota(jnp.int32, (N_EXPERTS, logits.shape[1]), 0)
    hit1 = eidx == e1
    hit2 = eidx == e2
    onehot = jnp.where(hit1 | hit2, 1.0, 0.0)
    before = _dot(onehot.astype(BF16), tri_ref[...])
    r1 = jnp.sum(jnp.where(hit1, before, 0.0), axis=0, keepdims=True)
    r2 = jnp.sum(jnp.where(hit2, before, 0.0), axis=0, keepdims=True)
    zi = jnp.zeros_like(e1)
    ri_ref[...] = jnp.concatenate([e1, e2, r1.astype(jnp.int32), r2.astype(jnp.int32), zi, zi, zi, zi], axis=0)
    zf = jnp.zeros_like(w1)
    rw_ref[...] = jnp.concatenate([w1, w2, zf, zf, zf, zf, zf, zf], axis=0)
    counts = jnp.sum(onehot, axis=1, keepdims=True)
    cnt_ref[0] = jnp.broadcast_to(counts, (N_EXPERTS, LANES)).astype(jnp.int32)


def _outproj(ret_out, diff_out, x2, mod, gain, w_out, wr, br, seq, tm):
    tokens = x2.shape[0]
    tiles_per_batch = seq // tm
    n_tiles = tokens // tm
    tri = jnp.asarray(np.arange(tm)[:, None] < np.arange(tm)[None, :], BF16)
    tok_spec = lambda w: pl.BlockSpec((tm, w), lambda i: (i, 0))
    row_spec = pl.BlockSpec((8, tm), lambda i: (0, i))
    full = lambda a: pl.BlockSpec(a.shape, lambda i: (0,) * a.ndim)
    return pl.pallas_call(
        functools.partial(_outproj_kernel, tiles_per_batch=tiles_per_batch),
        grid=(n_tiles,),
        in_specs=[tok_spec(RET_WIDTH), tok_spec(DIFF_WIDTH), tok_spec(D_MODEL), full(mod), full(gain),
                  full(w_out), full(wr), full(br), full(tri)],
        out_specs=[tok_spec(D_MODEL), tok_spec(D_MODEL), row_spec, row_spec,
                   pl.BlockSpec((1, N_EXPERTS, LANES), lambda i: (i, 0, 0))],
        out_shape=[jax.ShapeDtypeStruct((tokens, D_MODEL), F32),
                   jax.ShapeDtypeStruct((tokens, D_MODEL), BF16),
                   jax.ShapeDtypeStruct((8, tokens), jnp.int32),
                   jax.ShapeDtypeStruct((8, tokens), F32),
                   jax.ShapeDtypeStruct((n_tiles, N_EXPERTS, LANES), jnp.int32)],
        compiler_params=pltpu.CompilerParams(
            dimension_semantics=("parallel",), vmem_limit_bytes=VMEM_LIMIT),
        name="outproj",
    )(ret_out, diff_out, x2, mod, gain, w_out, wr, br, tri)


CHUNK = 8
BIG_PIECE = 4 * CHUNK
MAX_SMALL_PIECES = N_EXPERTS * (BIG_PIECE // CHUNK - 1)
PIECE_TABLES = ("n_big", "big_src", "big_dst", "n_small", "small_src", "small_dst")
TMX = 512


def _local_rows(tm):
    rows = 2 * tm + N_EXPERTS * (CHUNK - 1)
    return (rows + 15) // 16 * 16


def _max_big_pieces(tm):
    return _local_rows(tm) // BIG_PIECE


def _sorted_rows_alloc(tokens, tm):
    worst = 2 * tokens + (tokens // tm) * N_EXPERTS * (CHUNK - 1) + N_EXPERTS * (TMX - CHUNK)
    return (worst + TMX - 1) // TMX * TMX


def _dispatch_plan(cnt, tokens, tm):
    i32 = jnp.int32
    pad = (cnt + CHUNK - 1) // CHUNK * CHUNK
    local_end = jnp.cumsum(pad, axis=1)
    local_start = local_end - pad
    seg_rows = jnp.sum(pad, axis=0)
    seg_pad = (seg_rows + TMX - 1) // TMX * TMX
    seg_end = jnp.cumsum(seg_pad)
    seg_start = seg_end - seg_pad
    run_dst = seg_start[None, :] + jnp.cumsum(pad, axis=0) - pad

    def pieces(count, offset, size, max_n):
        end = jnp.cumsum(count, axis=1)
        start = end - count
        k = jnp.arange(max_n, dtype=i32)[None, :, None]
        owns = (start[:, None, :] <= k) & (k < end[:, None, :])
        within = size * (k - start[:, None, :]) + offset[:, None, :]
        src = jnp.sum(jnp.where(owns, local_start[:, None, :] + within, 0), axis=-1)
        dst = jnp.sum(jnp.where(owns, run_dst[:, None, :] + within, 0), axis=-1)
        return end[:, -1].astype(i32), src.reshape(-1).astype(i32), dst.reshape(-1).astype(i32)

    n_big, big_src, big_dst = pieces(pad // BIG_PIECE, jnp.zeros_like(pad), BIG_PIECE, _max_big_pieces(tm))
    n_small, small_src, small_dst = pieces(pad % BIG_PIECE // CHUNK, pad // BIG_PIECE * BIG_PIECE, CHUNK,
                                           MAX_SMALL_PIECES)
    m = TMX * jnp.arange(_sorted_rows_alloc(tokens, tm) // TMX, dtype=i32)
    tile_expert = jnp.minimum(jnp.sum(seg_end[None, :] <= m[:, None], axis=-1), N_EXPERTS - 1)
    towns = (seg_start[None, :] <= m[:, None]) & (m[:, None] < seg_end[None, :])
    used = seg_pad > 0
    parity = (jnp.cumsum(used) - used) % 2
    eids = jnp.arange(N_EXPERTS, dtype=i32)
    later_used = (eids[None, :] > eids[:, None]) & used[None, :]
    next_used = jnp.min(jnp.where(later_used, eids[None, :], N_EXPERTS), axis=1)
    next_used = jnp.where(next_used == N_EXPERTS, -1, next_used)
    pick = lambda per_expert: jnp.sum(jnp.where(towns, per_expert[None, :], 0), axis=-1)
    tile_first = jnp.sum(jnp.where(towns & (seg_start[None, :] == m[:, None]), 1, 0), axis=-1)
    tile_next = jnp.where(jnp.any(towns, axis=-1), pick(next_used), -1)
    return dict(
        tile_first=tile_first.astype(i32),
        tile_slot=pick(parity).astype(i32),
        tile_next=tile_next.astype(i32),
        tile_rows=jnp.clip(pick(seg_start + seg_rows) - m, 0, TMX).astype(i32),
        local_start=local_start.reshape(-1).astype(i32),
        n_big=n_big, big_src=big_src, big_dst=big_dst,
        n_small=n_small, small_src=small_src, small_dst=small_dst,
        tail_base=(seg_start + seg_rows).astype(i32),
        tail_rows=(seg_pad - seg_rows).astype(i32),
        tile_expert=tile_expert.astype(i32),
        n_used=(seg_end[-1:] // TMX).astype(i32),
    )


WAIT_UNROLL = 8


def _wait_times(copy, n):
    lax.fori_loop(0, n // WAIT_UNROLL, lambda i, c: ([copy.wait() for _ in range(WAIT_UNROLL)], c)[1], 0)
    lax.fori_loop(0, n % WAIT_UNROLL, lambda i, c: (copy.wait(), c)[1], 0)


def _for_each(n, body, unroll=4):
    main = n // unroll
    lax.fori_loop(0, main, lambda i, c: ([body(i * unroll + u) for u in range(unroll)], c)[1], 0)
    lax.fori_loop(main * unroll, n, lambda j, c: (body(j), c)[1], 0)


def _local_slots(ri_ref, local_start_ref, tile):
    e1, e2 = ri_ref[0:1, :], ri_ref[1:2, :]
    s1, s2 = ri_ref[2:3, :], ri_ref[3:4, :]
    for e in range(N_EXPERTS):
        start = local_start_ref[tile * N_EXPERTS + e]
        s1 = s1 + jnp.where(e1 == e, start, 0)
        s2 = s2 + jnp.where(e2 == e, start, 0)
    return s1, s2


def _run_pieces(piece_refs, local_ref, sorted_ref, sem_ref, to_sorted):
    n_big_ref, big_src_ref, big_dst_ref, n_small_ref, small_src_ref, small_dst_ref = piece_refs
    r_loc = local_ref.shape[1]
    kinds = [(BIG_PIECE, n_big_ref, big_src_ref, big_dst_ref, r_loc // BIG_PIECE),
             (CHUNK, n_small_ref, small_src_ref, small_dst_ref, MAX_SMALL_PIECES)]

    def copy(sl, size, local_row, sorted_row):
        local = local_ref.at[sl, pl.ds(pl.multiple_of(local_row, CHUNK), size), :]
        srt = sorted_ref.at[pl.ds(pl.multiple_of(sorted_row, CHUNK), size), :]
        return pltpu.make_async_copy(local, srt, sem_ref.at[sl]) if to_sorted else \
            pltpu.make_async_copy(srt, local, sem_ref.at[sl])

    def start(tile, sl):
        for size, n_ref, src_ref, dst_ref, max_n in kinds:
            _for_each(n_ref[tile], lambda k: copy(sl, size, src_ref[tile * max_n + k], dst_ref[tile * max_n + k])
                      .start())

    def wait(tile, sl):
        for size, n_ref, _, _, _ in kinds:
            _wait_times(copy(sl, size, 0, 0), n_ref[tile])

    return start, wait


def _dispatch_kernel(local_start_ref, n_big_ref, big_src_ref, big_dst_ref, n_small_ref, small_src_ref, small_dst_ref,
                     tail_base_ref, tail_rows_ref, n_used_ref,
                     h_ref, ri_ref, xs_ref, buf_ref, zero_ref, sem_ref, tail_sem_ref, *, r_loc):
    b = pl.program_id(0)
    nb = pl.num_programs(0)
    slot = b % 2
    start_runs, drain = _run_pieces(
        (n_big_ref, big_src_ref, big_dst_ref, n_small_ref, small_src_ref, small_dst_ref), buf_ref, xs_ref, sem_ref,
        to_sorted=True)

    @pl.when(b >= 2)
    def _():
        drain(b - 2, slot)

    s1, s2 = _local_slots(ri_ref, local_start_ref, b)
    rows = lax.broadcasted_iota(jnp.int32, (r_loc, s1.shape[1]), 0)
    perm = jnp.where((rows == s1) | (rows == s2), 1.0, 0.0).astype(BF16)
    buf_ref[slot] = _dot(perm, h_ref[...])
    start_runs(b, slot)

    def tail_pieces(e, act):
        n = tail_rows_ref[e]
        size = TMX // 2
        while size >= CHUNK:
            dst = pl.multiple_of(tail_base_ref[e] + (n & (-2 * size)), CHUNK)
            cp = pltpu.make_async_copy(zero_ref.at[pl.ds(0, size), :], xs_ref.at[pl.ds(dst, size), :],
                                       tail_sem_ref.at[0])
            pl.when((n & size) != 0)(functools.partial(act, cp))
            size //= 2

    def unused_tile_copy(m):
        dst = pl.multiple_of(m * TMX, TMX)
        return pltpu.make_async_copy(zero_ref, xs_ref.at[pl.ds(dst, TMX), :], tail_sem_ref.at[1])

    n_alloc = xs_ref.shape[0] // TMX

    @pl.when(b == 0)
    def _():
        zero_ref[...] = jnp.zeros_like(zero_ref)
        lax.fori_loop(0, N_EXPERTS, lambda e, c: (tail_pieces(e, lambda cp: cp.start()), c)[1], 0)
        lax.fori_loop(n_used_ref[0], n_alloc, lambda m, c: (unused_tile_copy(m).start(), c)[1], 0)

    @pl.when(b == nb - 1)
    def _():
        lax.fori_loop(0, N_EXPERTS, lambda e, c: (tail_pieces(e, lambda cp: cp.wait()), c)[1], 0)
        lax.fori_loop(n_used_ref[0], n_alloc, lambda m, c: (unused_tile_copy(m).wait(), c)[1], 0)

        @pl.when(b >= 1)
        def _():
            drain(b - 1, 1 - slot)

        drain(b, slot)


def _dispatch(h2, ri, plan, tm):
    tokens = h2.shape[0]
    r_loc = _local_rows(tm)
    prefetch = [plan["local_start"]] + [plan[k] for k in PIECE_TABLES] + [
        plan["tail_base"], plan["tail_rows"], plan["n_used"]]
    grid_spec = pltpu.PrefetchScalarGridSpec(
        num_scalar_prefetch=len(prefetch),
        grid=(tokens // tm,),
        in_specs=[pl.BlockSpec((tm, D_MODEL), lambda i, *_: (i, 0)),
                  pl.BlockSpec((8, tm), lambda i, *_: (0, i))],
        out_specs=pl.BlockSpec(memory_space=pl.ANY),
        scratch_shapes=[pltpu.VMEM((2, r_loc, D_MODEL), F32), pltpu.VMEM((TMX, D_MODEL), F32),
                        pltpu.SemaphoreType.DMA((2,)), pltpu.SemaphoreType.DMA((2,))],
    )
    return pl.pallas_call(
        functools.partial(_dispatch_kernel, r_loc=r_loc),
        grid_spec=grid_spec,
        out_shape=jax.ShapeDtypeStruct((_sorted_rows_alloc(tokens, tm), D_MODEL), F32),
        compiler_params=pltpu.CompilerParams(
            dimension_semantics=("arbitrary",), vmem_limit_bytes=VMEM_LIMIT),
        name="dispatch",
    )(*prefetch, h2, ri)


def _experts_kernel(tile_expert_ref, n_used_ref, first_ref, slot_ref, next_ref, rows_ref, xs_ref, wg_hbm, wu_hbm, wd_hbm,
                    ys_ref, wg_st, wu_st, wd_st, wg_bf, wu_bf, wd_bf, a_ref, u_ref, sem_ref):
    m = pl.program_id(0)

    def weight_copies(e, s):
        return [pltpu.make_async_copy(src.at[e], dst.at[s], sem_ref.at[s, n])
                for n, (src, dst) in enumerate([(wg_hbm, wg_st), (wu_hbm, wu_st), (wd_hbm, wd_st)])]

    @pl.when(m < n_used_ref[0])
    def _():
        @pl.when(first_ref[m] == 1)
        def _():
            s = slot_ref[m]

            @pl.when(m == 0)
            def _():
                for cp in weight_copies(tile_expert_ref[0], 0):
                    cp.start()

            for cp in weight_copies(tile_expert_ref[m], s):
                cp.wait()

            @pl.when(next_ref[m] >= 0)
            def _():
                for cp in weight_copies(next_ref[m], 1 - s):
                    cp.start()

            wg_bf[...] = wg_st[s].astype(BF16)
            wu_bf[...] = wu_st[s].astype(BF16)
            wd_bf[...] = wd_st[s].astype(BF16)

        def mlp(rows):
            x = xs_ref[rows, :].astype(BF16)
            a_ref[rows, :] = _dot(x, wg_bf[...])
            u_ref[rows, :] = _dot(x, wu_bf[...])
            hid = (_silu(a_ref[rows, :]) * u_ref[rows, :]).astype(BF16)
            ys_ref[rows, :] = _dot(hid, wd_bf[...])

        half = TMX // 2

        @pl.when(rows_ref[m] > half)
        def _():
            mlp(slice(0, TMX))

        @pl.when(rows_ref[m] <= half)
        def _():
            mlp(slice(0, half))
            ys_ref[half:, :] = jnp.zeros((TMX - half, D_MODEL), F32)


def _experts(xs, plan, wg, wu, wd):
    n_tiles = xs.shape[0] // TMX
    last_used = lambda m, n_used: jnp.minimum(m, n_used[0] - 1)
    row_spec = pl.BlockSpec((TMX, D_MODEL), lambda m, te, nu, *_: (last_used(m, nu), 0))
    hbm = pl.BlockSpec(memory_space=pl.ANY)
    up_shape, down_shape = (D_MODEL, D_EXPERT), (D_EXPERT, D_MODEL)
    grid_spec = pltpu.PrefetchScalarGridSpec(
        num_scalar_prefetch=6,
        grid=(n_tiles,),
        in_specs=[row_spec, hbm, hbm, hbm],
        out_specs=row_spec,
        scratch_shapes=[pltpu.VMEM((2,) + up_shape, F32), pltpu.VMEM((2,) + up_shape, F32),
                        pltpu.VMEM((2,) + down_shape, F32),
                        pltpu.VMEM(up_shape, BF16), pltpu.VMEM(up_shape, BF16), pltpu.VMEM(down_shape, BF16),
                        pltpu.VMEM((TMX, D_EXPERT), F32), pltpu.VMEM((TMX, D_EXPERT), F32),
                        pltpu.SemaphoreType.DMA((2, 3))],
    )
    return pl.pallas_call(
        _experts_kernel,
        grid_spec=grid_spec,
        out_shape=jax.ShapeDtypeStruct(xs.shape, F32),
        input_output_aliases={6: 0},
        compiler_params=pltpu.CompilerParams(
            dimension_semantics=("arbitrary",), vmem_limit_bytes=VMEM_LIMIT),
        name="experts",
    )(plan["tile_expert"], plan["n_used"], plan["tile_first"], plan["tile_slot"], plan["tile_next"], plan["tile_rows"],
      xs, wg, wu, wd)


def _combine_kernel(local_start_ref, n_big_ref, big_src_ref, big_dst_ref, n_small_ref, small_src_ref, small_dst_ref,
                    ys_ref, ri_ref, rw_ref, x1_ref, mod_ref, gain_ref, o_ref, buf_ref, sem_ref,
                    *, r_loc, tiles_per_batch):
    b = pl.program_id(0)
    nb = pl.num_programs(0)
    slot = b % 2
    fetch, wait_runs = _run_pieces(
        (n_big_ref, big_src_ref, big_dst_ref, n_small_ref, small_src_ref, small_dst_ref), buf_ref, ys_ref, sem_ref,
        to_sorted=False)

    @pl.when(b == 0)
    def _():
        buf_ref[...] = jnp.zeros_like(buf_ref)
        fetch(0, 0)

    @pl.when(b + 1 < nb)
    def _():
        fetch(b + 1, 1 - slot)

    wait_runs(b, slot)

    s1, s2 = _local_slots(ri_ref, local_start_ref, b)
    rows = lax.broadcasted_iota(jnp.int32, (r_loc, s1.shape[1]), 0)
    hit1 = rows == s1
    hit2 = rows == s2
    w_row = jnp.sum(jnp.where(hit1, rw_ref[0:1, :], jnp.where(hit2, rw_ref[1:2, :], 0.0)), axis=1, keepdims=True)
    perm = jnp.where(hit1 | hit2, 1.0, 0.0).astype(BF16)
    yw = (buf_ref[slot] * w_row).astype(BF16)
    moe = _dot_tn(perm, yw)
    batch = b // tiles_per_batch
    gate2 = mod_ref[pl.ds(batch, 1), 5 * D_MODEL:6 * D_MODEL]
    x2 = x1_ref[...] + gate2 * moe
    ms = jnp.mean(x2 * x2, axis=-1, keepdims=True)
    o_ref[...] = x2 * lax.rsqrt(ms + EPS) * gain_ref[...]


def _combine(ys, ri, rw, x1, mod, gain, plan, seq, tm):
    tokens = x1.shape[0]
    r_loc = _local_rows(tm)
    row_spec = pl.BlockSpec((8, tm), lambda i, *_: (0, i))
    tok_spec = pl.BlockSpec((tm, D_MODEL), lambda i, *_: (i, 0))
    full = lambda a: pl.BlockSpec(a.shape, lambda i, *_: (0,) * a.ndim)
    prefetch = [plan["local_start"]] + [plan[k] for k in PIECE_TABLES]
    grid_spec = pltpu.PrefetchScalarGridSpec(
        num_scalar_prefetch=len(prefetch),
        grid=(tokens // tm,),
        in_specs=[pl.BlockSpec(memory_space=pl.ANY), row_spec, row_spec, tok_spec, full(mod), full(gain)],
        out_specs=tok_spec,
        scratch_shapes=[pltpu.VMEM((2, r_loc, D_MODEL), F32), pltpu.SemaphoreType.DMA((2,))],
    )
    return pl.pallas_call(
        functools.partial(_combine_kernel, r_loc=r_loc, tiles_per_batch=seq // tm),
        grid_spec=grid_spec,
        out_shape=jax.ShapeDtypeStruct((tokens, D_MODEL), F32),
        compiler_params=pltpu.CompilerParams(
            dimension_semantics=("arbitrary",), vmem_limit_bytes=VMEM_LIMIT),
        name="combine",
    )(*prefetch, ys, ri, rw, x1, mod, gain)


def _rotary_tables(seq):
    half = RET_HEAD_DIM // 2
    inv_freq = 1.0 / (ROPE_BASE ** (np.arange(half, dtype=np.float64) / half))
    ang = np.arange(seq, dtype=np.float64)[:, None] * inv_freq[None, :]
    cos = np.cos(ang)
    sin = np.sin(ang)
    f32 = lambda a: jnp.asarray(a.astype(np.float32))
    return f32(np.tile(cos, (1, 4))), f32(np.concatenate([-sin, sin, -sin, sin], axis=1))


def _pick_tile(n, pref):
    t = min(n, pref)
    assert n % t == 0, (n, t)
    return t


def kernel(x, c, ada_w, ada_b, norm1_gain, norm2_gain, w_in, w_out, ret_gn_gain, lam_q1, lam_k1, lam_q2,
           lam_k2, diff_subln_gain, w_group, b_group, w_expert, b_expert, w_gate, w_up, w_down, final_gain):
    batch, seq, d = x.shape
    assert d == D_MODEL and batch <= 8 and ada_w.shape[0] == 1
    layer = 0
    lambda_init = 0.8 - 0.6 * math.exp(-0.3 * layer)
    tokens = batch * seq
    x2 = x.reshape(tokens, d)
    tm = _pick_tile(seq, 512)

    c_pad = jnp.zeros((8, d), F32).at[:batch].set(c)
    mod = _adaln(c_pad, ada_w[layer], ada_b[layer].reshape(1, -1))

    cos_t, sin_t = _rotary_tables(seq)
    rq, rk, rv, rg, dq, dk, dvt = _inproj(
        x2, mod, norm1_gain[layer].reshape(1, d), w_in[layer].astype(BF16), cos_t, sin_t, seq, tm)

    ret_out = _retention(rq, rk, rv, rg, ret_gn_gain[layer].reshape(1, RET_WIDTH), batch, seq,
                         _pick_tile(seq, 256))
    diff_out = _diffattn(
        dq, dk, dvt, lam_q1[layer].reshape(1, -1), lam_k1[layer].reshape(1, -1), lam_q2[layer].reshape(1, -1),
        lam_k2[layer].reshape(1, -1), diff_subln_gain[layer].reshape(-1, 1), batch, seq, lambda_init,
        _pick_tile(seq, 2048), 2 * QUERY_CHUNK)

    w_router = jnp.concatenate(
        [w_group[layer].T, jnp.zeros((8 - N_GROUPS, d), F32), w_expert[layer].reshape(d, N_EXPERTS).T], axis=0)
    b_router = jnp.concatenate(
        [b_group[layer], jnp.zeros((8 - N_GROUPS,), F32), b_expert[layer].reshape(N_EXPERTS)]).reshape(-1, 1)
    wr_hi = w_router.astype(BF16)
    wr_lo = (w_router - wr_hi.astype(F32)).astype(BF16)
    x1, h2, ri, rw, cnt = _outproj(ret_out, diff_out, x2, mod, norm2_gain[layer].reshape(1, d),
                                   w_out[layer].astype(BF16), jnp.concatenate([wr_hi, wr_lo], axis=0), b_router,
                                   seq, tm)

    plan = _dispatch_plan(cnt[:, :, 0], tokens, tm)
    xs = _dispatch(h2, ri, plan, tm)
    ys = _experts(xs, plan, w_gate[layer].reshape(N_EXPERTS, d, D_EXPERT),
                  w_up[layer].reshape(N_EXPERTS, d, D_EXPERT), w_down[layer].reshape(N_EXPERTS, D_EXPERT, d))
    out = _combine(ys, ri, rw, x1, mod, final_gain.reshape(1, d), plan, seq, tm)
    return out.reshape(batch, seq, d)
```

```python
import functools
import math

import jax
import jax.numpy as jnp
import numpy as np
from jax import lax
from jax.experimental import pallas as pl
from jax.experimental.pallas import tpu as pltpu

F32 = jnp.float32
BF16 = jnp.bfloat16

D_MODEL = 1024
RET_HEAD_DIM = 64
RET_WIDTH = 512
RET_HEADS = 8
RET_PAIRS = RET_HEADS // 2
DIFF_QK_DIM = 64
DIFF_V_DIM = 128
DIFF_HEADS = 4
DIFF_WIDTH = 512
N_GROUPS = 4
EXPERTS_PER_GROUP = 8
N_EXPERTS = N_GROUPS * EXPERTS_PER_GROUP
D_EXPERT = 512
N_MOD = 6
ROPE_BASE = 10000.0
EPS = 1e-6
LANES = 128
ROUTER_ROWS = 8 + N_EXPERTS
VMEM_LIMIT = 56 * 1024 * 1024


def _dot(a, b):
    return jnp.dot(a, b, preferred_element_type=F32)


def _dot_nt(a, b):
    return lax.dot_general(a, b, (((1,), (1,)), ((), ())), preferred_element_type=F32)


def _dot_tn(a, b):
    return lax.dot_general(a, b, (((0,), (0,)), ((), ())), preferred_element_type=F32)


def _split_bf16(x):
    hi = x.astype(BF16)
    lo = (x - hi.astype(F32)).astype(BF16)
    return hi, lo


def _silu(x):
    return x / (1.0 + jnp.exp(-x))


def _adaln_kernel(c_ref, w_ref, b_ref, o_ref):
    ca = _silu(c_ref[...])
    c_hi, c_lo = _split_bf16(ca)
    w_hi, w_lo = _split_bf16(w_ref[...])
    o_ref[...] = _dot(c_hi, w_hi) + _dot(c_lo, w_hi) + _dot(c_hi, w_lo) + b_ref[...]


def _adaln(c_pad, ada_w, ada_b):
    n_out = ada_w.shape[1]
    tn = D_MODEL
    return pl.pallas_call(
        _adaln_kernel,
        grid=(n_out // tn,),
        in_specs=[
            pl.BlockSpec((8, D_MODEL), lambda j: (0, 0)),
            pl.BlockSpec((D_MODEL, tn), lambda j: (0, j)),
            pl.BlockSpec((1, tn), lambda j: (0, j)),
        ],
        out_specs=pl.BlockSpec((8, tn), lambda j: (0, j)),
        out_shape=jax.ShapeDtypeStruct((8, n_out), F32),
        compiler_params=pltpu.CompilerParams(vmem_limit_bytes=VMEM_LIMIT),
        name="adaln",
    )(c_pad, ada_w, ada_b)


def _norm_modulate(x, gain, shift, scale):
    ms = jnp.mean(x * x, axis=-1, keepdims=True)
    y = x * lax.rsqrt(ms + EPS) * gain
    return y * (1.0 + scale) + shift


def _rotary_slab(x, cos, sin_signed, lane_lo):
    swapped = jnp.where(lane_lo, pltpu.roll(x, 96, 1), pltpu.roll(x, 32, 1))
    return x * cos + swapped * sin_signed


def _inproj_kernel(x_ref, mod_ref, gain_ref, w_ref, cos_ref, sin_ref,
                   rq_ref, rk_ref, rv_ref, rg_ref, dq_ref, dk_ref, dvt_ref, *, tiles_per_batch):
    b = pl.program_id(0) // tiles_per_batch
    shift = mod_ref[pl.ds(b, 1), 0:D_MODEL]
    scale = mod_ref[pl.ds(b, 1), D_MODEL:2 * D_MODEL]
    h = _norm_modulate(x_ref[...], gain_ref[...], shift, scale).astype(BF16)
    cos = cos_ref[...]
    sin = sin_ref[...]
    lane = lax.broadcasted_iota(jnp.int32, cos.shape, 1)
    lane_lo = (lane % 64) < 32

    def proj(chunk):
        return _dot(h, w_ref[:, chunk * RET_WIDTH:(chunk + 1) * RET_WIDTH])

    def rotary(acc, out_ref, post_scale):
        for s in range(RET_WIDTH // LANES):
            sl = slice(s * LANES, (s + 1) * LANES)
            out_ref[:, sl] = (_rotary_slab(acc[:, sl], cos, sin, lane_lo) * post_scale).astype(BF16)

    rotary(proj(0), rq_ref, 1.0)
    rotary(proj(1), rk_ref, RET_HEAD_DIM ** -0.5)
    rv_ref[...] = proj(2).astype(BF16)
    rg_ref[...] = _silu(proj(3)).astype(BF16)
    dq_ref[...] = (proj(4) * (DIFF_QK_DIM ** -0.5 * math.log2(math.e))).astype(BF16)
    dk_ref[...] = proj(5).astype(BF16)
    dvt_ref[...] = proj(6).T.astype(BF16)


def _inproj(x2, mod, gain, w_in, cos_t, sin_t, seq, tm):
    tokens = x2.shape[0]
    tiles_per_batch = seq // tm
    tok_spec = lambda w: pl.BlockSpec((tm, w), lambda i: (i, 0))
    tab_spec = pl.BlockSpec((tm, LANES), lambda i: (i % tiles_per_batch, 0))
    full = lambda a: pl.BlockSpec(a.shape, lambda i: (0,) * a.ndim)
    out = jax.ShapeDtypeStruct((tokens, RET_WIDTH), BF16)
    return pl.pallas_call(
        functools.partial(_inproj_kernel, tiles_per_batch=tiles_per_batch),
        grid=(tokens // tm,),
        in_specs=[tok_spec(D_MODEL), full(mod), full(gain), full(w_in), tab_spec, tab_spec],
        out_specs=[tok_spec(RET_WIDTH)] * 6 + [pl.BlockSpec((DIFF_WIDTH, tm), lambda i: (0, i))],
        out_shape=[out] * 6 + [jax.ShapeDtypeStruct((DIFF_WIDTH, tokens), BF16)],
        compiler_params=pltpu.CompilerParams(
            dimension_semantics=("parallel",), vmem_limit_bytes=VMEM_LIMIT),
        name="inproj",
    )(x2, mod, gain, w_in, cos_t, sin_t)


RET_SEQ_GROUP = 4


def _retention_kernel(q_ref, k_ref, v_ref, g_ref, dec_ref, qdec_ref, kdec_ref, rdec_ref,
                      bmask_ref, gmean_ref, gain_ref, o_ref, state_ref, *, chunk):
    @pl.when(pl.program_id(1) == 0)
    def _():
        state_ref[...] = jnp.zeros_like(state_ref)

    lane = lax.broadcasted_iota(jnp.int32, (chunk, LANES), 1)
    first_head = lane < RET_HEAD_DIM
    gmean = gmean_ref[...]
    bmask = bmask_ref[...]
    units = [(s, p) for s in range(q_ref.shape[0]) for p in range(RET_PAIRS)]
    sl = lambda p: slice(p * LANES, (p + 1) * LANES)
    q = [q_ref[s, :, sl(p)] for s, p in units]
    k = [k_ref[s, :, sl(p)] for s, p in units]
    v = [v_ref[s, :, sl(p)] for s, p in units]
    zero = jnp.zeros_like(q[0])
    q_stack = [jnp.concatenate([jnp.where(first_head, qu, zero), jnp.where(first_head, zero, qu)], axis=0)
               for qu in q]
    scores = [(_dot_nt(q_stack[n], k[n]) * dec_ref[p]).astype(BF16) for n, (_, p) in enumerate(units)]
    state = [state_ref[n] for n in range(len(units))]
    cross = [_dot(q[n], state[n].astype(BF16)) * qdec_ref[:, sl(p)] for n, (_, p) in enumerate(units)]
    k_dec = [(k[n].astype(F32) * kdec_ref[:, sl(p)]).astype(BF16) for n, (_, p) in enumerate(units)]
    for n, (_, p) in enumerate(units):
        state_ref[n] = state[n] * rdec_ref[p] + _dot_tn(k_dec[n], v[n]) * bmask
    intra2 = [_dot(scores[n], v[n]) for n in range(len(units))]
    y = [jnp.where(first_head, intra2[n][:chunk], intra2[n][chunk:]) + cross[n] for n in range(len(units))]
    seg_mean = lambda x: _dot(jnp.concatenate(_split_bf16(x), axis=1), gmean)
    mu = [seg_mean(yu) for yu in y]
    d = [yu - mu_u for yu, mu_u in zip(y, mu)]
    var = [seg_mean(du * du) for du in d]
    for n, (s, p) in enumerate(units):
        yn = d[n] * lax.rsqrt(var[n] + EPS) * gain_ref[:, sl(p)]
        o_ref[s, :, sl(p)] = (g_ref[s, :, sl(p)].astype(F32) * yn).astype(BF16)


def _retention_tables(chunk):
    heads = np.arange(RET_HEADS, dtype=np.float64)
    log_gamma = np.log(1.0 - np.exp2(-5.0 - heads))
    idx = np.arange(chunk)
    rel = (idx[:, None] - idx[None, :]).astype(np.float64)
    decay = np.where(rel[None] >= 0, np.exp(log_gamma[:, None, None] * np.maximum(rel, 0.0)[None]), 0.0)
    dec2 = decay.reshape(RET_PAIRS, 2 * chunk, chunk)
    lane_lg = np.repeat(log_gamma, RET_HEAD_DIM)
    qdec = np.exp(lane_lg[None, :] * (idx + 1)[:, None])
    kdec = np.exp(lane_lg[None, :] * (chunk - 1 - idx)[:, None])
    rdec = np.exp(lane_lg * chunk).reshape(RET_PAIRS, LANES, 1) * np.ones((1, 1, LANES))
    blk = np.arange(LANES) // RET_HEAD_DIM
    bmask = (blk[:, None] == blk[None, :]).astype(np.float64)
    f32 = lambda a: jnp.asarray(a.astype(np.float32))
    gmean2 = np.concatenate([bmask, bmask], axis=0) / RET_HEAD_DIM
    return f32(dec2), f32(qdec), f32(kdec), f32(rdec), f32(bmask), f32(gmean2).astype(BF16)


def _retention(rq, rk, rv, rg, gn_gain, batch, seq, chunk):
    nc = seq // chunk
    group = RET_SEQ_GROUP if batch % RET_SEQ_GROUP == 0 else 1
    dec2, qdec, kdec, rdec, bmask, gmean = _retention_tables(chunk)
    tok_spec = pl.BlockSpec((group, chunk, RET_WIDTH), lambda b, n: (b, n, 0))
    full = lambda a: pl.BlockSpec(a.shape, lambda b, n: (0,) * a.ndim)
    by_seq = lambda a: a.reshape(batch, seq, RET_WIDTH)
    out = pl.pallas_call(
        functools.partial(_retention_kernel, chunk=chunk),
        grid=(batch // group, nc),
        in_specs=[tok_spec] * 4 + [full(dec2), full(qdec), full(kdec), full(rdec), full(bmask),
                                   full(gmean), full(gn_gain)],
        out_specs=tok_spec,
        out_shape=jax.ShapeDtypeStruct((batch, seq, RET_WIDTH), BF16),
        scratch_shapes=[pltpu.VMEM((group * RET_PAIRS, LANES, LANES), F32)],
        compiler_params=pltpu.CompilerParams(
            dimension_semantics=("parallel", "arbitrary"), vmem_limit_bytes=VMEM_LIMIT),
        name="retention",
    )(by_seq(rq), by_seq(rk), by_seq(rv), by_seq(rg), dec2, qdec, kdec, rdec, bmask, gmean, gn_gain)
    return out.reshape(batch * seq, RET_WIDTH)


NEG_BIG = -1e30


V_EXT_ROWS = DIFF_V_DIM + 16
QUERY_CHUNK = 256
SCORES_AHEAD_FULL = 3
SCORES_AHEAD_DIAG = 3
STAGE_SLOTS = 16


def _diag_chunks(tq, tk, d):
    assert tk == 2 * QUERY_CHUNK
    per_softmax = tq // QUERY_CHUNK
    out = []
    for c in range(2 * per_softmax):
        q0 = (c % per_softmax) * QUERY_CHUNK
        if q0 + QUERY_CHUNK - 1 < d * tk:
            continue
        kind = "full" if q0 >= (d + 1) * tk else ("tri" if q0 == d * tk else "low_tri")
        out.append((c, kind))
    return out


def _accumulate(acc_ref, cs, alpha, pv):
    acc_ref[:, cs] = alpha * acc_ref[:, cs] + pv


def _diffattn_kernel(q_ref, k_ref, vt_ref, lq1_ref, lk1_ref, lq2_ref, lk2_ref, gain_ref, bias_ref, o_ref,
                     qs_ref, vext_ref, m_ref, acc_ref, stage_ref, *, tq, tk, lambda_init):
    i = pl.program_id(2)
    nk = vext_ref.shape[0]

    @pl.when(i == 0)
    def _():
        for j in range(nk):
            vext_ref[j, 0:DIFF_V_DIM, :] = vt_ref[:, j * tk:(j + 1) * tk]
            vext_ref[j, DIFF_V_DIM:V_EXT_ROWS, :] = jnp.ones((V_EXT_ROWS - DIFF_V_DIM, tk), BF16)

    q = q_ref[...]
    lane = lax.broadcasted_iota(jnp.int32, q.shape, 1)
    zero = jnp.zeros_like(q)
    qs_ref[0:tq, :] = jnp.where(lane < DIFF_QK_DIM, q, zero)
    qs_ref[tq:2 * tq, :] = jnp.where(lane < DIFF_QK_DIM, zero, q)
    m_ref[...] = jnp.full_like(m_ref, NEG_BIG)
    acc_ref[...] = jnp.zeros_like(acc_ref)

    def step(work, n_ahead):
        chunk = lambda c: slice(c * QUERY_CHUNK, (c + 1) * QUERY_CHUNK)

        def scores(j, c, kind):
            n_keys = QUERY_CHUNK if kind == "tri" else tk
            start = pl.multiple_of(j * tk, tk)
            return _dot_nt(k_ref[pl.ds(start, n_keys), :], qs_ref[chunk(c), :])

        ahead = [scores(*work[n]) for n in range(min(n_ahead, len(work)))]
        pending = None
        for n, (j, c, kind) in enumerate(work):
            cs = chunk(c)
            st = ahead.pop(0)
            if n + n_ahead < len(work):
                ahead.append(scores(*work[n + n_ahead]))
            slot = n % stage_ref.shape[0]
            n_keys = st.shape[0]
            stage_ref[slot, 0:n_keys, :] = st
            if kind == "full":
                st = stage_ref[slot]
            else:
                causal = stage_ref[slot, n_keys - QUERY_CHUNK:n_keys, :] + bias_ref[...]
                st = causal if kind == "tri" else jnp.concatenate(
                    [stage_ref[slot, 0:n_keys - QUERY_CHUNK, :], causal], axis=0)
            m_old = m_ref[:, cs]
            m_new = jnp.maximum(m_old, jnp.max(st, axis=0, keepdims=True))
            alpha = jnp.exp2(m_old - m_new)
            p = jnp.exp2(st - m_new).astype(BF16)
            m_ref[:, cs] = m_new
            pv = _dot(vext_ref[j, :, 0:st.shape[0]], p)
            if pending is not None:
                pending()
            pending = functools.partial(_accumulate, acc_ref, cs, alpha, pv)
        pending()

    tiles_per_q = tq // tk
    n_chunks = 2 * tq // QUERY_CHUNK

    def full_tiles(it):
        return [(it * tiles_per_q + d, c, "full") for d in range(tiles_per_q) for c in range(n_chunks)]

    lax.fori_loop(0, i, lambda it, c: (step(full_tiles(it), SCORES_AHEAD_FULL), c)[1], 0)
    kind_order = {"full": 0, "low_tri": 1, "tri": 2}
    diag = lambda it: [(it * tiles_per_q + d, c, kind) for d in range(tiles_per_q)
                       for c, kind in sorted(_diag_chunks(tq, tk, d), key=lambda ck: kind_order[ck[1]])]
    lax.fori_loop(i, i + 1, lambda it, c: (step(diag(it), SCORES_AHEAD_DIAG), c)[1], 0)

    lam = (jnp.exp(jnp.sum(lq1_ref[...] * lk1_ref[...], axis=-1, keepdims=True))
           - jnp.exp(jnp.sum(lq2_ref[...] * lk2_ref[...], axis=-1, keepdims=True)) + lambda_init)
    acc = acc_ref[...]
    o2 = acc[0:DIFF_V_DIM, :] * (1.0 / acc[DIFF_V_DIM:DIFF_V_DIM + 1, :])
    ot = o2[:, :tq] - lam * o2[:, tq:]
    ms = jnp.mean(ot * ot, axis=0, keepdims=True)
    ot = ot * lax.rsqrt(ms + EPS) * gain_ref[...] * (1.0 - lambda_init)
    o_ref[...] = ot.T.astype(BF16)


def _diffattn(dq, dk, dvt, lam_q1, lam_k1, lam_q2, lam_k2, gain, batch, seq, lambda_init, tq, tk):
    nq = seq // tq
    q_spec = pl.BlockSpec((tq, LANES), lambda b, h, i: (b * nq + i, h))
    k_spec = pl.BlockSpec((seq, LANES), lambda b, h, i: (b, h))
    vt_spec = pl.BlockSpec((DIFF_V_DIM, seq), lambda b, h, i: (h, b))
    vec = lambda a: pl.BlockSpec(a.shape, lambda b, h, i: (0, 0))
    key = np.arange(QUERY_CHUNK)[:, None]
    query = np.arange(QUERY_CHUNK)[None, :]
    bias = jnp.asarray(np.where(key <= query, 0.0, NEG_BIG), F32)
    return pl.pallas_call(
        functools.partial(_diffattn_kernel, tq=tq, tk=tk, lambda_init=lambda_init),
        grid=(batch, DIFF_HEADS, nq),
        in_specs=[q_spec, k_spec, vt_spec, vec(lam_q1), vec(lam_k1), vec(lam_q2), vec(lam_k2), vec(gain),
                  vec(bias)],
        out_specs=q_spec,
        out_shape=jax.ShapeDtypeStruct(dq.shape, BF16),
        scratch_shapes=[
            pltpu.VMEM((2 * tq, LANES), BF16),
            pltpu.VMEM((seq // tk, V_EXT_ROWS, tk), BF16),
            pltpu.VMEM((1, 2 * tq), F32),
            pltpu.VMEM((V_EXT_ROWS, 2 * tq), F32),
            pltpu.VMEM((STAGE_SLOTS, tk, QUERY_CHUNK), F32),
        ],
        compiler_params=pltpu.CompilerParams(
            dimension_semantics=("parallel", "parallel", "arbitrary"), vmem_limit_bytes=VMEM_LIMIT),
        name="diffattn",
    )(dq, dk, dvt, lam_q1, lam_k1, lam_q2, lam_k2, gain, bias)


def _route(logits):
    r = [logits[g:g + 1, :] for g in range(N_GROUPS)]
    gmax = jnp.maximum(jnp.maximum(r[0], r[1]), jnp.maximum(r[2], r[3]))
    g_idx = jnp.where(r[0] == gmax, 0, jnp.where(r[1] == gmax, 1, jnp.where(r[2] == gmax, 2, 3)))
    denom = sum(jnp.exp(rg - gmax) for rg in r)
    g_weight = 1.0 / denom
    sel = jnp.zeros((EXPERTS_PER_GROUP, logits.shape[1]), F32)
    for g in range(N_GROUPS):
        rows = logits[8 + g * EXPERTS_PER_GROUP:8 + (g + 1) * EXPERTS_PER_GROUP, :]
        sel = jnp.where(g_idx == g, rows, sel)
    eidx = lax.broadcasted_iota(jnp.int32, sel.shape, 0)
    v1 = jnp.max(sel, axis=0, keepdims=True)
    i1 = jnp.min(jnp.where(sel == v1, eidx, EXPERTS_PER_GROUP), axis=0, keepdims=True)
    sel2 = jnp.where(eidx == i1, -jnp.inf, sel)
    v2 = jnp.max(sel2, axis=0, keepdims=True)
    i2 = jnp.min(jnp.where(sel2 == v2, eidx, EXPERTS_PER_GROUP), axis=0, keepdims=True)
    e2 = jnp.exp(v2 - v1)
    w1 = g_weight / (1.0 + e2)
    w2 = g_weight * e2 / (1.0 + e2)
    return g_idx, i1, i2, w1, w2


OUTPROJ_PARTS = 2
OUTPROJ_TILES_PER_STEP = 2


def _outproj_kernel(ret_ref, diff_ref, x_ref, mod_ref, gain_ref, wo_ref, wr_ref, br_ref, tri_ref,
                    x1_ref, h2_ref, ri_ref, rw_ref, cnt_ref, *, tiles_per_batch):
    b = pl.program_id(0) // tiles_per_batch
    gate1 = mod_ref[pl.ds(b, 1), 2 * D_MODEL:3 * D_MODEL]
    shift = mod_ref[pl.ds(b, 1), 3 * D_MODEL:4 * D_MODEL]
    scale = mod_ref[pl.ds(b, 1), 4 * D_MODEL:5 * D_MODEL]
    wr = wr_ref[...]
    rows = x_ref.shape[0]
    tm = tri_ref.shape[0]
    n_parts = OUTPROJ_PARTS * rows // tm
    parts = [slice(n * rows // n_parts, (n + 1) * rows // n_parts) for n in range(n_parts)]
    mix = [_dot(jnp.concatenate([ret_ref[r, :], diff_ref[r, :]], axis=1), wo_ref[...]) for r in parts]
    for r, m in zip(parts, mix):
        x1_ref[r, :] = x_ref[r, :] + gate1 * m
    h_split = [_split_bf16(_norm_modulate(x1_ref[r, :], gain_ref[...], shift, scale)) for r in parts]
    for r, (h_hi, _) in zip(parts, h_split):
        h2_ref[r, :] = h_hi
    by_hi = [_dot_nt(wr, h_hi) for h_hi, _ in h_split]
    by_lo = [_dot_nt(wr[:ROUTER_ROWS], h_lo) for _, h_lo in h_split]
    logits = [a[:ROUTER_ROWS] + a[ROUTER_ROWS:] + c + br_ref[...] for a, c in zip(by_hi, by_lo)]
    routed = [_route(lg) for lg in logits]
    g_idx, i1, i2, w1, w2 = [jnp.concatenate([rt[n] for rt in routed], axis=1) for n in range(5)]
    logits = jnp.concatenate(logits, axis=1)
    e1 = g_idx * EXPERTS_PER_GROUP + i1
    e2 = g_idx * EXPERTS_PER_GROUP + i2
    eidx = lax.broadcasted_iota(jnp.int32, (N_EXPERTS, logits.shape[1]), 0)
    hit1 = eidx == e1
    hit2 = eidx == e2
    onehot = jnp.where(hit1 | hit2, 1.0, 0.0)
    tiles = [slice(n * tm, (n + 1) * tm) for n in range(rows // tm)]
    before = jnp.concatenate([_dot(onehot[:, t].astype(BF16), tri_ref[...]) for t in tiles], axis=1)
    r1 = jnp.sum(jnp.where(hit1, before, 0.0), axis=0, keepdims=True)
    r2 = jnp.sum(jnp.where(hit2, before, 0.0), axis=0, keepdims=True)
    zi = jnp.zeros_like(e1)
    ri_ref[...] = jnp.concatenate([e1, e2, r1.astype(jnp.int32), r2.astype(jnp.int32), zi, zi, zi, zi], axis=0)
    zf = jnp.zeros_like(w1)
    rw_ref[...] = jnp.concatenate([w1, w2, zf, zf, zf, zf, zf, zf], axis=0)
    for n, t in enumerate(tiles):
        counts = jnp.sum(onehot[:, t], axis=1, keepdims=True)
        cnt_ref[n] = jnp.broadcast_to(counts, (N_EXPERTS, LANES)).astype(jnp.int32)


def _outproj(ret_out, diff_out, x2, mod, gain, w_out, wr, br, seq, tm):
    tokens = x2.shape[0]
    n_tiles = tokens // tm
    per_step = OUTPROJ_TILES_PER_STEP if seq % (OUTPROJ_TILES_PER_STEP * tm) == 0 else 1
    rows = per_step * tm
    tri = jnp.asarray(np.arange(tm)[:, None] < np.arange(tm)[None, :], BF16)
    tok_spec = lambda w: pl.BlockSpec((rows, w), lambda i: (i, 0))
    row_spec = pl.BlockSpec((8, rows), lambda i: (0, i))
    full = lambda a: pl.BlockSpec(a.shape, lambda i: (0,) * a.ndim)
    return pl.pallas_call(
        functools.partial(_outproj_kernel, tiles_per_batch=seq // rows),
        grid=(tokens // rows,),
        in_specs=[tok_spec(RET_WIDTH), tok_spec(DIFF_WIDTH), tok_spec(D_MODEL), full(mod), full(gain),
                  full(w_out), full(wr), full(br), full(tri)],
        out_specs=[tok_spec(D_MODEL), tok_spec(D_MODEL), row_spec, row_spec,
                   pl.BlockSpec((per_step, N_EXPERTS, LANES), lambda i: (i, 0, 0))],
        out_shape=[jax.ShapeDtypeStruct((tokens, D_MODEL), F32),
                   jax.ShapeDtypeStruct((tokens, D_MODEL), BF16),
                   jax.ShapeDtypeStruct((8, tokens), jnp.int32),
                   jax.ShapeDtypeStruct((8, tokens), F32),
                   jax.ShapeDtypeStruct((n_tiles, N_EXPERTS, LANES), jnp.int32)],
        compiler_params=pltpu.CompilerParams(
            dimension_semantics=("parallel",), vmem_limit_bytes=VMEM_LIMIT),
        name="outproj",
    )(ret_out, diff_out, x2, mod, gain, w_out, wr, br, tri)


CHUNK = 8
BIG_PIECE = 4 * CHUNK
MAX_SMALL_PIECES = N_EXPERTS * (BIG_PIECE // CHUNK - 1)
PIECE_TABLES = ("n_big", "big_src", "big_dst", "n_small", "small_src", "small_dst")
TMX = 512


def _local_rows(tm):
    rows = 2 * tm + N_EXPERTS * (CHUNK - 1)
    return (rows + 15) // 16 * 16


def _max_big_pieces(tm):
    return _local_rows(tm) // BIG_PIECE


def _sorted_rows_alloc(tokens, tm):
    worst = 2 * tokens + (tokens // tm) * N_EXPERTS * (CHUNK - 1) + N_EXPERTS * (TMX - CHUNK)
    return (worst + TMX - 1) // TMX * TMX


def _dispatch_plan(cnt, tokens, tm):
    i32 = jnp.int32
    pad = (cnt + CHUNK - 1) // CHUNK * CHUNK
    local_end = jnp.cumsum(pad, axis=1)
    local_start = local_end - pad
    seg_rows = jnp.sum(pad, axis=0)
    seg_pad = (seg_rows + TMX - 1) // TMX * TMX
    seg_end = jnp.cumsum(seg_pad)
    seg_start = seg_end - seg_pad
    run_dst = seg_start[None, :] + jnp.cumsum(pad, axis=0) - pad

    def pieces(count, offset, size, max_n):
        end = jnp.cumsum(count, axis=1)
        start = end - count
        k = jnp.arange(max_n, dtype=i32)[None, :, None]
        owns = (start[:, None, :] <= k) & (k < end[:, None, :])
        within = size * (k - start[:, None, :]) + offset[:, None, :]
        src = jnp.sum(jnp.where(owns, local_start[:, None, :] + within, 0), axis=-1)
        dst = jnp.sum(jnp.where(owns, run_dst[:, None, :] + within, 0), axis=-1)
        return end[:, -1].astype(i32), src.reshape(-1).astype(i32), dst.reshape(-1).astype(i32)

    n_big, big_src, big_dst = pieces(pad // BIG_PIECE, jnp.zeros_like(pad), BIG_PIECE, _max_big_pieces(tm))
    n_small, small_src, small_dst = pieces(pad % BIG_PIECE // CHUNK, pad // BIG_PIECE * BIG_PIECE, CHUNK,
                                           MAX_SMALL_PIECES)
    m = TMX * jnp.arange(_sorted_rows_alloc(tokens, tm) // TMX, dtype=i32)
    tile_expert = jnp.minimum(jnp.sum(seg_end[None, :] <= m[:, None], axis=-1), N_EXPERTS - 1)
    towns = (seg_start[None, :] <= m[:, None]) & (m[:, None] < seg_end[None, :])
    used = seg_pad > 0
    parity = (jnp.cumsum(used) - used) % 2
    eids = jnp.arange(N_EXPERTS, dtype=i32)
    later_used = (eids[None, :] > eids[:, None]) & used[None, :]
    next_used = jnp.min(jnp.where(later_used, eids[None, :], N_EXPERTS), axis=1)
    next_used = jnp.where(next_used == N_EXPERTS, -1, next_used)
    pick = lambda per_expert: jnp.sum(jnp.where(towns, per_expert[None, :], 0), axis=-1)
    tile_first = jnp.sum(jnp.where(towns & (seg_start[None, :] == m[:, None]), 1, 0), axis=-1)
    tile_next = jnp.where(jnp.any(towns, axis=-1), pick(next_used), -1)
    return dict(
        tile_first=tile_first.astype(i32),
        tile_slot=pick(parity).astype(i32),
        tile_next=tile_next.astype(i32),
        tile_rows=jnp.clip(pick(seg_start + seg_rows) - m, 0, TMX).astype(i32),
        local_start=local_start.reshape(-1).astype(i32),
        n_big=n_big, big_src=big_src, big_dst=big_dst,
        n_small=n_small, small_src=small_src, small_dst=small_dst,
        tail_base=(seg_start + seg_rows).astype(i32),
        tail_rows=(seg_pad - seg_rows).astype(i32),
        tile_expert=tile_expert.astype(i32),
        n_used=(seg_end[-1:] // TMX).astype(i32),
    )


WAIT_UNROLL = 8


def _wait_times(copy, n):
    lax.fori_loop(0, n // WAIT_UNROLL, lambda i, c: ([copy.wait() for _ in range(WAIT_UNROLL)], c)[1], 0)
    lax.fori_loop(0, n % WAIT_UNROLL, lambda i, c: (copy.wait(), c)[1], 0)


def _for_each(n, body, unroll=4):
    main = n // unroll
    lax.fori_loop(0, main, lambda i, c: ([body(i * unroll + u) for u in range(unroll)], c)[1], 0)
    lax.fori_loop(main * unroll, n, lambda j, c: (body(j), c)[1], 0)


def _local_slots(ri_ref, local_start_ref, tile):
    e1, e2 = ri_ref[0:1, :], ri_ref[1:2, :]
    s1, s2 = ri_ref[2:3, :], ri_ref[3:4, :]
    for e in range(N_EXPERTS):
        start = local_start_ref[tile * N_EXPERTS + e]
        s1 = s1 + jnp.where(e1 == e, start, 0)
        s2 = s2 + jnp.where(e2 == e, start, 0)
    return s1, s2


def _run_pieces(piece_refs, local_ref, sorted_ref, sem_ref, to_sorted):
    n_big_ref, big_src_ref, big_dst_ref, n_small_ref, small_src_ref, small_dst_ref = piece_refs
    r_loc = local_ref.shape[1]
    kinds = [(BIG_PIECE, n_big_ref, big_src_ref, big_dst_ref, r_loc // BIG_PIECE),
             (CHUNK, n_small_ref, small_src_ref, small_dst_ref, MAX_SMALL_PIECES)]

    def copy(sl, size, local_row, sorted_row):
        local = local_ref.at[sl, pl.ds(pl.multiple_of(local_row, CHUNK), size), :]
        srt = sorted_ref.at[pl.ds(pl.multiple_of(sorted_row, CHUNK), size), :]
        return pltpu.make_async_copy(local, srt, sem_ref.at[sl]) if to_sorted else \
            pltpu.make_async_copy(srt, local, sem_ref.at[sl])

    def start(tile, sl):
        for size, n_ref, src_ref, dst_ref, max_n in kinds:
            _for_each(n_ref[tile], lambda k: copy(sl, size, src_ref[tile * max_n + k], dst_ref[tile * max_n + k])
                      .start())

    def wait(tile, sl):
        for size, n_ref, _, _, _ in kinds:
            _wait_times(copy(sl, size, 0, 0), n_ref[tile])

    return start, wait


def _dispatch_kernel(local_start_ref, n_big_ref, big_src_ref, big_dst_ref, n_small_ref, small_src_ref, small_dst_ref,
                     tail_base_ref, tail_rows_ref, n_used_ref,
                     h_ref, ri_ref, xs_ref, buf_ref, zero_ref, sem_ref, tail_sem_ref, *, r_loc):
    b = pl.program_id(0)
    nb = pl.num_programs(0)
    slot = b % 2
    start_runs, drain = _run_pieces(
        (n_big_ref, big_src_ref, big_dst_ref, n_small_ref, small_src_ref, small_dst_ref), buf_ref, xs_ref, sem_ref,
        to_sorted=True)

    @pl.when(b >= 2)
    def _():
        drain(b - 2, slot)

    s1, s2 = _local_slots(ri_ref, local_start_ref, b)
    rows = lax.broadcasted_iota(jnp.int32, (r_loc, s1.shape[1]), 0)
    perm = jnp.where((rows == s1) | (rows == s2), 1.0, 0.0).astype(BF16)
    buf_ref[slot] = _dot(perm, h_ref[...])
    start_runs(b, slot)

    def tail_pieces(e, act):
        n = tail_rows_ref[e]
        size = TMX // 2
        while size >= CHUNK:
            dst = pl.multiple_of(tail_base_ref[e] + (n & (-2 * size)), CHUNK)
            cp = pltpu.make_async_copy(zero_ref.at[pl.ds(0, size), :], xs_ref.at[pl.ds(dst, size), :],
                                       tail_sem_ref.at[0])
            pl.when((n & size) != 0)(functools.partial(act, cp))
            size //= 2

    def unused_tile_copy(m):
        dst = pl.multiple_of(m * TMX, TMX)
        return pltpu.make_async_copy(zero_ref, xs_ref.at[pl.ds(dst, TMX), :], tail_sem_ref.at[1])

    n_alloc = xs_ref.shape[0] // TMX

    @pl.when(b == 0)
    def _():
        zero_ref[...] = jnp.zeros_like(zero_ref)
        lax.fori_loop(0, N_EXPERTS, lambda e, c: (tail_pieces(e, lambda cp: cp.start()), c)[1], 0)
        lax.fori_loop(n_used_ref[0], n_alloc, lambda m, c: (unused_tile_copy(m).start(), c)[1], 0)

    @pl.when(b == nb - 1)
    def _():
        lax.fori_loop(0, N_EXPERTS, lambda e, c: (tail_pieces(e, lambda cp: cp.wait()), c)[1], 0)
        lax.fori_loop(n_used_ref[0], n_alloc, lambda m, c: (unused_tile_copy(m).wait(), c)[1], 0)

        @pl.when(b >= 1)
        def _():
            drain(b - 1, 1 - slot)

        drain(b, slot)


def _dispatch(h2, ri, plan, tm):
    tokens = h2.shape[0]
    r_loc = _local_rows(tm)
    prefetch = [plan["local_start"]] + [plan[k] for k in PIECE_TABLES] + [
        plan["tail_base"], plan["tail_rows"], plan["n_used"]]
    grid_spec = pltpu.PrefetchScalarGridSpec(
        num_scalar_prefetch=len(prefetch),
        grid=(tokens // tm,),
        in_specs=[pl.BlockSpec((tm, D_MODEL), lambda i, *_: (i, 0)),
                  pl.BlockSpec((8, tm), lambda i, *_: (0, i))],
        out_specs=pl.BlockSpec(memory_space=pl.ANY),
        scratch_shapes=[pltpu.VMEM((2, r_loc, D_MODEL), F32), pltpu.VMEM((TMX, D_MODEL), F32),
                        pltpu.SemaphoreType.DMA((2,)), pltpu.SemaphoreType.DMA((2,))],
    )
    return pl.pallas_call(
        functools.partial(_dispatch_kernel, r_loc=r_loc),
        grid_spec=grid_spec,
        out_shape=jax.ShapeDtypeStruct((_sorted_rows_alloc(tokens, tm), D_MODEL), F32),
        compiler_params=pltpu.CompilerParams(
            dimension_semantics=("arbitrary",), vmem_limit_bytes=VMEM_LIMIT),
        name="dispatch",
    )(*prefetch, h2, ri)


def _experts_kernel(tile_expert_ref, n_used_ref, first_ref, slot_ref, next_ref, rows_ref, xs_ref, wg_hbm, wu_hbm, wd_hbm,
                    ys_ref, wg_st, wu_st, wd_st, wg_bf, wu_bf, wd_bf, a_ref, u_ref, sem_ref):
    m = pl.program_id(0)

    def weight_copies(e, s):
        return [pltpu.make_async_copy(src.at[e], dst.at[s], sem_ref.at[s, n])
                for n, (src, dst) in enumerate([(wg_hbm, wg_st), (wu_hbm, wu_st), (wd_hbm, wd_st)])]

    @pl.when(m < n_used_ref[0])
    def _():
        @pl.when(first_ref[m] == 1)
        def _():
            s = slot_ref[m]

            @pl.when(m == 0)
            def _():
                for cp in weight_copies(tile_expert_ref[0], 0):
                    cp.start()

            for cp in weight_copies(tile_expert_ref[m], s):
                cp.wait()

            @pl.when(next_ref[m] >= 0)
            def _():
                for cp in weight_copies(next_ref[m], 1 - s):
                    cp.start()

            wg_bf[...] = wg_st[s].astype(BF16)
            wu_bf[...] = wu_st[s].astype(BF16)
            wd_bf[...] = wd_st[s].astype(BF16)

        def mlp(rows):
            x = xs_ref[rows, :].astype(BF16)
            a_ref[rows, :] = _dot(x, wg_bf[...])
            u_ref[rows, :] = _dot(x, wu_bf[...])
            hid = (_silu(a_ref[rows, :]) * u_ref[rows, :]).astype(BF16)
            ys_ref[rows, :] = _dot(hid, wd_bf[...])

        half = TMX // 2

        @pl.when(rows_ref[m] > half)
        def _():
            mlp(slice(0, TMX))

        @pl.when(rows_ref[m] <= half)
        def _():
            mlp(slice(0, half))
            ys_ref[half:, :] = jnp.zeros((TMX - half, D_MODEL), F32)


def _experts(xs, plan, wg, wu, wd):
    n_tiles = xs.shape[0] // TMX
    last_used = lambda m, n_used: jnp.minimum(m, n_used[0] - 1)
    row_spec = pl.BlockSpec((TMX, D_MODEL), lambda m, te, nu, *_: (last_used(m, nu), 0))
    hbm = pl.BlockSpec(memory_space=pl.ANY)
    up_shape, down_shape = (D_MODEL, D_EXPERT), (D_EXPERT, D_MODEL)
    grid_spec = pltpu.PrefetchScalarGridSpec(
        num_scalar_prefetch=6,
        grid=(n_tiles,),
        in_specs=[row_spec, hbm, hbm, hbm],
        out_specs=row_spec,
        scratch_shapes=[pltpu.VMEM((2,) + up_shape, F32), pltpu.VMEM((2,) + up_shape, F32),
                        pltpu.VMEM((2,) + down_shape, F32),
                        pltpu.VMEM(up_shape, BF16), pltpu.VMEM(up_shape, BF16), pltpu.VMEM(down_shape, BF16),
                        pltpu.VMEM((TMX, D_EXPERT), F32), pltpu.VMEM((TMX, D_EXPERT), F32),
                        pltpu.SemaphoreType.DMA((2, 3))],
    )
    return pl.pallas_call(
        _experts_kernel,
        grid_spec=grid_spec,
        out_shape=jax.ShapeDtypeStruct(xs.shape, F32),
        input_output_aliases={6: 0},
        compiler_params=pltpu.CompilerParams(
            dimension_semantics=("arbitrary",), vmem_limit_bytes=VMEM_LIMIT),
        name="experts",
    )(plan["tile_expert"], plan["n_used"], plan["tile_first"], plan["tile_slot"], plan["tile_next"], plan["tile_rows"],
      xs, wg, wu, wd)


def _combine_kernel(local_start_ref, n_big_ref, big_src_ref, big_dst_ref, n_small_ref, small_src_ref, small_dst_ref,
                    ys_ref, ri_ref, rw_ref, x1_ref, mod_ref, gain_ref, o_ref, buf_ref, sem_ref,
                    *, r_loc, tiles_per_batch):
    b = pl.program_id(0)
    nb = pl.num_programs(0)
    slot = b % 2
    fetch, wait_runs = _run_pieces(
        (n_big_ref, big_src_ref, big_dst_ref, n_small_ref, small_src_ref, small_dst_ref), buf_ref, ys_ref, sem_ref,
        to_sorted=False)

    @pl.when(b == 0)
    def _():
        buf_ref[...] = jnp.zeros_like(buf_ref)
        fetch(0, 0)

    @pl.when(b + 1 < nb)
    def _():
        fetch(b + 1, 1 - slot)

    wait_runs(b, slot)

    s1, s2 = _local_slots(ri_ref, local_start_ref, b)
    rows = lax.broadcasted_iota(jnp.int32, (r_loc, s1.shape[1]), 0)
    hit1 = rows == s1
    hit2 = rows == s2
    w_row = jnp.sum(jnp.where(hit1, rw_ref[0:1, :], jnp.where(hit2, rw_ref[1:2, :], 0.0)), axis=1, keepdims=True)
    perm = jnp.where(hit1 | hit2, 1.0, 0.0).astype(BF16)
    yw = (buf_ref[slot] * w_row).astype(BF16)
    moe = _dot_tn(perm, yw)
    batch = b // tiles_per_batch
    gate2 = mod_ref[pl.ds(batch, 1), 5 * D_MODEL:6 * D_MODEL]
    x2 = x1_ref[...] + gate2 * moe
    ms = jnp.mean(x2 * x2, axis=-1, keepdims=True)
    o_ref[...] = x2 * lax.rsqrt(ms + EPS) * gain_ref[...]


def _combine(ys, ri, rw, x1, mod, gain, plan, seq, tm):
    tokens = x1.shape[0]
    r_loc = _local_rows(tm)
    row_spec = pl.BlockSpec((8, tm), lambda i, *_: (0, i))
    tok_spec = pl.BlockSpec((tm, D_MODEL), lambda i, *_: (i, 0))
    full = lambda a: pl.BlockSpec(a.shape, lambda i, *_: (0,) * a.ndim)
    prefetch = [plan["local_start"]] + [plan[k] for k in PIECE_TABLES]
    grid_spec = pltpu.PrefetchScalarGridSpec(
        num_scalar_prefetch=len(prefetch),
        grid=(tokens // tm,),
        in_specs=[pl.BlockSpec(memory_space=pl.ANY), row_spec, row_spec, tok_spec, full(mod), full(gain)],
        out_specs=tok_spec,
        scratch_shapes=[pltpu.VMEM((2, r_loc, D_MODEL), F32), pltpu.SemaphoreType.DMA((2,))],
    )
    return pl.pallas_call(
        functools.partial(_combine_kernel, r_loc=r_loc, tiles_per_batch=seq // tm),
        grid_spec=grid_spec,
        out_shape=jax.ShapeDtypeStruct((tokens, D_MODEL), F32),
        compiler_params=pltpu.CompilerParams(
            dimension_semantics=("arbitrary",), vmem_limit_bytes=VMEM_LIMIT),
        name="combine",
    )(*prefetch, ys, ri, rw, x1, mod, gain)


def _rotary_tables(seq):
    half = RET_HEAD_DIM // 2
    inv_freq = 1.0 / (ROPE_BASE ** (np.arange(half, dtype=np.float64) / half))
    ang = np.arange(seq, dtype=np.float64)[:, None] * inv_freq[None, :]
    cos = np.cos(ang)
    sin = np.sin(ang)
    f32 = lambda a: jnp.asarray(a.astype(np.float32))
    return f32(np.tile(cos, (1, 4))), f32(np.concatenate([-sin, sin, -sin, sin], axis=1))


def _pick_tile(n, pref):
    t = min(n, pref)
    assert n % t == 0, (n, t)
    return t


def kernel(x, c, ada_w, ada_b, norm1_gain, norm2_gain, w_in, w_out, ret_gn_gain, lam_q1, lam_k1, lam_q2,
           lam_k2, diff_subln_gain, w_group, b_group, w_expert, b_expert, w_gate, w_up, w_down, final_gain):
    batch, seq, d = x.shape
    assert d == D_MODEL and batch <= 8 and ada_w.shape[0] == 1
    layer = 0
    lambda_init = 0.8 - 0.6 * math.exp(-0.3 * layer)
    tokens = batch * seq
    x2 = x.reshape(tokens, d)
    tm = _pick_tile(seq, 512)

    c_pad = jnp.zeros((8, d), F32).at[:batch].set(c)
    mod = _adaln(c_pad, ada_w[layer], ada_b[layer].reshape(1, -1))

    cos_t, sin_t = _rotary_tables(seq)
    rq, rk, rv, rg, dq, dk, dvt = _inproj(
        x2, mod, norm1_gain[layer].reshape(1, d), w_in[layer].astype(BF16), cos_t, sin_t, seq, tm)

    ret_out = _retention(rq, rk, rv, rg, ret_gn_gain[layer].reshape(1, RET_WIDTH), batch, seq,
                         _pick_tile(seq, 256))
    diff_out = _diffattn(
        dq, dk, dvt, lam_q1[layer].reshape(1, -1), lam_k1[layer].reshape(1, -1), lam_q2[layer].reshape(1, -1),
        lam_k2[layer].reshape(1, -1), diff_subln_gain[layer].reshape(-1, 1), batch, seq, lambda_init,
        _pick_tile(seq, 2048), 2 * QUERY_CHUNK)

    w_router = jnp.concatenate(
        [w_group[layer].T, jnp.zeros((8 - N_GROUPS, d), F32), w_expert[layer].reshape(d, N_EXPERTS).T], axis=0)
    b_router = jnp.concatenate(
        [b_group[layer], jnp.zeros((8 - N_GROUPS,), F32), b_expert[layer].reshape(N_EXPERTS)]).reshape(-1, 1)
    wr_hi = w_router.astype(BF16)
    wr_lo = (w_router - wr_hi.astype(F32)).astype(BF16)
    x1, h2, ri, rw, cnt = _outproj(ret_out, diff_out, x2, mod, norm2_gain[layer].reshape(1, d),
                                   w_out[layer].astype(BF16), jnp.concatenate([wr_hi, wr_lo], axis=0), b_router,
                                   seq, tm)

    plan = _dispatch_plan(cnt[:, :, 0], tokens, tm)
    xs = _dispatch(h2, ri, plan, tm)
    ys = _experts(xs, plan, w_gate[layer].reshape(N_EXPERTS, d, D_EXPERT),
                  w_up[layer].reshape(N_EXPERTS, d, D_EXPERT), w_down[layer].reshape(N_EXPERTS, D_EXPERT, d))
    out = _combine(ys, ri, rw, x1, mod, final_gain.reshape(1, d), plan, seq, tm)
    return out.reshape(batch, seq, d)
```

```python
import functools
import math

import jax
import jax.numpy as jnp
import numpy as np
from jax import lax
from jax.experimental import pallas as pl
from jax.experimental.pallas import tpu as pltpu

F32 = jnp.float32
BF16 = jnp.bfloat16

D_MODEL = 1024
RET_HEAD_DIM = 64
RET_WIDTH = 512
RET_HEADS = 8
RET_PAIRS = RET_HEADS // 2
DIFF_QK_DIM = 64
DIFF_V_DIM = 128
DIFF_HEADS = 4
DIFF_WIDTH = 512
N_GROUPS = 4
EXPERTS_PER_GROUP = 8
N_EXPERTS = N_GROUPS * EXPERTS_PER_GROUP
D_EXPERT = 512
N_MOD = 6
ROPE_BASE = 10000.0
EPS = 1e-6
LANES = 128
ROUTER_ROWS = 8 + N_EXPERTS
VMEM_LIMIT = 56 * 1024 * 1024


def _dot(a, b):
    return jnp.dot(a, b, preferred_element_type=F32)


def _dot_nt(a, b):
    return lax.dot_general(a, b, (((1,), (1,)), ((), ())), preferred_element_type=F32)


def _dot_tn(a, b):
    return lax.dot_general(a, b, (((0,), (0,)), ((), ())), preferred_element_type=F32)


def _split_bf16(x):
    hi = x.astype(BF16)
    lo = (x - hi.astype(F32)).astype(BF16)
    return hi, lo


def _silu(x):
    return x / (1.0 + jnp.exp(-x))


def _adaln_kernel(c_ref, w_ref, b_ref, o_ref):
    ca = _silu(c_ref[...])
    c_hi, c_lo = _split_bf16(ca)
    w_hi, w_lo = _split_bf16(w_ref[...])
    o_ref[...] = _dot(c_hi, w_hi) + _dot(c_lo, w_hi) + _dot(c_hi, w_lo) + b_ref[...]


def _adaln(c_pad, ada_w, ada_b):
    n_out = ada_w.shape[1]
    tn = D_MODEL
    return pl.pallas_call(
        _adaln_kernel,
        grid=(n_out // tn,),
        in_specs=[
            pl.BlockSpec((8, D_MODEL), lambda j: (0, 0)),
            pl.BlockSpec((D_MODEL, tn), lambda j: (0, j)),
            pl.BlockSpec((1, tn), lambda j: (0, j)),
        ],
        out_specs=pl.BlockSpec((8, tn), lambda j: (0, j)),
        out_shape=jax.ShapeDtypeStruct((8, n_out), F32),
        compiler_params=pltpu.CompilerParams(vmem_limit_bytes=VMEM_LIMIT),
        name="adaln",
    )(c_pad, ada_w, ada_b)


def _norm_modulate(x, gain, shift, scale):
    ms = jnp.mean(x * x, axis=-1, keepdims=True)
    y = x * lax.rsqrt(ms + EPS) * gain
    return y * (1.0 + scale) + shift


def _rotary_slab(x, cos, sin_signed, lane_lo):
    swapped = jnp.where(lane_lo, pltpu.roll(x, 96, 1), pltpu.roll(x, 32, 1))
    return x * cos + swapped * sin_signed


def _inproj_kernel(x_ref, mod_ref, gain_ref, w_ref, cos_ref, sin_ref,
                   rq_ref, rk_ref, rv_ref, rg_ref, dq_ref, dk_ref, dvt_ref, *, tiles_per_batch):
    b = pl.program_id(0) // tiles_per_batch
    shift = mod_ref[pl.ds(b, 1), 0:D_MODEL]
    scale = mod_ref[pl.ds(b, 1), D_MODEL:2 * D_MODEL]
    h = _norm_modulate(x_ref[...], gain_ref[...], shift, scale).astype(BF16)
    cos = cos_ref[...]
    sin = sin_ref[...]
    lane = lax.broadcasted_iota(jnp.int32, cos.shape, 1)
    lane_lo = (lane % 64) < 32

    def proj(chunk):
        return _dot(h, w_ref[:, chunk * RET_WIDTH:(chunk + 1) * RET_WIDTH])

    def rotary(acc, out_ref, post_scale):
        for s in range(RET_WIDTH // LANES):
            sl = slice(s * LANES, (s + 1) * LANES)
            out_ref[:, sl] = (_rotary_slab(acc[:, sl], cos, sin, lane_lo) * post_scale).astype(BF16)

    rotary(proj(0), rq_ref, 1.0)
    rotary(proj(1), rk_ref, RET_HEAD_DIM ** -0.5)
    rv_ref[...] = proj(2).astype(BF16)
    rg_ref[...] = _silu(proj(3)).astype(BF16)
    dq_ref[...] = (proj(4) * (DIFF_QK_DIM ** -0.5 * math.log2(math.e))).astype(BF16)
    dk_ref[...] = proj(5).astype(BF16)
    dvt_ref[...] = proj(6).T.astype(BF16)


def _inproj(x2, mod, gain, w_in, cos_t, sin_t, seq, tm):
    tokens = x2.shape[0]
    tiles_per_batch = seq // tm
    tok_spec = lambda w: pl.BlockSpec((tm, w), lambda i: (i, 0))
    tab_spec = pl.BlockSpec((tm, LANES), lambda i: (i % tiles_per_batch, 0))
    full = lambda a: pl.BlockSpec(a.shape, lambda i: (0,) * a.ndim)
    out = jax.ShapeDtypeStruct((tokens, RET_WIDTH), BF16)
    return pl.pallas_call(
        functools.partial(_inproj_kernel, tiles_per_batch=tiles_per_batch),
        grid=(tokens // tm,),
        in_specs=[tok_spec(D_MODEL), full(mod), full(gain), full(w_in), tab_spec, tab_spec],
        out_specs=[tok_spec(RET_WIDTH)] * 6 + [pl.BlockSpec((DIFF_WIDTH, tm), lambda i: (0, i))],
        out_shape=[out] * 6 + [jax.ShapeDtypeStruct((DIFF_WIDTH, tokens), BF16)],
        compiler_params=pltpu.CompilerParams(
            dimension_semantics=("parallel",), vmem_limit_bytes=VMEM_LIMIT),
        name="inproj",
    )(x2, mod, gain, w_in, cos_t, sin_t)


RET_SEQ_GROUP = 4


def _retention_kernel(q_ref, k_ref, v_ref, g_ref, dec_ref, qdec_ref, kdec_ref, rdec_ref,
                      bmask_ref, gmean_ref, gain_ref, o_ref, state_ref, *, chunk):
    @pl.when(pl.program_id(1) == 0)
    def _():
        state_ref[...] = jnp.zeros_like(state_ref)

    lane = lax.broadcasted_iota(jnp.int32, (chunk, LANES), 1)
    first_head = lane < RET_HEAD_DIM
    gmean = gmean_ref[...]
    bmask = bmask_ref[...]
    units = [(s, p) for s in range(q_ref.shape[0]) for p in range(RET_PAIRS)]
    sl = lambda p: slice(p * LANES, (p + 1) * LANES)
    q = [q_ref[s, :, sl(p)] for s, p in units]
    k = [k_ref[s, :, sl(p)] for s, p in units]
    v = [v_ref[s, :, sl(p)] for s, p in units]
    zero = jnp.zeros_like(q[0])
    q_stack = [jnp.concatenate([jnp.where(first_head, qu, zero), jnp.where(first_head, zero, qu)], axis=0)
               for qu in q]
    scores = [(_dot_nt(q_stack[n], k[n]) * dec_ref[p]).astype(BF16) for n, (_, p) in enumerate(units)]
    state = [state_ref[n] for n in range(len(units))]
    cross = [_dot(q[n], state[n].astype(BF16)) * qdec_ref[:, sl(p)] for n, (_, p) in enumerate(units)]
    k_dec = [(k[n].astype(F32) * kdec_ref[:, sl(p)]).astype(BF16) for n, (_, p) in enumerate(units)]
    for n, (_, p) in enumerate(units):
        state_ref[n] = state[n] * rdec_ref[p] + _dot_tn(k_dec[n], v[n]) * bmask
    intra2 = [_dot(scores[n], v[n]) for n in range(len(units))]
    y = [jnp.where(first_head, intra2[n][:chunk], intra2[n][chunk:]) + cross[n] for n in range(len(units))]
    seg_mean = lambda x: _dot(jnp.concatenate(_split_bf16(x), axis=1), gmean)
    mu = [seg_mean(yu) for yu in y]
    d = [yu - mu_u for yu, mu_u in zip(y, mu)]
    var = [seg_mean(du * du) for du in d]
    for n, (s, p) in enumerate(units):
        yn = d[n] * lax.rsqrt(var[n] + EPS) * gain_ref[:, sl(p)]
        o_ref[s, :, sl(p)] = (g_ref[s, :, sl(p)].astype(F32) * yn).astype(BF16)


def _retention_tables(chunk):
    heads = np.arange(RET_HEADS, dtype=np.float64)
    log_gamma = np.log(1.0 - np.exp2(-5.0 - heads))
    idx = np.arange(chunk)
    rel = (idx[:, None] - idx[None, :]).astype(np.float64)
    decay = np.where(rel[None] >= 0, np.exp(log_gamma[:, None, None] * np.maximum(rel, 0.0)[None]), 0.0)
    dec2 = decay.reshape(RET_PAIRS, 2 * chunk, chunk)
    lane_lg = np.repeat(log_gamma, RET_HEAD_DIM)
    qdec = np.exp(lane_lg[None, :] * (idx + 1)[:, None])
    kdec = np.exp(lane_lg[None, :] * (chunk - 1 - idx)[:, None])
    rdec = np.exp(lane_lg * chunk).reshape(RET_PAIRS, LANES, 1) * np.ones((1, 1, LANES))
    blk = np.arange(LANES) // RET_HEAD_DIM
    bmask = (blk[:, None] == blk[None, :]).astype(np.float64)
    f32 = lambda a: jnp.asarray(a.astype(np.float32))
    gmean2 = np.concatenate([bmask, bmask], axis=0) / RET_HEAD_DIM
    return f32(dec2), f32(qdec), f32(kdec), f32(rdec), f32(bmask), f32(gmean2).astype(BF16)


def _retention(rq, rk, rv, rg, gn_gain, batch, seq, chunk):
    nc = seq // chunk
    group = RET_SEQ_GROUP if batch % RET_SEQ_GROUP == 0 else 1
    dec2, qdec, kdec, rdec, bmask, gmean = _retention_tables(chunk)
    tok_spec = pl.BlockSpec((group, chunk, RET_WIDTH), lambda b, n: (b, n, 0))
    full = lambda a: pl.BlockSpec(a.shape, lambda b, n: (0,) * a.ndim)
    by_seq = lambda a: a.reshape(batch, seq, RET_WIDTH)
    out = pl.pallas_call(
        functools.partial(_retention_kernel, chunk=chunk),
        grid=(batch // group, nc),
        in_specs=[tok_spec] * 4 + [full(dec2), full(qdec), full(kdec), full(rdec), full(bmask),
                                   full(gmean), full(gn_gain)],
        out_specs=tok_spec,
        out_shape=jax.ShapeDtypeStruct((batch, seq, RET_WIDTH), BF16),
        scratch_shapes=[pltpu.VMEM((group * RET_PAIRS, LANES, LANES), F32)],
        compiler_params=pltpu.CompilerParams(
            dimension_semantics=("parallel", "arbitrary"), vmem_limit_bytes=VMEM_LIMIT),
        name="retention",
    )(by_seq(rq), by_seq(rk), by_seq(rv), by_seq(rg), dec2, qdec, kdec, rdec, bmask, gmean, gn_gain)
    return out.reshape(batch * seq, RET_WIDTH)


NEG_BIG = -1e30


V_EXT_ROWS = DIFF_V_DIM + 16
QUERY_CHUNK = 256
SCORES_AHEAD_FULL = 3
SCORES_AHEAD_DIAG = 3
STAGE_SLOTS = 16


def _diag_chunks(tq, tk, d):
    assert tk == 2 * QUERY_CHUNK
    per_softmax = tq // QUERY_CHUNK
    out = []
    for c in range(2 * per_softmax):
        q0 = (c % per_softmax) * QUERY_CHUNK
        if q0 + QUERY_CHUNK - 1 < d * tk:
            continue
        kind = "full" if q0 >= (d + 1) * tk else ("tri" if q0 == d * tk else "low_tri")
        out.append((c, kind))
    return out


def _accumulate(acc_ref, cs, alpha, pv):
    acc_ref[:, cs] = alpha * acc_ref[:, cs] + pv


def _diffattn_kernel(q_ref, k_ref, vt_ref, lq1_ref, lk1_ref, lq2_ref, lk2_ref, gain_ref, bias_ref, o_ref,
                     qs_ref, vext_ref, m_ref, acc_ref, stage_ref, *, tq, tk, lambda_init):
    i = pl.program_id(2)
    nk = vext_ref.shape[0]

    @pl.when(i == 0)
    def _():
        for j in range(nk):
            vext_ref[j, 0:DIFF_V_DIM, :] = vt_ref[:, j * tk:(j + 1) * tk]
            vext_ref[j, DIFF_V_DIM:V_EXT_ROWS, :] = jnp.ones((V_EXT_ROWS - DIFF_V_DIM, tk), BF16)

    q = q_ref[...]
    lane = lax.broadcasted_iota(jnp.int32, q.shape, 1)
    zero = jnp.zeros_like(q)
    qs_ref[0:tq, :] = jnp.where(lane < DIFF_QK_DIM, q, zero)
    qs_ref[tq:2 * tq, :] = jnp.where(lane < DIFF_QK_DIM, zero, q)
    m_ref[...] = jnp.full_like(m_ref, NEG_BIG)
    acc_ref[...] = jnp.zeros_like(acc_ref)

    def step(work, n_ahead):
        chunk = lambda c: slice(c * QUERY_CHUNK, (c + 1) * QUERY_CHUNK)

        def scores(j, c, kind):
            n_keys = QUERY_CHUNK if kind == "tri" else tk
            start = pl.multiple_of(j * tk, tk)
            return _dot_nt(k_ref[pl.ds(start, n_keys), :], qs_ref[chunk(c), :])

        ahead = [scores(*work[n]) for n in range(min(n_ahead, len(work)))]
        pending = None
        for n, (j, c, kind) in enumerate(work):
            cs = chunk(c)
            st = ahead.pop(0)
            if n + n_ahead < len(work):
                ahead.append(scores(*work[n + n_ahead]))
            slot = n % stage_ref.shape[0]
            n_keys = st.shape[0]
            stage_ref[slot, 0:n_keys, :] = st
            if kind == "full":
                st = stage_ref[slot]
            else:
                causal = stage_ref[slot, n_keys - QUERY_CHUNK:n_keys, :] + bias_ref[...]
                st = causal if kind == "tri" else jnp.concatenate(
                    [stage_ref[slot, 0:n_keys - QUERY_CHUNK, :], causal], axis=0)
            m_old = m_ref[:, cs]
            m_new = jnp.maximum(m_old, jnp.max(st, axis=0, keepdims=True))
            alpha = jnp.exp2(m_old - m_new)
            p = jnp.exp2(st - m_new).astype(BF16)
            m_ref[:, cs] = m_new
            pv = _dot(vext_ref[j, :, 0:st.shape[0]], p)
            if pending is not None:
                pending()
            pending = functools.partial(_accumulate, acc_ref, cs, alpha, pv)
        pending()

    tiles_per_q = tq // tk
    n_chunks = 2 * tq // QUERY_CHUNK

    def full_tiles(it):
        return [(it * tiles_per_q + d, c, "full") for d in range(tiles_per_q) for c in range(n_chunks)]

    lax.fori_loop(0, i, lambda it, c: (step(full_tiles(it), SCORES_AHEAD_FULL), c)[1], 0)
    kind_order = {"full": 0, "low_tri": 1, "tri": 2}
    diag = lambda it: [(it * tiles_per_q + d, c, kind) for d in range(tiles_per_q)
                       for c, kind in sorted(_diag_chunks(tq, tk, d), key=lambda ck: kind_order[ck[1]])]
    lax.fori_loop(i, i + 1, lambda it, c: (step(diag(it), SCORES_AHEAD_DIAG), c)[1], 0)

    lam = (jnp.exp(jnp.sum(lq1_ref[...] * lk1_ref[...], axis=-1, keepdims=True))
           - jnp.exp(jnp.sum(lq2_ref[...] * lk2_ref[...], axis=-1, keepdims=True)) + lambda_init)
    acc = acc_ref[...]
    o2 = acc[0:DIFF_V_DIM, :] * (1.0 / acc[DIFF_V_DIM:DIFF_V_DIM + 1, :])
    ot = o2[:, :tq] - lam * o2[:, tq:]
    ms = jnp.mean(ot * ot, axis=0, keepdims=True)
    ot = ot * lax.rsqrt(ms + EPS) * gain_ref[...] * (1.0 - lambda_init)
    o_ref[...] = ot.T.astype(BF16)


def _diffattn(dq, dk, dvt, lam_q1, lam_k1, lam_q2, lam_k2, gain, batch, seq, lambda_init, tq, tk):
    nq = seq // tq
    q_spec = pl.BlockSpec((tq, LANES), lambda b, h, i: (b * nq + i, h))
    k_spec = pl.BlockSpec((seq, LANES), lambda b, h, i: (b, h))
    vt_spec = pl.BlockSpec((DIFF_V_DIM, seq), lambda b, h, i: (h, b))
    vec = lambda a: pl.BlockSpec(a.shape, lambda b, h, i: (0, 0))
    key = np.arange(QUERY_CHUNK)[:, None]
    query = np.arange(QUERY_CHUNK)[None, :]
    bias = jnp.asarray(np.where(key <= query, 0.0, NEG_BIG), F32)
    return pl.pallas_call(
        functools.partial(_diffattn_kernel, tq=tq, tk=tk, lambda_init=lambda_init),
        grid=(batch, DIFF_HEADS, nq),
        in_specs=[q_spec, k_spec, vt_spec, vec(lam_q1), vec(lam_k1), vec(lam_q2), vec(lam_k2), vec(gain),
                  vec(bias)],
        out_specs=q_spec,
        out_shape=jax.ShapeDtypeStruct(dq.shape, BF16),
        scratch_shapes=[
            pltpu.VMEM((2 * tq, LANES), BF16),
            pltpu.VMEM((seq // tk, V_EXT_ROWS, tk), BF16),
            pltpu.VMEM((1, 2 * tq), F32),
            pltpu.VMEM((V_EXT_ROWS, 2 * tq), F32),
            pltpu.VMEM((STAGE_SLOTS, tk, QUERY_CHUNK), F32),
        ],
        compiler_params=pltpu.CompilerParams(
            dimension_semantics=("parallel", "parallel", "arbitrary"), vmem_limit_bytes=VMEM_LIMIT),
        name="diffattn",
    )(dq, dk, dvt, lam_q1, lam_k1, lam_q2, lam_k2, gain, bias)


def _route(logits):
    r = [logits[g:g + 1, :] for g in range(N_GROUPS)]
    gmax = jnp.maximum(jnp.maximum(r[0], r[1]), jnp.maximum(r[2], r[3]))
    g_idx = jnp.where(r[0] == gmax, 0, jnp.where(r[1] == gmax, 1, jnp.where(r[2] == gmax, 2, 3)))
    denom = sum(jnp.exp(rg - gmax) for rg in r)
    g_weight = 1.0 / denom
    sel = jnp.zeros((EXPERTS_PER_GROUP, logits.shape[1]), F32)
    for g in range(N_GROUPS):
        rows = logits[8 + g * EXPERTS_PER_GROUP:8 + (g + 1) * EXPERTS_PER_GROUP, :]
        sel = jnp.where(g_idx == g, rows, sel)
    eidx = lax.broadcasted_iota(jnp.int32, sel.shape, 0)
    v1 = jnp.max(sel, axis=0, keepdims=True)
    i1 = jnp.min(jnp.where(sel == v1, eidx, EXPERTS_PER_GROUP), axis=0, keepdims=True)
    sel2 = jnp.where(eidx == i1, -jnp.inf, sel)
    v2 = jnp.max(sel2, axis=0, keepdims=True)
    i2 = jnp.min(jnp.where(sel2 == v2, eidx, EXPERTS_PER_GROUP), axis=0, keepdims=True)
    e2 = jnp.exp(v2 - v1)
    w1 = g_weight / (1.0 + e2)
    w2 = g_weight * e2 / (1.0 + e2)
    return g_idx, i1, i2, w1, w2


OUTPROJ_PARTS = 2
OUTPROJ_TILES_PER_STEP = 2


def _outproj_kernel(ret_ref, diff_ref, x_ref, mod_ref, gain_ref, wo_ref, wr_ref, br_ref, tri_ref,
                    x1_ref, h2_ref, ri_ref, rw_ref, cnt_ref, *, tiles_per_batch):
    b = pl.program_id(0) // tiles_per_batch
    gate1 = mod_ref[pl.ds(b, 1), 2 * D_MODEL:3 * D_MODEL]
    shift = mod_ref[pl.ds(b, 1), 3 * D_MODEL:4 * D_MODEL]
    scale = mod_ref[pl.ds(b, 1), 4 * D_MODEL:5 * D_MODEL]
    wr = wr_ref[...]
    rows = x_ref.shape[0]
    tm = tri_ref.shape[0]
    n_parts = OUTPROJ_PARTS * rows // tm
    parts = [slice(n * rows // n_parts, (n + 1) * rows // n_parts) for n in range(n_parts)]
    mix = [_dot(jnp.concatenate([ret_ref[r, :], diff_ref[r, :]], axis=1), wo_ref[...]) for r in parts]
    for r, m in zip(parts, mix):
        x1_ref[r, :] = x_ref[r, :] + gate1 * m
    h_split = [_split_bf16(_norm_modulate(x1_ref[r, :], gain_ref[...], shift, scale)) for r in parts]
    for r, (h_hi, _) in zip(parts, h_split):
        h2_ref[r, :] = h_hi
    by_hi = [_dot_nt(wr, h_hi) for h_hi, _ in h_split]
    by_lo = [_dot_nt(wr[:ROUTER_ROWS], h_lo) for _, h_lo in h_split]
    logits = [a[:ROUTER_ROWS] + a[ROUTER_ROWS:] + c + br_ref[...] for a, c in zip(by_hi, by_lo)]
    routed = [_route(lg) for lg in logits]
    g_idx, i1, i2, w1, w2 = [jnp.concatenate([rt[n] for rt in routed], axis=1) for n in range(5)]
    logits = jnp.concatenate(logits, axis=1)
    e1 = g_idx * EXPERTS_PER_GROUP + i1
    e2 = g_idx * EXPERTS_PER_GROUP + i2
    eidx = lax.broadcasted_iota(jnp.int32, (N_EXPERTS, logits.shape[1]), 0)
    hit1 = eidx == e1
    hit2 = eidx == e2
    onehot = jnp.where(hit1 | hit2, 1.0, 0.0)
    tiles = [slice(n * tm, (n + 1) * tm) for n in range(rows // tm)]
    before = jnp.concatenate([_dot(onehot[:, t].astype(BF16), tri_ref[...]) for t in tiles], axis=1)
    r1 = jnp.sum(jnp.where(hit1, before, 0.0), axis=0, keepdims=True)
    r2 = jnp.sum(jnp.where(hit2, before, 0.0), axis=0, keepdims=True)
    zi = jnp.zeros_like(e1)
    ri_ref[...] = jnp.concatenate([e1, e2, r1.astype(jnp.int32), r2.astype(jnp.int32), zi, zi, zi, zi], axis=0)
    zf = jnp.zeros_like(w1)
    rw_ref[...] = jnp.concatenate([w1, w2, zf, zf, zf, zf, zf, zf], axis=0)
    for n, t in enumerate(tiles):
        counts = jnp.sum(onehot[:, t], axis=1, keepdims=True)
        cnt_ref[n] = jnp.broadcast_to(counts, (N_EXPERTS, LANES)).astype(jnp.int32)


def _outproj(ret_out, diff_out, x2, mod, gain, w_out, wr, br, seq, tm):
    tokens = x2.shape[0]
    n_tiles = tokens // tm
    per_step = OUTPROJ_TILES_PER_STEP if seq % (OUTPROJ_TILES_PER_STEP * tm) == 0 else 1
    rows = per_step * tm
    tri = jnp.asarray(np.arange(tm)[:, None] < np.arange(tm)[None, :], BF16)
    tok_spec = lambda w: pl.BlockSpec((rows, w), lambda i: (i, 0))
    row_spec = pl.BlockSpec((8, rows), lambda i: (0, i))
    full = lambda a: pl.BlockSpec(a.shape, lambda i: (0,) * a.ndim)
    return pl.pallas_call(
        functools.partial(_outproj_kernel, tiles_per_batch=seq // rows),
        grid=(tokens // rows,),
        in_specs=[tok_spec(RET_WIDTH), tok_spec(DIFF_WIDTH), tok_spec(D_MODEL), full(mod), full(gain),
                  full(w_out), full(wr), full(br), full(tri)],
        out_specs=[tok_spec(D_MODEL), tok_spec(D_MODEL), row_spec, row_spec,
                   pl.BlockSpec((per_step, N_EXPERTS, LANES), lambda i: (i, 0, 0))],
        out_shape=[jax.ShapeDtypeStruct((tokens, D_MODEL), F32),
                   jax.ShapeDtypeStruct((tokens, D_MODEL), BF16),
                   jax.ShapeDtypeStruct((8, tokens), jnp.int32),
                   jax.ShapeDtypeStruct((8, tokens), F32),
                   jax.ShapeDtypeStruct((n_tiles, N_EXPERTS, LANES), jnp.int32)],
        compiler_params=pltpu.CompilerParams(
            dimension_semantics=("parallel",), vmem_limit_bytes=VMEM_LIMIT),
        name="outproj",
    )(ret_out, diff_out, x2, mod, gain, w_out, wr, br, tri)


CHUNK = 8
BIG_PIECE = 4 * CHUNK
MAX_SMALL_PIECES = N_EXPERTS * (BIG_PIECE // CHUNK - 1)
PIECE_TABLES = ("n_big", "big_src", "big_dst", "n_small", "small_src", "small_dst")
DISPATCH_TILES_PER_STEP = 2
COMBINE_TILES_PER_STEP = 2
TMX = 512


def _local_rows(tm):
    rows = 2 * tm + N_EXPERTS * (CHUNK - 1)
    return (rows + 15) // 16 * 16


def _max_big_pieces(tm):
    return _local_rows(tm) // BIG_PIECE


def _sorted_rows_alloc(tokens, tm):
    worst = 2 * tokens + (tokens // tm) * N_EXPERTS * (CHUNK - 1) + N_EXPERTS * (TMX - CHUNK)
    return (worst + TMX - 1) // TMX * TMX


def _dispatch_plan(cnt, tokens, tm):
    i32 = jnp.int32
    pad = (cnt + CHUNK - 1) // CHUNK * CHUNK
    local_end = jnp.cumsum(pad, axis=1)
    local_start = local_end - pad
    seg_rows = jnp.sum(pad, axis=0)
    seg_pad = (seg_rows + TMX - 1) // TMX * TMX
    seg_end = jnp.cumsum(seg_pad)
    seg_start = seg_end - seg_pad
    run_dst = seg_start[None, :] + jnp.cumsum(pad, axis=0) - pad

    def pieces(count, offset, size, max_n):
        end = jnp.cumsum(count, axis=1)
        start = end - count
        k = jnp.arange(max_n, dtype=i32)[None, :, None]
        owns = (start[:, None, :] <= k) & (k < end[:, None, :])
        within = size * (k - start[:, None, :]) + offset[:, None, :]
        src = jnp.sum(jnp.where(owns, local_start[:, None, :] + within, 0), axis=-1)
        dst = jnp.sum(jnp.where(owns, run_dst[:, None, :] + within, 0), axis=-1)
        return end[:, -1].astype(i32), src.reshape(-1).astype(i32), dst.reshape(-1).astype(i32)

    n_big, big_src, big_dst = pieces(pad // BIG_PIECE, jnp.zeros_like(pad), BIG_PIECE, _max_big_pieces(tm))
    n_small, small_src, small_dst = pieces(pad % BIG_PIECE // CHUNK, pad // BIG_PIECE * BIG_PIECE, CHUNK,
                                           MAX_SMALL_PIECES)
    m = TMX * jnp.arange(_sorted_rows_alloc(tokens, tm) // TMX, dtype=i32)
    tile_expert = jnp.minimum(jnp.sum(seg_end[None, :] <= m[:, None], axis=-1), N_EXPERTS - 1)
    towns = (seg_start[None, :] <= m[:, None]) & (m[:, None] < seg_end[None, :])
    used = seg_pad > 0
    parity = (jnp.cumsum(used) - used) % 2
    eids = jnp.arange(N_EXPERTS, dtype=i32)
    later_used = (eids[None, :] > eids[:, None]) & used[None, :]
    next_used = jnp.min(jnp.where(later_used, eids[None, :], N_EXPERTS), axis=1)
    next_used = jnp.where(next_used == N_EXPERTS, -1, next_used)
    pick = lambda per_expert: jnp.sum(jnp.where(towns, per_expert[None, :], 0), axis=-1)
    tile_first = jnp.sum(jnp.where(towns & (seg_start[None, :] == m[:, None]), 1, 0), axis=-1)
    tile_next = jnp.where(jnp.any(towns, axis=-1), pick(next_used), -1)
    return dict(
        tile_first=tile_first.astype(i32),
        tile_slot=pick(parity).astype(i32),
        tile_next=tile_next.astype(i32),
        tile_rows=jnp.clip(pick(seg_start + seg_rows) - m, 0, TMX).astype(i32),
        local_start=local_start.reshape(-1).astype(i32),
        n_big=n_big, big_src=big_src, big_dst=big_dst,
        n_small=n_small, small_src=small_src, small_dst=small_dst,
        tail_base=(seg_start + seg_rows).astype(i32),
        tail_rows=(seg_pad - seg_rows).astype(i32),
        tile_expert=tile_expert.astype(i32),
        n_used=(seg_end[-1:] // TMX).astype(i32),
    )


WAIT_UNROLL = 8


def _wait_times(copy, n):
    lax.fori_loop(0, n // WAIT_UNROLL, lambda i, c: ([copy.wait() for _ in range(WAIT_UNROLL)], c)[1], 0)
    lax.fori_loop(0, n % WAIT_UNROLL, lambda i, c: (copy.wait(), c)[1], 0)


def _for_each(n, body, unroll=4):
    main = n // unroll
    lax.fori_loop(0, main, lambda i, c: ([body(i * unroll + u) for u in range(unroll)], c)[1], 0)
    lax.fori_loop(main * unroll, n, lambda j, c: (body(j), c)[1], 0)


def _local_slots(ri_ref, local_start_ref, tile, cols=slice(None)):
    e1, e2 = ri_ref[0:1, cols], ri_ref[1:2, cols]
    s1, s2 = ri_ref[2:3, cols], ri_ref[3:4, cols]
    for e in range(N_EXPERTS):
        start = local_start_ref[tile * N_EXPERTS + e]
        s1 = s1 + jnp.where(e1 == e, start, 0)
        s2 = s2 + jnp.where(e2 == e, start, 0)
    return s1, s2


def _run_pieces(piece_refs, local_ref, sorted_ref, sem_ref, to_sorted):
    n_big_ref, big_src_ref, big_dst_ref, n_small_ref, small_src_ref, small_dst_ref = piece_refs
    r_loc = local_ref.shape[1]
    kinds = [(BIG_PIECE, n_big_ref, big_src_ref, big_dst_ref, r_loc // BIG_PIECE),
             (CHUNK, n_small_ref, small_src_ref, small_dst_ref, MAX_SMALL_PIECES)]

    def copy(sl, size, local_row, sorted_row):
        local = local_ref.at[sl, pl.ds(pl.multiple_of(local_row, CHUNK), size), :]
        srt = sorted_ref.at[pl.ds(pl.multiple_of(sorted_row, CHUNK), size), :]
        return pltpu.make_async_copy(local, srt, sem_ref.at[sl]) if to_sorted else \
            pltpu.make_async_copy(srt, local, sem_ref.at[sl])

    def start(tile, sl):
        for size, n_ref, src_ref, dst_ref, max_n in kinds:
            _for_each(n_ref[tile], lambda k: copy(sl, size, src_ref[tile * max_n + k], dst_ref[tile * max_n + k])
                      .start())

    def wait(tile, sl):
        for size, n_ref, _, _, _ in kinds:
            _wait_times(copy(sl, size, 0, 0), n_ref[tile])

    return start, wait


def _dispatch_kernel(local_start_ref, n_big_ref, big_src_ref, big_dst_ref, n_small_ref, small_src_ref, small_dst_ref,
                     tail_base_ref, tail_rows_ref, n_used_ref,
                     h_ref, ri_ref, xs_ref, buf_ref, zero_ref, sem_ref, tail_sem_ref, *, r_loc):
    b = pl.program_id(0)
    nb = pl.num_programs(0)
    per_step = buf_ref.shape[0] // 2
    tm = h_ref.shape[0] // per_step
    gen = b % 2
    start_runs, drain_tile = _run_pieces(
        (n_big_ref, big_src_ref, big_dst_ref, n_small_ref, small_src_ref, small_dst_ref), buf_ref, xs_ref, sem_ref,
        to_sorted=True)

    def drain(step, g):
        for t in range(per_step):
            drain_tile(step * per_step + t, g * per_step + t)

    @pl.when(b >= 2)
    def _():
        drain(b - 2, gen)

    tiles = [b * per_step + t for t in range(per_step)]
    cols = [slice(t * tm, (t + 1) * tm) for t in range(per_step)]
    slots = [_local_slots(ri_ref, local_start_ref, tile, c) for tile, c in zip(tiles, cols)]
    rows = lax.broadcasted_iota(jnp.int32, (r_loc, tm), 0)
    perms = [jnp.where((rows == s1) | (rows == s2), 1.0, 0.0).astype(BF16) for s1, s2 in slots]
    for t, (perm, c) in enumerate(zip(perms, cols)):
        buf_ref[gen * per_step + t] = _dot(perm, h_ref[c, :])
    for t, tile in enumerate(tiles):
        start_runs(tile, gen * per_step + t)

    def tail_pieces(e, act):
        n = tail_rows_ref[e]
        size = TMX // 2
        while size >= CHUNK:
            dst = pl.multiple_of(tail_base_ref[e] + (n & (-2 * size)), CHUNK)
            cp = pltpu.make_async_copy(zero_ref.at[pl.ds(0, size), :], xs_ref.at[pl.ds(dst, size), :],
                                       tail_sem_ref.at[0])
            pl.when((n & size) != 0)(functools.partial(act, cp))
            size //= 2

    def unused_tile_copy(m):
        dst = pl.multiple_of(m * TMX, TMX)
        return pltpu.make_async_copy(zero_ref, xs_ref.at[pl.ds(dst, TMX), :], tail_sem_ref.at[1])

    n_alloc = xs_ref.shape[0] // TMX

    @pl.when(b == 0)
    def _():
        zero_ref[...] = jnp.zeros_like(zero_ref)
        lax.fori_loop(0, N_EXPERTS, lambda e, c: (tail_pieces(e, lambda cp: cp.start()), c)[1], 0)
        lax.fori_loop(n_used_ref[0], n_alloc, lambda m, c: (unused_tile_copy(m).start(), c)[1], 0)

    @pl.when(b == nb - 1)
    def _():
        lax.fori_loop(0, N_EXPERTS, lambda e, c: (tail_pieces(e, lambda cp: cp.wait()), c)[1], 0)
        lax.fori_loop(n_used_ref[0], n_alloc, lambda m, c: (unused_tile_copy(m).wait(), c)[1], 0)

        @pl.when(b >= 1)
        def _():
            drain(b - 1, 1 - gen)

        drain(b, gen)


def _dispatch(h2, ri, plan, tm):
    tokens = h2.shape[0]
    r_loc = _local_rows(tm)
    prefetch = [plan["local_start"]] + [plan[k] for k in PIECE_TABLES] + [
        plan["tail_base"], plan["tail_rows"], plan["n_used"]]
    per_step = DISPATCH_TILES_PER_STEP if (tokens // tm) % DISPATCH_TILES_PER_STEP == 0 else 1
    rows = per_step * tm
    grid_spec = pltpu.PrefetchScalarGridSpec(
        num_scalar_prefetch=len(prefetch),
        grid=(tokens // rows,),
        in_specs=[pl.BlockSpec((rows, D_MODEL), lambda i, *_: (i, 0)),
                  pl.BlockSpec((8, rows), lambda i, *_: (0, i))],
        out_specs=pl.BlockSpec(memory_space=pl.ANY),
        scratch_shapes=[pltpu.VMEM((2 * per_step, r_loc, D_MODEL), F32), pltpu.VMEM((TMX, D_MODEL), F32),
                        pltpu.SemaphoreType.DMA((2 * per_step,)), pltpu.SemaphoreType.DMA((2,))],
    )
    return pl.pallas_call(
        functools.partial(_dispatch_kernel, r_loc=r_loc),
        grid_spec=grid_spec,
        out_shape=jax.ShapeDtypeStruct((_sorted_rows_alloc(tokens, tm), D_MODEL), F32),
        compiler_params=pltpu.CompilerParams(
            dimension_semantics=("arbitrary",), vmem_limit_bytes=VMEM_LIMIT),
        name="dispatch",
    )(*prefetch, h2, ri)


def _experts_kernel(tile_expert_ref, n_used_ref, first_ref, slot_ref, next_ref, rows_ref, xs_ref, wg_hbm, wu_hbm, wd_hbm,
                    ys_ref, wg_st, wu_st, wd_st, wg_bf, wu_bf, wd_bf, a_ref, u_ref, sem_ref):
    m = pl.program_id(0)

    def weight_copies(e, s):
        return [pltpu.make_async_copy(src.at[e], dst.at[s], sem_ref.at[s, n])
                for n, (src, dst) in enumerate([(wg_hbm, wg_st), (wu_hbm, wu_st), (wd_hbm, wd_st)])]

    @pl.when(m < n_used_ref[0])
    def _():
        @pl.when(first_ref[m] == 1)
        def _():
            s = slot_ref[m]

            @pl.when(m == 0)
            def _():
                for cp in weight_copies(tile_expert_ref[0], 0):
                    cp.start()

            for cp in weight_copies(tile_expert_ref[m], s):
                cp.wait()

            @pl.when(next_ref[m] >= 0)
            def _():
                for cp in weight_copies(next_ref[m], 1 - s):
                    cp.start()

            wg_bf[...] = wg_st[s].astype(BF16)
            wu_bf[...] = wu_st[s].astype(BF16)
            wd_bf[...] = wd_st[s].astype(BF16)

        def mlp(rows):
            x = xs_ref[rows, :].astype(BF16)
            a_ref[rows, :] = _dot(x, wg_bf[...])
            u_ref[rows, :] = _dot(x, wu_bf[...])
            hid = (_silu(a_ref[rows, :]) * u_ref[rows, :]).astype(BF16)
            ys_ref[rows, :] = _dot(hid, wd_bf[...])

        half = TMX // 2

        @pl.when(rows_ref[m] > half)
        def _():
            mlp(slice(0, TMX))

        @pl.when(rows_ref[m] <= half)
        def _():
            mlp(slice(0, half))
            ys_ref[half:, :] = jnp.zeros((TMX - half, D_MODEL), F32)


def _experts(xs, plan, wg, wu, wd):
    n_tiles = xs.shape[0] // TMX
    last_used = lambda m, n_used: jnp.minimum(m, n_used[0] - 1)
    row_spec = pl.BlockSpec((TMX, D_MODEL), lambda m, te, nu, *_: (last_used(m, nu), 0))
    hbm = pl.BlockSpec(memory_space=pl.ANY)
    up_shape, down_shape = (D_MODEL, D_EXPERT), (D_EXPERT, D_MODEL)
    grid_spec = pltpu.PrefetchScalarGridSpec(
        num_scalar_prefetch=6,
        grid=(n_tiles,),
        in_specs=[row_spec, hbm, hbm, hbm],
        out_specs=row_spec,
        scratch_shapes=[pltpu.VMEM((2,) + up_shape, F32), pltpu.VMEM((2,) + up_shape, F32),
                        pltpu.VMEM((2,) + down_shape, F32),
                        pltpu.VMEM(up_shape, BF16), pltpu.VMEM(up_shape, BF16), pltpu.VMEM(down_shape, BF16),
                        pltpu.VMEM((TMX, D_EXPERT), F32), pltpu.VMEM((TMX, D_EXPERT), F32),
                        pltpu.SemaphoreType.DMA((2, 3))],
    )
    return pl.pallas_call(
        _experts_kernel,
        grid_spec=grid_spec,
        out_shape=jax.ShapeDtypeStruct(xs.shape, F32),
        input_output_aliases={6: 0},
        compiler_params=pltpu.CompilerParams(
            dimension_semantics=("arbitrary",), vmem_limit_bytes=VMEM_LIMIT),
        name="experts",
    )(plan["tile_expert"], plan["n_used"], plan["tile_first"], plan["tile_slot"], plan["tile_next"], plan["tile_rows"],
      xs, wg, wu, wd)


def _combine_kernel(local_start_ref, n_big_ref, big_src_ref, big_dst_ref, n_small_ref, small_src_ref, small_dst_ref,
                    ys_ref, ri_ref, rw_ref, x1_ref, mod_ref, gain_ref, o_ref, buf_ref, sem_ref,
                    *, r_loc, tiles_per_batch):
    b = pl.program_id(0)
    nb = pl.num_programs(0)
    per_step = buf_ref.shape[0] // 2
    tm = x1_ref.shape[0] // per_step
    gen = b % 2
    fetch_tile, wait_tile = _run_pieces(
        (n_big_ref, big_src_ref, big_dst_ref, n_small_ref, small_src_ref, small_dst_ref), buf_ref, ys_ref, sem_ref,
        to_sorted=False)

    def fetch(step, g):
        for t in range(per_step):
            fetch_tile(step * per_step + t, g * per_step + t)

    @pl.when(b == 0)
    def _():
        buf_ref[...] = jnp.zeros_like(buf_ref)
        fetch(0, 0)

    @pl.when(b + 1 < nb)
    def _():
        fetch(b + 1, 1 - gen)

    for t in range(per_step):
        wait_tile(b * per_step + t, gen * per_step + t)

    tiles = [b * per_step + t for t in range(per_step)]
    cols = [slice(t * tm, (t + 1) * tm) for t in range(per_step)]
    slots = [_local_slots(ri_ref, local_start_ref, tile, c) for tile, c in zip(tiles, cols)]
    rows = lax.broadcasted_iota(jnp.int32, (r_loc, tm), 0)
    hits = [(rows == s1, rows == s2) for s1, s2 in slots]
    w_rows = [jnp.sum(jnp.where(h1, rw_ref[0:1, c], jnp.where(h2, rw_ref[1:2, c], 0.0)), axis=1, keepdims=True)
              for (h1, h2), c in zip(hits, cols)]
    perms = [jnp.where(h1 | h2, 1.0, 0.0).astype(BF16) for h1, h2 in hits]
    yws = [(buf_ref[gen * per_step + t] * w_rows[t]).astype(BF16) for t in range(per_step)]
    moes = [_dot_tn(perm, yw) for perm, yw in zip(perms, yws)]
    batch = b // tiles_per_batch
    gate2 = mod_ref[pl.ds(batch, 1), 5 * D_MODEL:6 * D_MODEL]
    for c, moe in zip(cols, moes):
        x2 = x1_ref[c, :] + gate2 * moe
        ms = jnp.mean(x2 * x2, axis=-1, keepdims=True)
        o_ref[c, :] = x2 * lax.rsqrt(ms + EPS) * gain_ref[...]


def _combine(ys, ri, rw, x1, mod, gain, plan, seq, tm):
    tokens = x1.shape[0]
    r_loc = _local_rows(tm)
    per_step = COMBINE_TILES_PER_STEP if seq % (COMBINE_TILES_PER_STEP * tm) == 0 else 1
    rows = per_step * tm
    row_spec = pl.BlockSpec((8, rows), lambda i, *_: (0, i))
    tok_spec = pl.BlockSpec((rows, D_MODEL), lambda i, *_: (i, 0))
    full = lambda a: pl.BlockSpec(a.shape, lambda i, *_: (0,) * a.ndim)
    prefetch = [plan["local_start"]] + [plan[k] for k in PIECE_TABLES]
    grid_spec = pltpu.PrefetchScalarGridSpec(
        num_scalar_prefetch=len(prefetch),
        grid=(tokens // rows,),
        in_specs=[pl.BlockSpec(memory_space=pl.ANY), row_spec, row_spec, tok_spec, full(mod), full(gain)],
        out_specs=tok_spec,
        scratch_shapes=[pltpu.VMEM((2 * per_step, r_loc, D_MODEL), F32), pltpu.SemaphoreType.DMA((2 * per_step,))],
    )
    return pl.pallas_call(
        functools.partial(_combine_kernel, r_loc=r_loc, tiles_per_batch=seq // rows),
        grid_spec=grid_spec,
        out_shape=jax.ShapeDtypeStruct((tokens, D_MODEL), F32),
        compiler_params=pltpu.CompilerParams(
            dimension_semantics=("arbitrary",), vmem_limit_bytes=VMEM_LIMIT),
        name="combine",
    )(*prefetch, ys, ri, rw, x1, mod, gain)


def _rotary_tables(seq):
    half = RET_HEAD_DIM // 2
    inv_freq = 1.0 / (ROPE_BASE ** (np.arange(half, dtype=np.float64) / half))
    ang = np.arange(seq, dtype=np.float64)[:, None] * inv_freq[None, :]
    cos = np.cos(ang)
    sin = np.sin(ang)
    f32 = lambda a: jnp.asarray(a.astype(np.float32))
    return f32(np.tile(cos, (1, 4))), f32(np.concatenate([-sin, sin, -sin, sin], axis=1))


def _pick_tile(n, pref):
    t = min(n, pref)
    assert n % t == 0, (n, t)
    return t


def kernel(x, c, ada_w, ada_b, norm1_gain, norm2_gain, w_in, w_out, ret_gn_gain, lam_q1, lam_k1, lam_q2,
           lam_k2, diff_subln_gain, w_group, b_group, w_expert, b_expert, w_gate, w_up, w_down, final_gain):
    batch, seq, d = x.shape
    assert d == D_MODEL and batch <= 8 and ada_w.shape[0] == 1
    layer = 0
    lambda_init = 0.8 - 0.6 * math.exp(-0.3 * layer)
    tokens = batch * seq
    x2 = x.reshape(tokens, d)
    tm = _pick_tile(seq, 512)

    c_pad = jnp.zeros((8, d), F32).at[:batch].set(c)
    mod = _adaln(c_pad, ada_w[layer], ada_b[layer].reshape(1, -1))

    cos_t, sin_t = _rotary_tables(seq)
    rq, rk, rv, rg, dq, dk, dvt = _inproj(
        x2, mod, norm1_gain[layer].reshape(1, d), w_in[layer].astype(BF16), cos_t, sin_t, seq, tm)

    ret_out = _retention(rq, rk, rv, rg, ret_gn_gain[layer].reshape(1, RET_WIDTH), batch, seq,
                         _pick_tile(seq, 256))
    diff_out = _diffattn(
        dq, dk, dvt, lam_q1[layer].reshape(1, -1), lam_k1[layer].reshape(1, -1), lam_q2[layer].reshape(1, -1),
        lam_k2[layer].reshape(1, -1), diff_subln_gain[layer].reshape(-1, 1), batch, seq, lambda_init,
        _pick_tile(seq, 2048), 2 * QUERY_CHUNK)

    w_router = jnp.concatenate(
        [w_group[layer].T, jnp.zeros((8 - N_GROUPS, d), F32), w_expert[layer].reshape(d, N_EXPERTS).T], axis=0)
    b_router = jnp.concatenate(
        [b_group[layer], jnp.zeros((8 - N_GROUPS,), F32), b_expert[layer].reshape(N_EXPERTS)]).reshape(-1, 1)
    wr_hi = w_router.astype(BF16)
    wr_lo = (w_router - wr_hi.astype(F32)).astype(BF16)
    x1, h2, ri, rw, cnt = _outproj(ret_out, diff_out, x2, mod, norm2_gain[layer].reshape(1, d),
                                   w_out[layer].astype(BF16), jnp.concatenate([wr_hi, wr_lo], axis=0), b_router,
                                   seq, tm)

    plan = _dispatch_plan(cnt[:, :, 0], tokens, tm)
    xs = _dispatch(h2, ri, plan, tm)
    ys = _experts(xs, plan, w_gate[layer].reshape(N_EXPERTS, d, D_EXPERT),
                  w_up[layer].reshape(N_EXPERTS, d, D_EXPERT), w_down[layer].reshape(N_EXPERTS, D_EXPERT, d))
    out = _combine(ys, ri, rw, x1, mod, final_gain.reshape(1, d), plan, seq, tm)
    return out.reshape(batch, seq, d)
```

```python
import functools
import math

import jax
import jax.numpy as jnp
import numpy as np
from jax import lax
from jax.experimental import pallas as pl
from jax.experimental.pallas import tpu as pltpu

F32 = jnp.float32
BF16 = jnp.bfloat16

D_MODEL = 1024
RET_HEAD_DIM = 64
RET_WIDTH = 512
RET_HEADS = 8
RET_PAIRS = RET_HEADS // 2
DIFF_QK_DIM = 64
DIFF_V_DIM = 128
DIFF_HEADS = 4
DIFF_WIDTH = 512
N_GROUPS = 4
EXPERTS_PER_GROUP = 8
N_EXPERTS = N_GROUPS * EXPERTS_PER_GROUP
D_EXPERT = 512
N_MOD = 6
ROPE_BASE = 10000.0
EPS = 1e-6
LANES = 128
ROUTER_ROWS = 8 + N_EXPERTS
VMEM_LIMIT = 56 * 1024 * 1024


def _dot(a, b):
    return jnp.dot(a, b, preferred_element_type=F32)


def _dot_nt(a, b):
    return lax.dot_general(a, b, (((1,), (1,)), ((), ())), preferred_element_type=F32)


def _dot_tn(a, b):
    return lax.dot_general(a, b, (((0,), (0,)), ((), ())), preferred_element_type=F32)


def _split_bf16(x):
    hi = x.astype(BF16)
    lo = (x - hi.astype(F32)).astype(BF16)
    return hi, lo


def _silu(x):
    return x / (1.0 + jnp.exp(-x))


def _adaln_kernel(c_ref, w_ref, b_ref, o_ref):
    ca = _silu(c_ref[...])
    c_hi, c_lo = _split_bf16(ca)
    w_hi, w_lo = _split_bf16(w_ref[...])
    o_ref[...] = _dot(c_hi, w_hi) + _dot(c_lo, w_hi) + _dot(c_hi, w_lo) + b_ref[...]


def _adaln(c_pad, ada_w, ada_b):
    n_out = ada_w.shape[1]
    tn = D_MODEL
    return pl.pallas_call(
        _adaln_kernel,
        grid=(n_out // tn,),
        in_specs=[
            pl.BlockSpec((8, D_MODEL), lambda j: (0, 0)),
            pl.BlockSpec((D_MODEL, tn), lambda j: (0, j)),
            pl.BlockSpec((1, tn), lambda j: (0, j)),
        ],
        out_specs=pl.BlockSpec((8, tn), lambda j: (0, j)),
        out_shape=jax.ShapeDtypeStruct((8, n_out), F32),
        compiler_params=pltpu.CompilerParams(vmem_limit_bytes=VMEM_LIMIT),
        name="adaln",
    )(c_pad, ada_w, ada_b)


def _norm_modulate(x, gain, shift, scale):
    ms = jnp.mean(x * x, axis=-1, keepdims=True)
    y = x * lax.rsqrt(ms + EPS) * gain
    return y * (1.0 + scale) + shift


def _rotary_slab(x, cos, sin_signed, lane_lo):
    swapped = jnp.where(lane_lo, pltpu.roll(x, 96, 1), pltpu.roll(x, 32, 1))
    return x * cos + swapped * sin_signed


def _inproj_kernel(x_ref, mod_ref, gain_ref, w_ref, cos_ref, sin_ref,
                   rq_ref, rk_ref, rv_ref, rg_ref, dq_ref, dk_ref, dvt_ref, *, tiles_per_batch):
    b = pl.program_id(0) // tiles_per_batch
    shift = mod_ref[pl.ds(b, 1), 0:D_MODEL]
    scale = mod_ref[pl.ds(b, 1), D_MODEL:2 * D_MODEL]
    h = _norm_modulate(x_ref[...], gain_ref[...], shift, scale).astype(BF16)
    cos = cos_ref[...]
    sin = sin_ref[...]
    lane = lax.broadcasted_iota(jnp.int32, cos.shape, 1)
    lane_lo = (lane % 64) < 32

    def proj(chunk):
        return _dot(h, w_ref[:, chunk * RET_WIDTH:(chunk + 1) * RET_WIDTH])

    def rotary(acc, out_ref, post_scale):
        for s in range(RET_WIDTH // LANES):
            sl = slice(s * LANES, (s + 1) * LANES)
            out_ref[:, sl] = (_rotary_slab(acc[:, sl], cos, sin, lane_lo) * post_scale).astype(BF16)

    rotary(proj(0), rq_ref, 1.0)
    rotary(proj(1), rk_ref, RET_HEAD_DIM ** -0.5)
    rv_ref[...] = proj(2).astype(BF16)
    rg_ref[...] = _silu(proj(3)).astype(BF16)
    dq_ref[...] = (proj(4) * (DIFF_QK_DIM ** -0.5 * math.log2(math.e))).astype(BF16)
    dk_ref[...] = proj(5).astype(BF16)
    dvt_ref[...] = proj(6).T.astype(BF16)


def _inproj(x2, mod, gain, w_in, cos_t, sin_t, seq, tm):
    tokens = x2.shape[0]
    tiles_per_batch = seq // tm
    tok_spec = lambda w: pl.BlockSpec((tm, w), lambda i: (i, 0))
    tab_spec = pl.BlockSpec((tm, LANES), lambda i: (i % tiles_per_batch, 0))
    full = lambda a: pl.BlockSpec(a.shape, lambda i: (0,) * a.ndim)
    out = jax.ShapeDtypeStruct((tokens, RET_WIDTH), BF16)
    return pl.pallas_call(
        functools.partial(_inproj_kernel, tiles_per_batch=tiles_per_batch),
        grid=(tokens // tm,),
        in_specs=[tok_spec(D_MODEL), full(mod), full(gain), full(w_in), tab_spec, tab_spec],
        out_specs=[tok_spec(RET_WIDTH)] * 6 + [pl.BlockSpec((DIFF_WIDTH, tm), lambda i: (0, i))],
        out_shape=[out] * 6 + [jax.ShapeDtypeStruct((DIFF_WIDTH, tokens), BF16)],
        compiler_params=pltpu.CompilerParams(
            dimension_semantics=("parallel",), vmem_limit_bytes=VMEM_LIMIT),
        name="inproj",
    )(x2, mod, gain, w_in, cos_t, sin_t)


RET_SEQ_GROUP = 4


def _retention_kernel(q_ref, k_ref, v_ref, g_ref, dec_ref, qdec_ref, kdec_ref, rdec_ref,
                      bmask_ref, gmean_ref, gain_ref, o_ref, state_ref, *, chunk):
    @pl.when(pl.program_id(1) == 0)
    def _():
        state_ref[...] = jnp.zeros_like(state_ref)

    lane = lax.broadcasted_iota(jnp.int32, (chunk, LANES), 1)
    first_head = lane < RET_HEAD_DIM
    gmean = gmean_ref[...]
    bmask = bmask_ref[...]
    units = [(s, p) for s in range(q_ref.shape[0]) for p in range(RET_PAIRS)]
    sl = lambda p: slice(p * LANES, (p + 1) * LANES)
    q = [q_ref[s, :, sl(p)] for s, p in units]
    k = [k_ref[s, :, sl(p)] for s, p in units]
    v = [v_ref[s, :, sl(p)] for s, p in units]
    zero = jnp.zeros_like(q[0])
    q_stack = [jnp.concatenate([jnp.where(first_head, qu, zero), jnp.where(first_head, zero, qu)], axis=0)
               for qu in q]
    scores = [(_dot_nt(q_stack[n], k[n]) * dec_ref[p]).astype(BF16) for n, (_, p) in enumerate(units)]
    state = [state_ref[n] for n in range(len(units))]
    cross = [_dot(q[n], state[n].astype(BF16)) * qdec_ref[:, sl(p)] for n, (_, p) in enumerate(units)]
    k_dec = [(k[n].astype(F32) * kdec_ref[:, sl(p)]).astype(BF16) for n, (_, p) in enumerate(units)]
    for n, (_, p) in enumerate(units):
        state_ref[n] = state[n] * rdec_ref[p] + _dot_tn(k_dec[n], v[n]) * bmask
    intra2 = [_dot(scores[n], v[n]) for n in range(len(units))]
    y = [jnp.where(first_head, intra2[n][:chunk], intra2[n][chunk:]) + cross[n] for n in range(len(units))]
    seg_mean = lambda x: _dot(jnp.concatenate(_split_bf16(x), axis=1), gmean)
    mu = [seg_mean(yu) for yu in y]
    d = [yu - mu_u for yu, mu_u in zip(y, mu)]
    var = [seg_mean(du * du) for du in d]
    for n, (s, p) in enumerate(units):
        yn = d[n] * lax.rsqrt(var[n] + EPS) * gain_ref[:, sl(p)]
        o_ref[s, :, sl(p)] = (g_ref[s, :, sl(p)].astype(F32) * yn).astype(BF16)


def _retention_tables(chunk):
    heads = np.arange(RET_HEADS, dtype=np.float64)
    log_gamma = np.log(1.0 - np.exp2(-5.0 - heads))
    idx = np.arange(chunk)
    rel = (idx[:, None] - idx[None, :]).astype(np.float64)
    decay = np.where(rel[None] >= 0, np.exp(log_gamma[:, None, None] * np.maximum(rel, 0.0)[None]), 0.0)
    dec2 = decay.reshape(RET_PAIRS, 2 * chunk, chunk)
    lane_lg = np.repeat(log_gamma, RET_HEAD_DIM)
    qdec = np.exp(lane_lg[None, :] * (idx + 1)[:, None])
    kdec = np.exp(lane_lg[None, :] * (chunk - 1 - idx)[:, None])
    rdec = np.exp(lane_lg * chunk).reshape(RET_PAIRS, LANES, 1) * np.ones((1, 1, LANES))
    blk = np.arange(LANES) // RET_HEAD_DIM
    bmask = (blk[:, None] == blk[None, :]).astype(np.float64)
    f32 = lambda a: jnp.asarray(a.astype(np.float32))
    gmean2 = np.concatenate([bmask, bmask], axis=0) / RET_HEAD_DIM
    return f32(dec2), f32(qdec), f32(kdec), f32(rdec), f32(bmask), f32(gmean2).astype(BF16)


def _retention(rq, rk, rv, rg, gn_gain, batch, seq, chunk):
    nc = seq // chunk
    group = RET_SEQ_GROUP if batch % RET_SEQ_GROUP == 0 else 1
    dec2, qdec, kdec, rdec, bmask, gmean = _retention_tables(chunk)
    tok_spec = pl.BlockSpec((group, chunk, RET_WIDTH), lambda b, n: (b, n, 0))
    full = lambda a: pl.BlockSpec(a.shape, lambda b, n: (0,) * a.ndim)
    by_seq = lambda a: a.reshape(batch, seq, RET_WIDTH)
    out = pl.pallas_call(
        functools.partial(_retention_kernel, chunk=chunk),
        grid=(batch // group, nc),
        in_specs=[tok_spec] * 4 + [full(dec2), full(qdec), full(kdec), full(rdec), full(bmask),
                                   full(gmean), full(gn_gain)],
        out_specs=tok_spec,
        out_shape=jax.ShapeDtypeStruct((batch, seq, RET_WIDTH), BF16),
        scratch_shapes=[pltpu.VMEM((group * RET_PAIRS, LANES, LANES), F32)],
        compiler_params=pltpu.CompilerParams(
            dimension_semantics=("parallel", "arbitrary"), vmem_limit_bytes=VMEM_LIMIT),
        name="retention",
    )(by_seq(rq), by_seq(rk), by_seq(rv), by_seq(rg), dec2, qdec, kdec, rdec, bmask, gmean, gn_gain)
    return out.reshape(batch * seq, RET_WIDTH)


NEG_BIG = -1e30


V_EXT_ROWS = DIFF_V_DIM + 16
QUERY_CHUNK = 256
SCORES_AHEAD_FULL = 3
SCORES_AHEAD_DIAG = 3
STAGE_SLOTS = 16


def _diag_chunks(tq, tk, d):
    assert tk == 2 * QUERY_CHUNK
    per_softmax = tq // QUERY_CHUNK
    out = []
    for c in range(2 * per_softmax):
        q0 = (c % per_softmax) * QUERY_CHUNK
        if q0 + QUERY_CHUNK - 1 < d * tk:
            continue
        kind = "full" if q0 >= (d + 1) * tk else ("tri" if q0 == d * tk else "low_tri")
        out.append((c, kind))
    return out


def _accumulate(acc_ref, cs, alpha, pv):
    acc_ref[:, cs] = alpha * acc_ref[:, cs] + pv


def _diffattn_kernel(q_ref, k_ref, vt_ref, lq1_ref, lk1_ref, lq2_ref, lk2_ref, gain_ref, bias_ref, o_ref,
                     qs_ref, vext_ref, m_ref, acc_ref, stage_ref, *, tq, tk, lambda_init):
    i = pl.program_id(2)
    nk = vext_ref.shape[0]

    @pl.when(i == 0)
    def _():
        for j in range(nk):
            vext_ref[j, 0:DIFF_V_DIM, :] = vt_ref[:, j * tk:(j + 1) * tk]
            vext_ref[j, DIFF_V_DIM:V_EXT_ROWS, :] = jnp.ones((V_EXT_ROWS - DIFF_V_DIM, tk), BF16)

    q = q_ref[...]
    lane = lax.broadcasted_iota(jnp.int32, q.shape, 1)
    zero = jnp.zeros_like(q)
    qs_ref[0:tq, :] = jnp.where(lane < DIFF_QK_DIM, q, zero)
    qs_ref[tq:2 * tq, :] = jnp.where(lane < DIFF_QK_DIM, zero, q)
    m_ref[...] = jnp.full_like(m_ref, NEG_BIG)
    acc_ref[...] = jnp.zeros_like(acc_ref)

    def step(work, n_ahead):
        chunk = lambda c: slice(c * QUERY_CHUNK, (c + 1) * QUERY_CHUNK)

        def scores(j, c, kind):
            n_keys = QUERY_CHUNK if kind == "tri" else tk
            start = pl.multiple_of(j * tk, tk)
            return _dot_nt(k_ref[pl.ds(start, n_keys), :], qs_ref[chunk(c), :])

        ahead = [scores(*work[n]) for n in range(min(n_ahead, len(work)))]
        pending = None
        for n, (j, c, kind) in enumerate(work):
            cs = chunk(c)
            st = ahead.pop(0)
            if n + n_ahead < len(work):
                ahead.append(scores(*work[n + n_ahead]))
            slot = n % stage_ref.shape[0]
            n_keys = st.shape[0]
            stage_ref[slot, 0:n_keys, :] = st
            if kind == "full":
                st = stage_ref[slot]
            else:
                causal = stage_ref[slot, n_keys - QUERY_CHUNK:n_keys, :] + bias_ref[...]
                st = causal if kind == "tri" else jnp.concatenate(
                    [stage_ref[slot, 0:n_keys - QUERY_CHUNK, :], causal], axis=0)
            m_old = m_ref[:, cs]
            m_new = jnp.maximum(m_old, jnp.max(st, axis=0, keepdims=True))
            alpha = jnp.exp2(m_old - m_new)
            p = jnp.exp2(st - m_new).astype(BF16)
            m_ref[:, cs] = m_new
            pv = _dot(vext_ref[j, :, 0:st.shape[0]], p)
            if pending is not None:
                pending()
            pending = functools.partial(_accumulate, acc_ref, cs, alpha, pv)
        pending()

    tiles_per_q = tq // tk
    n_chunks = 2 * tq // QUERY_CHUNK

    def full_tiles(it):
        return [(it * tiles_per_q + d, c, "full") for d in range(tiles_per_q) for c in range(n_chunks)]

    lax.fori_loop(0, i, lambda it, c: (step(full_tiles(it), SCORES_AHEAD_FULL), c)[1], 0)
    kind_order = {"full": 0, "low_tri": 1, "tri": 2}
    diag = lambda it: [(it * tiles_per_q + d, c, kind) for d in range(tiles_per_q)
                       for c, kind in sorted(_diag_chunks(tq, tk, d), key=lambda ck: kind_order[ck[1]])]
    lax.fori_loop(i, i + 1, lambda it, c: (step(diag(it), SCORES_AHEAD_DIAG), c)[1], 0)

    lam = (jnp.exp(jnp.sum(lq1_ref[...] * lk1_ref[...], axis=-1, keepdims=True))
           - jnp.exp(jnp.sum(lq2_ref[...] * lk2_ref[...], axis=-1, keepdims=True)) + lambda_init)
    acc = acc_ref[...]
    o2 = acc[0:DIFF_V_DIM, :] * (1.0 / acc[DIFF_V_DIM:DIFF_V_DIM + 1, :])
    ot = o2[:, :tq] - lam * o2[:, tq:]
    ms = jnp.mean(ot * ot, axis=0, keepdims=True)
    ot = ot * lax.rsqrt(ms + EPS) * gain_ref[...] * (1.0 - lambda_init)
    o_ref[...] = ot.T.astype(BF16)


def _diffattn(dq, dk, dvt, lam_q1, lam_k1, lam_q2, lam_k2, gain, batch, seq, lambda_init, tq, tk):
    nq = seq // tq
    q_spec = pl.BlockSpec((tq, LANES), lambda b, h, i: (b * nq + i, h))
    k_spec = pl.BlockSpec((seq, LANES), lambda b, h, i: (b, h))
    vt_spec = pl.BlockSpec((DIFF_V_DIM, seq), lambda b, h, i: (h, b))
    vec = lambda a: pl.BlockSpec(a.shape, lambda b, h, i: (0, 0))
    key = np.arange(QUERY_CHUNK)[:, None]
    query = np.arange(QUERY_CHUNK)[None, :]
    bias = jnp.asarray(np.where(key <= query, 0.0, NEG_BIG), F32)
    return pl.pallas_call(
        functools.partial(_diffattn_kernel, tq=tq, tk=tk, lambda_init=lambda_init),
        grid=(batch, DIFF_HEADS, nq),
        in_specs=[q_spec, k_spec, vt_spec, vec(lam_q1), vec(lam_k1), vec(lam_q2), vec(lam_k2), vec(gain),
                  vec(bias)],
        out_specs=q_spec,
        out_shape=jax.ShapeDtypeStruct(dq.shape, BF16),
        scratch_shapes=[
            pltpu.VMEM((2 * tq, LANES), BF16),
            pltpu.VMEM((seq // tk, V_EXT_ROWS, tk), BF16),
            pltpu.VMEM((1, 2 * tq), F32),
            pltpu.VMEM((V_EXT_ROWS, 2 * tq), F32),
            pltpu.VMEM((STAGE_SLOTS, tk, QUERY_CHUNK), F32),
        ],
        compiler_params=pltpu.CompilerParams(
            dimension_semantics=("parallel", "parallel", "arbitrary"), vmem_limit_bytes=VMEM_LIMIT),
        name="diffattn",
    )(dq, dk, dvt, lam_q1, lam_k1, lam_q2, lam_k2, gain, bias)


def _route(logits):
    r = [logits[g:g + 1, :] for g in range(N_GROUPS)]
    gmax = jnp.maximum(jnp.maximum(r[0], r[1]), jnp.maximum(r[2], r[3]))
    g_idx = jnp.where(r[0] == gmax, 0, jnp.where(r[1] == gmax, 1, jnp.where(r[2] == gmax, 2, 3)))
    denom = sum(jnp.exp(rg - gmax) for rg in r)
    g_weight = 1.0 / denom
    sel = jnp.zeros((EXPERTS_PER_GROUP, logits.shape[1]), F32)
    for g in range(N_GROUPS):
        rows = logits[8 + g * EXPERTS_PER_GROUP:8 + (g + 1) * EXPERTS_PER_GROUP, :]
        sel = jnp.where(g_idx == g, rows, sel)
    eidx = lax.broadcasted_iota(jnp.int32, sel.shape, 0)
    v1 = jnp.max(sel, axis=0, keepdims=True)
    i1 = jnp.min(jnp.where(sel == v1, eidx, EXPERTS_PER_GROUP), axis=0, keepdims=True)
    sel2 = jnp.where(eidx == i1, -jnp.inf, sel)
    v2 = jnp.max(sel2, axis=0, keepdims=True)
    i2 = jnp.min(jnp.where(sel2 == v2, eidx, EXPERTS_PER_GROUP), axis=0, keepdims=True)
    e2 = jnp.exp(v2 - v1)
    w1 = g_weight / (1.0 + e2)
    w2 = g_weight * e2 / (1.0 + e2)
    return g_idx, i1, i2, w1, w2


OUTPROJ_PARTS = 2
OUTPROJ_TILES_PER_STEP = 2


def _outproj_kernel(ret_ref, diff_ref, x_ref, mod_ref, gain_ref, wo_ref, wr_ref, br_ref, tri_ref,
                    x1_ref, h2_ref, ri_ref, rw_ref, cnt_ref, *, tiles_per_batch):
    b = pl.program_id(0) // tiles_per_batch
    gate1 = mod_ref[pl.ds(b, 1), 2 * D_MODEL:3 * D_MODEL]
    shift = mod_ref[pl.ds(b, 1), 3 * D_MODEL:4 * D_MODEL]
    scale = mod_ref[pl.ds(b, 1), 4 * D_MODEL:5 * D_MODEL]
    wr = wr_ref[...]
    rows = x_ref.shape[0]
    tm = tri_ref.shape[0]
    n_parts = OUTPROJ_PARTS * rows // tm
    parts = [slice(n * rows // n_parts, (n + 1) * rows // n_parts) for n in range(n_parts)]
    mix = [_dot(jnp.concatenate([ret_ref[r, :], diff_ref[r, :]], axis=1), wo_ref[...]) for r in parts]
    for r, m in zip(parts, mix):
        x1_ref[r, :] = x_ref[r, :] + gate1 * m
    h_split = [_split_bf16(_norm_modulate(x1_ref[r, :], gain_ref[...], shift, scale)) for r in parts]
    for r, (h_hi, _) in zip(parts, h_split):
        h2_ref[r, :] = h_hi
    by_hi = [_dot_nt(wr, h_hi) for h_hi, _ in h_split]
    by_lo = [_dot_nt(wr[:ROUTER_ROWS], h_lo) for _, h_lo in h_split]
    logits = [a[:ROUTER_ROWS] + a[ROUTER_ROWS:] + c + br_ref[...] for a, c in zip(by_hi, by_lo)]
    routed = [_route(lg) for lg in logits]
    g_idx, i1, i2, w1, w2 = [jnp.concatenate([rt[n] for rt in routed], axis=1) for n in range(5)]
    logits = jnp.concatenate(logits, axis=1)
    e1 = g_idx * EXPERTS_PER_GROUP + i1
    e2 = g_idx * EXPERTS_PER_GROUP + i2
    eidx = lax.broadcasted_iota(jnp.int32, (N_EXPERTS, logits.shape[1]), 0)
    hit1 = eidx == e1
    hit2 = eidx == e2
    onehot = jnp.where(hit1 | hit2, 1.0, 0.0)
    tiles = [slice(n * tm, (n + 1) * tm) for n in range(rows // tm)]
    before = jnp.concatenate([_dot(onehot[:, t].astype(BF16), tri_ref[...]) for t in tiles], axis=1)
    r1 = jnp.sum(jnp.where(hit1, before, 0.0), axis=0, keepdims=True)
    r2 = jnp.sum(jnp.where(hit2, before, 0.0), axis=0, keepdims=True)
    zi = jnp.zeros_like(e1)
    ri_ref[...] = jnp.concatenate([e1, e2, r1.astype(jnp.int32), r2.astype(jnp.int32), zi, zi, zi, zi], axis=0)
    zf = jnp.zeros_like(w1)
    rw_ref[...] = jnp.concatenate([w1, w2, zf, zf, zf, zf, zf, zf], axis=0)
    for n, t in enumerate(tiles):
        counts = jnp.sum(onehot[:, t], axis=1, keepdims=True)
        cnt_ref[n] = jnp.broadcast_to(counts, (N_EXPERTS, LANES)).astype(jnp.int32)


def _outproj(ret_out, diff_out, x2, mod, gain, w_out, wr, br, seq, tm):
    tokens = x2.shape[0]
    n_tiles = tokens // tm
    per_step = OUTPROJ_TILES_PER_STEP if seq % (OUTPROJ_TILES_PER_STEP * tm) == 0 else 1
    rows = per_step * tm
    tri = jnp.asarray(np.arange(tm)[:, None] < np.arange(tm)[None, :], BF16)
    tok_spec = lambda w: pl.BlockSpec((rows, w), lambda i: (i, 0))
    row_spec = pl.BlockSpec((8, rows), lambda i: (0, i))
    full = lambda a: pl.BlockSpec(a.shape, lambda i: (0,) * a.ndim)
    return pl.pallas_call(
        functools.partial(_outproj_kernel, tiles_per_batch=seq // rows),
        grid=(tokens // rows,),
        in_specs=[tok_spec(RET_WIDTH), tok_spec(DIFF_WIDTH), tok_spec(D_MODEL), full(mod), full(gain),
                  full(w_out), full(wr), full(br), full(tri)],
        out_specs=[tok_spec(D_MODEL), tok_spec(D_MODEL), row_spec, row_spec,
                   pl.BlockSpec((per_step, N_EXPERTS, LANES), lambda i: (i, 0, 0))],
        out_shape=[jax.ShapeDtypeStruct((tokens, D_MODEL), F32),
                   jax.ShapeDtypeStruct((tokens, D_MODEL), BF16),
                   jax.ShapeDtypeStruct((8, tokens), jnp.int32),
                   jax.ShapeDtypeStruct((8, tokens), F32),
                   jax.ShapeDtypeStruct((n_tiles, N_EXPERTS, LANES), jnp.int32)],
        compiler_params=pltpu.CompilerParams(
            dimension_semantics=("parallel",), vmem_limit_bytes=VMEM_LIMIT),
        name="outproj",
    )(ret_out, diff_out, x2, mod, gain, w_out, wr, br, tri)


CHUNK = 8
BIG_PIECE = 4 * CHUNK
MAX_SMALL_PIECES = N_EXPERTS * (BIG_PIECE // CHUNK - 1)
PIECE_TABLES = ("n_big", "big_src", "big_dst", "n_small", "small_src", "small_dst")
DISPATCH_TILES_PER_STEP = 2
COMBINE_TILES_PER_STEP = 2
TMX = 512


def _local_rows(tm):
    rows = 2 * tm + N_EXPERTS * (CHUNK - 1)
    return (rows + 15) // 16 * 16


def _max_big_pieces(tm):
    return _local_rows(tm) // BIG_PIECE


def _sorted_rows_alloc(tokens, tm):
    worst = 2 * tokens + (tokens // tm) * N_EXPERTS * (CHUNK - 1) + N_EXPERTS * (TMX - CHUNK)
    return (worst + TMX - 1) // TMX * TMX


def _dispatch_plan(cnt, tokens, tm):
    i32 = jnp.int32
    pad = (cnt + CHUNK - 1) // CHUNK * CHUNK
    local_end = jnp.cumsum(pad, axis=1)
    local_start = local_end - pad
    seg_rows = jnp.sum(pad, axis=0)
    seg_pad = (seg_rows + TMX - 1) // TMX * TMX
    seg_end = jnp.cumsum(seg_pad)
    seg_start = seg_end - seg_pad
    run_dst = seg_start[None, :] + jnp.cumsum(pad, axis=0) - pad

    def pieces(count, offset, size, max_n):
        end = jnp.cumsum(count, axis=1)
        start = end - count
        k = jnp.arange(max_n, dtype=i32)[None, :, None]
        owns = (start[:, None, :] <= k) & (k < end[:, None, :])
        within = size * (k - start[:, None, :]) + offset[:, None, :]
        src = jnp.sum(jnp.where(owns, local_start[:, None, :] + within, 0), axis=-1)
        dst = jnp.sum(jnp.where(owns, run_dst[:, None, :] + within, 0), axis=-1)
        return end[:, -1].astype(i32), src.reshape(-1).astype(i32), dst.reshape(-1).astype(i32)

    n_big, big_src, big_dst = pieces(pad // BIG_PIECE, jnp.zeros_like(pad), BIG_PIECE, _max_big_pieces(tm))
    n_small, small_src, small_dst = pieces(pad % BIG_PIECE // CHUNK, pad // BIG_PIECE * BIG_PIECE, CHUNK,
                                           MAX_SMALL_PIECES)
    m = TMX * jnp.arange(_sorted_rows_alloc(tokens, tm) // TMX, dtype=i32)
    tile_expert = jnp.minimum(jnp.sum(seg_end[None, :] <= m[:, None], axis=-1), N_EXPERTS - 1)
    towns = (seg_start[None, :] <= m[:, None]) & (m[:, None] < seg_end[None, :])
    used = seg_pad > 0
    parity = (jnp.cumsum(used) - used) % 2
    eids = jnp.arange(N_EXPERTS, dtype=i32)
    later_used = (eids[None, :] > eids[:, None]) & used[None, :]
    next_used = jnp.min(jnp.where(later_used, eids[None, :], N_EXPERTS), axis=1)
    next_used = jnp.where(next_used == N_EXPERTS, -1, next_used)
    pick = lambda per_expert: jnp.sum(jnp.where(towns, per_expert[None, :], 0), axis=-1)
    tile_first = jnp.sum(jnp.where(towns & (seg_start[None, :] == m[:, None]), 1, 0), axis=-1)
    tile_next = jnp.where(jnp.any(towns, axis=-1), pick(next_used), -1)
    return dict(
        tile_first=tile_first.astype(i32),
        tile_slot=pick(parity).astype(i32),
        tile_next=tile_next.astype(i32),
        tile_rows=jnp.clip(pick(seg_start + seg_rows) - m, 0, TMX).astype(i32),
        local_start=local_start.reshape(-1).astype(i32),
        n_big=n_big, big_src=big_src, big_dst=big_dst,
        n_small=n_small, small_src=small_src, small_dst=small_dst,
        tail_base=(seg_start + seg_rows).astype(i32),
        tail_rows=(seg_pad - seg_rows).astype(i32),
        tile_expert=tile_expert.astype(i32),
        n_used=(seg_end[-1:] // TMX).astype(i32),
    )


WAIT_UNROLL = 8


def _wait_times(copy, n):
    lax.fori_loop(0, n // WAIT_UNROLL, lambda i, c: ([copy.wait() for _ in range(WAIT_UNROLL)], c)[1], 0)
    lax.fori_loop(0, n % WAIT_UNROLL, lambda i, c: (copy.wait(), c)[1], 0)


def _for_each(n, body, unroll=4):
    main = n // unroll
    lax.fori_loop(0, main, lambda i, c: ([body(i * unroll + u) for u in range(unroll)], c)[1], 0)
    lax.fori_loop(main * unroll, n, lambda j, c: (body(j), c)[1], 0)


def _local_slots(ri_ref, local_start_ref, tile, cols=slice(None)):
    e1, e2 = ri_ref[0:1, cols], ri_ref[1:2, cols]
    s1, s2 = ri_ref[2:3, cols], ri_ref[3:4, cols]
    for e in range(N_EXPERTS):
        start = local_start_ref[tile * N_EXPERTS + e]
        s1 = s1 + jnp.where(e1 == e, start, 0)
        s2 = s2 + jnp.where(e2 == e, start, 0)
    return s1, s2


def _run_pieces(piece_refs, local_ref, sorted_ref, sem_ref, to_sorted):
    n_big_ref, big_src_ref, big_dst_ref, n_small_ref, small_src_ref, small_dst_ref = piece_refs
    r_loc = local_ref.shape[1]
    kinds = [(BIG_PIECE, n_big_ref, big_src_ref, big_dst_ref, r_loc // BIG_PIECE),
             (CHUNK, n_small_ref, small_src_ref, small_dst_ref, MAX_SMALL_PIECES)]

    def copy(sl, size, local_row, sorted_row):
        local = local_ref.at[sl, pl.ds(pl.multiple_of(local_row, CHUNK), size), :]
        srt = sorted_ref.at[pl.ds(pl.multiple_of(sorted_row, CHUNK), size), :]
        return pltpu.make_async_copy(local, srt, sem_ref.at[sl]) if to_sorted else \
            pltpu.make_async_copy(srt, local, sem_ref.at[sl])

    def start(tile, sl):
        for size, n_ref, src_ref, dst_ref, max_n in kinds:
            _for_each(n_ref[tile], lambda k: copy(sl, size, src_ref[tile * max_n + k], dst_ref[tile * max_n + k])
                      .start())

    def wait(tile, sl):
        for size, n_ref, _, _, _ in kinds:
            _wait_times(copy(sl, size, 0, 0), n_ref[tile])

    return start, wait


def _dispatch_kernel(local_start_ref, n_big_ref, big_src_ref, big_dst_ref, n_small_ref, small_src_ref, small_dst_ref,
                     tail_base_ref, tail_rows_ref, n_used_ref,
                     h_ref, ri_ref, xs_ref, buf_ref, zero_ref, sem_ref, tail_sem_ref, *, r_loc):
    b = pl.program_id(0)
    nb = pl.num_programs(0)
    per_step = buf_ref.shape[0] // 2
    tm = h_ref.shape[0] // per_step
    gen = b % 2
    start_runs, drain_tile = _run_pieces(
        (n_big_ref, big_src_ref, big_dst_ref, n_small_ref, small_src_ref, small_dst_ref), buf_ref, xs_ref, sem_ref,
        to_sorted=True)

    def drain(step, g):
        for t in range(per_step):
            drain_tile(step * per_step + t, g * per_step + t)

    @pl.when(b >= 2)
    def _():
        drain(b - 2, gen)

    tiles = [b * per_step + t for t in range(per_step)]
    cols = [slice(t * tm, (t + 1) * tm) for t in range(per_step)]
    slots = [_local_slots(ri_ref, local_start_ref, tile, c) for tile, c in zip(tiles, cols)]
    rows = lax.broadcasted_iota(jnp.int32, (r_loc, tm), 0)
    perms = [jnp.where((rows == s1) | (rows == s2), 1.0, 0.0).astype(BF16) for s1, s2 in slots]
    for t, (perm, c) in enumerate(zip(perms, cols)):
        buf_ref[gen * per_step + t] = _dot(perm, h_ref[c, :])
    for t, tile in enumerate(tiles):
        start_runs(tile, gen * per_step + t)

    def tail_pieces(e, act):
        n = tail_rows_ref[e]
        size = TMX // 2
        while size >= CHUNK:
            dst = pl.multiple_of(tail_base_ref[e] + (n & (-2 * size)), CHUNK)
            cp = pltpu.make_async_copy(zero_ref.at[pl.ds(0, size), :], xs_ref.at[pl.ds(dst, size), :],
                                       tail_sem_ref.at[0])
            pl.when((n & size) != 0)(functools.partial(act, cp))
            size //= 2

    def unused_tile_copy(m):
        dst = pl.multiple_of(m * TMX, TMX)
        return pltpu.make_async_copy(zero_ref, xs_ref.at[pl.ds(dst, TMX), :], tail_sem_ref.at[1])

    n_alloc = xs_ref.shape[0] // TMX

    @pl.when(b == 0)
    def _():
        zero_ref[...] = jnp.zeros_like(zero_ref)
        lax.fori_loop(0, N_EXPERTS, lambda e, c: (tail_pieces(e, lambda cp: cp.start()), c)[1], 0)
        lax.fori_loop(n_used_ref[0], n_alloc, lambda m, c: (unused_tile_copy(m).start(), c)[1], 0)

    @pl.when(b == nb - 1)
    def _():
        lax.fori_loop(0, N_EXPERTS, lambda e, c: (tail_pieces(e, lambda cp: cp.wait()), c)[1], 0)
        lax.fori_loop(n_used_ref[0], n_alloc, lambda m, c: (unused_tile_copy(m).wait(), c)[1], 0)

        @pl.when(b >= 1)
        def _():
            drain(b - 1, 1 - gen)

        drain(b, gen)


def _dispatch(h2, ri, plan, tm):
    tokens = h2.shape[0]
    r_loc = _local_rows(tm)
    prefetch = [plan["local_start"]] + [plan[k] for k in PIECE_TABLES] + [
        plan["tail_base"], plan["tail_rows"], plan["n_used"]]
    per_step = DISPATCH_TILES_PER_STEP if (tokens // tm) % DISPATCH_TILES_PER_STEP == 0 else 1
    rows = per_step * tm
    grid_spec = pltpu.PrefetchScalarGridSpec(
        num_scalar_prefetch=len(prefetch),
        grid=(tokens // rows,),
        in_specs=[pl.BlockSpec((rows, D_MODEL), lambda i, *_: (i, 0)),
                  pl.BlockSpec((8, rows), lambda i, *_: (0, i))],
        out_specs=pl.BlockSpec(memory_space=pl.ANY),
        scratch_shapes=[pltpu.VMEM((2 * per_step, r_loc, D_MODEL), F32), pltpu.VMEM((TMX, D_MODEL), F32),
                        pltpu.SemaphoreType.DMA((2 * per_step,)), pltpu.SemaphoreType.DMA((2,))],
    )
    return pl.pallas_call(
        functools.partial(_dispatch_kernel, r_loc=r_loc),
        grid_spec=grid_spec,
        out_shape=jax.ShapeDtypeStruct((_sorted_rows_alloc(tokens, tm), D_MODEL), F32),
        compiler_params=pltpu.CompilerParams(
            dimension_semantics=("arbitrary",), vmem_limit_bytes=VMEM_LIMIT),
        name="dispatch",
    )(*prefetch, h2, ri)


def _experts_kernel(tile_expert_ref, n_used_ref, first_ref, slot_ref, next_ref, rows_ref, xs_ref, wg_hbm, wu_hbm, wd_hbm,
                    ys_ref, wg_st, wu_st, wd_st, wg_bf, wu_bf, wd_bf, a_ref, u_ref, sem_ref):
    m = pl.program_id(0)

    def weight_copies(e, s):
        return [pltpu.make_async_copy(src.at[e], dst.at[s], sem_ref.at[s, n])
                for n, (src, dst) in enumerate([(wg_hbm, wg_st), (wu_hbm, wu_st), (wd_hbm, wd_st)])]

    @pl.when(m < n_used_ref[0])
    def _():
        @pl.when(first_ref[m] == 1)
        def _():
            s = slot_ref[m]

            @pl.when(m == 0)
            def _():
                for cp in weight_copies(tile_expert_ref[0], 0):
                    cp.start()

            for cp in weight_copies(tile_expert_ref[m], s):
                cp.wait()

            @pl.when(next_ref[m] >= 0)
            def _():
                for cp in weight_copies(next_ref[m], 1 - s):
                    cp.start()

            wg_bf[...] = wg_st[s].astype(BF16)
            wu_bf[...] = wu_st[s].astype(BF16)
            wd_bf[...] = wd_st[s].astype(BF16)

        def mlp(rows):
            x = xs_ref[rows, :].astype(BF16)
            a_ref[rows, :] = _dot(x, wg_bf[...])
            u_ref[rows, :] = _dot(x, wu_bf[...])
            hid = (_silu(a_ref[rows, :]) * u_ref[rows, :]).astype(BF16)
            ys_ref[rows, :] = _dot(hid, wd_bf[...])

        half = TMX // 2

        @pl.when(rows_ref[m] > half)
        def _():
            mlp(slice(0, TMX))

        @pl.when(rows_ref[m] <= half)
        def _():
            mlp(slice(0, half))
            ys_ref[half:, :] = jnp.zeros((TMX - half, D_MODEL), F32)


def _experts(xs, plan, wg, wu, wd):
    n_tiles = xs.shape[0] // TMX
    last_used = lambda m, n_used: jnp.minimum(m, n_used[0] - 1)
    row_spec = pl.BlockSpec((TMX, D_MODEL), lambda m, te, nu, *_: (last_used(m, nu), 0))
    hbm = pl.BlockSpec(memory_space=pl.ANY)
    up_shape, down_shape = (D_MODEL, D_EXPERT), (D_EXPERT, D_MODEL)
    grid_spec = pltpu.PrefetchScalarGridSpec(
        num_scalar_prefetch=6,
        grid=(n_tiles,),
        in_specs=[row_spec, hbm, hbm, hbm],
        out_specs=row_spec,
        scratch_shapes=[pltpu.VMEM((2,) + up_shape, F32), pltpu.VMEM((2,) + up_shape, F32),
                        pltpu.VMEM((2,) + down_shape, F32),
                        pltpu.VMEM(up_shape, BF16), pltpu.VMEM(up_shape, BF16), pltpu.VMEM(down_shape, BF16),
                        pltpu.VMEM((TMX, D_EXPERT), F32), pltpu.VMEM((TMX, D_EXPERT), F32),
                        pltpu.SemaphoreType.DMA((2, 3))],
    )
    return pl.pallas_call(
        _experts_kernel,
        grid_spec=grid_spec,
        out_shape=jax.ShapeDtypeStruct(xs.shape, F32),
        input_output_aliases={6: 0},
        compiler_params=pltpu.CompilerParams(
            dimension_semantics=("arbitrary",), vmem_limit_bytes=VMEM_LIMIT),
        name="experts",
    )(plan["tile_expert"], plan["n_used"], plan["tile_first"], plan["tile_slot"], plan["tile_next"], plan["tile_rows"],
      xs, wg, wu, wd)


def _combine_kernel(local_start_ref, n_big_ref, big_src_ref, big_dst_ref, n_small_ref, small_src_ref, small_dst_ref,
                    ys_ref, ri_ref, rw_ref, x1_ref, mod_ref, gain_ref, o_ref, buf_ref, sem_ref,
                    *, r_loc, tiles_per_batch):
    b = pl.program_id(0)
    nb = pl.num_programs(0)
    per_step = buf_ref.shape[0] // 2
    tm = x1_ref.shape[0] // per_step
    gen = b % 2
    fetch_tile, wait_tile = _run_pieces(
        (n_big_ref, big_src_ref, big_dst_ref, n_small_ref, small_src_ref, small_dst_ref), buf_ref, ys_ref, sem_ref,
        to_sorted=False)

    def fetch(step, g):
        for t in range(per_step):
            fetch_tile(step * per_step + t, g * per_step + t)

    @pl.when(b == 0)
    def _():
        buf_ref[...] = jnp.zeros_like(buf_ref)
        fetch(0, 0)

    @pl.when(b + 1 < nb)
    def _():
        fetch(b + 1, 1 - gen)

    for t in range(per_step):
        wait_tile(b * per_step + t, gen * per_step + t)

    tiles = [b * per_step + t for t in range(per_step)]
    cols = [slice(t * tm, (t + 1) * tm) for t in range(per_step)]
    slots = [_local_slots(ri_ref, local_start_ref, tile, c) for tile, c in zip(tiles, cols)]
    rows = lax.broadcasted_iota(jnp.int32, (r_loc, tm), 0)
    hits = [(rows == s1, rows == s2) for s1, s2 in slots]
    w_rows = [jnp.sum(jnp.where(h1, rw_ref[0:1, c], jnp.where(h2, rw_ref[1:2, c], 0.0)), axis=1, keepdims=True)
              for (h1, h2), c in zip(hits, cols)]
    perms = [jnp.where(h1 | h2, 1.0, 0.0).astype(BF16) for h1, h2 in hits]
    yws = [(buf_ref[gen * per_step + t] * w_rows[t]).astype(BF16) for t in range(per_step)]
    moes = [_dot_tn(perm, yw) for perm, yw in zip(perms, yws)]
    batch = b // tiles_per_batch
    gate2 = mod_ref[pl.ds(batch, 1), 5 * D_MODEL:6 * D_MODEL]
    for c, moe in zip(cols, moes):
        x2 = x1_ref[c, :] + gate2 * moe
        ms = jnp.mean(x2 * x2, axis=-1, keepdims=True)
        o_ref[c, :] = x2 * lax.rsqrt(ms + EPS) * gain_ref[...]


def _combine(ys, ri, rw, x1, mod, gain, plan, seq, tm):
    tokens = x1.shape[0]
    r_loc = _local_rows(tm)
    per_step = COMBINE_TILES_PER_STEP if seq % (COMBINE_TILES_PER_STEP * tm) == 0 else 1
    rows = per_step * tm
    row_spec = pl.BlockSpec((8, rows), lambda i, *_: (0, i))
    tok_spec = pl.BlockSpec((rows, D_MODEL), lambda i, *_: (i, 0))
    full = lambda a: pl.BlockSpec(a.shape, lambda i, *_: (0,) * a.ndim)
    prefetch = [plan["local_start"]] + [plan[k] for k in PIECE_TABLES]
    grid_spec = pltpu.PrefetchScalarGridSpec(
        num_scalar_prefetch=len(prefetch),
        grid=(tokens // rows,),
        in_specs=[pl.BlockSpec(memory_space=pl.ANY), row_spec, row_spec, tok_spec, full(mod), full(gain)],
        out_specs=tok_spec,
        scratch_shapes=[pltpu.VMEM((2 * per_step, r_loc, D_MODEL), F32), pltpu.SemaphoreType.DMA((2 * per_step,))],
    )
    return pl.pallas_call(
        functools.partial(_combine_kernel, r_loc=r_loc, tiles_per_batch=seq // rows),
        grid_spec=grid_spec,
        out_shape=jax.ShapeDtypeStruct((tokens, D_MODEL), F32),
        compiler_params=pltpu.CompilerParams(
            dimension_semantics=("arbitrary",), vmem_limit_bytes=VMEM_LIMIT),
        name="combine",
    )(*prefetch, ys, ri, rw, x1, mod, gain)


def _rotary_tables(seq):
    half = RET_HEAD_DIM // 2
    inv_freq = 1.0 / (ROPE_BASE ** (np.arange(half, dtype=np.float64) / half))
    ang = np.arange(seq, dtype=np.float64)[:, None] * inv_freq[None, :]
    cos = np.cos(ang)
    sin = np.sin(ang)
    f32 = lambda a: jnp.asarray(a.astype(np.float32))
    return f32(np.tile(cos, (1, 4))), f32(np.concatenate([-sin, sin, -sin, sin], axis=1))


def _pick_tile(n, pref):
    t = min(n, pref)
    assert n % t == 0, (n, t)
    return t


def kernel(x, c, ada_w, ada_b, norm1_gain, norm2_gain, w_in, w_out, ret_gn_gain, lam_q1, lam_k1, lam_q2,
           lam_k2, diff_subln_gain, w_group, b_group, w_expert, b_expert, w_gate, w_up, w_down, final_gain):
    batch, seq, d = x.shape
    assert d == D_MODEL and batch <= 8 and ada_w.shape[0] == 1
    layer = 0
    lambda_init = 0.8 - 0.6 * math.exp(-0.3 * layer)
    tokens = batch * seq
    x2 = x.reshape(tokens, d)
    tm = _pick_tile(seq, 512)

    c_pad = jnp.zeros((8, d), F32).at[:batch].set(c)
    mod = _adaln(c_pad, ada_w[layer], ada_b[layer].reshape(1, -1))

    cos_t, sin_t = _rotary_tables(seq)
    rq, rk, rv, rg, dq, dk, dvt = _inproj(
        x2, mod, norm1_gain[layer].reshape(1, d), w_in[layer].astype(BF16), cos_t, sin_t, seq,
        _pick_tile(seq, 1024))

    ret_out = _retention(rq, rk, rv, rg, ret_gn_gain[layer].reshape(1, RET_WIDTH), batch, seq,
                         _pick_tile(seq, 256))
    diff_out = _diffattn(
        dq, dk, dvt, lam_q1[layer].reshape(1, -1), lam_k1[layer].reshape(1, -1), lam_q2[layer].reshape(1, -1),
        lam_k2[layer].reshape(1, -1), diff_subln_gain[layer].reshape(-1, 1), batch, seq, lambda_init,
        _pick_tile(seq, 2048), 2 * QUERY_CHUNK)

    w_router = jnp.concatenate(
        [w_group[layer].T, jnp.zeros((8 - N_GROUPS, d), F32), w_expert[layer].reshape(d, N_EXPERTS).T], axis=0)
    b_router = jnp.concatenate(
        [b_group[layer], jnp.zeros((8 - N_GROUPS,), F32), b_expert[layer].reshape(N_EXPERTS)]).reshape(-1, 1)
    wr_hi = w_router.astype(BF16)
    wr_lo = (w_router - wr_hi.astype(F32)).astype(BF16)
    x1, h2, ri, rw, cnt = _outproj(ret_out, diff_out, x2, mod, norm2_gain[layer].reshape(1, d),
                                   w_out[layer].astype(BF16), jnp.concatenate([wr_hi, wr_lo], axis=0), b_router,
                                   seq, tm)

    plan = _dispatch_plan(cnt[:, :, 0], tokens, tm)
    xs = _dispatch(h2, ri, plan, tm)
    ys = _experts(xs, plan, w_gate[layer].reshape(N_EXPERTS, d, D_EXPERT),
                  w_up[layer].reshape(N_EXPERTS, d, D_EXPERT), w_down[layer].reshape(N_EXPERTS, D_EXPERT, d))
    out = _combine(ys, ri, rw, x1, mod, final_gain.reshape(1, d), plan, seq, tm)
    return out.reshape(batch, seq, d)
```

```python
import functools
import math

import jax
import jax.numpy as jnp
import numpy as np
from jax import lax
from jax.experimental import pallas as pl
from jax.experimental.pallas import tpu as pltpu

F32 = jnp.float32
BF16 = jnp.bfloat16

D_MODEL = 1024
RET_HEAD_DIM = 64
RET_WIDTH = 512
RET_HEADS = 8
RET_PAIRS = RET_HEADS // 2
DIFF_QK_DIM = 64
DIFF_V_DIM = 128
DIFF_HEADS = 4
DIFF_WIDTH = 512
N_GROUPS = 4
EXPERTS_PER_GROUP = 8
N_EXPERTS = N_GROUPS * EXPERTS_PER_GROUP
D_EXPERT = 512
N_MOD = 6
ROPE_BASE = 10000.0
EPS = 1e-6
LANES = 128
ROUTER_ROWS = 8 + N_EXPERTS
VMEM_LIMIT = 56 * 1024 * 1024


def _dot(a, b):
    return jnp.dot(a, b, preferred_element_type=F32)


def _dot_nt(a, b):
    return lax.dot_general(a, b, (((1,), (1,)), ((), ())), preferred_element_type=F32)


def _dot_tn(a, b):
    return lax.dot_general(a, b, (((0,), (0,)), ((), ())), preferred_element_type=F32)


def _split_bf16(x):
    hi = x.astype(BF16)
    lo = (x - hi.astype(F32)).astype(BF16)
    return hi, lo


def _silu(x):
    return x / (1.0 + jnp.exp(-x))


def _adaln_kernel(c_ref, w_ref, b_ref, o_ref):
    ca = _silu(c_ref[...])
    c_hi, c_lo = _split_bf16(ca)
    w_hi, w_lo = _split_bf16(w_ref[...])
    o_ref[...] = _dot(c_hi, w_hi) + _dot(c_lo, w_hi) + _dot(c_hi, w_lo) + b_ref[...]


def _adaln(c_pad, ada_w, ada_b):
    n_out = ada_w.shape[1]
    tn = D_MODEL
    return pl.pallas_call(
        _adaln_kernel,
        grid=(n_out // tn,),
        in_specs=[
            pl.BlockSpec((8, D_MODEL), lambda j: (0, 0)),
            pl.BlockSpec((D_MODEL, tn), lambda j: (0, j)),
            pl.BlockSpec((1, tn), lambda j: (0, j)),
        ],
        out_specs=pl.BlockSpec((8, tn), lambda j: (0, j)),
        out_shape=jax.ShapeDtypeStruct((8, n_out), F32),
        compiler_params=pltpu.CompilerParams(vmem_limit_bytes=VMEM_LIMIT),
        name="adaln",
    )(c_pad, ada_w, ada_b)


def _norm_modulate(x, gain, shift, scale):
    ms = jnp.mean(x * x, axis=-1, keepdims=True)
    y = x * lax.rsqrt(ms + EPS) * gain
    return y * (1.0 + scale) + shift


def _rotary_slab(x, cos, sin_signed, lane_lo):
    swapped = jnp.where(lane_lo, pltpu.roll(x, 96, 1), pltpu.roll(x, 32, 1))
    return x * cos + swapped * sin_signed


def _inproj_kernel(x_ref, mod_ref, gain_ref, w_ref, cos_ref, sin_ref,
                   rq_ref, rk_ref, rv_ref, rg_ref, dq_ref, dk_ref, dvt_ref, *, tiles_per_batch):
    b = pl.program_id(0) // tiles_per_batch
    shift = mod_ref[pl.ds(b, 1), 0:D_MODEL]
    scale = mod_ref[pl.ds(b, 1), D_MODEL:2 * D_MODEL]
    h = _norm_modulate(x_ref[...], gain_ref[...], shift, scale).astype(BF16)
    cos = cos_ref[...]
    sin = sin_ref[...]
    lane = lax.broadcasted_iota(jnp.int32, cos.shape, 1)
    lane_lo = (lane % 64) < 32

    def proj(chunk):
        return _dot(h, w_ref[:, chunk * RET_WIDTH:(chunk + 1) * RET_WIDTH])

    def rotary(acc, out_ref, post_scale):
        for s in range(RET_WIDTH // LANES):
            sl = slice(s * LANES, (s + 1) * LANES)
            out_ref[:, sl] = (_rotary_slab(acc[:, sl], cos, sin, lane_lo) * post_scale).astype(BF16)

    rotary(proj(0), rq_ref, 1.0)
    rotary(proj(1), rk_ref, RET_HEAD_DIM ** -0.5)
    rv_ref[...] = proj(2).astype(BF16)
    rg_ref[...] = _silu(proj(3)).astype(BF16)
    dq_ref[...] = (proj(4) * (DIFF_QK_DIM ** -0.5 * math.log2(math.e))).astype(BF16)
    dk_ref[...] = proj(5).astype(BF16)
    dvt_ref[...] = proj(6).T.astype(BF16)


def _inproj(x2, mod, gain, w_in, cos_t, sin_t, seq, tm):
    tokens = x2.shape[0]
    tiles_per_batch = seq // tm
    tok_spec = lambda w: pl.BlockSpec((tm, w), lambda i: (i, 0))
    tab_spec = pl.BlockSpec((tm, LANES), lambda i: (i % tiles_per_batch, 0))
    full = lambda a: pl.BlockSpec(a.shape, lambda i: (0,) * a.ndim)
    out = jax.ShapeDtypeStruct((tokens, RET_WIDTH), BF16)
    return pl.pallas_call(
        functools.partial(_inproj_kernel, tiles_per_batch=tiles_per_batch),
        grid=(tokens // tm,),
        in_specs=[tok_spec(D_MODEL), full(mod), full(gain), full(w_in), tab_spec, tab_spec],
        out_specs=[tok_spec(RET_WIDTH)] * 6 + [pl.BlockSpec((DIFF_WIDTH, tm), lambda i: (0, i))],
        out_shape=[out] * 6 + [jax.ShapeDtypeStruct((DIFF_WIDTH, tokens), BF16)],
        compiler_params=pltpu.CompilerParams(
            dimension_semantics=("parallel",), vmem_limit_bytes=VMEM_LIMIT),
        name="inproj",
    )(x2, mod, gain, w_in, cos_t, sin_t)


RET_SEQ_GROUP = 4


def _retention_kernel(q_ref, k_ref, v_ref, g_ref, dec_ref, qdec_ref, kdec_ref, rdec_ref,
                      bmask_ref, gmean_ref, gain_ref, o_ref, state_ref, *, chunk):
    @pl.when(pl.program_id(1) == 0)
    def _():
        state_ref[...] = jnp.zeros_like(state_ref)

    lane = lax.broadcasted_iota(jnp.int32, (chunk, LANES), 1)
    first_head = lane < RET_HEAD_DIM
    gmean = gmean_ref[...]
    bmask = bmask_ref[...]
    units = [(s, p) for s in range(q_ref.shape[0]) for p in range(RET_PAIRS)]
    sl = lambda p: slice(p * LANES, (p + 1) * LANES)
    q = [q_ref[s, :, sl(p)] for s, p in units]
    k = [k_ref[s, :, sl(p)] for s, p in units]
    v = [v_ref[s, :, sl(p)] for s, p in units]
    zero = jnp.zeros_like(q[0])
    q_stack = [jnp.concatenate([jnp.where(first_head, qu, zero), jnp.where(first_head, zero, qu)], axis=0)
               for qu in q]
    scores = [(_dot_nt(q_stack[n], k[n]) * dec_ref[p]).astype(BF16) for n, (_, p) in enumerate(units)]
    state = [state_ref[n] for n in range(len(units))]
    cross = [_dot(q[n], state[n].astype(BF16)) * qdec_ref[:, sl(p)] for n, (_, p) in enumerate(units)]
    k_dec = [(k[n].astype(F32) * kdec_ref[:, sl(p)]).astype(BF16) for n, (_, p) in enumerate(units)]
    for n, (_, p) in enumerate(units):
        state_ref[n] = state[n] * rdec_ref[p] + _dot_tn(k_dec[n], v[n]) * bmask
    intra2 = [_dot(scores[n], v[n]) for n in range(len(units))]
    y = [jnp.where(first_head, intra2[n][:chunk], intra2[n][chunk:]) + cross[n] for n in range(len(units))]
    seg_mean = lambda x: _dot(jnp.concatenate(_split_bf16(x), axis=1), gmean)
    mu = [seg_mean(yu) for yu in y]
    d = [yu - mu_u for yu, mu_u in zip(y, mu)]
    var = [seg_mean(du * du) for du in d]
    for n, (s, p) in enumerate(units):
        yn = d[n] * lax.rsqrt(var[n] + EPS) * gain_ref[:, sl(p)]
        o_ref[s, :, sl(p)] = (g_ref[s, :, sl(p)].astype(F32) * yn).astype(BF16)


def _retention_tables(chunk):
    heads = np.arange(RET_HEADS, dtype=np.float64)
    log_gamma = np.log(1.0 - np.exp2(-5.0 - heads))
    idx = np.arange(chunk)
    rel = (idx[:, None] - idx[None, :]).astype(np.float64)
    decay = np.where(rel[None] >= 0, np.exp(log_gamma[:, None, None] * np.maximum(rel, 0.0)[None]), 0.0)
    dec2 = decay.reshape(RET_PAIRS, 2 * chunk, chunk)
    lane_lg = np.repeat(log_gamma, RET_HEAD_DIM)
    qdec = np.exp(lane_lg[None, :] * (idx + 1)[:, None])
    kdec = np.exp(lane_lg[None, :] * (chunk - 1 - idx)[:, None])
    rdec = np.exp(lane_lg * chunk).reshape(RET_PAIRS, LANES, 1) * np.ones((1, 1, LANES))
    blk = np.arange(LANES) // RET_HEAD_DIM
    bmask = (blk[:, None] == blk[None, :]).astype(np.float64)
    f32 = lambda a: jnp.asarray(a.astype(np.float32))
    gmean2 = np.concatenate([bmask, bmask], axis=0) / RET_HEAD_DIM
    return f32(dec2), f32(qdec), f32(kdec), f32(rdec), f32(bmask), f32(gmean2).astype(BF16)


def _retention(rq, rk, rv, rg, gn_gain, batch, seq, chunk):
    nc = seq // chunk
    group = RET_SEQ_GROUP if batch % RET_SEQ_GROUP == 0 else 1
    dec2, qdec, kdec, rdec, bmask, gmean = _retention_tables(chunk)
    tok_spec = pl.BlockSpec((group, chunk, RET_WIDTH), lambda b, n: (b, n, 0))
    full = lambda a: pl.BlockSpec(a.shape, lambda b, n: (0,) * a.ndim)
    by_seq = lambda a: a.reshape(batch, seq, RET_WIDTH)
    out = pl.pallas_call(
        functools.partial(_retention_kernel, chunk=chunk),
        grid=(batch // group, nc),
        in_specs=[tok_spec] * 4 + [full(dec2), full(qdec), full(kdec), full(rdec), full(bmask),
                                   full(gmean), full(gn_gain)],
        out_specs=tok_spec,
        out_shape=jax.ShapeDtypeStruct((batch, seq, RET_WIDTH), BF16),
        scratch_shapes=[pltpu.VMEM((group * RET_PAIRS, LANES, LANES), F32)],
        compiler_params=pltpu.CompilerParams(
            dimension_semantics=("parallel", "arbitrary"), vmem_limit_bytes=VMEM_LIMIT),
        name="retention",
    )(by_seq(rq), by_seq(rk), by_seq(rv), by_seq(rg), dec2, qdec, kdec, rdec, bmask, gmean, gn_gain)
    return out.reshape(batch * seq, RET_WIDTH)


NEG_BIG = -1e30


V_EXT_ROWS = DIFF_V_DIM + 16
QUERY_CHUNK = 256
SCORES_AHEAD_FULL = 3
SCORES_AHEAD_DIAG = 3
STAGE_SLOTS = 16


def _diag_chunks(tq, tk, d):
    assert tk == 2 * QUERY_CHUNK
    per_softmax = tq // QUERY_CHUNK
    out = []
    for c in range(2 * per_softmax):
        q0 = (c % per_softmax) * QUERY_CHUNK
        if q0 + QUERY_CHUNK - 1 < d * tk:
            continue
        kind = "full" if q0 >= (d + 1) * tk else ("tri" if q0 == d * tk else "low_tri")
        out.append((c, kind))
    return out


def _accumulate(acc_ref, cs, alpha, pv):
    acc_ref[:, cs] = alpha * acc_ref[:, cs] + pv


def _diffattn_kernel(q_ref, k_ref, vt_ref, lq1_ref, lk1_ref, lq2_ref, lk2_ref, gain_ref, bias_ref, o_ref,
                     qs_ref, vext_ref, m_ref, acc_ref, stage_ref, *, tq, tk, lambda_init):
    i = pl.program_id(2)
    nk = vext_ref.shape[0]

    @pl.when(i == 0)
    def _():
        for j in range(nk):
            vext_ref[j, 0:DIFF_V_DIM, :] = vt_ref[:, j * tk:(j + 1) * tk]
            vext_ref[j, DIFF_V_DIM:V_EXT_ROWS, :] = jnp.ones((V_EXT_ROWS - DIFF_V_DIM, tk), BF16)

    q = q_ref[...]
    lane = lax.broadcasted_iota(jnp.int32, q.shape, 1)
    zero = jnp.zeros_like(q)
    qs_ref[0:tq, :] = jnp.where(lane < DIFF_QK_DIM, q, zero)
    qs_ref[tq:2 * tq, :] = jnp.where(lane < DIFF_QK_DIM, zero, q)
    m_ref[...] = jnp.full_like(m_ref, NEG_BIG)
    acc_ref[...] = jnp.zeros_like(acc_ref)

    def step(work, n_ahead):
        chunk = lambda c: slice(c * QUERY_CHUNK, (c + 1) * QUERY_CHUNK)

        def scores(j, c, kind):
            n_keys = QUERY_CHUNK if kind == "tri" else tk
            start = pl.multiple_of(j * tk, tk)
            return _dot_nt(k_ref[pl.ds(start, n_keys), :], qs_ref[chunk(c), :])

        ahead = [scores(*work[n]) for n in range(min(n_ahead, len(work)))]
        pending = None
        for n, (j, c, kind) in enumerate(work):
            cs = chunk(c)
            st = ahead.pop(0)
            if n + n_ahead < len(work):
                ahead.append(scores(*work[n + n_ahead]))
            slot = n % stage_ref.shape[0]
            n_keys = st.shape[0]
            stage_ref[slot, 0:n_keys, :] = st
            if kind == "full":
                st = stage_ref[slot]
            else:
                causal = stage_ref[slot, n_keys - QUERY_CHUNK:n_keys, :] + bias_ref[...]
                st = causal if kind == "tri" else jnp.concatenate(
                    [stage_ref[slot, 0:n_keys - QUERY_CHUNK, :], causal], axis=0)
            m_old = m_ref[:, cs]
            m_new = jnp.maximum(m_old, jnp.max(st, axis=0, keepdims=True))
            alpha = jnp.exp2(m_old - m_new)
            p = jnp.exp2(st - m_new).astype(BF16)
            m_ref[:, cs] = m_new
            pv = _dot(vext_ref[j, :, 0:st.shape[0]], p)
            if pending is not None:
                pending()
            pending = functools.partial(_accumulate, acc_ref, cs, alpha, pv)
        pending()

    tiles_per_q = tq // tk
    n_chunks = 2 * tq // QUERY_CHUNK

    def full_tiles(it):
        return [(it * tiles_per_q + d, c, "full") for d in range(tiles_per_q) for c in range(n_chunks)]

    lax.fori_loop(0, i, lambda it, c: (step(full_tiles(it), SCORES_AHEAD_FULL), c)[1], 0)
    kind_order = {"full": 0, "low_tri": 1, "tri": 2}
    diag = lambda it: [(it * tiles_per_q + d, c, kind) for d in range(tiles_per_q)
                       for c, kind in sorted(_diag_chunks(tq, tk, d), key=lambda ck: kind_order[ck[1]])]
    lax.fori_loop(i, i + 1, lambda it, c: (step(diag(it), SCORES_AHEAD_DIAG), c)[1], 0)

    lam = (jnp.exp(jnp.sum(lq1_ref[...] * lk1_ref[...], axis=-1, keepdims=True))
           - jnp.exp(jnp.sum(lq2_ref[...] * lk2_ref[...], axis=-1, keepdims=True)) + lambda_init)
    acc = acc_ref[...]
    o2 = acc[0:DIFF_V_DIM, :] * (1.0 / acc[DIFF_V_DIM:DIFF_V_DIM + 1, :])
    ot = o2[:, :tq] - lam * o2[:, tq:]
    ms = jnp.mean(ot * ot, axis=0, keepdims=True)
    ot = ot * lax.rsqrt(ms + EPS) * gain_ref[...] * (1.0 - lambda_init)
    o_ref[...] = ot.T.astype(BF16)


def _diffattn(dq, dk, dvt, lam_q1, lam_k1, lam_q2, lam_k2, gain, batch, seq, lambda_init, tq, tk):
    nq = seq // tq
    q_spec = pl.BlockSpec((tq, LANES), lambda b, h, i: (b * nq + i, h))
    k_spec = pl.BlockSpec((seq, LANES), lambda b, h, i: (b, h))
    vt_spec = pl.BlockSpec((DIFF_V_DIM, seq), lambda b, h, i: (h, b))
    vec = lambda a: pl.BlockSpec(a.shape, lambda b, h, i: (0, 0))
    key = np.arange(QUERY_CHUNK)[:, None]
    query = np.arange(QUERY_CHUNK)[None, :]
    bias = jnp.asarray(np.where(key <= query, 0.0, NEG_BIG), F32)
    return pl.pallas_call(
        functools.partial(_diffattn_kernel, tq=tq, tk=tk, lambda_init=lambda_init),
        grid=(batch, DIFF_HEADS, nq),
        in_specs=[q_spec, k_spec, vt_spec, vec(lam_q1), vec(lam_k1), vec(lam_q2), vec(lam_k2), vec(gain),
                  vec(bias)],
        out_specs=q_spec,
        out_shape=jax.ShapeDtypeStruct(dq.shape, BF16),
        scratch_shapes=[
            pltpu.VMEM((2 * tq, LANES), BF16),
            pltpu.VMEM((seq // tk, V_EXT_ROWS, tk), BF16),
            pltpu.VMEM((1, 2 * tq), F32),
            pltpu.VMEM((V_EXT_ROWS, 2 * tq), F32),
            pltpu.VMEM((STAGE_SLOTS, tk, QUERY_CHUNK), F32),
        ],
        compiler_params=pltpu.CompilerParams(
            dimension_semantics=("parallel", "parallel", "arbitrary"), vmem_limit_bytes=VMEM_LIMIT),
        name="diffattn",
    )(dq, dk, dvt, lam_q1, lam_k1, lam_q2, lam_k2, gain, bias)


def _route(logits):
    r = [logits[g:g + 1, :] for g in range(N_GROUPS)]
    gmax = jnp.maximum(jnp.maximum(r[0], r[1]), jnp.maximum(r[2], r[3]))
    g_idx = jnp.where(r[0] == gmax, 0, jnp.where(r[1] == gmax, 1, jnp.where(r[2] == gmax, 2, 3)))
    denom = sum(jnp.exp(rg - gmax) for rg in r)
    g_weight = 1.0 / denom
    sel = jnp.zeros((EXPERTS_PER_GROUP, logits.shape[1]), F32)
    for g in range(N_GROUPS):
        rows = logits[8 + g * EXPERTS_PER_GROUP:8 + (g + 1) * EXPERTS_PER_GROUP, :]
        sel = jnp.where(g_idx == g, rows, sel)
    eidx = lax.broadcasted_iota(jnp.int32, sel.shape, 0)
    v1 = jnp.max(sel, axis=0, keepdims=True)
    i1 = jnp.min(jnp.where(sel == v1, eidx, EXPERTS_PER_GROUP), axis=0, keepdims=True)
    sel2 = jnp.where(eidx == i1, -jnp.inf, sel)
    v2 = jnp.max(sel2, axis=0, keepdims=True)
    i2 = jnp.min(jnp.where(sel2 == v2, eidx, EXPERTS_PER_GROUP), axis=0, keepdims=True)
    e2 = jnp.exp(v2 - v1)
    w1 = g_weight / (1.0 + e2)
    w2 = g_weight * e2 / (1.0 + e2)
    return g_idx, i1, i2, w1, w2


OUTPROJ_PARTS = 2
OUTPROJ_TILES_PER_STEP = 2


def _outproj_kernel(ret_ref, diff_ref, x_ref, mod_ref, gain_ref, wo_ref, wr_ref, br_ref, tri_ref,
                    x1_ref, h2_ref, ri_ref, rw_ref, cnt_ref, *, tiles_per_batch):
    b = pl.program_id(0) // tiles_per_batch
    gate1 = mod_ref[pl.ds(b, 1), 2 * D_MODEL:3 * D_MODEL]
    shift = mod_ref[pl.ds(b, 1), 3 * D_MODEL:4 * D_MODEL]
    scale = mod_ref[pl.ds(b, 1), 4 * D_MODEL:5 * D_MODEL]
    wr = wr_ref[...]
    rows = x_ref.shape[0]
    tm = tri_ref.shape[0]
    n_parts = OUTPROJ_PARTS * rows // tm
    parts = [slice(n * rows // n_parts, (n + 1) * rows // n_parts) for n in range(n_parts)]
    mix = [_dot(jnp.concatenate([ret_ref[r, :], diff_ref[r, :]], axis=1), wo_ref[...]) for r in parts]
    for r, m in zip(parts, mix):
        x1_ref[r, :] = x_ref[r, :] + gate1 * m
    h_split = [_split_bf16(_norm_modulate(x1_ref[r, :], gain_ref[...], shift, scale)) for r in parts]
    for r, (h_hi, _) in zip(parts, h_split):
        h2_ref[r, :] = h_hi
    by_hi = [_dot_nt(wr, h_hi) for h_hi, _ in h_split]
    by_lo = [_dot_nt(wr[:ROUTER_ROWS], h_lo) for _, h_lo in h_split]
    logits = [a[:ROUTER_ROWS] + a[ROUTER_ROWS:] + c + br_ref[...] for a, c in zip(by_hi, by_lo)]
    routed = [_route(lg) for lg in logits]
    g_idx, i1, i2, w1, w2 = [jnp.concatenate([rt[n] for rt in routed], axis=1) for n in range(5)]
    logits = jnp.concatenate(logits, axis=1)
    e1 = g_idx * EXPERTS_PER_GROUP + i1
    e2 = g_idx * EXPERTS_PER_GROUP + i2
    eidx = lax.broadcasted_iota(jnp.int32, (N_EXPERTS, logits.shape[1]), 0)
    hit1 = eidx == e1
    hit2 = eidx == e2
    onehot = jnp.where(hit1 | hit2, 1.0, 0.0)
    tiles = [slice(n * tm, (n + 1) * tm) for n in range(rows // tm)]
    before = jnp.concatenate([_dot(onehot[:, t].astype(BF16), tri_ref[...]) for t in tiles], axis=1)
    r1 = jnp.sum(jnp.where(hit1, before, 0.0), axis=0, keepdims=True)
    r2 = jnp.sum(jnp.where(hit2, before, 0.0), axis=0, keepdims=True)
    zi = jnp.zeros_like(e1)
    ri_ref[...] = jnp.concatenate([e1, e2, r1.astype(jnp.int32), r2.astype(jnp.int32), zi, zi, zi, zi], axis=0)
    zf = jnp.zeros_like(w1)
    rw_ref[...] = jnp.concatenate([w1, w2, zf, zf, zf, zf, zf, zf], axis=0)
    for n, t in enumerate(tiles):
        counts = jnp.sum(onehot[:, t], axis=1, keepdims=True)
        cnt_ref[n] = jnp.broadcast_to(counts, (N_EXPERTS, LANES)).astype(jnp.int32)


def _outproj(ret_out, diff_out, x2, mod, gain, w_out, wr, br, seq, tm):
    tokens = x2.shape[0]
    n_tiles = tokens // tm
    per_step = OUTPROJ_TILES_PER_STEP if seq % (OUTPROJ_TILES_PER_STEP * tm) == 0 else 1
    rows = per_step * tm
    tri = jnp.asarray(np.arange(tm)[:, None] < np.arange(tm)[None, :], BF16)
    tok_spec = lambda w: pl.BlockSpec((rows, w), lambda i: (i, 0))
    row_spec = pl.BlockSpec((8, rows), lambda i: (0, i))
    full = lambda a: pl.BlockSpec(a.shape, lambda i: (0,) * a.ndim)
    return pl.pallas_call(
        functools.partial(_outproj_kernel, tiles_per_batch=seq // rows),
        grid=(tokens // rows,),
        in_specs=[tok_spec(RET_WIDTH), tok_spec(DIFF_WIDTH), tok_spec(D_MODEL), full(mod), full(gain),
                  full(w_out), full(wr), full(br), full(tri)],
        out_specs=[tok_spec(D_MODEL), tok_spec(D_MODEL), row_spec, row_spec,
                   pl.BlockSpec((per_step, N_EXPERTS, LANES), lambda i: (i, 0, 0))],
        out_shape=[jax.ShapeDtypeStruct((tokens, D_MODEL), F32),
                   jax.ShapeDtypeStruct((tokens, D_MODEL), BF16),
                   jax.ShapeDtypeStruct((8, tokens), jnp.int32),
                   jax.ShapeDtypeStruct((8, tokens), F32),
                   jax.ShapeDtypeStruct((n_tiles, N_EXPERTS, LANES), jnp.int32)],
        compiler_params=pltpu.CompilerParams(
            dimension_semantics=("parallel",), vmem_limit_bytes=VMEM_LIMIT),
        name="outproj",
    )(ret_out, diff_out, x2, mod, gain, w_out, wr, br, tri)


CHUNK = 8
BIG_PIECE = 4 * CHUNK
MAX_SMALL_PIECES = N_EXPERTS * (BIG_PIECE // CHUNK - 1)
PIECE_TABLES = ("n_big", "big_src", "big_dst", "n_small", "small_src", "small_dst")
DISPATCH_TILES_PER_STEP = 2
COMBINE_TILES_PER_STEP = 2
TMX = 256


def _local_rows(tm):
    rows = 2 * tm + N_EXPERTS * (CHUNK - 1)
    return (rows + 15) // 16 * 16


def _max_big_pieces(tm):
    return _local_rows(tm) // BIG_PIECE


def _sorted_rows_alloc(tokens, tm):
    worst = 2 * tokens + (tokens // tm) * N_EXPERTS * (CHUNK - 1) + N_EXPERTS * (TMX - CHUNK)
    return (worst + TMX - 1) // TMX * TMX


def _dispatch_plan(cnt, tokens, tm):
    i32 = jnp.int32
    pad = (cnt + CHUNK - 1) // CHUNK * CHUNK
    local_end = jnp.cumsum(pad, axis=1)
    local_start = local_end - pad
    seg_rows = jnp.sum(pad, axis=0)
    seg_pad = (seg_rows + TMX - 1) // TMX * TMX
    seg_end = jnp.cumsum(seg_pad)
    seg_start = seg_end - seg_pad
    run_dst = seg_start[None, :] + jnp.cumsum(pad, axis=0) - pad

    def pieces(count, offset, size, max_n):
        end = jnp.cumsum(count, axis=1)
        start = end - count
        k = jnp.arange(max_n, dtype=i32)[None, :, None]
        owns = (start[:, None, :] <= k) & (k < end[:, None, :])
        within = size * (k - start[:, None, :]) + offset[:, None, :]
        src = jnp.sum(jnp.where(owns, local_start[:, None, :] + within, 0), axis=-1)
        dst = jnp.sum(jnp.where(owns, run_dst[:, None, :] + within, 0), axis=-1)
        return end[:, -1].astype(i32), src.reshape(-1).astype(i32), dst.reshape(-1).astype(i32)

    n_big, big_src, big_dst = pieces(pad // BIG_PIECE, jnp.zeros_like(pad), BIG_PIECE, _max_big_pieces(tm))
    n_small, small_src, small_dst = pieces(pad % BIG_PIECE // CHUNK, pad // BIG_PIECE * BIG_PIECE, CHUNK,
                                           MAX_SMALL_PIECES)
    m = TMX * jnp.arange(_sorted_rows_alloc(tokens, tm) // TMX, dtype=i32)
    tile_expert = jnp.minimum(jnp.sum(seg_end[None, :] <= m[:, None], axis=-1), N_EXPERTS - 1)
    towns = (seg_start[None, :] <= m[:, None]) & (m[:, None] < seg_end[None, :])
    used = seg_pad > 0
    parity = (jnp.cumsum(used) - used) % 2
    eids = jnp.arange(N_EXPERTS, dtype=i32)
    later_used = (eids[None, :] > eids[:, None]) & used[None, :]
    next_used = jnp.min(jnp.where(later_used, eids[None, :], N_EXPERTS), axis=1)
    next_used = jnp.where(next_used == N_EXPERTS, -1, next_used)
    pick = lambda per_expert: jnp.sum(jnp.where(towns, per_expert[None, :], 0), axis=-1)
    tile_first = jnp.sum(jnp.where(towns & (seg_start[None, :] == m[:, None]), 1, 0), axis=-1)
    tile_next = jnp.where(jnp.any(towns, axis=-1), pick(next_used), -1)
    return dict(
        tile_first=tile_first.astype(i32),
        tile_slot=pick(parity).astype(i32),
        tile_next=tile_next.astype(i32),
        tile_rows=jnp.clip(pick(seg_start + seg_rows) - m, 0, TMX).astype(i32),
        local_start=local_start.reshape(-1).astype(i32),
        n_big=n_big, big_src=big_src, big_dst=big_dst,
        n_small=n_small, small_src=small_src, small_dst=small_dst,
        tail_base=(seg_start + seg_rows).astype(i32),
        tail_rows=(seg_pad - seg_rows).astype(i32),
        tile_expert=tile_expert.astype(i32),
        n_used=(seg_end[-1:] // TMX).astype(i32),
    )


WAIT_UNROLL = 8


def _wait_times(copy, n):
    lax.fori_loop(0, n // WAIT_UNROLL, lambda i, c: ([copy.wait() for _ in range(WAIT_UNROLL)], c)[1], 0)
    lax.fori_loop(0, n % WAIT_UNROLL, lambda i, c: (copy.wait(), c)[1], 0)


def _for_each(n, body, unroll=4):
    main = n // unroll
    lax.fori_loop(0, main, lambda i, c: ([body(i * unroll + u) for u in range(unroll)], c)[1], 0)
    lax.fori_loop(main * unroll, n, lambda j, c: (body(j), c)[1], 0)


def _local_slots(ri_ref, local_start_ref, tile, cols=slice(None)):
    e1, e2 = ri_ref[0:1, cols], ri_ref[1:2, cols]
    s1, s2 = ri_ref[2:3, cols], ri_ref[3:4, cols]
    for e in range(N_EXPERTS):
        start = local_start_ref[tile * N_EXPERTS + e]
        s1 = s1 + jnp.where(e1 == e, start, 0)
        s2 = s2 + jnp.where(e2 == e, start, 0)
    return s1, s2


def _run_pieces(piece_refs, local_ref, sorted_ref, sem_ref, to_sorted):
    n_big_ref, big_src_ref, big_dst_ref, n_small_ref, small_src_ref, small_dst_ref = piece_refs
    r_loc = local_ref.shape[1]
    kinds = [(BIG_PIECE, n_big_ref, big_src_ref, big_dst_ref, r_loc // BIG_PIECE),
             (CHUNK, n_small_ref, small_src_ref, small_dst_ref, MAX_SMALL_PIECES)]

    def copy(sl, size, local_row, sorted_row):
        local = local_ref.at[sl, pl.ds(pl.multiple_of(local_row, CHUNK), size), :]
        srt = sorted_ref.at[pl.ds(pl.multiple_of(sorted_row, CHUNK), size), :]
        return pltpu.make_async_copy(local, srt, sem_ref.at[sl]) if to_sorted else \
            pltpu.make_async_copy(srt, local, sem_ref.at[sl])

    def start(tile, sl):
        for size, n_ref, src_ref, dst_ref, max_n in kinds:
            _for_each(n_ref[tile], lambda k: copy(sl, size, src_ref[tile * max_n + k], dst_ref[tile * max_n + k])
                      .start())

    def wait(tile, sl):
        for size, n_ref, _, _, _ in kinds:
            _wait_times(copy(sl, size, 0, 0), n_ref[tile])

    return start, wait


def _dispatch_kernel(local_start_ref, n_big_ref, big_src_ref, big_dst_ref, n_small_ref, small_src_ref, small_dst_ref,
                     tail_base_ref, tail_rows_ref, n_used_ref,
                     h_ref, ri_ref, xs_ref, buf_ref, zero_ref, sem_ref, tail_sem_ref, *, r_loc):
    b = pl.program_id(0)
    nb = pl.num_programs(0)
    per_step = buf_ref.shape[0] // 2
    tm = h_ref.shape[0] // per_step
    gen = b % 2
    start_runs, drain_tile = _run_pieces(
        (n_big_ref, big_src_ref, big_dst_ref, n_small_ref, small_src_ref, small_dst_ref), buf_ref, xs_ref, sem_ref,
        to_sorted=True)

    def drain(step, g):
        for t in range(per_step):
            drain_tile(step * per_step + t, g * per_step + t)

    @pl.when(b >= 2)
    def _():
        drain(b - 2, gen)

    tiles = [b * per_step + t for t in range(per_step)]
    cols = [slice(t * tm, (t + 1) * tm) for t in range(per_step)]
    slots = [_local_slots(ri_ref, local_start_ref, tile, c) for tile, c in zip(tiles, cols)]
    rows = lax.broadcasted_iota(jnp.int32, (r_loc, tm), 0)
    perms = [jnp.where((rows == s1) | (rows == s2), 1.0, 0.0).astype(BF16) for s1, s2 in slots]
    for t, (perm, c) in enumerate(zip(perms, cols)):
        buf_ref[gen * per_step + t] = _dot(perm, h_ref[c, :])
    for t, tile in enumerate(tiles):
        start_runs(tile, gen * per_step + t)

    def tail_pieces(e, act):
        n = tail_rows_ref[e]
        size = TMX // 2
        while size >= CHUNK:
            dst = pl.multiple_of(tail_base_ref[e] + (n & (-2 * size)), CHUNK)
            cp = pltpu.make_async_copy(zero_ref.at[pl.ds(0, size), :], xs_ref.at[pl.ds(dst, size), :],
                                       tail_sem_ref.at[0])
            pl.when((n & size) != 0)(functools.partial(act, cp))
            size //= 2

    def unused_tile_copy(m):
        dst = pl.multiple_of(m * TMX, TMX)
        return pltpu.make_async_copy(zero_ref, xs_ref.at[pl.ds(dst, TMX), :], tail_sem_ref.at[1])

    n_alloc = xs_ref.shape[0] // TMX

    @pl.when(b == 0)
    def _():
        zero_ref[...] = jnp.zeros_like(zero_ref)
        lax.fori_loop(0, N_EXPERTS, lambda e, c: (tail_pieces(e, lambda cp: cp.start()), c)[1], 0)
        lax.fori_loop(n_used_ref[0], n_alloc, lambda m, c: (unused_tile_copy(m).start(), c)[1], 0)

    @pl.when(b == nb - 1)
    def _():
        lax.fori_loop(0, N_EXPERTS, lambda e, c: (tail_pieces(e, lambda cp: cp.wait()), c)[1], 0)
        lax.fori_loop(n_used_ref[0], n_alloc, lambda m, c: (unused_tile_copy(m).wait(), c)[1], 0)

        @pl.when(b >= 1)
        def _():
            drain(b - 1, 1 - gen)

        drain(b, gen)


def _dispatch(h2, ri, plan, tm):
    tokens = h2.shape[0]
    r_loc = _local_rows(tm)
    prefetch = [plan["local_start"]] + [plan[k] for k in PIECE_TABLES] + [
        plan["tail_base"], plan["tail_rows"], plan["n_used"]]
    per_step = DISPATCH_TILES_PER_STEP if (tokens // tm) % DISPATCH_TILES_PER_STEP == 0 else 1
    rows = per_step * tm
    grid_spec = pltpu.PrefetchScalarGridSpec(
        num_scalar_prefetch=len(prefetch),
        grid=(tokens // rows,),
        in_specs=[pl.BlockSpec((rows, D_MODEL), lambda i, *_: (i, 0)),
                  pl.BlockSpec((8, rows), lambda i, *_: (0, i))],
        out_specs=pl.BlockSpec(memory_space=pl.ANY),
        scratch_shapes=[pltpu.VMEM((2 * per_step, r_loc, D_MODEL), F32), pltpu.VMEM((TMX, D_MODEL), F32),
                        pltpu.SemaphoreType.DMA((2 * per_step,)), pltpu.SemaphoreType.DMA((2,))],
    )
    return pl.pallas_call(
        functools.partial(_dispatch_kernel, r_loc=r_loc),
        grid_spec=grid_spec,
        out_shape=jax.ShapeDtypeStruct((_sorted_rows_alloc(tokens, tm), D_MODEL), F32),
        compiler_params=pltpu.CompilerParams(
            dimension_semantics=("arbitrary",), vmem_limit_bytes=VMEM_LIMIT),
        name="dispatch",
    )(*prefetch, h2, ri)


def _experts_kernel(tile_expert_ref, n_used_ref, first_ref, slot_ref, next_ref, rows_ref, xs_ref, wg_hbm, wu_hbm, wd_hbm,
                    ys_ref, wg_st, wu_st, wd_st, wg_bf, wu_bf, wd_bf, a_ref, u_ref, sem_ref):
    m = pl.program_id(0)

    def weight_copies(e, s):
        return [pltpu.make_async_copy(src.at[e], dst.at[s], sem_ref.at[s, n])
                for n, (src, dst) in enumerate([(wg_hbm, wg_st), (wu_hbm, wu_st), (wd_hbm, wd_st)])]

    @pl.when(m < n_used_ref[0])
    def _():
        @pl.when(first_ref[m] == 1)
        def _():
            s = slot_ref[m]

            @pl.when(m == 0)
            def _():
                for cp in weight_copies(tile_expert_ref[0], 0):
                    cp.start()

            for cp in weight_copies(tile_expert_ref[m], s):
                cp.wait()

            @pl.when(next_ref[m] >= 0)
            def _():
                for cp in weight_copies(next_ref[m], 1 - s):
                    cp.start()

            wg_bf[...] = wg_st[s].astype(BF16)
            wu_bf[...] = wu_st[s].astype(BF16)
            wd_bf[...] = wd_st[s].astype(BF16)

        def mlp(rows):
            x = xs_ref[rows, :].astype(BF16)
            a_ref[rows, :] = _dot(x, wg_bf[...])
            u_ref[rows, :] = _dot(x, wu_bf[...])
            hid = (_silu(a_ref[rows, :]) * u_ref[rows, :]).astype(BF16)
            ys_ref[rows, :] = _dot(hid, wd_bf[...])

        half = TMX // 2

        @pl.when(rows_ref[m] > half)
        def _():
            mlp(slice(0, TMX))

        @pl.when(rows_ref[m] <= half)
        def _():
            mlp(slice(0, half))
            ys_ref[half:, :] = jnp.zeros((TMX - half, D_MODEL), F32)


def _experts(xs, plan, wg, wu, wd):
    n_tiles = xs.shape[0] // TMX
    last_used = lambda m, n_used: jnp.minimum(m, n_used[0] - 1)
    row_spec = pl.BlockSpec((TMX, D_MODEL), lambda m, te, nu, *_: (last_used(m, nu), 0))
    hbm = pl.BlockSpec(memory_space=pl.ANY)
    up_shape, down_shape = (D_MODEL, D_EXPERT), (D_EXPERT, D_MODEL)
    grid_spec = pltpu.PrefetchScalarGridSpec(
        num_scalar_prefetch=6,
        grid=(n_tiles,),
        in_specs=[row_spec, hbm, hbm, hbm],
        out_specs=row_spec,
        scratch_shapes=[pltpu.VMEM((2,) + up_shape, F32), pltpu.VMEM((2,) + up_shape, F32),
                        pltpu.VMEM((2,) + down_shape, F32),
                        pltpu.VMEM(up_shape, BF16), pltpu.VMEM(up_shape, BF16), pltpu.VMEM(down_shape, BF16),
                        pltpu.VMEM((TMX, D_EXPERT), F32), pltpu.VMEM((TMX, D_EXPERT), F32),
                        pltpu.SemaphoreType.DMA((2, 3))],
    )
    return pl.pallas_call(
        _experts_kernel,
        grid_spec=grid_spec,
        out_shape=jax.ShapeDtypeStruct(xs.shape, F32),
        input_output_aliases={6: 0},
        compiler_params=pltpu.CompilerParams(
            dimension_semantics=("arbitrary",), vmem_limit_bytes=VMEM_LIMIT),
        name="experts",
    )(plan["tile_expert"], plan["n_used"], plan["tile_first"], plan["tile_slot"], plan["tile_next"], plan["tile_rows"],
      xs, wg, wu, wd)


def _combine_kernel(local_start_ref, n_big_ref, big_src_ref, big_dst_ref, n_small_ref, small_src_ref, small_dst_ref,
                    ys_ref, ri_ref, rw_ref, x1_ref, mod_ref, gain_ref, o_ref, buf_ref, sem_ref,
                    *, r_loc, tiles_per_batch):
    b = pl.program_id(0)
    nb = pl.num_programs(0)
    per_step = buf_ref.shape[0] // 2
    tm = x1_ref.shape[0] // per_step
    gen = b % 2
    fetch_tile, wait_tile = _run_pieces(
        (n_big_ref, big_src_ref, big_dst_ref, n_small_ref, small_src_ref, small_dst_ref), buf_ref, ys_ref, sem_ref,
        to_sorted=False)

    def fetch(step, g):
        for t in range(per_step):
            fetch_tile(step * per_step + t, g * per_step + t)

    @pl.when(b == 0)
    def _():
        buf_ref[...] = jnp.zeros_like(buf_ref)
        fetch(0, 0)

    @pl.when(b + 1 < nb)
    def _():
        fetch(b + 1, 1 - gen)

    for t in range(per_step):
        wait_tile(b * per_step + t, gen * per_step + t)

    tiles = [b * per_step + t for t in range(per_step)]
    cols = [slice(t * tm, (t + 1) * tm) for t in range(per_step)]
    slots = [_local_slots(ri_ref, local_start_ref, tile, c) for tile, c in zip(tiles, cols)]
    rows = lax.broadcasted_iota(jnp.int32, (r_loc, tm), 0)
    hits = [(rows == s1, rows == s2) for s1, s2 in slots]
    w_rows = [jnp.sum(jnp.where(h1, rw_ref[0:1, c], jnp.where(h2, rw_ref[1:2, c], 0.0)), axis=1, keepdims=True)
              for (h1, h2), c in zip(hits, cols)]
    perms = [jnp.where(h1 | h2, 1.0, 0.0).astype(BF16) for h1, h2 in hits]
    yws = [(buf_ref[gen * per_step + t] * w_rows[t]).astype(BF16) for t in range(per_step)]
    moes = [_dot_tn(perm, yw) for perm, yw in zip(perms, yws)]
    batch = b // tiles_per_batch
    gate2 = mod_ref[pl.ds(batch, 1), 5 * D_MODEL:6 * D_MODEL]
    for c, moe in zip(cols, moes):
        x2 = x1_ref[c, :] + gate2 * moe
        ms = jnp.mean(x2 * x2, axis=-1, keepdims=True)
        o_ref[c, :] = x2 * lax.rsqrt(ms + EPS) * gain_ref[...]


def _combine(ys, ri, rw, x1, mod, gain, plan, seq, tm):
    tokens = x1.shape[0]
    r_loc = _local_rows(tm)
    per_step = COMBINE_TILES_PER_STEP if seq % (COMBINE_TILES_PER_STEP * tm) == 0 else 1
    rows = per_step * tm
    row_spec = pl.BlockSpec((8, rows), lambda i, *_: (0, i))
    tok_spec = pl.BlockSpec((rows, D_MODEL), lambda i, *_: (i, 0))
    full = lambda a: pl.BlockSpec(a.shape, lambda i, *_: (0,) * a.ndim)
    prefetch = [plan["local_start"]] + [plan[k] for k in PIECE_TABLES]
    grid_spec = pltpu.PrefetchScalarGridSpec(
        num_scalar_prefetch=len(prefetch),
        grid=(tokens // rows,),
        in_specs=[pl.BlockSpec(memory_space=pl.ANY), row_spec, row_spec, tok_spec, full(mod), full(gain)],
        out_specs=tok_spec,
        scratch_shapes=[pltpu.VMEM((2 * per_step, r_loc, D_MODEL), F32), pltpu.SemaphoreType.DMA((2 * per_step,))],
    )
    return pl.pallas_call(
        functools.partial(_combine_kernel, r_loc=r_loc, tiles_per_batch=seq // rows),
        grid_spec=grid_spec,
        out_shape=jax.ShapeDtypeStruct((tokens, D_MODEL), F32),
        compiler_params=pltpu.CompilerParams(
            dimension_semantics=("arbitrary",), vmem_limit_bytes=VMEM_LIMIT),
        name="combine",
    )(*prefetch, ys, ri, rw, x1, mod, gain)


def _rotary_tables(seq):
    half = RET_HEAD_DIM // 2
    inv_freq = 1.0 / (ROPE_BASE ** (np.arange(half, dtype=np.float64) / half))
    ang = np.arange(seq, dtype=np.float64)[:, None] * inv_freq[None, :]
    cos = np.cos(ang)
    sin = np.sin(ang)
    f32 = lambda a: jnp.asarray(a.astype(np.float32))
    return f32(np.tile(cos, (1, 4))), f32(np.concatenate([-sin, sin, -sin, sin], axis=1))


def _pick_tile(n, pref):
    t = min(n, pref)
    assert n % t == 0, (n, t)
    return t


def kernel(x, c, ada_w, ada_b, norm1_gain, norm2_gain, w_in, w_out, ret_gn_gain, lam_q1, lam_k1, lam_q2,
           lam_k2, diff_subln_gain, w_group, b_group, w_expert, b_expert, w_gate, w_up, w_down, final_gain):
    batch, seq, d = x.shape
    assert d == D_MODEL and batch <= 8 and ada_w.shape[0] == 1
    layer = 0
    lambda_init = 0.8 - 0.6 * math.exp(-0.3 * layer)
    tokens = batch * seq
    x2 = x.reshape(tokens, d)
    tm = _pick_tile(seq, 512)

    c_pad = jnp.zeros((8, d), F32).at[:batch].set(c)
    mod = _adaln(c_pad, ada_w[layer], ada_b[layer].reshape(1, -1))

    cos_t, sin_t = _rotary_tables(seq)
    rq, rk, rv, rg, dq, dk, dvt = _inproj(
        x2, mod, norm1_gain[layer].reshape(1, d), w_in[layer].astype(BF16), cos_t, sin_t, seq, tm)

    ret_out = _retention(rq, rk, rv, rg, ret_gn_gain[layer].reshape(1, RET_WIDTH), batch, seq,
                         _pick_tile(seq, 256))
    diff_out = _diffattn(
        dq, dk, dvt, lam_q1[layer].reshape(1, -1), lam_k1[layer].reshape(1, -1), lam_q2[layer].reshape(1, -1),
        lam_k2[layer].reshape(1, -1), diff_subln_gain[layer].reshape(-1, 1), batch, seq, lambda_init,
        _pick_tile(seq, 2048), 2 * QUERY_CHUNK)

    w_router = jnp.concatenate(
        [w_group[layer].T, jnp.zeros((8 - N_GROUPS, d), F32), w_expert[layer].reshape(d, N_EXPERTS).T], axis=0)
    b_router = jnp.concatenate(
        [b_group[layer], jnp.zeros((8 - N_GROUPS,), F32), b_expert[layer].reshape(N_EXPERTS)]).reshape(-1, 1)
    wr_hi = w_router.astype(BF16)
    wr_lo = (w_router - wr_hi.astype(F32)).astype(BF16)
    x1, h2, ri, rw, cnt = _outproj(ret_out, diff_out, x2, mod, norm2_gain[layer].reshape(1, d),
                                   w_out[layer].astype(BF16), jnp.concatenate([wr_hi, wr_lo], axis=0), b_router,
                                   seq, tm)

    plan = _dispatch_plan(cnt[:, :, 0], tokens, tm)
    xs = _dispatch(h2, ri, plan, tm)
    ys = _experts(xs, plan, w_gate[layer].reshape(N_EXPERTS, d, D_EXPERT),
                  w_up[layer].reshape(N_EXPERTS, d, D_EXPERT), w_down[layer].reshape(N_EXPERTS, D_EXPERT, d))
    out = _combine(ys, ri, rw, x1, mod, final_gain.reshape(1, d), plan, seq, tm)
    return out.reshape(batch, seq, d)
```

```python
import functools
import math

import jax
import jax.numpy as jnp
import numpy as np
from jax import lax
from jax.experimental import pallas as pl
from jax.experimental.pallas import tpu as pltpu

F32 = jnp.float32
BF16 = jnp.bfloat16

D_MODEL = 1024
RET_HEAD_DIM = 64
RET_WIDTH = 512
RET_HEADS = 8
RET_PAIRS = RET_HEADS // 2
DIFF_QK_DIM = 64
DIFF_V_DIM = 128
DIFF_HEADS = 4
DIFF_WIDTH = 512
N_GROUPS = 4
EXPERTS_PER_GROUP = 8
N_EXPERTS = N_GROUPS * EXPERTS_PER_GROUP
D_EXPERT = 512
N_MOD = 6
ROPE_BASE = 10000.0
EPS = 1e-6
LANES = 128
ROUTER_ROWS = 8 + N_EXPERTS
VMEM_LIMIT = 56 * 1024 * 1024


def _dot(a, b):
    return jnp.dot(a, b, preferred_element_type=F32)


def _dot_nt(a, b):
    return lax.dot_general(a, b, (((1,), (1,)), ((), ())), preferred_element_type=F32)


def _dot_tn(a, b):
    return lax.dot_general(a, b, (((0,), (0,)), ((), ())), preferred_element_type=F32)


def _split_bf16(x):
    hi = x.astype(BF16)
    lo = (x - hi.astype(F32)).astype(BF16)
    return hi, lo


def _silu(x):
    return x / (1.0 + jnp.exp(-x))


def _adaln_kernel(c_ref, w_ref, b_ref, o_ref):
    ca = _silu(c_ref[...])
    c_hi, c_lo = _split_bf16(ca)
    w_hi, w_lo = _split_bf16(w_ref[...])
    o_ref[...] = _dot(c_hi, w_hi) + _dot(c_lo, w_hi) + _dot(c_hi, w_lo) + b_ref[...]


def _adaln(c_pad, ada_w, ada_b):
    n_out = ada_w.shape[1]
    tn = D_MODEL
    return pl.pallas_call(
        _adaln_kernel,
        grid=(n_out // tn,),
        in_specs=[
            pl.BlockSpec((8, D_MODEL), lambda j: (0, 0)),
            pl.BlockSpec((D_MODEL, tn), lambda j: (0, j)),
            pl.BlockSpec((1, tn), lambda j: (0, j)),
        ],
        out_specs=pl.BlockSpec((8, tn), lambda j: (0, j)),
        out_shape=jax.ShapeDtypeStruct((8, n_out), F32),
        compiler_params=pltpu.CompilerParams(vmem_limit_bytes=VMEM_LIMIT),
        name="adaln",
    )(c_pad, ada_w, ada_b)


def _norm_modulate(x, gain, shift, scale):
    ms = jnp.mean(x * x, axis=-1, keepdims=True)
    y = x * lax.rsqrt(ms + EPS) * gain
    return y * (1.0 + scale) + shift


def _rotary_slab(x, cos, sin_signed, lane_lo):
    swapped = jnp.where(lane_lo, pltpu.roll(x, 96, 1), pltpu.roll(x, 32, 1))
    return x * cos + swapped * sin_signed


def _inproj_kernel(x_ref, mod_ref, gain_ref, w_ref, cos_ref, sin_ref,
                   rq_ref, rk_ref, rv_ref, rg_ref, dq_ref, dk_ref, dvt_ref, *, tiles_per_batch):
    b = pl.program_id(0) // tiles_per_batch
    shift = mod_ref[pl.ds(b, 1), 0:D_MODEL]
    scale = mod_ref[pl.ds(b, 1), D_MODEL:2 * D_MODEL]
    h = _norm_modulate(x_ref[...], gain_ref[...], shift, scale).astype(BF16)
    cos = cos_ref[...]
    sin = sin_ref[...]
    lane = lax.broadcasted_iota(jnp.int32, cos.shape, 1)
    lane_lo = (lane % 64) < 32

    def proj(chunk):
        return _dot(h, w_ref[:, chunk * RET_WIDTH:(chunk + 1) * RET_WIDTH])

    def rotary(acc, out_ref, post_scale):
        for s in range(RET_WIDTH // LANES):
            sl = slice(s * LANES, (s + 1) * LANES)
            out_ref[:, sl] = (_rotary_slab(acc[:, sl], cos, sin, lane_lo) * post_scale).astype(BF16)

    rotary(proj(0), rq_ref, 1.0)
    rotary(proj(1), rk_ref, RET_HEAD_DIM ** -0.5)
    rv_ref[...] = proj(2).astype(BF16)
    rg_ref[...] = _silu(proj(3)).astype(BF16)
    dq_ref[...] = (proj(4) * (DIFF_QK_DIM ** -0.5 * math.log2(math.e))).astype(BF16)
    dk_ref[...] = proj(5).astype(BF16)
    dvt_ref[...] = proj(6).T.astype(BF16)


def _inproj(x2, mod, gain, w_in, cos_t, sin_t, seq, tm):
    tokens = x2.shape[0]
    tiles_per_batch = seq // tm
    tok_spec = lambda w: pl.BlockSpec((tm, w), lambda i: (i, 0))
    tab_spec = pl.BlockSpec((tm, LANES), lambda i: (i % tiles_per_batch, 0))
    full = lambda a: pl.BlockSpec(a.shape, lambda i: (0,) * a.ndim)
    out = jax.ShapeDtypeStruct((tokens, RET_WIDTH), BF16)
    return pl.pallas_call(
        functools.partial(_inproj_kernel, tiles_per_batch=tiles_per_batch),
        grid=(tokens // tm,),
        in_specs=[tok_spec(D_MODEL), full(mod), full(gain), full(w_in), tab_spec, tab_spec],
        out_specs=[tok_spec(RET_WIDTH)] * 6 + [pl.BlockSpec((DIFF_WIDTH, tm), lambda i: (0, i))],
        out_shape=[out] * 6 + [jax.ShapeDtypeStruct((DIFF_WIDTH, tokens), BF16)],
        compiler_params=pltpu.CompilerParams(
            dimension_semantics=("parallel",), vmem_limit_bytes=VMEM_LIMIT),
        name="inproj",
    )(x2, mod, gain, w_in, cos_t, sin_t)


RET_SEQ_GROUP = 4


def _retention_kernel(q_ref, k_ref, v_ref, g_ref, dec_ref, qdec_ref, kdec_ref, rdec_ref,
                      bmask_ref, gmean_ref, gain_ref, o_ref, state_ref, *, chunk):
    @pl.when(pl.program_id(1) == 0)
    def _():
        state_ref[...] = jnp.zeros_like(state_ref)

    lane = lax.broadcasted_iota(jnp.int32, (chunk, LANES), 1)
    first_head = lane < RET_HEAD_DIM
    gmean = gmean_ref[...]
    bmask = bmask_ref[...]
    units = [(s, p) for s in range(q_ref.shape[0]) for p in range(RET_PAIRS)]
    sl = lambda p: slice(p * LANES, (p + 1) * LANES)
    q = [q_ref[s, :, sl(p)] for s, p in units]
    k = [k_ref[s, :, sl(p)] for s, p in units]
    v = [v_ref[s, :, sl(p)] for s, p in units]
    zero = jnp.zeros_like(q[0])
    q_stack = [jnp.concatenate([jnp.where(first_head, qu, zero), jnp.where(first_head, zero, qu)], axis=0)
               for qu in q]
    scores = [(_dot_nt(q_stack[n], k[n]) * dec_ref[p]).astype(BF16) for n, (_, p) in enumerate(units)]
    state = [state_ref[n] for n in range(len(units))]
    cross = [_dot(q[n], state[n].astype(BF16)) * qdec_ref[:, sl(p)] for n, (_, p) in enumerate(units)]
    k_dec = [(k[n].astype(F32) * kdec_ref[:, sl(p)]).astype(BF16) for n, (_, p) in enumerate(units)]
    for n, (_, p) in enumerate(units):
        state_ref[n] = state[n] * rdec_ref[p] + _dot_tn(k_dec[n], v[n]) * bmask
    intra2 = [_dot(scores[n], v[n]) for n in range(len(units))]
    y = [jnp.where(first_head, intra2[n][:chunk], intra2[n][chunk:]) + cross[n] for n in range(len(units))]
    seg_mean = lambda x: _dot(jnp.concatenate(_split_bf16(x), axis=1), gmean)
    mu = [seg_mean(yu) for yu in y]
    d = [yu - mu_u for yu, mu_u in zip(y, mu)]
    var = [seg_mean(du * du) for du in d]
    for n, (s, p) in enumerate(units):
        yn = d[n] * lax.rsqrt(var[n] + EPS) * gain_ref[:, sl(p)]
        o_ref[s, :, sl(p)] = (g_ref[s, :, sl(p)].astype(F32) * yn).astype(BF16)


def _retention_tables(chunk):
    heads = np.arange(RET_HEADS, dtype=np.float64)
    log_gamma = np.log(1.0 - np.exp2(-5.0 - heads))
    idx = np.arange(chunk)
    rel = (idx[:, None] - idx[None, :]).astype(np.float64)
    decay = np.where(rel[None] >= 0, np.exp(log_gamma[:, None, None] * np.maximum(rel, 0.0)[None]), 0.0)
    dec2 = decay.reshape(RET_PAIRS, 2 * chunk, chunk)
    lane_lg = np.repeat(log_gamma, RET_HEAD_DIM)
    qdec = np.exp(lane_lg[None, :] * (idx + 1)[:, None])
    kdec = np.exp(lane_lg[None, :] * (chunk - 1 - idx)[:, None])
    rdec = np.exp(lane_lg * chunk).reshape(RET_PAIRS, LANES, 1) * np.ones((1, 1, LANES))
    blk = np.arange(LANES) // RET_HEAD_DIM
    bmask = (blk[:, None] == blk[None, :]).astype(np.float64)
    f32 = lambda a: jnp.asarray(a.astype(np.float32))
    gmean2 = np.concatenate([bmask, bmask], axis=0) / RET_HEAD_DIM
    return f32(dec2), f32(qdec), f32(kdec), f32(rdec), f32(bmask), f32(gmean2).astype(BF16)


def _retention(rq, rk, rv, rg, gn_gain, batch, seq, chunk):
    nc = seq // chunk
    group = RET_SEQ_GROUP if batch % RET_SEQ_GROUP == 0 else 1
    dec2, qdec, kdec, rdec, bmask, gmean = _retention_tables(chunk)
    tok_spec = pl.BlockSpec((group, chunk, RET_WIDTH), lambda b, n: (b, n, 0))
    full = lambda a: pl.BlockSpec(a.shape, lambda b, n: (0,) * a.ndim)
    by_seq = lambda a: a.reshape(batch, seq, RET_WIDTH)
    out = pl.pallas_call(
        functools.partial(_retention_kernel, chunk=chunk),
        grid=(batch // group, nc),
        in_specs=[tok_spec] * 4 + [full(dec2), full(qdec), full(kdec), full(rdec), full(bmask),
                                   full(gmean), full(gn_gain)],
        out_specs=tok_spec,
        out_shape=jax.ShapeDtypeStruct((batch, seq, RET_WIDTH), BF16),
        scratch_shapes=[pltpu.VMEM((group * RET_PAIRS, LANES, LANES), F32)],
        compiler_params=pltpu.CompilerParams(
            dimension_semantics=("parallel", "arbitrary"), vmem_limit_bytes=VMEM_LIMIT),
        name="retention",
    )(by_seq(rq), by_seq(rk), by_seq(rv), by_seq(rg), dec2, qdec, kdec, rdec, bmask, gmean, gn_gain)
    return out.reshape(batch * seq, RET_WIDTH)


NEG_BIG = -1e30


V_EXT_ROWS = DIFF_V_DIM + 16
QUERY_CHUNK = 256
SCORES_AHEAD_FULL = 3
SCORES_AHEAD_DIAG = 3
STAGE_SLOTS = 16


def _diag_chunks(tq, tk, d):
    assert tk == 2 * QUERY_CHUNK
    per_softmax = tq // QUERY_CHUNK
    out = []
    for c in range(2 * per_softmax):
        q0 = (c % per_softmax) * QUERY_CHUNK
        if q0 + QUERY_CHUNK - 1 < d * tk:
            continue
        kind = "full" if q0 >= (d + 1) * tk else ("tri" if q0 == d * tk else "low_tri")
        out.append((c, kind))
    return out


def _accumulate(acc_ref, cs, alpha, pv):
    acc_ref[:, cs] = alpha * acc_ref[:, cs] + pv


def _diffattn_kernel(q_ref, k_ref, vt_ref, lq1_ref, lk1_ref, lq2_ref, lk2_ref, gain_ref, bias_ref, o_ref,
                     qs_ref, vext_ref, m_ref, acc_ref, stage_ref, *, tq, tk, lambda_init):
    i = pl.program_id(2)
    nk = vext_ref.shape[0]

    @pl.when(i == 0)
    def _():
        for j in range(nk):
            vext_ref[j, 0:DIFF_V_DIM, :] = vt_ref[:, j * tk:(j + 1) * tk]
            vext_ref[j, DIFF_V_DIM:V_EXT_ROWS, :] = jnp.ones((V_EXT_ROWS - DIFF_V_DIM, tk), BF16)

    q = q_ref[...]
    lane = lax.broadcasted_iota(jnp.int32, q.shape, 1)
    zero = jnp.zeros_like(q)
    qs_ref[0:tq, :] = jnp.where(lane < DIFF_QK_DIM, q, zero)
    qs_ref[tq:2 * tq, :] = jnp.where(lane < DIFF_QK_DIM, zero, q)
    m_ref[...] = jnp.full_like(m_ref, NEG_BIG)
    acc_ref[...] = jnp.zeros_like(acc_ref)

    def step(work, n_ahead):
        chunk = lambda c: slice(c * QUERY_CHUNK, (c + 1) * QUERY_CHUNK)

        def scores(j, c, kind):
            n_keys = QUERY_CHUNK if kind == "tri" else tk
            start = pl.multiple_of(j * tk, tk)
            return _dot_nt(k_ref[pl.ds(start, n_keys), :], qs_ref[chunk(c), :])

        ahead = [scores(*work[n]) for n in range(min(n_ahead, len(work)))]
        pending = None
        for n, (j, c, kind) in enumerate(work):
            cs = chunk(c)
            st = ahead.pop(0)
            if n + n_ahead < len(work):
                ahead.append(scores(*work[n + n_ahead]))
            slot = n % stage_ref.shape[0]
            n_keys = st.shape[0]
            stage_ref[slot, 0:n_keys, :] = st
            if kind == "full":
                st = stage_ref[slot]
            else:
                causal = stage_ref[slot, n_keys - QUERY_CHUNK:n_keys, :] + bias_ref[...]
                st = causal if kind == "tri" else jnp.concatenate(
                    [stage_ref[slot, 0:n_keys - QUERY_CHUNK, :], causal], axis=0)
            m_old = m_ref[:, cs]
            m_new = jnp.maximum(m_old, jnp.max(st, axis=0, keepdims=True))
            alpha = jnp.exp2(m_old - m_new)
            p = jnp.exp2(st - m_new).astype(BF16)
            m_ref[:, cs] = m_new
            pv = _dot(vext_ref[j, :, 0:st.shape[0]], p)
            if pending is not None:
                pending()
            pending = functools.partial(_accumulate, acc_ref, cs, alpha, pv)
        pending()

    tiles_per_q = tq // tk
    n_chunks = 2 * tq // QUERY_CHUNK

    def full_tiles(it):
        return [(it * tiles_per_q + d, c, "full") for d in range(tiles_per_q) for c in range(n_chunks)]

    lax.fori_loop(0, i, lambda it, c: (step(full_tiles(it), SCORES_AHEAD_FULL), c)[1], 0)
    kind_order = {"full": 0, "low_tri": 1, "tri": 2}
    diag = lambda it: [(it * tiles_per_q + d, c, kind) for d in range(tiles_per_q)
                       for c, kind in sorted(_diag_chunks(tq, tk, d), key=lambda ck: kind_order[ck[1]])]
    lax.fori_loop(i, i + 1, lambda it, c: (step(diag(it), SCORES_AHEAD_DIAG), c)[1], 0)

    lam = (jnp.exp(jnp.sum(lq1_ref[...] * lk1_ref[...], axis=-1, keepdims=True))
           - jnp.exp(jnp.sum(lq2_ref[...] * lk2_ref[...], axis=-1, keepdims=True)) + lambda_init)
    acc = acc_ref[...]
    o2 = acc[0:DIFF_V_DIM, :] * (1.0 / acc[DIFF_V_DIM:DIFF_V_DIM + 1, :])
    ot = o2[:, :tq] - lam * o2[:, tq:]
    ms = jnp.mean(ot * ot, axis=0, keepdims=True)
    ot = ot * lax.rsqrt(ms + EPS) * gain_ref[...] * (1.0 - lambda_init)
    o_ref[...] = ot.T.astype(BF16)


def _diffattn(dq, dk, dvt, lam_q1, lam_k1, lam_q2, lam_k2, gain, batch, seq, lambda_init, tq, tk):
    nq = seq // tq
    q_spec = pl.BlockSpec((tq, LANES), lambda b, h, i: (b * nq + i, h))
    k_spec = pl.BlockSpec((seq, LANES), lambda b, h, i: (b, h))
    vt_spec = pl.BlockSpec((DIFF_V_DIM, seq), lambda b, h, i: (h, b))
    vec = lambda a: pl.BlockSpec(a.shape, lambda b, h, i: (0, 0))
    key = np.arange(QUERY_CHUNK)[:, None]
    query = np.arange(QUERY_CHUNK)[None, :]
    bias = jnp.asarray(np.where(key <= query, 0.0, NEG_BIG), F32)
    return pl.pallas_call(
        functools.partial(_diffattn_kernel, tq=tq, tk=tk, lambda_init=lambda_init),
        grid=(batch, DIFF_HEADS, nq),
        in_specs=[q_spec, k_spec, vt_spec, vec(lam_q1), vec(lam_k1), vec(lam_q2), vec(lam_k2), vec(gain),
                  vec(bias)],
        out_specs=q_spec,
        out_shape=jax.ShapeDtypeStruct(dq.shape, BF16),
        scratch_shapes=[
            pltpu.VMEM((2 * tq, LANES), BF16),
            pltpu.VMEM((seq // tk, V_EXT_ROWS, tk), BF16),
            pltpu.VMEM((1, 2 * tq), F32),
            pltpu.VMEM((V_EXT_ROWS, 2 * tq), F32),
            pltpu.VMEM((STAGE_SLOTS, tk, QUERY_CHUNK), F32),
        ],
        compiler_params=pltpu.CompilerParams(
            dimension_semantics=("parallel", "parallel", "arbitrary"), vmem_limit_bytes=VMEM_LIMIT),
        name="diffattn",
    )(dq, dk, dvt, lam_q1, lam_k1, lam_q2, lam_k2, gain, bias)


def _route(logits):
    r = [logits[g:g + 1, :] for g in range(N_GROUPS)]
    gmax = jnp.maximum(jnp.maximum(r[0], r[1]), jnp.maximum(r[2], r[3]))
    g_idx = jnp.where(r[0] == gmax, 0, jnp.where(r[1] == gmax, 1, jnp.where(r[2] == gmax, 2, 3)))
    denom = sum(jnp.exp(rg - gmax) for rg in r)
    g_weight = 1.0 / denom
    sel = jnp.zeros((EXPERTS_PER_GROUP, logits.shape[1]), F32)
    for g in range(N_GROUPS):
        rows = logits[8 + g * EXPERTS_PER_GROUP:8 + (g + 1) * EXPERTS_PER_GROUP, :]
        sel = jnp.where(g_idx == g, rows, sel)
    eidx = lax.broadcasted_iota(jnp.int32, sel.shape, 0)
    v1 = jnp.max(sel, axis=0, keepdims=True)
    i1 = jnp.min(jnp.where(sel == v1, eidx, EXPERTS_PER_GROUP), axis=0, keepdims=True)
    sel2 = jnp.where(eidx == i1, -jnp.inf, sel)
    v2 = jnp.max(sel2, axis=0, keepdims=True)
    i2 = jnp.min(jnp.where(sel2 == v2, eidx, EXPERTS_PER_GROUP), axis=0, keepdims=True)
    e2 = jnp.exp(v2 - v1)
    w1 = g_weight / (1.0 + e2)
    w2 = g_weight * e2 / (1.0 + e2)
    return g_idx, i1, i2, w1, w2


OUTPROJ_PARTS = 2
OUTPROJ_TILES_PER_STEP = 2


def _outproj_kernel(ret_ref, diff_ref, x_ref, mod_ref, gain_ref, wo_ref, wr_ref, br_ref, tri_ref,
                    x1_ref, h2_ref, ri_ref, rw_ref, cnt_ref, *, tiles_per_batch):
    b = pl.program_id(0) // tiles_per_batch
    gate1 = mod_ref[pl.ds(b, 1), 2 * D_MODEL:3 * D_MODEL]
    shift = mod_ref[pl.ds(b, 1), 3 * D_MODEL:4 * D_MODEL]
    scale = mod_ref[pl.ds(b, 1), 4 * D_MODEL:5 * D_MODEL]
    wr = wr_ref[...]
    rows = x_ref.shape[0]
    tm = tri_ref.shape[0]
    n_parts = OUTPROJ_PARTS * rows // tm
    parts = [slice(n * rows // n_parts, (n + 1) * rows // n_parts) for n in range(n_parts)]
    mix = [_dot(jnp.concatenate([ret_ref[r, :], diff_ref[r, :]], axis=1), wo_ref[...]) for r in parts]
    for r, m in zip(parts, mix):
        x1_ref[r, :] = x_ref[r, :] + gate1 * m
    h_split = [_split_bf16(_norm_modulate(x1_ref[r, :], gain_ref[...], shift, scale)) for r in parts]
    for r, (h_hi, _) in zip(parts, h_split):
        h2_ref[r, :] = h_hi
    by_hi = [_dot_nt(wr, h_hi) for h_hi, _ in h_split]
    by_lo = [_dot_nt(wr[:ROUTER_ROWS], h_lo) for _, h_lo in h_split]
    logits = [a[:ROUTER_ROWS] + a[ROUTER_ROWS:] + c + br_ref[...] for a, c in zip(by_hi, by_lo)]
    routed = [_route(lg) for lg in logits]
    g_idx, i1, i2, w1, w2 = [jnp.concatenate([rt[n] for rt in routed], axis=1) for n in range(5)]
    logits = jnp.concatenate(logits, axis=1)
    e1 = g_idx * EXPERTS_PER_GROUP + i1
    e2 = g_idx * EXPERTS_PER_GROUP + i2
    eidx = lax.broadcasted_iota(jnp.int32, (N_EXPERTS, logits.shape[1]), 0)
    hit1 = eidx == e1
    hit2 = eidx == e2
    onehot = jnp.where(hit1 | hit2, 1.0, 0.0)
    tiles = [slice(n * tm, (n + 1) * tm) for n in range(rows // tm)]
    before = jnp.concatenate([_dot(onehot[:, t].astype(BF16), tri_ref[...]) for t in tiles], axis=1)
    r1 = jnp.sum(jnp.where(hit1, before, 0.0), axis=0, keepdims=True)
    r2 = jnp.sum(jnp.where(hit2, before, 0.0), axis=0, keepdims=True)
    zi = jnp.zeros_like(e1)
    ri_ref[...] = jnp.concatenate([e1, e2, r1.astype(jnp.int32), r2.astype(jnp.int32), zi, zi, zi, zi], axis=0)
    zf = jnp.zeros_like(w1)
    rw_ref[...] = jnp.concatenate([w1, w2, zf, zf, zf, zf, zf, zf], axis=0)
    for n, t in enumerate(tiles):
        counts = jnp.sum(onehot[:, t], axis=1, keepdims=True)
        cnt_ref[n] = jnp.broadcast_to(counts, (N_EXPERTS, LANES)).astype(jnp.int32)


def _outproj(ret_out, diff_out, x2, mod, gain, w_out, wr, br, seq, tm):
    tokens = x2.shape[0]
    n_tiles = tokens // tm
    per_step = OUTPROJ_TILES_PER_STEP if seq % (OUTPROJ_TILES_PER_STEP * tm) == 0 else 1
    rows = per_step * tm
    tri = jnp.asarray(np.arange(tm)[:, None] < np.arange(tm)[None, :], BF16)
    tok_spec = lambda w: pl.BlockSpec((rows, w), lambda i: (i, 0))
    row_spec = pl.BlockSpec((8, rows), lambda i: (0, i))
    full = lambda a: pl.BlockSpec(a.shape, lambda i: (0,) * a.ndim)
    return pl.pallas_call(
        functools.partial(_outproj_kernel, tiles_per_batch=seq // rows),
        grid=(tokens // rows,),
        in_specs=[tok_spec(RET_WIDTH), tok_spec(DIFF_WIDTH), tok_spec(D_MODEL), full(mod), full(gain),
                  full(w_out), full(wr), full(br), full(tri)],
        out_specs=[tok_spec(D_MODEL), tok_spec(D_MODEL), row_spec, row_spec,
                   pl.BlockSpec((per_step, N_EXPERTS, LANES), lambda i: (i, 0, 0))],
        out_shape=[jax.ShapeDtypeStruct((tokens, D_MODEL), F32),
                   jax.ShapeDtypeStruct((tokens, D_MODEL), BF16),
                   jax.ShapeDtypeStruct((8, tokens), jnp.int32),
                   jax.ShapeDtypeStruct((8, tokens), F32),
                   jax.ShapeDtypeStruct((n_tiles, N_EXPERTS, LANES), jnp.int32)],
        compiler_params=pltpu.CompilerParams(
            dimension_semantics=("parallel",), vmem_limit_bytes=VMEM_LIMIT),
        name="outproj",
    )(ret_out, diff_out, x2, mod, gain, w_out, wr, br, tri)


CHUNK = 8
BIG_PIECE = 4 * CHUNK
MAX_SMALL_PIECES = N_EXPERTS * (BIG_PIECE // CHUNK - 1)
PIECE_TABLES = ("n_big", "big_src", "big_dst", "n_small", "small_src", "small_dst")
DISPATCH_TILES_PER_STEP = 2
COMBINE_TILES_PER_STEP = 2
TMX = 512
IN_SLOTS = 3
OUT_SLOTS = 2


def _local_rows(tm):
    rows = 2 * tm + N_EXPERTS * (CHUNK - 1)
    return (rows + 15) // 16 * 16


def _max_big_pieces(tm):
    return _local_rows(tm) // BIG_PIECE


def _sorted_rows_alloc(tokens, tm):
    worst = 2 * tokens + (tokens // tm) * N_EXPERTS * (CHUNK - 1) + N_EXPERTS * (TMX - CHUNK)
    return (worst + TMX - 1) // TMX * TMX


def _dispatch_plan(cnt, tokens, tm):
    i32 = jnp.int32
    pad = (cnt + CHUNK - 1) // CHUNK * CHUNK
    local_end = jnp.cumsum(pad, axis=1)
    local_start = local_end - pad
    seg_rows = jnp.sum(pad, axis=0)
    seg_pad = (seg_rows + TMX - 1) // TMX * TMX
    seg_end = jnp.cumsum(seg_pad)
    seg_start = seg_end - seg_pad
    run_dst = seg_start[None, :] + jnp.cumsum(pad, axis=0) - pad

    def pieces(count, offset, size, max_n):
        end = jnp.cumsum(count, axis=1)
        start = end - count
        k = jnp.arange(max_n, dtype=i32)[None, :, None]
        owns = (start[:, None, :] <= k) & (k < end[:, None, :])
        within = size * (k - start[:, None, :]) + offset[:, None, :]
        src = jnp.sum(jnp.where(owns, local_start[:, None, :] + within, 0), axis=-1)
        dst = jnp.sum(jnp.where(owns, run_dst[:, None, :] + within, 0), axis=-1)
        return end[:, -1].astype(i32), src.reshape(-1).astype(i32), dst.reshape(-1).astype(i32)

    n_big, big_src, big_dst = pieces(pad // BIG_PIECE, jnp.zeros_like(pad), BIG_PIECE, _max_big_pieces(tm))
    n_small, small_src, small_dst = pieces(pad % BIG_PIECE // CHUNK, pad // BIG_PIECE * BIG_PIECE, CHUNK,
                                           MAX_SMALL_PIECES)
    m = TMX * jnp.arange(_sorted_rows_alloc(tokens, tm) // TMX, dtype=i32)
    tile_expert = jnp.minimum(jnp.sum(seg_end[None, :] <= m[:, None], axis=-1), N_EXPERTS - 1)
    towns = (seg_start[None, :] <= m[:, None]) & (m[:, None] < seg_end[None, :])
    used = seg_pad > 0
    parity = (jnp.cumsum(used) - used) % 2
    eids = jnp.arange(N_EXPERTS, dtype=i32)
    later_used = (eids[None, :] > eids[:, None]) & used[None, :]
    next_used = jnp.min(jnp.where(later_used, eids[None, :], N_EXPERTS), axis=1)
    next_used = jnp.where(next_used == N_EXPERTS, -1, next_used)
    pick = lambda per_expert: jnp.sum(jnp.where(towns, per_expert[None, :], 0), axis=-1)
    tile_first = jnp.sum(jnp.where(towns & (seg_start[None, :] == m[:, None]), 1, 0), axis=-1)
    tile_next = jnp.where(jnp.any(towns, axis=-1), pick(next_used), -1)
    return dict(
        tile_first=tile_first.astype(i32),
        tile_slot=pick(parity).astype(i32),
        tile_next=tile_next.astype(i32),
        tile_rows=jnp.clip(pick(seg_start + seg_rows) - m, 0, TMX).astype(i32),
        local_start=local_start.reshape(-1).astype(i32),
        n_big=n_big, big_src=big_src, big_dst=big_dst,
        n_small=n_small, small_src=small_src, small_dst=small_dst,
        tail_base=(seg_start + seg_rows).astype(i32),
        tail_rows=(seg_pad - seg_rows).astype(i32),
        tile_expert=tile_expert.astype(i32),
        n_used=(seg_end[-1:] // TMX).astype(i32),
    )


WAIT_UNROLL = 8


def _wait_times(copy, n):
    lax.fori_loop(0, n // WAIT_UNROLL, lambda i, c: ([copy.wait() for _ in range(WAIT_UNROLL)], c)[1], 0)
    lax.fori_loop(0, n % WAIT_UNROLL, lambda i, c: (copy.wait(), c)[1], 0)


def _for_each(n, body, unroll=4):
    main = n // unroll
    lax.fori_loop(0, main, lambda i, c: ([body(i * unroll + u) for u in range(unroll)], c)[1], 0)
    lax.fori_loop(main * unroll, n, lambda j, c: (body(j), c)[1], 0)


def _local_slots(ri_ref, local_start_ref, tile, cols=slice(None)):
    e1, e2 = ri_ref[0:1, cols], ri_ref[1:2, cols]
    s1, s2 = ri_ref[2:3, cols], ri_ref[3:4, cols]
    for e in range(N_EXPERTS):
        start = local_start_ref[tile * N_EXPERTS + e]
        s1 = s1 + jnp.where(e1 == e, start, 0)
        s2 = s2 + jnp.where(e2 == e, start, 0)
    return s1, s2


def _run_pieces(piece_refs, local_ref, sorted_ref, sem_ref, to_sorted):
    n_big_ref, big_src_ref, big_dst_ref, n_small_ref, small_src_ref, small_dst_ref = piece_refs
    r_loc = local_ref.shape[1]
    kinds = [(BIG_PIECE, n_big_ref, big_src_ref, big_dst_ref, r_loc // BIG_PIECE),
             (CHUNK, n_small_ref, small_src_ref, small_dst_ref, MAX_SMALL_PIECES)]

    def copy(sl, size, local_row, sorted_row):
        local = local_ref.at[sl, pl.ds(pl.multiple_of(local_row, CHUNK), size), :]
        srt = sorted_ref.at[pl.ds(pl.multiple_of(sorted_row, CHUNK), size), :]
        return pltpu.make_async_copy(local, srt, sem_ref.at[sl]) if to_sorted else \
            pltpu.make_async_copy(srt, local, sem_ref.at[sl])

    def start(tile, sl):
        for size, n_ref, src_ref, dst_ref, max_n in kinds:
            _for_each(n_ref[tile], lambda k: copy(sl, size, src_ref[tile * max_n + k], dst_ref[tile * max_n + k])
                      .start())

    def wait(tile, sl):
        for size, n_ref, _, _, _ in kinds:
            _wait_times(copy(sl, size, 0, 0), n_ref[tile])

    return start, wait


def _dispatch_kernel(local_start_ref, n_big_ref, big_src_ref, big_dst_ref, n_small_ref, small_src_ref, small_dst_ref,
                     tail_base_ref, tail_rows_ref, n_used_ref,
                     h_ref, ri_ref, xs_ref, buf_ref, zero_ref, sem_ref, tail_sem_ref, *, r_loc):
    b = pl.program_id(0)
    nb = pl.num_programs(0)
    per_step = buf_ref.shape[0] // 2
    tm = h_ref.shape[0] // per_step
    gen = b % 2
    start_runs, drain_tile = _run_pieces(
        (n_big_ref, big_src_ref, big_dst_ref, n_small_ref, small_src_ref, small_dst_ref), buf_ref, xs_ref, sem_ref,
        to_sorted=True)

    def drain(step, g):
        for t in range(per_step):
            drain_tile(step * per_step + t, g * per_step + t)

    @pl.when(b >= 2)
    def _():
        drain(b - 2, gen)

    tiles = [b * per_step + t for t in range(per_step)]
    cols = [slice(t * tm, (t + 1) * tm) for t in range(per_step)]
    slots = [_local_slots(ri_ref, local_start_ref, tile, c) for tile, c in zip(tiles, cols)]
    rows = lax.broadcasted_iota(jnp.int32, (r_loc, tm), 0)
    perms = [jnp.where((rows == s1) | (rows == s2), 1.0, 0.0).astype(BF16) for s1, s2 in slots]
    for t, (perm, c) in enumerate(zip(perms, cols)):
        buf_ref[gen * per_step + t] = _dot(perm, h_ref[c, :])
    for t, tile in enumerate(tiles):
        start_runs(tile, gen * per_step + t)

    def tail_pieces(e, act):
        n = tail_rows_ref[e]
        size = TMX // 2
        while size >= CHUNK:
            dst = pl.multiple_of(tail_base_ref[e] + (n & (-2 * size)), CHUNK)
            cp = pltpu.make_async_copy(zero_ref.at[pl.ds(0, size), :], xs_ref.at[pl.ds(dst, size), :],
                                       tail_sem_ref.at[0])
            pl.when((n & size) != 0)(functools.partial(act, cp))
            size //= 2

    def unused_tile_copy(m):
        dst = pl.multiple_of(m * TMX, TMX)
        return pltpu.make_async_copy(zero_ref, xs_ref.at[pl.ds(dst, TMX), :], tail_sem_ref.at[1])

    n_alloc = xs_ref.shape[0] // TMX

    @pl.when(b == 0)
    def _():
        zero_ref[...] = jnp.zeros_like(zero_ref)
        lax.fori_loop(0, N_EXPERTS, lambda e, c: (tail_pieces(e, lambda cp: cp.start()), c)[1], 0)
        lax.fori_loop(n_used_ref[0], n_alloc, lambda m, c: (unused_tile_copy(m).start(), c)[1], 0)

    @pl.when(b == nb - 1)
    def _():
        lax.fori_loop(0, N_EXPERTS, lambda e, c: (tail_pieces(e, lambda cp: cp.wait()), c)[1], 0)
        lax.fori_loop(n_used_ref[0], n_alloc, lambda m, c: (unused_tile_copy(m).wait(), c)[1], 0)

        @pl.when(b >= 1)
        def _():
            drain(b - 1, 1 - gen)

        drain(b, gen)


def _dispatch(h2, ri, plan, tm):
    tokens = h2.shape[0]
    r_loc = _local_rows(tm)
    prefetch = [plan["local_start"]] + [plan[k] for k in PIECE_TABLES] + [
        plan["tail_base"], plan["tail_rows"], plan["n_used"]]
    per_step = DISPATCH_TILES_PER_STEP if (tokens // tm) % DISPATCH_TILES_PER_STEP == 0 else 1
    rows = per_step * tm
    grid_spec = pltpu.PrefetchScalarGridSpec(
        num_scalar_prefetch=len(prefetch),
        grid=(tokens // rows,),
        in_specs=[pl.BlockSpec((rows, D_MODEL), lambda i, *_: (i, 0)),
                  pl.BlockSpec((8, rows), lambda i, *_: (0, i))],
        out_specs=pl.BlockSpec(memory_space=pl.ANY),
        scratch_shapes=[pltpu.VMEM((2 * per_step, r_loc, D_MODEL), F32), pltpu.VMEM((TMX, D_MODEL), F32),
                        pltpu.SemaphoreType.DMA((2 * per_step,)), pltpu.SemaphoreType.DMA((2,))],
    )
    return pl.pallas_call(
        functools.partial(_dispatch_kernel, r_loc=r_loc),
        grid_spec=grid_spec,
        out_shape=jax.ShapeDtypeStruct((_sorted_rows_alloc(tokens, tm), D_MODEL), F32),
        compiler_params=pltpu.CompilerParams(
            dimension_semantics=("arbitrary",), vmem_limit_bytes=VMEM_LIMIT),
        name="dispatch",
    )(*prefetch, h2, ri)


def _experts_kernel(tile_expert_ref, n_used_ref, first_ref, slot_ref, next_ref, rows_ref, xs_hbm, wg_hbm, wu_hbm, wd_hbm,
                    ys_hbm, wg_st, wu_st, wd_st, wg_bf, wu_bf, wd_bf, a_ref, u_ref, xin_ref, yout_ref,
                    sem_ref, in_sem_ref, out_sem_ref):
    m = pl.program_id(0)
    n_used = n_used_ref[0]

    def weight_copies(e, s):
        return [pltpu.make_async_copy(src.at[e], dst.at[s], sem_ref.at[s, n])
                for n, (src, dst) in enumerate([(wg_hbm, wg_st), (wu_hbm, wu_st), (wd_hbm, wd_st)])]

    def in_copy(t):
        s = t % IN_SLOTS
        return pltpu.make_async_copy(xs_hbm.at[pl.ds(pl.multiple_of(t * TMX, TMX), TMX), :], xin_ref.at[s],
                                     in_sem_ref.at[s])

    def out_copy(t):
        s = t % OUT_SLOTS
        return pltpu.make_async_copy(yout_ref.at[s], ys_hbm.at[pl.ds(pl.multiple_of(t * TMX, TMX), TMX), :],
                                     out_sem_ref.at[s])

    @pl.when(m == 0)
    def _():
        for t in range(IN_SLOTS - 1):
            pl.when(t < n_used)(in_copy(t).start)

    @pl.when(m < n_used)
    def _():
        @pl.when(m + IN_SLOTS - 1 < n_used)
        def _():
            in_copy(m + IN_SLOTS - 1).start()

        @pl.when(first_ref[m] == 1)
        def _():
            s = slot_ref[m]

            @pl.when(m == 0)
            def _():
                for cp in weight_copies(tile_expert_ref[0], 0):
                    cp.start()

            for cp in weight_copies(tile_expert_ref[m], s):
                cp.wait()

            @pl.when(next_ref[m] >= 0)
            def _():
                for cp in weight_copies(next_ref[m], 1 - s):
                    cp.start()

            wg_bf[...] = wg_st[s].astype(BF16)
            wu_bf[...] = wu_st[s].astype(BF16)
            wd_bf[...] = wd_st[s].astype(BF16)

        xs_ref = xin_ref.at[m % IN_SLOTS]
        ys_ref = yout_ref.at[m % OUT_SLOTS]
        in_copy(m).wait()

        @pl.when(m >= OUT_SLOTS)
        def _():
            out_copy(m - OUT_SLOTS).wait()

        def mlp(rows):
            x = xs_ref[rows, :].astype(BF16)
            a_ref[rows, :] = _dot(x, wg_bf[...])
            u_ref[rows, :] = _dot(x, wu_bf[...])
            hid = (_silu(a_ref[rows, :]) * u_ref[rows, :]).astype(BF16)
            ys_ref[rows, :] = _dot(hid, wd_bf[...])

        half = TMX // 2

        @pl.when(rows_ref[m] > half)
        def _():
            mlp(slice(0, TMX))

        @pl.when(rows_ref[m] <= half)
        def _():
            mlp(slice(0, half))
            ys_ref[half:, :] = jnp.zeros((TMX - half, D_MODEL), F32)

        out_copy(m).start()

        @pl.when(m == n_used - 1)
        def _():
            for back in range(OUT_SLOTS):
                pl.when(m - back >= 0)(out_copy(m - back).wait)


def _experts(xs, plan, wg, wu, wd):
    n_tiles = xs.shape[0] // TMX
    hbm = pl.BlockSpec(memory_space=pl.ANY)
    up_shape, down_shape = (D_MODEL, D_EXPERT), (D_EXPERT, D_MODEL)
    grid_spec = pltpu.PrefetchScalarGridSpec(
        num_scalar_prefetch=6,
        grid=(n_tiles,),
        in_specs=[hbm, hbm, hbm, hbm],
        out_specs=hbm,
        scratch_shapes=[pltpu.VMEM((2,) + up_shape, F32), pltpu.VMEM((2,) + up_shape, F32),
                        pltpu.VMEM((2,) + down_shape, F32),
                        pltpu.VMEM(up_shape, BF16), pltpu.VMEM(up_shape, BF16), pltpu.VMEM(down_shape, BF16),
                        pltpu.VMEM((TMX, D_EXPERT), F32), pltpu.VMEM((TMX, D_EXPERT), F32),
                        pltpu.VMEM((IN_SLOTS, TMX, D_MODEL), F32), pltpu.VMEM((OUT_SLOTS, TMX, D_MODEL), F32),
                        pltpu.SemaphoreType.DMA((2, 3)), pltpu.SemaphoreType.DMA((IN_SLOTS,)),
                        pltpu.SemaphoreType.DMA((OUT_SLOTS,))],
    )
    return pl.pallas_call(
        _experts_kernel,
        grid_spec=grid_spec,
        out_shape=jax.ShapeDtypeStruct(xs.shape, F32),
        input_output_aliases={6: 0},
        compiler_params=pltpu.CompilerParams(
            dimension_semantics=("arbitrary",), vmem_limit_bytes=VMEM_LIMIT),
        name="experts",
    )(plan["tile_expert"], plan["n_used"], plan["tile_first"], plan["tile_slot"], plan["tile_next"], plan["tile_rows"],
      xs, wg, wu, wd)


def _combine_kernel(local_start_ref, n_big_ref, big_src_ref, big_dst_ref, n_small_ref, small_src_ref, small_dst_ref,
                    ys_ref, ri_ref, rw_ref, x1_ref, mod_ref, gain_ref, o_ref, buf_ref, sem_ref,
                    *, r_loc, tiles_per_batch):
    b = pl.program_id(0)
    nb = pl.num_programs(0)
    per_step = buf_ref.shape[0] // 2
    tm = x1_ref.shape[0] // per_step
    gen = b % 2
    fetch_tile, wait_tile = _run_pieces(
        (n_big_ref, big_src_ref, big_dst_ref, n_small_ref, small_src_ref, small_dst_ref), buf_ref, ys_ref, sem_ref,
        to_sorted=False)

    def fetch(step, g):
        for t in range(per_step):
            fetch_tile(step * per_step + t, g * per_step + t)

    @pl.when(b == 0)
    def _():
        buf_ref[...] = jnp.zeros_like(buf_ref)
        fetch(0, 0)

    @pl.when(b + 1 < nb)
    def _():
        fetch(b + 1, 1 - gen)

    for t in range(per_step):
        wait_tile(b * per_step + t, gen * per_step + t)

    tiles = [b * per_step + t for t in range(per_step)]
    cols = [slice(t * tm, (t + 1) * tm) for t in range(per_step)]
    slots = [_local_slots(ri_ref, local_start_ref, tile, c) for tile, c in zip(tiles, cols)]
    rows = lax.broadcasted_iota(jnp.int32, (r_loc, tm), 0)
    hits = [(rows == s1, rows == s2) for s1, s2 in slots]
    w_rows = [jnp.sum(jnp.where(h1, rw_ref[0:1, c], jnp.where(h2, rw_ref[1:2, c], 0.0)), axis=1, keepdims=True)
              for (h1, h2), c in zip(hits, cols)]
    perms = [jnp.where(h1 | h2, 1.0, 0.0).astype(BF16) for h1, h2 in hits]
    yws = [(buf_ref[gen * per_step + t] * w_rows[t]).astype(BF16) for t in range(per_step)]
    moes = [_dot_tn(perm, yw) for perm, yw in zip(perms, yws)]
    batch = b // tiles_per_batch
    gate2 = mod_ref[pl.ds(batch, 1), 5 * D_MODEL:6 * D_MODEL]
    for c, moe in zip(cols, moes):
        x2 = x1_ref[c, :] + gate2 * moe
        ms = jnp.mean(x2 * x2, axis=-1, keepdims=True)
        o_ref[c, :] = x2 * lax.rsqrt(ms + EPS) * gain_ref[...]


def _combine(ys, ri, rw, x1, mod, gain, plan, seq, tm):
    tokens = x1.shape[0]
    r_loc = _local_rows(tm)
    per_step = COMBINE_TILES_PER_STEP if seq % (COMBINE_TILES_PER_STEP * tm) == 0 else 1
    rows = per_step * tm
    row_spec = pl.BlockSpec((8, rows), lambda i, *_: (0, i))
    tok_spec = pl.BlockSpec((rows, D_MODEL), lambda i, *_: (i, 0))
    full = lambda a: pl.BlockSpec(a.shape, lambda i, *_: (0,) * a.ndim)
    prefetch = [plan["local_start"]] + [plan[k] for k in PIECE_TABLES]
    grid_spec = pltpu.PrefetchScalarGridSpec(
        num_scalar_prefetch=len(prefetch),
        grid=(tokens // rows,),
        in_specs=[pl.BlockSpec(memory_space=pl.ANY), row_spec, row_spec, tok_spec, full(mod), full(gain)],
        out_specs=tok_spec,
        scratch_shapes=[pltpu.VMEM((2 * per_step, r_loc, D_MODEL), F32), pltpu.SemaphoreType.DMA((2 * per_step,))],
    )
    return pl.pallas_call(
        functools.partial(_combine_kernel, r_loc=r_loc, tiles_per_batch=seq // rows),
        grid_spec=grid_spec,
        out_shape=jax.ShapeDtypeStruct((tokens, D_MODEL), F32),
        compiler_params=pltpu.CompilerParams(
            dimension_semantics=("arbitrary",), vmem_limit_bytes=VMEM_LIMIT),
        name="combine",
    )(*prefetch, ys, ri, rw, x1, mod, gain)


def _rotary_tables(seq):
    half = RET_HEAD_DIM // 2
    inv_freq = 1.0 / (ROPE_BASE ** (np.arange(half, dtype=np.float64) / half))
    ang = np.arange(seq, dtype=np.float64)[:, None] * inv_freq[None, :]
    cos = np.cos(ang)
    sin = np.sin(ang)
    f32 = lambda a: jnp.asarray(a.astype(np.float32))
    return f32(np.tile(cos, (1, 4))), f32(np.concatenate([-sin, sin, -sin, sin], axis=1))


def _pick_tile(n, pref):
    t = min(n, pref)
    assert n % t == 0, (n, t)
    return t


def kernel(x, c, ada_w, ada_b, norm1_gain, norm2_gain, w_in, w_out, ret_gn_gain, lam_q1, lam_k1, lam_q2,
           lam_k2, diff_subln_gain, w_group, b_group, w_expert, b_expert, w_gate, w_up, w_down, final_gain):
    batch, seq, d = x.shape
    assert d == D_MODEL and batch <= 8 and ada_w.shape[0] == 1
    layer = 0
    lambda_init = 0.8 - 0.6 * math.exp(-0.3 * layer)
    tokens = batch * seq
    x2 = x.reshape(tokens, d)
    tm = _pick_tile(seq, 512)

    c_pad = jnp.zeros((8, d), F32).at[:batch].set(c)
    mod = _adaln(c_pad, ada_w[layer], ada_b[layer].reshape(1, -1))

    cos_t, sin_t = _rotary_tables(seq)
    rq, rk, rv, rg, dq, dk, dvt = _inproj(
        x2, mod, norm1_gain[layer].reshape(1, d), w_in[layer].astype(BF16), cos_t, sin_t, seq, tm)

    ret_out = _retention(rq, rk, rv, rg, ret_gn_gain[layer].reshape(1, RET_WIDTH), batch, seq,
                         _pick_tile(seq, 256))
    diff_out = _diffattn(
        dq, dk, dvt, lam_q1[layer].reshape(1, -1), lam_k1[layer].reshape(1, -1), lam_q2[layer].reshape(1, -1),
        lam_k2[layer].reshape(1, -1), diff_subln_gain[layer].reshape(-1, 1), batch, seq, lambda_init,
        _pick_tile(seq, 2048), 2 * QUERY_CHUNK)

    w_router = jnp.concatenate(
        [w_group[layer].T, jnp.zeros((8 - N_GROUPS, d), F32), w_expert[layer].reshape(d, N_EXPERTS).T], axis=0)
    b_router = jnp.concatenate(
        [b_group[layer], jnp.zeros((8 - N_GROUPS,), F32), b_expert[layer].reshape(N_EXPERTS)]).reshape(-1, 1)
    wr_hi = w_router.astype(BF16)
    wr_lo = (w_router - wr_hi.astype(F32)).astype(BF16)
    x1, h2, ri, rw, cnt = _outproj(ret_out, diff_out, x2, mod, norm2_gain[layer].reshape(1, d),
                                   w_out[layer].astype(BF16), jnp.concatenate([wr_hi, wr_lo], axis=0), b_router,
                                   seq, tm)

    plan = _dispatch_plan(cnt[:, :, 0], tokens, tm)
    xs = _dispatch(h2, ri, plan, tm)
    ys = _experts(xs, plan, w_gate[layer].reshape(N_EXPERTS, d, D_EXPERT),
                  w_up[layer].reshape(N_EXPERTS, d, D_EXPERT), w_down[layer].reshape(N_EXPERTS, D_EXPERT, d))
    out = _combine(ys, ri, rw, x1, mod, final_gain.reshape(1, d), plan, seq, tm)
    return out.reshape(batch, seq, d)
```

```python
import functools
import math

import jax
import jax.numpy as jnp
import numpy as np
from jax import lax
from jax.experimental import pallas as pl
from jax.experimental.pallas import tpu as pltpu

F32 = jnp.float32
BF16 = jnp.bfloat16

D_MODEL = 1024
RET_HEAD_DIM = 64
RET_WIDTH = 512
RET_HEADS = 8
RET_PAIRS = RET_HEADS // 2
DIFF_QK_DIM = 64
DIFF_V_DIM = 128
DIFF_HEADS = 4
DIFF_WIDTH = 512
N_GROUPS = 4
EXPERTS_PER_GROUP = 8
N_EXPERTS = N_GROUPS * EXPERTS_PER_GROUP
D_EXPERT = 512
N_MOD = 6
ROPE_BASE = 10000.0
EPS = 1e-6
LANES = 128
ROUTER_ROWS = 8 + N_EXPERTS
VMEM_LIMIT = 56 * 1024 * 1024


def _dot(a, b):
    return jnp.dot(a, b, preferred_element_type=F32)


def _dot_nt(a, b):
    return lax.dot_general(a, b, (((1,), (1,)), ((), ())), preferred_element_type=F32)


def _dot_tn(a, b):
    return lax.dot_general(a, b, (((0,), (0,)), ((), ())), preferred_element_type=F32)


def _split_bf16(x):
    hi = x.astype(BF16)
    lo = (x - hi.astype(F32)).astype(BF16)
    return hi, lo


def _silu(x):
    return x / (1.0 + jnp.exp(-x))


def _adaln_kernel(c_ref, w_ref, b_ref, o_ref):
    ca = _silu(c_ref[...])
    c_hi, c_lo = _split_bf16(ca)
    w_hi, w_lo = _split_bf16(w_ref[...])
    o_ref[...] = _dot(c_hi, w_hi) + _dot(c_lo, w_hi) + _dot(c_hi, w_lo) + b_ref[...]


def _adaln(c_pad, ada_w, ada_b):
    n_out = ada_w.shape[1]
    tn = D_MODEL
    return pl.pallas_call(
        _adaln_kernel,
        grid=(n_out // tn,),
        in_specs=[
            pl.BlockSpec((8, D_MODEL), lambda j: (0, 0)),
            pl.BlockSpec((D_MODEL, tn), lambda j: (0, j)),
            pl.BlockSpec((1, tn), lambda j: (0, j)),
        ],
        out_specs=pl.BlockSpec((8, tn), lambda j: (0, j)),
        out_shape=jax.ShapeDtypeStruct((8, n_out), F32),
        compiler_params=pltpu.CompilerParams(vmem_limit_bytes=VMEM_LIMIT),
        name="adaln",
    )(c_pad, ada_w, ada_b)


def _norm_modulate(x, gain, shift, scale):
    ms = jnp.mean(x * x, axis=-1, keepdims=True)
    y = x * lax.rsqrt(ms + EPS) * gain
    return y * (1.0 + scale) + shift


def _rotary_slab(x, cos, sin_signed, lane_lo):
    swapped = jnp.where(lane_lo, pltpu.roll(x, 96, 1), pltpu.roll(x, 32, 1))
    return x * cos + swapped * sin_signed


def _inproj_kernel(x_ref, mod_ref, gain_ref, w_ref, cos_ref, sin_ref,
                   rq_ref, rk_ref, rv_ref, rg_ref, dq_ref, dk_ref, dvt_ref, *, tiles_per_batch):
    b = pl.program_id(0) // tiles_per_batch
    shift = mod_ref[pl.ds(b, 1), 0:D_MODEL]
    scale = mod_ref[pl.ds(b, 1), D_MODEL:2 * D_MODEL]
    h = _norm_modulate(x_ref[...], gain_ref[...], shift, scale).astype(BF16)
    cos = cos_ref[...]
    sin = sin_ref[...]
    lane = lax.broadcasted_iota(jnp.int32, cos.shape, 1)
    lane_lo = (lane % 64) < 32

    def proj(chunk):
        return _dot(h, w_ref[:, chunk * RET_WIDTH:(chunk + 1) * RET_WIDTH])

    def rotary(acc, out_ref, post_scale):
        for s in range(RET_WIDTH // LANES):
            sl = slice(s * LANES, (s + 1) * LANES)
            out_ref[:, sl] = (_rotary_slab(acc[:, sl], cos, sin, lane_lo) * post_scale).astype(BF16)

    rotary(proj(0), rq_ref, 1.0)
    rotary(proj(1), rk_ref, RET_HEAD_DIM ** -0.5)
    rv_ref[...] = proj(2).astype(BF16)
    rg_ref[...] = _silu(proj(3)).astype(BF16)
    dq_ref[...] = (proj(4) * (DIFF_QK_DIM ** -0.5 * math.log2(math.e))).astype(BF16)
    dk_ref[...] = proj(5).astype(BF16)
    dvt_ref[...] = proj(6).T.astype(BF16)


def _inproj(x2, mod, gain, w_in, cos_t, sin_t, seq, tm):
    tokens = x2.shape[0]
    tiles_per_batch = seq // tm
    tok_spec = lambda w: pl.BlockSpec((tm, w), lambda i: (i, 0))
    tab_spec = pl.BlockSpec((tm, LANES), lambda i: (i % tiles_per_batch, 0))
    full = lambda a: pl.BlockSpec(a.shape, lambda i: (0,) * a.ndim)
    out = jax.ShapeDtypeStruct((tokens, RET_WIDTH), BF16)
    return pl.pallas_call(
        functools.partial(_inproj_kernel, tiles_per_batch=tiles_per_batch),
        grid=(tokens // tm,),
        in_specs=[tok_spec(D_MODEL), full(mod), full(gain), full(w_in), tab_spec, tab_spec],
        out_specs=[tok_spec(RET_WIDTH)] * 6 + [pl.BlockSpec((DIFF_WIDTH, tm), lambda i: (0, i))],
        out_shape=[out] * 6 + [jax.ShapeDtypeStruct((DIFF_WIDTH, tokens), BF16)],
        compiler_params=pltpu.CompilerParams(
            dimension_semantics=("parallel",), vmem_limit_bytes=VMEM_LIMIT),
        name="inproj",
    )(x2, mod, gain, w_in, cos_t, sin_t)


RET_SEQ_GROUP = 4


def _retention_kernel(q_ref, k_ref, v_ref, g_ref, dec_ref, qdec_ref, kdec_ref, rdec_ref,
                      bmask_ref, gmean_ref, gain_ref, o_ref, state_ref, *, chunk):
    @pl.when(pl.program_id(1) == 0)
    def _():
        state_ref[...] = jnp.zeros_like(state_ref)

    lane = lax.broadcasted_iota(jnp.int32, (chunk, LANES), 1)
    first_head = lane < RET_HEAD_DIM
    gmean = gmean_ref[...]
    bmask = bmask_ref[...]
    units = [(s, p) for s in range(q_ref.shape[0]) for p in range(RET_PAIRS)]
    sl = lambda p: slice(p * LANES, (p + 1) * LANES)
    q = [q_ref[s, :, sl(p)] for s, p in units]
    k = [k_ref[s, :, sl(p)] for s, p in units]
    v = [v_ref[s, :, sl(p)] for s, p in units]
    zero = jnp.zeros_like(q[0])
    q_stack = [jnp.concatenate([jnp.where(first_head, qu, zero), jnp.where(first_head, zero, qu)], axis=0)
               for qu in q]
    scores = [(_dot_nt(q_stack[n], k[n]) * dec_ref[p]).astype(BF16) for n, (_, p) in enumerate(units)]
    state = [state_ref[n] for n in range(len(units))]
    cross = [_dot(q[n], state[n].astype(BF16)) * qdec_ref[:, sl(p)] for n, (_, p) in enumerate(units)]
    k_dec = [(k[n].astype(F32) * kdec_ref[:, sl(p)]).astype(BF16) for n, (_, p) in enumerate(units)]
    for n, (_, p) in enumerate(units):
        state_ref[n] = state[n] * rdec_ref[p] + _dot_tn(k_dec[n], v[n]) * bmask
    intra2 = [_dot(scores[n], v[n]) for n in range(len(units))]
    y = [jnp.where(first_head, intra2[n][:chunk], intra2[n][chunk:]) + cross[n] for n in range(len(units))]
    seg_mean = lambda x: _dot(jnp.concatenate(_split_bf16(x), axis=1), gmean)
    mu = [seg_mean(yu) for yu in y]
    d = [yu - mu_u for yu, mu_u in zip(y, mu)]
    var = [seg_mean(du * du) for du in d]
    for n, (s, p) in enumerate(units):
        yn = d[n] * lax.rsqrt(var[n] + EPS) * gain_ref[:, sl(p)]
        o_ref[s, :, sl(p)] = (g_ref[s, :, sl(p)].astype(F32) * yn).astype(BF16)


def _retention_tables(chunk):
    heads = np.arange(RET_HEADS, dtype=np.float64)
    log_gamma = np.log(1.0 - np.exp2(-5.0 - heads))
    idx = np.arange(chunk)
    rel = (idx[:, None] - idx[None, :]).astype(np.float64)
    decay = np.where(rel[None] >= 0, np.exp(log_gamma[:, None, None] * np.maximum(rel, 0.0)[None]), 0.0)
    dec2 = decay.reshape(RET_PAIRS, 2 * chunk, chunk)
    lane_lg = np.repeat(log_gamma, RET_HEAD_DIM)
    qdec = np.exp(lane_lg[None, :] * (idx + 1)[:, None])
    kdec = np.exp(lane_lg[None, :] * (chunk - 1 - idx)[:, None])
    rdec = np.exp(lane_lg * chunk).reshape(RET_PAIRS, LANES, 1) * np.ones((1, 1, LANES))
    blk = np.arange(LANES) // RET_HEAD_DIM
    bmask = (blk[:, None] == blk[None, :]).astype(np.float64)
    f32 = lambda a: jnp.asarray(a.astype(np.float32))
    gmean2 = np.concatenate([bmask, bmask], axis=0) / RET_HEAD_DIM
    return f32(dec2), f32(qdec), f32(kdec), f32(rdec), f32(bmask), f32(gmean2).astype(BF16)


def _retention(rq, rk, rv, rg, gn_gain, batch, seq, chunk):
    nc = seq // chunk
    group = RET_SEQ_GROUP if batch % RET_SEQ_GROUP == 0 else 1
    dec2, qdec, kdec, rdec, bmask, gmean = _retention_tables(chunk)
    tok_spec = pl.BlockSpec((group, chunk, RET_WIDTH), lambda b, n: (b, n, 0))
    full = lambda a: pl.BlockSpec(a.shape, lambda b, n: (0,) * a.ndim)
    by_seq = lambda a: a.reshape(batch, seq, RET_WIDTH)
    out = pl.pallas_call(
        functools.partial(_retention_kernel, chunk=chunk),
        grid=(batch // group, nc),
        in_specs=[tok_spec] * 4 + [full(dec2), full(qdec), full(kdec), full(rdec), full(bmask),
                                   full(gmean), full(gn_gain)],
        out_specs=tok_spec,
        out_shape=jax.ShapeDtypeStruct((batch, seq, RET_WIDTH), BF16),
        scratch_shapes=[pltpu.VMEM((group * RET_PAIRS, LANES, LANES), F32)],
        compiler_params=pltpu.CompilerParams(
            dimension_semantics=("parallel", "arbitrary"), vmem_limit_bytes=VMEM_LIMIT),
        name="retention",
    )(by_seq(rq), by_seq(rk), by_seq(rv), by_seq(rg), dec2, qdec, kdec, rdec, bmask, gmean, gn_gain)
    return out.reshape(batch * seq, RET_WIDTH)


NEG_BIG = -1e30


V_EXT_ROWS = DIFF_V_DIM + 16
QUERY_CHUNK = 256
SCORES_AHEAD_FULL = 3
SCORES_AHEAD_DIAG = 3
STAGE_SLOTS = 16


def _diag_chunks(tq, tk, d):
    assert tk == 2 * QUERY_CHUNK
    per_softmax = tq // QUERY_CHUNK
    out = []
    for c in range(2 * per_softmax):
        q0 = (c % per_softmax) * QUERY_CHUNK
        if q0 + QUERY_CHUNK - 1 < d * tk:
            continue
        kind = "full" if q0 >= (d + 1) * tk else ("tri" if q0 == d * tk else "low_tri")
        out.append((c, kind))
    return out


def _accumulate(acc_ref, cs, alpha, pv):
    acc_ref[:, cs] = alpha * acc_ref[:, cs] + pv


def _diffattn_kernel(q_ref, k_ref, vt_ref, lq1_ref, lk1_ref, lq2_ref, lk2_ref, gain_ref, bias_ref, o_ref,
                     qs_ref, vext_ref, m_ref, acc_ref, stage_ref, *, tq, tk, lambda_init):
    i = pl.program_id(2)
    nk = vext_ref.shape[0]

    @pl.when(i == 0)
    def _():
        for j in range(nk):
            vext_ref[j, 0:DIFF_V_DIM, :] = vt_ref[:, j * tk:(j + 1) * tk]
            vext_ref[j, DIFF_V_DIM:V_EXT_ROWS, :] = jnp.ones((V_EXT_ROWS - DIFF_V_DIM, tk), BF16)

    q = q_ref[...]
    lane = lax.broadcasted_iota(jnp.int32, q.shape, 1)
    zero = jnp.zeros_like(q)
    qs_ref[0:tq, :] = jnp.where(lane < DIFF_QK_DIM, q, zero)
    qs_ref[tq:2 * tq, :] = jnp.where(lane < DIFF_QK_DIM, zero, q)
    m_ref[...] = jnp.full_like(m_ref, NEG_BIG)
    acc_ref[...] = jnp.zeros_like(acc_ref)

    def step(work, n_ahead):
        chunk = lambda c: slice(c * QUERY_CHUNK, (c + 1) * QUERY_CHUNK)

        def scores(j, c, kind):
            n_keys = QUERY_CHUNK if kind == "tri" else tk
            start = pl.multiple_of(j * tk, tk)
            return _dot_nt(k_ref[pl.ds(start, n_keys), :], qs_ref[chunk(c), :])

        ahead = [scores(*work[n]) for n in range(min(n_ahead, len(work)))]
        pending = None
        for n, (j, c, kind) in enumerate(work):
            cs = chunk(c)
            st = ahead.pop(0)
            if n + n_ahead < len(work):
                ahead.append(scores(*work[n + n_ahead]))
            slot = n % stage_ref.shape[0]
            n_keys = st.shape[0]
            stage_ref[slot, 0:n_keys, :] = st
            if kind == "full":
                st = stage_ref[slot]
            else:
                causal = stage_ref[slot, n_keys - QUERY_CHUNK:n_keys, :] + bias_ref[...]
                st = causal if kind == "tri" else jnp.concatenate(
                    [stage_ref[slot, 0:n_keys - QUERY_CHUNK, :], causal], axis=0)
            m_old = m_ref[:, cs]
            m_new = jnp.maximum(m_old, jnp.max(st, axis=0, keepdims=True))
            alpha = jnp.exp2(m_old - m_new)
            p = jnp.exp2(st - m_new).astype(BF16)
            m_ref[:, cs] = m_new
            pv = _dot(vext_ref[j, :, 0:st.shape[0]], p)
            if pending is not None:
                pending()
            pending = functools.partial(_accumulate, acc_ref, cs, alpha, pv)
        pending()

    tiles_per_q = tq // tk
    n_chunks = 2 * tq // QUERY_CHUNK

    def full_tiles(it):
        return [(it * tiles_per_q + d, c, "full") for d in range(tiles_per_q) for c in range(n_chunks)]

    lax.fori_loop(0, i, lambda it, c: (step(full_tiles(it), SCORES_AHEAD_FULL), c)[1], 0)
    kind_order = {"full": 0, "low_tri": 1, "tri": 2}
    diag = lambda it: [(it * tiles_per_q + d, c, kind) for d in range(tiles_per_q)
                       for c, kind in sorted(_diag_chunks(tq, tk, d), key=lambda ck: kind_order[ck[1]])]
    lax.fori_loop(i, i + 1, lambda it, c: (step(diag(it), SCORES_AHEAD_DIAG), c)[1], 0)

    lam = (jnp.exp(jnp.sum(lq1_ref[...] * lk1_ref[...], axis=-1, keepdims=True))
           - jnp.exp(jnp.sum(lq2_ref[...] * lk2_ref[...], axis=-1, keepdims=True)) + lambda_init)
    acc = acc_ref[...]
    o2 = acc[0:DIFF_V_DIM, :] * (1.0 / acc[DIFF_V_DIM:DIFF_V_DIM + 1, :])
    ot = o2[:, :tq] - lam * o2[:, tq:]
    ms = jnp.mean(ot * ot, axis=0, keepdims=True)
    ot = ot * lax.rsqrt(ms + EPS) * gain_ref[...] * (1.0 - lambda_init)
    o_ref[...] = ot.T.astype(BF16)


def _diffattn(dq, dk, dvt, lam_q1, lam_k1, lam_q2, lam_k2, gain, batch, seq, lambda_init, tq, tk):
    nq = seq // tq
    q_spec = pl.BlockSpec((tq, LANES), lambda b, h, i: (b * nq + i, h))
    k_spec = pl.BlockSpec((seq, LANES), lambda b, h, i: (b, h))
    vt_spec = pl.BlockSpec((DIFF_V_DIM, seq), lambda b, h, i: (h, b))
    vec = lambda a: pl.BlockSpec(a.shape, lambda b, h, i: (0, 0))
    key = np.arange(QUERY_CHUNK)[:, None]
    query = np.arange(QUERY_CHUNK)[None, :]
    bias = jnp.asarray(np.where(key <= query, 0.0, NEG_BIG), F32)
    return pl.pallas_call(
        functools.partial(_diffattn_kernel, tq=tq, tk=tk, lambda_init=lambda_init),
        grid=(batch, DIFF_HEADS, nq),
        in_specs=[q_spec, k_spec, vt_spec, vec(lam_q1), vec(lam_k1), vec(lam_q2), vec(lam_k2), vec(gain),
                  vec(bias)],
        out_specs=q_spec,
        out_shape=jax.ShapeDtypeStruct(dq.shape, BF16),
        scratch_shapes=[
            pltpu.VMEM((2 * tq, LANES), BF16),
            pltpu.VMEM((seq // tk, V_EXT_ROWS, tk), BF16),
            pltpu.VMEM((1, 2 * tq), F32),
            pltpu.VMEM((V_EXT_ROWS, 2 * tq), F32),
            pltpu.VMEM((STAGE_SLOTS, tk, QUERY_CHUNK), F32),
        ],
        compiler_params=pltpu.CompilerParams(
            dimension_semantics=("parallel", "parallel", "arbitrary"), vmem_limit_bytes=VMEM_LIMIT),
        name="diffattn",
    )(dq, dk, dvt, lam_q1, lam_k1, lam_q2, lam_k2, gain, bias)


def _route(logits):
    r = [logits[g:g + 1, :] for g in range(N_GROUPS)]
    gmax = jnp.maximum(jnp.maximum(r[0], r[1]), jnp.maximum(r[2], r[3]))
    g_idx = jnp.where(r[0] == gmax, 0, jnp.where(r[1] == gmax, 1, jnp.where(r[2] == gmax, 2, 3)))
    denom = sum(jnp.exp(rg - gmax) for rg in r)
    g_weight = 1.0 / denom
    sel = jnp.zeros((EXPERTS_PER_GROUP, logits.shape[1]), F32)
    for g in range(N_GROUPS):
        rows = logits[8 + g * EXPERTS_PER_GROUP:8 + (g + 1) * EXPERTS_PER_GROUP, :]
        sel = jnp.where(g_idx == g, rows, sel)
    eidx = lax.broadcasted_iota(jnp.int32, sel.shape, 0)
    v1 = jnp.max(sel, axis=0, keepdims=True)
    i1 = jnp.min(jnp.where(sel == v1, eidx, EXPERTS_PER_GROUP), axis=0, keepdims=True)
    sel2 = jnp.where(eidx == i1, -jnp.inf, sel)
    v2 = jnp.max(sel2, axis=0, keepdims=True)
    i2 = jnp.min(jnp.where(sel2 == v2, eidx, EXPERTS_PER_GROUP), axis=0, keepdims=True)
    e2 = jnp.exp(v2 - v1)
    w1 = g_weight / (1.0 + e2)
    w2 = g_weight * e2 / (1.0 + e2)
    return g_idx, i1, i2, w1, w2


OUTPROJ_PARTS = 2
OUTPROJ_TILES_PER_STEP = 2


def _outproj_kernel(ret_ref, diff_ref, x_ref, mod_ref, gain_ref, wo_ref, wr_ref, br_ref, tri_ref,
                    x1_ref, h2_ref, ri_ref, rw_ref, cnt_ref, *, tiles_per_batch):
    b = pl.program_id(0) // tiles_per_batch
    gate1 = mod_ref[pl.ds(b, 1), 2 * D_MODEL:3 * D_MODEL]
    shift = mod_ref[pl.ds(b, 1), 3 * D_MODEL:4 * D_MODEL]
    scale = mod_ref[pl.ds(b, 1), 4 * D_MODEL:5 * D_MODEL]
    wr = wr_ref[...]
    rows = x_ref.shape[0]
    tm = tri_ref.shape[0]
    n_parts = OUTPROJ_PARTS * rows // tm
    parts = [slice(n * rows // n_parts, (n + 1) * rows // n_parts) for n in range(n_parts)]
    mix = [_dot(jnp.concatenate([ret_ref[r, :], diff_ref[r, :]], axis=1), wo_ref[...]) for r in parts]
    for r, m in zip(parts, mix):
        x1_ref[r, :] = x_ref[r, :] + gate1 * m
    h_split = [_split_bf16(_norm_modulate(x1_ref[r, :], gain_ref[...], shift, scale)) for r in parts]
    for r, (h_hi, _) in zip(parts, h_split):
        h2_ref[r, :] = h_hi
    by_hi = [_dot_nt(wr, h_hi) for h_hi, _ in h_split]
    by_lo = [_dot_nt(wr[:ROUTER_ROWS], h_lo) for _, h_lo in h_split]
    logits = [a[:ROUTER_ROWS] + a[ROUTER_ROWS:] + c + br_ref[...] for a, c in zip(by_hi, by_lo)]
    routed = [_route(lg) for lg in logits]
    g_idx, i1, i2, w1, w2 = [jnp.concatenate([rt[n] for rt in routed], axis=1) for n in range(5)]
    logits = jnp.concatenate(logits, axis=1)
    e1 = g_idx * EXPERTS_PER_GROUP + i1
    e2 = g_idx * EXPERTS_PER_GROUP + i2
    eidx = lax.broadcasted_iota(jnp.int32, (N_EXPERTS, logits.shape[1]), 0)
    hit1 = eidx == e1
    hit2 = eidx == e2
    onehot = jnp.where(hit1 | hit2, 1.0, 0.0)
    tiles = [slice(n * tm, (n + 1) * tm) for n in range(rows // tm)]
    before = jnp.concatenate([_dot(onehot[:, t].astype(BF16), tri_ref[...]) for t in tiles], axis=1)
    r1 = jnp.sum(jnp.where(hit1, before, 0.0), axis=0, keepdims=True)
    r2 = jnp.sum(jnp.where(hit2, before, 0.0), axis=0, keepdims=True)
    zi = jnp.zeros_like(e1)
    ri_ref[...] = jnp.concatenate([e1, e2, r1.astype(jnp.int32), r2.astype(jnp.int32), zi, zi, zi, zi], axis=0)
    zf = jnp.zeros_like(w1)
    rw_ref[...] = jnp.concatenate([w1, w2, zf, zf, zf, zf, zf, zf], axis=0)
    for n, t in enumerate(tiles):
        counts = jnp.sum(onehot[:, t], axis=1, keepdims=True)
        cnt_ref[n] = jnp.broadcast_to(counts, (N_EXPERTS, LANES)).astype(jnp.int32)


def _outproj(ret_out, diff_out, x2, mod, gain, w_out, wr, br, seq, tm):
    tokens = x2.shape[0]
    n_tiles = tokens // tm
    per_step = OUTPROJ_TILES_PER_STEP if seq % (OUTPROJ_TILES_PER_STEP * tm) == 0 else 1
    rows = per_step * tm
    tri = jnp.asarray(np.arange(tm)[:, None] < np.arange(tm)[None, :], BF16)
    tok_spec = lambda w: pl.BlockSpec((rows, w), lambda i: (i, 0))
    row_spec = pl.BlockSpec((8, rows), lambda i: (0, i))
    full = lambda a: pl.BlockSpec(a.shape, lambda i: (0,) * a.ndim)
    return pl.pallas_call(
        functools.partial(_outproj_kernel, tiles_per_batch=seq // rows),
        grid=(tokens // rows,),
        in_specs=[tok_spec(RET_WIDTH), tok_spec(DIFF_WIDTH), tok_spec(D_MODEL), full(mod), full(gain),
                  full(w_out), full(wr), full(br), full(tri)],
        out_specs=[tok_spec(D_MODEL), tok_spec(D_MODEL), row_spec, row_spec,
                   pl.BlockSpec((per_step, N_EXPERTS, LANES), lambda i: (i, 0, 0))],
        out_shape=[jax.ShapeDtypeStruct((tokens, D_MODEL), F32),
                   jax.ShapeDtypeStruct((tokens, D_MODEL), BF16),
                   jax.ShapeDtypeStruct((8, tokens), jnp.int32),
                   jax.ShapeDtypeStruct((8, tokens), F32),
                   jax.ShapeDtypeStruct((n_tiles, N_EXPERTS, LANES), jnp.int32)],
        compiler_params=pltpu.CompilerParams(
            dimension_semantics=("parallel",), vmem_limit_bytes=VMEM_LIMIT),
        name="outproj",
    )(ret_out, diff_out, x2, mod, gain, w_out, wr, br, tri)


CHUNK = 8
BIG_PIECE = 4 * CHUNK
MAX_SMALL_PIECES = N_EXPERTS * (BIG_PIECE // CHUNK - 1)
PIECE_TABLES = ("n_big", "big_src", "big_dst", "n_small", "small_src", "small_dst")
DISPATCH_TILES_PER_STEP = 2
COMBINE_TILES_PER_STEP = 2
TMX = 512
IN_SLOTS = 4
OUT_SLOTS = 3


def _local_rows(tm):
    rows = 2 * tm + N_EXPERTS * (CHUNK - 1)
    return (rows + 15) // 16 * 16


def _max_big_pieces(tm):
    return _local_rows(tm) // BIG_PIECE


def _sorted_rows_alloc(tokens, tm):
    worst = 2 * tokens + (tokens // tm) * N_EXPERTS * (CHUNK - 1) + N_EXPERTS * (TMX - CHUNK)
    return (worst + TMX - 1) // TMX * TMX


def _dispatch_plan(cnt, tokens, tm):
    i32 = jnp.int32
    pad = (cnt + CHUNK - 1) // CHUNK * CHUNK
    local_end = jnp.cumsum(pad, axis=1)
    local_start = local_end - pad
    seg_rows = jnp.sum(pad, axis=0)
    seg_pad = (seg_rows + TMX - 1) // TMX * TMX
    seg_end = jnp.cumsum(seg_pad)
    seg_start = seg_end - seg_pad
    run_dst = seg_start[None, :] + jnp.cumsum(pad, axis=0) - pad

    def pieces(count, offset, size, max_n):
        end = jnp.cumsum(count, axis=1)
        start = end - count
        k = jnp.arange(max_n, dtype=i32)[None, :, None]
        owns = (start[:, None, :] <= k) & (k < end[:, None, :])
        within = size * (k - start[:, None, :]) + offset[:, None, :]
        src = jnp.sum(jnp.where(owns, local_start[:, None, :] + within, 0), axis=-1)
        dst = jnp.sum(jnp.where(owns, run_dst[:, None, :] + within, 0), axis=-1)
        return end[:, -1].astype(i32), src.reshape(-1).astype(i32), dst.reshape(-1).astype(i32)

    n_big, big_src, big_dst = pieces(pad // BIG_PIECE, jnp.zeros_like(pad), BIG_PIECE, _max_big_pieces(tm))
    n_small, small_src, small_dst = pieces(pad % BIG_PIECE // CHUNK, pad // BIG_PIECE * BIG_PIECE, CHUNK,
                                           MAX_SMALL_PIECES)
    m = TMX * jnp.arange(_sorted_rows_alloc(tokens, tm) // TMX, dtype=i32)
    tile_expert = jnp.minimum(jnp.sum(seg_end[None, :] <= m[:, None], axis=-1), N_EXPERTS - 1)
    towns = (seg_start[None, :] <= m[:, None]) & (m[:, None] < seg_end[None, :])
    used = seg_pad > 0
    parity = (jnp.cumsum(used) - used) % 2
    eids = jnp.arange(N_EXPERTS, dtype=i32)
    later_used = (eids[None, :] > eids[:, None]) & used[None, :]
    next_used = jnp.min(jnp.where(later_used, eids[None, :], N_EXPERTS), axis=1)
    next_used = jnp.where(next_used == N_EXPERTS, -1, next_used)
    pick = lambda per_expert: jnp.sum(jnp.where(towns, per_expert[None, :], 0), axis=-1)
    tile_first = jnp.sum(jnp.where(towns & (seg_start[None, :] == m[:, None]), 1, 0), axis=-1)
    tile_next = jnp.where(jnp.any(towns, axis=-1), pick(next_used), -1)
    return dict(
        tile_first=tile_first.astype(i32),
        tile_slot=pick(parity).astype(i32),
        tile_next=tile_next.astype(i32),
        tile_rows=jnp.clip(pick(seg_start + seg_rows) - m, 0, TMX).astype(i32),
        local_start=local_start.reshape(-1).astype(i32),
        n_big=n_big, big_src=big_src, big_dst=big_dst,
        n_small=n_small, small_src=small_src, small_dst=small_dst,
        tail_base=(seg_start + seg_rows).astype(i32),
        tail_rows=(seg_pad - seg_rows).astype(i32),
        tile_expert=tile_expert.astype(i32),
        n_used=(seg_end[-1:] // TMX).astype(i32),
    )


WAIT_UNROLL = 8


def _wait_times(copy, n):
    lax.fori_loop(0, n // WAIT_UNROLL, lambda i, c: ([copy.wait() for _ in range(WAIT_UNROLL)], c)[1], 0)
    lax.fori_loop(0, n % WAIT_UNROLL, lambda i, c: (copy.wait(), c)[1], 0)


def _for_each(n, body, unroll=4):
    main = n // unroll
    lax.fori_loop(0, main, lambda i, c: ([body(i * unroll + u) for u in range(unroll)], c)[1], 0)
    lax.fori_loop(main * unroll, n, lambda j, c: (body(j), c)[1], 0)


def _local_slots(ri_ref, local_start_ref, tile, cols=slice(None)):
    e1, e2 = ri_ref[0:1, cols], ri_ref[1:2, cols]
    s1, s2 = ri_ref[2:3, cols], ri_ref[3:4, cols]
    for e in range(N_EXPERTS):
        start = local_start_ref[tile * N_EXPERTS + e]
        s1 = s1 + jnp.where(e1 == e, start, 0)
        s2 = s2 + jnp.where(e2 == e, start, 0)
    return s1, s2


def _run_pieces(piece_refs, local_ref, sorted_ref, sem_ref, to_sorted):
    n_big_ref, big_src_ref, big_dst_ref, n_small_ref, small_src_ref, small_dst_ref = piece_refs
    r_loc = local_ref.shape[1]
    kinds = [(BIG_PIECE, n_big_ref, big_src_ref, big_dst_ref, r_loc // BIG_PIECE),
             (CHUNK, n_small_ref, small_src_ref, small_dst_ref, MAX_SMALL_PIECES)]

    def copy(sl, size, local_row, sorted_row):
        local = local_ref.at[sl, pl.ds(pl.multiple_of(local_row, CHUNK), size), :]
        srt = sorted_ref.at[pl.ds(pl.multiple_of(sorted_row, CHUNK), size), :]
        return pltpu.make_async_copy(local, srt, sem_ref.at[sl]) if to_sorted else \
            pltpu.make_async_copy(srt, local, sem_ref.at[sl])

    def start(tile, sl):
        for size, n_ref, src_ref, dst_ref, max_n in kinds:
            _for_each(n_ref[tile], lambda k: copy(sl, size, src_ref[tile * max_n + k], dst_ref[tile * max_n + k])
                      .start())

    def wait(tile, sl):
        for size, n_ref, _, _, _ in kinds:
            _wait_times(copy(sl, size, 0, 0), n_ref[tile])

    return start, wait


def _dispatch_kernel(local_start_ref, n_big_ref, big_src_ref, big_dst_ref, n_small_ref, small_src_ref, small_dst_ref,
                     tail_base_ref, tail_rows_ref, n_used_ref,
                     h_ref, ri_ref, xs_ref, buf_ref, zero_ref, sem_ref, tail_sem_ref, *, r_loc):
    b = pl.program_id(0)
    nb = pl.num_programs(0)
    per_step = buf_ref.shape[0] // 2
    tm = h_ref.shape[0] // per_step
    gen = b % 2
    start_runs, drain_tile = _run_pieces(
        (n_big_ref, big_src_ref, big_dst_ref, n_small_ref, small_src_ref, small_dst_ref), buf_ref, xs_ref, sem_ref,
        to_sorted=True)

    def drain(step, g):
        for t in range(per_step):
            drain_tile(step * per_step + t, g * per_step + t)

    @pl.when(b >= 2)
    def _():
        drain(b - 2, gen)

    tiles = [b * per_step + t for t in range(per_step)]
    cols = [slice(t * tm, (t + 1) * tm) for t in range(per_step)]
    slots = [_local_slots(ri_ref, local_start_ref, tile, c) for tile, c in zip(tiles, cols)]
    rows = lax.broadcasted_iota(jnp.int32, (r_loc, tm), 0)
    perms = [jnp.where((rows == s1) | (rows == s2), 1.0, 0.0).astype(BF16) for s1, s2 in slots]
    for t, (perm, c) in enumerate(zip(perms, cols)):
        buf_ref[gen * per_step + t] = _dot(perm, h_ref[c, :])
    for t, tile in enumerate(tiles):
        start_runs(tile, gen * per_step + t)

    def tail_pieces(e, act):
        n = tail_rows_ref[e]
        size = TMX // 2
        while size >= CHUNK:
            dst = pl.multiple_of(tail_base_ref[e] + (n & (-2 * size)), CHUNK)
            cp = pltpu.make_async_copy(zero_ref.at[pl.ds(0, size), :], xs_ref.at[pl.ds(dst, size), :],
                                       tail_sem_ref.at[0])
            pl.when((n & size) != 0)(functools.partial(act, cp))
            size //= 2

    def unused_tile_copy(m):
        dst = pl.multiple_of(m * TMX, TMX)
        return pltpu.make_async_copy(zero_ref, xs_ref.at[pl.ds(dst, TMX), :], tail_sem_ref.at[1])

    n_alloc = xs_ref.shape[0] // TMX

    @pl.when(b == 0)
    def _():
        zero_ref[...] = jnp.zeros_like(zero_ref)
        lax.fori_loop(0, N_EXPERTS, lambda e, c: (tail_pieces(e, lambda cp: cp.start()), c)[1], 0)
        lax.fori_loop(n_used_ref[0], n_alloc, lambda m, c: (unused_tile_copy(m).start(), c)[1], 0)

    @pl.when(b == nb - 1)
    def _():
        lax.fori_loop(0, N_EXPERTS, lambda e, c: (tail_pieces(e, lambda cp: cp.wait()), c)[1], 0)
        lax.fori_loop(n_used_ref[0], n_alloc, lambda m, c: (unused_tile_copy(m).wait(), c)[1], 0)

        @pl.when(b >= 1)
        def _():
            drain(b - 1, 1 - gen)

        drain(b, gen)


def _dispatch(h2, ri, plan, tm):
    tokens = h2.shape[0]
    r_loc = _local_rows(tm)
    prefetch = [plan["local_start"]] + [plan[k] for k in PIECE_TABLES] + [
        plan["tail_base"], plan["tail_rows"], plan["n_used"]]
    per_step = DISPATCH_TILES_PER_STEP if (tokens // tm) % DISPATCH_TILES_PER_STEP == 0 else 1
    rows = per_step * tm
    grid_spec = pltpu.PrefetchScalarGridSpec(
        num_scalar_prefetch=len(prefetch),
        grid=(tokens // rows,),
        in_specs=[pl.BlockSpec((rows, D_MODEL), lambda i, *_: (i, 0)),
                  pl.BlockSpec((8, rows), lambda i, *_: (0, i))],
        out_specs=pl.BlockSpec(memory_space=pl.ANY),
        scratch_shapes=[pltpu.VMEM((2 * per_step, r_loc, D_MODEL), F32), pltpu.VMEM((TMX, D_MODEL), F32),
                        pltpu.SemaphoreType.DMA((2 * per_step,)), pltpu.SemaphoreType.DMA((2,))],
    )
    return pl.pallas_call(
        functools.partial(_dispatch_kernel, r_loc=r_loc),
        grid_spec=grid_spec,
        out_shape=jax.ShapeDtypeStruct((_sorted_rows_alloc(tokens, tm), D_MODEL), F32),
        compiler_params=pltpu.CompilerParams(
            dimension_semantics=("arbitrary",), vmem_limit_bytes=VMEM_LIMIT),
        name="dispatch",
    )(*prefetch, h2, ri)


def _experts_kernel(tile_expert_ref, n_used_ref, first_ref, slot_ref, next_ref, rows_ref, xs_hbm, wg_hbm, wu_hbm, wd_hbm,
                    ys_hbm, wg_st, wu_st, wd_st, wg_bf, wu_bf, wd_bf, a_ref, u_ref, xin_ref, yout_ref,
                    sem_ref, in_sem_ref, out_sem_ref):
    m = pl.program_id(0)
    n_used = n_used_ref[0]

    def weight_copies(e, s):
        return [pltpu.make_async_copy(src.at[e], dst.at[s], sem_ref.at[s, n])
                for n, (src, dst) in enumerate([(wg_hbm, wg_st), (wu_hbm, wu_st), (wd_hbm, wd_st)])]

    def in_copy(t):
        s = t % IN_SLOTS
        return pltpu.make_async_copy(xs_hbm.at[pl.ds(pl.multiple_of(t * TMX, TMX), TMX), :], xin_ref.at[s],
                                     in_sem_ref.at[s])

    def out_copy(t):
        s = t % OUT_SLOTS
        return pltpu.make_async_copy(yout_ref.at[s], ys_hbm.at[pl.ds(pl.multiple_of(t * TMX, TMX), TMX), :],
                                     out_sem_ref.at[s])

    @pl.when(m == 0)
    def _():
        for t in range(IN_SLOTS - 1):
            pl.when(t < n_used)(in_copy(t).start)

    @pl.when(m < n_used)
    def _():
        @pl.when(m + IN_SLOTS - 1 < n_used)
        def _():
            in_copy(m + IN_SLOTS - 1).start()

        @pl.when(first_ref[m] == 1)
        def _():
            s = slot_ref[m]

            @pl.when(m == 0)
            def _():
                for cp in weight_copies(tile_expert_ref[0], 0):
                    cp.start()

            for cp in weight_copies(tile_expert_ref[m], s):
                cp.wait()

            @pl.when(next_ref[m] >= 0)
            def _():
                for cp in weight_copies(next_ref[m], 1 - s):
                    cp.start()

            wg_bf[...] = wg_st[s].astype(BF16)
            wu_bf[...] = wu_st[s].astype(BF16)
            wd_bf[...] = wd_st[s].astype(BF16)

        xs_ref = xin_ref.at[m % IN_SLOTS]
        ys_ref = yout_ref.at[m % OUT_SLOTS]
        in_copy(m).wait()

        @pl.when(m >= OUT_SLOTS)
        def _():
            out_copy(m - OUT_SLOTS).wait()

        def mlp(rows):
            x = xs_ref[rows, :].astype(BF16)
            a_ref[rows, :] = _dot(x, wg_bf[...])
            u_ref[rows, :] = _dot(x, wu_bf[...])
            hid = (_silu(a_ref[rows, :]) * u_ref[rows, :]).astype(BF16)
            ys_ref[rows, :] = _dot(hid, wd_bf[...])

        half = TMX // 2

        @pl.when(rows_ref[m] > half)
        def _():
            mlp(slice(0, TMX))

        @pl.when(rows_ref[m] <= half)
        def _():
            mlp(slice(0, half))
            ys_ref[half:, :] = jnp.zeros((TMX - half, D_MODEL), F32)

        out_copy(m).start()

        @pl.when(m == n_used - 1)
        def _():
            for back in range(OUT_SLOTS):
                pl.when(m - back >= 0)(out_copy(m - back).wait)


def _experts(xs, plan, wg, wu, wd):
    n_tiles = xs.shape[0] // TMX
    hbm = pl.BlockSpec(memory_space=pl.ANY)
    up_shape, down_shape = (D_MODEL, D_EXPERT), (D_EXPERT, D_MODEL)
    grid_spec = pltpu.PrefetchScalarGridSpec(
        num_scalar_prefetch=6,
        grid=(n_tiles,),
        in_specs=[hbm, hbm, hbm, hbm],
        out_specs=hbm,
        scratch_shapes=[pltpu.VMEM((2,) + up_shape, F32), pltpu.VMEM((2,) + up_shape, F32),
                        pltpu.VMEM((2,) + down_shape, F32),
                        pltpu.VMEM(up_shape, BF16), pltpu.VMEM(up_shape, BF16), pltpu.VMEM(down_shape, BF16),
                        pltpu.VMEM((TMX, D_EXPERT), F32), pltpu.VMEM((TMX, D_EXPERT), F32),
                        pltpu.VMEM((IN_SLOTS, TMX, D_MODEL), F32), pltpu.VMEM((OUT_SLOTS, TMX, D_MODEL), F32),
                        pltpu.SemaphoreType.DMA((2, 3)), pltpu.SemaphoreType.DMA((IN_SLOTS,)),
                        pltpu.SemaphoreType.DMA((OUT_SLOTS,))],
    )
    return pl.pallas_call(
        _experts_kernel,
        grid_spec=grid_spec,
        out_shape=jax.ShapeDtypeStruct(xs.shape, F32),
        input_output_aliases={6: 0},
        compiler_params=pltpu.CompilerParams(
            dimension_semantics=("arbitrary",), vmem_limit_bytes=VMEM_LIMIT),
        name="experts",
    )(plan["tile_expert"], plan["n_used"], plan["tile_first"], plan["tile_slot"], plan["tile_next"], plan["tile_rows"],
      xs, wg, wu, wd)


def _combine_kernel(local_start_ref, n_big_ref, big_src_ref, big_dst_ref, n_small_ref, small_src_ref, small_dst_ref,
                    ys_ref, ri_ref, rw_ref, x1_ref, mod_ref, gain_ref, o_ref, buf_ref, sem_ref,
                    *, r_loc, tiles_per_batch):
    b = pl.program_id(0)
    nb = pl.num_programs(0)
    per_step = buf_ref.shape[0] // 2
    tm = x1_ref.shape[0] // per_step
    gen = b % 2
    fetch_tile, wait_tile = _run_pieces(
        (n_big_ref, big_src_ref, big_dst_ref, n_small_ref, small_src_ref, small_dst_ref), buf_ref, ys_ref, sem_ref,
        to_sorted=False)

    def fetch(step, g):
        for t in range(per_step):
            fetch_tile(step * per_step + t, g * per_step + t)

    @pl.when(b == 0)
    def _():
        buf_ref[...] = jnp.zeros_like(buf_ref)
        fetch(0, 0)

    @pl.when(b + 1 < nb)
    def _():
        fetch(b + 1, 1 - gen)

    for t in range(per_step):
        wait_tile(b * per_step + t, gen * per_step + t)

    tiles = [b * per_step + t for t in range(per_step)]
    cols = [slice(t * tm, (t + 1) * tm) for t in range(per_step)]
    slots = [_local_slots(ri_ref, local_start_ref, tile, c) for tile, c in zip(tiles, cols)]
    rows = lax.broadcasted_iota(jnp.int32, (r_loc, tm), 0)
    hits = [(rows == s1, rows == s2) for s1, s2 in slots]
    w_rows = [jnp.sum(jnp.where(h1, rw_ref[0:1, c], jnp.where(h2, rw_ref[1:2, c], 0.0)), axis=1, keepdims=True)
              for (h1, h2), c in zip(hits, cols)]
    perms = [jnp.where(h1 | h2, 1.0, 0.0).astype(BF16) for h1, h2 in hits]
    yws = [(buf_ref[gen * per_step + t] * w_rows[t]).astype(BF16) for t in range(per_step)]
    moes = [_dot_tn(perm, yw) for perm, yw in zip(perms, yws)]
    batch = b // tiles_per_batch
    gate2 = mod_ref[pl.ds(batch, 1), 5 * D_MODEL:6 * D_MODEL]
    for c, moe in zip(cols, moes):
        x2 = x1_ref[c, :] + gate2 * moe
        ms = jnp.mean(x2 * x2, axis=-1, keepdims=True)
        o_ref[c, :] = x2 * lax.rsqrt(ms + EPS) * gain_ref[...]


def _combine(ys, ri, rw, x1, mod, gain, plan, seq, tm):
    tokens = x1.shape[0]
    r_loc = _local_rows(tm)
    per_step = COMBINE_TILES_PER_STEP if seq % (COMBINE_TILES_PER_STEP * tm) == 0 else 1
    rows = per_step * tm
    row_spec = pl.BlockSpec((8, rows), lambda i, *_: (0, i))
    tok_spec = pl.BlockSpec((rows, D_MODEL), lambda i, *_: (i, 0))
    full = lambda a: pl.BlockSpec(a.shape, lambda i, *_: (0,) * a.ndim)
    prefetch = [plan["local_start"]] + [plan[k] for k in PIECE_TABLES]
    grid_spec = pltpu.PrefetchScalarGridSpec(
        num_scalar_prefetch=len(prefetch),
        grid=(tokens // rows,),
        in_specs=[pl.BlockSpec(memory_space=pl.ANY), row_spec, row_spec, tok_spec, full(mod), full(gain)],
        out_specs=tok_spec,
        scratch_shapes=[pltpu.VMEM((2 * per_step, r_loc, D_MODEL), F32), pltpu.SemaphoreType.DMA((2 * per_step,))],
    )
    return pl.pallas_call(
        functools.partial(_combine_kernel, r_loc=r_loc, tiles_per_batch=seq // rows),
        grid_spec=grid_spec,
        out_shape=jax.ShapeDtypeStruct((tokens, D_MODEL), F32),
        compiler_params=pltpu.CompilerParams(
            dimension_semantics=("arbitrary",), vmem_limit_bytes=VMEM_LIMIT),
        name="combine",
    )(*prefetch, ys, ri, rw, x1, mod, gain)


def _rotary_tables(seq):
    half = RET_HEAD_DIM // 2
    inv_freq = 1.0 / (ROPE_BASE ** (np.arange(half, dtype=np.float64) / half))
    ang = np.arange(seq, dtype=np.float64)[:, None] * inv_freq[None, :]
    cos = np.cos(ang)
    sin = np.sin(ang)
    f32 = lambda a: jnp.asarray(a.astype(np.float32))
    return f32(np.tile(cos, (1, 4))), f32(np.concatenate([-sin, sin, -sin, sin], axis=1))


def _pick_tile(n, pref):
    t = min(n, pref)
    assert n % t == 0, (n, t)
    return t


def kernel(x, c, ada_w, ada_b, norm1_gain, norm2_gain, w_in, w_out, ret_gn_gain, lam_q1, lam_k1, lam_q2,
           lam_k2, diff_subln_gain, w_group, b_group, w_expert, b_expert, w_gate, w_up, w_down, final_gain):
    batch, seq, d = x.shape
    assert d == D_MODEL and batch <= 8 and ada_w.shape[0] == 1
    layer = 0
    lambda_init = 0.8 - 0.6 * math.exp(-0.3 * layer)
    tokens = batch * seq
    x2 = x.reshape(tokens, d)
    tm = _pick_tile(seq, 512)

    c_pad = jnp.zeros((8, d), F32).at[:batch].set(c)
    mod = _adaln(c_pad, ada_w[layer], ada_b[layer].reshape(1, -1))

    cos_t, sin_t = _rotary_tables(seq)
    rq, rk, rv, rg, dq, dk, dvt = _inproj(
        x2, mod, norm1_gain[layer].reshape(1, d), w_in[layer].astype(BF16), cos_t, sin_t, seq, tm)

    ret_out = _retention(rq, rk, rv, rg, ret_gn_gain[layer].reshape(1, RET_WIDTH), batch, seq,
                         _pick_tile(seq, 256))
    diff_out = _diffattn(
        dq, dk, dvt, lam_q1[layer].reshape(1, -1), lam_k1[layer].reshape(1, -1), lam_q2[layer].reshape(1, -1),
        lam_k2[layer].reshape(1, -1), diff_subln_gain[layer].reshape(-1, 1), batch, seq, lambda_init,
        _pick_tile(seq, 2048), 2 * QUERY_CHUNK)

    w_router = jnp.concatenate(
        [w_group[layer].T, jnp.zeros((8 - N_GROUPS, d), F32), w_expert[layer].reshape(d, N_EXPERTS).T], axis=0)
    b_router = jnp.concatenate(
        [b_group[layer], jnp.zeros((8 - N_GROUPS,), F32), b_expert[layer].reshape(N_EXPERTS)]).reshape(-1, 1)
    wr_hi = w_router.astype(BF16)
    wr_lo = (w_router - wr_hi.astype(F32)).astype(BF16)
    x1, h2, ri, rw, cnt = _outproj(ret_out, diff_out, x2, mod, norm2_gain[layer].reshape(1, d),
                                   w_out[layer].astype(BF16), jnp.concatenate([wr_hi, wr_lo], axis=0), b_router,
                                   seq, tm)

    plan = _dispatch_plan(cnt[:, :, 0], tokens, tm)
    xs = _dispatch(h2, ri, plan, tm)
    ys = _experts(xs, plan, w_gate[layer].reshape(N_EXPERTS, d, D_EXPERT),
                  w_up[layer].reshape(N_EXPERTS, d, D_EXPERT), w_down[layer].reshape(N_EXPERTS, D_EXPERT, d))
    out = _combine(ys, ri, rw, x1, mod, final_gain.reshape(1, d), plan, seq, tm)
    return out.reshape(batch, seq, d)
```

```python
import functools
import math

import jax
import jax.numpy as jnp
import numpy as np
from jax import lax
from jax.experimental import pallas as pl
from jax.experimental.pallas import tpu as pltpu

F32 = jnp.float32
BF16 = jnp.bfloat16

D_MODEL = 1024
RET_HEAD_DIM = 64
RET_WIDTH = 512
RET_HEADS = 8
RET_PAIRS = RET_HEADS // 2
DIFF_QK_DIM = 64
DIFF_V_DIM = 128
DIFF_HEADS = 4
DIFF_WIDTH = 512
N_GROUPS = 4
EXPERTS_PER_GROUP = 8
N_EXPERTS = N_GROUPS * EXPERTS_PER_GROUP
D_EXPERT = 512
N_MOD = 6
ROPE_BASE = 10000.0
EPS = 1e-6
LANES = 128
ROUTER_ROWS = 8 + N_EXPERTS
VMEM_LIMIT = 56 * 1024 * 1024


def _dot(a, b):
    return jnp.dot(a, b, preferred_element_type=F32)


def _dot_nt(a, b):
    return lax.dot_general(a, b, (((1,), (1,)), ((), ())), preferred_element_type=F32)


def _dot_tn(a, b):
    return lax.dot_general(a, b, (((0,), (0,)), ((), ())), preferred_element_type=F32)


def _split_bf16(x):
    hi = x.astype(BF16)
    lo = (x - hi.astype(F32)).astype(BF16)
    return hi, lo


def _silu(x):
    return x / (1.0 + jnp.exp(-x))


def _adaln_kernel(c_ref, w_ref, b_ref, o_ref):
    ca = _silu(c_ref[...])
    c_hi, c_lo = _split_bf16(ca)
    w_hi, w_lo = _split_bf16(w_ref[...])
    o_ref[...] = _dot(c_hi, w_hi) + _dot(c_lo, w_hi) + _dot(c_hi, w_lo) + b_ref[...]


def _adaln(c_pad, ada_w, ada_b):
    n_out = ada_w.shape[1]
    tn = D_MODEL
    return pl.pallas_call(
        _adaln_kernel,
        grid=(n_out // tn,),
        in_specs=[
            pl.BlockSpec((8, D_MODEL), lambda j: (0, 0)),
            pl.BlockSpec((D_MODEL, tn), lambda j: (0, j)),
            pl.BlockSpec((1, tn), lambda j: (0, j)),
        ],
        out_specs=pl.BlockSpec((8, tn), lambda j: (0, j)),
        out_shape=jax.ShapeDtypeStruct((8, n_out), F32),
        compiler_params=pltpu.CompilerParams(vmem_limit_bytes=VMEM_LIMIT),
        name="adaln",
    )(c_pad, ada_w, ada_b)


def _norm_modulate(x, gain, shift, scale):
    ms = jnp.mean(x * x, axis=-1, keepdims=True)
    y = x * lax.rsqrt(ms + EPS) * gain
    return y * (1.0 + scale) + shift


def _rotary_slab(x, cos, sin_signed, lane_lo):
    swapped = jnp.where(lane_lo, pltpu.roll(x, 96, 1), pltpu.roll(x, 32, 1))
    return x * cos + swapped * sin_signed


def _inproj_kernel(x_ref, mod_ref, gain_ref, w_ref, cos_ref, sin_ref,
                   rq_ref, rk_ref, rv_ref, rg_ref, dq_ref, dk_ref, dvt_ref, *, tiles_per_batch):
    b = pl.program_id(0) // tiles_per_batch
    shift = mod_ref[pl.ds(b, 1), 0:D_MODEL]
    scale = mod_ref[pl.ds(b, 1), D_MODEL:2 * D_MODEL]
    h = _norm_modulate(x_ref[...], gain_ref[...], shift, scale).astype(BF16)
    cos = cos_ref[...]
    sin = sin_ref[...]
    lane = lax.broadcasted_iota(jnp.int32, cos.shape, 1)
    lane_lo = (lane % 64) < 32

    def proj(chunk):
        return _dot(h, w_ref[:, chunk * RET_WIDTH:(chunk + 1) * RET_WIDTH])

    def rotary(acc, out_ref, post_scale):
        for s in range(RET_WIDTH // LANES):
            sl = slice(s * LANES, (s + 1) * LANES)
            out_ref[:, sl] = (_rotary_slab(acc[:, sl], cos, sin, lane_lo) * post_scale).astype(BF16)

    rotary(proj(0), rq_ref, 1.0)
    rotary(proj(1), rk_ref, RET_HEAD_DIM ** -0.5)
    rv_ref[...] = proj(2).astype(BF16)
    rg_ref[...] = _silu(proj(3)).astype(BF16)
    dq_ref[...] = (proj(4) * (DIFF_QK_DIM ** -0.5 * math.log2(math.e))).astype(BF16)
    dk_ref[...] = proj(5).astype(BF16)
    dvt_ref[...] = proj(6).T.astype(BF16)


def _inproj(x2, mod, gain, w_in, cos_t, sin_t, seq, tm):
    tokens = x2.shape[0]
    tiles_per_batch = seq // tm
    tok_spec = lambda w: pl.BlockSpec((tm, w), lambda i: (i, 0))
    tab_spec = pl.BlockSpec((tm, LANES), lambda i: (i % tiles_per_batch, 0))
    full = lambda a: pl.BlockSpec(a.shape, lambda i: (0,) * a.ndim)
    out = jax.ShapeDtypeStruct((tokens, RET_WIDTH), BF16)
    return pl.pallas_call(
        functools.partial(_inproj_kernel, tiles_per_batch=tiles_per_batch),
        grid=(tokens // tm,),
        in_specs=[tok_spec(D_MODEL), full(mod), full(gain), full(w_in), tab_spec, tab_spec],
        out_specs=[tok_spec(RET_WIDTH)] * 6 + [pl.BlockSpec((DIFF_WIDTH, tm), lambda i: (0, i))],
        out_shape=[out] * 6 + [jax.ShapeDtypeStruct((DIFF_WIDTH, tokens), BF16)],
        compiler_params=pltpu.CompilerParams(
            dimension_semantics=("parallel",), vmem_limit_bytes=VMEM_LIMIT),
        name="inproj",
    )(x2, mod, gain, w_in, cos_t, sin_t)


RET_SEQ_GROUP = 4


def _retention_kernel(q_ref, k_ref, v_ref, g_ref, dec_ref, qdec_ref, kdec_ref, rdec_ref,
                      bmask_ref, gmean_ref, gain_ref, o_ref, state_ref, *, chunk):
    @pl.when(pl.program_id(1) == 0)
    def _():
        state_ref[...] = jnp.zeros_like(state_ref)

    lane = lax.broadcasted_iota(jnp.int32, (chunk, LANES), 1)
    first_head = lane < RET_HEAD_DIM
    gmean = gmean_ref[...]
    bmask = bmask_ref[...]
    units = [(s, p) for s in range(q_ref.shape[0]) for p in range(RET_PAIRS)]
    sl = lambda p: slice(p * LANES, (p + 1) * LANES)
    q = [q_ref[s, :, sl(p)] for s, p in units]
    k = [k_ref[s, :, sl(p)] for s, p in units]
    v = [v_ref[s, :, sl(p)] for s, p in units]
    zero = jnp.zeros_like(q[0])
    q_stack = [jnp.concatenate([jnp.where(first_head, qu, zero), jnp.where(first_head, zero, qu)], axis=0)
               for qu in q]
    scores = [(_dot_nt(q_stack[n], k[n]) * dec_ref[p]).astype(BF16) for n, (_, p) in enumerate(units)]
    state = [state_ref[n] for n in range(len(units))]
    cross = [_dot(q[n], state[n].astype(BF16)) * qdec_ref[:, sl(p)] for n, (_, p) in enumerate(units)]
    k_dec = [(k[n].astype(F32) * kdec_ref[:, sl(p)]).astype(BF16) for n, (_, p) in enumerate(units)]
    for n, (_, p) in enumerate(units):
        state_ref[n] = state[n] * rdec_ref[p] + _dot_tn(k_dec[n], v[n]) * bmask
    intra2 = [_dot(scores[n], v[n]) for n in range(len(units))]
    y = [jnp.where(first_head, intra2[n][:chunk], intra2[n][chunk:]) + cross[n] for n in range(len(units))]
    seg_mean = lambda x: _dot(jnp.concatenate(_split_bf16(x), axis=1), gmean)
    mu = [seg_mean(yu) for yu in y]
    d = [yu - mu_u for yu, mu_u in zip(y, mu)]
    var = [seg_mean(du * du) for du in d]
    for n, (s, p) in enumerate(units):
        yn = d[n] * lax.rsqrt(var[n] + EPS) * gain_ref[:, sl(p)]
        o_ref[s, :, sl(p)] = (g_ref[s, :, sl(p)].astype(F32) * yn).astype(BF16)


def _retention_tables(chunk):
    heads = np.arange(RET_HEADS, dtype=np.float64)
    log_gamma = np.log(1.0 - np.exp2(-5.0 - heads))
    idx = np.arange(chunk)
    rel = (idx[:, None] - idx[None, :]).astype(np.float64)
    decay = np.where(rel[None] >= 0, np.exp(log_gamma[:, None, None] * np.maximum(rel, 0.0)[None]), 0.0)
    dec2 = decay.reshape(RET_PAIRS, 2 * chunk, chunk)
    lane_lg = np.repeat(log_gamma, RET_HEAD_DIM)
    qdec = np.exp(lane_lg[None, :] * (idx + 1)[:, None])
    kdec = np.exp(lane_lg[None, :] * (chunk - 1 - idx)[:, None])
    rdec = np.exp(lane_lg * chunk).reshape(RET_PAIRS, LANES, 1) * np.ones((1, 1, LANES))
    blk = np.arange(LANES) // RET_HEAD_DIM
    bmask = (blk[:, None] == blk[None, :]).astype(np.float64)
    f32 = lambda a: jnp.asarray(a.astype(np.float32))
    gmean2 = np.concatenate([bmask, bmask], axis=0) / RET_HEAD_DIM
    return f32(dec2), f32(qdec), f32(kdec), f32(rdec), f32(bmask), f32(gmean2).astype(BF16)


def _retention(rq, rk, rv, rg, gn_gain, batch, seq, chunk):
    nc = seq // chunk
    group = RET_SEQ_GROUP if batch % RET_SEQ_GROUP == 0 else 1
    dec2, qdec, kdec, rdec, bmask, gmean = _retention_tables(chunk)
    tok_spec = pl.BlockSpec((group, chunk, RET_WIDTH), lambda b, n: (b, n, 0))
    full = lambda a: pl.BlockSpec(a.shape, lambda b, n: (0,) * a.ndim)
    by_seq = lambda a: a.reshape(batch, seq, RET_WIDTH)
    out = pl.pallas_call(
        functools.partial(_retention_kernel, chunk=chunk),
        grid=(batch // group, nc),
        in_specs=[tok_spec] * 4 + [full(dec2), full(qdec), full(kdec), full(rdec), full(bmask),
                                   full(gmean), full(gn_gain)],
        out_specs=tok_spec,
        out_shape=jax.ShapeDtypeStruct((batch, seq, RET_WIDTH), BF16),
        scratch_shapes=[pltpu.VMEM((group * RET_PAIRS, LANES, LANES), F32)],
        compiler_params=pltpu.CompilerParams(
            dimension_semantics=("parallel", "arbitrary"), vmem_limit_bytes=VMEM_LIMIT),
        name="retention",
    )(by_seq(rq), by_seq(rk), by_seq(rv), by_seq(rg), dec2, qdec, kdec, rdec, bmask, gmean, gn_gain)
    return out.reshape(batch * seq, RET_WIDTH)


NEG_BIG = -1e30


V_EXT_ROWS = DIFF_V_DIM + 16
QUERY_CHUNK = 256
SCORES_AHEAD_FULL = 3
SCORES_AHEAD_DIAG = 3
STAGE_SLOTS = 16


def _diag_chunks(tq, tk, d):
    assert tk == 2 * QUERY_CHUNK
    per_softmax = tq // QUERY_CHUNK
    out = []
    for c in range(2 * per_softmax):
        q0 = (c % per_softmax) * QUERY_CHUNK
        if q0 + QUERY_CHUNK - 1 < d * tk:
            continue
        kind = "full" if q0 >= (d + 1) * tk else ("tri" if q0 == d * tk else "low_tri")
        out.append((c, kind))
    return out


def _accumulate(acc_ref, cs, alpha, pv):
    acc_ref[:, cs] = alpha * acc_ref[:, cs] + pv


def _diffattn_kernel(q_ref, k_ref, vt_ref, lq1_ref, lk1_ref, lq2_ref, lk2_ref, gain_ref, bias_ref, o_ref,
                     qs_ref, vext_ref, m_ref, acc_ref, stage_ref, *, tq, tk, lambda_init):
    i = pl.program_id(2)
    nk = vext_ref.shape[0]

    @pl.when(i == 0)
    def _():
        for j in range(nk):
            vext_ref[j, 0:DIFF_V_DIM, :] = vt_ref[:, j * tk:(j + 1) * tk]
            vext_ref[j, DIFF_V_DIM:V_EXT_ROWS, :] = jnp.ones((V_EXT_ROWS - DIFF_V_DIM, tk), BF16)

    q = q_ref[...]
    lane = lax.broadcasted_iota(jnp.int32, q.shape, 1)
    zero = jnp.zeros_like(q)
    qs_ref[0:tq, :] = jnp.where(lane < DIFF_QK_DIM, q, zero)
    qs_ref[tq:2 * tq, :] = jnp.where(lane < DIFF_QK_DIM, zero, q)
    m_ref[...] = jnp.full_like(m_ref, NEG_BIG)
    acc_ref[...] = jnp.zeros_like(acc_ref)

    def step(work, n_ahead):
        chunk = lambda c: slice(c * QUERY_CHUNK, (c + 1) * QUERY_CHUNK)

        def scores(j, c, kind):
            n_keys = QUERY_CHUNK if kind == "tri" else tk
            start = pl.multiple_of(j * tk, tk)
            return _dot_nt(k_ref[pl.ds(start, n_keys), :], qs_ref[chunk(c), :])

        ahead = [scores(*work[n]) for n in range(min(n_ahead, len(work)))]
        pending = None
        for n, (j, c, kind) in enumerate(work):
            cs = chunk(c)
            st = ahead.pop(0)
            if n + n_ahead < len(work):
                ahead.append(scores(*work[n + n_ahead]))
            slot = n % stage_ref.shape[0]
            n_keys = st.shape[0]
            stage_ref[slot, 0:n_keys, :] = st
            if kind == "full":
                st = stage_ref[slot]
            else:
                causal = stage_ref[slot, n_keys - QUERY_CHUNK:n_keys, :] + bias_ref[...]
                st = causal if kind == "tri" else jnp.concatenate(
                    [stage_ref[slot, 0:n_keys - QUERY_CHUNK, :], causal], axis=0)
            m_old = m_ref[:, cs]
            m_new = jnp.maximum(m_old, jnp.max(st, axis=0, keepdims=True))
            alpha = jnp.exp2(m_old - m_new)
            p = jnp.exp2(st - m_new).astype(BF16)
            m_ref[:, cs] = m_new
            pv = _dot(vext_ref[j, :, 0:st.shape[0]], p)
            if pending is not None:
                pending()
            pending = functools.partial(_accumulate, acc_ref, cs, alpha, pv)
        pending()

    tiles_per_q = tq // tk
    n_chunks = 2 * tq // QUERY_CHUNK

    def full_tiles(it):
        return [(it * tiles_per_q + d, c, "full") for d in range(tiles_per_q) for c in range(n_chunks)]

    lax.fori_loop(0, i, lambda it, c: (step(full_tiles(it), SCORES_AHEAD_FULL), c)[1], 0)
    kind_order = {"full": 0, "low_tri": 1, "tri": 2}
    diag = lambda it: [(it * tiles_per_q + d, c, kind) for d in range(tiles_per_q)
                       for c, kind in sorted(_diag_chunks(tq, tk, d), key=lambda ck: kind_order[ck[1]])]
    lax.fori_loop(i, i + 1, lambda it, c: (step(diag(it), SCORES_AHEAD_DIAG), c)[1], 0)

    lam = (jnp.exp(jnp.sum(lq1_ref[...] * lk1_ref[...], axis=-1, keepdims=True))
           - jnp.exp(jnp.sum(lq2_ref[...] * lk2_ref[...], axis=-1, keepdims=True)) + lambda_init)
    acc = acc_ref[...]
    o2 = acc[0:DIFF_V_DIM, :] * (1.0 / acc[DIFF_V_DIM:DIFF_V_DIM + 1, :])
    ot = o2[:, :tq] - lam * o2[:, tq:]
    ms = jnp.mean(ot * ot, axis=0, keepdims=True)
    ot = ot * lax.rsqrt(ms + EPS) * gain_ref[...] * (1.0 - lambda_init)
    o_ref[...] = ot.T.astype(BF16)


def _diffattn(dq, dk, dvt, lam_q1, lam_k1, lam_q2, lam_k2, gain, batch, seq, lambda_init, tq, tk):
    nq = seq // tq
    q_spec = pl.BlockSpec((tq, LANES), lambda b, h, i: (b * nq + i, h))
    k_spec = pl.BlockSpec((seq, LANES), lambda b, h, i: (b, h))
    vt_spec = pl.BlockSpec((DIFF_V_DIM, seq), lambda b, h, i: (h, b))
    vec = lambda a: pl.BlockSpec(a.shape, lambda b, h, i: (0, 0))
    key = np.arange(QUERY_CHUNK)[:, None]
    query = np.arange(QUERY_CHUNK)[None, :]
    bias = jnp.asarray(np.where(key <= query, 0.0, NEG_BIG), F32)
    return pl.pallas_call(
        functools.partial(_diffattn_kernel, tq=tq, tk=tk, lambda_init=lambda_init),
        grid=(batch, DIFF_HEADS, nq),
        in_specs=[q_spec, k_spec, vt_spec, vec(lam_q1), vec(lam_k1), vec(lam_q2), vec(lam_k2), vec(gain),
                  vec(bias)],
        out_specs=q_spec,
        out_shape=jax.ShapeDtypeStruct(dq.shape, BF16),
        scratch_shapes=[
            pltpu.VMEM((2 * tq, LANES), BF16),
            pltpu.VMEM((seq // tk, V_EXT_ROWS, tk), BF16),
            pltpu.VMEM((1, 2 * tq), F32),
            pltpu.VMEM((V_EXT_ROWS, 2 * tq), F32),
            pltpu.VMEM((STAGE_SLOTS, tk, QUERY_CHUNK), F32),
        ],
        compiler_params=pltpu.CompilerParams(
            dimension_semantics=("parallel", "parallel", "arbitrary"), vmem_limit_bytes=VMEM_LIMIT),
        name="diffattn",
    )(dq, dk, dvt, lam_q1, lam_k1, lam_q2, lam_k2, gain, bias)


def _route(logits):
    r = [logits[g:g + 1, :] for g in range(N_GROUPS)]
    gmax = jnp.maximum(jnp.maximum(r[0], r[1]), jnp.maximum(r[2], r[3]))
    g_idx = jnp.where(r[0] == gmax, 0, jnp.where(r[1] == gmax, 1, jnp.where(r[2] == gmax, 2, 3)))
    denom = sum(jnp.exp(rg - gmax) for rg in r)
    g_weight = 1.0 / denom
    sel = jnp.zeros((EXPERTS_PER_GROUP, logits.shape[1]), F32)
    for g in range(N_GROUPS):
        rows = logits[8 + g * EXPERTS_PER_GROUP:8 + (g + 1) * EXPERTS_PER_GROUP, :]
        sel = jnp.where(g_idx == g, rows, sel)
    eidx = lax.broadcasted_iota(jnp.int32, sel.shape, 0)
    v1 = jnp.max(sel, axis=0, keepdims=True)
    i1 = jnp.min(jnp.where(sel == v1, eidx, EXPERTS_PER_GROUP), axis=0, keepdims=True)
    sel2 = jnp.where(eidx == i1, -jnp.inf, sel)
    v2 = jnp.max(sel2, axis=0, keepdims=True)
    i2 = jnp.min(jnp.where(sel2 == v2, eidx, EXPERTS_PER_GROUP), axis=0, keepdims=True)
    e2 = jnp.exp(v2 - v1)
    w1 = g_weight / (1.0 + e2)
    w2 = g_weight * e2 / (1.0 + e2)
    return g_idx, i1, i2, w1, w2


OUTPROJ_PARTS = 2
OUTPROJ_TILES_PER_STEP = 2


def _outproj_kernel(ret_ref, diff_ref, x_ref, mod_ref, gain_ref, wo_ref, wr_ref, br_ref, tri_ref,
                    x1_ref, h2_ref, ri_ref, rw_ref, cnt_ref, *, tiles_per_batch):
    b = pl.program_id(0) // tiles_per_batch
    gate1 = mod_ref[pl.ds(b, 1), 2 * D_MODEL:3 * D_MODEL]
    shift = mod_ref[pl.ds(b, 1), 3 * D_MODEL:4 * D_MODEL]
    scale = mod_ref[pl.ds(b, 1), 4 * D_MODEL:5 * D_MODEL]
    wr = wr_ref[...]
    rows = x_ref.shape[0]
    tm = tri_ref.shape[0]
    n_parts = OUTPROJ_PARTS * rows // tm
    parts = [slice(n * rows // n_parts, (n + 1) * rows // n_parts) for n in range(n_parts)]
    mix = [_dot(jnp.concatenate([ret_ref[r, :], diff_ref[r, :]], axis=1), wo_ref[...]) for r in parts]
    for r, m in zip(parts, mix):
        x1_ref[r, :] = x_ref[r, :] + gate1 * m
    h_split = [_split_bf16(_norm_modulate(x1_ref[r, :], gain_ref[...], shift, scale)) for r in parts]
    for r, (h_hi, _) in zip(parts, h_split):
        h2_ref[r, :] = h_hi
    by_hi = [_dot_nt(wr, h_hi) for h_hi, _ in h_split]
    by_lo = [_dot_nt(wr[:ROUTER_ROWS], h_lo) for _, h_lo in h_split]
    logits = [a[:ROUTER_ROWS] + a[ROUTER_ROWS:] + c + br_ref[...] for a, c in zip(by_hi, by_lo)]
    routed = [_route(lg) for lg in logits]
    g_idx, i1, i2, w1, w2 = [jnp.concatenate([rt[n] for rt in routed], axis=1) for n in range(5)]
    logits = jnp.concatenate(logits, axis=1)
    e1 = g_idx * EXPERTS_PER_GROUP + i1
    e2 = g_idx * EXPERTS_PER_GROUP + i2
    eidx = lax.broadcasted_iota(jnp.int32, (N_EXPERTS, logits.shape[1]), 0)
    hit1 = eidx == e1
    hit2 = eidx == e2
    onehot = jnp.where(hit1 | hit2, 1.0, 0.0)
    tiles = [slice(n * tm, (n + 1) * tm) for n in range(rows // tm)]
    before = jnp.concatenate([_dot(onehot[:, t].astype(BF16), tri_ref[...]) for t in tiles], axis=1)
    r1 = jnp.sum(jnp.where(hit1, before, 0.0), axis=0, keepdims=True)
    r2 = jnp.sum(jnp.where(hit2, before, 0.0), axis=0, keepdims=True)
    zi = jnp.zeros_like(e1)
    ri_ref[...] = jnp.concatenate([e1, e2, r1.astype(jnp.int32), r2.astype(jnp.int32), zi, zi, zi, zi], axis=0)
    zf = jnp.zeros_like(w1)
    rw_ref[...] = jnp.concatenate([w1, w2, zf, zf, zf, zf, zf, zf], axis=0)
    for n, t in enumerate(tiles):
        counts = jnp.sum(onehot[:, t], axis=1, keepdims=True)
        cnt_ref[n] = jnp.broadcast_to(counts, (N_EXPERTS, LANES)).astype(jnp.int32)


def _outproj(ret_out, diff_out, x2, mod, gain, w_out, wr, br, seq, tm):
    tokens = x2.shape[0]
    n_tiles = tokens // tm
    per_step = OUTPROJ_TILES_PER_STEP if seq % (OUTPROJ_TILES_PER_STEP * tm) == 0 else 1
    rows = per_step * tm
    tri = jnp.asarray(np.arange(tm)[:, None] < np.arange(tm)[None, :], BF16)
    tok_spec = lambda w: pl.BlockSpec((rows, w), lambda i: (i, 0))
    row_spec = pl.BlockSpec((8, rows), lambda i: (0, i))
    full = lambda a: pl.BlockSpec(a.shape, lambda i: (0,) * a.ndim)
    return pl.pallas_call(
        functools.partial(_outproj_kernel, tiles_per_batch=seq // rows),
        grid=(tokens // rows,),
        in_specs=[tok_spec(RET_WIDTH), tok_spec(DIFF_WIDTH), tok_spec(D_MODEL), full(mod), full(gain),
                  full(w_out), full(wr), full(br), full(tri)],
        out_specs=[tok_spec(D_MODEL), tok_spec(D_MODEL), row_spec, row_spec,
                   pl.BlockSpec((per_step, N_EXPERTS, LANES), lambda i: (i, 0, 0))],
        out_shape=[jax.ShapeDtypeStruct((tokens, D_MODEL), F32),
                   jax.ShapeDtypeStruct((tokens, D_MODEL), BF16),
                   jax.ShapeDtypeStruct((8, tokens), jnp.int32),
                   jax.ShapeDtypeStruct((8, tokens), F32),
                   jax.ShapeDtypeStruct((n_tiles, N_EXPERTS, LANES), jnp.int32)],
        compiler_params=pltpu.CompilerParams(
            dimension_semantics=("parallel",), vmem_limit_bytes=VMEM_LIMIT),
        name="outproj",
    )(ret_out, diff_out, x2, mod, gain, w_out, wr, br, tri)


CHUNK = 8
BIG_PIECE = 4 * CHUNK
MAX_SMALL_PIECES = N_EXPERTS * (BIG_PIECE // CHUNK - 1)
PIECE_TABLES = ("n_big", "big_src", "big_dst", "n_small", "small_src", "small_dst")
DISPATCH_TILES_PER_STEP = 2
COMBINE_TILES_PER_STEP = 2
TMX = 512
IN_SLOTS = 3
OUT_SLOTS = 2


def _local_rows(tm):
    rows = 2 * tm + N_EXPERTS * (CHUNK - 1)
    return (rows + 15) // 16 * 16


def _max_big_pieces(tm):
    return _local_rows(tm) // BIG_PIECE


def _sorted_rows_alloc(tokens, tm):
    worst = 2 * tokens + (tokens // tm) * N_EXPERTS * (CHUNK - 1) + N_EXPERTS * (TMX - CHUNK)
    return (worst + TMX - 1) // TMX * TMX


def _dispatch_plan(cnt, tokens, tm):
    i32 = jnp.int32
    pad = (cnt + CHUNK - 1) // CHUNK * CHUNK
    local_end = jnp.cumsum(pad, axis=1)
    local_start = local_end - pad
    seg_rows = jnp.sum(pad, axis=0)
    seg_pad = (seg_rows + TMX - 1) // TMX * TMX
    seg_end = jnp.cumsum(seg_pad)
    seg_start = seg_end - seg_pad
    run_dst = seg_start[None, :] + jnp.cumsum(pad, axis=0) - pad

    def pieces(count, offset, size, max_n):
        end = jnp.cumsum(count, axis=1)
        start = end - count
        k = jnp.arange(max_n, dtype=i32)[None, :, None]
        owns = (start[:, None, :] <= k) & (k < end[:, None, :])
        within = size * (k - start[:, None, :]) + offset[:, None, :]
        src = jnp.sum(jnp.where(owns, local_start[:, None, :] + within, 0), axis=-1)
        dst = jnp.sum(jnp.where(owns, run_dst[:, None, :] + within, 0), axis=-1)
        return end[:, -1].astype(i32), src.reshape(-1).astype(i32), dst.reshape(-1).astype(i32)

    n_big, big_src, big_dst = pieces(pad // BIG_PIECE, jnp.zeros_like(pad), BIG_PIECE, _max_big_pieces(tm))
    n_small, small_src, small_dst = pieces(pad % BIG_PIECE // CHUNK, pad // BIG_PIECE * BIG_PIECE, CHUNK,
                                           MAX_SMALL_PIECES)
    m = TMX * jnp.arange(_sorted_rows_alloc(tokens, tm) // TMX, dtype=i32)
    tile_expert = jnp.minimum(jnp.sum(seg_end[None, :] <= m[:, None], axis=-1), N_EXPERTS - 1)
    towns = (seg_start[None, :] <= m[:, None]) & (m[:, None] < seg_end[None, :])
    used = seg_pad > 0
    parity = (jnp.cumsum(used) - used) % 2
    eids = jnp.arange(N_EXPERTS, dtype=i32)
    later_used = (eids[None, :] > eids[:, None]) & used[None, :]
    next_used = jnp.min(jnp.where(later_used, eids[None, :], N_EXPERTS), axis=1)
    next_used = jnp.where(next_used == N_EXPERTS, -1, next_used)
    pick = lambda per_expert: jnp.sum(jnp.where(towns, per_expert[None, :], 0), axis=-1)
    tile_first = jnp.sum(jnp.where(towns & (seg_start[None, :] == m[:, None]), 1, 0), axis=-1)
    tile_next = jnp.where(jnp.any(towns, axis=-1), pick(next_used), -1)
    return dict(
        tile_first=tile_first.astype(i32),
        tile_slot=pick(parity).astype(i32),
        tile_next=tile_next.astype(i32),
        tile_rows=jnp.clip(pick(seg_start + seg_rows) - m, 0, TMX).astype(i32),
        local_start=local_start.reshape(-1).astype(i32),
        n_big=n_big, big_src=big_src, big_dst=big_dst,
        n_small=n_small, small_src=small_src, small_dst=small_dst,
        tail_base=(seg_start + seg_rows).astype(i32),
        tail_rows=(seg_pad - seg_rows).astype(i32),
        tile_expert=tile_expert.astype(i32),
        n_used=(seg_end[-1:] // TMX).astype(i32),
    )


WAIT_UNROLL = 8


def _wait_times(copy, n):
    lax.fori_loop(0, n // WAIT_UNROLL, lambda i, c: ([copy.wait() for _ in range(WAIT_UNROLL)], c)[1], 0)
    lax.fori_loop(0, n % WAIT_UNROLL, lambda i, c: (copy.wait(), c)[1], 0)


def _for_each(n, body, unroll=4):
    main = n // unroll
    lax.fori_loop(0, main, lambda i, c: ([body(i * unroll + u) for u in range(unroll)], c)[1], 0)
    lax.fori_loop(main * unroll, n, lambda j, c: (body(j), c)[1], 0)


def _local_slots(ri_ref, local_start_ref, tile, cols=slice(None)):
    e1, e2 = ri_ref[0:1, cols], ri_ref[1:2, cols]
    s1, s2 = ri_ref[2:3, cols], ri_ref[3:4, cols]
    for e in range(N_EXPERTS):
        start = local_start_ref[tile * N_EXPERTS + e]
        s1 = s1 + jnp.where(e1 == e, start, 0)
        s2 = s2 + jnp.where(e2 == e, start, 0)
    return s1, s2


def _run_pieces(piece_refs, local_ref, sorted_ref, sem_ref, to_sorted):
    n_big_ref, big_src_ref, big_dst_ref, n_small_ref, small_src_ref, small_dst_ref = piece_refs
    r_loc = local_ref.shape[1]
    kinds = [(BIG_PIECE, n_big_ref, big_src_ref, big_dst_ref, r_loc // BIG_PIECE),
             (CHUNK, n_small_ref, small_src_ref, small_dst_ref, MAX_SMALL_PIECES)]

    def copy(sl, size, local_row, sorted_row):
        local = local_ref.at[sl, pl.ds(pl.multiple_of(local_row, CHUNK), size), :]
        srt = sorted_ref.at[pl.ds(pl.multiple_of(sorted_row, CHUNK), size), :]
        return pltpu.make_async_copy(local, srt, sem_ref.at[sl]) if to_sorted else \
            pltpu.make_async_copy(srt, local, sem_ref.at[sl])

    def start(tile, sl):
        for size, n_ref, src_ref, dst_ref, max_n in kinds:
            _for_each(n_ref[tile], lambda k: copy(sl, size, src_ref[tile * max_n + k], dst_ref[tile * max_n + k])
                      .start())

    def wait(tile, sl):
        for size, n_ref, _, _, _ in kinds:
            _wait_times(copy(sl, size, 0, 0), n_ref[tile])

    return start, wait


def _dispatch_kernel(local_start_ref, n_big_ref, big_src_ref, big_dst_ref, n_small_ref, small_src_ref, small_dst_ref,
                     tail_base_ref, tail_rows_ref, n_used_ref,
                     h_ref, ri_ref, xs_ref, buf_ref, zero_ref, sem_ref, tail_sem_ref, *, r_loc):
    b = pl.program_id(0)
    nb = pl.num_programs(0)
    per_step = buf_ref.shape[0] // 2
    tm = h_ref.shape[0] // per_step
    gen = b % 2
    start_runs, drain_tile = _run_pieces(
        (n_big_ref, big_src_ref, big_dst_ref, n_small_ref, small_src_ref, small_dst_ref), buf_ref, xs_ref, sem_ref,
        to_sorted=True)

    def drain(step, g):
        for t in range(per_step):
            drain_tile(step * per_step + t, g * per_step + t)

    @pl.when(b >= 2)
    def _():
        drain(b - 2, gen)

    tiles = [b * per_step + t for t in range(per_step)]
    cols = [slice(t * tm, (t + 1) * tm) for t in range(per_step)]
    slots = [_local_slots(ri_ref, local_start_ref, tile, c) for tile, c in zip(tiles, cols)]
    rows = lax.broadcasted_iota(jnp.int32, (r_loc, tm), 0)
    perms = [jnp.where((rows == s1) | (rows == s2), 1.0, 0.0).astype(BF16) for s1, s2 in slots]
    for t, (perm, c) in enumerate(zip(perms, cols)):
        buf_ref[gen * per_step + t] = _dot(perm, h_ref[c, :])
    for t, tile in enumerate(tiles):
        start_runs(tile, gen * per_step + t)

    def tail_pieces(e, act):
        n = tail_rows_ref[e]
        size = TMX // 2
        while size >= CHUNK:
            dst = pl.multiple_of(tail_base_ref[e] + (n & (-2 * size)), CHUNK)
            cp = pltpu.make_async_copy(zero_ref.at[pl.ds(0, size), :], xs_ref.at[pl.ds(dst, size), :],
                                       tail_sem_ref.at[0])
            pl.when((n & size) != 0)(functools.partial(act, cp))
            size //= 2

    def unused_tile_copy(m):
        dst = pl.multiple_of(m * TMX, TMX)
        return pltpu.make_async_copy(zero_ref, xs_ref.at[pl.ds(dst, TMX), :], tail_sem_ref.at[1])

    n_alloc = xs_ref.shape[0] // TMX

    @pl.when(b == 0)
    def _():
        zero_ref[...] = jnp.zeros_like(zero_ref)

    experts_per_step = pl.cdiv(N_EXPERTS, nb)
    lax.fori_loop(jnp.minimum(b * experts_per_step, N_EXPERTS), jnp.minimum((b + 1) * experts_per_step, N_EXPERTS),
                  lambda e, c: (tail_pieces(e, lambda cp: cp.start()), c)[1], 0)
    tiles_per_step = pl.cdiv(n_alloc, nb)
    first_unused = n_used_ref[0]
    lax.fori_loop(jnp.minimum(first_unused + b * tiles_per_step, n_alloc),
                  jnp.minimum(first_unused + (b + 1) * tiles_per_step, n_alloc),
                  lambda m, c: (unused_tile_copy(m).start(), c)[1], 0)

    @pl.when(b == nb - 1)
    def _():
        lax.fori_loop(0, N_EXPERTS, lambda e, c: (tail_pieces(e, lambda cp: cp.wait()), c)[1], 0)
        lax.fori_loop(n_used_ref[0], n_alloc, lambda m, c: (unused_tile_copy(m).wait(), c)[1], 0)

        @pl.when(b >= 1)
        def _():
            drain(b - 1, 1 - gen)

        drain(b, gen)


def _dispatch(h2, ri, plan, tm):
    tokens = h2.shape[0]
    r_loc = _local_rows(tm)
    prefetch = [plan["local_start"]] + [plan[k] for k in PIECE_TABLES] + [
        plan["tail_base"], plan["tail_rows"], plan["n_used"]]
    per_step = DISPATCH_TILES_PER_STEP if (tokens // tm) % DISPATCH_TILES_PER_STEP == 0 else 1
    rows = per_step * tm
    grid_spec = pltpu.PrefetchScalarGridSpec(
        num_scalar_prefetch=len(prefetch),
        grid=(tokens // rows,),
        in_specs=[pl.BlockSpec((rows, D_MODEL), lambda i, *_: (i, 0)),
                  pl.BlockSpec((8, rows), lambda i, *_: (0, i))],
        out_specs=pl.BlockSpec(memory_space=pl.ANY),
        scratch_shapes=[pltpu.VMEM((2 * per_step, r_loc, D_MODEL), F32), pltpu.VMEM((TMX, D_MODEL), F32),
                        pltpu.SemaphoreType.DMA((2 * per_step,)), pltpu.SemaphoreType.DMA((2,))],
    )
    return pl.pallas_call(
        functools.partial(_dispatch_kernel, r_loc=r_loc),
        grid_spec=grid_spec,
        out_shape=jax.ShapeDtypeStruct((_sorted_rows_alloc(tokens, tm), D_MODEL), F32),
        compiler_params=pltpu.CompilerParams(
            dimension_semantics=("arbitrary",), vmem_limit_bytes=VMEM_LIMIT),
        name="dispatch",
    )(*prefetch, h2, ri)


def _experts_kernel(tile_expert_ref, n_used_ref, first_ref, slot_ref, next_ref, rows_ref, xs_hbm, wg_hbm, wu_hbm, wd_hbm,
                    ys_hbm, wg_st, wu_st, wd_st, wg_bf, wu_bf, wd_bf, a_ref, u_ref, xin_ref, yout_ref,
                    sem_ref, in_sem_ref, out_sem_ref):
    m = pl.program_id(0)
    n_used = n_used_ref[0]

    def weight_copies(e, s):
        return [pltpu.make_async_copy(src.at[e], dst.at[s], sem_ref.at[s, n])
                for n, (src, dst) in enumerate([(wg_hbm, wg_st), (wu_hbm, wu_st), (wd_hbm, wd_st)])]

    def in_copy(t):
        s = t % IN_SLOTS
        return pltpu.make_async_copy(xs_hbm.at[pl.ds(pl.multiple_of(t * TMX, TMX), TMX), :], xin_ref.at[s],
                                     in_sem_ref.at[s])

    def out_copy(t):
        s = t % OUT_SLOTS
        return pltpu.make_async_copy(yout_ref.at[s], ys_hbm.at[pl.ds(pl.multiple_of(t * TMX, TMX), TMX), :],
                                     out_sem_ref.at[s])

    @pl.when(m == 0)
    def _():
        for t in range(IN_SLOTS - 1):
            pl.when(t < n_used)(in_copy(t).start)

    @pl.when(m < n_used)
    def _():
        @pl.when(m + IN_SLOTS - 1 < n_used)
        def _():
            in_copy(m + IN_SLOTS - 1).start()

        @pl.when(first_ref[m] == 1)
        def _():
            s = slot_ref[m]

            @pl.when(m == 0)
            def _():
                for cp in weight_copies(tile_expert_ref[0], 0):
                    cp.start()

            for cp in weight_copies(tile_expert_ref[m], s):
                cp.wait()

            @pl.when(next_ref[m] >= 0)
            def _():
                for cp in weight_copies(next_ref[m], 1 - s):
                    cp.start()

            wg_bf[...] = wg_st[s].astype(BF16)
            wu_bf[...] = wu_st[s].astype(BF16)
            wd_bf[...] = wd_st[s].astype(BF16)

        xs_ref = xin_ref.at[m % IN_SLOTS]
        ys_ref = yout_ref.at[m % OUT_SLOTS]
        in_copy(m).wait()

        @pl.when(m >= OUT_SLOTS)
        def _():
            out_copy(m - OUT_SLOTS).wait()

        def mlp(rows):
            x = xs_ref[rows, :].astype(BF16)
            a_ref[rows, :] = _dot(x, wg_bf[...])
            u_ref[rows, :] = _dot(x, wu_bf[...])
            hid = (_silu(a_ref[rows, :]) * u_ref[rows, :]).astype(BF16)
            ys_ref[rows, :] = _dot(hid, wd_bf[...])

        half = TMX // 2

        @pl.when(rows_ref[m] > half)
        def _():
            mlp(slice(0, TMX))

        @pl.when(rows_ref[m] <= half)
        def _():
            mlp(slice(0, half))
            ys_ref[half:, :] = jnp.zeros((TMX - half, D_MODEL), F32)

        out_copy(m).start()

        @pl.when(m == n_used - 1)
        def _():
            for back in range(OUT_SLOTS):
                pl.when(m - back >= 0)(out_copy(m - back).wait)


def _experts(xs, plan, wg, wu, wd):
    n_tiles = xs.shape[0] // TMX
    hbm = pl.BlockSpec(memory_space=pl.ANY)
    up_shape, down_shape = (D_MODEL, D_EXPERT), (D_EXPERT, D_MODEL)
    grid_spec = pltpu.PrefetchScalarGridSpec(
        num_scalar_prefetch=6,
        grid=(n_tiles,),
        in_specs=[hbm, hbm, hbm, hbm],
        out_specs=hbm,
        scratch_shapes=[pltpu.VMEM((2,) + up_shape, F32), pltpu.VMEM((2,) + up_shape, F32),
                        pltpu.VMEM((2,) + down_shape, F32),
                        pltpu.VMEM(up_shape, BF16), pltpu.VMEM(up_shape, BF16), pltpu.VMEM(down_shape, BF16),
                        pltpu.VMEM((TMX, D_EXPERT), F32), pltpu.VMEM((TMX, D_EXPERT), F32),
                        pltpu.VMEM((IN_SLOTS, TMX, D_MODEL), F32), pltpu.VMEM((OUT_SLOTS, TMX, D_MODEL), F32),
                        pltpu.SemaphoreType.DMA((2, 3)), pltpu.SemaphoreType.DMA((IN_SLOTS,)),
                        pltpu.SemaphoreType.DMA((OUT_SLOTS,))],
    )
    return pl.pallas_call(
        _experts_kernel,
        grid_spec=grid_spec,
        out_shape=jax.ShapeDtypeStruct(xs.shape, F32),
        input_output_aliases={6: 0},
        compiler_params=pltpu.CompilerParams(
            dimension_semantics=("arbitrary",), vmem_limit_bytes=VMEM_LIMIT),
        name="experts",
    )(plan["tile_expert"], plan["n_used"], plan["tile_first"], plan["tile_slot"], plan["tile_next"], plan["tile_rows"],
      xs, wg, wu, wd)


def _combine_kernel(local_start_ref, n_big_ref, big_src_ref, big_dst_ref, n_small_ref, small_src_ref, small_dst_ref,
                    ys_ref, ri_ref, rw_ref, x1_ref, mod_ref, gain_ref, o_ref, buf_ref, sem_ref,
                    *, r_loc, tiles_per_batch):
    b = pl.program_id(0)
    nb = pl.num_programs(0)
    per_step = buf_ref.shape[0] // 2
    tm = x1_ref.shape[0] // per_step
    gen = b % 2
    fetch_tile, wait_tile = _run_pieces(
        (n_big_ref, big_src_ref, big_dst_ref, n_small_ref, small_src_ref, small_dst_ref), buf_ref, ys_ref, sem_ref,
        to_sorted=False)

    def fetch(step, g):
        for t in range(per_step):
            fetch_tile(step * per_step + t, g * per_step + t)

    @pl.when(b == 0)
    def _():
        buf_ref[...] = jnp.zeros_like(buf_ref)
        fetch(0, 0)

    @pl.when(b + 1 < nb)
    def _():
        fetch(b + 1, 1 - gen)

    for t in range(per_step):
        wait_tile(b * per_step + t, gen * per_step + t)

    tiles = [b * per_step + t for t in range(per_step)]
    cols = [slice(t * tm, (t + 1) * tm) for t in range(per_step)]
    slots = [_local_slots(ri_ref, local_start_ref, tile, c) for tile, c in zip(tiles, cols)]
    rows = lax.broadcasted_iota(jnp.int32, (r_loc, tm), 0)
    hits = [(rows == s1, rows == s2) for s1, s2 in slots]
    w_rows = [jnp.sum(jnp.where(h1, rw_ref[0:1, c], jnp.where(h2, rw_ref[1:2, c], 0.0)), axis=1, keepdims=True)
              for (h1, h2), c in zip(hits, cols)]
    perms = [jnp.where(h1 | h2, 1.0, 0.0).astype(BF16) for h1, h2 in hits]
    yws = [(buf_ref[gen * per_step + t] * w_rows[t]).astype(BF16) for t in range(per_step)]
    moes = [_dot_tn(perm, yw) for perm, yw in zip(perms, yws)]
    batch = b // tiles_per_batch
    gate2 = mod_ref[pl.ds(batch, 1), 5 * D_MODEL:6 * D_MODEL]
    for c, moe in zip(cols, moes):
        x2 = x1_ref[c, :] + gate2 * moe
        ms = jnp.mean(x2 * x2, axis=-1, keepdims=True)
        o_ref[c, :] = x2 * lax.rsqrt(ms + EPS) * gain_ref[...]


def _combine(ys, ri, rw, x1, mod, gain, plan, seq, tm):
    tokens = x1.shape[0]
    r_loc = _local_rows(tm)
    per_step = COMBINE_TILES_PER_STEP if seq % (COMBINE_TILES_PER_STEP * tm) == 0 else 1
    rows = per_step * tm
    row_spec = pl.BlockSpec((8, rows), lambda i, *_: (0, i))
    tok_spec = pl.BlockSpec((rows, D_MODEL), lambda i, *_: (i, 0))
    full = lambda a: pl.BlockSpec(a.shape, lambda i, *_: (0,) * a.ndim)
    prefetch = [plan["local_start"]] + [plan[k] for k in PIECE_TABLES]
    grid_spec = pltpu.PrefetchScalarGridSpec(
        num_scalar_prefetch=len(prefetch),
        grid=(tokens // rows,),
        in_specs=[pl.BlockSpec(memory_space=pl.ANY), row_spec, row_spec, tok_spec, full(mod), full(gain)],
        out_specs=tok_spec,
        scratch_shapes=[pltpu.VMEM((2 * per_step, r_loc, D_MODEL), F32), pltpu.SemaphoreType.DMA((2 * per_step,))],
    )
    return pl.pallas_call(
        functools.partial(_combine_kernel, r_loc=r_loc, tiles_per_batch=seq // rows),
        grid_spec=grid_spec,
        out_shape=jax.ShapeDtypeStruct((tokens, D_MODEL), F32),
        compiler_params=pltpu.CompilerParams(
            dimension_semantics=("arbitrary",), vmem_limit_bytes=VMEM_LIMIT),
        name="combine",
    )(*prefetch, ys, ri, rw, x1, mod, gain)


def _rotary_tables(seq):
    half = RET_HEAD_DIM // 2
    inv_freq = 1.0 / (ROPE_BASE ** (np.arange(half, dtype=np.float64) / half))
    ang = np.arange(seq, dtype=np.float64)[:, None] * inv_freq[None, :]
    cos = np.cos(ang)
    sin = np.sin(ang)
    f32 = lambda a: jnp.asarray(a.astype(np.float32))
    return f32(np.tile(cos, (1, 4))), f32(np.concatenate([-sin, sin, -sin, sin], axis=1))


def _pick_tile(n, pref):
    t = min(n, pref)
    assert n % t == 0, (n, t)
    return t


def kernel(x, c, ada_w, ada_b, norm1_gain, norm2_gain, w_in, w_out, ret_gn_gain, lam_q1, lam_k1, lam_q2,
           lam_k2, diff_subln_gain, w_group, b_group, w_expert, b_expert, w_gate, w_up, w_down, final_gain):
    batch, seq, d = x.shape
    assert d == D_MODEL and batch <= 8 and ada_w.shape[0] == 1
    layer = 0
    lambda_init = 0.8 - 0.6 * math.exp(-0.3 * layer)
    tokens = batch * seq
    x2 = x.reshape(tokens, d)
    tm = _pick_tile(seq, 512)

    c_pad = jnp.zeros((8, d), F32).at[:batch].set(c)
    mod = _adaln(c_pad, ada_w[layer], ada_b[layer].reshape(1, -1))

    cos_t, sin_t = _rotary_tables(seq)
    rq, rk, rv, rg, dq, dk, dvt = _inproj(
        x2, mod, norm1_gain[layer].reshape(1, d), w_in[layer].astype(BF16), cos_t, sin_t, seq, tm)

    ret_out = _retention(rq, rk, rv, rg, ret_gn_gain[layer].reshape(1, RET_WIDTH), batch, seq,
                         _pick_tile(seq, 256))
    diff_out = _diffattn(
        dq, dk, dvt, lam_q1[layer].reshape(1, -1), lam_k1[layer].reshape(1, -1), lam_q2[layer].reshape(1, -1),
        lam_k2[layer].reshape(1, -1), diff_subln_gain[layer].reshape(-1, 1), batch, seq, lambda_init,
        _pick_tile(seq, 2048), 2 * QUERY_CHUNK)

    w_router = jnp.concatenate(
        [w_group[layer].T, jnp.zeros((8 - N_GROUPS, d), F32), w_expert[layer].reshape(d, N_EXPERTS).T], axis=0)
    b_router = jnp.concatenate(
        [b_group[layer], jnp.zeros((8 - N_GROUPS,), F32), b_expert[layer].reshape(N_EXPERTS)]).reshape(-1, 1)
    wr_hi = w_router.astype(BF16)
    wr_lo = (w_router - wr_hi.astype(F32)).astype(BF16)
    x1, h2, ri, rw, cnt = _outproj(ret_out, diff_out, x2, mod, norm2_gain[layer].reshape(1, d),
                                   w_out[layer].astype(BF16), jnp.concatenate([wr_hi, wr_lo], axis=0), b_router,
                                   seq, tm)

    plan = _dispatch_plan(cnt[:, :, 0], tokens, tm)
    xs = _dispatch(h2, ri, plan, tm)
    ys = _experts(xs, plan, w_gate[layer].reshape(N_EXPERTS, d, D_EXPERT),
                  w_up[layer].reshape(N_EXPERTS, d, D_EXPERT), w_down[layer].reshape(N_EXPERTS, D_EXPERT, d))
    out = _combine(ys, ri, rw, x1, mod, final_gain.reshape(1, d), plan, seq, tm)
    return out.reshape(batch, seq, d)
```

```python
import functools
import math

import jax
import jax.numpy as jnp
import numpy as np
from jax import lax
from jax.experimental import pallas as pl
from jax.experimental.pallas import tpu as pltpu

F32 = jnp.float32
BF16 = jnp.bfloat16

D_MODEL = 1024
RET_HEAD_DIM = 64
RET_WIDTH = 512
RET_HEADS = 8
RET_PAIRS = RET_HEADS // 2
DIFF_QK_DIM = 64
DIFF_V_DIM = 128
DIFF_HEADS = 4
DIFF_WIDTH = 512
N_GROUPS = 4
EXPERTS_PER_GROUP = 8
N_EXPERTS = N_GROUPS * EXPERTS_PER_GROUP
D_EXPERT = 512
N_MOD = 6
ROPE_BASE = 10000.0
EPS = 1e-6
LANES = 128
ROUTER_ROWS = 8 + N_EXPERTS
VMEM_LIMIT = 56 * 1024 * 1024


def _dot(a, b):
    return jnp.dot(a, b, preferred_element_type=F32)


def _dot_nt(a, b):
    return lax.dot_general(a, b, (((1,), (1,)), ((), ())), preferred_element_type=F32)


def _dot_tn(a, b):
    return lax.dot_general(a, b, (((0,), (0,)), ((), ())), preferred_element_type=F32)


def _split_bf16(x):
    hi = x.astype(BF16)
    lo = (x - hi.astype(F32)).astype(BF16)
    return hi, lo


def _silu(x):
    return x / (1.0 + jnp.exp(-x))


def _adaln_kernel(c_ref, w_ref, b_ref, o_ref):
    ca = _silu(c_ref[...])
    c_hi, c_lo = _split_bf16(ca)
    w_hi, w_lo = _split_bf16(w_ref[...])
    o_ref[...] = _dot(c_hi, w_hi) + _dot(c_lo, w_hi) + _dot(c_hi, w_lo) + b_ref[...]


def _adaln(c_pad, ada_w, ada_b):
    n_out = ada_w.shape[1]
    tn = D_MODEL
    return pl.pallas_call(
        _adaln_kernel,
        grid=(n_out // tn,),
        in_specs=[
            pl.BlockSpec((8, D_MODEL), lambda j: (0, 0)),
            pl.BlockSpec((D_MODEL, tn), lambda j: (0, j)),
            pl.BlockSpec((1, tn), lambda j: (0, j)),
        ],
        out_specs=pl.BlockSpec((8, tn), lambda j: (0, j)),
        out_shape=jax.ShapeDtypeStruct((8, n_out), F32),
        compiler_params=pltpu.CompilerParams(vmem_limit_bytes=VMEM_LIMIT),
        name="adaln",
    )(c_pad, ada_w, ada_b)


def _norm_modulate(x, gain, shift, scale):
    ms = jnp.mean(x * x, axis=-1, keepdims=True)
    y = x * lax.rsqrt(ms + EPS) * gain
    return y * (1.0 + scale) + shift


def _rotary_slab(x, cos, sin_signed, lane_lo):
    swapped = jnp.where(lane_lo, pltpu.roll(x, 96, 1), pltpu.roll(x, 32, 1))
    return x * cos + swapped * sin_signed


def _inproj_kernel(x_ref, mod_ref, gain_ref, w_ref, cos_ref, sin_ref,
                   rq_ref, rk_ref, rv_ref, rg_ref, dq_ref, dk_ref, dvt_ref, *, tiles_per_batch):
    b = pl.program_id(0) // tiles_per_batch
    shift = mod_ref[pl.ds(b, 1), 0:D_MODEL]
    scale = mod_ref[pl.ds(b, 1), D_MODEL:2 * D_MODEL]
    h = _norm_modulate(x_ref[...], gain_ref[...], shift, scale).astype(BF16)
    cos = cos_ref[...]
    sin = sin_ref[...]
    lane = lax.broadcasted_iota(jnp.int32, cos.shape, 1)
    lane_lo = (lane % 64) < 32

    def proj(chunk):
        return _dot(h, w_ref[:, chunk * RET_WIDTH:(chunk + 1) * RET_WIDTH])

    def rotary(acc, out_ref, post_scale):
        for s in range(RET_WIDTH // LANES):
            sl = slice(s * LANES, (s + 1) * LANES)
            out_ref[:, sl] = (_rotary_slab(acc[:, sl], cos, sin, lane_lo) * post_scale).astype(BF16)

    rotary(proj(0), rq_ref, 1.0)
    rotary(proj(1), rk_ref, RET_HEAD_DIM ** -0.5)
    rv_ref[...] = proj(2).astype(BF16)
    rg_ref[...] = _silu(proj(3)).astype(BF16)
    dq_ref[...] = (proj(4) * (DIFF_QK_DIM ** -0.5 * math.log2(math.e))).astype(BF16)
    dk_ref[...] = proj(5).astype(BF16)
    dvt_ref[...] = proj(6).T.astype(BF16)


def _inproj(x2, mod, gain, w_in, cos_t, sin_t, seq, tm):
    tokens = x2.shape[0]
    tiles_per_batch = seq // tm
    tok_spec = lambda w: pl.BlockSpec((tm, w), lambda i: (i, 0))
    tab_spec = pl.BlockSpec((tm, LANES), lambda i: (i % tiles_per_batch, 0))
    full = lambda a: pl.BlockSpec(a.shape, lambda i: (0,) * a.ndim)
    out = jax.ShapeDtypeStruct((tokens, RET_WIDTH), BF16)
    return pl.pallas_call(
        functools.partial(_inproj_kernel, tiles_per_batch=tiles_per_batch),
        grid=(tokens // tm,),
        in_specs=[tok_spec(D_MODEL), full(mod), full(gain), full(w_in), tab_spec, tab_spec],
        out_specs=[tok_spec(RET_WIDTH)] * 6 + [pl.BlockSpec((DIFF_WIDTH, tm), lambda i: (0, i))],
        out_shape=[out] * 6 + [jax.ShapeDtypeStruct((DIFF_WIDTH, tokens), BF16)],
        compiler_params=pltpu.CompilerParams(
            dimension_semantics=("parallel",), vmem_limit_bytes=VMEM_LIMIT),
        name="inproj",
    )(x2, mod, gain, w_in, cos_t, sin_t)


RET_SEQ_GROUP = 4


def _retention_kernel(q_ref, k_ref, v_ref, g_ref, dec_ref, qdec_ref, kdec_ref, rdec_ref,
                      bmask_ref, gmean_ref, gain_ref, o_ref, state_ref, *, chunk):
    @pl.when(pl.program_id(1) == 0)
    def _():
        state_ref[...] = jnp.zeros_like(state_ref)

    lane = lax.broadcasted_iota(jnp.int32, (chunk, LANES), 1)
    first_head = lane < RET_HEAD_DIM
    gmean = gmean_ref[...]
    bmask = bmask_ref[...]
    units = [(s, p) for s in range(q_ref.shape[0]) for p in range(RET_PAIRS)]
    sl = lambda p: slice(p * LANES, (p + 1) * LANES)
    q = [q_ref[s, :, sl(p)] for s, p in units]
    k = [k_ref[s, :, sl(p)] for s, p in units]
    v = [v_ref[s, :, sl(p)] for s, p in units]
    zero = jnp.zeros_like(q[0])
    q_stack = [jnp.concatenate([jnp.where(first_head, qu, zero), jnp.where(first_head, zero, qu)], axis=0)
               for qu in q]
    scores = [(_dot_nt(q_stack[n], k[n]) * dec_ref[p]).astype(BF16) for n, (_, p) in enumerate(units)]
    state = [state_ref[n] for n in range(len(units))]
    cross = [_dot(q[n], state[n].astype(BF16)) * qdec_ref[:, sl(p)] for n, (_, p) in enumerate(units)]
    k_dec = [(k[n].astype(F32) * kdec_ref[:, sl(p)]).astype(BF16) for n, (_, p) in enumerate(units)]
    for n, (_, p) in enumerate(units):
        state_ref[n] = state[n] * rdec_ref[p] + _dot_tn(k_dec[n], v[n]) * bmask
    intra2 = [_dot(scores[n], v[n]) for n in range(len(units))]
    y = [jnp.where(first_head, intra2[n][:chunk], intra2[n][chunk:]) + cross[n] for n in range(len(units))]
    seg_mean = lambda x: _dot(jnp.concatenate(_split_bf16(x), axis=1), gmean)
    mu = [seg_mean(yu) for yu in y]
    d = [yu - mu_u for yu, mu_u in zip(y, mu)]
    var = [seg_mean(du * du) for du in d]
    for n, (s, p) in enumerate(units):
        yn = d[n] * lax.rsqrt(var[n] + EPS) * gain_ref[:, sl(p)]
        o_ref[s, :, sl(p)] = (g_ref[s, :, sl(p)].astype(F32) * yn).astype(BF16)


def _retention_tables(chunk):
    heads = np.arange(RET_HEADS, dtype=np.float64)
    log_gamma = np.log(1.0 - np.exp2(-5.0 - heads))
    idx = np.arange(chunk)
    rel = (idx[:, None] - idx[None, :]).astype(np.float64)
    decay = np.where(rel[None] >= 0, np.exp(log_gamma[:, None, None] * np.maximum(rel, 0.0)[None]), 0.0)
    dec2 = decay.reshape(RET_PAIRS, 2 * chunk, chunk)
    lane_lg = np.repeat(log_gamma, RET_HEAD_DIM)
    qdec = np.exp(lane_lg[None, :] * (idx + 1)[:, None])
    kdec = np.exp(lane_lg[None, :] * (chunk - 1 - idx)[:, None])
    rdec = np.exp(lane_lg * chunk).reshape(RET_PAIRS, LANES, 1) * np.ones((1, 1, LANES))
    blk = np.arange(LANES) // RET_HEAD_DIM
    bmask = (blk[:, None] == blk[None, :]).astype(np.float64)
    f32 = lambda a: jnp.asarray(a.astype(np.float32))
    gmean2 = np.concatenate([bmask, bmask], axis=0) / RET_HEAD_DIM
    return f32(dec2), f32(qdec), f32(kdec), f32(rdec), f32(bmask), f32(gmean2).astype(BF16)


def _retention(rq, rk, rv, rg, gn_gain, batch, seq, chunk):
    nc = seq // chunk
    group = RET_SEQ_GROUP if batch % RET_SEQ_GROUP == 0 else 1
    dec2, qdec, kdec, rdec, bmask, gmean = _retention_tables(chunk)
    tok_spec = pl.BlockSpec((group, chunk, RET_WIDTH), lambda b, n: (b, n, 0))
    full = lambda a: pl.BlockSpec(a.shape, lambda b, n: (0,) * a.ndim)
    by_seq = lambda a: a.reshape(batch, seq, RET_WIDTH)
    out = pl.pallas_call(
        functools.partial(_retention_kernel, chunk=chunk),
        grid=(batch // group, nc),
        in_specs=[tok_spec] * 4 + [full(dec2), full(qdec), full(kdec), full(rdec), full(bmask),
                                   full(gmean), full(gn_gain)],
        out_specs=tok_spec,
        out_shape=jax.ShapeDtypeStruct((batch, seq, RET_WIDTH), BF16),
        scratch_shapes=[pltpu.VMEM((group * RET_PAIRS, LANES, LANES), F32)],
        compiler_params=pltpu.CompilerParams(
            dimension_semantics=("parallel", "arbitrary"), vmem_limit_bytes=VMEM_LIMIT),
        name="retention",
    )(by_seq(rq), by_seq(rk), by_seq(rv), by_seq(rg), dec2, qdec, kdec, rdec, bmask, gmean, gn_gain)
    return out.reshape(batch * seq, RET_WIDTH)


NEG_BIG = -1e30


V_EXT_ROWS = DIFF_V_DIM + 16
QUERY_CHUNK = 256
SCORES_AHEAD_FULL = 3
SCORES_AHEAD_DIAG = 3
STAGE_SLOTS = 16


def _diag_chunks(tq, tk, d):
    assert tk == 2 * QUERY_CHUNK
    per_softmax = tq // QUERY_CHUNK
    out = []
    for c in range(2 * per_softmax):
        q0 = (c % per_softmax) * QUERY_CHUNK
        if q0 + QUERY_CHUNK - 1 < d * tk:
            continue
        kind = "full" if q0 >= (d + 1) * tk else ("tri" if q0 == d * tk else "low_tri")
        out.append((c, kind))
    return out


def _accumulate(acc_ref, cs, alpha, pv):
    acc_ref[:, cs] = alpha * acc_ref[:, cs] + pv


def _diffattn_kernel(q_ref, k_ref, vt_ref, lq1_ref, lk1_ref, lq2_ref, lk2_ref, gain_ref, bias_ref, o_ref,
                     qs_ref, vext_ref, m_ref, acc_ref, stage_ref, *, tq, tk, lambda_init):
    i = pl.program_id(2)
    nk = vext_ref.shape[0]

    @pl.when(i == 0)
    def _():
        for j in range(nk):
            vext_ref[j, 0:DIFF_V_DIM, :] = vt_ref[:, j * tk:(j + 1) * tk]
            vext_ref[j, DIFF_V_DIM:V_EXT_ROWS, :] = jnp.ones((V_EXT_ROWS - DIFF_V_DIM, tk), BF16)

    q = q_ref[...]
    lane = lax.broadcasted_iota(jnp.int32, q.shape, 1)
    zero = jnp.zeros_like(q)
    qs_ref[0:tq, :] = jnp.where(lane < DIFF_QK_DIM, q, zero)
    qs_ref[tq:2 * tq, :] = jnp.where(lane < DIFF_QK_DIM, zero, q)
    m_ref[...] = jnp.full_like(m_ref, NEG_BIG)
    acc_ref[...] = jnp.zeros_like(acc_ref)

    def step(work, n_ahead):
        chunk = lambda c: slice(c * QUERY_CHUNK, (c + 1) * QUERY_CHUNK)

        def scores(j, c, kind):
            n_keys = QUERY_CHUNK if kind == "tri" else tk
            start = pl.multiple_of(j * tk, tk)
            return _dot_nt(k_ref[pl.ds(start, n_keys), :], qs_ref[chunk(c), :])

        ahead = [scores(*work[n]) for n in range(min(n_ahead, len(work)))]
        pending = None
        for n, (j, c, kind) in enumerate(work):
            cs = chunk(c)
            st = ahead.pop(0)
            if n + n_ahead < len(work):
                ahead.append(scores(*work[n + n_ahead]))
            slot = n % stage_ref.shape[0]
            n_keys = st.shape[0]
            stage_ref[slot, 0:n_keys, :] = st
            if kind == "full":
                st = stage_ref[slot]
            else:
                causal = stage_ref[slot, n_keys - QUERY_CHUNK:n_keys, :] + bias_ref[...]
                st = causal if kind == "tri" else jnp.concatenate(
                    [stage_ref[slot, 0:n_keys - QUERY_CHUNK, :], causal], axis=0)
            m_old = m_ref[:, cs]
            m_new = jnp.maximum(m_old, jnp.max(st, axis=0, keepdims=True))
            alpha = jnp.exp2(m_old - m_new)
            p = jnp.exp2(st - m_new).astype(BF16)
            m_ref[:, cs] = m_new
            pv = _dot(vext_ref[j, :, 0:st.shape[0]], p)
            if pending is not None:
                pending()
            pending = functools.partial(_accumulate, acc_ref, cs, alpha, pv)
        pending()

    tiles_per_q = tq // tk
    n_chunks = 2 * tq // QUERY_CHUNK

    def full_tiles(it):
        return [(it * tiles_per_q + d, c, "full") for d in range(tiles_per_q) for c in range(n_chunks)]

    lax.fori_loop(0, i, lambda it, c: (step(full_tiles(it), SCORES_AHEAD_FULL), c)[1], 0)
    kind_order = {"full": 0, "low_tri": 1, "tri": 2}
    diag = lambda it: [(it * tiles_per_q + d, c, kind) for d in range(tiles_per_q)
                       for c, kind in sorted(_diag_chunks(tq, tk, d), key=lambda ck: kind_order[ck[1]])]
    lax.fori_loop(i, i + 1, lambda it, c: (step(diag(it), SCORES_AHEAD_DIAG), c)[1], 0)

    lam = (jnp.exp(jnp.sum(lq1_ref[...] * lk1_ref[...], axis=-1, keepdims=True))
           - jnp.exp(jnp.sum(lq2_ref[...] * lk2_ref[...], axis=-1, keepdims=True)) + lambda_init)
    acc = acc_ref[...]
    o2 = acc[0:DIFF_V_DIM, :] * (1.0 / acc[DIFF_V_DIM:DIFF_V_DIM + 1, :])
    ot = o2[:, :tq] - lam * o2[:, tq:]
    ms = jnp.mean(ot * ot, axis=0, keepdims=True)
    ot = ot * lax.rsqrt(ms + EPS) * gain_ref[...] * (1.0 - lambda_init)
    o_ref[...] = ot.T.astype(BF16)


def _diffattn(dq, dk, dvt, lam_q1, lam_k1, lam_q2, lam_k2, gain, batch, seq, lambda_init, tq, tk):
    nq = seq // tq
    q_spec = pl.BlockSpec((tq, LANES), lambda b, h, i: (b * nq + i, h))
    k_spec = pl.BlockSpec((seq, LANES), lambda b, h, i: (b, h))
    vt_spec = pl.BlockSpec((DIFF_V_DIM, seq), lambda b, h, i: (h, b))
    vec = lambda a: pl.BlockSpec(a.shape, lambda b, h, i: (0, 0))
    key = np.arange(QUERY_CHUNK)[:, None]
    query = np.arange(QUERY_CHUNK)[None, :]
    bias = jnp.asarray(np.where(key <= query, 0.0, NEG_BIG), F32)
    return pl.pallas_call(
        functools.partial(_diffattn_kernel, tq=tq, tk=tk, lambda_init=lambda_init),
        grid=(batch, DIFF_HEADS, nq),
        in_specs=[q_spec, k_spec, vt_spec, vec(lam_q1), vec(lam_k1), vec(lam_q2), vec(lam_k2), vec(gain),
                  vec(bias)],
        out_specs=q_spec,
        out_shape=jax.ShapeDtypeStruct(dq.shape, BF16),
        scratch_shapes=[
            pltpu.VMEM((2 * tq, LANES), BF16),
            pltpu.VMEM((seq // tk, V_EXT_ROWS, tk), BF16),
            pltpu.VMEM((1, 2 * tq), F32),
            pltpu.VMEM((V_EXT_ROWS, 2 * tq), F32),
            pltpu.VMEM((STAGE_SLOTS, tk, QUERY_CHUNK), F32),
        ],
        compiler_params=pltpu.CompilerParams(
            dimension_semantics=("parallel", "parallel", "arbitrary"), vmem_limit_bytes=VMEM_LIMIT),
        name="diffattn",
    )(dq, dk, dvt, lam_q1, lam_k1, lam_q2, lam_k2, gain, bias)


def _route(logits):
    r = [logits[g:g + 1, :] for g in range(N_GROUPS)]
    gmax = jnp.maximum(jnp.maximum(r[0], r[1]), jnp.maximum(r[2], r[3]))
    g_idx = jnp.where(r[0] == gmax, 0, jnp.where(r[1] == gmax, 1, jnp.where(r[2] == gmax, 2, 3)))
    denom = sum(jnp.exp(rg - gmax) for rg in r)
    g_weight = 1.0 / denom
    sel = jnp.zeros((EXPERTS_PER_GROUP, logits.shape[1]), F32)
    for g in range(N_GROUPS):
        rows = logits[8 + g * EXPERTS_PER_GROUP:8 + (g + 1) * EXPERTS_PER_GROUP, :]
        sel = jnp.where(g_idx == g, rows, sel)
    eidx = lax.broadcasted_iota(jnp.int32, sel.shape, 0)
    v1 = jnp.max(sel, axis=0, keepdims=True)
    i1 = jnp.min(jnp.where(sel == v1, eidx, EXPERTS_PER_GROUP), axis=0, keepdims=True)
    sel2 = jnp.where(eidx == i1, -jnp.inf, sel)
    v2 = jnp.max(sel2, axis=0, keepdims=True)
    i2 = jnp.min(jnp.where(sel2 == v2, eidx, EXPERTS_PER_GROUP), axis=0, keepdims=True)
    e2 = jnp.exp(v2 - v1)
    w1 = g_weight / (1.0 + e2)
    w2 = g_weight * e2 / (1.0 + e2)
    return g_idx, i1, i2, w1, w2


OUTPROJ_PARTS = 2
OUTPROJ_TILES_PER_STEP = 2


def _outproj_kernel(ret_ref, diff_ref, x_ref, mod_ref, gain_ref, wo_ref, wr_ref, br_ref, tri_ref,
                    x1_ref, h2_ref, ri_ref, rw_ref, cnt_ref, *, tiles_per_batch):
    b = pl.program_id(0) // tiles_per_batch
    gate1 = mod_ref[pl.ds(b, 1), 2 * D_MODEL:3 * D_MODEL]
    shift = mod_ref[pl.ds(b, 1), 3 * D_MODEL:4 * D_MODEL]
    scale = mod_ref[pl.ds(b, 1), 4 * D_MODEL:5 * D_MODEL]
    wr = wr_ref[...]
    rows = x_ref.shape[0]
    tm = tri_ref.shape[0]
    n_parts = OUTPROJ_PARTS * rows // tm
    parts = [slice(n * rows // n_parts, (n + 1) * rows // n_parts) for n in range(n_parts)]
    mix = [_dot(jnp.concatenate([ret_ref[r, :], diff_ref[r, :]], axis=1), wo_ref[...]) for r in parts]
    for r, m in zip(parts, mix):
        x1_ref[r, :] = x_ref[r, :] + gate1 * m
    h_split = [_split_bf16(_norm_modulate(x1_ref[r, :], gain_ref[...], shift, scale)) for r in parts]
    for r, (h_hi, _) in zip(parts, h_split):
        h2_ref[r, :] = h_hi
    by_hi = [_dot_nt(wr, h_hi) for h_hi, _ in h_split]
    by_lo = [_dot_nt(wr[:ROUTER_ROWS], h_lo) for _, h_lo in h_split]
    logits = [a[:ROUTER_ROWS] + a[ROUTER_ROWS:] + c + br_ref[...] for a, c in zip(by_hi, by_lo)]
    routed = [_route(lg) for lg in logits]
    g_idx, i1, i2, w1, w2 = [jnp.concatenate([rt[n] for rt in routed], axis=1) for n in range(5)]
    logits = jnp.concatenate(logits, axis=1)
    e1 = g_idx * EXPERTS_PER_GROUP + i1
    e2 = g_idx * EXPERTS_PER_GROUP + i2
    eidx = lax.broadcasted_iota(jnp.int32, (N_EXPERTS, logits.shape[1]), 0)
    hit1 = eidx == e1
    hit2 = eidx == e2
    onehot = jnp.where(hit1 | hit2, 1.0, 0.0)
    tiles = [slice(n * tm, (n + 1) * tm) for n in range(rows // tm)]
    before = jnp.concatenate([_dot(onehot[:, t].astype(BF16), tri_ref[...]) for t in tiles], axis=1)
    r1 = jnp.sum(jnp.where(hit1, before, 0.0), axis=0, keepdims=True)
    r2 = jnp.sum(jnp.where(hit2, before, 0.0), axis=0, keepdims=True)
    zi = jnp.zeros_like(e1)
    ri_ref[...] = jnp.concatenate([e1, e2, r1.astype(jnp.int32), r2.astype(jnp.int32), zi, zi, zi, zi], axis=0)
    zf = jnp.zeros_like(w1)
    rw_ref[...] = jnp.concatenate([w1, w2, zf, zf, zf, zf, zf, zf], axis=0)
    for n, t in enumerate(tiles):
        counts = jnp.sum(onehot[:, t], axis=1, keepdims=True)
        cnt_ref[n] = jnp.broadcast_to(counts, (N_EXPERTS, LANES)).astype(jnp.int32)


def _outproj(ret_out, diff_out, x2, mod, gain, w_out, wr, br, seq, tm):
    tokens = x2.shape[0]
    n_tiles = tokens // tm
    per_step = OUTPROJ_TILES_PER_STEP if seq % (OUTPROJ_TILES_PER_STEP * tm) == 0 else 1
    rows = per_step * tm
    tri = jnp.asarray(np.arange(tm)[:, None] < np.arange(tm)[None, :], BF16)
    tok_spec = lambda w: pl.BlockSpec((rows, w), lambda i: (i, 0))
    row_spec = pl.BlockSpec((8, rows), lambda i: (0, i))
    full = lambda a: pl.BlockSpec(a.shape, lambda i: (0,) * a.ndim)
    return pl.pallas_call(
        functools.partial(_outproj_kernel, tiles_per_batch=seq // rows),
        grid=(tokens // rows,),
        in_specs=[tok_spec(RET_WIDTH), tok_spec(DIFF_WIDTH), tok_spec(D_MODEL), full(mod), full(gain),
                  full(w_out), full(wr), full(br), full(tri)],
        out_specs=[tok_spec(D_MODEL), tok_spec(D_MODEL), row_spec, row_spec,
                   pl.BlockSpec((per_step, N_EXPERTS, LANES), lambda i: (i, 0, 0))],
        out_shape=[jax.ShapeDtypeStruct((tokens, D_MODEL), F32),
                   jax.ShapeDtypeStruct((tokens, D_MODEL), BF16),
                   jax.ShapeDtypeStruct((8, tokens), jnp.int32),
                   jax.ShapeDtypeStruct((8, tokens), F32),
                   jax.ShapeDtypeStruct((n_tiles, N_EXPERTS, LANES), jnp.int32)],
        compiler_params=pltpu.CompilerParams(
            dimension_semantics=("parallel",), vmem_limit_bytes=VMEM_LIMIT),
        name="outproj",
    )(ret_out, diff_out, x2, mod, gain, w_out, wr, br, tri)


SORTED = BF16
CHUNK = 16
BIG_PIECE = 2 * CHUNK
MAX_SMALL_PIECES = N_EXPERTS * (BIG_PIECE // CHUNK - 1)
PIECE_TABLES = ("n_big", "big_src", "big_dst", "n_small", "small_src", "small_dst")
DISPATCH_TILES_PER_STEP = 2
COMBINE_TILES_PER_STEP = 2
TMX = 512
IN_SLOTS = 3
OUT_SLOTS = 2


def _local_rows(tm):
    rows = 2 * tm + N_EXPERTS * (CHUNK - 1)
    return (rows + 15) // 16 * 16


def _max_big_pieces(tm):
    return _local_rows(tm) // BIG_PIECE


def _sorted_rows_alloc(tokens, tm):
    worst = 2 * tokens + (tokens // tm) * N_EXPERTS * (CHUNK - 1) + N_EXPERTS * (TMX - CHUNK)
    return (worst + TMX - 1) // TMX * TMX


def _dispatch_plan(cnt, tokens, tm):
    i32 = jnp.int32
    pad = (cnt + CHUNK - 1) // CHUNK * CHUNK
    local_end = jnp.cumsum(pad, axis=1)
    local_start = local_end - pad
    seg_rows = jnp.sum(pad, axis=0)
    seg_pad = (seg_rows + TMX - 1) // TMX * TMX
    seg_end = jnp.cumsum(seg_pad)
    seg_start = seg_end - seg_pad
    run_dst = seg_start[None, :] + jnp.cumsum(pad, axis=0) - pad

    def pieces(count, offset, size, max_n):
        end = jnp.cumsum(count, axis=1)
        start = end - count
        k = jnp.arange(max_n, dtype=i32)[None, :, None]
        owns = (start[:, None, :] <= k) & (k < end[:, None, :])
        within = size * (k - start[:, None, :]) + offset[:, None, :]
        src = jnp.sum(jnp.where(owns, local_start[:, None, :] + within, 0), axis=-1)
        dst = jnp.sum(jnp.where(owns, run_dst[:, None, :] + within, 0), axis=-1)
        return end[:, -1].astype(i32), src.reshape(-1).astype(i32), dst.reshape(-1).astype(i32)

    n_big, big_src, big_dst = pieces(pad // BIG_PIECE, jnp.zeros_like(pad), BIG_PIECE, _max_big_pieces(tm))
    n_small, small_src, small_dst = pieces(pad % BIG_PIECE // CHUNK, pad // BIG_PIECE * BIG_PIECE, CHUNK,
                                           MAX_SMALL_PIECES)
    m = TMX * jnp.arange(_sorted_rows_alloc(tokens, tm) // TMX, dtype=i32)
    tile_expert = jnp.minimum(jnp.sum(seg_end[None, :] <= m[:, None], axis=-1), N_EXPERTS - 1)
    towns = (seg_start[None, :] <= m[:, None]) & (m[:, None] < seg_end[None, :])
    used = seg_pad > 0
    parity = (jnp.cumsum(used) - used) % 2
    eids = jnp.arange(N_EXPERTS, dtype=i32)
    later_used = (eids[None, :] > eids[:, None]) & used[None, :]
    next_used = jnp.min(jnp.where(later_used, eids[None, :], N_EXPERTS), axis=1)
    next_used = jnp.where(next_used == N_EXPERTS, -1, next_used)
    pick = lambda per_expert: jnp.sum(jnp.where(towns, per_expert[None, :], 0), axis=-1)
    tile_first = jnp.sum(jnp.where(towns & (seg_start[None, :] == m[:, None]), 1, 0), axis=-1)
    tile_next = jnp.where(jnp.any(towns, axis=-1), pick(next_used), -1)
    return dict(
        tile_first=tile_first.astype(i32),
        tile_slot=pick(parity).astype(i32),
        tile_next=tile_next.astype(i32),
        tile_rows=jnp.clip(pick(seg_start + seg_rows) - m, 0, TMX).astype(i32),
        local_start=local_start.reshape(-1).astype(i32),
        n_big=n_big, big_src=big_src, big_dst=big_dst,
        n_small=n_small, small_src=small_src, small_dst=small_dst,
        tail_base=(seg_start + seg_rows).astype(i32),
        tail_rows=(seg_pad - seg_rows).astype(i32),
        tile_expert=tile_expert.astype(i32),
        n_used=(seg_end[-1:] // TMX).astype(i32),
    )


WAIT_UNROLL = 8


def _wait_times(copy, n):
    lax.fori_loop(0, n // WAIT_UNROLL, lambda i, c: ([copy.wait() for _ in range(WAIT_UNROLL)], c)[1], 0)
    lax.fori_loop(0, n % WAIT_UNROLL, lambda i, c: (copy.wait(), c)[1], 0)


def _for_each(n, body, unroll=4):
    main = n // unroll
    lax.fori_loop(0, main, lambda i, c: ([body(i * unroll + u) for u in range(unroll)], c)[1], 0)
    lax.fori_loop(main * unroll, n, lambda j, c: (body(j), c)[1], 0)


def _local_slots(ri_ref, local_start_ref, tile, cols=slice(None)):
    e1, e2 = ri_ref[0:1, cols], ri_ref[1:2, cols]
    s1, s2 = ri_ref[2:3, cols], ri_ref[3:4, cols]
    for e in range(N_EXPERTS):
        start = local_start_ref[tile * N_EXPERTS + e]
        s1 = s1 + jnp.where(e1 == e, start, 0)
        s2 = s2 + jnp.where(e2 == e, start, 0)
    return s1, s2


def _run_pieces(piece_refs, local_ref, sorted_ref, sem_ref, to_sorted):
    n_big_ref, big_src_ref, big_dst_ref, n_small_ref, small_src_ref, small_dst_ref = piece_refs
    r_loc = local_ref.shape[1]
    kinds = [(BIG_PIECE, n_big_ref, big_src_ref, big_dst_ref, r_loc // BIG_PIECE),
             (CHUNK, n_small_ref, small_src_ref, small_dst_ref, MAX_SMALL_PIECES)]

    def copy(sl, size, local_row, sorted_row):
        local = local_ref.at[sl, pl.ds(pl.multiple_of(local_row, CHUNK), size), :]
        srt = sorted_ref.at[pl.ds(pl.multiple_of(sorted_row, CHUNK), size), :]
        return pltpu.make_async_copy(local, srt, sem_ref.at[sl]) if to_sorted else \
            pltpu.make_async_copy(srt, local, sem_ref.at[sl])

    def start(tile, sl):
        for size, n_ref, src_ref, dst_ref, max_n in kinds:
            _for_each(n_ref[tile], lambda k: copy(sl, size, src_ref[tile * max_n + k], dst_ref[tile * max_n + k])
                      .start())

    def wait(tile, sl):
        for size, n_ref, _, _, _ in kinds:
            _wait_times(copy(sl, size, 0, 0), n_ref[tile])

    return start, wait


def _dispatch_kernel(local_start_ref, n_big_ref, big_src_ref, big_dst_ref, n_small_ref, small_src_ref, small_dst_ref,
                     tail_base_ref, tail_rows_ref, n_used_ref,
                     h_ref, ri_ref, xs_ref, buf_ref, zero_ref, sem_ref, tail_sem_ref, *, r_loc):
    b = pl.program_id(0)
    nb = pl.num_programs(0)
    per_step = buf_ref.shape[0] // 2
    tm = h_ref.shape[0] // per_step
    gen = b % 2
    start_runs, drain_tile = _run_pieces(
        (n_big_ref, big_src_ref, big_dst_ref, n_small_ref, small_src_ref, small_dst_ref), buf_ref, xs_ref, sem_ref,
        to_sorted=True)

    def drain(step, g):
        for t in range(per_step):
            drain_tile(step * per_step + t, g * per_step + t)

    @pl.when(b >= 2)
    def _():
        drain(b - 2, gen)

    tiles = [b * per_step + t for t in range(per_step)]
    cols = [slice(t * tm, (t + 1) * tm) for t in range(per_step)]
    slots = [_local_slots(ri_ref, local_start_ref, tile, c) for tile, c in zip(tiles, cols)]
    rows = lax.broadcasted_iota(jnp.int32, (r_loc, tm), 0)
    perms = [jnp.where((rows == s1) | (rows == s2), 1.0, 0.0).astype(BF16) for s1, s2 in slots]
    for t, (perm, c) in enumerate(zip(perms, cols)):
        buf_ref[gen * per_step + t] = _dot(perm, h_ref[c, :]).astype(SORTED)
    for t, tile in enumerate(tiles):
        start_runs(tile, gen * per_step + t)

    def tail_pieces(e, act):
        n = tail_rows_ref[e]
        size = TMX // 2
        while size >= CHUNK:
            dst = pl.multiple_of(tail_base_ref[e] + (n & (-2 * size)), CHUNK)
            cp = pltpu.make_async_copy(zero_ref.at[pl.ds(0, size), :], xs_ref.at[pl.ds(dst, size), :],
                                       tail_sem_ref.at[0])
            pl.when((n & size) != 0)(functools.partial(act, cp))
            size //= 2

    def unused_tile_copy(m):
        dst = pl.multiple_of(m * TMX, TMX)
        return pltpu.make_async_copy(zero_ref, xs_ref.at[pl.ds(dst, TMX), :], tail_sem_ref.at[1])

    n_alloc = xs_ref.shape[0] // TMX

    @pl.when(b == 0)
    def _():
        zero_ref[...] = jnp.zeros_like(zero_ref)

    experts_per_step = pl.cdiv(N_EXPERTS, nb)
    lax.fori_loop(jnp.minimum(b * experts_per_step, N_EXPERTS), jnp.minimum((b + 1) * experts_per_step, N_EXPERTS),
                  lambda e, c: (tail_pieces(e, lambda cp: cp.start()), c)[1], 0)
    tiles_per_step = pl.cdiv(n_alloc, nb)
    first_unused = n_used_ref[0]
    lax.fori_loop(jnp.minimum(first_unused + b * tiles_per_step, n_alloc),
                  jnp.minimum(first_unused + (b + 1) * tiles_per_step, n_alloc),
                  lambda m, c: (unused_tile_copy(m).start(), c)[1], 0)

    @pl.when(b == nb - 1)
    def _():
        lax.fori_loop(0, N_EXPERTS, lambda e, c: (tail_pieces(e, lambda cp: cp.wait()), c)[1], 0)
        lax.fori_loop(n_used_ref[0], n_alloc, lambda m, c: (unused_tile_copy(m).wait(), c)[1], 0)

        @pl.when(b >= 1)
        def _():
            drain(b - 1, 1 - gen)

        drain(b, gen)


def _dispatch(h2, ri, plan, tm):
    tokens = h2.shape[0]
    r_loc = _local_rows(tm)
    prefetch = [plan["local_start"]] + [plan[k] for k in PIECE_TABLES] + [
        plan["tail_base"], plan["tail_rows"], plan["n_used"]]
    per_step = DISPATCH_TILES_PER_STEP if (tokens // tm) % DISPATCH_TILES_PER_STEP == 0 else 1
    rows = per_step * tm
    grid_spec = pltpu.PrefetchScalarGridSpec(
        num_scalar_prefetch=len(prefetch),
        grid=(tokens // rows,),
        in_specs=[pl.BlockSpec((rows, D_MODEL), lambda i, *_: (i, 0)),
                  pl.BlockSpec((8, rows), lambda i, *_: (0, i))],
        out_specs=pl.BlockSpec(memory_space=pl.ANY),
        scratch_shapes=[pltpu.VMEM((2 * per_step, r_loc, D_MODEL), SORTED), pltpu.VMEM((TMX, D_MODEL), SORTED),
                        pltpu.SemaphoreType.DMA((2 * per_step,)), pltpu.SemaphoreType.DMA((2,))],
    )
    return pl.pallas_call(
        functools.partial(_dispatch_kernel, r_loc=r_loc),
        grid_spec=grid_spec,
        out_shape=jax.ShapeDtypeStruct((_sorted_rows_alloc(tokens, tm), D_MODEL), SORTED),
        compiler_params=pltpu.CompilerParams(
            dimension_semantics=("arbitrary",), vmem_limit_bytes=VMEM_LIMIT),
        name="dispatch",
    )(*prefetch, h2, ri)


def _experts_kernel(tile_expert_ref, n_used_ref, first_ref, slot_ref, next_ref, rows_ref, xs_hbm, wg_hbm, wu_hbm, wd_hbm,
                    ys_hbm, wg_st, wu_st, wd_st, wg_bf, wu_bf, wd_bf, a_ref, u_ref, xin_ref, yout_ref,
                    sem_ref, in_sem_ref, out_sem_ref):
    m = pl.program_id(0)
    n_used = n_used_ref[0]

    def weight_copies(e, s):
        return [pltpu.make_async_copy(src.at[e], dst.at[s], sem_ref.at[s, n])
                for n, (src, dst) in enumerate([(wg_hbm, wg_st), (wu_hbm, wu_st), (wd_hbm, wd_st)])]

    def in_copy(t):
        s = t % IN_SLOTS
        return pltpu.make_async_copy(xs_hbm.at[pl.ds(pl.multiple_of(t * TMX, TMX), TMX), :], xin_ref.at[s],
                                     in_sem_ref.at[s])

    def out_copy(t):
        s = t % OUT_SLOTS
        return pltpu.make_async_copy(yout_ref.at[s], ys_hbm.at[pl.ds(pl.multiple_of(t * TMX, TMX), TMX), :],
                                     out_sem_ref.at[s])

    @pl.when(m == 0)
    def _():
        for t in range(IN_SLOTS - 1):
            pl.when(t < n_used)(in_copy(t).start)

    @pl.when(m < n_used)
    def _():
        @pl.when(m + IN_SLOTS - 1 < n_used)
        def _():
            in_copy(m + IN_SLOTS - 1).start()

        @pl.when(first_ref[m] == 1)
        def _():
            s = slot_ref[m]

            @pl.when(m == 0)
            def _():
                for cp in weight_copies(tile_expert_ref[0], 0):
                    cp.start()

            for cp in weight_copies(tile_expert_ref[m], s):
                cp.wait()

            @pl.when(next_ref[m] >= 0)
            def _():
                for cp in weight_copies(next_ref[m], 1 - s):
                    cp.start()

            wg_bf[...] = wg_st[s].astype(BF16)
            wu_bf[...] = wu_st[s].astype(BF16)
            wd_bf[...] = wd_st[s].astype(BF16)

        xs_ref = xin_ref.at[m % IN_SLOTS]
        ys_ref = yout_ref.at[m % OUT_SLOTS]
        in_copy(m).wait()

        @pl.when(m >= OUT_SLOTS)
        def _():
            out_copy(m - OUT_SLOTS).wait()

        def mlp(rows):
            x = xs_ref[rows, :].astype(BF16)
            a_ref[rows, :] = _dot(x, wg_bf[...])
            u_ref[rows, :] = _dot(x, wu_bf[...])
            hid = (_silu(a_ref[rows, :]) * u_ref[rows, :]).astype(BF16)
            ys_ref[rows, :] = _dot(hid, wd_bf[...]).astype(SORTED)

        half = TMX // 2

        @pl.when(rows_ref[m] > half)
        def _():
            mlp(slice(0, TMX))

        @pl.when(rows_ref[m] <= half)
        def _():
            mlp(slice(0, half))
            ys_ref[half:, :] = jnp.zeros((TMX - half, D_MODEL), SORTED)

        out_copy(m).start()

        @pl.when(m == n_used - 1)
        def _():
            for back in range(OUT_SLOTS):
                pl.when(m - back >= 0)(out_copy(m - back).wait)


def _experts(xs, plan, wg, wu, wd):
    n_tiles = xs.shape[0] // TMX
    hbm = pl.BlockSpec(memory_space=pl.ANY)
    up_shape, down_shape = (D_MODEL, D_EXPERT), (D_EXPERT, D_MODEL)
    grid_spec = pltpu.PrefetchScalarGridSpec(
        num_scalar_prefetch=6,
        grid=(n_tiles,),
        in_specs=[hbm, hbm, hbm, hbm],
        out_specs=hbm,
        scratch_shapes=[pltpu.VMEM((2,) + up_shape, F32), pltpu.VMEM((2,) + up_shape, F32),
                        pltpu.VMEM((2,) + down_shape, F32),
                        pltpu.VMEM(up_shape, BF16), pltpu.VMEM(up_shape, BF16), pltpu.VMEM(down_shape, BF16),
                        pltpu.VMEM((TMX, D_EXPERT), F32), pltpu.VMEM((TMX, D_EXPERT), F32),
                        pltpu.VMEM((IN_SLOTS, TMX, D_MODEL), SORTED), pltpu.VMEM((OUT_SLOTS, TMX, D_MODEL), SORTED),
                        pltpu.SemaphoreType.DMA((2, 3)), pltpu.SemaphoreType.DMA((IN_SLOTS,)),
                        pltpu.SemaphoreType.DMA((OUT_SLOTS,))],
    )
    return pl.pallas_call(
        _experts_kernel,
        grid_spec=grid_spec,
        out_shape=jax.ShapeDtypeStruct(xs.shape, SORTED),
        input_output_aliases={6: 0},
        compiler_params=pltpu.CompilerParams(
            dimension_semantics=("arbitrary",), vmem_limit_bytes=VMEM_LIMIT),
        name="experts",
    )(plan["tile_expert"], plan["n_used"], plan["tile_first"], plan["tile_slot"], plan["tile_next"], plan["tile_rows"],
      xs, wg, wu, wd)


def _combine_kernel(local_start_ref, n_big_ref, big_src_ref, big_dst_ref, n_small_ref, small_src_ref, small_dst_ref,
                    ys_ref, ri_ref, rw_ref, x1_ref, mod_ref, gain_ref, o_ref, buf_ref, sem_ref,
                    *, r_loc, tiles_per_batch):
    b = pl.program_id(0)
    nb = pl.num_programs(0)
    per_step = buf_ref.shape[0] // 2
    tm = x1_ref.shape[0] // per_step
    gen = b % 2
    fetch_tile, wait_tile = _run_pieces(
        (n_big_ref, big_src_ref, big_dst_ref, n_small_ref, small_src_ref, small_dst_ref), buf_ref, ys_ref, sem_ref,
        to_sorted=False)

    def fetch(step, g):
        for t in range(per_step):
            fetch_tile(step * per_step + t, g * per_step + t)

    @pl.when(b == 0)
    def _():
        buf_ref[...] = jnp.zeros_like(buf_ref)
        fetch(0, 0)

    @pl.when(b + 1 < nb)
    def _():
        fetch(b + 1, 1 - gen)

    for t in range(per_step):
        wait_tile(b * per_step + t, gen * per_step + t)

    tiles = [b * per_step + t for t in range(per_step)]
    cols = [slice(t * tm, (t + 1) * tm) for t in range(per_step)]
    slots = [_local_slots(ri_ref, local_start_ref, tile, c) for tile, c in zip(tiles, cols)]
    rows = lax.broadcasted_iota(jnp.int32, (r_loc, tm), 0)
    hits = [(rows == s1, rows == s2) for s1, s2 in slots]
    w_rows = [jnp.sum(jnp.where(h1, rw_ref[0:1, c], jnp.where(h2, rw_ref[1:2, c], 0.0)), axis=1, keepdims=True)
              for (h1, h2), c in zip(hits, cols)]
    perms = [jnp.where(h1 | h2, 1.0, 0.0).astype(BF16) for h1, h2 in hits]
    yws = [(buf_ref[gen * per_step + t].astype(F32) * w_rows[t]).astype(BF16) for t in range(per_step)]
    moes = [_dot_tn(perm, yw) for perm, yw in zip(perms, yws)]
    batch = b // tiles_per_batch
    gate2 = mod_ref[pl.ds(batch, 1), 5 * D_MODEL:6 * D_MODEL]
    for c, moe in zip(cols, moes):
        x2 = x1_ref[c, :] + gate2 * moe
        ms = jnp.mean(x2 * x2, axis=-1, keepdims=True)
        o_ref[c, :] = x2 * lax.rsqrt(ms + EPS) * gain_ref[...]


def _combine(ys, ri, rw, x1, mod, gain, plan, seq, tm):
    tokens = x1.shape[0]
    r_loc = _local_rows(tm)
    per_step = COMBINE_TILES_PER_STEP if seq % (COMBINE_TILES_PER_STEP * tm) == 0 else 1
    rows = per_step * tm
    row_spec = pl.BlockSpec((8, rows), lambda i, *_: (0, i))
    tok_spec = pl.BlockSpec((rows, D_MODEL), lambda i, *_: (i, 0))
    full = lambda a: pl.BlockSpec(a.shape, lambda i, *_: (0,) * a.ndim)
    prefetch = [plan["local_start"]] + [plan[k] for k in PIECE_TABLES]
    grid_spec = pltpu.PrefetchScalarGridSpec(
        num_scalar_prefetch=len(prefetch),
        grid=(tokens // rows,),
        in_specs=[pl.BlockSpec(memory_space=pl.ANY), row_spec, row_spec, tok_spec, full(mod), full(gain)],
        out_specs=tok_spec,
        scratch_shapes=[pltpu.VMEM((2 * per_step, r_loc, D_MODEL), SORTED),
                        pltpu.SemaphoreType.DMA((2 * per_step,))],
    )
    return pl.pallas_call(
        functools.partial(_combine_kernel, r_loc=r_loc, tiles_per_batch=seq // rows),
        grid_spec=grid_spec,
        out_shape=jax.ShapeDtypeStruct((tokens, D_MODEL), F32),
        compiler_params=pltpu.CompilerParams(
            dimension_semantics=("arbitrary",), vmem_limit_bytes=VMEM_LIMIT),
        name="combine",
    )(*prefetch, ys, ri, rw, x1, mod, gain)


def _rotary_tables(seq):
    half = RET_HEAD_DIM // 2
    inv_freq = 1.0 / (ROPE_BASE ** (np.arange(half, dtype=np.float64) / half))
    ang = np.arange(seq, dtype=np.float64)[:, None] * inv_freq[None, :]
    cos = np.cos(ang)
    sin = np.sin(ang)
    f32 = lambda a: jnp.asarray(a.astype(np.float32))
    return f32(np.tile(cos, (1, 4))), f32(np.concatenate([-sin, sin, -sin, sin], axis=1))


def _pick_tile(n, pref):
    t = min(n, pref)
    assert n % t == 0, (n, t)
    return t


def kernel(x, c, ada_w, ada_b, norm1_gain, norm2_gain, w_in, w_out, ret_gn_gain, lam_q1, lam_k1, lam_q2,
           lam_k2, diff_subln_gain, w_group, b_group, w_expert, b_expert, w_gate, w_up, w_down, final_gain):
    batch, seq, d = x.shape
    assert d == D_MODEL and batch <= 8 and ada_w.shape[0] == 1
    layer = 0
    lambda_init = 0.8 - 0.6 * math.exp(-0.3 * layer)
    tokens = batch * seq
    x2 = x.reshape(tokens, d)
    tm = _pick_tile(seq, 512)

    c_pad = jnp.zeros((8, d), F32).at[:batch].set(c)
    mod = _adaln(c_pad, ada_w[layer], ada_b[layer].reshape(1, -1))

    cos_t, sin_t = _rotary_tables(seq)
    rq, rk, rv, rg, dq, dk, dvt = _inproj(
        x2, mod, norm1_gain[layer].reshape(1, d), w_in[layer].astype(BF16), cos_t, sin_t, seq, tm)

    ret_out = _retention(rq, rk, rv, rg, ret_gn_gain[layer].reshape(1, RET_WIDTH), batch, seq,
                         _pick_tile(seq, 256))
    diff_out = _diffattn(
        dq, dk, dvt, lam_q1[layer].reshape(1, -1), lam_k1[layer].reshape(1, -1), lam_q2[layer].reshape(1, -1),
        lam_k2[layer].reshape(1, -1), diff_subln_gain[layer].reshape(-1, 1), batch, seq, lambda_init,
        _pick_tile(seq, 2048), 2 * QUERY_CHUNK)

    w_router = jnp.concatenate(
        [w_group[layer].T, jnp.zeros((8 - N_GROUPS, d), F32), w_expert[layer].reshape(d, N_EXPERTS).T], axis=0)
    b_router = jnp.concatenate(
        [b_group[layer], jnp.zeros((8 - N_GROUPS,), F32), b_expert[layer].reshape(N_EXPERTS)]).reshape(-1, 1)
    wr_hi = w_router.astype(BF16)
    wr_lo = (w_router - wr_hi.astype(F32)).astype(BF16)
    x1, h2, ri, rw, cnt = _outproj(ret_out, diff_out, x2, mod, norm2_gain[layer].reshape(1, d),
                                   w_out[layer].astype(BF16), jnp.concatenate([wr_hi, wr_lo], axis=0), b_router,
                                   seq, tm)

    plan = _dispatch_plan(cnt[:, :, 0], tokens, tm)
    xs = _dispatch(h2, ri, plan, tm)
    ys = _experts(xs, plan, w_gate[layer].reshape(N_EXPERTS, d, D_EXPERT),
                  w_up[layer].reshape(N_EXPERTS, d, D_EXPERT), w_down[layer].reshape(N_EXPERTS, D_EXPERT, d))
    out = _combine(ys, ri, rw, x1, mod, final_gain.reshape(1, d), plan, seq, tm)
    return out.reshape(batch, seq, d)
```

```python
import functools
import math

import jax
import jax.numpy as jnp
import numpy as np
from jax import lax
from jax.experimental import pallas as pl
from jax.experimental.pallas import tpu as pltpu

F32 = jnp.float32
BF16 = jnp.bfloat16

D_MODEL = 1024
RET_HEAD_DIM = 64
RET_WIDTH = 512
RET_HEADS = 8
RET_PAIRS = RET_HEADS // 2
DIFF_QK_DIM = 64
DIFF_V_DIM = 128
DIFF_HEADS = 4
DIFF_WIDTH = 512
N_GROUPS = 4
EXPERTS_PER_GROUP = 8
N_EXPERTS = N_GROUPS * EXPERTS_PER_GROUP
D_EXPERT = 512
N_MOD = 6
ROPE_BASE = 10000.0
EPS = 1e-6
LANES = 128
SUBLANES = 8
BF16_SUBLANES = 2 * SUBLANES
ROUTER_ROWS = SUBLANES + N_EXPERTS
VMEM_LIMIT = 56 * 1024 * 1024
TOKEN_TILE = 512
RET_CHUNK = 256
ATTN_Q_TILE = 2048


def _dot(a, b):
    return jnp.dot(a, b, preferred_element_type=F32)


def _dot_nt(a, b):
    return lax.dot_general(a, b, (((1,), (1,)), ((), ())), preferred_element_type=F32)


def _dot_tn(a, b):
    return lax.dot_general(a, b, (((0,), (0,)), ((), ())), preferred_element_type=F32)


def _split_bf16(x):
    hi = x.astype(BF16)
    lo = (x - hi.astype(F32)).astype(BF16)
    return hi, lo


def _silu(x):
    return x / (1.0 + jnp.exp(-x))


def _adaln_kernel(c_ref, w_ref, b_ref, o_ref):
    ca = _silu(c_ref[...])
    c_hi, c_lo = _split_bf16(ca)
    w_hi, w_lo = _split_bf16(w_ref[...])
    o_ref[...] = _dot(c_hi, w_hi) + _dot(c_lo, w_hi) + _dot(c_hi, w_lo) + b_ref[...]


def _adaln(c_pad, ada_w, ada_b):
    n_out = ada_w.shape[1]
    tn = D_MODEL
    return pl.pallas_call(
        _adaln_kernel,
        grid=(n_out // tn,),
        in_specs=[
            pl.BlockSpec((SUBLANES, D_MODEL), lambda j: (0, 0)),
            pl.BlockSpec((D_MODEL, tn), lambda j: (0, j)),
            pl.BlockSpec((1, tn), lambda j: (0, j)),
        ],
        out_specs=pl.BlockSpec((SUBLANES, tn), lambda j: (0, j)),
        out_shape=jax.ShapeDtypeStruct((SUBLANES, n_out), F32),
        compiler_params=pltpu.CompilerParams(vmem_limit_bytes=VMEM_LIMIT),
        name="adaln",
    )(c_pad, ada_w, ada_b)


def _norm_modulate(x, gain, shift, scale):
    ms = jnp.mean(x * x, axis=-1, keepdims=True)
    y = x * lax.rsqrt(ms + EPS) * gain
    return y * (1.0 + scale) + shift


def _rotary_slab(x, cos, sin_signed, lane_lo):
    swapped = jnp.where(lane_lo, pltpu.roll(x, 96, 1), pltpu.roll(x, 32, 1))
    return x * cos + swapped * sin_signed


def _inproj_kernel(x_ref, mod_ref, gain_ref, w_ref, cos_ref, sin_ref,
                   rq_ref, rk_ref, rv_ref, rg_ref, dq_ref, dk_ref, dvt_ref, *, tiles_per_batch):
    b = pl.program_id(0) // tiles_per_batch
    shift = mod_ref[pl.ds(b, 1), 0:D_MODEL]
    scale = mod_ref[pl.ds(b, 1), D_MODEL:2 * D_MODEL]
    h = _norm_modulate(x_ref[...], gain_ref[...], shift, scale).astype(BF16)
    cos = cos_ref[...]
    sin = sin_ref[...]
    lane = lax.broadcasted_iota(jnp.int32, cos.shape, 1)
    lane_lo = (lane % 64) < 32

    def proj(chunk):
        return _dot(h, w_ref[:, chunk * RET_WIDTH:(chunk + 1) * RET_WIDTH])

    def rotary(acc, out_ref, post_scale):
        for s in range(RET_WIDTH // LANES):
            sl = slice(s * LANES, (s + 1) * LANES)
            out_ref[:, sl] = (_rotary_slab(acc[:, sl], cos, sin, lane_lo) * post_scale).astype(BF16)

    rotary(proj(0), rq_ref, 1.0)
    rotary(proj(1), rk_ref, RET_HEAD_DIM ** -0.5)
    rv_ref[...] = proj(2).astype(BF16)
    rg_ref[...] = _silu(proj(3)).astype(BF16)
    dq_ref[...] = (proj(4) * (DIFF_QK_DIM ** -0.5 * math.log2(math.e))).astype(BF16)
    dk_ref[...] = proj(5).astype(BF16)
    dvt_ref[...] = proj(6).T.astype(BF16)


def _inproj(x2, mod, gain, w_in, cos_t, sin_t, seq, tm):
    tokens = x2.shape[0]
    tiles_per_batch = seq // tm
    tok_spec = lambda w: pl.BlockSpec((tm, w), lambda i: (i, 0))
    tab_spec = pl.BlockSpec((tm, LANES), lambda i: (i % tiles_per_batch, 0))
    full = lambda a: pl.BlockSpec(a.shape, lambda i: (0,) * a.ndim)
    out = jax.ShapeDtypeStruct((tokens, RET_WIDTH), BF16)
    return pl.pallas_call(
        functools.partial(_inproj_kernel, tiles_per_batch=tiles_per_batch),
        grid=(tokens // tm,),
        in_specs=[tok_spec(D_MODEL), full(mod), full(gain), full(w_in), tab_spec, tab_spec],
        out_specs=[tok_spec(RET_WIDTH)] * 6 + [pl.BlockSpec((DIFF_WIDTH, tm), lambda i: (0, i))],
        out_shape=[out] * 6 + [jax.ShapeDtypeStruct((DIFF_WIDTH, tokens), BF16)],
        compiler_params=pltpu.CompilerParams(
            dimension_semantics=("parallel",), vmem_limit_bytes=VMEM_LIMIT),
        name="inproj",
    )(x2, mod, gain, w_in, cos_t, sin_t)


RET_SEQ_GROUP = 4


def _retention_kernel(q_ref, k_ref, v_ref, g_ref, dec_ref, qdec_ref, kdec_ref, rdec_ref,
                      bmask_ref, gmean_ref, gain_ref, o_ref, state_ref, *, chunk):
    @pl.when(pl.program_id(1) == 0)
    def _():
        state_ref[...] = jnp.zeros_like(state_ref)

    lane = lax.broadcasted_iota(jnp.int32, (chunk, LANES), 1)
    first_head = lane < RET_HEAD_DIM
    gmean = gmean_ref[...]
    bmask = bmask_ref[...]
    units = [(s, p) for s in range(q_ref.shape[0]) for p in range(RET_PAIRS)]
    sl = lambda p: slice(p * LANES, (p + 1) * LANES)
    q = [q_ref[s, :, sl(p)] for s, p in units]
    k = [k_ref[s, :, sl(p)] for s, p in units]
    v = [v_ref[s, :, sl(p)] for s, p in units]
    zero = jnp.zeros_like(q[0])
    q_stack = [jnp.concatenate([jnp.where(first_head, qu, zero), jnp.where(first_head, zero, qu)], axis=0)
               for qu in q]
    scores = [(_dot_nt(q_stack[n], k[n]) * dec_ref[p]).astype(BF16) for n, (_, p) in enumerate(units)]
    state = [state_ref[n] for n in range(len(units))]
    cross = [_dot(q[n], state[n].astype(BF16)) * qdec_ref[:, sl(p)] for n, (_, p) in enumerate(units)]
    k_dec = [(k[n].astype(F32) * kdec_ref[:, sl(p)]).astype(BF16) for n, (_, p) in enumerate(units)]
    for n, (_, p) in enumerate(units):
        state_ref[n] = state[n] * rdec_ref[p] + _dot_tn(k_dec[n], v[n]) * bmask
    intra2 = [_dot(scores[n], v[n]) for n in range(len(units))]
    y = [jnp.where(first_head, intra2[n][:chunk], intra2[n][chunk:]) + cross[n] for n in range(len(units))]
    seg_mean = lambda x: _dot(jnp.concatenate(_split_bf16(x), axis=1), gmean)
    mu = [seg_mean(yu) for yu in y]
    d = [yu - mu_u for yu, mu_u in zip(y, mu)]
    var = [seg_mean(du * du) for du in d]
    for n, (s, p) in enumerate(units):
        yn = d[n] * lax.rsqrt(var[n] + EPS) * gain_ref[:, sl(p)]
        o_ref[s, :, sl(p)] = (g_ref[s, :, sl(p)].astype(F32) * yn).astype(BF16)


def _retention_tables(chunk):
    heads = np.arange(RET_HEADS, dtype=np.float64)
    log_gamma = np.log(1.0 - np.exp2(-5.0 - heads))
    idx = np.arange(chunk)
    rel = (idx[:, None] - idx[None, :]).astype(np.float64)
    decay = np.where(rel[None] >= 0, np.exp(log_gamma[:, None, None] * np.maximum(rel, 0.0)[None]), 0.0)
    dec2 = decay.reshape(RET_PAIRS, 2 * chunk, chunk)
    lane_lg = np.repeat(log_gamma, RET_HEAD_DIM)
    qdec = np.exp(lane_lg[None, :] * (idx + 1)[:, None])
    kdec = np.exp(lane_lg[None, :] * (chunk - 1 - idx)[:, None])
    rdec = np.exp(lane_lg * chunk).reshape(RET_PAIRS, LANES, 1) * np.ones((1, 1, LANES))
    blk = np.arange(LANES) // RET_HEAD_DIM
    bmask = (blk[:, None] == blk[None, :]).astype(np.float64)
    f32 = lambda a: jnp.asarray(a.astype(np.float32))
    gmean2 = np.concatenate([bmask, bmask], axis=0) / RET_HEAD_DIM
    return f32(dec2), f32(qdec), f32(kdec), f32(rdec), f32(bmask), f32(gmean2).astype(BF16)


def _retention(rq, rk, rv, rg, gn_gain, batch, seq, chunk):
    nc = seq // chunk
    group = RET_SEQ_GROUP if batch % RET_SEQ_GROUP == 0 else 1
    dec2, qdec, kdec, rdec, bmask, gmean = _retention_tables(chunk)
    tok_spec = pl.BlockSpec((group, chunk, RET_WIDTH), lambda b, n: (b, n, 0))
    full = lambda a: pl.BlockSpec(a.shape, lambda b, n: (0,) * a.ndim)
    by_seq = lambda a: a.reshape(batch, seq, RET_WIDTH)
    out = pl.pallas_call(
        functools.partial(_retention_kernel, chunk=chunk),
        grid=(batch // group, nc),
        in_specs=[tok_spec] * 4 + [full(dec2), full(qdec), full(kdec), full(rdec), full(bmask),
                                   full(gmean), full(gn_gain)],
        out_specs=tok_spec,
        out_shape=jax.ShapeDtypeStruct((batch, seq, RET_WIDTH), BF16),
        scratch_shapes=[pltpu.VMEM((group * RET_PAIRS, LANES, LANES), F32)],
        compiler_params=pltpu.CompilerParams(
            dimension_semantics=("parallel", "arbitrary"), vmem_limit_bytes=VMEM_LIMIT),
        name="retention",
    )(by_seq(rq), by_seq(rk), by_seq(rv), by_seq(rg), dec2, qdec, kdec, rdec, bmask, gmean, gn_gain)
    return out.reshape(batch * seq, RET_WIDTH)


NEG_BIG = -1e30


V_EXT_ROWS = DIFF_V_DIM + BF16_SUBLANES
QUERY_CHUNK = 256
SCORES_AHEAD_FULL = 3
SCORES_AHEAD_DIAG = 3
STAGE_SLOTS = 16


def _diag_chunks(tq, tk, d):
    assert tk == 2 * QUERY_CHUNK
    per_softmax = tq // QUERY_CHUNK
    out = []
    for c in range(2 * per_softmax):
        q0 = (c % per_softmax) * QUERY_CHUNK
        if q0 + QUERY_CHUNK - 1 < d * tk:
            continue
        kind = "full" if q0 >= (d + 1) * tk else ("tri" if q0 == d * tk else "low_tri")
        out.append((c, kind))
    return out


def _accumulate(acc_ref, cs, alpha, pv):
    acc_ref[:, cs] = alpha * acc_ref[:, cs] + pv


def _diffattn_kernel(q_ref, k_ref, vt_ref, lq1_ref, lk1_ref, lq2_ref, lk2_ref, gain_ref, bias_ref, o_ref,
                     qs_ref, vext_ref, m_ref, acc_ref, stage_ref, *, tq, tk, lambda_init):
    i = pl.program_id(2)
    nk = vext_ref.shape[0]

    @pl.when(i == 0)
    def _():
        for j in range(nk):
            vext_ref[j, 0:DIFF_V_DIM, :] = vt_ref[:, j * tk:(j + 1) * tk]
            vext_ref[j, DIFF_V_DIM:V_EXT_ROWS, :] = jnp.ones((V_EXT_ROWS - DIFF_V_DIM, tk), BF16)

    q = q_ref[...]
    lane = lax.broadcasted_iota(jnp.int32, q.shape, 1)
    zero = jnp.zeros_like(q)
    qs_ref[0:tq, :] = jnp.where(lane < DIFF_QK_DIM, q, zero)
    qs_ref[tq:2 * tq, :] = jnp.where(lane < DIFF_QK_DIM, zero, q)
    m_ref[...] = jnp.full_like(m_ref, NEG_BIG)
    acc_ref[...] = jnp.zeros_like(acc_ref)

    def step(work, n_ahead):
        chunk = lambda c: slice(c * QUERY_CHUNK, (c + 1) * QUERY_CHUNK)

        def scores(j, c, kind):
            n_keys = QUERY_CHUNK if kind == "tri" else tk
            start = pl.multiple_of(j * tk, tk)
            return _dot_nt(k_ref[pl.ds(start, n_keys), :], qs_ref[chunk(c), :])

        ahead = [scores(*work[n]) for n in range(min(n_ahead, len(work)))]
        pending = None
        for n, (j, c, kind) in enumerate(work):
            cs = chunk(c)
            st = ahead.pop(0)
            if n + n_ahead < len(work):
                ahead.append(scores(*work[n + n_ahead]))
            slot = n % stage_ref.shape[0]
            n_keys = st.shape[0]
            stage_ref[slot, 0:n_keys, :] = st
            if kind == "full":
                st = stage_ref[slot]
            else:
                causal = stage_ref[slot, n_keys - QUERY_CHUNK:n_keys, :] + bias_ref[...]
                st = causal if kind == "tri" else jnp.concatenate(
                    [stage_ref[slot, 0:n_keys - QUERY_CHUNK, :], causal], axis=0)
            m_old = m_ref[:, cs]
            m_new = jnp.maximum(m_old, jnp.max(st, axis=0, keepdims=True))
            alpha = jnp.exp2(m_old - m_new)
            p = jnp.exp2(st - m_new).astype(BF16)
            m_ref[:, cs] = m_new
            pv = _dot(vext_ref[j, :, 0:st.shape[0]], p)
            if pending is not None:
                pending()
            pending = functools.partial(_accumulate, acc_ref, cs, alpha, pv)
        pending()

    tiles_per_q = tq // tk
    n_chunks = 2 * tq // QUERY_CHUNK

    def full_tiles(it):
        return [(it * tiles_per_q + d, c, "full") for d in range(tiles_per_q) for c in range(n_chunks)]

    lax.fori_loop(0, i, lambda it, c: (step(full_tiles(it), SCORES_AHEAD_FULL), c)[1], 0)
    kind_order = {"full": 0, "low_tri": 1, "tri": 2}
    diag = lambda it: [(it * tiles_per_q + d, c, kind) for d in range(tiles_per_q)
                       for c, kind in sorted(_diag_chunks(tq, tk, d), key=lambda ck: kind_order[ck[1]])]
    lax.fori_loop(i, i + 1, lambda it, c: (step(diag(it), SCORES_AHEAD_DIAG), c)[1], 0)

    lam = (jnp.exp(jnp.sum(lq1_ref[...] * lk1_ref[...], axis=-1, keepdims=True))
           - jnp.exp(jnp.sum(lq2_ref[...] * lk2_ref[...], axis=-1, keepdims=True)) + lambda_init)
    acc = acc_ref[...]
    o2 = acc[0:DIFF_V_DIM, :] * (1.0 / acc[DIFF_V_DIM:DIFF_V_DIM + 1, :])
    ot = o2[:, :tq] - lam * o2[:, tq:]
    ms = jnp.mean(ot * ot, axis=0, keepdims=True)
    ot = ot * lax.rsqrt(ms + EPS) * gain_ref[...] * (1.0 - lambda_init)
    o_ref[...] = ot.T.astype(BF16)


def _diffattn(dq, dk, dvt, lam_q1, lam_k1, lam_q2, lam_k2, gain, batch, seq, lambda_init, tq, tk):
    nq = seq // tq
    q_spec = pl.BlockSpec((tq, LANES), lambda b, h, i: (b * nq + i, h))
    k_spec = pl.BlockSpec((seq, LANES), lambda b, h, i: (b, h))
    vt_spec = pl.BlockSpec((DIFF_V_DIM, seq), lambda b, h, i: (h, b))
    vec = lambda a: pl.BlockSpec(a.shape, lambda b, h, i: (0, 0))
    key = np.arange(QUERY_CHUNK)[:, None]
    query = np.arange(QUERY_CHUNK)[None, :]
    bias = jnp.asarray(np.where(key <= query, 0.0, NEG_BIG), F32)
    return pl.pallas_call(
        functools.partial(_diffattn_kernel, tq=tq, tk=tk, lambda_init=lambda_init),
        grid=(batch, DIFF_HEADS, nq),
        in_specs=[q_spec, k_spec, vt_spec, vec(lam_q1), vec(lam_k1), vec(lam_q2), vec(lam_k2), vec(gain),
                  vec(bias)],
        out_specs=q_spec,
        out_shape=jax.ShapeDtypeStruct(dq.shape, BF16),
        scratch_shapes=[
            pltpu.VMEM((2 * tq, LANES), BF16),
            pltpu.VMEM((seq // tk, V_EXT_ROWS, tk), BF16),
            pltpu.VMEM((1, 2 * tq), F32),
            pltpu.VMEM((V_EXT_ROWS, 2 * tq), F32),
            pltpu.VMEM((STAGE_SLOTS, tk, QUERY_CHUNK), F32),
        ],
        compiler_params=pltpu.CompilerParams(
            dimension_semantics=("parallel", "parallel", "arbitrary"), vmem_limit_bytes=VMEM_LIMIT),
        name="diffattn",
    )(dq, dk, dvt, lam_q1, lam_k1, lam_q2, lam_k2, gain, bias)


def _route(logits):
    r = [logits[g:g + 1, :] for g in range(N_GROUPS)]
    gmax = jnp.maximum(jnp.maximum(r[0], r[1]), jnp.maximum(r[2], r[3]))
    g_idx = jnp.where(r[0] == gmax, 0, jnp.where(r[1] == gmax, 1, jnp.where(r[2] == gmax, 2, 3)))
    denom = sum(jnp.exp(rg - gmax) for rg in r)
    g_weight = 1.0 / denom
    sel = jnp.zeros((EXPERTS_PER_GROUP, logits.shape[1]), F32)
    for g in range(N_GROUPS):
        rows = logits[SUBLANES + g * EXPERTS_PER_GROUP:SUBLANES + (g + 1) * EXPERTS_PER_GROUP, :]
        sel = jnp.where(g_idx == g, rows, sel)
    eidx = lax.broadcasted_iota(jnp.int32, sel.shape, 0)
    v1 = jnp.max(sel, axis=0, keepdims=True)
    i1 = jnp.min(jnp.where(sel == v1, eidx, EXPERTS_PER_GROUP), axis=0, keepdims=True)
    sel2 = jnp.where(eidx == i1, -jnp.inf, sel)
    v2 = jnp.max(sel2, axis=0, keepdims=True)
    i2 = jnp.min(jnp.where(sel2 == v2, eidx, EXPERTS_PER_GROUP), axis=0, keepdims=True)
    e2 = jnp.exp(v2 - v1)
    w1 = g_weight / (1.0 + e2)
    w2 = g_weight * e2 / (1.0 + e2)
    return g_idx, i1, i2, w1, w2


OUTPROJ_PARTS = 2
OUTPROJ_TILES_PER_STEP = 2


def _outproj_kernel(ret_ref, diff_ref, x_ref, mod_ref, gain_ref, wo_ref, wr_ref, br_ref, tri_ref,
                    x1_ref, h2_ref, ri_ref, rw_ref, cnt_ref, *, tiles_per_batch):
    b = pl.program_id(0) // tiles_per_batch
    gate1 = mod_ref[pl.ds(b, 1), 2 * D_MODEL:3 * D_MODEL]
    shift = mod_ref[pl.ds(b, 1), 3 * D_MODEL:4 * D_MODEL]
    scale = mod_ref[pl.ds(b, 1), 4 * D_MODEL:5 * D_MODEL]
    wr = wr_ref[...]
    rows = x_ref.shape[0]
    tm = tri_ref.shape[0]
    n_parts = OUTPROJ_PARTS * rows // tm
    parts = [slice(n * rows // n_parts, (n + 1) * rows // n_parts) for n in range(n_parts)]
    mix = [_dot(jnp.concatenate([ret_ref[r, :], diff_ref[r, :]], axis=1), wo_ref[...]) for r in parts]
    for r, m in zip(parts, mix):
        x1_ref[r, :] = x_ref[r, :] + gate1 * m
    h_split = [_split_bf16(_norm_modulate(x1_ref[r, :], gain_ref[...], shift, scale)) for r in parts]
    for r, (h_hi, _) in zip(parts, h_split):
        h2_ref[r, :] = h_hi
    by_hi = [_dot_nt(wr, h_hi) for h_hi, _ in h_split]
    by_lo = [_dot_nt(wr[:ROUTER_ROWS], h_lo) for _, h_lo in h_split]
    logits = [a[:ROUTER_ROWS] + a[ROUTER_ROWS:] + c + br_ref[...] for a, c in zip(by_hi, by_lo)]
    routed = [_route(lg) for lg in logits]
    g_idx, i1, i2, w1, w2 = [jnp.concatenate([rt[n] for rt in routed], axis=1) for n in range(5)]
    logits = jnp.concatenate(logits, axis=1)
    e1 = g_idx * EXPERTS_PER_GROUP + i1
    e2 = g_idx * EXPERTS_PER_GROUP + i2
    eidx = lax.broadcasted_iota(jnp.int32, (N_EXPERTS, logits.shape[1]), 0)
    hit1 = eidx == e1
    hit2 = eidx == e2
    onehot = jnp.where(hit1 | hit2, 1.0, 0.0)
    tiles = [slice(n * tm, (n + 1) * tm) for n in range(rows // tm)]
    before = jnp.concatenate([_dot(onehot[:, t].astype(BF16), tri_ref[...]) for t in tiles], axis=1)
    r1 = jnp.sum(jnp.where(hit1, before, 0.0), axis=0, keepdims=True)
    r2 = jnp.sum(jnp.where(hit2, before, 0.0), axis=0, keepdims=True)
    zi = jnp.zeros_like(e1)
    ri_ref[...] = jnp.concatenate([e1, e2, r1.astype(jnp.int32), r2.astype(jnp.int32), zi, zi, zi, zi], axis=0)
    zf = jnp.zeros_like(w1)
    rw_ref[...] = jnp.concatenate([w1, w2, zf, zf, zf, zf, zf, zf], axis=0)
    for n, t in enumerate(tiles):
        counts = jnp.sum(onehot[:, t], axis=1, keepdims=True)
        cnt_ref[n] = jnp.broadcast_to(counts, (N_EXPERTS, LANES)).astype(jnp.int32)


def _outproj(ret_out, diff_out, x2, mod, gain, w_out, wr, br, seq, tm):
    tokens = x2.shape[0]
    n_tiles = tokens // tm
    per_step = OUTPROJ_TILES_PER_STEP if seq % (OUTPROJ_TILES_PER_STEP * tm) == 0 else 1
    rows = per_step * tm
    tri = jnp.asarray(np.arange(tm)[:, None] < np.arange(tm)[None, :], BF16)
    tok_spec = lambda w: pl.BlockSpec((rows, w), lambda i: (i, 0))
    row_spec = pl.BlockSpec((SUBLANES, rows), lambda i: (0, i))
    full = lambda a: pl.BlockSpec(a.shape, lambda i: (0,) * a.ndim)
    return pl.pallas_call(
        functools.partial(_outproj_kernel, tiles_per_batch=seq // rows),
        grid=(tokens // rows,),
        in_specs=[tok_spec(RET_WIDTH), tok_spec(DIFF_WIDTH), tok_spec(D_MODEL), full(mod), full(gain),
                  full(w_out), full(wr), full(br), full(tri)],
        out_specs=[tok_spec(D_MODEL), tok_spec(D_MODEL), row_spec, row_spec,
                   pl.BlockSpec((per_step, N_EXPERTS, LANES), lambda i: (i, 0, 0))],
        out_shape=[jax.ShapeDtypeStruct((tokens, D_MODEL), F32),
                   jax.ShapeDtypeStruct((tokens, D_MODEL), BF16),
                   jax.ShapeDtypeStruct((SUBLANES, tokens), jnp.int32),
                   jax.ShapeDtypeStruct((SUBLANES, tokens), F32),
                   jax.ShapeDtypeStruct((n_tiles, N_EXPERTS, LANES), jnp.int32)],
        compiler_params=pltpu.CompilerParams(
            dimension_semantics=("parallel",), vmem_limit_bytes=VMEM_LIMIT),
        name="outproj",
    )(ret_out, diff_out, x2, mod, gain, w_out, wr, br, tri)


CHUNK = 8
BIG_PIECE = 4 * CHUNK
MAX_SMALL_PIECES = N_EXPERTS * (BIG_PIECE // CHUNK - 1)
PIECE_TABLES = ("n_big", "big_src", "big_dst", "n_small", "small_src", "small_dst")
DISPATCH_TILES_PER_STEP = 2
COMBINE_TILES_PER_STEP = 2
TMX = 512
IN_SLOTS = 3
OUT_SLOTS = 2


def _local_rows(tm):
    rows = 2 * tm + N_EXPERTS * (CHUNK - 1)
    return pl.cdiv(rows, BF16_SUBLANES) * BF16_SUBLANES


def _max_big_pieces(tm):
    return _local_rows(tm) // BIG_PIECE


def _sorted_rows_alloc(tokens, tm):
    worst = 2 * tokens + (tokens // tm) * N_EXPERTS * (CHUNK - 1) + N_EXPERTS * (TMX - CHUNK)
    return (worst + TMX - 1) // TMX * TMX


def _dispatch_plan(cnt, tokens, tm):
    i32 = jnp.int32
    pad = (cnt + CHUNK - 1) // CHUNK * CHUNK
    local_end = jnp.cumsum(pad, axis=1)
    local_start = local_end - pad
    seg_rows = jnp.sum(pad, axis=0)
    seg_pad = (seg_rows + TMX - 1) // TMX * TMX
    seg_end = jnp.cumsum(seg_pad)
    seg_start = seg_end - seg_pad
    run_dst = seg_start[None, :] + jnp.cumsum(pad, axis=0) - pad

    def pieces(count, offset, size, max_n):
        end = jnp.cumsum(count, axis=1)
        start = end - count
        k = jnp.arange(max_n, dtype=i32)[None, :, None]
        owns = (start[:, None, :] <= k) & (k < end[:, None, :])
        within = size * (k - start[:, None, :]) + offset[:, None, :]
        src = jnp.sum(jnp.where(owns, local_start[:, None, :] + within, 0), axis=-1)
        dst = jnp.sum(jnp.where(owns, run_dst[:, None, :] + within, 0), axis=-1)
        return end[:, -1].astype(i32), src.reshape(-1).astype(i32), dst.reshape(-1).astype(i32)

    n_big, big_src, big_dst = pieces(pad // BIG_PIECE, jnp.zeros_like(pad), BIG_PIECE, _max_big_pieces(tm))
    n_small, small_src, small_dst = pieces(pad % BIG_PIECE // CHUNK, pad // BIG_PIECE * BIG_PIECE, CHUNK,
                                           MAX_SMALL_PIECES)
    m = TMX * jnp.arange(_sorted_rows_alloc(tokens, tm) // TMX, dtype=i32)
    tile_expert = jnp.minimum(jnp.sum(seg_end[None, :] <= m[:, None], axis=-1), N_EXPERTS - 1)
    towns = (seg_start[None, :] <= m[:, None]) & (m[:, None] < seg_end[None, :])
    used = seg_pad > 0
    parity = (jnp.cumsum(used) - used) % 2
    eids = jnp.arange(N_EXPERTS, dtype=i32)
    later_used = (eids[None, :] > eids[:, None]) & used[None, :]
    next_used = jnp.min(jnp.where(later_used, eids[None, :], N_EXPERTS), axis=1)
    next_used = jnp.where(next_used == N_EXPERTS, -1, next_used)
    pick = lambda per_expert: jnp.sum(jnp.where(towns, per_expert[None, :], 0), axis=-1)
    tile_first = jnp.sum(jnp.where(towns & (seg_start[None, :] == m[:, None]), 1, 0), axis=-1)
    tile_next = jnp.where(jnp.any(towns, axis=-1), pick(next_used), -1)
    return dict(
        tile_first=tile_first.astype(i32),
        tile_slot=pick(parity).astype(i32),
        tile_next=tile_next.astype(i32),
        tile_rows=jnp.clip(pick(seg_start + seg_rows) - m, 0, TMX).astype(i32),
        local_start=local_start.reshape(-1).astype(i32),
        n_big=n_big, big_src=big_src, big_dst=big_dst,
        n_small=n_small, small_src=small_src, small_dst=small_dst,
        tail_base=(seg_start + seg_rows).astype(i32),
        tail_rows=(seg_pad - seg_rows).astype(i32),
        tile_expert=tile_expert.astype(i32),
        n_used=(seg_end[-1:] // TMX).astype(i32),
    )


WAIT_UNROLL = 8


def _wait_times(copy, n):
    lax.fori_loop(0, n // WAIT_UNROLL, lambda i, c: ([copy.wait() for _ in range(WAIT_UNROLL)], c)[1], 0)
    lax.fori_loop(0, n % WAIT_UNROLL, lambda i, c: (copy.wait(), c)[1], 0)


def _for_each(n, body, unroll=4):
    main = n // unroll
    lax.fori_loop(0, main, lambda i, c: ([body(i * unroll + u) for u in range(unroll)], c)[1], 0)
    lax.fori_loop(main * unroll, n, lambda j, c: (body(j), c)[1], 0)


def _local_slots(ri_ref, local_start_ref, tile, cols=slice(None)):
    e1, e2 = ri_ref[0:1, cols], ri_ref[1:2, cols]
    s1, s2 = ri_ref[2:3, cols], ri_ref[3:4, cols]
    for e in range(N_EXPERTS):
        start = local_start_ref[tile * N_EXPERTS + e]
        s1 = s1 + jnp.where(e1 == e, start, 0)
        s2 = s2 + jnp.where(e2 == e, start, 0)
    return s1, s2


def _run_pieces(piece_refs, local_ref, sorted_ref, sem_ref, to_sorted):
    n_big_ref, big_src_ref, big_dst_ref, n_small_ref, small_src_ref, small_dst_ref = piece_refs
    r_loc = local_ref.shape[1]
    kinds = [(BIG_PIECE, n_big_ref, big_src_ref, big_dst_ref, r_loc // BIG_PIECE),
             (CHUNK, n_small_ref, small_src_ref, small_dst_ref, MAX_SMALL_PIECES)]

    def copy(sl, size, local_row, sorted_row):
        local = local_ref.at[sl, pl.ds(pl.multiple_of(local_row, CHUNK), size), :]
        srt = sorted_ref.at[pl.ds(pl.multiple_of(sorted_row, CHUNK), size), :]
        return pltpu.make_async_copy(local, srt, sem_ref.at[sl]) if to_sorted else \
            pltpu.make_async_copy(srt, local, sem_ref.at[sl])

    def start(tile, sl):
        for size, n_ref, src_ref, dst_ref, max_n in kinds:
            _for_each(n_ref[tile], lambda k: copy(sl, size, src_ref[tile * max_n + k], dst_ref[tile * max_n + k])
                      .start())

    def wait(tile, sl):
        for size, n_ref, _, _, _ in kinds:
            _wait_times(copy(sl, size, 0, 0), n_ref[tile])

    return start, wait


def _dispatch_kernel(local_start_ref, n_big_ref, big_src_ref, big_dst_ref, n_small_ref, small_src_ref, small_dst_ref,
                     tail_base_ref, tail_rows_ref, n_used_ref,
                     h_ref, ri_ref, xs_ref, buf_ref, zero_ref, sem_ref, tail_sem_ref, *, r_loc):
    b = pl.program_id(0)
    nb = pl.num_programs(0)
    per_step = buf_ref.shape[0] // 2
    tm = h_ref.shape[0] // per_step
    gen = b % 2
    start_runs, drain_tile = _run_pieces(
        (n_big_ref, big_src_ref, big_dst_ref, n_small_ref, small_src_ref, small_dst_ref), buf_ref, xs_ref, sem_ref,
        to_sorted=True)

    def drain(step, g):
        for t in range(per_step):
            drain_tile(step * per_step + t, g * per_step + t)

    @pl.when(b >= 2)
    def _():
        drain(b - 2, gen)

    tiles = [b * per_step + t for t in range(per_step)]
    cols = [slice(t * tm, (t + 1) * tm) for t in range(per_step)]
    slots = [_local_slots(ri_ref, local_start_ref, tile, c) for tile, c in zip(tiles, cols)]
    rows = lax.broadcasted_iota(jnp.int32, (r_loc, tm), 0)
    perms = [jnp.where((rows == s1) | (rows == s2), 1.0, 0.0).astype(BF16) for s1, s2 in slots]
    for t, (perm, c) in enumerate(zip(perms, cols)):
        buf_ref[gen * per_step + t] = _dot(perm, h_ref[c, :])
    for t, tile in enumerate(tiles):
        start_runs(tile, gen * per_step + t)

    def tail_pieces(e, act):
        n = tail_rows_ref[e]
        size = TMX // 2
        while size >= CHUNK:
            dst = pl.multiple_of(tail_base_ref[e] + (n & (-2 * size)), CHUNK)
            cp = pltpu.make_async_copy(zero_ref.at[pl.ds(0, size), :], xs_ref.at[pl.ds(dst, size), :],
                                       tail_sem_ref.at[0])
            pl.when((n & size) != 0)(functools.partial(act, cp))
            size //= 2

    def unused_tile_copy(m):
        dst = pl.multiple_of(m * TMX, TMX)
        return pltpu.make_async_copy(zero_ref, xs_ref.at[pl.ds(dst, TMX), :], tail_sem_ref.at[1])

    n_alloc = xs_ref.shape[0] // TMX

    @pl.when(b == 0)
    def _():
        zero_ref[...] = jnp.zeros_like(zero_ref)

    experts_per_step = pl.cdiv(N_EXPERTS, nb)
    lax.fori_loop(jnp.minimum(b * experts_per_step, N_EXPERTS), jnp.minimum((b + 1) * experts_per_step, N_EXPERTS),
                  lambda e, c: (tail_pieces(e, lambda cp: cp.start()), c)[1], 0)
    tiles_per_step = pl.cdiv(n_alloc, nb)
    first_unused = n_used_ref[0]
    lax.fori_loop(jnp.minimum(first_unused + b * tiles_per_step, n_alloc),
                  jnp.minimum(first_unused + (b + 1) * tiles_per_step, n_alloc),
                  lambda m, c: (unused_tile_copy(m).start(), c)[1], 0)

    @pl.when(b == nb - 1)
    def _():
        lax.fori_loop(0, N_EXPERTS, lambda e, c: (tail_pieces(e, lambda cp: cp.wait()), c)[1], 0)
        lax.fori_loop(n_used_ref[0], n_alloc, lambda m, c: (unused_tile_copy(m).wait(), c)[1], 0)

        @pl.when(b >= 1)
        def _():
            drain(b - 1, 1 - gen)

        drain(b, gen)


def _dispatch(h2, ri, plan, tm):
    tokens = h2.shape[0]
    r_loc = _local_rows(tm)
    prefetch = [plan["local_start"]] + [plan[k] for k in PIECE_TABLES] + [
        plan["tail_base"], plan["tail_rows"], plan["n_used"]]
    per_step = DISPATCH_TILES_PER_STEP if (tokens // tm) % DISPATCH_TILES_PER_STEP == 0 else 1
    rows = per_step * tm
    grid_spec = pltpu.PrefetchScalarGridSpec(
        num_scalar_prefetch=len(prefetch),
        grid=(tokens // rows,),
        in_specs=[pl.BlockSpec((rows, D_MODEL), lambda i, *_: (i, 0)),
                  pl.BlockSpec((SUBLANES, rows), lambda i, *_: (0, i))],
        out_specs=pl.BlockSpec(memory_space=pl.ANY),
        scratch_shapes=[pltpu.VMEM((2 * per_step, r_loc, D_MODEL), F32), pltpu.VMEM((TMX, D_MODEL), F32),
                        pltpu.SemaphoreType.DMA((2 * per_step,)), pltpu.SemaphoreType.DMA((2,))],
    )
    return pl.pallas_call(
        functools.partial(_dispatch_kernel, r_loc=r_loc),
        grid_spec=grid_spec,
        out_shape=jax.ShapeDtypeStruct((_sorted_rows_alloc(tokens, tm), D_MODEL), F32),
        compiler_params=pltpu.CompilerParams(
            dimension_semantics=("arbitrary",), vmem_limit_bytes=VMEM_LIMIT),
        name="dispatch",
    )(*prefetch, h2, ri)


def _experts_kernel(tile_expert_ref, n_used_ref, first_ref, slot_ref, next_ref, rows_ref, xs_hbm, wg_hbm, wu_hbm, wd_hbm,
                    ys_hbm, wg_st, wu_st, wd_st, wg_bf, wu_bf, wd_bf, a_ref, u_ref, xin_ref, yout_ref,
                    sem_ref, in_sem_ref, out_sem_ref):
    m = pl.program_id(0)
    n_used = n_used_ref[0]

    def weight_copies(e, s):
        return [pltpu.make_async_copy(src.at[e], dst.at[s], sem_ref.at[s, n])
                for n, (src, dst) in enumerate([(wg_hbm, wg_st), (wu_hbm, wu_st), (wd_hbm, wd_st)])]

    def in_copy(t):
        s = t % IN_SLOTS
        return pltpu.make_async_copy(xs_hbm.at[pl.ds(pl.multiple_of(t * TMX, TMX), TMX), :], xin_ref.at[s],
                                     in_sem_ref.at[s])

    def out_copy(t):
        s = t % OUT_SLOTS
        return pltpu.make_async_copy(yout_ref.at[s], ys_hbm.at[pl.ds(pl.multiple_of(t * TMX, TMX), TMX), :],
                                     out_sem_ref.at[s])

    @pl.when(m == 0)
    def _():
        for t in range(IN_SLOTS - 1):
            pl.when(t < n_used)(in_copy(t).start)

    @pl.when(m < n_used)
    def _():
        @pl.when(m + IN_SLOTS - 1 < n_used)
        def _():
            in_copy(m + IN_SLOTS - 1).start()

        @pl.when(first_ref[m] == 1)
        def _():
            s = slot_ref[m]

            @pl.when(m == 0)
            def _():
                for cp in weight_copies(tile_expert_ref[0], 0):
                    cp.start()

            for cp in weight_copies(tile_expert_ref[m], s):
                cp.wait()

            @pl.when(next_ref[m] >= 0)
            def _():
                for cp in weight_copies(next_ref[m], 1 - s):
                    cp.start()

            wg_bf[...] = wg_st[s].astype(BF16)
            wu_bf[...] = wu_st[s].astype(BF16)
            wd_bf[...] = wd_st[s].astype(BF16)

        xs_ref = xin_ref.at[m % IN_SLOTS]
        ys_ref = yout_ref.at[m % OUT_SLOTS]
        in_copy(m).wait()

        @pl.when(m >= OUT_SLOTS)
        def _():
            out_copy(m - OUT_SLOTS).wait()

        def mlp(rows):
            x = xs_ref[rows, :].astype(BF16)
            a_ref[rows, :] = _dot(x, wg_bf[...])
            u_ref[rows, :] = _dot(x, wu_bf[...])
            hid = (_silu(a_ref[rows, :]) * u_ref[rows, :]).astype(BF16)
            ys_ref[rows, :] = _dot(hid, wd_bf[...])

        half = TMX // 2

        @pl.when(rows_ref[m] > half)
        def _():
            mlp(slice(0, TMX))

        @pl.when(rows_ref[m] <= half)
        def _():
            mlp(slice(0, half))
            ys_ref[half:, :] = jnp.zeros((TMX - half, D_MODEL), F32)

        out_copy(m).start()

        @pl.when(m == n_used - 1)
        def _():
            for back in range(OUT_SLOTS):
                pl.when(m - back >= 0)(out_copy(m - back).wait)


def _experts(xs, plan, wg, wu, wd):
    n_tiles = xs.shape[0] // TMX
    hbm = pl.BlockSpec(memory_space=pl.ANY)
    up_shape, down_shape = (D_MODEL, D_EXPERT), (D_EXPERT, D_MODEL)
    grid_spec = pltpu.PrefetchScalarGridSpec(
        num_scalar_prefetch=6,
        grid=(n_tiles,),
        in_specs=[hbm, hbm, hbm, hbm],
        out_specs=hbm,
        scratch_shapes=[pltpu.VMEM((2,) + up_shape, F32), pltpu.VMEM((2,) + up_shape, F32),
                        pltpu.VMEM((2,) + down_shape, F32),
                        pltpu.VMEM(up_shape, BF16), pltpu.VMEM(up_shape, BF16), pltpu.VMEM(down_shape, BF16),
                        pltpu.VMEM((TMX, D_EXPERT), F32), pltpu.VMEM((TMX, D_EXPERT), F32),
                        pltpu.VMEM((IN_SLOTS, TMX, D_MODEL), F32), pltpu.VMEM((OUT_SLOTS, TMX, D_MODEL), F32),
                        pltpu.SemaphoreType.DMA((2, 3)), pltpu.SemaphoreType.DMA((IN_SLOTS,)),
                        pltpu.SemaphoreType.DMA((OUT_SLOTS,))],
    )
    return pl.pallas_call(
        _experts_kernel,
        grid_spec=grid_spec,
        out_shape=jax.ShapeDtypeStruct(xs.shape, F32),
        input_output_aliases={6: 0},
        compiler_params=pltpu.CompilerParams(
            dimension_semantics=("arbitrary",), vmem_limit_bytes=VMEM_LIMIT),
        name="experts",
    )(plan["tile_expert"], plan["n_used"], plan["tile_first"], plan["tile_slot"], plan["tile_next"], plan["tile_rows"],
      xs, wg, wu, wd)


def _combine_kernel(local_start_ref, n_big_ref, big_src_ref, big_dst_ref, n_small_ref, small_src_ref, small_dst_ref,
                    ys_ref, ri_ref, rw_ref, x1_ref, mod_ref, gain_ref, o_ref, buf_ref, sem_ref,
                    *, r_loc, tiles_per_batch):
    b = pl.program_id(0)
    nb = pl.num_programs(0)
    per_step = buf_ref.shape[0] // 2
    tm = x1_ref.shape[0] // per_step
    gen = b % 2
    fetch_tile, wait_tile = _run_pieces(
        (n_big_ref, big_src_ref, big_dst_ref, n_small_ref, small_src_ref, small_dst_ref), buf_ref, ys_ref, sem_ref,
        to_sorted=False)

    def fetch(step, g):
        for t in range(per_step):
            fetch_tile(step * per_step + t, g * per_step + t)

    @pl.when(b == 0)
    def _():
        buf_ref[...] = jnp.zeros_like(buf_ref)
        fetch(0, 0)

    @pl.when(b + 1 < nb)
    def _():
        fetch(b + 1, 1 - gen)

    for t in range(per_step):
        wait_tile(b * per_step + t, gen * per_step + t)

    tiles = [b * per_step + t for t in range(per_step)]
    cols = [slice(t * tm, (t + 1) * tm) for t in range(per_step)]
    slots = [_local_slots(ri_ref, local_start_ref, tile, c) for tile, c in zip(tiles, cols)]
    rows = lax.broadcasted_iota(jnp.int32, (r_loc, tm), 0)
    hits = [(rows == s1, rows == s2) for s1, s2 in slots]
    w_rows = [jnp.sum(jnp.where(h1, rw_ref[0:1, c], jnp.where(h2, rw_ref[1:2, c], 0.0)), axis=1, keepdims=True)
              for (h1, h2), c in zip(hits, cols)]
    perms = [jnp.where(h1 | h2, 1.0, 0.0).astype(BF16) for h1, h2 in hits]
    yws = [(buf_ref[gen * per_step + t] * w_rows[t]).astype(BF16) for t in range(per_step)]
    moes = [_dot_tn(perm, yw) for perm, yw in zip(perms, yws)]
    batch = b // tiles_per_batch
    gate2 = mod_ref[pl.ds(batch, 1), 5 * D_MODEL:6 * D_MODEL]
    for c, moe in zip(cols, moes):
        x2 = x1_ref[c, :] + gate2 * moe
        ms = jnp.mean(x2 * x2, axis=-1, keepdims=True)
        o_ref[c, :] = x2 * lax.rsqrt(ms + EPS) * gain_ref[...]


def _combine(ys, ri, rw, x1, mod, gain, plan, seq, tm):
    tokens = x1.shape[0]
    r_loc = _local_rows(tm)
    per_step = COMBINE_TILES_PER_STEP if seq % (COMBINE_TILES_PER_STEP * tm) == 0 else 1
    rows = per_step * tm
    row_spec = pl.BlockSpec((SUBLANES, rows), lambda i, *_: (0, i))
    tok_spec = pl.BlockSpec((rows, D_MODEL), lambda i, *_: (i, 0))
    full = lambda a: pl.BlockSpec(a.shape, lambda i, *_: (0,) * a.ndim)
    prefetch = [plan["local_start"]] + [plan[k] for k in PIECE_TABLES]
    grid_spec = pltpu.PrefetchScalarGridSpec(
        num_scalar_prefetch=len(prefetch),
        grid=(tokens // rows,),
        in_specs=[pl.BlockSpec(memory_space=pl.ANY), row_spec, row_spec, tok_spec, full(mod), full(gain)],
        out_specs=tok_spec,
        scratch_shapes=[pltpu.VMEM((2 * per_step, r_loc, D_MODEL), F32), pltpu.SemaphoreType.DMA((2 * per_step,))],
    )
    return pl.pallas_call(
        functools.partial(_combine_kernel, r_loc=r_loc, tiles_per_batch=seq // rows),
        grid_spec=grid_spec,
        out_shape=jax.ShapeDtypeStruct((tokens, D_MODEL), F32),
        compiler_params=pltpu.CompilerParams(
            dimension_semantics=("arbitrary",), vmem_limit_bytes=VMEM_LIMIT),
        name="combine",
    )(*prefetch, ys, ri, rw, x1, mod, gain)


def _rotary_tables(seq):
    half = RET_HEAD_DIM // 2
    inv_freq = 1.0 / (ROPE_BASE ** (np.arange(half, dtype=np.float64) / half))
    ang = np.arange(seq, dtype=np.float64)[:, None] * inv_freq[None, :]
    cos = np.cos(ang)
    sin = np.sin(ang)
    f32 = lambda a: jnp.asarray(a.astype(np.float32))
    return f32(np.tile(cos, (1, 4))), f32(np.concatenate([-sin, sin, -sin, sin], axis=1))


def _pick_tile(n, pref):
    t = min(n, pref)
    assert n % t == 0, (n, t)
    return t


def kernel(x, c, ada_w, ada_b, norm1_gain, norm2_gain, w_in, w_out, ret_gn_gain, lam_q1, lam_k1, lam_q2,
           lam_k2, diff_subln_gain, w_group, b_group, w_expert, b_expert, w_gate, w_up, w_down, final_gain):
    batch, seq, d = x.shape
    assert d == D_MODEL and batch <= 8 and ada_w.shape[0] == 1
    layer = 0
    lambda_init = 0.8 - 0.6 * math.exp(-0.3 * layer)
    tokens = batch * seq
    x2 = x.reshape(tokens, d)
    tm = _pick_tile(seq, TOKEN_TILE)

    c_pad = jnp.zeros((SUBLANES, d), F32).at[:batch].set(c)
    mod = _adaln(c_pad, ada_w[layer], ada_b[layer].reshape(1, -1))

    cos_t, sin_t = _rotary_tables(seq)
    rq, rk, rv, rg, dq, dk, dvt = _inproj(
        x2, mod, norm1_gain[layer].reshape(1, d), w_in[layer].astype(BF16), cos_t, sin_t, seq, tm)

    ret_out = _retention(rq, rk, rv, rg, ret_gn_gain[layer].reshape(1, RET_WIDTH), batch, seq,
                         _pick_tile(seq, RET_CHUNK))
    diff_out = _diffattn(
        dq, dk, dvt, lam_q1[layer].reshape(1, -1), lam_k1[layer].reshape(1, -1), lam_q2[layer].reshape(1, -1),
        lam_k2[layer].reshape(1, -1), diff_subln_gain[layer].reshape(-1, 1), batch, seq, lambda_init,
        _pick_tile(seq, ATTN_Q_TILE), 2 * QUERY_CHUNK)

    w_router = jnp.concatenate(
        [w_group[layer].T, jnp.zeros((SUBLANES - N_GROUPS, d), F32), w_expert[layer].reshape(d, N_EXPERTS).T], axis=0)
    b_router = jnp.concatenate(
        [b_group[layer], jnp.zeros((SUBLANES - N_GROUPS,), F32), b_expert[layer].reshape(N_EXPERTS)]).reshape(-1, 1)
    wr_hi = w_router.astype(BF16)
    wr_lo = (w_router - wr_hi.astype(F32)).astype(BF16)
    x1, h2, ri, rw, cnt = _outproj(ret_out, diff_out, x2, mod, norm2_gain[layer].reshape(1, d),
                                   w_out[layer].astype(BF16), jnp.concatenate([wr_hi, wr_lo], axis=0), b_router,
                                   seq, tm)

    plan = _dispatch_plan(cnt[:, :, 0], tokens, tm)
    xs = _dispatch(h2, ri, plan, tm)
    ys = _experts(xs, plan, w_gate[layer].reshape(N_EXPERTS, d, D_EXPERT),
                  w_up[layer].reshape(N_EXPERTS, d, D_EXPERT), w_down[layer].reshape(N_EXPERTS, D_EXPERT, d))
    out = _combine(ys, ri, rw, x1, mod, final_gain.reshape(1, d), plan, seq, tm)
    return out.reshape(batch, seq, d)
```

```python
import functools
import math

import jax
import jax.numpy as jnp
import numpy as np
from jax import lax
from jax.experimental import pallas as pl
from jax.experimental.pallas import tpu as pltpu

F32 = jnp.float32
BF16 = jnp.bfloat16

D_MODEL = 1024
RET_HEAD_DIM = 64
RET_WIDTH = 512
RET_HEADS = 8
RET_PAIRS = RET_HEADS // 2
DIFF_QK_DIM = 64
DIFF_V_DIM = 128
DIFF_HEADS = 4
DIFF_WIDTH = 512
N_GROUPS = 4
EXPERTS_PER_GROUP = 8
N_EXPERTS = N_GROUPS * EXPERTS_PER_GROUP
D_EXPERT = 512
N_MOD = 6
ROPE_BASE = 10000.0
EPS = 1e-6
LANES = 128
SUBLANES = 8
BF16_SUBLANES = 2 * SUBLANES
ROUTER_ROWS = SUBLANES + N_EXPERTS
VMEM_LIMIT = 56 * 1024 * 1024
TOKEN_TILE = 512
RET_CHUNK = 128
ATTN_Q_TILE = 2048


def _dot(a, b):
    return jnp.dot(a, b, preferred_element_type=F32)


def _dot_nt(a, b):
    return lax.dot_general(a, b, (((1,), (1,)), ((), ())), preferred_element_type=F32)


def _dot_tn(a, b):
    return lax.dot_general(a, b, (((0,), (0,)), ((), ())), preferred_element_type=F32)


def _split_bf16(x):
    hi = x.astype(BF16)
    lo = (x - hi.astype(F32)).astype(BF16)
    return hi, lo


def _silu(x):
    return x / (1.0 + jnp.exp(-x))


def _adaln_kernel(c_ref, w_ref, b_ref, o_ref):
    ca = _silu(c_ref[...])
    c_hi, c_lo = _split_bf16(ca)
    w_hi, w_lo = _split_bf16(w_ref[...])
    o_ref[...] = _dot(c_hi, w_hi) + _dot(c_lo, w_hi) + _dot(c_hi, w_lo) + b_ref[...]


def _adaln(c_pad, ada_w, ada_b):
    n_out = ada_w.shape[1]
    tn = D_MODEL
    return pl.pallas_call(
        _adaln_kernel,
        grid=(n_out // tn,),
        in_specs=[
            pl.BlockSpec((SUBLANES, D_MODEL), lambda j: (0, 0)),
            pl.BlockSpec((D_MODEL, tn), lambda j: (0, j)),
            pl.BlockSpec((1, tn), lambda j: (0, j)),
        ],
        out_specs=pl.BlockSpec((SUBLANES, tn), lambda j: (0, j)),
        out_shape=jax.ShapeDtypeStruct((SUBLANES, n_out), F32),
        compiler_params=pltpu.CompilerParams(vmem_limit_bytes=VMEM_LIMIT),
        name="adaln",
    )(c_pad, ada_w, ada_b)


def _norm_modulate(x, gain, shift, scale):
    ms = jnp.mean(x * x, axis=-1, keepdims=True)
    y = x * lax.rsqrt(ms + EPS) * gain
    return y * (1.0 + scale) + shift


def _rotary_slab(x, cos, sin_signed, lane_lo):
    swapped = jnp.where(lane_lo, pltpu.roll(x, 96, 1), pltpu.roll(x, 32, 1))
    return x * cos + swapped * sin_signed


def _inproj_kernel(x_ref, mod_ref, gain_ref, w_ref, cos_ref, sin_ref,
                   rq_ref, rk_ref, rv_ref, rg_ref, dq_ref, dk_ref, dvt_ref, *, tiles_per_batch):
    b = pl.program_id(0) // tiles_per_batch
    shift = mod_ref[pl.ds(b, 1), 0:D_MODEL]
    scale = mod_ref[pl.ds(b, 1), D_MODEL:2 * D_MODEL]
    h = _norm_modulate(x_ref[...], gain_ref[...], shift, scale).astype(BF16)
    cos = cos_ref[...]
    sin = sin_ref[...]
    lane = lax.broadcasted_iota(jnp.int32, cos.shape, 1)
    lane_lo = (lane % 64) < 32

    def proj(chunk):
        return _dot(h, w_ref[:, chunk * RET_WIDTH:(chunk + 1) * RET_WIDTH])

    def rotary(acc, out_ref, post_scale):
        for s in range(RET_WIDTH // LANES):
            sl = slice(s * LANES, (s + 1) * LANES)
            out_ref[:, sl] = (_rotary_slab(acc[:, sl], cos, sin, lane_lo) * post_scale).astype(BF16)

    rotary(proj(0), rq_ref, 1.0)
    rotary(proj(1), rk_ref, RET_HEAD_DIM ** -0.5)
    rv_ref[...] = proj(2).astype(BF16)
    rg_ref[...] = _silu(proj(3)).astype(BF16)
    dq_ref[...] = (proj(4) * (DIFF_QK_DIM ** -0.5 * math.log2(math.e))).astype(BF16)
    dk_ref[...] = proj(5).astype(BF16)
    dvt_ref[...] = proj(6).T.astype(BF16)


def _inproj(x2, mod, gain, w_in, cos_t, sin_t, seq, tm):
    tokens = x2.shape[0]
    tiles_per_batch = seq // tm
    tok_spec = lambda w: pl.BlockSpec((tm, w), lambda i: (i, 0))
    tab_spec = pl.BlockSpec((tm, LANES), lambda i: (i % tiles_per_batch, 0))
    full = lambda a: pl.BlockSpec(a.shape, lambda i: (0,) * a.ndim)
    out = jax.ShapeDtypeStruct((tokens, RET_WIDTH), BF16)
    return pl.pallas_call(
        functools.partial(_inproj_kernel, tiles_per_batch=tiles_per_batch),
        grid=(tokens // tm,),
        in_specs=[tok_spec(D_MODEL), full(mod), full(gain), full(w_in), tab_spec, tab_spec],
        out_specs=[tok_spec(RET_WIDTH)] * 6 + [pl.BlockSpec((DIFF_WIDTH, tm), lambda i: (0, i))],
        out_shape=[out] * 6 + [jax.ShapeDtypeStruct((DIFF_WIDTH, tokens), BF16)],
        compiler_params=pltpu.CompilerParams(
            dimension_semantics=("parallel",), vmem_limit_bytes=VMEM_LIMIT),
        name="inproj",
    )(x2, mod, gain, w_in, cos_t, sin_t)


RET_SEQ_GROUP = 4


def _retention_kernel(q_ref, k_ref, v_ref, g_ref, dec_ref, qdec_ref, kdec_ref, rdec_ref,
                      bmask_ref, gmean_ref, gain_ref, o_ref, state_ref, *, chunk):
    @pl.when(pl.program_id(1) == 0)
    def _():
        state_ref[...] = jnp.zeros_like(state_ref)

    lane = lax.broadcasted_iota(jnp.int32, (chunk, LANES), 1)
    first_head = lane < RET_HEAD_DIM
    gmean = gmean_ref[...]
    bmask = bmask_ref[...]
    units = [(s, p) for s in range(q_ref.shape[0]) for p in range(RET_PAIRS)]
    sl = lambda p: slice(p * LANES, (p + 1) * LANES)
    q = [q_ref[s, :, sl(p)] for s, p in units]
    k = [k_ref[s, :, sl(p)] for s, p in units]
    v = [v_ref[s, :, sl(p)] for s, p in units]
    zero = jnp.zeros_like(q[0])
    q_stack = [jnp.concatenate([jnp.where(first_head, qu, zero), jnp.where(first_head, zero, qu)], axis=0)
               for qu in q]
    scores = [(_dot_nt(q_stack[n], k[n]) * dec_ref[p]).astype(BF16) for n, (_, p) in enumerate(units)]
    state = [state_ref[n] for n in range(len(units))]
    cross = [_dot(q[n], state[n].astype(BF16)) * qdec_ref[:, sl(p)] for n, (_, p) in enumerate(units)]
    k_dec = [(k[n].astype(F32) * kdec_ref[:, sl(p)]).astype(BF16) for n, (_, p) in enumerate(units)]
    for n, (_, p) in enumerate(units):
        state_ref[n] = state[n] * rdec_ref[p] + _dot_tn(k_dec[n], v[n]) * bmask
    intra2 = [_dot(scores[n], v[n]) for n in range(len(units))]
    y = [jnp.where(first_head, intra2[n][:chunk], intra2[n][chunk:]) + cross[n] for n in range(len(units))]
    seg_mean = lambda x: _dot(jnp.concatenate(_split_bf16(x), axis=1), gmean)
    mu = [seg_mean(yu) for yu in y]
    d = [yu - mu_u for yu, mu_u in zip(y, mu)]
    var = [seg_mean(du * du) for du in d]
    for n, (s, p) in enumerate(units):
        yn = d[n] * lax.rsqrt(var[n] + EPS) * gain_ref[:, sl(p)]
        o_ref[s, :, sl(p)] = (g_ref[s, :, sl(p)].astype(F32) * yn).astype(BF16)


def _retention_tables(chunk):
    heads = np.arange(RET_HEADS, dtype=np.float64)
    log_gamma = np.log(1.0 - np.exp2(-5.0 - heads))
    idx = np.arange(chunk)
    rel = (idx[:, None] - idx[None, :]).astype(np.float64)
    decay = np.where(rel[None] >= 0, np.exp(log_gamma[:, None, None] * np.maximum(rel, 0.0)[None]), 0.0)
    dec2 = decay.reshape(RET_PAIRS, 2 * chunk, chunk)
    lane_lg = np.repeat(log_gamma, RET_HEAD_DIM)
    qdec = np.exp(lane_lg[None, :] * (idx + 1)[:, None])
    kdec = np.exp(lane_lg[None, :] * (chunk - 1 - idx)[:, None])
    rdec = np.exp(lane_lg * chunk).reshape(RET_PAIRS, LANES, 1) * np.ones((1, 1, LANES))
    blk = np.arange(LANES) // RET_HEAD_DIM
    bmask = (blk[:, None] == blk[None, :]).astype(np.float64)
    f32 = lambda a: jnp.asarray(a.astype(np.float32))
    gmean2 = np.concatenate([bmask, bmask], axis=0) / RET_HEAD_DIM
    return f32(dec2), f32(qdec), f32(kdec), f32(rdec), f32(bmask), f32(gmean2).astype(BF16)


def _retention(rq, rk, rv, rg, gn_gain, batch, seq, chunk):
    nc = seq // chunk
    group = RET_SEQ_GROUP if batch % RET_SEQ_GROUP == 0 else 1
    dec2, qdec, kdec, rdec, bmask, gmean = _retention_tables(chunk)
    tok_spec = pl.BlockSpec((group, chunk, RET_WIDTH), lambda b, n: (b, n, 0))
    full = lambda a: pl.BlockSpec(a.shape, lambda b, n: (0,) * a.ndim)
    by_seq = lambda a: a.reshape(batch, seq, RET_WIDTH)
    out = pl.pallas_call(
        functools.partial(_retention_kernel, chunk=chunk),
        grid=(batch // group, nc),
        in_specs=[tok_spec] * 4 + [full(dec2), full(qdec), full(kdec), full(rdec), full(bmask),
                                   full(gmean), full(gn_gain)],
        out_specs=tok_spec,
        out_shape=jax.ShapeDtypeStruct((batch, seq, RET_WIDTH), BF16),
        scratch_shapes=[pltpu.VMEM((group * RET_PAIRS, LANES, LANES), F32)],
        compiler_params=pltpu.CompilerParams(
            dimension_semantics=("parallel", "arbitrary"), vmem_limit_bytes=VMEM_LIMIT),
        name="retention",
    )(by_seq(rq), by_seq(rk), by_seq(rv), by_seq(rg), dec2, qdec, kdec, rdec, bmask, gmean, gn_gain)
    return out.reshape(batch * seq, RET_WIDTH)


NEG_BIG = -1e30


V_EXT_ROWS = DIFF_V_DIM + BF16_SUBLANES
QUERY_CHUNK = 256
SCORES_AHEAD_FULL = 3
SCORES_AHEAD_DIAG = 3
STAGE_SLOTS = 16


def _diag_chunks(tq, tk, d):
    assert tk == 2 * QUERY_CHUNK
    per_softmax = tq // QUERY_CHUNK
    out = []
    for c in range(2 * per_softmax):
        q0 = (c % per_softmax) * QUERY_CHUNK
        if q0 + QUERY_CHUNK - 1 < d * tk:
            continue
        kind = "full" if q0 >= (d + 1) * tk else ("tri" if q0 == d * tk else "low_tri")
        out.append((c, kind))
    return out


def _accumulate(acc_ref, cs, alpha, pv):
    acc_ref[:, cs] = alpha * acc_ref[:, cs] + pv


def _diffattn_kernel(q_ref, k_ref, vt_ref, lq1_ref, lk1_ref, lq2_ref, lk2_ref, gain_ref, bias_ref, o_ref,
                     qs_ref, vext_ref, m_ref, acc_ref, stage_ref, *, tq, tk, lambda_init):
    i = pl.program_id(2)
    nk = vext_ref.shape[0]

    @pl.when(i == 0)
    def _():
        for j in range(nk):
            vext_ref[j, 0:DIFF_V_DIM, :] = vt_ref[:, j * tk:(j + 1) * tk]
            vext_ref[j, DIFF_V_DIM:V_EXT_ROWS, :] = jnp.ones((V_EXT_ROWS - DIFF_V_DIM, tk), BF16)

    q = q_ref[...]
    lane = lax.broadcasted_iota(jnp.int32, q.shape, 1)
    zero = jnp.zeros_like(q)
    qs_ref[0:tq, :] = jnp.where(lane < DIFF_QK_DIM, q, zero)
    qs_ref[tq:2 * tq, :] = jnp.where(lane < DIFF_QK_DIM, zero, q)
    m_ref[...] = jnp.full_like(m_ref, NEG_BIG)
    acc_ref[...] = jnp.zeros_like(acc_ref)

    def step(work, n_ahead):
        chunk = lambda c: slice(c * QUERY_CHUNK, (c + 1) * QUERY_CHUNK)

        def scores(j, c, kind):
            n_keys = QUERY_CHUNK if kind == "tri" else tk
            start = pl.multiple_of(j * tk, tk)
            return _dot_nt(k_ref[pl.ds(start, n_keys), :], qs_ref[chunk(c), :])

        ahead = [scores(*work[n]) for n in range(min(n_ahead, len(work)))]
        pending = None
        for n, (j, c, kind) in enumerate(work):
            cs = chunk(c)
            st = ahead.pop(0)
            if n + n_ahead < len(work):
                ahead.append(scores(*work[n + n_ahead]))
            slot = n % stage_ref.shape[0]
            n_keys = st.shape[0]
            stage_ref[slot, 0:n_keys, :] = st
            if kind == "full":
                st = stage_ref[slot]
            else:
                causal = stage_ref[slot, n_keys - QUERY_CHUNK:n_keys, :] + bias_ref[...]
                st = causal if kind == "tri" else jnp.concatenate(
                    [stage_ref[slot, 0:n_keys - QUERY_CHUNK, :], causal], axis=0)
            m_old = m_ref[:, cs]
            m_new = jnp.maximum(m_old, jnp.max(st, axis=0, keepdims=True))
            alpha = jnp.exp2(m_old - m_new)
            p = jnp.exp2(st - m_new).astype(BF16)
            m_ref[:, cs] = m_new
            pv = _dot(vext_ref[j, :, 0:st.shape[0]], p)
            if pending is not None:
                pending()
            pending = functools.partial(_accumulate, acc_ref, cs, alpha, pv)
        pending()

    tiles_per_q = tq // tk
    n_chunks = 2 * tq // QUERY_CHUNK

    def full_tiles(it):
        return [(it * tiles_per_q + d, c, "full") for d in range(tiles_per_q) for c in range(n_chunks)]

    lax.fori_loop(0, i, lambda it, c: (step(full_tiles(it), SCORES_AHEAD_FULL), c)[1], 0)
    kind_order = {"full": 0, "low_tri": 1, "tri": 2}
    diag = lambda it: [(it * tiles_per_q + d, c, kind) for d in range(tiles_per_q)
                       for c, kind in sorted(_diag_chunks(tq, tk, d), key=lambda ck: kind_order[ck[1]])]
    lax.fori_loop(i, i + 1, lambda it, c: (step(diag(it), SCORES_AHEAD_DIAG), c)[1], 0)

    lam = (jnp.exp(jnp.sum(lq1_ref[...] * lk1_ref[...], axis=-1, keepdims=True))
           - jnp.exp(jnp.sum(lq2_ref[...] * lk2_ref[...], axis=-1, keepdims=True)) + lambda_init)
    acc = acc_ref[...]
    o2 = acc[0:DIFF_V_DIM, :] * (1.0 / acc[DIFF_V_DIM:DIFF_V_DIM + 1, :])
    ot = o2[:, :tq] - lam * o2[:, tq:]
    ms = jnp.mean(ot * ot, axis=0, keepdims=True)
    ot = ot * lax.rsqrt(ms + EPS) * gain_ref[...] * (1.0 - lambda_init)
    o_ref[...] = ot.T.astype(BF16)


def _diffattn(dq, dk, dvt, lam_q1, lam_k1, lam_q2, lam_k2, gain, batch, seq, lambda_init, tq, tk):
    nq = seq // tq
    q_spec = pl.BlockSpec((tq, LANES), lambda b, h, i: (b * nq + i, h))
    k_spec = pl.BlockSpec((seq, LANES), lambda b, h, i: (b, h))
    vt_spec = pl.BlockSpec((DIFF_V_DIM, seq), lambda b, h, i: (h, b))
    vec = lambda a: pl.BlockSpec(a.shape, lambda b, h, i: (0, 0))
    key = np.arange(QUERY_CHUNK)[:, None]
    query = np.arange(QUERY_CHUNK)[None, :]
    bias = jnp.asarray(np.where(key <= query, 0.0, NEG_BIG), F32)
    return pl.pallas_call(
        functools.partial(_diffattn_kernel, tq=tq, tk=tk, lambda_init=lambda_init),
        grid=(batch, DIFF_HEADS, nq),
        in_specs=[q_spec, k_spec, vt_spec, vec(lam_q1), vec(lam_k1), vec(lam_q2), vec(lam_k2), vec(gain),
                  vec(bias)],
        out_specs=q_spec,
        out_shape=jax.ShapeDtypeStruct(dq.shape, BF16),
        scratch_shapes=[
            pltpu.VMEM((2 * tq, LANES), BF16),
            pltpu.VMEM((seq // tk, V_EXT_ROWS, tk), BF16),
            pltpu.VMEM((1, 2 * tq), F32),
            pltpu.VMEM((V_EXT_ROWS, 2 * tq), F32),
            pltpu.VMEM((STAGE_SLOTS, tk, QUERY_CHUNK), F32),
        ],
        compiler_params=pltpu.CompilerParams(
            dimension_semantics=("parallel", "parallel", "arbitrary"), vmem_limit_bytes=VMEM_LIMIT),
        name="diffattn",
    )(dq, dk, dvt, lam_q1, lam_k1, lam_q2, lam_k2, gain, bias)


def _route(logits):
    r = [logits[g:g + 1, :] for g in range(N_GROUPS)]
    gmax = jnp.maximum(jnp.maximum(r[0], r[1]), jnp.maximum(r[2], r[3]))
    g_idx = jnp.where(r[0] == gmax, 0, jnp.where(r[1] == gmax, 1, jnp.where(r[2] == gmax, 2, 3)))
    denom = sum(jnp.exp(rg - gmax) for rg in r)
    g_weight = 1.0 / denom
    sel = jnp.zeros((EXPERTS_PER_GROUP, logits.shape[1]), F32)
    for g in range(N_GROUPS):
        rows = logits[SUBLANES + g * EXPERTS_PER_GROUP:SUBLANES + (g + 1) * EXPERTS_PER_GROUP, :]
        sel = jnp.where(g_idx == g, rows, sel)
    eidx = lax.broadcasted_iota(jnp.int32, sel.shape, 0)
    v1 = jnp.max(sel, axis=0, keepdims=True)
    i1 = jnp.min(jnp.where(sel == v1, eidx, EXPERTS_PER_GROUP), axis=0, keepdims=True)
    sel2 = jnp.where(eidx == i1, -jnp.inf, sel)
    v2 = jnp.max(sel2, axis=0, keepdims=True)
    i2 = jnp.min(jnp.where(sel2 == v2, eidx, EXPERTS_PER_GROUP), axis=0, keepdims=True)
    e2 = jnp.exp(v2 - v1)
    w1 = g_weight / (1.0 + e2)
    w2 = g_weight * e2 / (1.0 + e2)
    return g_idx, i1, i2, w1, w2


OUTPROJ_PARTS = 2
OUTPROJ_TILES_PER_STEP = 2


def _outproj_kernel(ret_ref, diff_ref, x_ref, mod_ref, gain_ref, wo_ref, wr_ref, br_ref, tri_ref,
                    x1_ref, h2_ref, ri_ref, rw_ref, cnt_ref, *, tiles_per_batch):
    b = pl.program_id(0) // tiles_per_batch
    gate1 = mod_ref[pl.ds(b, 1), 2 * D_MODEL:3 * D_MODEL]
    shift = mod_ref[pl.ds(b, 1), 3 * D_MODEL:4 * D_MODEL]
    scale = mod_ref[pl.ds(b, 1), 4 * D_MODEL:5 * D_MODEL]
    wr = wr_ref[...]
    rows = x_ref.shape[0]
    tm = tri_ref.shape[0]
    n_parts = OUTPROJ_PARTS * rows // tm
    parts = [slice(n * rows // n_parts, (n + 1) * rows // n_parts) for n in range(n_parts)]
    mix = [_dot(jnp.concatenate([ret_ref[r, :], diff_ref[r, :]], axis=1), wo_ref[...]) for r in parts]
    for r, m in zip(parts, mix):
        x1_ref[r, :] = x_ref[r, :] + gate1 * m
    h_split = [_split_bf16(_norm_modulate(x1_ref[r, :], gain_ref[...], shift, scale)) for r in parts]
    for r, (h_hi, _) in zip(parts, h_split):
        h2_ref[r, :] = h_hi
    by_hi = [_dot_nt(wr, h_hi) for h_hi, _ in h_split]
    by_lo = [_dot_nt(wr[:ROUTER_ROWS], h_lo) for _, h_lo in h_split]
    logits = [a[:ROUTER_ROWS] + a[ROUTER_ROWS:] + c + br_ref[...] for a, c in zip(by_hi, by_lo)]
    routed = [_route(lg) for lg in logits]
    g_idx, i1, i2, w1, w2 = [jnp.concatenate([rt[n] for rt in routed], axis=1) for n in range(5)]
    logits = jnp.concatenate(logits, axis=1)
    e1 = g_idx * EXPERTS_PER_GROUP + i1
    e2 = g_idx * EXPERTS_PER_GROUP + i2
    eidx = lax.broadcasted_iota(jnp.int32, (N_EXPERTS, logits.shape[1]), 0)
    hit1 = eidx == e1
    hit2 = eidx == e2
    onehot = jnp.where(hit1 | hit2, 1.0, 0.0)
    tiles = [slice(n * tm, (n + 1) * tm) for n in range(rows // tm)]
    before = jnp.concatenate([_dot(onehot[:, t].astype(BF16), tri_ref[...]) for t in tiles], axis=1)
    r1 = jnp.sum(jnp.where(hit1, before, 0.0), axis=0, keepdims=True)
    r2 = jnp.sum(jnp.where(hit2, before, 0.0), axis=0, keepdims=True)
    zi = jnp.zeros_like(e1)
    ri_ref[...] = jnp.concatenate([e1, e2, r1.astype(jnp.int32), r2.astype(jnp.int32), zi, zi, zi, zi], axis=0)
    zf = jnp.zeros_like(w1)
    rw_ref[...] = jnp.concatenate([w1, w2, zf, zf, zf, zf, zf, zf], axis=0)
    for n, t in enumerate(tiles):
        counts = jnp.sum(onehot[:, t], axis=1, keepdims=True)
        cnt_ref[n] = jnp.broadcast_to(counts, (N_EXPERTS, LANES)).astype(jnp.int32)


def _outproj(ret_out, diff_out, x2, mod, gain, w_out, wr, br, seq, tm):
    tokens = x2.shape[0]
    n_tiles = tokens // tm
    per_step = OUTPROJ_TILES_PER_STEP if seq % (OUTPROJ_TILES_PER_STEP * tm) == 0 else 1
    rows = per_step * tm
    tri = jnp.asarray(np.arange(tm)[:, None] < np.arange(tm)[None, :], BF16)
    tok_spec = lambda w: pl.BlockSpec((rows, w), lambda i: (i, 0))
    row_spec = pl.BlockSpec((SUBLANES, rows), lambda i: (0, i))
    full = lambda a: pl.BlockSpec(a.shape, lambda i: (0,) * a.ndim)
    return pl.pallas_call(
        functools.partial(_outproj_kernel, tiles_per_batch=seq // rows),
        grid=(tokens // rows,),
        in_specs=[tok_spec(RET_WIDTH), tok_spec(DIFF_WIDTH), tok_spec(D_MODEL), full(mod), full(gain),
                  full(w_out), full(wr), full(br), full(tri)],
        out_specs=[tok_spec(D_MODEL), tok_spec(D_MODEL), row_spec, row_spec,
                   pl.BlockSpec((per_step, N_EXPERTS, LANES), lambda i: (i, 0, 0))],
        out_shape=[jax.ShapeDtypeStruct((tokens, D_MODEL), F32),
                   jax.ShapeDtypeStruct((tokens, D_MODEL), BF16),
                   jax.ShapeDtypeStruct((SUBLANES, tokens), jnp.int32),
                   jax.ShapeDtypeStruct((SUBLANES, tokens), F32),
                   jax.ShapeDtypeStruct((n_tiles, N_EXPERTS, LANES), jnp.int32)],
        compiler_params=pltpu.CompilerParams(
            dimension_semantics=("parallel",), vmem_limit_bytes=VMEM_LIMIT),
        name="outproj",
    )(ret_out, diff_out, x2, mod, gain, w_out, wr, br, tri)


CHUNK = 8
BIG_PIECE = 4 * CHUNK
MAX_SMALL_PIECES = N_EXPERTS * (BIG_PIECE // CHUNK - 1)
PIECE_TABLES = ("n_big", "big_src", "big_dst", "n_small", "small_src", "small_dst")
DISPATCH_TILES_PER_STEP = 2
COMBINE_TILES_PER_STEP = 2
TMX = 512
IN_SLOTS = 3
OUT_SLOTS = 2


def _local_rows(tm):
    rows = 2 * tm + N_EXPERTS * (CHUNK - 1)
    return pl.cdiv(rows, BF16_SUBLANES) * BF16_SUBLANES


def _max_big_pieces(tm):
    return _local_rows(tm) // BIG_PIECE


def _sorted_rows_alloc(tokens, tm):
    worst = 2 * tokens + (tokens // tm) * N_EXPERTS * (CHUNK - 1) + N_EXPERTS * (TMX - CHUNK)
    return (worst + TMX - 1) // TMX * TMX


def _dispatch_plan(cnt, tokens, tm):
    i32 = jnp.int32
    pad = (cnt + CHUNK - 1) // CHUNK * CHUNK
    local_end = jnp.cumsum(pad, axis=1)
    local_start = local_end - pad
    seg_rows = jnp.sum(pad, axis=0)
    seg_pad = (seg_rows + TMX - 1) // TMX * TMX
    seg_end = jnp.cumsum(seg_pad)
    seg_start = seg_end - seg_pad
    run_dst = seg_start[None, :] + jnp.cumsum(pad, axis=0) - pad

    def pieces(count, offset, size, max_n):
        end = jnp.cumsum(count, axis=1)
        start = end - count
        k = jnp.arange(max_n, dtype=i32)[None, :, None]
        owns = (start[:, None, :] <= k) & (k < end[:, None, :])
        within = size * (k - start[:, None, :]) + offset[:, None, :]
        src = jnp.sum(jnp.where(owns, local_start[:, None, :] + within, 0), axis=-1)
        dst = jnp.sum(jnp.where(owns, run_dst[:, None, :] + within, 0), axis=-1)
        return end[:, -1].astype(i32), src.reshape(-1).astype(i32), dst.reshape(-1).astype(i32)

    n_big, big_src, big_dst = pieces(pad // BIG_PIECE, jnp.zeros_like(pad), BIG_PIECE, _max_big_pieces(tm))
    n_small, small_src, small_dst = pieces(pad % BIG_PIECE // CHUNK, pad // BIG_PIECE * BIG_PIECE, CHUNK,
                                           MAX_SMALL_PIECES)
    m = TMX * jnp.arange(_sorted_rows_alloc(tokens, tm) // TMX, dtype=i32)
    tile_expert = jnp.minimum(jnp.sum(seg_end[None, :] <= m[:, None], axis=-1), N_EXPERTS - 1)
    towns = (seg_start[None, :] <= m[:, None]) & (m[:, None] < seg_end[None, :])
    used = seg_pad > 0
    parity = (jnp.cumsum(used) - used) % 2
    eids = jnp.arange(N_EXPERTS, dtype=i32)
    later_used = (eids[None, :] > eids[:, None]) & used[None, :]
    next_used = jnp.min(jnp.where(later_used, eids[None, :], N_EXPERTS), axis=1)
    next_used = jnp.where(next_used == N_EXPERTS, -1, next_used)
    pick = lambda per_expert: jnp.sum(jnp.where(towns, per_expert[None, :], 0), axis=-1)
    tile_first = jnp.sum(jnp.where(towns & (seg_start[None, :] == m[:, None]), 1, 0), axis=-1)
    tile_next = jnp.where(jnp.any(towns, axis=-1), pick(next_used), -1)
    return dict(
        tile_first=tile_first.astype(i32),
        tile_slot=pick(parity).astype(i32),
        tile_next=tile_next.astype(i32),
        tile_rows=jnp.clip(pick(seg_start + seg_rows) - m, 0, TMX).astype(i32),
        local_start=local_start.reshape(-1).astype(i32),
        n_big=n_big, big_src=big_src, big_dst=big_dst,
        n_small=n_small, small_src=small_src, small_dst=small_dst,
        tail_base=(seg_start + seg_rows).astype(i32),
        tail_rows=(seg_pad - seg_rows).astype(i32),
        tile_expert=tile_expert.astype(i32),
        n_used=(seg_end[-1:] // TMX).astype(i32),
    )


WAIT_UNROLL = 8


def _wait_times(copy, n):
    lax.fori_loop(0, n // WAIT_UNROLL, lambda i, c: ([copy.wait() for _ in range(WAIT_UNROLL)], c)[1], 0)
    lax.fori_loop(0, n % WAIT_UNROLL, lambda i, c: (copy.wait(), c)[1], 0)


def _for_each(n, body, unroll=4):
    main = n // unroll
    lax.fori_loop(0, main, lambda i, c: ([body(i * unroll + u) for u in range(unroll)], c)[1], 0)
    lax.fori_loop(main * unroll, n, lambda j, c: (body(j), c)[1], 0)


def _local_slots(ri_ref, local_start_ref, tile, cols=slice(None)):
    e1, e2 = ri_ref[0:1, cols], ri_ref[1:2, cols]
    s1, s2 = ri_ref[2:3, cols], ri_ref[3:4, cols]
    for e in range(N_EXPERTS):
        start = local_start_ref[tile * N_EXPERTS + e]
        s1 = s1 + jnp.where(e1 == e, start, 0)
        s2 = s2 + jnp.where(e2 == e, start, 0)
    return s1, s2


def _run_pieces(piece_refs, local_ref, sorted_ref, sem_ref, to_sorted):
    n_big_ref, big_src_ref, big_dst_ref, n_small_ref, small_src_ref, small_dst_ref = piece_refs
    r_loc = local_ref.shape[1]
    kinds = [(BIG_PIECE, n_big_ref, big_src_ref, big_dst_ref, r_loc // BIG_PIECE),
             (CHUNK, n_small_ref, small_src_ref, small_dst_ref, MAX_SMALL_PIECES)]

    def copy(sl, size, local_row, sorted_row):
        local = local_ref.at[sl, pl.ds(pl.multiple_of(local_row, CHUNK), size), :]
        srt = sorted_ref.at[pl.ds(pl.multiple_of(sorted_row, CHUNK), size), :]
        return pltpu.make_async_copy(local, srt, sem_ref.at[sl]) if to_sorted else \
            pltpu.make_async_copy(srt, local, sem_ref.at[sl])

    def start(tile, sl):
        for size, n_ref, src_ref, dst_ref, max_n in kinds:
            _for_each(n_ref[tile], lambda k: copy(sl, size, src_ref[tile * max_n + k], dst_ref[tile * max_n + k])
                      .start())

    def wait(tile, sl):
        for size, n_ref, _, _, _ in kinds:
            _wait_times(copy(sl, size, 0, 0), n_ref[tile])

    return start, wait


def _dispatch_kernel(local_start_ref, n_big_ref, big_src_ref, big_dst_ref, n_small_ref, small_src_ref, small_dst_ref,
                     tail_base_ref, tail_rows_ref, n_used_ref,
                     h_ref, ri_ref, xs_ref, buf_ref, zero_ref, sem_ref, tail_sem_ref, *, r_loc):
    b = pl.program_id(0)
    nb = pl.num_programs(0)
    per_step = buf_ref.shape[0] // 2
    tm = h_ref.shape[0] // per_step
    gen = b % 2
    start_runs, drain_tile = _run_pieces(
        (n_big_ref, big_src_ref, big_dst_ref, n_small_ref, small_src_ref, small_dst_ref), buf_ref, xs_ref, sem_ref,
        to_sorted=True)

    def drain(step, g):
        for t in range(per_step):
            drain_tile(step * per_step + t, g * per_step + t)

    @pl.when(b >= 2)
    def _():
        drain(b - 2, gen)

    tiles = [b * per_step + t for t in range(per_step)]
    cols = [slice(t * tm, (t + 1) * tm) for t in range(per_step)]
    slots = [_local_slots(ri_ref, local_start_ref, tile, c) for tile, c in zip(tiles, cols)]
    rows = lax.broadcasted_iota(jnp.int32, (r_loc, tm), 0)
    perms = [jnp.where((rows == s1) | (rows == s2), 1.0, 0.0).astype(BF16) for s1, s2 in slots]
    for t, (perm, c) in enumerate(zip(perms, cols)):
        buf_ref[gen * per_step + t] = _dot(perm, h_ref[c, :])
    for t, tile in enumerate(tiles):
        start_runs(tile, gen * per_step + t)

    def tail_pieces(e, act):
        n = tail_rows_ref[e]
        size = TMX // 2
        while size >= CHUNK:
            dst = pl.multiple_of(tail_base_ref[e] + (n & (-2 * size)), CHUNK)
            cp = pltpu.make_async_copy(zero_ref.at[pl.ds(0, size), :], xs_ref.at[pl.ds(dst, size), :],
                                       tail_sem_ref.at[0])
            pl.when((n & size) != 0)(functools.partial(act, cp))
            size //= 2

    def unused_tile_copy(m):
        dst = pl.multiple_of(m * TMX, TMX)
        return pltpu.make_async_copy(zero_ref, xs_ref.at[pl.ds(dst, TMX), :], tail_sem_ref.at[1])

    n_alloc = xs_ref.shape[0] // TMX

    @pl.when(b == 0)
    def _():
        zero_ref[...] = jnp.zeros_like(zero_ref)

    experts_per_step = pl.cdiv(N_EXPERTS, nb)
    lax.fori_loop(jnp.minimum(b * experts_per_step, N_EXPERTS), jnp.minimum((b + 1) * experts_per_step, N_EXPERTS),
                  lambda e, c: (tail_pieces(e, lambda cp: cp.start()), c)[1], 0)
    tiles_per_step = pl.cdiv(n_alloc, nb)
    first_unused = n_used_ref[0]
    lax.fori_loop(jnp.minimum(first_unused + b * tiles_per_step, n_alloc),
                  jnp.minimum(first_unused + (b + 1) * tiles_per_step, n_alloc),
                  lambda m, c: (unused_tile_copy(m).start(), c)[1], 0)

    @pl.when(b == nb - 1)
    def _():
        lax.fori_loop(0, N_EXPERTS, lambda e, c: (tail_pieces(e, lambda cp: cp.wait()), c)[1], 0)
        lax.fori_loop(n_used_ref[0], n_alloc, lambda m, c: (unused_tile_copy(m).wait(), c)[1], 0)

        @pl.when(b >= 1)
        def _():
            drain(b - 1, 1 - gen)

        drain(b, gen)


def _dispatch(h2, ri, plan, tm):
    tokens = h2.shape[0]
    r_loc = _local_rows(tm)
    prefetch = [plan["local_start"]] + [plan[k] for k in PIECE_TABLES] + [
        plan["tail_base"], plan["tail_rows"], plan["n_used"]]
    per_step = DISPATCH_TILES_PER_STEP if (tokens // tm) % DISPATCH_TILES_PER_STEP == 0 else 1
    rows = per_step * tm
    grid_spec = pltpu.PrefetchScalarGridSpec(
        num_scalar_prefetch=len(prefetch),
        grid=(tokens // rows,),
        in_specs=[pl.BlockSpec((rows, D_MODEL), lambda i, *_: (i, 0)),
                  pl.BlockSpec((SUBLANES, rows), lambda i, *_: (0, i))],
        out_specs=pl.BlockSpec(memory_space=pl.ANY),
        scratch_shapes=[pltpu.VMEM((2 * per_step, r_loc, D_MODEL), F32), pltpu.VMEM((TMX, D_MODEL), F32),
                        pltpu.SemaphoreType.DMA((2 * per_step,)), pltpu.SemaphoreType.DMA((2,))],
    )
    return pl.pallas_call(
        functools.partial(_dispatch_kernel, r_loc=r_loc),
        grid_spec=grid_spec,
        out_shape=jax.ShapeDtypeStruct((_sorted_rows_alloc(tokens, tm), D_MODEL), F32),
        compiler_params=pltpu.CompilerParams(
            dimension_semantics=("arbitrary",), vmem_limit_bytes=VMEM_LIMIT),
        name="dispatch",
    )(*prefetch, h2, ri)


def _experts_kernel(tile_expert_ref, n_used_ref, first_ref, slot_ref, next_ref, rows_ref, xs_hbm, wg_hbm, wu_hbm, wd_hbm,
                    ys_hbm, wg_st, wu_st, wd_st, wg_bf, wu_bf, wd_bf, a_ref, u_ref, xin_ref, yout_ref,
                    sem_ref, in_sem_ref, out_sem_ref):
    m = pl.program_id(0)
    n_used = n_used_ref[0]

    def weight_copies(e, s):
        return [pltpu.make_async_copy(src.at[e], dst.at[s], sem_ref.at[s, n])
                for n, (src, dst) in enumerate([(wg_hbm, wg_st), (wu_hbm, wu_st), (wd_hbm, wd_st)])]

    def in_copy(t):
        s = t % IN_SLOTS
        return pltpu.make_async_copy(xs_hbm.at[pl.ds(pl.multiple_of(t * TMX, TMX), TMX), :], xin_ref.at[s],
                                     in_sem_ref.at[s])

    def out_copy(t):
        s = t % OUT_SLOTS
        return pltpu.make_async_copy(yout_ref.at[s], ys_hbm.at[pl.ds(pl.multiple_of(t * TMX, TMX), TMX), :],
                                     out_sem_ref.at[s])

    @pl.when(m == 0)
    def _():
        for t in range(IN_SLOTS - 1):
            pl.when(t < n_used)(in_copy(t).start)

    @pl.when(m < n_used)
    def _():
        @pl.when(m + IN_SLOTS - 1 < n_used)
        def _():
            in_copy(m + IN_SLOTS - 1).start()

        @pl.when(first_ref[m] == 1)
        def _():
            s = slot_ref[m]

            @pl.when(m == 0)
            def _():
                for cp in weight_copies(tile_expert_ref[0], 0):
                    cp.start()

            for cp in weight_copies(tile_expert_ref[m], s):
                cp.wait()

            @pl.when(next_ref[m] >= 0)
            def _():
                for cp in weight_copies(next_ref[m], 1 - s):
                    cp.start()

            wg_bf[...] = wg_st[s].astype(BF16)
            wu_bf[...] = wu_st[s].astype(BF16)
            wd_bf[...] = wd_st[s].astype(BF16)

        xs_ref = xin_ref.at[m % IN_SLOTS]
        ys_ref = yout_ref.at[m % OUT_SLOTS]
        in_copy(m).wait()

        @pl.when(m >= OUT_SLOTS)
        def _():
            out_copy(m - OUT_SLOTS).wait()

        def mlp(rows):
            x = xs_ref[rows, :].astype(BF16)
            a_ref[rows, :] = _dot(x, wg_bf[...])
            u_ref[rows, :] = _dot(x, wu_bf[...])
            hid = (_silu(a_ref[rows, :]) * u_ref[rows, :]).astype(BF16)
            ys_ref[rows, :] = _dot(hid, wd_bf[...])

        half = TMX // 2

        @pl.when(rows_ref[m] > half)
        def _():
            mlp(slice(0, TMX))

        @pl.when(rows_ref[m] <= half)
        def _():
            mlp(slice(0, half))
            ys_ref[half:, :] = jnp.zeros((TMX - half, D_MODEL), F32)

        out_copy(m).start()

        @pl.when(m == n_used - 1)
        def _():
            for back in range(OUT_SLOTS):
                pl.when(m - back >= 0)(out_copy(m - back).wait)


def _experts(xs, plan, wg, wu, wd):
    n_tiles = xs.shape[0] // TMX
    hbm = pl.BlockSpec(memory_space=pl.ANY)
    up_shape, down_shape = (D_MODEL, D_EXPERT), (D_EXPERT, D_MODEL)
    grid_spec = pltpu.PrefetchScalarGridSpec(
        num_scalar_prefetch=6,
        grid=(n_tiles,),
        in_specs=[hbm, hbm, hbm, hbm],
        out_specs=hbm,
        scratch_shapes=[pltpu.VMEM((2,) + up_shape, F32), pltpu.VMEM((2,) + up_shape, F32),
                        pltpu.VMEM((2,) + down_shape, F32),
                        pltpu.VMEM(up_shape, BF16), pltpu.VMEM(up_shape, BF16), pltpu.VMEM(down_shape, BF16),
                        pltpu.VMEM((TMX, D_EXPERT), F32), pltpu.VMEM((TMX, D_EXPERT), F32),
                        pltpu.VMEM((IN_SLOTS, TMX, D_MODEL), F32), pltpu.VMEM((OUT_SLOTS, TMX, D_MODEL), F32),
                        pltpu.SemaphoreType.DMA((2, 3)), pltpu.SemaphoreType.DMA((IN_SLOTS,)),
                        pltpu.SemaphoreType.DMA((OUT_SLOTS,))],
    )
    return pl.pallas_call(
        _experts_kernel,
        grid_spec=grid_spec,
        out_shape=jax.ShapeDtypeStruct(xs.shape, F32),
        input_output_aliases={6: 0},
        compiler_params=pltpu.CompilerParams(
            dimension_semantics=("arbitrary",), vmem_limit_bytes=VMEM_LIMIT),
        name="experts",
    )(plan["tile_expert"], plan["n_used"], plan["tile_first"], plan["tile_slot"], plan["tile_next"], plan["tile_rows"],
      xs, wg, wu, wd)


def _combine_kernel(local_start_ref, n_big_ref, big_src_ref, big_dst_ref, n_small_ref, small_src_ref, small_dst_ref,
                    ys_ref, ri_ref, rw_ref, x1_ref, mod_ref, gain_ref, o_ref, buf_ref, sem_ref,
                    *, r_loc, tiles_per_batch):
    b = pl.program_id(0)
    nb = pl.num_programs(0)
    per_step = buf_ref.shape[0] // 2
    tm = x1_ref.shape[0] // per_step
    gen = b % 2
    fetch_tile, wait_tile = _run_pieces(
        (n_big_ref, big_src_ref, big_dst_ref, n_small_ref, small_src_ref, small_dst_ref), buf_ref, ys_ref, sem_ref,
        to_sorted=False)

    def fetch(step, g):
        for t in range(per_step):
            fetch_tile(step * per_step + t, g * per_step + t)

    @pl.when(b == 0)
    def _():
        buf_ref[...] = jnp.zeros_like(buf_ref)
        fetch(0, 0)

    @pl.when(b + 1 < nb)
    def _():
        fetch(b + 1, 1 - gen)

    for t in range(per_step):
        wait_tile(b * per_step + t, gen * per_step + t)

    tiles = [b * per_step + t for t in range(per_step)]
    cols = [slice(t * tm, (t + 1) * tm) for t in range(per_step)]
    slots = [_local_slots(ri_ref, local_start_ref, tile, c) for tile, c in zip(tiles, cols)]
    rows = lax.broadcasted_iota(jnp.int32, (r_loc, tm), 0)
    hits = [(rows == s1, rows == s2) for s1, s2 in slots]
    w_rows = [jnp.sum(jnp.where(h1, rw_ref[0:1, c], jnp.where(h2, rw_ref[1:2, c], 0.0)), axis=1, keepdims=True)
              for (h1, h2), c in zip(hits, cols)]
    perms = [jnp.where(h1 | h2, 1.0, 0.0).astype(BF16) for h1, h2 in hits]
    yws = [(buf_ref[gen * per_step + t] * w_rows[t]).astype(BF16) for t in range(per_step)]
    moes = [_dot_tn(perm, yw) for perm, yw in zip(perms, yws)]
    batch = b // tiles_per_batch
    gate2 = mod_ref[pl.ds(batch, 1), 5 * D_MODEL:6 * D_MODEL]
    for c, moe in zip(cols, moes):
        x2 = x1_ref[c, :] + gate2 * moe
        ms = jnp.mean(x2 * x2, axis=-1, keepdims=True)
        o_ref[c, :] = x2 * lax.rsqrt(ms + EPS) * gain_ref[...]


def _combine(ys, ri, rw, x1, mod, gain, plan, seq, tm):
    tokens = x1.shape[0]
    r_loc = _local_rows(tm)
    per_step = COMBINE_TILES_PER_STEP if seq % (COMBINE_TILES_PER_STEP * tm) == 0 else 1
    rows = per_step * tm
    row_spec = pl.BlockSpec((SUBLANES, rows), lambda i, *_: (0, i))
    tok_spec = pl.BlockSpec((rows, D_MODEL), lambda i, *_: (i, 0))
    full = lambda a: pl.BlockSpec(a.shape, lambda i, *_: (0,) * a.ndim)
    prefetch = [plan["local_start"]] + [plan[k] for k in PIECE_TABLES]
    grid_spec = pltpu.PrefetchScalarGridSpec(
        num_scalar_prefetch=len(prefetch),
        grid=(tokens // rows,),
        in_specs=[pl.BlockSpec(memory_space=pl.ANY), row_spec, row_spec, tok_spec, full(mod), full(gain)],
        out_specs=tok_spec,
        scratch_shapes=[pltpu.VMEM((2 * per_step, r_loc, D_MODEL), F32), pltpu.SemaphoreType.DMA((2 * per_step,))],
    )
    return pl.pallas_call(
        functools.partial(_combine_kernel, r_loc=r_loc, tiles_per_batch=seq // rows),
        grid_spec=grid_spec,
        out_shape=jax.ShapeDtypeStruct((tokens, D_MODEL), F32),
        compiler_params=pltpu.CompilerParams(
            dimension_semantics=("arbitrary",), vmem_limit_bytes=VMEM_LIMIT),
        name="combine",
    )(*prefetch, ys, ri, rw, x1, mod, gain)


def _rotary_tables(seq):
    half = RET_HEAD_DIM // 2
    inv_freq = 1.0 / (ROPE_BASE ** (np.arange(half, dtype=np.float64) / half))
    ang = np.arange(seq, dtype=np.float64)[:, None] * inv_freq[None, :]
    cos = np.cos(ang)
    sin = np.sin(ang)
    f32 = lambda a: jnp.asarray(a.astype(np.float32))
    return f32(np.tile(cos, (1, 4))), f32(np.concatenate([-sin, sin, -sin, sin], axis=1))


def _pick_tile(n, pref):
    t = min(n, pref)
    assert n % t == 0, (n, t)
    return t


def kernel(x, c, ada_w, ada_b, norm1_gain, norm2_gain, w_in, w_out, ret_gn_gain, lam_q1, lam_k1, lam_q2,
           lam_k2, diff_subln_gain, w_group, b_group, w_expert, b_expert, w_gate, w_up, w_down, final_gain):
    batch, seq, d = x.shape
    assert d == D_MODEL and batch <= 8 and ada_w.shape[0] == 1
    layer = 0
    lambda_init = 0.8 - 0.6 * math.exp(-0.3 * layer)
    tokens = batch * seq
    x2 = x.reshape(tokens, d)
    tm = _pick_tile(seq, TOKEN_TILE)

    c_pad = jnp.zeros((SUBLANES, d), F32).at[:batch].set(c)
    mod = _adaln(c_pad, ada_w[layer], ada_b[layer].reshape(1, -1))

    cos_t, sin_t = _rotary_tables(seq)
    rq, rk, rv, rg, dq, dk, dvt = _inproj(
        x2, mod, norm1_gain[layer].reshape(1, d), w_in[layer].astype(BF16), cos_t, sin_t, seq, tm)

    ret_out = _retention(rq, rk, rv, rg, ret_gn_gain[layer].reshape(1, RET_WIDTH), batch, seq,
                         _pick_tile(seq, RET_CHUNK))
    diff_out = _diffattn(
        dq, dk, dvt, lam_q1[layer].reshape(1, -1), lam_k1[layer].reshape(1, -1), lam_q2[layer].reshape(1, -1),
        lam_k2[layer].reshape(1, -1), diff_subln_gain[layer].reshape(-1, 1), batch, seq, lambda_init,
        _pick_tile(seq, ATTN_Q_TILE), 2 * QUERY_CHUNK)

    w_router = jnp.concatenate(
        [w_group[layer].T, jnp.zeros((SUBLANES - N_GROUPS, d), F32), w_expert[layer].reshape(d, N_EXPERTS).T], axis=0)
    b_router = jnp.concatenate(
        [b_group[layer], jnp.zeros((SUBLANES - N_GROUPS,), F32), b_expert[layer].reshape(N_EXPERTS)]).reshape(-1, 1)
    wr_hi = w_router.astype(BF16)
    wr_lo = (w_router - wr_hi.astype(F32)).astype(BF16)
    x1, h2, ri, rw, cnt = _outproj(ret_out, diff_out, x2, mod, norm2_gain[layer].reshape(1, d),
                                   w_out[layer].astype(BF16), jnp.concatenate([wr_hi, wr_lo], axis=0), b_router,
                                   seq, tm)

    plan = _dispatch_plan(cnt[:, :, 0], tokens, tm)
    xs = _dispatch(h2, ri, plan, tm)
    ys = _experts(xs, plan, w_gate[layer].reshape(N_EXPERTS, d, D_EXPERT),
                  w_up[layer].reshape(N_EXPERTS, d, D_EXPERT), w_down[layer].reshape(N_EXPERTS, D_EXPERT, d))
    out = _combine(ys, ri, rw, x1, mod, final_gain.reshape(1, d), plan, seq, tm)
    return out.reshape(batch, seq, d)
```

```python
import functools
import math

import jax
import jax.numpy as jnp
import numpy as np
from jax import lax
from jax.experimental import pallas as pl
from jax.experimental.pallas import tpu as pltpu

F32 = jnp.float32
BF16 = jnp.bfloat16

D_MODEL = 1024
RET_HEAD_DIM = 64
RET_WIDTH = 512
RET_HEADS = 8
RET_PAIRS = RET_HEADS // 2
DIFF_QK_DIM = 64
DIFF_V_DIM = 128
DIFF_HEADS = 4
DIFF_WIDTH = 512
N_GROUPS = 4
EXPERTS_PER_GROUP = 8
N_EXPERTS = N_GROUPS * EXPERTS_PER_GROUP
D_EXPERT = 512
N_MOD = 6
ROPE_BASE = 10000.0
EPS = 1e-6
LANES = 128
SUBLANES = 8
BF16_SUBLANES = 2 * SUBLANES
ROUTER_ROWS = SUBLANES + N_EXPERTS
VMEM_LIMIT = 56 * 1024 * 1024
TOKEN_TILE = 512
RET_CHUNK = 128
ATTN_Q_TILE = 2048


def _dot(a, b):
    return jnp.dot(a, b, preferred_element_type=F32)


def _dot_nt(a, b):
    return lax.dot_general(a, b, (((1,), (1,)), ((), ())), preferred_element_type=F32)


def _dot_tn(a, b):
    return lax.dot_general(a, b, (((0,), (0,)), ((), ())), preferred_element_type=F32)


def _split_bf16(x):
    hi = x.astype(BF16)
    lo = (x - hi.astype(F32)).astype(BF16)
    return hi, lo


def _silu(x):
    return x / (1.0 + jnp.exp(-x))


def _adaln_kernel(c_ref, w_ref, b_ref, o_ref):
    ca = _silu(c_ref[...])
    c_hi, c_lo = _split_bf16(ca)
    w_hi, w_lo = _split_bf16(w_ref[...])
    o_ref[...] = _dot(c_hi, w_hi) + _dot(c_lo, w_hi) + _dot(c_hi, w_lo) + b_ref[...]


def _adaln(c_pad, ada_w, ada_b):
    n_out = ada_w.shape[1]
    tn = D_MODEL
    return pl.pallas_call(
        _adaln_kernel,
        grid=(n_out // tn,),
        in_specs=[
            pl.BlockSpec((SUBLANES, D_MODEL), lambda j: (0, 0)),
            pl.BlockSpec((D_MODEL, tn), lambda j: (0, j)),
            pl.BlockSpec((1, tn), lambda j: (0, j)),
        ],
        out_specs=pl.BlockSpec((SUBLANES, tn), lambda j: (0, j)),
        out_shape=jax.ShapeDtypeStruct((SUBLANES, n_out), F32),
        compiler_params=pltpu.CompilerParams(vmem_limit_bytes=VMEM_LIMIT),
        name="adaln",
    )(c_pad, ada_w, ada_b)


def _norm_modulate(x, gain, shift, scale):
    ms = jnp.mean(x * x, axis=-1, keepdims=True)
    y = x * lax.rsqrt(ms + EPS) * gain
    return y * (1.0 + scale) + shift


def _rotary_slab(x, cos, sin_signed, lane_lo):
    swapped = jnp.where(lane_lo, pltpu.roll(x, 96, 1), pltpu.roll(x, 32, 1))
    return x * cos + swapped * sin_signed


def _inproj_kernel(x_ref, mod_ref, gain_ref, w_ref, cos_ref, sin_ref,
                   rq_ref, rk_ref, rv_ref, rg_ref, dq_ref, dk_ref, dvt_ref, *, tiles_per_batch):
    b = pl.program_id(0) // tiles_per_batch
    shift = mod_ref[pl.ds(b, 1), 0:D_MODEL]
    scale = mod_ref[pl.ds(b, 1), D_MODEL:2 * D_MODEL]
    h = _norm_modulate(x_ref[...], gain_ref[...], shift, scale).astype(BF16)
    cos = cos_ref[...]
    sin = sin_ref[...]
    lane = lax.broadcasted_iota(jnp.int32, cos.shape, 1)
    lane_lo = (lane % 64) < 32

    def proj(chunk):
        return _dot(h, w_ref[:, chunk * RET_WIDTH:(chunk + 1) * RET_WIDTH])

    def rotary(acc, out_ref, post_scale):
        for s in range(RET_WIDTH // LANES):
            sl = slice(s * LANES, (s + 1) * LANES)
            out_ref[:, sl] = (_rotary_slab(acc[:, sl], cos, sin, lane_lo) * post_scale).astype(BF16)

    rotary(proj(0), rq_ref, 1.0)
    rotary(proj(1), rk_ref, RET_HEAD_DIM ** -0.5)
    rv_ref[...] = proj(2).astype(BF16)
    rg_ref[...] = _silu(proj(3)).astype(BF16)
    dq_ref[...] = (proj(4) * (DIFF_QK_DIM ** -0.5 * math.log2(math.e))).astype(BF16)
    dk_ref[...] = proj(5).astype(BF16)
    dvt_ref[...] = proj(6).T.astype(BF16)


def _inproj(x2, mod, gain, w_in, cos_t, sin_t, seq, tm):
    tokens = x2.shape[0]
    tiles_per_batch = seq // tm
    tok_spec = lambda w: pl.BlockSpec((tm, w), lambda i: (i, 0))
    tab_spec = pl.BlockSpec((tm, LANES), lambda i: (i % tiles_per_batch, 0))
    full = lambda a: pl.BlockSpec(a.shape, lambda i: (0,) * a.ndim)
    out = jax.ShapeDtypeStruct((tokens, RET_WIDTH), BF16)
    return pl.pallas_call(
        functools.partial(_inproj_kernel, tiles_per_batch=tiles_per_batch),
        grid=(tokens // tm,),
        in_specs=[tok_spec(D_MODEL), full(mod), full(gain), full(w_in), tab_spec, tab_spec],
        out_specs=[tok_spec(RET_WIDTH)] * 6 + [pl.BlockSpec((DIFF_WIDTH, tm), lambda i: (0, i))],
        out_shape=[out] * 6 + [jax.ShapeDtypeStruct((DIFF_WIDTH, tokens), BF16)],
        compiler_params=pltpu.CompilerParams(
            dimension_semantics=("parallel",), vmem_limit_bytes=VMEM_LIMIT),
        name="inproj",
    )(x2, mod, gain, w_in, cos_t, sin_t)


RET_SEQ_GROUP = 4


def _retention_kernel(q_ref, k_ref, v_ref, g_ref, dec_ref, qdec_ref, kdec_ref, rdec_ref,
                      bmask_ref, gmean_ref, gain_ref, o_ref, state_ref, *, chunk):
    @pl.when(pl.program_id(1) == 0)
    def _():
        state_ref[...] = jnp.zeros_like(state_ref)

    lane = lax.broadcasted_iota(jnp.int32, (chunk, LANES), 1)
    first_head = lane < RET_HEAD_DIM
    gmean = gmean_ref[...]
    bmask = bmask_ref[...]
    units = [(s, p) for s in range(q_ref.shape[0]) for p in range(RET_PAIRS)]
    sl = lambda p: slice(p * LANES, (p + 1) * LANES)
    q = [q_ref[s, :, sl(p)] for s, p in units]
    k = [k_ref[s, :, sl(p)] for s, p in units]
    v = [v_ref[s, :, sl(p)] for s, p in units]
    zero = jnp.zeros_like(q[0])
    q_stack = [jnp.concatenate([jnp.where(first_head, qu, zero), jnp.where(first_head, zero, qu)], axis=0)
               for qu in q]
    scores = [(_dot_nt(q_stack[n], k[n]) * dec_ref[p]).astype(BF16) for n, (_, p) in enumerate(units)]
    state = [state_ref[n] for n in range(len(units))]
    cross = [_dot(q[n], state[n].astype(BF16)) * qdec_ref[:, sl(p)] for n, (_, p) in enumerate(units)]
    k_dec = [(k[n].astype(F32) * kdec_ref[:, sl(p)]).astype(BF16) for n, (_, p) in enumerate(units)]
    for n, (_, p) in enumerate(units):
        state_ref[n] = state[n] * rdec_ref[p] + _dot_tn(k_dec[n], v[n]) * bmask
    intra2 = [_dot(scores[n], v[n]) for n in range(len(units))]
    y = [jnp.where(first_head, intra2[n][:chunk], intra2[n][chunk:]) + cross[n] for n in range(len(units))]
    seg_mean = lambda x: _dot(jnp.concatenate(_split_bf16(x), axis=1), gmean)
    mu = [seg_mean(yu) for yu in y]
    d = [yu - mu_u for yu, mu_u in zip(y, mu)]
    var = [seg_mean(du * du) for du in d]
    for n, (s, p) in enumerate(units):
        yn = d[n] * lax.rsqrt(var[n] + EPS) * gain_ref[:, sl(p)]
        o_ref[s, :, sl(p)] = (g_ref[s, :, sl(p)].astype(F32) * yn).astype(BF16)


def _retention_tables(chunk):
    heads = np.arange(RET_HEADS, dtype=np.float64)
    log_gamma = np.log(1.0 - np.exp2(-5.0 - heads))
    idx = np.arange(chunk)
    rel = (idx[:, None] - idx[None, :]).astype(np.float64)
    decay = np.where(rel[None] >= 0, np.exp(log_gamma[:, None, None] * np.maximum(rel, 0.0)[None]), 0.0)
    dec2 = decay.reshape(RET_PAIRS, 2 * chunk, chunk)
    lane_lg = np.repeat(log_gamma, RET_HEAD_DIM)
    qdec = np.exp(lane_lg[None, :] * (idx + 1)[:, None])
    kdec = np.exp(lane_lg[None, :] * (chunk - 1 - idx)[:, None])
    rdec = np.exp(lane_lg * chunk).reshape(RET_PAIRS, LANES, 1) * np.ones((1, 1, LANES))
    blk = np.arange(LANES) // RET_HEAD_DIM
    bmask = (blk[:, None] == blk[None, :]).astype(np.float64)
    f32 = lambda a: jnp.asarray(a.astype(np.float32))
    gmean2 = np.concatenate([bmask, bmask], axis=0) / RET_HEAD_DIM
    return f32(dec2), f32(qdec), f32(kdec), f32(rdec), f32(bmask), f32(gmean2).astype(BF16)


def _retention(rq, rk, rv, rg, gn_gain, batch, seq, chunk):
    nc = seq // chunk
    group = RET_SEQ_GROUP if batch % RET_SEQ_GROUP == 0 else 1
    dec2, qdec, kdec, rdec, bmask, gmean = _retention_tables(chunk)
    tok_spec = pl.BlockSpec((group, chunk, RET_WIDTH), lambda b, n: (b, n, 0))
    full = lambda a: pl.BlockSpec(a.shape, lambda b, n: (0,) * a.ndim)
    by_seq = lambda a: a.reshape(batch, seq, RET_WIDTH)
    out = pl.pallas_call(
        functools.partial(_retention_kernel, chunk=chunk),
        grid=(batch // group, nc),
        in_specs=[tok_spec] * 4 + [full(dec2), full(qdec), full(kdec), full(rdec), full(bmask),
                                   full(gmean), full(gn_gain)],
        out_specs=tok_spec,
        out_shape=jax.ShapeDtypeStruct((batch, seq, RET_WIDTH), BF16),
        scratch_shapes=[pltpu.VMEM((group * RET_PAIRS, LANES, LANES), F32)],
        compiler_params=pltpu.CompilerParams(
            dimension_semantics=("parallel", "arbitrary"), vmem_limit_bytes=VMEM_LIMIT),
        name="retention",
    )(by_seq(rq), by_seq(rk), by_seq(rv), by_seq(rg), dec2, qdec, kdec, rdec, bmask, gmean, gn_gain)
    return out.reshape(batch * seq, RET_WIDTH)


NEG_BIG = -1e30


V_EXT_ROWS = DIFF_V_DIM + BF16_SUBLANES
QUERY_CHUNK = 256
SCORES_AHEAD_FULL = 3
SCORES_AHEAD_DIAG = 3
STAGE_SLOTS = 16


def _diag_chunks(tq, tk, d):
    assert tk == 2 * QUERY_CHUNK
    per_softmax = tq // QUERY_CHUNK
    out = []
    for c in range(2 * per_softmax):
        q0 = (c % per_softmax) * QUERY_CHUNK
        if q0 + QUERY_CHUNK - 1 < d * tk:
            continue
        kind = "full" if q0 >= (d + 1) * tk else ("tri" if q0 == d * tk else "low_tri")
        out.append((c, kind))
    return out


def _accumulate(acc_ref, cs, alpha, pv):
    acc_ref[:, cs] = alpha * acc_ref[:, cs] + pv


def _diffattn_kernel(q_ref, k_ref, vt_ref, lq1_ref, lk1_ref, lq2_ref, lk2_ref, gain_ref, bias_ref, o_ref,
                     qs_ref, vext_ref, m_ref, acc_ref, stage_ref, *, tq, tk, lambda_init):
    i = pl.program_id(2)
    nk = vext_ref.shape[0]

    @pl.when(i == 0)
    def _():
        for j in range(nk):
            vext_ref[j, 0:DIFF_V_DIM, :] = vt_ref[:, j * tk:(j + 1) * tk]
            vext_ref[j, DIFF_V_DIM:V_EXT_ROWS, :] = jnp.ones((V_EXT_ROWS - DIFF_V_DIM, tk), BF16)

    q = q_ref[...]
    lane = lax.broadcasted_iota(jnp.int32, q.shape, 1)
    zero = jnp.zeros_like(q)
    qs_ref[0:tq, :] = jnp.where(lane < DIFF_QK_DIM, q, zero)
    qs_ref[tq:2 * tq, :] = jnp.where(lane < DIFF_QK_DIM, zero, q)
    m_ref[...] = jnp.full_like(m_ref, NEG_BIG)
    acc_ref[...] = jnp.zeros_like(acc_ref)

    def step(work, n_ahead):
        chunk = lambda c: slice(c * QUERY_CHUNK, (c + 1) * QUERY_CHUNK)

        def scores(j, c, kind):
            n_keys = QUERY_CHUNK if kind == "tri" else tk
            start = pl.multiple_of(j * tk, tk)
            return _dot_nt(k_ref[pl.ds(start, n_keys), :], qs_ref[chunk(c), :])

        ahead = [scores(*work[n]) for n in range(min(n_ahead, len(work)))]
        pending = None
        for n, (j, c, kind) in enumerate(work):
            cs = chunk(c)
            st = ahead.pop(0)
            if n + n_ahead < len(work):
                ahead.append(scores(*work[n + n_ahead]))
            slot = n % stage_ref.shape[0]
            n_keys = st.shape[0]
            stage_ref[slot, 0:n_keys, :] = st
            if kind == "full":
                st = stage_ref[slot]
            else:
                causal = stage_ref[slot, n_keys - QUERY_CHUNK:n_keys, :] + bias_ref[...]
                st = causal if kind == "tri" else jnp.concatenate(
                    [stage_ref[slot, 0:n_keys - QUERY_CHUNK, :], causal], axis=0)
            m_old = m_ref[:, cs]
            m_new = jnp.maximum(m_old, jnp.max(st, axis=0, keepdims=True))
            alpha = jnp.exp2(m_old - m_new)
            p = jnp.exp2(st - m_new).astype(BF16)
            m_ref[:, cs] = m_new
            pv = _dot(vext_ref[j, :, 0:st.shape[0]], p)
            if pending is not None:
                pending()
            pending = functools.partial(_accumulate, acc_ref, cs, alpha, pv)
        pending()

    tiles_per_q = tq // tk
    n_chunks = 2 * tq // QUERY_CHUNK

    def full_tiles(it):
        return [(it * tiles_per_q + d, c, "full") for d in range(tiles_per_q) for c in range(n_chunks)]

    lax.fori_loop(0, i, lambda it, c: (step(full_tiles(it), SCORES_AHEAD_FULL), c)[1], 0)
    kind_order = {"full": 0, "low_tri": 1, "tri": 2}
    diag = lambda it: [(it * tiles_per_q + d, c, kind) for d in range(tiles_per_q)
                       for c, kind in sorted(_diag_chunks(tq, tk, d), key=lambda ck: kind_order[ck[1]])]
    lax.fori_loop(i, i + 1, lambda it, c: (step(diag(it), SCORES_AHEAD_DIAG), c)[1], 0)

    lam = (jnp.exp(jnp.sum(lq1_ref[...] * lk1_ref[...], axis=-1, keepdims=True))
           - jnp.exp(jnp.sum(lq2_ref[...] * lk2_ref[...], axis=-1, keepdims=True)) + lambda_init)
    acc = acc_ref[...]
    o2 = acc[0:DIFF_V_DIM, :] * (1.0 / acc[DIFF_V_DIM:DIFF_V_DIM + 1, :])
    ot = o2[:, :tq] - lam * o2[:, tq:]
    ms = jnp.mean(ot * ot, axis=0, keepdims=True)
    ot = ot * lax.rsqrt(ms + EPS) * gain_ref[...] * (1.0 - lambda_init)
    o_ref[...] = ot.T.astype(BF16)


def _diffattn(dq, dk, dvt, lam_q1, lam_k1, lam_q2, lam_k2, gain, batch, seq, lambda_init, tq, tk):
    nq = seq // tq
    q_spec = pl.BlockSpec((tq, LANES), lambda b, h, i: (b * nq + i, h))
    k_spec = pl.BlockSpec((seq, LANES), lambda b, h, i: (b, h))
    vt_spec = pl.BlockSpec((DIFF_V_DIM, seq), lambda b, h, i: (h, b))
    vec = lambda a: pl.BlockSpec(a.shape, lambda b, h, i: (0, 0))
    key = np.arange(QUERY_CHUNK)[:, None]
    query = np.arange(QUERY_CHUNK)[None, :]
    bias = jnp.asarray(np.where(key <= query, 0.0, NEG_BIG), F32)
    return pl.pallas_call(
        functools.partial(_diffattn_kernel, tq=tq, tk=tk, lambda_init=lambda_init),
        grid=(batch, DIFF_HEADS, nq),
        in_specs=[q_spec, k_spec, vt_spec, vec(lam_q1), vec(lam_k1), vec(lam_q2), vec(lam_k2), vec(gain),
                  vec(bias)],
        out_specs=q_spec,
        out_shape=jax.ShapeDtypeStruct(dq.shape, BF16),
        scratch_shapes=[
            pltpu.VMEM((2 * tq, LANES), BF16),
            pltpu.VMEM((seq // tk, V_EXT_ROWS, tk), BF16),
            pltpu.VMEM((1, 2 * tq), F32),
            pltpu.VMEM((V_EXT_ROWS, 2 * tq), F32),
            pltpu.VMEM((STAGE_SLOTS, tk, QUERY_CHUNK), F32),
        ],
        compiler_params=pltpu.CompilerParams(
            dimension_semantics=("parallel", "parallel", "arbitrary"), vmem_limit_bytes=VMEM_LIMIT),
        name="diffattn",
    )(dq, dk, dvt, lam_q1, lam_k1, lam_q2, lam_k2, gain, bias)


def _route(logits):
    r = [logits[g:g + 1, :] for g in range(N_GROUPS)]
    gmax = jnp.maximum(jnp.maximum(r[0], r[1]), jnp.maximum(r[2], r[3]))
    g_idx = jnp.where(r[0] == gmax, 0, jnp.where(r[1] == gmax, 1, jnp.where(r[2] == gmax, 2, 3)))
    denom = sum(jnp.exp(rg - gmax) for rg in r)
    g_weight = 1.0 / denom
    sel = jnp.zeros((EXPERTS_PER_GROUP, logits.shape[1]), F32)
    for g in range(N_GROUPS):
        rows = logits[SUBLANES + g * EXPERTS_PER_GROUP:SUBLANES + (g + 1) * EXPERTS_PER_GROUP, :]
        sel = jnp.where(g_idx == g, rows, sel)
    eidx = lax.broadcasted_iota(jnp.int32, sel.shape, 0)
    v1 = jnp.max(sel, axis=0, keepdims=True)
    i1 = jnp.min(jnp.where(sel == v1, eidx, EXPERTS_PER_GROUP), axis=0, keepdims=True)
    sel2 = jnp.where(eidx == i1, -jnp.inf, sel)
    v2 = jnp.max(sel2, axis=0, keepdims=True)
    i2 = jnp.min(jnp.where(sel2 == v2, eidx, EXPERTS_PER_GROUP), axis=0, keepdims=True)
    e2 = jnp.exp(v2 - v1)
    w1 = g_weight / (1.0 + e2)
    w2 = g_weight * e2 / (1.0 + e2)
    return g_idx, i1, i2, w1, w2


OUTPROJ_PARTS = 2
OUTPROJ_TILES_PER_STEP = 2


def _outproj_kernel(ret_ref, diff_ref, x_ref, mod_ref, gain_ref, wo_ref, wr_ref, br_ref, tri_ref,
                    x1_ref, h2_ref, ri_ref, rw_ref, cnt_ref, *, tiles_per_batch):
    b = pl.program_id(0) // tiles_per_batch
    gate1 = mod_ref[pl.ds(b, 1), 2 * D_MODEL:3 * D_MODEL]
    shift = mod_ref[pl.ds(b, 1), 3 * D_MODEL:4 * D_MODEL]
    scale = mod_ref[pl.ds(b, 1), 4 * D_MODEL:5 * D_MODEL]
    wr = wr_ref[...]
    rows = x_ref.shape[0]
    tm = tri_ref.shape[0]
    n_parts = OUTPROJ_PARTS * rows // tm
    parts = [slice(n * rows // n_parts, (n + 1) * rows // n_parts) for n in range(n_parts)]
    mix = [_dot(jnp.concatenate([ret_ref[r, :], diff_ref[r, :]], axis=1), wo_ref[...]) for r in parts]
    for r, m in zip(parts, mix):
        x1_ref[r, :] = x_ref[r, :] + gate1 * m
    h_split = [_split_bf16(_norm_modulate(x1_ref[r, :], gain_ref[...], shift, scale)) for r in parts]
    for r, (h_hi, _) in zip(parts, h_split):
        h2_ref[r, :] = h_hi
    by_hi = [_dot_nt(wr, h_hi) for h_hi, _ in h_split]
    by_lo = [_dot_nt(wr[:ROUTER_ROWS], h_lo) for _, h_lo in h_split]
    logits = [a[:ROUTER_ROWS] + a[ROUTER_ROWS:] + c + br_ref[...] for a, c in zip(by_hi, by_lo)]
    routed = [_route(lg) for lg in logits]
    g_idx, i1, i2, w1, w2 = [jnp.concatenate([rt[n] for rt in routed], axis=1) for n in range(5)]
    logits = jnp.concatenate(logits, axis=1)
    e1 = g_idx * EXPERTS_PER_GROUP + i1
    e2 = g_idx * EXPERTS_PER_GROUP + i2
    eidx = lax.broadcasted_iota(jnp.int32, (N_EXPERTS, logits.shape[1]), 0)
    hit1 = eidx == e1
    hit2 = eidx == e2
    onehot = jnp.where(hit1 | hit2, 1.0, 0.0)
    tiles = [slice(n * tm, (n + 1) * tm) for n in range(rows // tm)]
    before = jnp.concatenate([_dot(onehot[:, t].astype(BF16), tri_ref[...]) for t in tiles], axis=1)
    r1 = jnp.sum(jnp.where(hit1, before, 0.0), axis=0, keepdims=True)
    r2 = jnp.sum(jnp.where(hit2, before, 0.0), axis=0, keepdims=True)
    zi = jnp.zeros_like(e1)
    ri_ref[...] = jnp.concatenate([e1, e2, r1.astype(jnp.int32), r2.astype(jnp.int32), zi, zi, zi, zi], axis=0)
    zf = jnp.zeros_like(w1)
    rw_ref[...] = jnp.concatenate([w1, w2, zf, zf, zf, zf, zf, zf], axis=0)
    for n, t in enumerate(tiles):
        counts = jnp.sum(onehot[:, t], axis=1, keepdims=True)
        cnt_ref[n] = jnp.broadcast_to(counts, (N_EXPERTS, LANES)).astype(jnp.int32)


def _outproj(ret_out, diff_out, x2, mod, gain, w_out, wr, br, seq, tm):
    tokens = x2.shape[0]
    n_tiles = tokens // tm
    per_step = OUTPROJ_TILES_PER_STEP if seq % (OUTPROJ_TILES_PER_STEP * tm) == 0 else 1
    rows = per_step * tm
    tri = jnp.asarray(np.arange(tm)[:, None] < np.arange(tm)[None, :], BF16)
    tok_spec = lambda w: pl.BlockSpec((rows, w), lambda i: (i, 0))
    row_spec = pl.BlockSpec((SUBLANES, rows), lambda i: (0, i))
    full = lambda a: pl.BlockSpec(a.shape, lambda i: (0,) * a.ndim)
    return pl.pallas_call(
        functools.partial(_outproj_kernel, tiles_per_batch=seq // rows),
        grid=(tokens // rows,),
        in_specs=[tok_spec(RET_WIDTH), tok_spec(DIFF_WIDTH), tok_spec(D_MODEL), full(mod), full(gain),
                  full(w_out), full(wr), full(br), full(tri)],
        out_specs=[tok_spec(D_MODEL), tok_spec(D_MODEL), row_spec, row_spec,
                   pl.BlockSpec((per_step, N_EXPERTS, LANES), lambda i: (i, 0, 0))],
        out_shape=[jax.ShapeDtypeStruct((tokens, D_MODEL), F32),
                   jax.ShapeDtypeStruct((tokens, D_MODEL), BF16),
                   jax.ShapeDtypeStruct((SUBLANES, tokens), jnp.int32),
                   jax.ShapeDtypeStruct((SUBLANES, tokens), F32),
                   jax.ShapeDtypeStruct((n_tiles, N_EXPERTS, LANES), jnp.int32)],
        compiler_params=pltpu.CompilerParams(
            dimension_semantics=("parallel",), vmem_limit_bytes=VMEM_LIMIT),
        name="outproj",
    )(ret_out, diff_out, x2, mod, gain, w_out, wr, br, tri)


CHUNK = 8
BIG_PIECE = 4 * CHUNK
MAX_SMALL_PIECES = N_EXPERTS * (BIG_PIECE // CHUNK - 1)
PIECE_TABLES = ("n_big", "big_src", "big_dst", "n_small", "small_src", "small_dst")
DISPATCH_TILES_PER_STEP = 2
COMBINE_TILES_PER_STEP = 2
TMX = 512
IN_SLOTS = 3
OUT_SLOTS = 2


def _local_rows(tm):
    rows = 2 * tm + N_EXPERTS * (CHUNK - 1)
    return pl.cdiv(rows, BF16_SUBLANES) * BF16_SUBLANES


def _max_big_pieces(tm):
    return _local_rows(tm) // BIG_PIECE


def _sorted_rows_alloc(tokens, tm):
    worst = 2 * tokens + (tokens // tm) * N_EXPERTS * (CHUNK - 1) + N_EXPERTS * (TMX - CHUNK)
    return (worst + TMX - 1) // TMX * TMX


def _dispatch_plan(cnt, tokens, tm):
    i32 = jnp.int32
    pad = (cnt + CHUNK - 1) // CHUNK * CHUNK
    local_end = jnp.cumsum(pad, axis=1)
    local_start = local_end - pad
    seg_rows = jnp.sum(pad, axis=0)
    seg_pad = (seg_rows + TMX - 1) // TMX * TMX
    seg_end = jnp.cumsum(seg_pad)
    seg_start = seg_end - seg_pad
    run_dst = seg_start[None, :] + jnp.cumsum(pad, axis=0) - pad

    def pieces(count, offset, size, max_n):
        end = jnp.cumsum(count, axis=1)
        start = end - count
        k = jnp.arange(max_n, dtype=i32)[None, :, None]
        owns = (start[:, None, :] <= k) & (k < end[:, None, :])
        within = size * (k - start[:, None, :]) + offset[:, None, :]
        src = jnp.sum(jnp.where(owns, local_start[:, None, :] + within, 0), axis=-1)
        dst = jnp.sum(jnp.where(owns, run_dst[:, None, :] + within, 0), axis=-1)
        return end[:, -1].astype(i32), src.reshape(-1).astype(i32), dst.reshape(-1).astype(i32)

    n_big, big_src, big_dst = pieces(pad // BIG_PIECE, jnp.zeros_like(pad), BIG_PIECE, _max_big_pieces(tm))
    n_small, small_src, small_dst = pieces(pad % BIG_PIECE // CHUNK, pad // BIG_PIECE * BIG_PIECE, CHUNK,
                                           MAX_SMALL_PIECES)
    m = TMX * jnp.arange(_sorted_rows_alloc(tokens, tm) // TMX, dtype=i32)
    tile_expert = jnp.minimum(jnp.sum(seg_end[None, :] <= m[:, None], axis=-1), N_EXPERTS - 1)
    towns = (seg_start[None, :] <= m[:, None]) & (m[:, None] < seg_end[None, :])
    used = seg_pad > 0
    parity = (jnp.cumsum(used) - used) % 2
    eids = jnp.arange(N_EXPERTS, dtype=i32)
    later_used = (eids[None, :] > eids[:, None]) & used[None, :]
    next_used = jnp.min(jnp.where(later_used, eids[None, :], N_EXPERTS), axis=1)
    next_used = jnp.where(next_used == N_EXPERTS, -1, next_used)
    pick = lambda per_expert: jnp.sum(jnp.where(towns, per_expert[None, :], 0), axis=-1)
    tile_first = jnp.sum(jnp.where(towns & (seg_start[None, :] == m[:, None]), 1, 0), axis=-1)
    tile_next = jnp.where(jnp.any(towns, axis=-1), pick(next_used), -1)
    return dict(
        tile_first=tile_first.astype(i32),
        tile_slot=pick(parity).astype(i32),
        tile_next=tile_next.astype(i32),
        tile_rows=jnp.clip(pick(seg_start + seg_rows) - m, 0, TMX).astype(i32),
        local_start=local_start.reshape(-1).astype(i32),
        n_big=n_big, big_src=big_src, big_dst=big_dst,
        n_small=n_small, small_src=small_src, small_dst=small_dst,
        tail_base=(seg_start + seg_rows).astype(i32),
        tail_rows=(seg_pad - seg_rows).astype(i32),
        tile_expert=tile_expert.astype(i32),
        n_used=(seg_end[-1:] // TMX).astype(i32),
    )


WAIT_UNROLL = 8


def _wait_times(copy, n):
    lax.fori_loop(0, n // WAIT_UNROLL, lambda i, c: ([copy.wait() for _ in range(WAIT_UNROLL)], c)[1], 0)
    lax.fori_loop(0, n % WAIT_UNROLL, lambda i, c: (copy.wait(), c)[1], 0)


def _for_each(n, body, unroll=4):
    main = n // unroll
    lax.fori_loop(0, main, lambda i, c: ([body(i * unroll + u, u) for u in range(unroll)], c)[1], 0)
    lax.fori_loop(main * unroll, n, lambda j, c: (body(j, 0), c)[1], 0)


def _local_slots(ri_ref, local_start_ref, tile, cols=slice(None)):
    e1, e2 = ri_ref[0:1, cols], ri_ref[1:2, cols]
    s1, s2 = ri_ref[2:3, cols], ri_ref[3:4, cols]
    for e in range(N_EXPERTS):
        start = local_start_ref[tile * N_EXPERTS + e]
        s1 = s1 + jnp.where(e1 == e, start, 0)
        s2 = s2 + jnp.where(e2 == e, start, 0)
    return s1, s2


def _run_pieces(piece_refs, local_ref, sorted_ref, sem_ref, to_sorted):
    n_big_ref, big_src_ref, big_dst_ref, n_small_ref, small_src_ref, small_dst_ref = piece_refs
    r_loc = local_ref.shape[1]
    kinds = [(BIG_PIECE, n_big_ref, big_src_ref, big_dst_ref, r_loc // BIG_PIECE),
             (CHUNK, n_small_ref, small_src_ref, small_dst_ref, MAX_SMALL_PIECES)]

    def copy(sl, size, local_row, sorted_row):
        local = local_ref.at[sl, pl.ds(pl.multiple_of(local_row, CHUNK), size), :]
        srt = sorted_ref.at[pl.ds(pl.multiple_of(sorted_row, CHUNK), size), :]
        return pltpu.make_async_copy(local, srt, sem_ref.at[sl]) if to_sorted else \
            pltpu.make_async_copy(srt, local, sem_ref.at[sl])

    def start(tile, sl):
        for size, n_ref, src_ref, dst_ref, max_n in kinds:
            _for_each(n_ref[tile], lambda k, lane: copy(sl, size, src_ref[tile * max_n + k],
                                                        dst_ref[tile * max_n + k]).start(priority=lane % 2))

    def wait(tile, sl):
        for size, n_ref, _, _, _ in kinds:
            _wait_times(copy(sl, size, 0, 0), n_ref[tile])

    return start, wait


def _dispatch_kernel(local_start_ref, n_big_ref, big_src_ref, big_dst_ref, n_small_ref, small_src_ref, small_dst_ref,
                     tail_base_ref, tail_rows_ref, n_used_ref,
                     h_ref, ri_ref, xs_ref, buf_ref, zero_ref, sem_ref, tail_sem_ref, *, r_loc):
    b = pl.program_id(0)
    nb = pl.num_programs(0)
    per_step = buf_ref.shape[0] // 2
    tm = h_ref.shape[0] // per_step
    gen = b % 2
    start_runs, drain_tile = _run_pieces(
        (n_big_ref, big_src_ref, big_dst_ref, n_small_ref, small_src_ref, small_dst_ref), buf_ref, xs_ref, sem_ref,
        to_sorted=True)

    def drain(step, g):
        for t in range(per_step):
            drain_tile(step * per_step + t, g * per_step + t)

    @pl.when(b >= 2)
    def _():
        drain(b - 2, gen)

    tiles = [b * per_step + t for t in range(per_step)]
    cols = [slice(t * tm, (t + 1) * tm) for t in range(per_step)]
    slots = [_local_slots(ri_ref, local_start_ref, tile, c) for tile, c in zip(tiles, cols)]
    rows = lax.broadcasted_iota(jnp.int32, (r_loc, tm), 0)
    perms = [jnp.where((rows == s1) | (rows == s2), 1.0, 0.0).astype(BF16) for s1, s2 in slots]
    for t, (perm, c) in enumerate(zip(perms, cols)):
        buf_ref[gen * per_step + t] = _dot(perm, h_ref[c, :])
    for t, tile in enumerate(tiles):
        start_runs(tile, gen * per_step + t)

    def tail_pieces(e, act):
        n = tail_rows_ref[e]
        size = TMX // 2
        while size >= CHUNK:
            dst = pl.multiple_of(tail_base_ref[e] + (n & (-2 * size)), CHUNK)
            cp = pltpu.make_async_copy(zero_ref.at[pl.ds(0, size), :], xs_ref.at[pl.ds(dst, size), :],
                                       tail_sem_ref.at[0])
            pl.when((n & size) != 0)(functools.partial(act, cp))
            size //= 2

    def unused_tile_copy(m):
        dst = pl.multiple_of(m * TMX, TMX)
        return pltpu.make_async_copy(zero_ref, xs_ref.at[pl.ds(dst, TMX), :], tail_sem_ref.at[1])

    n_alloc = xs_ref.shape[0] // TMX

    @pl.when(b == 0)
    def _():
        zero_ref[...] = jnp.zeros_like(zero_ref)

    experts_per_step = pl.cdiv(N_EXPERTS, nb)
    lax.fori_loop(jnp.minimum(b * experts_per_step, N_EXPERTS), jnp.minimum((b + 1) * experts_per_step, N_EXPERTS),
                  lambda e, c: (tail_pieces(e, lambda cp: cp.start()), c)[1], 0)
    tiles_per_step = pl.cdiv(n_alloc, nb)
    first_unused = n_used_ref[0]
    lax.fori_loop(jnp.minimum(first_unused + b * tiles_per_step, n_alloc),
                  jnp.minimum(first_unused + (b + 1) * tiles_per_step, n_alloc),
                  lambda m, c: (unused_tile_copy(m).start(), c)[1], 0)

    @pl.when(b == nb - 1)
    def _():
        lax.fori_loop(0, N_EXPERTS, lambda e, c: (tail_pieces(e, lambda cp: cp.wait()), c)[1], 0)
        lax.fori_loop(n_used_ref[0], n_alloc, lambda m, c: (unused_tile_copy(m).wait(), c)[1], 0)

        @pl.when(b >= 1)
        def _():
            drain(b - 1, 1 - gen)

        drain(b, gen)


def _dispatch(h2, ri, plan, tm):
    tokens = h2.shape[0]
    r_loc = _local_rows(tm)
    prefetch = [plan["local_start"]] + [plan[k] for k in PIECE_TABLES] + [
        plan["tail_base"], plan["tail_rows"], plan["n_used"]]
    per_step = DISPATCH_TILES_PER_STEP if (tokens // tm) % DISPATCH_TILES_PER_STEP == 0 else 1
    rows = per_step * tm
    grid_spec = pltpu.PrefetchScalarGridSpec(
        num_scalar_prefetch=len(prefetch),
        grid=(tokens // rows,),
        in_specs=[pl.BlockSpec((rows, D_MODEL), lambda i, *_: (i, 0)),
                  pl.BlockSpec((SUBLANES, rows), lambda i, *_: (0, i))],
        out_specs=pl.BlockSpec(memory_space=pl.ANY),
        scratch_shapes=[pltpu.VMEM((2 * per_step, r_loc, D_MODEL), F32), pltpu.VMEM((TMX, D_MODEL), F32),
                        pltpu.SemaphoreType.DMA((2 * per_step,)), pltpu.SemaphoreType.DMA((2,))],
    )
    return pl.pallas_call(
        functools.partial(_dispatch_kernel, r_loc=r_loc),
        grid_spec=grid_spec,
        out_shape=jax.ShapeDtypeStruct((_sorted_rows_alloc(tokens, tm), D_MODEL), F32),
        compiler_params=pltpu.CompilerParams(
            dimension_semantics=("arbitrary",), vmem_limit_bytes=VMEM_LIMIT),
        name="dispatch",
    )(*prefetch, h2, ri)


def _experts_kernel(tile_expert_ref, n_used_ref, first_ref, slot_ref, next_ref, rows_ref, xs_hbm, wg_hbm, wu_hbm, wd_hbm,
                    ys_hbm, wg_st, wu_st, wd_st, wg_bf, wu_bf, wd_bf, a_ref, u_ref, xin_ref, yout_ref,
                    sem_ref, in_sem_ref, out_sem_ref):
    m = pl.program_id(0)
    n_used = n_used_ref[0]

    def weight_copies(e, s):
        return [pltpu.make_async_copy(src.at[e], dst.at[s], sem_ref.at[s, n])
                for n, (src, dst) in enumerate([(wg_hbm, wg_st), (wu_hbm, wu_st), (wd_hbm, wd_st)])]

    def in_copy(t):
        s = t % IN_SLOTS
        return pltpu.make_async_copy(xs_hbm.at[pl.ds(pl.multiple_of(t * TMX, TMX), TMX), :], xin_ref.at[s],
                                     in_sem_ref.at[s])

    def out_copy(t):
        s = t % OUT_SLOTS
        return pltpu.make_async_copy(yout_ref.at[s], ys_hbm.at[pl.ds(pl.multiple_of(t * TMX, TMX), TMX), :],
                                     out_sem_ref.at[s])

    @pl.when(m == 0)
    def _():
        for t in range(IN_SLOTS - 1):
            pl.when(t < n_used)(in_copy(t).start)

    @pl.when(m < n_used)
    def _():
        @pl.when(m + IN_SLOTS - 1 < n_used)
        def _():
            in_copy(m + IN_SLOTS - 1).start()

        @pl.when(first_ref[m] == 1)
        def _():
            s = slot_ref[m]

            @pl.when(m == 0)
            def _():
                for cp in weight_copies(tile_expert_ref[0], 0):
                    cp.start()

            for cp in weight_copies(tile_expert_ref[m], s):
                cp.wait()

            @pl.when(next_ref[m] >= 0)
            def _():
                for cp in weight_copies(next_ref[m], 1 - s):
                    cp.start()

            wg_bf[...] = wg_st[s].astype(BF16)
            wu_bf[...] = wu_st[s].astype(BF16)
            wd_bf[...] = wd_st[s].astype(BF16)

        xs_ref = xin_ref.at[m % IN_SLOTS]
        ys_ref = yout_ref.at[m % OUT_SLOTS]
        in_copy(m).wait()

        @pl.when(m >= OUT_SLOTS)
        def _():
            out_copy(m - OUT_SLOTS).wait()

        def mlp(rows):
            x = xs_ref[rows, :].astype(BF16)
            a_ref[rows, :] = _dot(x, wg_bf[...])
            u_ref[rows, :] = _dot(x, wu_bf[...])
            hid = (_silu(a_ref[rows, :]) * u_ref[rows, :]).astype(BF16)
            ys_ref[rows, :] = _dot(hid, wd_bf[...])

        half = TMX // 2

        @pl.when(rows_ref[m] > half)
        def _():
            mlp(slice(0, TMX))

        @pl.when(rows_ref[m] <= half)
        def _():
            mlp(slice(0, half))
            ys_ref[half:, :] = jnp.zeros((TMX - half, D_MODEL), F32)

        out_copy(m).start()

        @pl.when(m == n_used - 1)
        def _():
            for back in range(OUT_SLOTS):
                pl.when(m - back >= 0)(out_copy(m - back).wait)


def _experts(xs, plan, wg, wu, wd):
    n_tiles = xs.shape[0] // TMX
    hbm = pl.BlockSpec(memory_space=pl.ANY)
    up_shape, down_shape = (D_MODEL, D_EXPERT), (D_EXPERT, D_MODEL)
    grid_spec = pltpu.PrefetchScalarGridSpec(
        num_scalar_prefetch=6,
        grid=(n_tiles,),
        in_specs=[hbm, hbm, hbm, hbm],
        out_specs=hbm,
        scratch_shapes=[pltpu.VMEM((2,) + up_shape, F32), pltpu.VMEM((2,) + up_shape, F32),
                        pltpu.VMEM((2,) + down_shape, F32),
                        pltpu.VMEM(up_shape, BF16), pltpu.VMEM(up_shape, BF16), pltpu.VMEM(down_shape, BF16),
                        pltpu.VMEM((TMX, D_EXPERT), F32), pltpu.VMEM((TMX, D_EXPERT), F32),
                        pltpu.VMEM((IN_SLOTS, TMX, D_MODEL), F32), pltpu.VMEM((OUT_SLOTS, TMX, D_MODEL), F32),
                        pltpu.SemaphoreType.DMA((2, 3)), pltpu.SemaphoreType.DMA((IN_SLOTS,)),
                        pltpu.SemaphoreType.DMA((OUT_SLOTS,))],
    )
    return pl.pallas_call(
        _experts_kernel,
        grid_spec=grid_spec,
        out_shape=jax.ShapeDtypeStruct(xs.shape, F32),
        input_output_aliases={6: 0},
        compiler_params=pltpu.CompilerParams(
            dimension_semantics=("arbitrary",), vmem_limit_bytes=VMEM_LIMIT),
        name="experts",
    )(plan["tile_expert"], plan["n_used"], plan["tile_first"], plan["tile_slot"], plan["tile_next"], plan["tile_rows"],
      xs, wg, wu, wd)


def _combine_kernel(local_start_ref, n_big_ref, big_src_ref, big_dst_ref, n_small_ref, small_src_ref, small_dst_ref,
                    ys_ref, ri_ref, rw_ref, x1_ref, mod_ref, gain_ref, o_ref, buf_ref, sem_ref,
                    *, r_loc, tiles_per_batch):
    b = pl.program_id(0)
    nb = pl.num_programs(0)
    per_step = buf_ref.shape[0] // 2
    tm = x1_ref.shape[0] // per_step
    gen = b % 2
    fetch_tile, wait_tile = _run_pieces(
        (n_big_ref, big_src_ref, big_dst_ref, n_small_ref, small_src_ref, small_dst_ref), buf_ref, ys_ref, sem_ref,
        to_sorted=False)

    def fetch(step, g):
        for t in range(per_step):
            fetch_tile(step * per_step + t, g * per_step + t)

    @pl.when(b == 0)
    def _():
        buf_ref[...] = jnp.zeros_like(buf_ref)
        fetch(0, 0)

    @pl.when(b + 1 < nb)
    def _():
        fetch(b + 1, 1 - gen)

    for t in range(per_step):
        wait_tile(b * per_step + t, gen * per_step + t)

    tiles = [b * per_step + t for t in range(per_step)]
    cols = [slice(t * tm, (t + 1) * tm) for t in range(per_step)]
    slots = [_local_slots(ri_ref, local_start_ref, tile, c) for tile, c in zip(tiles, cols)]
    rows = lax.broadcasted_iota(jnp.int32, (r_loc, tm), 0)
    hits = [(rows == s1, rows == s2) for s1, s2 in slots]
    w_rows = [jnp.sum(jnp.where(h1, rw_ref[0:1, c], jnp.where(h2, rw_ref[1:2, c], 0.0)), axis=1, keepdims=True)
              for (h1, h2), c in zip(hits, cols)]
    perms = [jnp.where(h1 | h2, 1.0, 0.0).astype(BF16) for h1, h2 in hits]
    yws = [(buf_ref[gen * per_step + t] * w_rows[t]).astype(BF16) for t in range(per_step)]
    moes = [_dot_tn(perm, yw) for perm, yw in zip(perms, yws)]
    batch = b // tiles_per_batch
    gate2 = mod_ref[pl.ds(batch, 1), 5 * D_MODEL:6 * D_MODEL]
    for c, moe in zip(cols, moes):
        x2 = x1_ref[c, :] + gate2 * moe
        ms = jnp.mean(x2 * x2, axis=-1, keepdims=True)
        o_ref[c, :] = x2 * lax.rsqrt(ms + EPS) * gain_ref[...]


def _combine(ys, ri, rw, x1, mod, gain, plan, seq, tm):
    tokens = x1.shape[0]
    r_loc = _local_rows(tm)
    per_step = COMBINE_TILES_PER_STEP if seq % (COMBINE_TILES_PER_STEP * tm) == 0 else 1
    rows = per_step * tm
    row_spec = pl.BlockSpec((SUBLANES, rows), lambda i, *_: (0, i))
    tok_spec = pl.BlockSpec((rows, D_MODEL), lambda i, *_: (i, 0))
    full = lambda a: pl.BlockSpec(a.shape, lambda i, *_: (0,) * a.ndim)
    prefetch = [plan["local_start"]] + [plan[k] for k in PIECE_TABLES]
    grid_spec = pltpu.PrefetchScalarGridSpec(
        num_scalar_prefetch=len(prefetch),
        grid=(tokens // rows,),
        in_specs=[pl.BlockSpec(memory_space=pl.ANY), row_spec, row_spec, tok_spec, full(mod), full(gain)],
        out_specs=tok_spec,
        scratch_shapes=[pltpu.VMEM((2 * per_step, r_loc, D_MODEL), F32), pltpu.SemaphoreType.DMA((2 * per_step,))],
    )
    return pl.pallas_call(
        functools.partial(_combine_kernel, r_loc=r_loc, tiles_per_batch=seq // rows),
        grid_spec=grid_spec,
        out_shape=jax.ShapeDtypeStruct((tokens, D_MODEL), F32),
        compiler_params=pltpu.CompilerParams(
            dimension_semantics=("arbitrary",), vmem_limit_bytes=VMEM_LIMIT),
        name="combine",
    )(*prefetch, ys, ri, rw, x1, mod, gain)


def _rotary_tables(seq):
    half = RET_HEAD_DIM // 2
    inv_freq = 1.0 / (ROPE_BASE ** (np.arange(half, dtype=np.float64) / half))
    ang = np.arange(seq, dtype=np.float64)[:, None] * inv_freq[None, :]
    cos = np.cos(ang)
    sin = np.sin(ang)
    f32 = lambda a: jnp.asarray(a.astype(np.float32))
    return f32(np.tile(cos, (1, 4))), f32(np.concatenate([-sin, sin, -sin, sin], axis=1))


def _pick_tile(n, pref):
    t = min(n, pref)
    assert n % t == 0, (n, t)
    return t


def kernel(x, c, ada_w, ada_b, norm1_gain, norm2_gain, w_in, w_out, ret_gn_gain, lam_q1, lam_k1, lam_q2,
           lam_k2, diff_subln_gain, w_group, b_group, w_expert, b_expert, w_gate, w_up, w_down, final_gain):
    batch, seq, d = x.shape
    assert d == D_MODEL and batch <= 8 and ada_w.shape[0] == 1
    layer = 0
    lambda_init = 0.8 - 0.6 * math.exp(-0.3 * layer)
    tokens = batch * seq
    x2 = x.reshape(tokens, d)
    tm = _pick_tile(seq, TOKEN_TILE)

    c_pad = jnp.zeros((SUBLANES, d), F32).at[:batch].set(c)
    mod = _adaln(c_pad, ada_w[layer], ada_b[layer].reshape(1, -1))

    cos_t, sin_t = _rotary_tables(seq)
    rq, rk, rv, rg, dq, dk, dvt = _inproj(
        x2, mod, norm1_gain[layer].reshape(1, d), w_in[layer].astype(BF16), cos_t, sin_t, seq, tm)

    ret_out = _retention(rq, rk, rv, rg, ret_gn_gain[layer].reshape(1, RET_WIDTH), batch, seq,
                         _pick_tile(seq, RET_CHUNK))
    diff_out = _diffattn(
        dq, dk, dvt, lam_q1[layer].reshape(1, -1), lam_k1[layer].reshape(1, -1), lam_q2[layer].reshape(1, -1),
        lam_k2[layer].reshape(1, -1), diff_subln_gain[layer].reshape(-1, 1), batch, seq, lambda_init,
        _pick_tile(seq, ATTN_Q_TILE), 2 * QUERY_CHUNK)

    w_router = jnp.concatenate(
        [w_group[layer].T, jnp.zeros((SUBLANES - N_GROUPS, d), F32), w_expert[layer].reshape(d, N_EXPERTS).T], axis=0)
    b_router = jnp.concatenate(
        [b_group[layer], jnp.zeros((SUBLANES - N_GROUPS,), F32), b_expert[layer].reshape(N_EXPERTS)]).reshape(-1, 1)
    wr_hi = w_router.astype(BF16)
    wr_lo = (w_router - wr_hi.astype(F32)).astype(BF16)
    x1, h2, ri, rw, cnt = _outproj(ret_out, diff_out, x2, mod, norm2_gain[layer].reshape(1, d),
                                   w_out[layer].astype(BF16), jnp.concatenate([wr_hi, wr_lo], axis=0), b_router,
                                   seq, tm)

    plan = _dispatch_plan(cnt[:, :, 0], tokens, tm)
    xs = _dispatch(h2, ri, plan, tm)
    ys = _experts(xs, plan, w_gate[layer].reshape(N_EXPERTS, d, D_EXPERT),
                  w_up[layer].reshape(N_EXPERTS, d, D_EXPERT), w_down[layer].reshape(N_EXPERTS, D_EXPERT, d))
    out = _combine(ys, ri, rw, x1, mod, final_gain.reshape(1, d), plan, seq, tm)
    return out.reshape(batch, seq, d)
```

```python
import functools
import math

import jax
import jax.numpy as jnp
import numpy as np
from jax import lax
from jax.experimental import pallas as pl
from jax.experimental.pallas import tpu as pltpu

F32 = jnp.float32
BF16 = jnp.bfloat16

D_MODEL = 1024
RET_HEAD_DIM = 64
RET_WIDTH = 512
RET_HEADS = 8
RET_PAIRS = RET_HEADS // 2
DIFF_QK_DIM = 64
DIFF_V_DIM = 128
DIFF_HEADS = 4
DIFF_WIDTH = 512
N_GROUPS = 4
EXPERTS_PER_GROUP = 8
N_EXPERTS = N_GROUPS * EXPERTS_PER_GROUP
D_EXPERT = 512
N_MOD = 6
ROPE_BASE = 10000.0
EPS = 1e-6
LANES = 128
SUBLANES = 8
BF16_SUBLANES = 2 * SUBLANES
ROUTER_ROWS = SUBLANES + N_EXPERTS
VMEM_LIMIT = 56 * 1024 * 1024
TOKEN_TILE = 512
RET_CHUNK = 128
ATTN_Q_TILE = 2048


def _dot(a, b):
    return jnp.dot(a, b, preferred_element_type=F32)


def _dot_nt(a, b):
    return lax.dot_general(a, b, (((1,), (1,)), ((), ())), preferred_element_type=F32)


def _dot_tn(a, b):
    return lax.dot_general(a, b, (((0,), (0,)), ((), ())), preferred_element_type=F32)


def _split_bf16(x):
    hi = x.astype(BF16)
    lo = (x - hi.astype(F32)).astype(BF16)
    return hi, lo


def _silu(x):
    return x / (1.0 + jnp.exp(-x))


def _adaln_kernel(c_ref, w_ref, b_ref, o_ref):
    ca = _silu(c_ref[...])
    c_hi, c_lo = _split_bf16(ca)
    w_hi, w_lo = _split_bf16(w_ref[...])
    by_hi = _dot(jnp.concatenate([c_hi, c_lo], axis=0), w_hi)
    o_ref[...] = by_hi[:SUBLANES] + by_hi[SUBLANES:] + _dot(c_hi, w_lo) + b_ref[...]


def _adaln(c_pad, ada_w, ada_b):
    n_out = ada_w.shape[1]
    tn = D_MODEL
    return pl.pallas_call(
        _adaln_kernel,
        grid=(n_out // tn,),
        in_specs=[
            pl.BlockSpec((SUBLANES, D_MODEL), lambda j: (0, 0)),
            pl.BlockSpec((D_MODEL, tn), lambda j: (0, j)),
            pl.BlockSpec((1, tn), lambda j: (0, j)),
        ],
        out_specs=pl.BlockSpec((SUBLANES, tn), lambda j: (0, j)),
        out_shape=jax.ShapeDtypeStruct((SUBLANES, n_out), F32),
        compiler_params=pltpu.CompilerParams(vmem_limit_bytes=VMEM_LIMIT),
        name="adaln",
    )(c_pad, ada_w, ada_b)


def _norm_modulate(x, gain, shift, scale):
    ms = jnp.mean(x * x, axis=-1, keepdims=True)
    y = x * lax.rsqrt(ms + EPS) * gain
    return y * (1.0 + scale) + shift


def _rotary_slab(x, cos, sin_signed, lane_lo):
    swapped = jnp.where(lane_lo, pltpu.roll(x, 96, 1), pltpu.roll(x, 32, 1))
    return x * cos + swapped * sin_signed


def _inproj_kernel(x_ref, mod_ref, gain_ref, w_ref, cos_ref, sin_ref,
                   rq_ref, rk_ref, rv_ref, rg_ref, dq_ref, dk_ref, dvt_ref, *, tiles_per_batch):
    b = pl.program_id(0) // tiles_per_batch
    shift = mod_ref[pl.ds(b, 1), 0:D_MODEL]
    scale = mod_ref[pl.ds(b, 1), D_MODEL:2 * D_MODEL]
    h = _norm_modulate(x_ref[...], gain_ref[...], shift, scale).astype(BF16)
    cos = cos_ref[...]
    sin = sin_ref[...]
    lane = lax.broadcasted_iota(jnp.int32, cos.shape, 1)
    lane_lo = (lane % 64) < 32

    def proj(chunk):
        return _dot(h, w_ref[:, chunk * RET_WIDTH:(chunk + 1) * RET_WIDTH])

    def rotary(acc, out_ref, post_scale):
        for s in range(RET_WIDTH // LANES):
            sl = slice(s * LANES, (s + 1) * LANES)
            out_ref[:, sl] = (_rotary_slab(acc[:, sl], cos, sin, lane_lo) * post_scale).astype(BF16)

    rotary(proj(0), rq_ref, 1.0)
    rotary(proj(1), rk_ref, RET_HEAD_DIM ** -0.5)
    rv_ref[...] = proj(2).astype(BF16)
    rg_ref[...] = _silu(proj(3)).astype(BF16)
    dq_ref[...] = (proj(4) * (DIFF_QK_DIM ** -0.5 * math.log2(math.e))).astype(BF16)
    dk_ref[...] = proj(5).astype(BF16)
    dvt_ref[...] = proj(6).T.astype(BF16)


def _inproj(x2, mod, gain, w_in, cos_t, sin_t, seq, tm):
    tokens = x2.shape[0]
    tiles_per_batch = seq // tm
    tok_spec = lambda w: pl.BlockSpec((tm, w), lambda i: (i, 0))
    tab_spec = pl.BlockSpec((tm, LANES), lambda i: (i % tiles_per_batch, 0))
    full = lambda a: pl.BlockSpec(a.shape, lambda i: (0,) * a.ndim)
    out = jax.ShapeDtypeStruct((tokens, RET_WIDTH), BF16)
    return pl.pallas_call(
        functools.partial(_inproj_kernel, tiles_per_batch=tiles_per_batch),
        grid=(tokens // tm,),
        in_specs=[tok_spec(D_MODEL), full(mod), full(gain), full(w_in), tab_spec, tab_spec],
        out_specs=[tok_spec(RET_WIDTH)] * 6 + [pl.BlockSpec((DIFF_WIDTH, tm), lambda i: (0, i))],
        out_shape=[out] * 6 + [jax.ShapeDtypeStruct((DIFF_WIDTH, tokens), BF16)],
        compiler_params=pltpu.CompilerParams(
            dimension_semantics=("parallel",), vmem_limit_bytes=VMEM_LIMIT),
        name="inproj",
    )(x2, mod, gain, w_in, cos_t, sin_t)


RET_SEQ_GROUP = 4


def _retention_kernel(q_ref, k_ref, v_ref, g_ref, dec_ref, qdec_ref, kdec_ref, rdec_ref,
                      bmask_ref, gmean_ref, gain_ref, o_ref, state_ref, *, chunk):
    @pl.when(pl.program_id(1) == 0)
    def _():
        state_ref[...] = jnp.zeros_like(state_ref)

    lane = lax.broadcasted_iota(jnp.int32, (chunk, LANES), 1)
    first_head = lane < RET_HEAD_DIM
    gmean = gmean_ref[...]
    bmask = bmask_ref[...]
    units = [(s, p) for s in range(q_ref.shape[0]) for p in range(RET_PAIRS)]
    sl = lambda p: slice(p * LANES, (p + 1) * LANES)
    q = [q_ref[s, :, sl(p)] for s, p in units]
    k = [k_ref[s, :, sl(p)] for s, p in units]
    v = [v_ref[s, :, sl(p)] for s, p in units]
    zero = jnp.zeros_like(q[0])
    q_stack = [jnp.concatenate([jnp.where(first_head, qu, zero), jnp.where(first_head, zero, qu)], axis=0)
               for qu in q]
    scores = [(_dot_nt(q_stack[n], k[n]) * dec_ref[p]).astype(BF16) for n, (_, p) in enumerate(units)]
    state = [state_ref[n] for n in range(len(units))]
    cross = [_dot(q[n], state[n].astype(BF16)) * qdec_ref[:, sl(p)] for n, (_, p) in enumerate(units)]
    k_dec = [(k[n].astype(F32) * kdec_ref[:, sl(p)]).astype(BF16) for n, (_, p) in enumerate(units)]
    for n, (_, p) in enumerate(units):
        state_ref[n] = state[n] * rdec_ref[p] + _dot_tn(k_dec[n], v[n]) * bmask
    intra2 = [_dot(scores[n], v[n]) for n in range(len(units))]
    y = [jnp.where(first_head, intra2[n][:chunk], intra2[n][chunk:]) + cross[n] for n in range(len(units))]
    seg_mean = lambda x: _dot(jnp.concatenate(_split_bf16(x), axis=1), gmean)
    mu = [seg_mean(yu) for yu in y]
    d = [yu - mu_u for yu, mu_u in zip(y, mu)]
    var = [seg_mean(du * du) for du in d]
    for n, (s, p) in enumerate(units):
        yn = d[n] * lax.rsqrt(var[n] + EPS) * gain_ref[:, sl(p)]
        o_ref[s, :, sl(p)] = (g_ref[s, :, sl(p)].astype(F32) * yn).astype(BF16)


def _retention_tables(chunk):
    heads = np.arange(RET_HEADS, dtype=np.float64)
    log_gamma = np.log(1.0 - np.exp2(-5.0 - heads))
    idx = np.arange(chunk)
    rel = (idx[:, None] - idx[None, :]).astype(np.float64)
    decay = np.where(rel[None] >= 0, np.exp(log_gamma[:, None, None] * np.maximum(rel, 0.0)[None]), 0.0)
    dec2 = decay.reshape(RET_PAIRS, 2 * chunk, chunk)
    lane_lg = np.repeat(log_gamma, RET_HEAD_DIM)
    qdec = np.exp(lane_lg[None, :] * (idx + 1)[:, None])
    kdec = np.exp(lane_lg[None, :] * (chunk - 1 - idx)[:, None])
    rdec = np.exp(lane_lg * chunk).reshape(RET_PAIRS, LANES, 1) * np.ones((1, 1, LANES))
    blk = np.arange(LANES) // RET_HEAD_DIM
    bmask = (blk[:, None] == blk[None, :]).astype(np.float64)
    f32 = lambda a: jnp.asarray(a.astype(np.float32))
    gmean2 = np.concatenate([bmask, bmask], axis=0) / RET_HEAD_DIM
    return f32(dec2), f32(qdec), f32(kdec), f32(rdec), f32(bmask), f32(gmean2).astype(BF16)


def _retention(rq, rk, rv, rg, gn_gain, batch, seq, chunk):
    nc = seq // chunk
    group = RET_SEQ_GROUP if batch % RET_SEQ_GROUP == 0 else 1
    dec2, qdec, kdec, rdec, bmask, gmean = _retention_tables(chunk)
    tok_spec = pl.BlockSpec((group, chunk, RET_WIDTH), lambda b, n: (b, n, 0))
    full = lambda a: pl.BlockSpec(a.shape, lambda b, n: (0,) * a.ndim)
    by_seq = lambda a: a.reshape(batch, seq, RET_WIDTH)
    out = pl.pallas_call(
        functools.partial(_retention_kernel, chunk=chunk),
        grid=(batch // group, nc),
        in_specs=[tok_spec] * 4 + [full(dec2), full(qdec), full(kdec), full(rdec), full(bmask),
                                   full(gmean), full(gn_gain)],
        out_specs=tok_spec,
        out_shape=jax.ShapeDtypeStruct((batch, seq, RET_WIDTH), BF16),
        scratch_shapes=[pltpu.VMEM((group * RET_PAIRS, LANES, LANES), F32)],
        compiler_params=pltpu.CompilerParams(
            dimension_semantics=("parallel", "arbitrary"), vmem_limit_bytes=VMEM_LIMIT),
        name="retention",
    )(by_seq(rq), by_seq(rk), by_seq(rv), by_seq(rg), dec2, qdec, kdec, rdec, bmask, gmean, gn_gain)
    return out.reshape(batch * seq, RET_WIDTH)


NEG_BIG = -1e30


V_EXT_ROWS = DIFF_V_DIM + BF16_SUBLANES
QUERY_CHUNK = 256
SCORES_AHEAD_FULL = 3
SCORES_AHEAD_DIAG = 3
STAGE_SLOTS = 16


def _diag_chunks(tq, tk, d):
    assert tk == 2 * QUERY_CHUNK
    per_softmax = tq // QUERY_CHUNK
    out = []
    for c in range(2 * per_softmax):
        q0 = (c % per_softmax) * QUERY_CHUNK
        if q0 + QUERY_CHUNK - 1 < d * tk:
            continue
        kind = "full" if q0 >= (d + 1) * tk else ("tri" if q0 == d * tk else "low_tri")
        out.append((c, kind))
    return out


def _accumulate(acc_ref, cs, alpha, pv):
    acc_ref[:, cs] = alpha * acc_ref[:, cs] + pv


def _diffattn_kernel(q_ref, k_ref, vt_ref, lq1_ref, lk1_ref, lq2_ref, lk2_ref, gain_ref, bias_ref, o_ref,
                     qs_ref, vext_ref, m_ref, acc_ref, stage_ref, *, tq, tk, lambda_init):
    i = pl.program_id(2)
    nk = vext_ref.shape[0]

    @pl.when(i == 0)
    def _():
        for j in range(nk):
            vext_ref[j, 0:DIFF_V_DIM, :] = vt_ref[:, j * tk:(j + 1) * tk]
            vext_ref[j, DIFF_V_DIM:V_EXT_ROWS, :] = jnp.ones((V_EXT_ROWS - DIFF_V_DIM, tk), BF16)

    q = q_ref[...]
    lane = lax.broadcasted_iota(jnp.int32, q.shape, 1)
    zero = jnp.zeros_like(q)
    qs_ref[0:tq, :] = jnp.where(lane < DIFF_QK_DIM, q, zero)
    qs_ref[tq:2 * tq, :] = jnp.where(lane < DIFF_QK_DIM, zero, q)
    m_ref[...] = jnp.full_like(m_ref, NEG_BIG)
    acc_ref[...] = jnp.zeros_like(acc_ref)

    def step(work, n_ahead):
        chunk = lambda c: slice(c * QUERY_CHUNK, (c + 1) * QUERY_CHUNK)

        def scores(j, c, kind):
            n_keys = QUERY_CHUNK if kind == "tri" else tk
            start = pl.multiple_of(j * tk, tk)
            return _dot_nt(k_ref[pl.ds(start, n_keys), :], qs_ref[chunk(c), :])

        ahead = [scores(*work[n]) for n in range(min(n_ahead, len(work)))]
        pending = None
        for n, (j, c, kind) in enumerate(work):
            cs = chunk(c)
            st = ahead.pop(0)
            if n + n_ahead < len(work):
                ahead.append(scores(*work[n + n_ahead]))
            slot = n % stage_ref.shape[0]
            n_keys = st.shape[0]
            stage_ref[slot, 0:n_keys, :] = st
            if kind == "full":
                st = stage_ref[slot]
            else:
                causal = stage_ref[slot, n_keys - QUERY_CHUNK:n_keys, :] + bias_ref[...]
                st = causal if kind == "tri" else jnp.concatenate(
                    [stage_ref[slot, 0:n_keys - QUERY_CHUNK, :], causal], axis=0)
            m_old = m_ref[:, cs]
            m_new = jnp.maximum(m_old, jnp.max(st, axis=0, keepdims=True))
            alpha = jnp.exp2(m_old - m_new)
            p = jnp.exp2(st - m_new).astype(BF16)
            m_ref[:, cs] = m_new
            pv = _dot(vext_ref[j, :, 0:st.shape[0]], p)
            if pending is not None:
                pending()
            pending = functools.partial(_accumulate, acc_ref, cs, alpha, pv)
        pending()

    tiles_per_q = tq // tk
    n_chunks = 2 * tq // QUERY_CHUNK

    def full_tiles(it):
        return [(it * tiles_per_q + d, c, "full") for d in range(tiles_per_q) for c in range(n_chunks)]

    lax.fori_loop(0, i, lambda it, c: (step(full_tiles(it), SCORES_AHEAD_FULL), c)[1], 0)
    kind_order = {"full": 0, "low_tri": 1, "tri": 2}
    diag = lambda it: [(it * tiles_per_q + d, c, kind) for d in range(tiles_per_q)
                       for c, kind in sorted(_diag_chunks(tq, tk, d), key=lambda ck: kind_order[ck[1]])]
    lax.fori_loop(i, i + 1, lambda it, c: (step(diag(it), SCORES_AHEAD_DIAG), c)[1], 0)

    lam = (jnp.exp(jnp.sum(lq1_ref[...] * lk1_ref[...], axis=-1, keepdims=True))
           - jnp.exp(jnp.sum(lq2_ref[...] * lk2_ref[...], axis=-1, keepdims=True)) + lambda_init)
    acc = acc_ref[...]
    o2 = acc[0:DIFF_V_DIM, :] * (1.0 / acc[DIFF_V_DIM:DIFF_V_DIM + 1, :])
    ot = o2[:, :tq] - lam * o2[:, tq:]
    ms = jnp.mean(ot * ot, axis=0, keepdims=True)
    ot = ot * lax.rsqrt(ms + EPS) * gain_ref[...] * (1.0 - lambda_init)
    o_ref[...] = ot.T.astype(BF16)


def _diffattn(dq, dk, dvt, lam_q1, lam_k1, lam_q2, lam_k2, gain, batch, seq, lambda_init, tq, tk):
    nq = seq // tq
    q_spec = pl.BlockSpec((tq, LANES), lambda b, h, i: (b * nq + i, h))
    k_spec = pl.BlockSpec((seq, LANES), lambda b, h, i: (b, h))
    vt_spec = pl.BlockSpec((DIFF_V_DIM, seq), lambda b, h, i: (h, b))
    vec = lambda a: pl.BlockSpec(a.shape, lambda b, h, i: (0, 0))
    key = np.arange(QUERY_CHUNK)[:, None]
    query = np.arange(QUERY_CHUNK)[None, :]
    bias = jnp.asarray(np.where(key <= query, 0.0, NEG_BIG), F32)
    return pl.pallas_call(
        functools.partial(_diffattn_kernel, tq=tq, tk=tk, lambda_init=lambda_init),
        grid=(batch, DIFF_HEADS, nq),
        in_specs=[q_spec, k_spec, vt_spec, vec(lam_q1), vec(lam_k1), vec(lam_q2), vec(lam_k2), vec(gain),
                  vec(bias)],
        out_specs=q_spec,
        out_shape=jax.ShapeDtypeStruct(dq.shape, BF16),
        scratch_shapes=[
            pltpu.VMEM((2 * tq, LANES), BF16),
            pltpu.VMEM((seq // tk, V_EXT_ROWS, tk), BF16),
            pltpu.VMEM((1, 2 * tq), F32),
            pltpu.VMEM((V_EXT_ROWS, 2 * tq), F32),
            pltpu.VMEM((STAGE_SLOTS, tk, QUERY_CHUNK), F32),
        ],
        compiler_params=pltpu.CompilerParams(
            dimension_semantics=("parallel", "parallel", "arbitrary"), vmem_limit_bytes=VMEM_LIMIT),
        name="diffattn",
    )(dq, dk, dvt, lam_q1, lam_k1, lam_q2, lam_k2, gain, bias)


def _route(logits):
    r = [logits[g:g + 1, :] for g in range(N_GROUPS)]
    gmax = jnp.maximum(jnp.maximum(r[0], r[1]), jnp.maximum(r[2], r[3]))
    g_idx = jnp.where(r[0] == gmax, 0, jnp.where(r[1] == gmax, 1, jnp.where(r[2] == gmax, 2, 3)))
    denom = sum(jnp.exp(rg - gmax) for rg in r)
    g_weight = 1.0 / denom
    sel = jnp.zeros((EXPERTS_PER_GROUP, logits.shape[1]), F32)
    for g in range(N_GROUPS):
        rows = logits[SUBLANES + g * EXPERTS_PER_GROUP:SUBLANES + (g + 1) * EXPERTS_PER_GROUP, :]
        sel = jnp.where(g_idx == g, rows, sel)
    eidx = lax.broadcasted_iota(jnp.int32, sel.shape, 0)
    v1 = jnp.max(sel, axis=0, keepdims=True)
    i1 = jnp.min(jnp.where(sel == v1, eidx, EXPERTS_PER_GROUP), axis=0, keepdims=True)
    sel2 = jnp.where(eidx == i1, -jnp.inf, sel)
    v2 = jnp.max(sel2, axis=0, keepdims=True)
    i2 = jnp.min(jnp.where(sel2 == v2, eidx, EXPERTS_PER_GROUP), axis=0, keepdims=True)
    e2 = jnp.exp(v2 - v1)
    w1 = g_weight / (1.0 + e2)
    w2 = g_weight * e2 / (1.0 + e2)
    return g_idx, i1, i2, w1, w2


OUTPROJ_PARTS = 2
OUTPROJ_TILES_PER_STEP = 2


def _outproj_kernel(ret_ref, diff_ref, x_ref, mod_ref, gain_ref, wo_ref, wr_ref, br_ref, tri_ref,
                    x1_ref, h2_ref, ri_ref, rw_ref, cnt_ref, *, tiles_per_batch):
    b = pl.program_id(0) // tiles_per_batch
    gate1 = mod_ref[pl.ds(b, 1), 2 * D_MODEL:3 * D_MODEL]
    shift = mod_ref[pl.ds(b, 1), 3 * D_MODEL:4 * D_MODEL]
    scale = mod_ref[pl.ds(b, 1), 4 * D_MODEL:5 * D_MODEL]
    wr = wr_ref[...]
    rows = x_ref.shape[0]
    tm = tri_ref.shape[0]
    n_parts = OUTPROJ_PARTS * rows // tm
    parts = [slice(n * rows // n_parts, (n + 1) * rows // n_parts) for n in range(n_parts)]
    mix = [_dot(jnp.concatenate([ret_ref[r, :], diff_ref[r, :]], axis=1), wo_ref[...]) for r in parts]
    for r, m in zip(parts, mix):
        x1_ref[r, :] = x_ref[r, :] + gate1 * m
    h_split = [_split_bf16(_norm_modulate(x1_ref[r, :], gain_ref[...], shift, scale)) for r in parts]
    for r, (h_hi, _) in zip(parts, h_split):
        h2_ref[r, :] = h_hi
    by_hi = [_dot_nt(wr, h_hi) for h_hi, _ in h_split]
    by_lo = [_dot_nt(wr[:ROUTER_ROWS], h_lo) for _, h_lo in h_split]
    logits = [a[:ROUTER_ROWS] + a[ROUTER_ROWS:] + c + br_ref[...] for a, c in zip(by_hi, by_lo)]
    routed = [_route(lg) for lg in logits]
    g_idx, i1, i2, w1, w2 = [jnp.concatenate([rt[n] for rt in routed], axis=1) for n in range(5)]
    logits = jnp.concatenate(logits, axis=1)
    e1 = g_idx * EXPERTS_PER_GROUP + i1
    e2 = g_idx * EXPERTS_PER_GROUP + i2
    eidx = lax.broadcasted_iota(jnp.int32, (N_EXPERTS, logits.shape[1]), 0)
    hit1 = eidx == e1
    hit2 = eidx == e2
    onehot = jnp.where(hit1 | hit2, 1.0, 0.0)
    tiles = [slice(n * tm, (n + 1) * tm) for n in range(rows // tm)]
    before = jnp.concatenate([_dot(onehot[:, t].astype(BF16), tri_ref[...]) for t in tiles], axis=1)
    r1 = jnp.sum(jnp.where(hit1, before, 0.0), axis=0, keepdims=True)
    r2 = jnp.sum(jnp.where(hit2, before, 0.0), axis=0, keepdims=True)
    zi = jnp.zeros_like(e1)
    ri_ref[...] = jnp.concatenate([e1, e2, r1.astype(jnp.int32), r2.astype(jnp.int32), zi, zi, zi, zi], axis=0)
    zf = jnp.zeros_like(w1)
    rw_ref[...] = jnp.concatenate([w1, w2, zf, zf, zf, zf, zf, zf], axis=0)
    for n, t in enumerate(tiles):
        counts = jnp.sum(onehot[:, t], axis=1, keepdims=True)
        cnt_ref[n] = jnp.broadcast_to(counts, (N_EXPERTS, LANES)).astype(jnp.int32)


def _outproj(ret_out, diff_out, x2, mod, gain, w_out, wr, br, seq, tm):
    tokens = x2.shape[0]
    n_tiles = tokens // tm
    per_step = OUTPROJ_TILES_PER_STEP if seq % (OUTPROJ_TILES_PER_STEP * tm) == 0 else 1
    rows = per_step * tm
    tri = jnp.asarray(np.arange(tm)[:, None] < np.arange(tm)[None, :], BF16)
    tok_spec = lambda w: pl.BlockSpec((rows, w), lambda i: (i, 0))
    row_spec = pl.BlockSpec((SUBLANES, rows), lambda i: (0, i))
    full = lambda a: pl.BlockSpec(a.shape, lambda i: (0,) * a.ndim)
    return pl.pallas_call(
        functools.partial(_outproj_kernel, tiles_per_batch=seq // rows),
        grid=(tokens // rows,),
        in_specs=[tok_spec(RET_WIDTH), tok_spec(DIFF_WIDTH), tok_spec(D_MODEL), full(mod), full(gain),
                  full(w_out), full(wr), full(br), full(tri)],
        out_specs=[tok_spec(D_MODEL), tok_spec(D_MODEL), row_spec, row_spec,
                   pl.BlockSpec((per_step, N_EXPERTS, LANES), lambda i: (i, 0, 0))],
        out_shape=[jax.ShapeDtypeStruct((tokens, D_MODEL), F32),
                   jax.ShapeDtypeStruct((tokens, D_MODEL), BF16),
                   jax.ShapeDtypeStruct((SUBLANES, tokens), jnp.int32),
                   jax.ShapeDtypeStruct((SUBLANES, tokens), F32),
                   jax.ShapeDtypeStruct((n_tiles, N_EXPERTS, LANES), jnp.int32)],
        compiler_params=pltpu.CompilerParams(
            dimension_semantics=("parallel",), vmem_limit_bytes=VMEM_LIMIT),
        name="outproj",
    )(ret_out, diff_out, x2, mod, gain, w_out, wr, br, tri)


CHUNK = 8
BIG_PIECE = 4 * CHUNK
MAX_SMALL_PIECES = N_EXPERTS * (BIG_PIECE // CHUNK - 1)
PIECE_TABLES = ("n_big", "big_src", "big_dst", "n_small", "small_src", "small_dst")
DISPATCH_TILES_PER_STEP = 2
COMBINE_TILES_PER_STEP = 2
TMX = 512
IN_SLOTS = 3
OUT_SLOTS = 2


def _local_rows(tm):
    rows = 2 * tm + N_EXPERTS * (CHUNK - 1)
    return pl.cdiv(rows, BF16_SUBLANES) * BF16_SUBLANES


def _max_big_pieces(tm):
    return _local_rows(tm) // BIG_PIECE


def _sorted_rows_alloc(tokens, tm):
    worst = 2 * tokens + (tokens // tm) * N_EXPERTS * (CHUNK - 1) + N_EXPERTS * (TMX - CHUNK)
    return (worst + TMX - 1) // TMX * TMX


def _dispatch_plan(cnt, tokens, tm):
    i32 = jnp.int32
    pad = (cnt + CHUNK - 1) // CHUNK * CHUNK
    local_end = jnp.cumsum(pad, axis=1)
    local_start = local_end - pad
    seg_rows = jnp.sum(pad, axis=0)
    seg_pad = (seg_rows + TMX - 1) // TMX * TMX
    seg_end = jnp.cumsum(seg_pad)
    seg_start = seg_end - seg_pad
    run_dst = seg_start[None, :] + jnp.cumsum(pad, axis=0) - pad

    def pieces(count, offset, size, max_n):
        end = jnp.cumsum(count, axis=1)
        start = end - count
        k = jnp.arange(max_n, dtype=i32)[None, :, None]
        owns = (start[:, None, :] <= k) & (k < end[:, None, :])
        within = size * (k - start[:, None, :]) + offset[:, None, :]
        src = jnp.sum(jnp.where(owns, local_start[:, None, :] + within, 0), axis=-1)
        dst = jnp.sum(jnp.where(owns, run_dst[:, None, :] + within, 0), axis=-1)
        return end[:, -1].astype(i32), src.reshape(-1).astype(i32), dst.reshape(-1).astype(i32)

    n_big, big_src, big_dst = pieces(pad // BIG_PIECE, jnp.zeros_like(pad), BIG_PIECE, _max_big_pieces(tm))
    n_small, small_src, small_dst = pieces(pad % BIG_PIECE // CHUNK, pad // BIG_PIECE * BIG_PIECE, CHUNK,
                                           MAX_SMALL_PIECES)
    m = TMX * jnp.arange(_sorted_rows_alloc(tokens, tm) // TMX, dtype=i32)
    tile_expert = jnp.minimum(jnp.sum(seg_end[None, :] <= m[:, None], axis=-1), N_EXPERTS - 1)
    towns = (seg_start[None, :] <= m[:, None]) & (m[:, None] < seg_end[None, :])
    used = seg_pad > 0
    parity = (jnp.cumsum(used) - used) % 2
    eids = jnp.arange(N_EXPERTS, dtype=i32)
    later_used = (eids[None, :] > eids[:, None]) & used[None, :]
    next_used = jnp.min(jnp.where(later_used, eids[None, :], N_EXPERTS), axis=1)
    next_used = jnp.where(next_used == N_EXPERTS, -1, next_used)
    pick = lambda per_expert: jnp.sum(jnp.where(towns, per_expert[None, :], 0), axis=-1)
    tile_first = jnp.sum(jnp.where(towns & (seg_start[None, :] == m[:, None]), 1, 0), axis=-1)
    tile_next = jnp.where(jnp.any(towns, axis=-1), pick(next_used), -1)
    return dict(
        tile_first=tile_first.astype(i32),
        tile_slot=pick(parity).astype(i32),
        tile_next=tile_next.astype(i32),
        tile_rows=jnp.clip(pick(seg_start + seg_rows) - m, 0, TMX).astype(i32),
        local_start=local_start.reshape(-1).astype(i32),
        n_big=n_big, big_src=big_src, big_dst=big_dst,
        n_small=n_small, small_src=small_src, small_dst=small_dst,
        tail_base=(seg_start + seg_rows).astype(i32),
        tail_rows=(seg_pad - seg_rows).astype(i32),
        tile_expert=tile_expert.astype(i32),
        n_used=(seg_end[-1:] // TMX).astype(i32),
    )


WAIT_UNROLL = 8


def _wait_times(copy, n):
    lax.fori_loop(0, n // WAIT_UNROLL, lambda i, c: ([copy.wait() for _ in range(WAIT_UNROLL)], c)[1], 0)
    lax.fori_loop(0, n % WAIT_UNROLL, lambda i, c: (copy.wait(), c)[1], 0)


def _for_each(n, body, unroll=4):
    main = n // unroll
    lax.fori_loop(0, main, lambda i, c: ([body(i * unroll + u) for u in range(unroll)], c)[1], 0)
    lax.fori_loop(main * unroll, n, lambda j, c: (body(j), c)[1], 0)


def _local_slots(ri_ref, local_start_ref, tile, cols=slice(None)):
    e1, e2 = ri_ref[0:1, cols], ri_ref[1:2, cols]
    s1, s2 = ri_ref[2:3, cols], ri_ref[3:4, cols]
    for e in range(N_EXPERTS):
        start = local_start_ref[tile * N_EXPERTS + e]
        s1 = s1 + jnp.where(e1 == e, start, 0)
        s2 = s2 + jnp.where(e2 == e, start, 0)
    return s1, s2


def _run_pieces(piece_refs, local_ref, sorted_ref, sem_ref, to_sorted):
    n_big_ref, big_src_ref, big_dst_ref, n_small_ref, small_src_ref, small_dst_ref = piece_refs
    r_loc = local_ref.shape[1]
    kinds = [(BIG_PIECE, n_big_ref, big_src_ref, big_dst_ref, r_loc // BIG_PIECE),
             (CHUNK, n_small_ref, small_src_ref, small_dst_ref, MAX_SMALL_PIECES)]

    def copy(sl, size, local_row, sorted_row):
        local = local_ref.at[sl, pl.ds(pl.multiple_of(local_row, CHUNK), size), :]
        srt = sorted_ref.at[pl.ds(pl.multiple_of(sorted_row, CHUNK), size), :]
        return pltpu.make_async_copy(local, srt, sem_ref.at[sl]) if to_sorted else \
            pltpu.make_async_copy(srt, local, sem_ref.at[sl])

    def start(tile, sl):
        for size, n_ref, src_ref, dst_ref, max_n in kinds:
            _for_each(n_ref[tile], lambda k: copy(sl, size, src_ref[tile * max_n + k], dst_ref[tile * max_n + k])
                      .start())

    def wait(tile, sl):
        for size, n_ref, _, _, _ in kinds:
            _wait_times(copy(sl, size, 0, 0), n_ref[tile])

    return start, wait


def _dispatch_kernel(local_start_ref, n_big_ref, big_src_ref, big_dst_ref, n_small_ref, small_src_ref, small_dst_ref,
                     tail_base_ref, tail_rows_ref, n_used_ref,
                     h_ref, ri_ref, xs_ref, buf_ref, zero_ref, sem_ref, tail_sem_ref, *, r_loc):
    b = pl.program_id(0)
    nb = pl.num_programs(0)
    per_step = buf_ref.shape[0] // 2
    tm = h_ref.shape[0] // per_step
    gen = b % 2
    start_runs, drain_tile = _run_pieces(
        (n_big_ref, big_src_ref, big_dst_ref, n_small_ref, small_src_ref, small_dst_ref), buf_ref, xs_ref, sem_ref,
        to_sorted=True)

    def drain(step, g):
        for t in range(per_step):
            drain_tile(step * per_step + t, g * per_step + t)

    @pl.when(b >= 2)
    def _():
        drain(b - 2, gen)

    tiles = [b * per_step + t for t in range(per_step)]
    cols = [slice(t * tm, (t + 1) * tm) for t in range(per_step)]
    slots = [_local_slots(ri_ref, local_start_ref, tile, c) for tile, c in zip(tiles, cols)]
    rows = lax.broadcasted_iota(jnp.int32, (r_loc, tm), 0)
    perms = [jnp.where((rows == s1) | (rows == s2), 1.0, 0.0).astype(BF16) for s1, s2 in slots]
    for t, (perm, c) in enumerate(zip(perms, cols)):
        buf_ref[gen * per_step + t] = _dot(perm, h_ref[c, :])
    for t, tile in enumerate(tiles):
        start_runs(tile, gen * per_step + t)

    def tail_pieces(e, act):
        n = tail_rows_ref[e]
        size = TMX // 2
        while size >= CHUNK:
            dst = pl.multiple_of(tail_base_ref[e] + (n & (-2 * size)), CHUNK)
            cp = pltpu.make_async_copy(zero_ref.at[pl.ds(0, size), :], xs_ref.at[pl.ds(dst, size), :],
                                       tail_sem_ref.at[0])
            pl.when((n & size) != 0)(functools.partial(act, cp))
            size //= 2

    def unused_tile_copy(m):
        dst = pl.multiple_of(m * TMX, TMX)
        return pltpu.make_async_copy(zero_ref, xs_ref.at[pl.ds(dst, TMX), :], tail_sem_ref.at[1])

    n_alloc = xs_ref.shape[0] // TMX

    @pl.when(b == 0)
    def _():
        zero_ref[...] = jnp.zeros_like(zero_ref)

    experts_per_step = pl.cdiv(N_EXPERTS, nb)
    lax.fori_loop(jnp.minimum(b * experts_per_step, N_EXPERTS), jnp.minimum((b + 1) * experts_per_step, N_EXPERTS),
                  lambda e, c: (tail_pieces(e, lambda cp: cp.start()), c)[1], 0)
    tiles_per_step = pl.cdiv(n_alloc, nb)
    first_unused = n_used_ref[0]
    lax.fori_loop(jnp.minimum(first_unused + b * tiles_per_step, n_alloc),
                  jnp.minimum(first_unused + (b + 1) * tiles_per_step, n_alloc),
                  lambda m, c: (unused_tile_copy(m).start(), c)[1], 0)

    @pl.when(b == nb - 1)
    def _():
        lax.fori_loop(0, N_EXPERTS, lambda e, c: (tail_pieces(e, lambda cp: cp.wait()), c)[1], 0)
        lax.fori_loop(n_used_ref[0], n_alloc, lambda m, c: (unused_tile_copy(m).wait(), c)[1], 0)

        @pl.when(b >= 1)
        def _():
            drain(b - 1, 1 - gen)

        drain(b, gen)


def _dispatch(h2, ri, plan, tm):
    tokens = h2.shape[0]
    r_loc = _local_rows(tm)
    prefetch = [plan["local_start"]] + [plan[k] for k in PIECE_TABLES] + [
        plan["tail_base"], plan["tail_rows"], plan["n_used"]]
    per_step = DISPATCH_TILES_PER_STEP if (tokens // tm) % DISPATCH_TILES_PER_STEP == 0 else 1
    rows = per_step * tm
    grid_spec = pltpu.PrefetchScalarGridSpec(
        num_scalar_prefetch=len(prefetch),
        grid=(tokens // rows,),
        in_specs=[pl.BlockSpec((rows, D_MODEL), lambda i, *_: (i, 0)),
                  pl.BlockSpec((SUBLANES, rows), lambda i, *_: (0, i))],
        out_specs=pl.BlockSpec(memory_space=pl.ANY),
        scratch_shapes=[pltpu.VMEM((2 * per_step, r_loc, D_MODEL), F32), pltpu.VMEM((TMX, D_MODEL), F32),
                        pltpu.SemaphoreType.DMA((2 * per_step,)), pltpu.SemaphoreType.DMA((2,))],
    )
    return pl.pallas_call(
        functools.partial(_dispatch_kernel, r_loc=r_loc),
        grid_spec=grid_spec,
        out_shape=jax.ShapeDtypeStruct((_sorted_rows_alloc(tokens, tm), D_MODEL), F32),
        compiler_params=pltpu.CompilerParams(
            dimension_semantics=("arbitrary",), vmem_limit_bytes=VMEM_LIMIT),
        name="dispatch",
    )(*prefetch, h2, ri)


def _experts_kernel(tile_expert_ref, n_used_ref, first_ref, slot_ref, next_ref, rows_ref, xs_hbm, wg_hbm, wu_hbm, wd_hbm,
                    ys_hbm, wg_st, wu_st, wd_st, wg_bf, wu_bf, wd_bf, a_ref, u_ref, xin_ref, yout_ref,
                    sem_ref, in_sem_ref, out_sem_ref):
    m = pl.program_id(0)
    n_used = n_used_ref[0]

    def weight_copies(e, s):
        return [pltpu.make_async_copy(src.at[e], dst.at[s], sem_ref.at[s, n])
                for n, (src, dst) in enumerate([(wg_hbm, wg_st), (wu_hbm, wu_st), (wd_hbm, wd_st)])]

    def in_copy(t):
        s = t % IN_SLOTS
        return pltpu.make_async_copy(xs_hbm.at[pl.ds(pl.multiple_of(t * TMX, TMX), TMX), :], xin_ref.at[s],
                                     in_sem_ref.at[s])

    def out_copy(t):
        s = t % OUT_SLOTS
        return pltpu.make_async_copy(yout_ref.at[s], ys_hbm.at[pl.ds(pl.multiple_of(t * TMX, TMX), TMX), :],
                                     out_sem_ref.at[s])

    @pl.when(m == 0)
    def _():
        for t in range(IN_SLOTS - 1):
            pl.when(t < n_used)(in_copy(t).start)

    @pl.when(m < n_used)
    def _():
        @pl.when(m + IN_SLOTS - 1 < n_used)
        def _():
            in_copy(m + IN_SLOTS - 1).start()

        @pl.when(first_ref[m] == 1)
        def _():
            s = slot_ref[m]

            @pl.when(m == 0)
            def _():
                for cp in weight_copies(tile_expert_ref[0], 0):
                    cp.start()

            for cp in weight_copies(tile_expert_ref[m], s):
                cp.wait()

            @pl.when(next_ref[m] >= 0)
            def _():
                for cp in weight_copies(next_ref[m], 1 - s):
                    cp.start()

            wg_bf[...] = wg_st[s].astype(BF16)
            wu_bf[...] = wu_st[s].astype(BF16)
            wd_bf[...] = wd_st[s].astype(BF16)

        xs_ref = xin_ref.at[m % IN_SLOTS]
        ys_ref = yout_ref.at[m % OUT_SLOTS]
        in_copy(m).wait()

        @pl.when(m >= OUT_SLOTS)
        def _():
            out_copy(m - OUT_SLOTS).wait()

        def mlp(rows):
            x = xs_ref[rows, :].astype(BF16)
            a_ref[rows, :] = _dot(x, wg_bf[...])
            u_ref[rows, :] = _dot(x, wu_bf[...])
            hid = (_silu(a_ref[rows, :]) * u_ref[rows, :]).astype(BF16)
            ys_ref[rows, :] = _dot(hid, wd_bf[...])

        half = TMX // 2

        @pl.when(rows_ref[m] > half)
        def _():
            mlp(slice(0, TMX))

        @pl.when(rows_ref[m] <= half)
        def _():
            mlp(slice(0, half))
            ys_ref[half:, :] = jnp.zeros((TMX - half, D_MODEL), F32)

        out_copy(m).start()

        @pl.when(m == n_used - 1)
        def _():
            for back in range(OUT_SLOTS):
                pl.when(m - back >= 0)(out_copy(m - back).wait)


def _experts(xs, plan, wg, wu, wd):
    n_tiles = xs.shape[0] // TMX
    hbm = pl.BlockSpec(memory_space=pl.ANY)
    up_shape, down_shape = (D_MODEL, D_EXPERT), (D_EXPERT, D_MODEL)
    grid_spec = pltpu.PrefetchScalarGridSpec(
        num_scalar_prefetch=6,
        grid=(n_tiles,),
        in_specs=[hbm, hbm, hbm, hbm],
        out_specs=hbm,
        scratch_shapes=[pltpu.VMEM((2,) + up_shape, F32), pltpu.VMEM((2,) + up_shape, F32),
                        pltpu.VMEM((2,) + down_shape, F32),
                        pltpu.VMEM(up_shape, BF16), pltpu.VMEM(up_shape, BF16), pltpu.VMEM(down_shape, BF16),
                        pltpu.VMEM((TMX, D_EXPERT), F32), pltpu.VMEM((TMX, D_EXPERT), F32),
                        pltpu.VMEM((IN_SLOTS, TMX, D_MODEL), F32), pltpu.VMEM((OUT_SLOTS, TMX, D_MODEL), F32),
                        pltpu.SemaphoreType.DMA((2, 3)), pltpu.SemaphoreType.DMA((IN_SLOTS,)),
                        pltpu.SemaphoreType.DMA((OUT_SLOTS,))],
    )
    return pl.pallas_call(
        _experts_kernel,
        grid_spec=grid_spec,
        out_shape=jax.ShapeDtypeStruct(xs.shape, F32),
        input_output_aliases={6: 0},
        compiler_params=pltpu.CompilerParams(
            dimension_semantics=("arbitrary",), vmem_limit_bytes=VMEM_LIMIT),
        name="experts",
    )(plan["tile_expert"], plan["n_used"], plan["tile_first"], plan["tile_slot"], plan["tile_next"], plan["tile_rows"],
      xs, wg, wu, wd)


def _combine_kernel(local_start_ref, n_big_ref, big_src_ref, big_dst_ref, n_small_ref, small_src_ref, small_dst_ref,
                    ys_ref, ri_ref, rw_ref, x1_ref, mod_ref, gain_ref, o_ref, buf_ref, sem_ref,
                    *, r_loc, tiles_per_batch):
    b = pl.program_id(0)
    nb = pl.num_programs(0)
    per_step = buf_ref.shape[0] // 2
    tm = x1_ref.shape[0] // per_step
    gen = b % 2
    fetch_tile, wait_tile = _run_pieces(
        (n_big_ref, big_src_ref, big_dst_ref, n_small_ref, small_src_ref, small_dst_ref), buf_ref, ys_ref, sem_ref,
        to_sorted=False)

    def fetch(step, g):
        for t in range(per_step):
            fetch_tile(step * per_step + t, g * per_step + t)

    @pl.when(b == 0)
    def _():
        buf_ref[...] = jnp.zeros_like(buf_ref)
        fetch(0, 0)

    @pl.when(b + 1 < nb)
    def _():
        fetch(b + 1, 1 - gen)

    for t in range(per_step):
        wait_tile(b * per_step + t, gen * per_step + t)

    tiles = [b * per_step + t for t in range(per_step)]
    cols = [slice(t * tm, (t + 1) * tm) for t in range(per_step)]
    slots = [_local_slots(ri_ref, local_start_ref, tile, c) for tile, c in zip(tiles, cols)]
    rows = lax.broadcasted_iota(jnp.int32, (r_loc, tm), 0)
    hits = [(rows == s1, rows == s2) for s1, s2 in slots]
    w_rows = [jnp.sum(jnp.where(h1, rw_ref[0:1, c], jnp.where(h2, rw_ref[1:2, c], 0.0)), axis=1, keepdims=True)
              for (h1, h2), c in zip(hits, cols)]
    perms = [jnp.where(h1 | h2, 1.0, 0.0).astype(BF16) for h1, h2 in hits]
    yws = [(buf_ref[gen * per_step + t] * w_rows[t]).astype(BF16) for t in range(per_step)]
    moes = [_dot_tn(perm, yw) for perm, yw in zip(perms, yws)]
    batch = b // tiles_per_batch
    gate2 = mod_ref[pl.ds(batch, 1), 5 * D_MODEL:6 * D_MODEL]
    for c, moe in zip(cols, moes):
        x2 = x1_ref[c, :] + gate2 * moe
        ms = jnp.mean(x2 * x2, axis=-1, keepdims=True)
        o_ref[c, :] = x2 * lax.rsqrt(ms + EPS) * gain_ref[...]


def _combine(ys, ri, rw, x1, mod, gain, plan, seq, tm):
    tokens = x1.shape[0]
    r_loc = _local_rows(tm)
    per_step = COMBINE_TILES_PER_STEP if seq % (COMBINE_TILES_PER_STEP * tm) == 0 else 1
    rows = per_step * tm
    row_spec = pl.BlockSpec((SUBLANES, rows), lambda i, *_: (0, i))
    tok_spec = pl.BlockSpec((rows, D_MODEL), lambda i, *_: (i, 0))
    full = lambda a: pl.BlockSpec(a.shape, lambda i, *_: (0,) * a.ndim)
    prefetch = [plan["local_start"]] + [plan[k] for k in PIECE_TABLES]
    grid_spec = pltpu.PrefetchScalarGridSpec(
        num_scalar_prefetch=len(prefetch),
        grid=(tokens // rows,),
        in_specs=[pl.BlockSpec(memory_space=pl.ANY), row_spec, row_spec, tok_spec, full(mod), full(gain)],
        out_specs=tok_spec,
        scratch_shapes=[pltpu.VMEM((2 * per_step, r_loc, D_MODEL), F32), pltpu.SemaphoreType.DMA((2 * per_step,))],
    )
    return pl.pallas_call(
        functools.partial(_combine_kernel, r_loc=r_loc, tiles_per_batch=seq // rows),
        grid_spec=grid_spec,
        out_shape=jax.ShapeDtypeStruct((tokens, D_MODEL), F32),
        compiler_params=pltpu.CompilerParams(
            dimension_semantics=("arbitrary",), vmem_limit_bytes=VMEM_LIMIT),
        name="combine",
    )(*prefetch, ys, ri, rw, x1, mod, gain)


def _rotary_tables(seq):
    half = RET_HEAD_DIM // 2
    inv_freq = 1.0 / (ROPE_BASE ** (np.arange(half, dtype=np.float64) / half))
    ang = np.arange(seq, dtype=np.float64)[:, None] * inv_freq[None, :]
    cos = np.cos(ang)
    sin = np.sin(ang)
    f32 = lambda a: jnp.asarray(a.astype(np.float32))
    return f32(np.tile(cos, (1, 4))), f32(np.concatenate([-sin, sin, -sin, sin], axis=1))


def _pick_tile(n, pref):
    t = min(n, pref)
    assert n % t == 0, (n, t)
    return t


def kernel(x, c, ada_w, ada_b, norm1_gain, norm2_gain, w_in, w_out, ret_gn_gain, lam_q1, lam_k1, lam_q2,
           lam_k2, diff_subln_gain, w_group, b_group, w_expert, b_expert, w_gate, w_up, w_down, final_gain):
    batch, seq, d = x.shape
    assert d == D_MODEL and batch <= 8 and ada_w.shape[0] == 1
    layer = 0
    lambda_init = 0.8 - 0.6 * math.exp(-0.3 * layer)
    tokens = batch * seq
    x2 = x.reshape(tokens, d)
    tm = _pick_tile(seq, TOKEN_TILE)

    c_pad = jnp.zeros((SUBLANES, d), F32).at[:batch].set(c)
    mod = _adaln(c_pad, ada_w[layer], ada_b[layer].reshape(1, -1))

    cos_t, sin_t = _rotary_tables(seq)
    rq, rk, rv, rg, dq, dk, dvt = _inproj(
        x2, mod, norm1_gain[layer].reshape(1, d), w_in[layer].astype(BF16), cos_t, sin_t, seq, tm)

    ret_out = _retention(rq, rk, rv, rg, ret_gn_gain[layer].reshape(1, RET_WIDTH), batch, seq,
                         _pick_tile(seq, RET_CHUNK))
    diff_out = _diffattn(
        dq, dk, dvt, lam_q1[layer].reshape(1, -1), lam_k1[layer].reshape(1, -1), lam_q2[layer].reshape(1, -1),
        lam_k2[layer].reshape(1, -1), diff_subln_gain[layer].reshape(-1, 1), batch, seq, lambda_init,
        _pick_tile(seq, ATTN_Q_TILE), 2 * QUERY_CHUNK)

    w_router = jnp.concatenate(
        [w_group[layer].T, jnp.zeros((SUBLANES - N_GROUPS, d), F32), w_expert[layer].reshape(d, N_EXPERTS).T], axis=0)
    b_router = jnp.concatenate(
        [b_group[layer], jnp.zeros((SUBLANES - N_GROUPS,), F32), b_expert[layer].reshape(N_EXPERTS)]).reshape(-1, 1)
    wr_hi = w_router.astype(BF16)
    wr_lo = (w_router - wr_hi.astype(F32)).astype(BF16)
    x1, h2, ri, rw, cnt = _outproj(ret_out, diff_out, x2, mod, norm2_gain[layer].reshape(1, d),
                                   w_out[layer].astype(BF16), jnp.concatenate([wr_hi, wr_lo], axis=0), b_router,
                                   seq, tm)

    plan = _dispatch_plan(cnt[:, :, 0], tokens, tm)
    xs = _dispatch(h2, ri, plan, tm)
    ys = _experts(xs, plan, w_gate[layer].reshape(N_EXPERTS, d, D_EXPERT),
                  w_up[layer].reshape(N_EXPERTS, d, D_EXPERT), w_down[layer].reshape(N_EXPERTS, D_EXPERT, d))
    out = _combine(ys, ri, rw, x1, mod, final_gain.reshape(1, d), plan, seq, tm)
    return out.reshape(batch, seq, d)
```
